```python
import math
import jax
import jax.numpy as jnp
from jax import lax
import numpy as np

D_MODEL = 1024
BATCH = 8
SEQ = 2048
DEPTH = 2
DEC_BATCH = 128
DEC_SEQ = 4
PAST_LEN = 16384
PAGE_SIZE = 128

N_MIXERS = 2
N_GLA_LAYERS = (DEPTH + 1) // 2
N_MLSTM_LAYERS = DEPTH // 2
N_DENSE_LAYERS = (DEPTH + 1) // 2
N_MOE_LAYERS = DEPTH // 2

QK_W = D_MODEL // 2
V_W = D_MODEL

GLA_HEADS = 4
GLA_DK = QK_W // GLA_HEADS
GLA_DV = V_W // GLA_HEADS
GLA_RANK = 16
GLA_TAU = 16.0
GLA_IN = 2 * QK_W + 2 * V_W + GLA_RANK

MLSTM_HEADS = 8
MLSTM_DK = QK_W // MLSTM_HEADS
MLSTM_DV = V_W // MLSTM_HEADS
GATE_CAP = 15.0
FORGET_BIAS = 3.0
MLSTM_IN = 2 * QK_W + 2 * V_W + 2 * MLSTM_HEADS

CHUNK = 64
D_FF = 2816
N_EXPERTS = 8
TOP_K = 2
D_FF_EXPERT = 3584
EPS = 1e-6
F32 = jnp.float32

kernel_name = 'gla_mlstm_hybrid_moe_decode_step'


def rms_norm(x, g):
    xf = x.astype(F32)
    y = xf * lax.rsqrt(jnp.mean(xf * xf, axis=-1, keepdims=True) + EPS)
    return (y * g.astype(F32)).astype(x.dtype)


def chunk_len(t):
    return math.gcd(t, CHUNK)


def split_chunks(a, L):
    b, t, h, d = a.shape
    return a.reshape(b, t // L, L, h, d).transpose(1, 0, 3, 2, 4)


def split_gate_chunks(a, L):
    b, t, h = a.shape
    return a.reshape(b, t // L, L, h).transpose(1, 0, 3, 2)


def merge_chunks(a):
    nc, b, h, L, d = a.shape
    return a.transpose(1, 0, 3, 2, 4).reshape(b, nc * L, h, d)


def gla_recurrence(q, k, v, log_a, S0):
    L = chunk_len(q.shape[1])
    causal = jnp.tril(jnp.ones((L, L), dtype=bool))

    def step(S, blk):
        qi, ki, vi, gi = blk
        b = jnp.cumsum(gi, axis=2)
        qg = qi * jnp.exp(b)
        kg = ki * jnp.exp(-b)
        A = jnp.where(causal, jnp.einsum('bhtd,bhsd->bhts', qg, kg), 0.0)
        o = jnp.einsum('bhtd,bhdv->bhtv', qg, S) + jnp.einsum('bhts,bhsv->bhtv', A, vi)
        b_end = b[:, :, -1:, :]
        k_end = ki * jnp.exp(b_end - b)
        S_new = jnp.exp(b_end[:, :, 0, :])[..., None] * S + jnp.einsum('bhsd,bhsv->bhdv', k_end, vi)
        return S_new, o

    S_T, o = lax.scan(step, S0, (split_chunks(q, L), split_chunks(k, L),
                                 split_chunks(v, L), split_chunks(log_a, L)))
    return merge_chunks(o), S_T


def mlstm_recurrence(q, k, v, i_pre, log_f, C0, n0, m0):
    L = chunk_len(q.shape[1])
    causal = jnp.tril(jnp.ones((L, L), dtype=bool))

    def step(carry, blk):
        C, n, m = carry
        qi, ki, vi, ii, fi = blk
        F = jnp.cumsum(fi, axis=-1)
        D = jnp.where(causal, F[..., :, None] - F[..., None, :] + ii[..., None, :], -jnp.inf)
        inter = F + m[..., None]
        m_t = jnp.maximum(inter, jnp.max(D, axis=-1))
        w_inter = jnp.exp(inter - m_t)
        S = jnp.einsum('bhtd,bhsd->bhts', qi, ki) * jnp.exp(D - m_t[..., None])
        num = w_inter[..., None] * jnp.einsum('bhtd,bhdv->bhtv', qi, C) + jnp.einsum('bhts,bhsv->bhtv', S, vi)
        den = w_inter * jnp.einsum('bhtd,bhd->bht', qi, n) + jnp.sum(S, axis=-1)
        h = num / jnp.maximum(jnp.abs(den), jnp.exp(-m_t))[..., None]
        m_new = m_t[..., -1]
        w_C = jnp.exp(F[..., -1] + m - m_new)
        k_end = ki * jnp.exp(F[..., -1:] - F + ii - m_new[..., None])[..., None]
        C_new = w_C[..., None, None] * C + jnp.einsum('bhsd,bhsv->bhdv', k_end, vi)
        n_new = w_C[..., None] * n + jnp.sum(k_end, axis=2)
        return (C_new, n_new, m_new), h

    (C_T, n_T, m_T), h = lax.scan(
        step, (C0, n0, m0),
        (split_chunks(q, L), split_chunks(k, L), split_chunks(v, L),
         split_gate_chunks(i_pre, L), split_gate_chunks(log_f, L)))
    return merge_chunks(h), C_T, n_T, m_T


def gla_mixer(x, S0, w_in, w_a2, b_a, g_norm, w_out):
    b, t, _ = x.shape
    q, k, v, r, a_lr = jnp.split(x @ w_in, [QK_W, 2 * QK_W, 2 * QK_W + V_W, 2 * QK_W + 2 * V_W], axis=-1)
    log_a = jax.nn.log_sigmoid((a_lr @ w_a2 + b_a).astype(F32)) / GLA_TAU
    q = q.astype(F32).reshape(b, t, GLA_HEADS, GLA_DK)
    k = k.astype(F32).reshape(b, t, GLA_HEADS, GLA_DK) * GLA_DK ** -0.5
    v = v.astype(F32).reshape(b, t, GLA_HEADS, GLA_DV)
    o, S_T = gla_recurrence(q, k, v, log_a.reshape(b, t, GLA_HEADS, GLA_DK), S0.astype(F32))
    o = rms_norm(o, g_norm).reshape(b, t, V_W).astype(x.dtype)
    y = (jax.nn.silu(r) * o) @ w_out
    return y, S_T.astype(S0.dtype)


def mlstm_mixer(x, C0, n0, m0, w_in, b_gate, g_norm, w_out):
    b, t, _ = x.shape
    q, k, v, o_pre, g_pre = jnp.split(x @ w_in, [QK_W, 2 * QK_W, 2 * QK_W + V_W, 2 * QK_W + 2 * V_W], axis=-1)
    g = GATE_CAP * jnp.tanh((g_pre + b_gate).astype(F32) / GATE_CAP)
    i_pre = g[..., :MLSTM_HEADS]
    log_f = jax.nn.log_sigmoid(g[..., MLSTM_HEADS:])
    q = q.astype(F32).reshape(b, t, MLSTM_HEADS, MLSTM_DK)
    k = k.astype(F32).reshape(b, t, MLSTM_HEADS, MLSTM_DK) * MLSTM_DK ** -0.5
    v = v.astype(F32).reshape(b, t, MLSTM_HEADS, MLSTM_DV)
    h, C_T, n_T, m_T = mlstm_recurrence(q, k, v, i_pre, log_f, C0.astype(F32), n0.astype(F32), m0.astype(F32))
    h = rms_norm(h, g_norm).reshape(b, t, V_W).astype(x.dtype)
    y = (jax.nn.sigmoid(o_pre) * h) @ w_out
    return y, C_T.astype(C0.dtype), n_T.astype(n0.dtype), m_T.astype(m0.dtype)


def swiglu(x, w_gu, w_down):
    g, u = jnp.split(x @ w_gu, 2, axis=-1)
    return (jax.nn.silu(g) * u) @ w_down


def moe_swiglu(x, w_router, b_router, w_gu, w_down):
    probs = jax.nn.softmax((x @ w_router + b_router).astype(F32), axis=-1)
    top_p, top_i = lax.top_k(probs, TOP_K)
    top_p = top_p / jnp.sum(top_p, axis=-1, keepdims=True)
    gates = jnp.sum(jax.nn.one_hot(top_i, N_EXPERTS, dtype=F32) * top_p[..., None], axis=-2)
    y = jnp.zeros_like(x)
    for e in range(N_EXPERTS):
        y = y + (gates[..., e:e + 1] * swiglu(x, w_gu[e], w_down[e]).astype(F32)).astype(x.dtype)
    return y


def trunk(x, gla_S, m_C, m_n, m_m, weights):
    (norm_mix, norm_ffn, norm_final,
     gla_w_in, gla_w_a2, gla_b_a, gla_g_norm, gla_w_out,
     mlstm_w_in, mlstm_b_gate, mlstm_g_norm, mlstm_w_out,
     ffn_w_gu, ffn_w_down,
     moe_w_router, moe_b_router, moe_w_gu, moe_w_down) = weights
    new_S, new_C, new_n, new_m = [], [], [], []
    for i in range(DEPTH):
        j = i // N_MIXERS
        h = rms_norm(x, norm_mix[i])
        if i % N_MIXERS == 0:
            y, S = gla_mixer(h, gla_S[j], gla_w_in[j], gla_w_a2[j], gla_b_a[j], gla_g_norm[j], gla_w_out[j])
            new_S.append(S)
        else:
            y, C, n, m = mlstm_mixer(h, m_C[j], m_n[j], m_m[j], mlstm_w_in[j], mlstm_b_gate[j],
                                     mlstm_g_norm[j], mlstm_w_out[j])
            new_C.append(C)
            new_n.append(n)
            new_m.append(m)
        x = x + y
        h = rms_norm(x, norm_ffn[i])
        jf = i // 2
        if i % 2 == 0:
            x = x + swiglu(h, ffn_w_gu[jf], ffn_w_down[jf])
        else:
            x = x + moe_swiglu(h, moe_w_router[jf], moe_b_router[jf], moe_w_gu[jf], moe_w_down[jf])
    return rms_norm(x, norm_final), jnp.stack(new_S), jnp.stack(new_C), jnp.stack(new_n), jnp.stack(new_m)


def setup_inputs(seed: int = 0) -> dict:
    key = jax.random.key(seed)
    ks = jax.random.split(key, 28)

    def nrm(k, shape, s):
        return jax.random.normal(k, shape, F32) * s

    return {
        'x_prompt': nrm(ks[0], (BATCH, SEQ, D_MODEL), 1.0),
        'x_sample': nrm(ks[1], (DEC_BATCH, DEC_SEQ, D_MODEL), 1.0),
        'state_gla_S': nrm(ks[2], (N_GLA_LAYERS, DEC_BATCH, GLA_HEADS, GLA_DK, GLA_DV), 0.1),
        'state_mlstm_C': nrm(ks[3], (N_MLSTM_LAYERS, DEC_BATCH, MLSTM_HEADS, MLSTM_DK, MLSTM_DV), 0.1),
        'state_mlstm_n': nrm(ks[4], (N_MLSTM_LAYERS, DEC_BATCH, MLSTM_HEADS, MLSTM_DK), 0.1),
        'state_mlstm_m': 2.0 + nrm(ks[5], (N_MLSTM_LAYERS, DEC_BATCH, MLSTM_HEADS), 0.5),
        'norm_mix': 1.0 + nrm(ks[6], (DEPTH, D_MODEL), 0.02),
        'norm_ffn': 1.0 + nrm(ks[7], (DEPTH, D_MODEL), 0.02),
        'norm_final': 1.0 + nrm(ks[8], (D_MODEL,), 0.02),
        'gla_w_in': nrm(ks[9], (N_GLA_LAYERS, D_MODEL, GLA_IN), D_MODEL ** -0.5),
        'gla_w_a2': nrm(ks[10], (N_GLA_LAYERS, GLA_RANK, QK_W), GLA_RANK ** -0.5),
        'gla_b_a': nrm(ks[11], (N_GLA_LAYERS, QK_W), 0.1),
        'gla_g_norm': 1.0 + nrm(ks[12], (N_GLA_LAYERS, GLA_HEADS, GLA_DV), 0.02),
        'gla_w_out': nrm(ks[13], (N_GLA_LAYERS, V_W, D_MODEL), V_W ** -0.5),
        'mlstm_w_in': nrm(ks[14], (N_MLSTM_LAYERS, D_MODEL, MLSTM_IN), D_MODEL ** -0.5),
        'mlstm_b_gate': jnp.concatenate(
            [nrm(ks[15], (N_MLSTM_LAYERS, MLSTM_HEADS), 0.1),
             FORGET_BIAS + nrm(ks[16], (N_MLSTM_LAYERS, MLSTM_HEADS), 0.1)], axis=-1),
        'mlstm_g_norm': 1.0 + nrm(ks[17], (N_MLSTM_LAYERS, MLSTM_HEADS, MLSTM_DV), 0.02),
        'mlstm_w_out': nrm(ks[18], (N_MLSTM_LAYERS, V_W, D_MODEL), V_W ** -0.5),
        'ffn_w_gu': nrm(ks[19], (N_DENSE_LAYERS, D_MODEL, 2 * D_FF), D_MODEL ** -0.5),
        'ffn_w_down': nrm(ks[20], (N_DENSE_LAYERS, D_FF, D_MODEL), D_FF ** -0.5),
        'moe_w_router': nrm(ks[21], (N_MOE_LAYERS, D_MODEL, N_EXPERTS), D_MODEL ** -0.5),
        'moe_b_router': nrm(ks[22], (N_MOE_LAYERS, N_EXPERTS), 0.01),
        'moe_w_gu': nrm(ks[23], (N_MOE_LAYERS, N_EXPERTS, D_MODEL, 2 * D_FF_EXPERT), D_MODEL ** -0.5),
        'moe_w_down': nrm(ks[24], (N_MOE_LAYERS, N_EXPERTS, D_FF_EXPERT, D_MODEL), D_FF_EXPERT ** -0.5),
    }


def reference(x_prompt, x_sample, state_gla_S, state_mlstm_C, state_mlstm_n, state_mlstm_m,
              norm_mix, norm_ffn, norm_final,
              gla_w_in, gla_w_a2, gla_b_a, gla_g_norm, gla_w_out,
              mlstm_w_in, mlstm_b_gate, mlstm_g_norm, mlstm_w_out,
              ffn_w_gu, ffn_w_down,
              moe_w_router, moe_b_router, moe_w_gu, moe_w_down):
    weights = (norm_mix, norm_ffn, norm_final,
               gla_w_in, gla_w_a2, gla_b_a, gla_g_norm, gla_w_out,
               mlstm_w_in, mlstm_b_gate, mlstm_g_norm, mlstm_w_out,
               ffn_w_gu, ffn_w_down,
               moe_w_router, moe_b_router, moe_w_gu, moe_w_down)
    b = x_prompt.shape[0]
    z_S = jnp.zeros((N_GLA_LAYERS, b, GLA_HEADS, GLA_DK, GLA_DV), state_gla_S.dtype)
    z_C = jnp.zeros((N_MLSTM_LAYERS, b, MLSTM_HEADS, MLSTM_DK, MLSTM_DV), state_mlstm_C.dtype)
    z_n = jnp.zeros((N_MLSTM_LAYERS, b, MLSTM_HEADS, MLSTM_DK), state_mlstm_n.dtype)
    z_m = jnp.zeros((N_MLSTM_LAYERS, b, MLSTM_HEADS), state_mlstm_m.dtype)
    y_prompt, S_p, C_p, n_p, m_p = trunk(x_prompt, z_S, z_C, z_n, z_m, weights)
    y_sample, S_s, C_s, n_s, m_s = trunk(x_sample, state_gla_S, state_mlstm_C, state_mlstm_n,
                                         state_mlstm_m, weights)
    return (y_prompt, y_sample, S_p, C_p, n_p, m_p, S_s, C_s, n_s, m_s)
```

```python
import functools

import jax
import jax.numpy as jnp
from jax import lax
from jax.experimental import pallas as pl
from jax.experimental.pallas import tpu as pltpu

F32 = jnp.float32
BF16 = jnp.bfloat16
I32 = jnp.int32

EPS = 1e-6
GLA_HEADS = 4
GLA_RANK = 16
GLA_TAU = 16.0
MLSTM_HEADS = 8
GATE_CAP = 15.0
CHUNK = 64
N_EXPERTS = 8
NEG_BIG = -1e30

LANES = 128
TM = 512
TM_EXPERT = 256
SAMPLE_PAD_T = 8
VMEM_LIMIT = 56 * 1024 * 1024


def _dot(a, b):
    return jnp.dot(a, b, preferred_element_type=F32)


def _dot_nt(a, b):
    return lax.dot_general(a, b, (((1,), (1,)), ((), ())), preferred_element_type=F32)


def _dot_tn(a, b):
    return lax.dot_general(a, b, (((0,), (0,)), ((), ())), preferred_element_type=F32)


def _sigmoid(x):
    return 1.0 / (1.0 + jnp.exp(-x))


def _log_sigmoid(x):
    return jnp.minimum(x, 0.0) - jnp.log1p(jnp.exp(-jnp.abs(x)))


def _rms(x, g):
    return x * lax.rsqrt(jnp.mean(x * x, axis=-1, keepdims=True) + EPS) * g


def _cumsum_rows(x, n):
    ridx = lax.broadcasted_iota(I32, x.shape, 0)
    s = 1
    while s < n:
        x = x + jnp.where(ridx >= s, pltpu.roll(x, s, 0), 0.0)
        s *= 2
    return x


def _params(sem):
    return pltpu.CompilerParams(dimension_semantics=sem, vmem_limit_bytes=VMEM_LIMIT)


IN_PROJ_COL_CHUNKS = 4


def _gla_in_proj_kernel(x_ref, g_ref, w_ref, wg_ref, w2_ref, b_ref, z_ref, gate_ref):
    hn = _rms(x_ref[...], g_ref[...]).astype(BF16)
    width = w_ref.shape[1] // IN_PROJ_COL_CHUNKS
    for c in range(IN_PROJ_COL_CHUNKS):
        cs = slice(c * width, (c + 1) * width)
        z_ref[:, cs] = _dot(hn, w_ref[:, cs])
    a = _dot(hn, wg_ref[...])
    la = _dot(a.astype(BF16), w2_ref[...]) + b_ref[...]
    gate_ref[...] = _log_sigmoid(la) * (1.0 / GLA_TAU)


def _gla_in_proj(x, g, w_main, w_gate, w2, b):
    n, d = x.shape
    wz = w_main.shape[1]
    gw = b.shape[1]
    return pl.pallas_call(
        _gla_in_proj_kernel,
        grid=(n // TM,),
        in_specs=[
            pl.BlockSpec((TM, d), lambda i: (i, 0)),
            pl.BlockSpec((1, d), lambda i: (0, 0)),
            pl.BlockSpec((d, wz), lambda i: (0, 0)),
            pl.BlockSpec((d, LANES), lambda i: (0, 0)),
            pl.BlockSpec(w2.shape, lambda i: (0, 0)),
            pl.BlockSpec((1, gw), lambda i: (0, 0)),
        ],
        out_specs=[
            pl.BlockSpec((TM, wz), lambda i: (i, 0)),
            pl.BlockSpec((TM, gw), lambda i: (i, 0)),
        ],
        out_shape=[jax.ShapeDtypeStruct((n, wz), F32), jax.ShapeDtypeStruct((n, gw), F32)],
        compiler_params=_params(("arbitrary",)),
        name="in_proj_gla",
    )(x, g, w_main, w_gate, w2, b)


def _gla_kernel(q_ref, k_ref, v_ref, r_ref, la_ref, s0_ref, gn_ref, u_ref, st_ref, st_sc,
                *, chunk, tb, nblk):
    heads = GLA_HEADS
    dk = q_ref.shape[1] // heads
    dv = v_ref.shape[1] // heads
    c = pl.program_id(1)

    @pl.when(c == 0)
    def _():
        for h in range(heads):
            st_sc[h] = s0_ref[0, h].T

    ri = lax.broadcasted_iota(I32, (chunk, chunk), 0)
    ci = lax.broadcasted_iota(I32, (chunk, chunk), 1)
    causal = ri >= ci
    kscale = dk ** -0.5
    for j in range(tb // chunk):
        rows = slice(j * chunk, (j + 1) * chunk)
        b_all = _cumsum_rows(la_ref[rows, :], chunk)
        for h in range(heads):
            ks = slice(h * dk, (h + 1) * dk)
            vs = slice(h * dv, (h + 1) * dv)
            b = b_all[:, ks]
            b_end = b[chunk - 1:chunk, :]
            q = q_ref[rows, ks]
            k = k_ref[rows, ks] * kscale
            qg = (q * jnp.exp(b)).astype(BF16)
            kg = (k * jnp.exp(-b)).astype(BF16)
            ke = (k * jnp.exp(b_end - b)).astype(BF16)
            v = v_ref[rows, vs].astype(BF16)
            a = jnp.where(causal, _dot_nt(qg, kg), 0.0).astype(BF16)
            st = st_sc[h]
            o = _dot_nt(qg, st.astype(BF16)) + _dot(a, v)
            st_sc[h] = st * jnp.exp(b_end) + _dot_tn(v, ke)
            on = _rms(o, gn_ref[h:h + 1, :])
            r = r_ref[rows, vs]
            u_ref[rows, vs] = (r * _sigmoid(r) * on).astype(BF16)

    @pl.when(c == nblk - 1)
    def _():
        for h in range(heads):
            st_ref[0, h] = st_sc[h].T


def _gla_recurrence(z, log_a, s0, g_norm, nseq, t, chunk, tb):
    qk_w = log_a.shape[1]
    v_w = 2 * qk_w
    nblk = t // tb
    heads, dk, dv = s0.shape[1:]
    kern = functools.partial(_gla_kernel, chunk=chunk, tb=tb, nblk=nblk)
    row = lambda b, c: b * nblk + c
    return pl.pallas_call(
        kern,
        grid=(nseq, nblk),
        in_specs=[
            pl.BlockSpec((tb, qk_w), lambda b, c: (row(b, c), 0)),
            pl.BlockSpec((tb, qk_w), lambda b, c: (row(b, c), 1)),
            pl.BlockSpec((tb, v_w), lambda b, c: (row(b, c), 1)),
            pl.BlockSpec((tb, v_w), lambda b, c: (row(b, c), 2)),
            pl.BlockSpec((tb, qk_w), lambda b, c: (row(b, c), 0)),
            pl.BlockSpec((1, heads, dk, dv), lambda b, c: (b, 0, 0, 0)),
            pl.BlockSpec((heads, dv), lambda b, c: (0, 0)),
        ],
        out_specs=[
            pl.BlockSpec((tb, v_w), lambda b, c: (row(b, c), 0)),
            pl.BlockSpec((1, heads, dk, dv), lambda b, c: (b, 0, 0, 0)),
        ],
        out_shape=[
            jax.ShapeDtypeStruct((nseq * t, v_w), BF16),
            jax.ShapeDtypeStruct(s0.shape, F32),
        ],
        scratch_shapes=[pltpu.VMEM((heads, dv, dk), F32)],
        compiler_params=_params(("arbitrary", "arbitrary")),
        name="gla_recurrence_t%d" % t,
    )(z, z, z, z, log_a, s0, g_norm)


def _mlstm_kernel(q_ref, k_ref, v_ref, op_ref, gt_ref, c0_ref, n0_ref, m0_ref, gn_ref,
                  u_ref, ct_ref, nt_ref, mt_ref, c_sc, n_sc, m_sc, *, chunk, tb, nblk):
    heads = MLSTM_HEADS
    pairs = heads // 2
    dv = v_ref.shape[1] // heads
    dk = q_ref.shape[1] // heads
    c = pl.program_id(1)

    @pl.when(c == 0)
    def _():
        c_sc[...] = c0_ref[0]
        n_sc[...] = n0_ref[0]
        m_sc[...] = m0_ref[0]

    ri = lax.broadcasted_iota(I32, (chunk, chunk), 0)
    ci = lax.broadcasted_iota(I32, (chunk, chunk), 1)
    causal = ri >= ci
    lane = lax.broadcasted_iota(I32, (1, LANES), 1)
    rowi = lax.broadcasted_iota(I32, (LANES, 1), 0)
    kscale = dk ** -0.5
    for j in range(tb // chunk):
        rows = slice(j * chunk, (j + 1) * chunk)
        gts = gt_ref[rows, :]
        fa = _cumsum_rows(gts, chunk)
        bx = gts - pltpu.roll(fa, LANES - heads, 1)
        if chunk < LANES:
            bx = jnp.concatenate([bx, jnp.zeros((LANES - chunk, LANES), F32)], axis=0)
        bxt = bx.T
        for p in range(pairs):
            ps = slice(p * LANES, (p + 1) * LANES)
            qp = q_ref[rows, ps]
            kp = k_ref[rows, ps] * kscale
            kpb = kp.astype(BF16)
            cp = c_sc[p]
            npr = n_sc[p:p + 1, :]
            w_c = []
            k_sc = []
            for jj in range(2):
                h = 2 * p + jj
                hs = slice(h * dv, (h + 1) * dv)
                qh = jnp.where((lane >= jj * dk) & (lane < (jj + 1) * dk), qp, 0.0)
                qhb = qh.astype(BF16)
                fc = fa[:, heads + h:heads + h + 1]
                ic = gts[:, h:h + 1]
                brow = bxt[h:h + 1, 0:chunk]
                m_prev = m_sc[h:h + 1, 0:1]
                d = jnp.where(causal, fc + brow, -jnp.inf)
                inter = fc + m_prev
                m_t = jnp.maximum(inter, jnp.max(d, axis=-1, keepdims=True))
                w_i = jnp.exp(inter - m_t)
                s = _dot_nt(qhb, kpb) * jnp.exp(d - m_t)
                v = v_ref[rows, hs].astype(BF16)
                num = w_i * _dot(qhb, cp.astype(BF16)) + _dot(s.astype(BF16), v)
                qn = jnp.sum(qh * npr, axis=-1, keepdims=True)
                den = w_i * qn + jnp.sum(s, axis=-1, keepdims=True)
                hh = num / jnp.maximum(jnp.abs(den), jnp.exp(-m_t))
                m_new = m_t[chunk - 1:chunk, :]
                f_end = fc[chunk - 1:chunk, :]
                w_c.append(jnp.exp(f_end + m_prev - m_new))
                k_sc.append(jnp.exp(f_end - fc + ic - m_new))
                hn = _rms(hh, gn_ref[h:h + 1, :])
                u_ref[rows, hs] = (_sigmoid(op_ref[rows, hs]) * hn).astype(BF16)
                m_sc[h:h + 1, :] = jnp.broadcast_to(m_new, (1, LANES))
            ke = kp * jnp.where(lane < dk, k_sc[0], k_sc[1])
            vp = v_ref[rows, 2 * p * dv:(2 * p + 2) * dv].astype(BF16)
            full = _dot_tn(ke.astype(BF16), vp)
            upd = jnp.where(rowi < dk, full[:, :dv], full[:, dv:])
            c_sc[p] = jnp.where(rowi < dk, w_c[0], w_c[1]) * cp + upd
            n_sc[p:p + 1, :] = (jnp.where(lane < dk, w_c[0], w_c[1]) * npr
                                + jnp.sum(ke, axis=0, keepdims=True))

    @pl.when(c == nblk - 1)
    def _():
        ct_ref[0] = c_sc[...]
        nt_ref[0] = n_sc[...]
        mt_ref[0] = m_sc[...]


def _mlstm_recurrence(z, gates, c0, n0, m0, g_norm, nseq, t, chunk, tb):
    v_w = g_norm.shape[0] * g_norm.shape[1]
    qk_w = v_w // 2
    nblk = t // tb
    kern = functools.partial(_mlstm_kernel, chunk=chunk, tb=tb, nblk=nblk)
    row = lambda b, c: b * nblk + c
    st4 = lambda b, c: (b, 0, 0, 0)
    st3 = lambda b, c: (b, 0, 0)
    return pl.pallas_call(
        kern,
        grid=(nseq, nblk),
        in_specs=[
            pl.BlockSpec((tb, qk_w), lambda b, c: (row(b, c), 0)),
            pl.BlockSpec((tb, qk_w), lambda b, c: (row(b, c), 1)),
            pl.BlockSpec((tb, v_w), lambda b, c: (row(b, c), 1)),
            pl.BlockSpec((tb, v_w), lambda b, c: (row(b, c), 2)),
            pl.BlockSpec((tb, LANES), lambda b, c: (row(b, c), 0)),
            pl.BlockSpec((1,) + c0.shape[1:], st4),
            pl.BlockSpec((1,) + n0.shape[1:], st3),
            pl.BlockSpec((1,) + m0.shape[1:], st3),
            pl.BlockSpec(g_norm.shape, lambda b, c: (0, 0)),
        ],
        out_specs=[
            pl.BlockSpec((tb, v_w), lambda b, c: (row(b, c), 0)),
            pl.BlockSpec((1,) + c0.shape[1:], st4),
            pl.BlockSpec((1,) + n0.shape[1:], st3),
            pl.BlockSpec((1,) + m0.shape[1:], st3),
        ],
        out_shape=[
            jax.ShapeDtypeStruct((nseq * t, v_w), BF16),
            jax.ShapeDtypeStruct(c0.shape, F32),
            jax.ShapeDtypeStruct(n0.shape, F32),
            jax.ShapeDtypeStruct(m0.shape, F32),
        ],
        scratch_shapes=[
            pltpu.VMEM(c0.shape[1:], F32),
            pltpu.VMEM(n0.shape[1:], F32),
            pltpu.VMEM(m0.shape[1:], F32),
        ],
        compiler_params=_params(("arbitrary", "arbitrary")),
        name="mlstm_recurrence_t%d" % t,
    )(z, z, z, z, gates, c0, n0, m0, g_norm)


def _out_proj_kernel(u_ref, x_ref, w_ref, g_ref, xo_ref, hn_ref):
    x = x_ref[...] + _dot(u_ref[...], w_ref[...])
    xo_ref[...] = x
    hn_ref[...] = _rms(x, g_ref[...]).astype(hn_ref.dtype)


def _out_proj(u, x, w, g):
    n, d = x.shape
    return pl.pallas_call(
        _out_proj_kernel,
        grid=(n // TM,),
        in_specs=[
            pl.BlockSpec((TM, u.shape[1]), lambda i: (i, 0)),
            pl.BlockSpec((TM, d), lambda i: (i, 0)),
            pl.BlockSpec(w.shape, lambda i: (0, 0)),
            pl.BlockSpec((1, d), lambda i: (0, 0)),
        ],
        out_specs=[pl.BlockSpec((TM, d), lambda i: (i, 0)), pl.BlockSpec((TM, d), lambda i: (i, 0))],
        out_shape=[jax.ShapeDtypeStruct((n, d), F32), jax.ShapeDtypeStruct((n, d), BF16)],
        compiler_params=_params(("arbitrary",)),
        name="out_proj",
    )(u, x, w, g)


def _out_proj_router_kernel(u_ref, x_ref, w_ref, g_ref, wr_ref, br_ref,
                            xo_ref, hn_ref, gate_ref, idx_ref, cnt_ref, cnt_sc):
    i = pl.program_id(0)

    @pl.when(i == 0)
    def _():
        cnt_sc[...] = jnp.zeros_like(cnt_sc)

    x = x_ref[...] + _dot(u_ref[...], w_ref[...])
    xo_ref[...] = x
    hn = _rms(x, g_ref[...])
    hn_ref[...] = hn
    logits = jnp.dot(hn, wr_ref[...], preferred_element_type=F32,
                     precision=lax.Precision.HIGHEST) + br_ref[...]
    tm = logits.shape[0]
    lane = lax.broadcasted_iota(I32, logits.shape, 1)
    valid = lane < N_EXPERTS
    logits = jnp.where(valid, logits, -jnp.inf)
    ex = jnp.exp(logits - jnp.max(logits, axis=-1, keepdims=True))
    probs = jnp.where(valid, ex / jnp.sum(ex, axis=-1, keepdims=True), -1.0)
    p1 = jnp.max(probs, axis=-1, keepdims=True)
    i1 = jnp.min(jnp.where(probs == p1, lane, LANES), axis=-1, keepdims=True)
    rest = jnp.where(lane == i1, -1.0, probs)
    p2 = jnp.max(rest, axis=-1, keepdims=True)
    i2 = jnp.min(jnp.where(rest == p2, lane, LANES), axis=-1, keepdims=True)
    tot = p1 + p2
    onehot = ((lane == i1) | (lane == i2)).astype(BF16)
    rr = lax.broadcasted_iota(I32, (tm, tm), 0)
    cc = lax.broadcasted_iota(I32, (tm, tm), 1)
    strict = (rr > cc).astype(BF16)
    before = _dot(strict, onehot) + cnt_sc[...]
    r1 = jnp.sum(jnp.where(lane == i1, before, 0.0), axis=-1, keepdims=True)
    r2 = jnp.sum(jnp.where(lane == i2, before, 0.0), axis=-1, keepdims=True)
    cnt_sc[...] = cnt_sc[...] + jnp.sum(onehot.astype(F32), axis=0, keepdims=True)
    gate_ref[...] = jnp.where(lane == 0, p1 / tot, jnp.where(lane == 1, p2 / tot, 0.0))
    idx_ref[...] = jnp.where(lane == 0, i1,
                             jnp.where(lane == 1, i2,
                                       jnp.where(lane == 2, r1.astype(I32),
                                                 jnp.where(lane == 3, r2.astype(I32), 0))))
    cnt_ref[...] = cnt_sc[...]


def _out_proj_router(u, x, w, g, w_router, b_router):
    n, d = x.shape
    row = lambda i: (i, 0)
    fix = lambda i: (0, 0)
    return pl.pallas_call(
        _out_proj_router_kernel,
        grid=(n // TM,),
        in_specs=[
            pl.BlockSpec((TM, u.shape[1]), row),
            pl.BlockSpec((TM, d), row),
            pl.BlockSpec(w.shape, fix),
            pl.BlockSpec((1, d), fix),
            pl.BlockSpec(w_router.shape, fix),
            pl.BlockSpec((1, LANES), fix),
        ],
        out_specs=[
            pl.BlockSpec((TM, d), row),
            pl.BlockSpec((TM, d), row),
            pl.BlockSpec((TM, LANES), row),
            pl.BlockSpec((TM, LANES), row),
            pl.BlockSpec((1, LANES), fix),
        ],
        out_shape=[
            jax.ShapeDtypeStruct((n, d), F32),
            jax.ShapeDtypeStruct((n, d), F32),
            jax.ShapeDtypeStruct((n, LANES), F32),
            jax.ShapeDtypeStruct((n, LANES), I32),
            jax.ShapeDtypeStruct((1, LANES), F32),
        ],
        scratch_shapes=[pltpu.VMEM((1, LANES), F32)],
        compiler_params=_params(("arbitrary",)),
        name="out_proj_router",
    )(u, x, w, g, w_router, b_router)


def _swiglu_kernel(hn_ref, x_ref, wg_ref, wu_ref, wd_ref, g_ref, xo_ref, ho_ref, acc_sc, *, nf):
    f = pl.program_id(1)
    hn = hn_ref[...]
    gate = _dot(hn, wg_ref[...])
    up = _dot(hn, wu_ref[...])
    act = (gate * _sigmoid(gate) * up).astype(BF16)
    part = _dot(act, wd_ref[...])

    @pl.when(f == 0)
    def _():
        acc_sc[...] = x_ref[...] + part

    @pl.when(f > 0)
    def _():
        acc_sc[...] = acc_sc[...] + part

    @pl.when(f == nf - 1)
    def _():
        x = acc_sc[...]
        xo_ref[...] = x
        ho_ref[...] = _rms(x, g_ref[...]).astype(BF16)


def _swiglu(hn, x, w_gu, w_down, g, nf):
    n, d = x.shape
    dff = w_down.shape[0]
    tf = dff // nf
    kern = functools.partial(_swiglu_kernel, nf=nf)
    row = lambda i, f: (i, 0)
    return pl.pallas_call(
        kern,
        grid=(n // TM, nf),
        in_specs=[
            pl.BlockSpec((TM, d), row),
            pl.BlockSpec((TM, d), row),
            pl.BlockSpec((d, tf), lambda i, f: (0, f)),
            pl.BlockSpec((d, tf), lambda i, f: (0, nf + f)),
            pl.BlockSpec((tf, d), lambda i, f: (f, 0)),
            pl.BlockSpec((1, d), lambda i, f: (0, 0)),
        ],
        out_specs=[pl.BlockSpec((TM, d), row), pl.BlockSpec((TM, d), row)],
        out_shape=[jax.ShapeDtypeStruct((n, d), F32), jax.ShapeDtypeStruct((n, d), BF16)],
        scratch_shapes=[pltpu.VMEM((TM, d), F32)],
        compiler_params=_params(("arbitrary", "arbitrary")),
        name="dense_swiglu",
    )(hn, x, w_gu, w_gu, w_down, g)


def _row_copy(src_ref, src_row, dst_ref, dst_row, sem):
    return pltpu.make_async_copy(src_ref.at[pl.ds(src_row, 1), :], dst_ref.at[pl.ds(dst_row, 1), :], sem)


def _dispatch_kernel(dest_ref, hn_ref, xs_in_ref, xs_ref, sem):
    del xs_in_ref
    i = pl.program_id(0)
    tm = hn_ref.shape[0]

    def issue(r, carry):
        t = i * tm + r
        _row_copy(hn_ref, r, xs_ref, dest_ref[2 * t], sem).start()
        _row_copy(hn_ref, r, xs_ref, dest_ref[2 * t + 1], sem).start()
        return carry

    lax.fori_loop(0, tm, issue, 0)

    def drain(r, carry):
        _row_copy(hn_ref, 0, xs_ref, 0, sem).wait()
        _row_copy(hn_ref, 0, xs_ref, 0, sem).wait()
        return carry

    lax.fori_loop(0, tm, drain, 0)


def _dispatch(dest_flat, hn, xs_zero):
    n, d = hn.shape
    return pl.pallas_call(
        _dispatch_kernel,
        grid_spec=pltpu.PrefetchScalarGridSpec(
            num_scalar_prefetch=1,
            grid=(n // TM,),
            in_specs=[
                pl.BlockSpec((TM, d), lambda i, dest: (i, 0)),
                pl.BlockSpec(memory_space=pl.ANY),
            ],
            out_specs=pl.BlockSpec(memory_space=pl.ANY),
            scratch_shapes=[pltpu.SemaphoreType.DMA(())],
        ),
        out_shape=jax.ShapeDtypeStruct(xs_zero.shape, xs_zero.dtype),
        input_output_aliases={2: 0},
        compiler_params=_params(("arbitrary",)),
        name="moe_dispatch",
    )(dest_flat, hn, xs_zero)


def _expert_kernel(te_ref, tv_ref, xs_ref, wg_ref, wu_ref, wd_ref, ys_ref, *, nchunk):
    i = pl.program_id(0)

    @pl.when(tv_ref[i] > 0)
    def _():
        x = xs_ref[...].astype(BF16)
        width = wg_ref.shape[2] // nchunk
        acc = None
        for c in range(nchunk):
            cs = slice(c * width, (c + 1) * width)
            gate = _dot(x, wg_ref[0, :, cs])
            up = _dot(x, wu_ref[0, :, cs])
            act = (gate * _sigmoid(gate) * up).astype(BF16)
            part = _dot(act, wd_ref[0, cs, :])
            acc = part if acc is None else acc + part
        ys_ref[...] = acc

    @pl.when(tv_ref[i] == 0)
    def _():
        ys_ref[...] = jnp.zeros_like(ys_ref)


def _experts(tile_expert, tile_valid, xs, w_gu, w_down):
    r, d = xs.shape
    dff = w_down.shape[1]
    kern = functools.partial(_expert_kernel, nchunk=7)
    return pl.pallas_call(
        kern,
        grid_spec=pltpu.PrefetchScalarGridSpec(
            num_scalar_prefetch=2,
            grid=(r // TM_EXPERT,),
            in_specs=[
                pl.BlockSpec((TM_EXPERT, d), lambda i, te, tv: (i, 0)),
                pl.BlockSpec((1, d, dff), lambda i, te, tv: (te[i], 0, 0)),
                pl.BlockSpec((1, d, dff), lambda i, te, tv: (te[i], 0, 1)),
                pl.BlockSpec((1, dff, d), lambda i, te, tv: (te[i], 0, 0)),
            ],
            out_specs=pl.BlockSpec((TM_EXPERT, d), lambda i, te, tv: (i, 0)),
        ),
        out_shape=jax.ShapeDtypeStruct((r, d), F32),
        compiler_params=_params(("arbitrary",)),
        name="moe_experts",
    )(tile_expert, tile_valid, xs, w_gu, w_gu, w_down)


def _combine_kernel(dest_ref, ys_ref, x_ref, gate_ref, g_ref, y_ref, buf, sem):
    i = pl.program_id(0)
    tm = x_ref.shape[0]

    def issue(r, carry):
        t = i * tm + r
        _row_copy(ys_ref, dest_ref[2 * t], buf.at[0], r, sem).start()
        _row_copy(ys_ref, dest_ref[2 * t + 1], buf.at[1], r, sem).start()
        return carry

    lax.fori_loop(0, tm, issue, 0)

    def drain(r, carry):
        _row_copy(ys_ref, 0, buf.at[0], 0, sem).wait()
        _row_copy(ys_ref, 0, buf.at[1], 0, sem).wait()
        return carry

    lax.fori_loop(0, tm, drain, 0)
    gate = gate_ref[...]
    y = x_ref[...] + gate[:, 0:1] * buf[0] + gate[:, 1:2] * buf[1]
    y_ref[...] = _rms(y, g_ref[...])


def _combine(dest_flat, ys, x, gates, g):
    n, d = x.shape
    return pl.pallas_call(
        _combine_kernel,
        grid_spec=pltpu.PrefetchScalarGridSpec(
            num_scalar_prefetch=1,
            grid=(n // TM,),
            in_specs=[
                pl.BlockSpec(memory_space=pl.ANY),
                pl.BlockSpec((TM, d), lambda i, dest: (i, 0)),
                pl.BlockSpec((TM, LANES), lambda i, dest: (i, 0)),
                pl.BlockSpec((1, d), lambda i, dest: (0, 0)),
            ],
            out_specs=pl.BlockSpec((TM, d), lambda i, dest: (i, 0)),
            scratch_shapes=[pltpu.VMEM((2, TM, d), F32), pltpu.SemaphoreType.DMA(())],
        ),
        out_shape=jax.ShapeDtypeStruct((n, d), F32),
        compiler_params=_params(("arbitrary",)),
        name="moe_combine",
    )(dest_flat, ys, x, gates, g)


def _routing_tables(idx, counts, n_rows):
    e = idx[:, 0:2]
    rank = idx[:, 2:4]
    cnt = counts[0, :N_EXPERTS].astype(I32)
    padded = ((cnt + TM_EXPERT - 1) // TM_EXPERT) * TM_EXPERT
    ends = jnp.cumsum(padded)
    starts = ends - padded
    dest = (starts[e] + rank).reshape(-1)
    tile_start = jnp.arange(n_rows // TM_EXPERT, dtype=I32) * TM_EXPERT
    tile_valid = (tile_start < ends[-1]).astype(I32)
    tile_expert = jnp.searchsorted(ends, tile_start, side="right").astype(I32)
    last_expert = jnp.searchsorted(ends, ends[-1] - 1, side="right").astype(I32)
    tile_expert = jnp.minimum(tile_expert, last_expert)
    return dest, tile_expert, tile_valid


def _pad_time(a, nseq, t, fill=None):
    w = a.shape[1]
    a = a.reshape(nseq, t, w)
    a = jnp.pad(a, ((0, 0), (0, SAMPLE_PAD_T - t), (0, 0)))
    if fill is not None:
        pad_row = jnp.arange(SAMPLE_PAD_T)[None, :, None] >= t
        lane = jnp.arange(w)[None, None, :]
        a = jnp.where(pad_row & (lane < MLSTM_HEADS), fill, a)
    return a.reshape(nseq * SAMPLE_PAD_T, w)


def _unpad_time(a, nseq, t):
    w = a.shape[1]
    return a.reshape(nseq, SAMPLE_PAD_T, w)[:, :t].reshape(nseq * t, w)


def kernel(x_prompt, x_sample, state_gla_S, state_mlstm_C, state_mlstm_n, state_mlstm_m,
           norm_mix, norm_ffn, norm_final,
           gla_w_in, gla_w_a2, gla_b_a, gla_g_norm, gla_w_out,
           mlstm_w_in, mlstm_b_gate, mlstm_g_norm, mlstm_w_out,
           ffn_w_gu, ffn_w_down,
           moe_w_router, moe_b_router, moe_w_gu, moe_w_down):
    bp, tp, d = x_prompt.shape
    bs, ts, _ = x_sample.shape
    n_p = bp * tp
    n_s = bs * ts
    n = n_p + n_s
    assert n % TM == 0 and n_p % TM == 0 and tp % (2 * CHUNK) == 0 and ts <= SAMPLE_PAD_T
    assert norm_mix.shape[0] == 2, "one GLA layer followed by one mLSTM layer"
    qk_w = d // 2
    main_w = 2 * qk_w + 2 * d

    x = jnp.concatenate([x_prompt.reshape(n_p, d), x_sample.reshape(n_s, d)], axis=0)

    w_in = gla_w_in[0]
    w_lr = jnp.pad(w_in[:, main_w:], ((0, 0), (0, LANES - GLA_RANK))).astype(BF16)
    w_a2 = jnp.pad(gla_w_a2[0], ((0, LANES - GLA_RANK), (0, 0))).astype(BF16)
    z, log_a = _gla_in_proj(x, norm_mix[0][None], w_in[:, :main_w].astype(BF16), w_lr, w_a2,
                            gla_b_a[0][None])
    s0_p = jnp.zeros((bp,) + state_gla_S.shape[2:], F32)
    u_p, s_p = _gla_recurrence(z, log_a, s0_p, gla_g_norm[0], bp, tp, CHUNK, 2 * CHUNK)
    u_s, s_s = _gla_recurrence(_pad_time(z[n_p:], bs, ts), _pad_time(log_a[n_p:], bs, ts),
                               state_gla_S[0], gla_g_norm[0], bs, SAMPLE_PAD_T, SAMPLE_PAD_T,
                               SAMPLE_PAD_T)
    u = jnp.concatenate([u_p, _unpad_time(u_s, bs, ts)], axis=0)
    x, hn = _out_proj(u, x, gla_w_out[0].astype(BF16), norm_ffn[0][None])
    x, hn = _swiglu(hn, x, ffn_w_gu[0].astype(BF16), ffn_w_down[0].astype(BF16),
                    norm_mix[1][None], 2)

    w_in = mlstm_w_in[0]
    w_gt = jnp.pad(w_in[:, main_w:], ((0, 0), (0, LANES - 2 * MLSTM_HEADS))).astype(BF16)
    b_gt = jnp.pad(mlstm_b_gate[0], (0, LANES - 2 * MLSTM_HEADS))[None]
    z, gates = _mlstm_in_proj(hn, w_in[:, :main_w].astype(BF16), w_gt, b_gt)
    pairs = MLSTM_HEADS // 2
    dk2 = 2 * state_mlstm_C.shape[3]
    dvm = state_mlstm_C.shape[4]
    c0_p = jnp.zeros((bp, pairs, dk2, dvm), F32)
    n0_p = jnp.zeros((bp, pairs, dk2), F32)
    m0_p = jnp.zeros((bp, MLSTM_HEADS, LANES), F32)
    u_p, c_p, nn_p, m_p = _mlstm_recurrence(z, gates, c0_p, n0_p, m0_p,
                                            mlstm_g_norm[0], bp, tp, CHUNK, 2 * CHUNK)
    c0_s = state_mlstm_C[0].reshape(bs, pairs, dk2, dvm)
    n0_s = state_mlstm_n[0].reshape(bs, pairs, dk2)
    m0_s = jnp.broadcast_to(state_mlstm_m[0][:, :, None], (bs, MLSTM_HEADS, LANES))
    u_s, c_s, nn_s, m_s = _mlstm_recurrence(
        _pad_time(z[n_p:], bs, ts), _pad_time(gates[n_p:], bs, ts, fill=NEG_BIG),
        c0_s, n0_s, m0_s, mlstm_g_norm[0], bs, SAMPLE_PAD_T, SAMPLE_PAD_T, SAMPLE_PAD_T)
    u = jnp.concatenate([u_p, _unpad_time(u_s, bs, ts)], axis=0)

    w_r = jnp.pad(moe_w_router[0], ((0, 0), (0, LANES - N_EXPERTS)))
    b_r = jnp.pad(moe_b_router[0], (0, LANES - N_EXPERTS))[None]
    x, hn, route_g, route_i, counts = _out_proj_router(
        u, x, mlstm_w_out[0].astype(BF16), norm_ffn[1][None], w_r, b_r)
    n_rows = -(-(2 * n + N_EXPERTS * (TM_EXPERT - 1)) // TM_EXPERT) * TM_EXPERT
    dest, tile_expert, tile_valid = _routing_tables(route_i, counts, n_rows)
    xs = _dispatch(dest, hn, jnp.zeros((n_rows, d), F32))
    ys = _experts(tile_expert, tile_valid, xs, moe_w_gu[0].astype(BF16), moe_w_down[0].astype(BF16))
    y = _combine(dest, ys, x, route_g, norm_final[None])

    y_prompt = y[:n_p].reshape(bp, tp, d)
    y_sample = y[n_p:].reshape(bs, ts, d)
    hd = state_mlstm_C.shape[2:]
    return (y_prompt, y_sample,
            s_p[None], c_p.reshape((1, bp) + hd), nn_p.reshape(1, bp, hd[0], hd[1]), m_p[None, :, :, 0],
            s_s[None], c_s.reshape((1, bs) + hd), nn_s.reshape(1, bs, hd[0], hd[1]), m_s[None, :, :, 0])


def _mlstm_in_proj(hn, w_main, w_gate, b_gate):
    n, d = hn.shape
    wz = w_main.shape[1]

    def kern(h_ref, w_ref, wg_ref, b_ref, z_ref, gate_ref):
        h = h_ref[...]
        width = wz // IN_PROJ_COL_CHUNKS
        for c in range(IN_PROJ_COL_CHUNKS):
            cs = slice(c * width, (c + 1) * width)
            z_ref[:, cs] = _dot(h, w_ref[:, cs])
        gp = _dot(h, wg_ref[...]) + b_ref[...]
        gc = GATE_CAP * jnp.tanh(gp * (1.0 / GATE_CAP))
        lane = lax.broadcasted_iota(I32, gc.shape, 1)
        out = jnp.where(lane < MLSTM_HEADS, gc, _log_sigmoid(gc))
        gate_ref[...] = jnp.where(lane < 2 * MLSTM_HEADS, out, 0.0)

    return pl.pallas_call(
        kern,
        grid=(n // TM,),
        in_specs=[
            pl.BlockSpec((TM, d), lambda i: (i, 0)),
            pl.BlockSpec((d, wz), lambda i: (0, 0)),
            pl.BlockSpec((d, LANES), lambda i: (0, 0)),
            pl.BlockSpec((1, LANES), lambda i: (0, 0)),
        ],
        out_specs=[pl.BlockSpec((TM, wz), lambda i: (i, 0)), pl.BlockSpec((TM, LANES), lambda i: (i, 0))],
        out_shape=[jax.ShapeDtypeStruct((n, wz), F32), jax.ShapeDtypeStruct((n, LANES), F32)],
        compiler_params=_params(("arbitrary",)),
        name="in_proj_mlstm",
    )(hn, w_main, w_gate, b_gate)
```

```python
import functools

import jax
import jax.numpy as jnp
from jax import lax
from jax.experimental import pallas as pl
from jax.experimental.pallas import tpu as pltpu

F32 = jnp.float32
BF16 = jnp.bfloat16
I32 = jnp.int32

EPS = 1e-6
GLA_HEADS = 4
GLA_RANK = 16
GLA_TAU = 16.0
MLSTM_HEADS = 8
GATE_CAP = 15.0
CHUNK = 64
N_EXPERTS = 8
NEG_BIG = -1e30

LANES = 128
TM = 512
TM_EXPERT = 256
SAMPLE_PAD_T = 8
VMEM_LIMIT = 56 * 1024 * 1024


def _dot(a, b):
    return jnp.dot(a, b, preferred_element_type=F32)


def _dot_nt(a, b):
    return lax.dot_general(a, b, (((1,), (1,)), ((), ())), preferred_element_type=F32)


def _dot_tn(a, b):
    return lax.dot_general(a, b, (((0,), (0,)), ((), ())), preferred_element_type=F32)


def _sigmoid(x):
    return 1.0 / (1.0 + jnp.exp(-x))


def _log_sigmoid(x):
    return jnp.minimum(x, 0.0) - jnp.log1p(jnp.exp(-jnp.abs(x)))


def _rms(x, g):
    return x * lax.rsqrt(jnp.mean(x * x, axis=-1, keepdims=True) + EPS) * g


def _cumsum_rows(x, n):
    ridx = lax.broadcasted_iota(I32, x.shape, 0)
    s = 1
    while s < n:
        x = x + jnp.where(ridx >= s, pltpu.roll(x, s, 0), 0.0)
        s *= 2
    return x


def _params(sem):
    return pltpu.CompilerParams(dimension_semantics=sem, vmem_limit_bytes=VMEM_LIMIT)


IN_PROJ_COL_CHUNKS = 4


def _pick_rows(i, n_first, first_ref, second_ref):
    return jnp.where(i < n_first, first_ref[...], second_ref[...])


def _split_specs(n_first, width):
    return [pl.BlockSpec((TM, width), lambda i, *_: (jnp.minimum(i, n_first - 1), 0)),
            pl.BlockSpec((TM, width), lambda i, *_: (jnp.maximum(i - n_first, 0), 0))]


def _gla_in_proj_kernel(xp_ref, xs_ref, g_ref, w_ref, wg_ref, w2_ref, b_ref, z_ref, gate_ref, *,
                        n_first):
    x = _pick_rows(pl.program_id(0), n_first, xp_ref, xs_ref)
    hn = _rms(x, g_ref[...]).astype(BF16)
    width = w_ref.shape[1] // IN_PROJ_COL_CHUNKS
    for c in range(IN_PROJ_COL_CHUNKS):
        cs = slice(c * width, (c + 1) * width)
        z_ref[:, cs] = _dot(hn, w_ref[:, cs])
    a = _dot(hn, wg_ref[...])
    la = _dot(a.astype(BF16), w2_ref[...]) + b_ref[...]
    gate_ref[...] = _log_sigmoid(la) * (1.0 / GLA_TAU)


def _gla_in_proj(x_p, x_s, g, w_main, w_gate, w2, b):
    d = x_p.shape[1]
    n = x_p.shape[0] + x_s.shape[0]
    n_first = x_p.shape[0] // TM
    wz = w_main.shape[1]
    gw = b.shape[1]
    return pl.pallas_call(
        functools.partial(_gla_in_proj_kernel, n_first=n_first),
        grid=(n // TM,),
        in_specs=_split_specs(n_first, d) + [
            pl.BlockSpec((1, d), lambda i: (0, 0)),
            pl.BlockSpec((d, wz), lambda i: (0, 0)),
            pl.BlockSpec((d, LANES), lambda i: (0, 0)),
            pl.BlockSpec(w2.shape, lambda i: (0, 0)),
            pl.BlockSpec((1, gw), lambda i: (0, 0)),
        ],
        out_specs=[
            pl.BlockSpec((TM, wz), lambda i: (i, 0)),
            pl.BlockSpec((TM, gw), lambda i: (i, 0)),
        ],
        out_shape=[jax.ShapeDtypeStruct((n, wz), F32), jax.ShapeDtypeStruct((n, gw), F32)],
        compiler_params=_params(("arbitrary",)),
        name="in_proj_gla",
    )(x_p, x_s, g, w_main, w_gate, w2, b)


def _gla_kernel(q_ref, k_ref, v_ref, r_ref, la_ref, s0_ref, gn_ref, u_ref, st_ref, st_sc,
                *, chunk, tb, nblk):
    heads = GLA_HEADS
    dk = q_ref.shape[1] // heads
    dv = v_ref.shape[1] // heads
    c = pl.program_id(1)

    @pl.when(c == 0)
    def _():
        for h in range(heads):
            st_sc[h] = s0_ref[0, h].T

    ri = lax.broadcasted_iota(I32, (chunk, chunk), 0)
    ci = lax.broadcasted_iota(I32, (chunk, chunk), 1)
    causal = ri >= ci
    kscale = dk ** -0.5
    for j in range(tb // chunk):
        rows = slice(j * chunk, (j + 1) * chunk)
        b_all = _cumsum_rows(la_ref[rows, :], chunk)
        for h in range(heads):
            ks = slice(h * dk, (h + 1) * dk)
            vs = slice(h * dv, (h + 1) * dv)
            b = b_all[:, ks]
            b_end = b[chunk - 1:chunk, :]
            q = q_ref[rows, ks]
            k = k_ref[rows, ks] * kscale
            qg = (q * jnp.exp(b)).astype(BF16)
            kg = (k * jnp.exp(-b)).astype(BF16)
            ke = (k * jnp.exp(b_end - b)).astype(BF16)
            v = v_ref[rows, vs].astype(BF16)
            a = jnp.where(causal, _dot_nt(qg, kg), 0.0).astype(BF16)
            st = st_sc[h]
            o = _dot_nt(qg, st.astype(BF16)) + _dot(a, v)
            st_sc[h] = st * jnp.exp(b_end) + _dot_tn(v, ke)
            on = _rms(o, gn_ref[h:h + 1, :])
            r = r_ref[rows, vs]
            u_ref[rows, vs] = (r * _sigmoid(r) * on).astype(BF16)

    @pl.when(c == nblk - 1)
    def _():
        for h in range(heads):
            st_ref[0, h] = st_sc[h].T


def _gla_recurrence(z, log_a, s0, g_norm, nseq, t, chunk, tb):
    qk_w = log_a.shape[1]
    v_w = 2 * qk_w
    nblk = t // tb
    heads, dk, dv = s0.shape[1:]
    kern = functools.partial(_gla_kernel, chunk=chunk, tb=tb, nblk=nblk)
    row = lambda b, c: b * nblk + c
    return pl.pallas_call(
        kern,
        grid=(nseq, nblk),
        in_specs=[
            pl.BlockSpec((tb, qk_w), lambda b, c: (row(b, c), 0)),
            pl.BlockSpec((tb, qk_w), lambda b, c: (row(b, c), 1)),
            pl.BlockSpec((tb, v_w), lambda b, c: (row(b, c), 1)),
            pl.BlockSpec((tb, v_w), lambda b, c: (row(b, c), 2)),
            pl.BlockSpec((tb, qk_w), lambda b, c: (row(b, c), 0)),
            pl.BlockSpec((1, heads, dk, dv), lambda b, c: (b, 0, 0, 0)),
            pl.BlockSpec((heads, dv), lambda b, c: (0, 0)),
        ],
        out_specs=[
            pl.BlockSpec((tb, v_w), lambda b, c: (row(b, c), 0)),
            pl.BlockSpec((1, heads, dk, dv), lambda b, c: (b, 0, 0, 0)),
        ],
        out_shape=[
            jax.ShapeDtypeStruct((nseq * t, v_w), BF16),
            jax.ShapeDtypeStruct(s0.shape, F32),
        ],
        scratch_shapes=[pltpu.VMEM((heads, dv, dk), F32)],
        compiler_params=_params(("arbitrary", "arbitrary")),
        name="gla_recurrence_t%d" % t,
    )(z, z, z, z, log_a, s0, g_norm)


def _mlstm_kernel(q_ref, k_ref, v_ref, op_ref, gt_ref, c0_ref, n0_ref, m0_ref, gn_ref,
                  u_ref, ct_ref, nt_ref, mt_ref, c_sc, n_sc, m_sc, *, chunk, tb, nblk):
    heads = MLSTM_HEADS
    pairs = heads // 2
    dv = v_ref.shape[1] // heads
    dk = q_ref.shape[1] // heads
    c = pl.program_id(1)

    @pl.when(c == 0)
    def _():
        c_sc[...] = c0_ref[0]
        n_sc[...] = n0_ref[0]
        m_sc[...] = m0_ref[0]

    ri = lax.broadcasted_iota(I32, (chunk, chunk), 0)
    ci = lax.broadcasted_iota(I32, (chunk, chunk), 1)
    causal = ri >= ci
    lane = lax.broadcasted_iota(I32, (1, LANES), 1)
    rowi = lax.broadcasted_iota(I32, (LANES, 1), 0)
    kscale = dk ** -0.5
    for j in range(tb // chunk):
        rows = slice(j * chunk, (j + 1) * chunk)
        gts = gt_ref[rows, :]
        fa = _cumsum_rows(gts, chunk)
        bx = gts - pltpu.roll(fa, LANES - heads, 1)
        if chunk < LANES:
            bx = jnp.concatenate([bx, jnp.zeros((LANES - chunk, LANES), F32)], axis=0)
        bxt = bx.T
        for p in range(pairs):
            ps = slice(p * LANES, (p + 1) * LANES)
            qp = q_ref[rows, ps]
            kp = k_ref[rows, ps] * kscale
            kpb = kp.astype(BF16)
            cp = c_sc[p]
            npr = n_sc[p:p + 1, :]
            w_c = []
            k_sc = []
            for jj in range(2):
                h = 2 * p + jj
                hs = slice(h * dv, (h + 1) * dv)
                qh = jnp.where((lane >= jj * dk) & (lane < (jj + 1) * dk), qp, 0.0)
                qhb = qh.astype(BF16)
                fc = fa[:, heads + h:heads + h + 1]
                ic = gts[:, h:h + 1]
                brow = bxt[h:h + 1, 0:chunk]
                m_prev = m_sc[h:h + 1, 0:1]
                d = jnp.where(causal, fc + brow, -jnp.inf)
                inter = fc + m_prev
                m_t = jnp.maximum(inter, jnp.max(d, axis=-1, keepdims=True))
                w_i = jnp.exp(inter - m_t)
                s = _dot_nt(qhb, kpb) * jnp.exp(d - m_t)
                v = v_ref[rows, hs].astype(BF16)
                num = w_i * _dot(qhb, cp.astype(BF16)) + _dot(s.astype(BF16), v)
                qn = jnp.sum(qh * npr, axis=-1, keepdims=True)
                den = w_i * qn + jnp.sum(s, axis=-1, keepdims=True)
                hh = num / jnp.maximum(jnp.abs(den), jnp.exp(-m_t))
                m_new = m_t[chunk - 1:chunk, :]
                f_end = fc[chunk - 1:chunk, :]
                w_c.append(jnp.exp(f_end + m_prev - m_new))
                k_sc.append(jnp.exp(f_end - fc + ic - m_new))
                hn = _rms(hh, gn_ref[h:h + 1, :])
                u_ref[rows, hs] = (_sigmoid(op_ref[rows, hs]) * hn).astype(BF16)
                m_sc[h:h + 1, :] = jnp.broadcast_to(m_new, (1, LANES))
            ke = kp * jnp.where(lane < dk, k_sc[0], k_sc[1])
            vp = v_ref[rows, 2 * p * dv:(2 * p + 2) * dv].astype(BF16)
            full = _dot_tn(ke.astype(BF16), vp)
            upd = jnp.where(rowi < dk, full[:, :dv], full[:, dv:])
            c_sc[p] = jnp.where(rowi < dk, w_c[0], w_c[1]) * cp + upd
            n_sc[p:p + 1, :] = (jnp.where(lane < dk, w_c[0], w_c[1]) * npr
                                + jnp.sum(ke, axis=0, keepdims=True))

    @pl.when(c == nblk - 1)
    def _():
        ct_ref[0] = c_sc[...]
        nt_ref[0] = n_sc[...]
        mt_ref[0] = m_sc[...]


def _mlstm_recurrence(z, gates, c0, n0, m0, g_norm, nseq, t, chunk, tb):
    v_w = g_norm.shape[0] * g_norm.shape[1]
    qk_w = v_w // 2
    nblk = t // tb
    kern = functools.partial(_mlstm_kernel, chunk=chunk, tb=tb, nblk=nblk)
    row = lambda b, c: b * nblk + c
    st4 = lambda b, c: (b, 0, 0, 0)
    st3 = lambda b, c: (b, 0, 0)
    return pl.pallas_call(
        kern,
        grid=(nseq, nblk),
        in_specs=[
            pl.BlockSpec((tb, qk_w), lambda b, c: (row(b, c), 0)),
            pl.BlockSpec((tb, qk_w), lambda b, c: (row(b, c), 1)),
            pl.BlockSpec((tb, v_w), lambda b, c: (row(b, c), 1)),
            pl.BlockSpec((tb, v_w), lambda b, c: (row(b, c), 2)),
            pl.BlockSpec((tb, LANES), lambda b, c: (row(b, c), 0)),
            pl.BlockSpec((1,) + c0.shape[1:], st4),
            pl.BlockSpec((1,) + n0.shape[1:], st3),
            pl.BlockSpec((1,) + m0.shape[1:], st3),
            pl.BlockSpec(g_norm.shape, lambda b, c: (0, 0)),
        ],
        out_specs=[
            pl.BlockSpec((tb, v_w), lambda b, c: (row(b, c), 0)),
            pl.BlockSpec((1,) + c0.shape[1:], st4),
            pl.BlockSpec((1,) + n0.shape[1:], st3),
            pl.BlockSpec((1,) + m0.shape[1:], st3),
        ],
        out_shape=[
            jax.ShapeDtypeStruct((nseq * t, v_w), BF16),
            jax.ShapeDtypeStruct(c0.shape, F32),
            jax.ShapeDtypeStruct(n0.shape, F32),
            jax.ShapeDtypeStruct(m0.shape, F32),
        ],
        scratch_shapes=[
            pltpu.VMEM(c0.shape[1:], F32),
            pltpu.VMEM(n0.shape[1:], F32),
            pltpu.VMEM(m0.shape[1:], F32),
        ],
        compiler_params=_params(("arbitrary", "arbitrary")),
        name="mlstm_recurrence_t%d" % t,
    )(z, z, z, z, gates, c0, n0, m0, g_norm)


def _out_proj_kernel(up_ref, us_ref, xp_ref, xs_ref, w_ref, g_ref, xo_ref, hn_ref, *, n_first):
    i = pl.program_id(0)
    u = _pick_rows(i, n_first, up_ref, us_ref)
    x = _pick_rows(i, n_first, xp_ref, xs_ref) + _dot(u, w_ref[...])
    xo_ref[...] = x
    hn_ref[...] = _rms(x, g_ref[...]).astype(hn_ref.dtype)


def _out_proj(u_p, u_s, x_p, x_s, w, g):
    d = x_p.shape[1]
    n = x_p.shape[0] + x_s.shape[0]
    n_first = x_p.shape[0] // TM
    return pl.pallas_call(
        functools.partial(_out_proj_kernel, n_first=n_first),
        grid=(n // TM,),
        in_specs=_split_specs(n_first, u_p.shape[1]) + _split_specs(n_first, d) + [
            pl.BlockSpec(w.shape, lambda i: (0, 0)),
            pl.BlockSpec((1, d), lambda i: (0, 0)),
        ],
        out_specs=[pl.BlockSpec((TM, d), lambda i: (i, 0)), pl.BlockSpec((TM, d), lambda i: (i, 0))],
        out_shape=[jax.ShapeDtypeStruct((n, d), F32), jax.ShapeDtypeStruct((n, d), BF16)],
        compiler_params=_params(("arbitrary",)),
        name="out_proj",
    )(u_p, u_s, x_p, x_s, w, g)


def _out_proj_router_kernel(up_ref, us_ref, x_ref, w_ref, g_ref, wr_ref, br_ref,
                            xo_ref, hn_ref, gate_ref, idx_ref, cnt_ref, cnt_sc, *, n_first):
    i = pl.program_id(0)
    u = _pick_rows(i, n_first, up_ref, us_ref)

    @pl.when(i == 0)
    def _():
        cnt_sc[...] = jnp.zeros_like(cnt_sc)

    x = x_ref[...] + _dot(u, w_ref[...])
    xo_ref[...] = x
    hn = _rms(x, g_ref[...])
    hn_ref[...] = hn
    logits = jnp.dot(hn, wr_ref[...], preferred_element_type=F32,
                     precision=lax.Precision.HIGHEST) + br_ref[...]
    tm = logits.shape[0]
    lane = lax.broadcasted_iota(I32, logits.shape, 1)
    valid = lane < N_EXPERTS
    logits = jnp.where(valid, logits, -jnp.inf)
    ex = jnp.exp(logits - jnp.max(logits, axis=-1, keepdims=True))
    probs = jnp.where(valid, ex / jnp.sum(ex, axis=-1, keepdims=True), -1.0)
    p1 = jnp.max(probs, axis=-1, keepdims=True)
    i1 = jnp.min(jnp.where(probs == p1, lane, LANES), axis=-1, keepdims=True)
    rest = jnp.where(lane == i1, -1.0, probs)
    p2 = jnp.max(rest, axis=-1, keepdims=True)
    i2 = jnp.min(jnp.where(rest == p2, lane, LANES), axis=-1, keepdims=True)
    tot = p1 + p2
    onehot = ((lane == i1) | (lane == i2)).astype(BF16)
    rr = lax.broadcasted_iota(I32, (tm, tm), 0)
    cc = lax.broadcasted_iota(I32, (tm, tm), 1)
    strict = (rr > cc).astype(BF16)
    before = _dot(strict, onehot) + cnt_sc[...]
    r1 = jnp.sum(jnp.where(lane == i1, before, 0.0), axis=-1, keepdims=True)
    r2 = jnp.sum(jnp.where(lane == i2, before, 0.0), axis=-1, keepdims=True)
    cnt_sc[...] = cnt_sc[...] + jnp.sum(onehot.astype(F32), axis=0, keepdims=True)
    gate_ref[...] = jnp.where(lane == 0, p1 / tot, jnp.where(lane == 1, p2 / tot, 0.0))
    idx_ref[...] = jnp.where(lane == 0, i1,
                             jnp.where(lane == 1, i2,
                                       jnp.where(lane == 2, r1.astype(I32),
                                                 jnp.where(lane == 3, r2.astype(I32), 0))))
    cnt_ref[...] = cnt_sc[...]


def _out_proj_router(u_p, u_s, x, w, g, w_router, b_router):
    n, d = x.shape
    n_first = u_p.shape[0] // TM
    row = lambda i: (i, 0)
    fix = lambda i: (0, 0)
    return pl.pallas_call(
        functools.partial(_out_proj_router_kernel, n_first=n_first),
        grid=(n // TM,),
        in_specs=_split_specs(n_first, u_p.shape[1]) + [
            pl.BlockSpec((TM, d), row),
            pl.BlockSpec(w.shape, fix),
            pl.BlockSpec((1, d), fix),
            pl.BlockSpec(w_router.shape, fix),
            pl.BlockSpec((1, LANES), fix),
        ],
        out_specs=[
            pl.BlockSpec((TM, d), row),
            pl.BlockSpec((TM, d), row),
            pl.BlockSpec((TM, LANES), row),
            pl.BlockSpec((TM, LANES), row),
            pl.BlockSpec((1, LANES), fix),
        ],
        out_shape=[
            jax.ShapeDtypeStruct((n, d), F32),
            jax.ShapeDtypeStruct((n, d), F32),
            jax.ShapeDtypeStruct((n, LANES), F32),
            jax.ShapeDtypeStruct((n, LANES), I32),
            jax.ShapeDtypeStruct((1, LANES), F32),
        ],
        scratch_shapes=[pltpu.VMEM((1, LANES), F32)],
        compiler_params=_params(("arbitrary",)),
        name="out_proj_router",
    )(u_p, u_s, x, w, g, w_router, b_router)


def _swiglu_kernel(hn_ref, x_ref, wg_ref, wu_ref, wd_ref, g_ref, xo_ref, ho_ref, acc_sc, *, nf):
    f = pl.program_id(1)
    hn = hn_ref[...]
    gate = _dot(hn, wg_ref[...])
    up = _dot(hn, wu_ref[...])
    act = (gate * _sigmoid(gate) * up).astype(BF16)
    part = _dot(act, wd_ref[...])

    @pl.when(f == 0)
    def _():
        acc_sc[...] = x_ref[...] + part

    @pl.when(f > 0)
    def _():
        acc_sc[...] = acc_sc[...] + part

    @pl.when(f == nf - 1)
    def _():
        x = acc_sc[...]
        xo_ref[...] = x
        ho_ref[...] = _rms(x, g_ref[...]).astype(BF16)


def _swiglu(hn, x, w_gu, w_down, g, nf):
    n, d = x.shape
    dff = w_down.shape[0]
    tf = dff // nf
    kern = functools.partial(_swiglu_kernel, nf=nf)
    row = lambda i, f: (i, 0)
    return pl.pallas_call(
        kern,
        grid=(n // TM, nf),
        in_specs=[
            pl.BlockSpec((TM, d), row),
            pl.BlockSpec((TM, d), row),
            pl.BlockSpec((d, tf), lambda i, f: (0, f)),
            pl.BlockSpec((d, tf), lambda i, f: (0, nf + f)),
            pl.BlockSpec((tf, d), lambda i, f: (f, 0)),
            pl.BlockSpec((1, d), lambda i, f: (0, 0)),
        ],
        out_specs=[pl.BlockSpec((TM, d), row), pl.BlockSpec((TM, d), row)],
        out_shape=[jax.ShapeDtypeStruct((n, d), F32), jax.ShapeDtypeStruct((n, d), BF16)],
        scratch_shapes=[pltpu.VMEM((TM, d), F32)],
        compiler_params=_params(("arbitrary", "arbitrary")),
        name="dense_swiglu",
    )(hn, x, w_gu, w_gu, w_down, g)


DMA_UNROLL = 8


def _row_copy(src_ref, src_row, dst_ref, dst_row, sem):
    return pltpu.make_async_copy(src_ref.at[pl.ds(src_row, 1), :], dst_ref.at[pl.ds(dst_row, 1), :], sem)


def _dispatch_kernel(dest_ref, hn_ref, xs_in_ref, xs_ref, sem):
    del xs_in_ref
    i = pl.program_id(0)
    tm = hn_ref.shape[0]

    def issue(blk, carry):
        for j in range(DMA_UNROLL):
            r = blk * DMA_UNROLL + j
            t = i * tm + r
            _row_copy(hn_ref, r, xs_ref, dest_ref[2 * t], sem).start(priority=0)
            _row_copy(hn_ref, r, xs_ref, dest_ref[2 * t + 1], sem).start(priority=1)
        return carry

    lax.fori_loop(0, tm // DMA_UNROLL, issue, 0)
    for _ in range(2):
        pltpu.make_async_copy(hn_ref, xs_ref.at[pl.ds(0, tm), :], sem).wait()


def _dispatch(dest_flat, hn, xs_zero):
    n, d = hn.shape
    return pl.pallas_call(
        _dispatch_kernel,
        grid_spec=pltpu.PrefetchScalarGridSpec(
            num_scalar_prefetch=1,
            grid=(n // TM,),
            in_specs=[
                pl.BlockSpec((TM, d), lambda i, dest: (i, 0)),
                pl.BlockSpec(memory_space=pl.ANY),
            ],
            out_specs=pl.BlockSpec(memory_space=pl.ANY),
            scratch_shapes=[pltpu.SemaphoreType.DMA(())],
        ),
        out_shape=jax.ShapeDtypeStruct(xs_zero.shape, xs_zero.dtype),
        input_output_aliases={2: 0},
        compiler_params=_params(("arbitrary",)),
        name="moe_dispatch",
    )(dest_flat, hn, xs_zero)


def _expert_kernel(te_ref, tv_ref, xs_ref, wg_ref, wu_ref, wd_ref, ys_ref, *, nchunk):
    i = pl.program_id(0)

    @pl.when(tv_ref[i] > 0)
    def _():
        x = xs_ref[...].astype(BF16)
        width = wg_ref.shape[2] // nchunk
        acc = None
        for c in range(nchunk):
            cs = slice(c * width, (c + 1) * width)
            gate = _dot(x, wg_ref[0, :, cs])
            up = _dot(x, wu_ref[0, :, cs])
            act = (gate * _sigmoid(gate) * up).astype(BF16)
            part = _dot(act, wd_ref[0, cs, :])
            acc = part if acc is None else acc + part
        ys_ref[...] = acc

    @pl.when(tv_ref[i] == 0)
    def _():
        ys_ref[...] = jnp.zeros_like(ys_ref)


def _experts(tile_expert, tile_valid, xs, w_gu, w_down):
    r, d = xs.shape
    dff = w_down.shape[1]
    kern = functools.partial(_expert_kernel, nchunk=7)
    return pl.pallas_call(
        kern,
        grid_spec=pltpu.PrefetchScalarGridSpec(
            num_scalar_prefetch=2,
            grid=(r // TM_EXPERT,),
            in_specs=[
                pl.BlockSpec((TM_EXPERT, d), lambda i, te, tv: (i, 0)),
                pl.BlockSpec((1, d, dff), lambda i, te, tv: (te[i], 0, 0)),
                pl.BlockSpec((1, d, dff), lambda i, te, tv: (te[i], 0, 1)),
                pl.BlockSpec((1, dff, d), lambda i, te, tv: (te[i], 0, 0)),
            ],
            out_specs=pl.BlockSpec((TM_EXPERT, d), lambda i, te, tv: (i, 0)),
        ),
        out_shape=jax.ShapeDtypeStruct((r, d), F32),
        compiler_params=_params(("arbitrary",)),
        name="moe_experts",
    )(tile_expert, tile_valid, xs, w_gu, w_gu, w_down)


def _combine_kernel(dest_ref, ys_ref, x_ref, gate_ref, g_ref, yp_ref, ysm_ref, buf, sem, *, n_first):
    i = pl.program_id(0)
    tm = x_ref.shape[0]

    def issue(blk, carry):
        for j in range(DMA_UNROLL):
            r = blk * DMA_UNROLL + j
            t = i * tm + r
            _row_copy(ys_ref, dest_ref[2 * t], buf.at[0], r, sem).start(priority=0)
            _row_copy(ys_ref, dest_ref[2 * t + 1], buf.at[1], r, sem).start(priority=1)
        return carry

    lax.fori_loop(0, tm // DMA_UNROLL, issue, 0)
    for slot in range(2):
        pltpu.make_async_copy(ys_ref.at[pl.ds(0, tm), :], buf.at[slot], sem).wait()
    gate = gate_ref[...]
    y = x_ref[...] + gate[:, 0:1] * buf[0] + gate[:, 1:2] * buf[1]
    y = _rms(y, g_ref[...])

    @pl.when(i < n_first)
    def _():
        yp_ref[...] = y

    @pl.when(i >= n_first)
    def _():
        ysm_ref[...] = y


def _combine(dest_flat, ys, x, gates, g, n_p):
    n, d = x.shape
    n_first = n_p // TM
    p_spec, s_spec = _split_specs(n_first, d)
    return pl.pallas_call(
        functools.partial(_combine_kernel, n_first=n_first),
        grid_spec=pltpu.PrefetchScalarGridSpec(
            num_scalar_prefetch=1,
            grid=(n // TM,),
            in_specs=[
                pl.BlockSpec(memory_space=pl.ANY),
                pl.BlockSpec((TM, d), lambda i, dest: (i, 0)),
                pl.BlockSpec((TM, LANES), lambda i, dest: (i, 0)),
                pl.BlockSpec((1, d), lambda i, dest: (0, 0)),
            ],
            out_specs=[p_spec, s_spec],
            scratch_shapes=[pltpu.VMEM((2, TM, d), F32), pltpu.SemaphoreType.DMA(())],
        ),
        out_shape=[jax.ShapeDtypeStruct((n_p, d), F32), jax.ShapeDtypeStruct((n - n_p, d), F32)],
        compiler_params=_params(("arbitrary",)),
        name="moe_combine",
    )(dest_flat, ys, x, gates, g)


def _routing_tables(idx, counts, n_rows):
    e = idx[:, 0:2]
    rank = idx[:, 2:4]
    cnt = counts[0, :N_EXPERTS].astype(I32)
    padded = ((cnt + TM_EXPERT - 1) // TM_EXPERT) * TM_EXPERT
    ends = jnp.cumsum(padded)
    starts = ends - padded
    dest = (starts[e] + rank).reshape(-1)
    tile_start = jnp.arange(n_rows // TM_EXPERT, dtype=I32) * TM_EXPERT
    tile_valid = (tile_start < ends[-1]).astype(I32)
    tile_expert = jnp.sum((tile_start[:, None] >= ends[None, :]).astype(I32), axis=1)
    last_expert = jnp.sum((ends[-1] - 1 >= ends).astype(I32))
    tile_expert = jnp.minimum(tile_expert, last_expert)
    return dest, tile_expert, tile_valid


def _pad_time(a, nseq, t, fill=None):
    w = a.shape[1]
    a = a.reshape(nseq, t, w)
    a = jnp.pad(a, ((0, 0), (0, SAMPLE_PAD_T - t), (0, 0)))
    if fill is not None:
        pad_row = jnp.arange(SAMPLE_PAD_T)[None, :, None] >= t
        lane = jnp.arange(w)[None, None, :]
        a = jnp.where(pad_row & (lane < MLSTM_HEADS), fill, a)
    return a.reshape(nseq * SAMPLE_PAD_T, w)


def _unpad_time(a, nseq, t):
    w = a.shape[1]
    return a.reshape(nseq, SAMPLE_PAD_T, w)[:, :t].reshape(nseq * t, w)


def kernel(x_prompt, x_sample, state_gla_S, state_mlstm_C, state_mlstm_n, state_mlstm_m,
           norm_mix, norm_ffn, norm_final,
           gla_w_in, gla_w_a2, gla_b_a, gla_g_norm, gla_w_out,
           mlstm_w_in, mlstm_b_gate, mlstm_g_norm, mlstm_w_out,
           ffn_w_gu, ffn_w_down,
           moe_w_router, moe_b_router, moe_w_gu, moe_w_down):
    bp, tp, d = x_prompt.shape
    bs, ts, _ = x_sample.shape
    n_p = bp * tp
    n_s = bs * ts
    n = n_p + n_s
    assert n_p % TM == 0 and n_s % TM == 0 and tp % (2 * CHUNK) == 0 and ts <= SAMPLE_PAD_T
    assert norm_mix.shape[0] == 2, "one GLA layer followed by one mLSTM layer"
    qk_w = d // 2
    main_w = 2 * qk_w + 2 * d

    x_p = x_prompt.reshape(n_p, d)
    x_s = x_sample.reshape(n_s, d)

    w_in = gla_w_in[0]
    w_lr = jnp.pad(w_in[:, main_w:], ((0, 0), (0, LANES - GLA_RANK))).astype(BF16)
    w_a2 = jnp.pad(gla_w_a2[0], ((0, LANES - GLA_RANK), (0, 0))).astype(BF16)
    z, log_a = _gla_in_proj(x_p, x_s, norm_mix[0][None], w_in[:, :main_w].astype(BF16), w_lr, w_a2,
                            gla_b_a[0][None])
    s0_p = jnp.zeros((bp,) + state_gla_S.shape[2:], F32)
    u_p, s_p = _gla_recurrence(z, log_a, s0_p, gla_g_norm[0], bp, tp, CHUNK, 2 * CHUNK)
    u_s, s_s = _gla_recurrence(_pad_time(z[n_p:], bs, ts), _pad_time(log_a[n_p:], bs, ts),
                               state_gla_S[0], gla_g_norm[0], bs, SAMPLE_PAD_T, SAMPLE_PAD_T,
                               SAMPLE_PAD_T)
    x, hn = _out_proj(u_p, _unpad_time(u_s, bs, ts), x_p, x_s, gla_w_out[0].astype(BF16),
                      norm_ffn[0][None])
    x, hn = _swiglu(hn, x, ffn_w_gu[0].astype(BF16), ffn_w_down[0].astype(BF16),
                    norm_mix[1][None], 2)

    w_in = mlstm_w_in[0]
    w_gt = jnp.pad(w_in[:, main_w:], ((0, 0), (0, LANES - 2 * MLSTM_HEADS))).astype(BF16)
    b_gt = jnp.pad(mlstm_b_gate[0], (0, LANES - 2 * MLSTM_HEADS))[None]
    z, gates = _mlstm_in_proj(hn, w_in[:, :main_w].astype(BF16), w_gt, b_gt)
    pairs = MLSTM_HEADS // 2
    dk2 = 2 * state_mlstm_C.shape[3]
    dvm = state_mlstm_C.shape[4]
    c0_p = jnp.zeros((bp, pairs, dk2, dvm), F32)
    n0_p = jnp.zeros((bp, pairs, dk2), F32)
    m0_p = jnp.zeros((bp, MLSTM_HEADS, LANES), F32)
    u_p, c_p, nn_p, m_p = _mlstm_recurrence(z, gates, c0_p, n0_p, m0_p,
                                            mlstm_g_norm[0], bp, tp, CHUNK, 2 * CHUNK)
    c0_s = state_mlstm_C[0].reshape(bs, pairs, dk2, dvm)
    n0_s = state_mlstm_n[0].reshape(bs, pairs, dk2)
    m0_s = jnp.broadcast_to(state_mlstm_m[0][:, :, None], (bs, MLSTM_HEADS, LANES))
    u_s, c_s, nn_s, m_s = _mlstm_recurrence(
        _pad_time(z[n_p:], bs, ts), _pad_time(gates[n_p:], bs, ts, fill=NEG_BIG),
        c0_s, n0_s, m0_s, mlstm_g_norm[0], bs, SAMPLE_PAD_T, SAMPLE_PAD_T, SAMPLE_PAD_T)

    w_r = jnp.pad(moe_w_router[0], ((0, 0), (0, LANES - N_EXPERTS)))
    b_r = jnp.pad(moe_b_router[0], (0, LANES - N_EXPERTS))[None]
    x, hn, route_g, route_i, counts = _out_proj_router(
        u_p, _unpad_time(u_s, bs, ts), x, mlstm_w_out[0].astype(BF16), norm_ffn[1][None], w_r, b_r)
    n_rows = -(-(2 * n + N_EXPERTS * (TM_EXPERT - 1)) // TM_EXPERT) * TM_EXPERT
    dest, tile_expert, tile_valid = _routing_tables(route_i, counts, n_rows)
    xs = _dispatch(dest, hn, jnp.zeros((n_rows, d), F32))
    ys = _experts(tile_expert, tile_valid, xs, moe_w_gu[0].astype(BF16), moe_w_down[0].astype(BF16))
    y_p, y_s = _combine(dest, ys, x, route_g, norm_final[None], n_p)

    y_prompt = y_p.reshape(bp, tp, d)
    y_sample = y_s.reshape(bs, ts, d)
    hd = state_mlstm_C.shape[2:]
    return (y_prompt, y_sample,
            s_p[None], c_p.reshape((1, bp) + hd), nn_p.reshape(1, bp, hd[0], hd[1]), m_p[None, :, :, 0],
            s_s[None], c_s.reshape((1, bs) + hd), nn_s.reshape(1, bs, hd[0], hd[1]), m_s[None, :, :, 0])


def _mlstm_in_proj(hn, w_main, w_gate, b_gate):
    n, d = hn.shape
    wz = w_main.shape[1]

    def kern(h_ref, w_ref, wg_ref, b_ref, z_ref, gate_ref):
        h = h_ref[...]
        width = wz // IN_PROJ_COL_CHUNKS
        for c in range(IN_PROJ_COL_CHUNKS):
            cs = slice(c * width, (c + 1) * width)
            z_ref[:, cs] = _dot(h, w_ref[:, cs])
        gp = _dot(h, wg_ref[...]) + b_ref[...]
        gc = GATE_CAP * jnp.tanh(gp * (1.0 / GATE_CAP))
        lane = lax.broadcasted_iota(I32, gc.shape, 1)
        out = jnp.where(lane < MLSTM_HEADS, gc, _log_sigmoid(gc))
        gate_ref[...] = jnp.where(lane < 2 * MLSTM_HEADS, out, 0.0)

    return pl.pallas_call(
        kern,
        grid=(n // TM,),
        in_specs=[
            pl.BlockSpec((TM, d), lambda i: (i, 0)),
            pl.BlockSpec((d, wz), lambda i: (0, 0)),
            pl.BlockSpec((d, LANES), lambda i: (0, 0)),
            pl.BlockSpec((1, LANES), lambda i: (0, 0)),
        ],
        out_specs=[pl.BlockSpec((TM, wz), lambda i: (i, 0)), pl.BlockSpec((TM, LANES), lambda i: (i, 0))],
        out_shape=[jax.ShapeDtypeStruct((n, wz), F32), jax.ShapeDtypeStruct((n, LANES), F32)],
        compiler_params=_params(("arbitrary",)),
        name="in_proj_mlstm",
    )(hn, w_main, w_gate, b_gate)
```

```python
import functools

import jax
import jax.numpy as jnp
import numpy as np
from jax import lax
from jax.experimental import pallas as pl
from jax.experimental.pallas import tpu as pltpu

F32 = jnp.float32
BF16 = jnp.bfloat16
I32 = jnp.int32

EPS = 1e-6
GLA_HEADS = 4
GLA_RANK = 16
GLA_TAU = 16.0
MLSTM_HEADS = 8
GATE_CAP = 15.0
CHUNK = 64
N_EXPERTS = 8
NEG_BIG = -1e30

LANES = 128
TM = 512
TM_EXPERT = 256
SAMPLE_PAD_T = 8
SAMPLE_SEQS_PER_STEP = 8
VMEM_LIMIT = 56 * 1024 * 1024


def _dot(a, b):
    return jnp.dot(a, b, preferred_element_type=F32)


def _dot_nt(a, b):
    return lax.dot_general(a, b, (((1,), (1,)), ((), ())), preferred_element_type=F32)


def _dot_tn(a, b):
    return lax.dot_general(a, b, (((0,), (0,)), ((), ())), preferred_element_type=F32)


def _sigmoid(x):
    return 1.0 / (1.0 + jnp.exp(-x))


def _log_sigmoid(x):
    return jnp.minimum(x, 0.0) - jnp.log1p(jnp.exp(-jnp.abs(x)))


def _rms(x, g):
    return x * lax.rsqrt(jnp.mean(x * x, axis=-1, keepdims=True) + EPS) * g


def _cumsum_rows(x, n):
    ridx = lax.broadcasted_iota(I32, x.shape, 0)
    s = 1
    while s < n:
        x = x + jnp.where(ridx >= s, pltpu.roll(x, s, 0), 0.0)
        s *= 2
    return x


def _params(sem):
    return pltpu.CompilerParams(dimension_semantics=sem, vmem_limit_bytes=VMEM_LIMIT)


IN_PROJ_COL_CHUNKS = 4


def _pick_rows(i, n_first, first_ref, second_ref):
    return jnp.where(i < n_first, first_ref[...], second_ref[...])


def _split_specs(n_first, width):
    return [pl.BlockSpec((TM, width), lambda i, *_: (jnp.minimum(i, n_first - 1), 0)),
            pl.BlockSpec((TM, width), lambda i, *_: (jnp.maximum(i - n_first, 0), 0))]


def _gla_in_proj_kernel(xp_ref, xs_ref, g_ref, w_ref, wg_ref, w2_ref, b_ref, z_ref, gate_ref, *,
                        n_first):
    x = _pick_rows(pl.program_id(0), n_first, xp_ref, xs_ref)
    hn = _rms(x, g_ref[...]).astype(BF16)
    width = w_ref.shape[1] // IN_PROJ_COL_CHUNKS
    for c in range(IN_PROJ_COL_CHUNKS):
        cs = slice(c * width, (c + 1) * width)
        z_ref[:, cs] = _dot(hn, w_ref[:, cs])
    a = _dot(hn, wg_ref[...])
    la = _dot(a.astype(BF16), w2_ref[...]) + b_ref[...]
    gate_ref[...] = _log_sigmoid(la) * (1.0 / GLA_TAU)


def _gla_in_proj(x_p, x_s, g, w_main, w_gate, w2, b):
    d = x_p.shape[1]
    n = x_p.shape[0] + x_s.shape[0]
    n_first = x_p.shape[0] // TM
    wz = w_main.shape[1]
    gw = b.shape[1]
    return pl.pallas_call(
        functools.partial(_gla_in_proj_kernel, n_first=n_first),
        grid=(n // TM,),
        in_specs=_split_specs(n_first, d) + [
            pl.BlockSpec((1, d), lambda i: (0, 0)),
            pl.BlockSpec((d, wz), lambda i: (0, 0)),
            pl.BlockSpec((d, LANES), lambda i: (0, 0)),
            pl.BlockSpec(w2.shape, lambda i: (0, 0)),
            pl.BlockSpec((1, gw), lambda i: (0, 0)),
        ],
        out_specs=[
            pl.BlockSpec((TM, wz), lambda i: (i, 0)),
            pl.BlockSpec((TM, gw), lambda i: (i, 0)),
        ],
        out_shape=[jax.ShapeDtypeStruct((n, wz), F32), jax.ShapeDtypeStruct((n, gw), F32)],
        compiler_params=_params(("arbitrary",)),
        name="in_proj_gla",
    )(x_p, x_s, g, w_main, w_gate, w2, b)


def _gla_kernel(q_ref, k_ref, v_ref, r_ref, la_ref, s0_ref, gn_ref, u_ref, st_ref, st_sc,
                *, chunk, tb, nblk, nsq):
    heads = GLA_HEADS
    dk = q_ref.shape[1] // heads
    dv = v_ref.shape[1] // heads
    c = pl.program_id(1)

    @pl.when(c == 0)
    def _():
        for sq in range(nsq):
            for h in range(heads):
                st_sc[sq * heads + h] = s0_ref[sq, h].T

    ri = lax.broadcasted_iota(I32, (chunk, chunk), 0)
    ci = lax.broadcasted_iota(I32, (chunk, chunk), 1)
    causal = ri >= ci
    kscale = dk ** -0.5
    for sq in range(nsq):
        for j in range(tb // chunk):
            rows = slice(sq * tb + j * chunk, sq * tb + (j + 1) * chunk)
            b_all = _cumsum_rows(la_ref[rows, :], chunk)
            for h in range(heads):
                ks = slice(h * dk, (h + 1) * dk)
                vs = slice(h * dv, (h + 1) * dv)
                b = b_all[:, ks]
                b_end = b[chunk - 1:chunk, :]
                q = q_ref[rows, ks]
                k = k_ref[rows, ks] * kscale
                qg = (q * jnp.exp(b)).astype(BF16)
                kg = (k * jnp.exp(-b)).astype(BF16)
                ke = (k * jnp.exp(b_end - b)).astype(BF16)
                v = v_ref[rows, vs].astype(BF16)
                a = jnp.where(causal, _dot_nt(qg, kg), 0.0).astype(BF16)
                st = st_sc[sq * heads + h]
                o = _dot_nt(qg, st.astype(BF16)) + _dot(a, v)
                st_sc[sq * heads + h] = st * jnp.exp(b_end) + _dot_tn(v, ke)
                on = _rms(o, gn_ref[h:h + 1, :])
                r = r_ref[rows, vs]
                u_ref[rows, vs] = (r * _sigmoid(r) * on).astype(BF16)

    @pl.when(c == nblk - 1)
    def _():
        for sq in range(nsq):
            for h in range(heads):
                st_ref[sq, h] = st_sc[sq * heads + h].T


def _gla_recurrence(z, log_a, s0, g_norm, nseq, t, chunk, tb, nsq=1):
    qk_w = log_a.shape[1]
    v_w = 2 * qk_w
    nblk = t // tb
    assert nsq == 1 or nblk == 1
    heads, dk, dv = s0.shape[1:]
    kern = functools.partial(_gla_kernel, chunk=chunk, tb=tb, nblk=nblk, nsq=nsq)
    rb = nsq * tb
    row = lambda b, c: b * nblk + c
    return pl.pallas_call(
        kern,
        grid=(nseq // nsq, nblk),
        in_specs=[
            pl.BlockSpec((rb, qk_w), lambda b, c: (row(b, c), 0)),
            pl.BlockSpec((rb, qk_w), lambda b, c: (row(b, c), 1)),
            pl.BlockSpec((rb, v_w), lambda b, c: (row(b, c), 1)),
            pl.BlockSpec((rb, v_w), lambda b, c: (row(b, c), 2)),
            pl.BlockSpec((rb, qk_w), lambda b, c: (row(b, c), 0)),
            pl.BlockSpec((nsq, heads, dk, dv), lambda b, c: (b, 0, 0, 0)),
            pl.BlockSpec((heads, dv), lambda b, c: (0, 0)),
        ],
        out_specs=[
            pl.BlockSpec((rb, v_w), lambda b, c: (row(b, c), 0)),
            pl.BlockSpec((nsq, heads, dk, dv), lambda b, c: (b, 0, 0, 0)),
        ],
        out_shape=[
            jax.ShapeDtypeStruct((nseq * t, v_w), BF16),
            jax.ShapeDtypeStruct(s0.shape, F32),
        ],
        scratch_shapes=[pltpu.VMEM((nsq * heads, dv, dk), F32)],
        compiler_params=_params(("arbitrary", "arbitrary")),
        name="gla_recurrence_t%d" % t,
    )(z, z, z, z, log_a, s0, g_norm)


def _mlstm_short_kernel(q_ref, k_ref, v_ref, op_ref, gt_ref, c0_ref, n0_ref, m0_ref, gn_ref,
                        u_ref, ct_ref, nt_ref, mt_ref, *, t, nsq):
    heads = MLSTM_HEADS
    pairs = heads // 2
    dv = v_ref.shape[1] // heads
    dk = q_ref.shape[1] // heads
    ri = lax.broadcasted_iota(I32, (t, t), 0)
    ci = lax.broadcasted_iota(I32, (t, t), 1)
    causal = ri >= ci
    lane = lax.broadcasted_iota(I32, (1, LANES), 1)
    rowi = lax.broadcasted_iota(I32, (LANES, 1), 0)
    kscale = dk ** -0.5
    for sq in range(nsq):
        rows = slice(sq * t, (sq + 1) * t)
        gts = gt_ref[rows, :]
        fa = _cumsum_rows(gts, t)
        bx = gts - pltpu.roll(fa, LANES - heads, 1)
        bx = jnp.concatenate([bx, jnp.zeros((LANES - t, LANES), F32)], axis=0)
        bxt = bx.T
        for p in range(pairs):
            ps = slice(p * LANES, (p + 1) * LANES)
            qp = q_ref[rows, ps]
            kp = k_ref[rows, ps] * kscale
            kpb = kp.astype(BF16)
            cp = c0_ref[sq, p]
            npr = n0_ref[sq, p:p + 1, :]
            w_c = []
            k_sc = []
            for jj in range(2):
                h = 2 * p + jj
                hs = slice(h * dv, (h + 1) * dv)
                qh = jnp.where((lane >= jj * dk) & (lane < (jj + 1) * dk), qp, 0.0)
                qhb = qh.astype(BF16)
                fc = fa[:, heads + h:heads + h + 1]
                ic = gts[:, h:h + 1]
                brow = bxt[h:h + 1, 0:t]
                m_prev = m0_ref[sq, h:h + 1, 0:1]
                d = jnp.where(causal, fc + brow, -jnp.inf)
                inter = fc + m_prev
                m_t = jnp.maximum(inter, jnp.max(d, axis=-1, keepdims=True))
                w_i = jnp.exp(inter - m_t)
                s = _dot_nt(qhb, kpb) * jnp.exp(d - m_t)
                v = v_ref[rows, hs].astype(BF16)
                num = w_i * _dot(qhb, cp.astype(BF16)) + _dot(s.astype(BF16), v)
                qn = jnp.sum(qh * npr, axis=-1, keepdims=True)
                den = w_i * qn + jnp.sum(s, axis=-1, keepdims=True)
                hh = num / jnp.maximum(jnp.abs(den), jnp.exp(-m_t))
                m_new = m_t[t - 1:t, :]
                f_end = fc[t - 1:t, :]
                w_c.append(jnp.exp(f_end + m_prev - m_new))
                k_sc.append(jnp.exp(f_end - fc + ic - m_new))
                hn = _rms(hh, gn_ref[h:h + 1, :])
                u_ref[rows, hs] = (_sigmoid(op_ref[rows, hs]) * hn).astype(BF16)
                mt_ref[sq, h:h + 1, :] = jnp.broadcast_to(m_new, (1, LANES))
            ke = kp * jnp.where(lane < dk, k_sc[0], k_sc[1])
            vp = v_ref[rows, 2 * p * dv:(2 * p + 2) * dv].astype(BF16)
            full = _dot_tn(ke.astype(BF16), vp)
            upd = jnp.where(rowi < dk, full[:, :dv], full[:, dv:])
            ct_ref[sq, p] = jnp.where(rowi < dk, w_c[0], w_c[1]) * cp + upd
            nt_ref[sq, p:p + 1, :] = (jnp.where(lane < dk, w_c[0], w_c[1]) * npr
                                      + jnp.sum(ke, axis=0, keepdims=True))


def _mlstm_recurrence_short(z, gates, c0, n0, m0, g_norm, nseq, t, nsq):
    v_w = g_norm.shape[0] * g_norm.shape[1]
    qk_w = v_w // 2
    kern = functools.partial(_mlstm_short_kernel, t=t, nsq=nsq)
    rb = nsq * t
    st4 = lambda b: (b, 0, 0, 0)
    st3 = lambda b: (b, 0, 0)
    return pl.pallas_call(
        kern,
        grid=(nseq // nsq,),
        in_specs=[
            pl.BlockSpec((rb, qk_w), lambda b: (b, 0)),
            pl.BlockSpec((rb, qk_w), lambda b: (b, 1)),
            pl.BlockSpec((rb, v_w), lambda b: (b, 1)),
            pl.BlockSpec((rb, v_w), lambda b: (b, 2)),
            pl.BlockSpec((rb, LANES), lambda b: (b, 0)),
            pl.BlockSpec((nsq,) + c0.shape[1:], st4),
            pl.BlockSpec((nsq,) + n0.shape[1:], st3),
            pl.BlockSpec((nsq,) + m0.shape[1:], st3),
            pl.BlockSpec(g_norm.shape, lambda b: (0, 0)),
        ],
        out_specs=[
            pl.BlockSpec((rb, v_w), lambda b: (b, 0)),
            pl.BlockSpec((nsq,) + c0.shape[1:], st4),
            pl.BlockSpec((nsq,) + n0.shape[1:], st3),
            pl.BlockSpec((nsq,) + m0.shape[1:], st3),
        ],
        out_shape=[
            jax.ShapeDtypeStruct((nseq * t, v_w), BF16),
            jax.ShapeDtypeStruct(c0.shape, F32),
            jax.ShapeDtypeStruct(n0.shape, F32),
            jax.ShapeDtypeStruct(m0.shape, F32),
        ],
        compiler_params=_params(("arbitrary",)),
        name="mlstm_recurrence_t%d" % t,
    )(z, z, z, z, gates, c0, n0, m0, g_norm)


N_SPLIT = 3
EXPAND_WIDTHS = (64, 64, 128, 128)


def _expand_matrix(heads):
    cols = []
    for qi, width in enumerate(EXPAND_WIDTHS):
        sel = np.zeros((LANES, heads * width), np.float32)
        for term in range(N_SPLIT):
            for h in range(heads):
                sel[(qi * N_SPLIT + term) * heads + h, h * width:(h + 1) * width] = 1.0
        cols.append(sel)
    return jnp.asarray(np.concatenate(cols, axis=1), BF16)


def _chunk_scan(x, rc, n, op, fill):
    s = 1
    while s < n:
        x = op(x, jnp.where(rc >= s, pltpu.roll(x, s, 0), fill))
        s *= 2
    return x


def _pack_terms(values, heads):
    lane = lax.broadcasted_iota(I32, values[0].shape, 1)
    packed = jnp.zeros(values[0].shape, F32)
    slot = 0
    for val in values:
        rest = val
        for term in range(N_SPLIT):
            part = rest.astype(BF16).astype(F32)
            rest = rest - part
            moved = part if slot == 0 else pltpu.roll(part, slot * heads, 1)
            packed = jnp.where((lane >= slot * heads) & (lane < (slot + 1) * heads), moved, packed)
            slot += 1
    return packed.astype(BF16)


def _mlstm_block_kernel(q_ref, k_ref, v_ref, op_ref, gt_ref, e_ref, c0_ref, n0_ref, m0_ref, gn_ref,
                        u_ref, ct_ref, nt_ref, mt_ref, c_sc, nb_sc, m_sc, *, chunk, tb, nblk):
    heads = MLSTM_HEADS
    pairs = heads // 2
    dv = v_ref.shape[1] // heads
    dk = q_ref.shape[1] // heads
    c = pl.program_id(1)

    @pl.when(c == 0)
    def _():
        c_sc[...] = c0_ref[0]
        for p in range(pairs):
            nb_sc[p] = jnp.broadcast_to(n0_ref[0, p:p + 1, :], (LANES, LANES)).T
        m_sc[...] = m0_ref[0]

    nch = tb // chunk
    rc = lax.broadcasted_iota(I32, (tb, LANES), 0) & (chunk - 1)
    gts = gt_ref[...]
    f_cum = pltpu.roll(_chunk_scan(gts, rc, chunk, jnp.add, 0.0), LANES - heads, 1)
    b = gts - f_cum
    cmb = _chunk_scan(b, rc, chunk, jnp.maximum, NEG_BIG)
    m_prev = m_sc[...]
    mp_rows, mn_rows, fe_rows, w_c = [], [], [], []
    for j in range(nch):
        last = (j + 1) * chunk - 1
        f_end = f_cum[last:last + 1, :]
        m_new = f_end + jnp.maximum(m_prev, cmb[last:last + 1, :])
        w_c.append(jnp.exp(f_end + m_prev - m_new))
        mp_rows.append(jnp.broadcast_to(m_prev, (chunk, LANES)))
        mn_rows.append(jnp.broadcast_to(m_new, (chunk, LANES)))
        fe_rows.append(jnp.broadcast_to(f_end, (chunk, LANES)))
        m_prev = m_new
    m_sc[...] = m_prev
    mp = jnp.concatenate(mp_rows, axis=0)
    mn = jnp.concatenate(mn_rows, axis=0)
    fe = jnp.concatenate(fe_rows, axis=0)
    big_m = jnp.maximum(mp, cmb)
    w_i = jnp.exp(mp - big_m)
    em = jnp.exp(-(f_cum + big_m))
    k_sc = jnp.exp(fe - f_cum + gts - mn)
    ex = _dot(_pack_terms([w_i, k_sc, -big_m, em], heads), e_ref[...])
    o1 = heads * EXPAND_WIDTHS[0]
    o2 = o1 + heads * EXPAND_WIDTHS[1]
    o3 = o2 + heads * EXPAND_WIDTHS[2]
    q = q_ref[...]
    k = k_ref[...] * (dk ** -0.5)
    qs = q * ex[:, :o1]
    ke = k * ex[:, o1:o2]
    neg_m = ex[:, o2:o3]
    em_v = ex[:, o3:]
    bt = b.T

    ri = lax.broadcasted_iota(I32, (chunk, chunk), 0)
    ci = lax.broadcasted_iota(I32, (chunk, chunk), 1)
    causal = ri >= ci
    lane1 = lax.broadcasted_iota(I32, (1, LANES), 1)
    rowi = lax.broadcasted_iota(I32, (LANES, LANES), 0)
    ones_v = jnp.ones((chunk, dv), BF16)
    for j in range(nch):
        rows = slice(j * chunk, (j + 1) * chunk)
        for p in range(pairs):
            ps = slice(p * LANES, (p + 1) * LANES)
            qp = q[rows, ps]
            qsp = qs[rows, ps]
            kpb = k[rows, ps].astype(BF16)
            cp = c_sc[p]
            nb = nb_sc[p]
            state = jnp.concatenate([cp, nb], axis=1).astype(BF16)
            for jj in range(2):
                h = 2 * p + jj
                hs = slice(h * dv, (h + 1) * dv)
                mine = (lane1 >= jj * dk) & (lane1 < (jj + 1) * dk)
                qh = jnp.where(mine, qp, 0.0).astype(BF16)
                qsh = jnp.where(mine, qsp, 0.0).astype(BF16)
                d = neg_m[rows, h * dv:h * dv + chunk] + bt[h:h + 1, j * chunk:(j + 1) * chunk]
                s = _dot_nt(qh, kpb) * jnp.where(causal, jnp.exp(d), 0.0)
                vh = jnp.concatenate([v_ref[rows, hs].astype(BF16), ones_v], axis=1)
                out = _dot(qsh, state) + _dot(s.astype(BF16), vh)
                hh = out[:, :dv] / jnp.maximum(jnp.abs(out[:, dv:]), em_v[rows, hs])
                hn = _rms(hh, gn_ref[h:h + 1, :])
                u_ref[rows, hs] = (_sigmoid(op_ref[rows, hs]) * hn).astype(BF16)
            vp = jnp.concatenate([v_ref[rows, 2 * p * dv:(2 * p + 2) * dv].astype(BF16), ones_v], axis=1)
            full = _dot_tn(ke[rows, ps].astype(BF16), vp)
            upd = jnp.where(rowi < dk, full[:, :dv], full[:, dv:2 * dv])
            w_col = jnp.where(rowi < dk, w_c[j][:, 2 * p:2 * p + 1], w_c[j][:, 2 * p + 1:2 * p + 2])
            c_sc[p] = w_col * cp + upd
            nb_sc[p] = w_col * nb + full[:, 2 * dv:]

    @pl.when(c == nblk - 1)
    def _():
        ct_ref[0] = c_sc[...]
        for p in range(pairs):
            nt_ref[0, p:p + 1, :] = nb_sc[p].T[0:1, :]
        mt_ref[0] = m_sc[...]


def _mlstm_recurrence_blocked(z, gates, c0, n0, m0, g_norm, nseq, t, chunk, tb):
    heads = g_norm.shape[0]
    v_w = heads * g_norm.shape[1]
    qk_w = v_w // 2
    nblk = t // tb
    expand = _expand_matrix(heads)
    kern = functools.partial(_mlstm_block_kernel, chunk=chunk, tb=tb, nblk=nblk)
    row = lambda b, c: b * nblk + c
    st4 = lambda b, c: (b, 0, 0, 0)
    st3 = lambda b, c: (b, 0, 0)
    return pl.pallas_call(
        kern,
        grid=(nseq, nblk),
        in_specs=[
            pl.BlockSpec((tb, qk_w), lambda b, c: (row(b, c), 0)),
            pl.BlockSpec((tb, qk_w), lambda b, c: (row(b, c), 1)),
            pl.BlockSpec((tb, v_w), lambda b, c: (row(b, c), 1)),
            pl.BlockSpec((tb, v_w), lambda b, c: (row(b, c), 2)),
            pl.BlockSpec((tb, LANES), lambda b, c: (row(b, c), 0)),
            pl.BlockSpec(expand.shape, lambda b, c: (0, 0)),
            pl.BlockSpec((1,) + c0.shape[1:], st4),
            pl.BlockSpec((1,) + n0.shape[1:], st3),
            pl.BlockSpec((1,) + m0.shape[1:], st3),
            pl.BlockSpec(g_norm.shape, lambda b, c: (0, 0)),
        ],
        out_specs=[
            pl.BlockSpec((tb, v_w), lambda b, c: (row(b, c), 0)),
            pl.BlockSpec((1,) + c0.shape[1:], st4),
            pl.BlockSpec((1,) + n0.shape[1:], st3),
            pl.BlockSpec((1,) + m0.shape[1:], st3),
        ],
        out_shape=[
            jax.ShapeDtypeStruct((nseq * t, v_w), BF16),
            jax.ShapeDtypeStruct(c0.shape, F32),
            jax.ShapeDtypeStruct(n0.shape, F32),
            jax.ShapeDtypeStruct(m0.shape, F32),
        ],
        scratch_shapes=[
            pltpu.VMEM(c0.shape[1:], F32),
            pltpu.VMEM(c0.shape[1:], F32),
            pltpu.VMEM(m0.shape[1:], F32),
        ],
        compiler_params=_params(("arbitrary", "arbitrary")),
        name="mlstm_recurrence_t%d" % t,
    )(z, z, z, z, gates, expand, c0, n0, m0, g_norm)


def _out_proj_kernel(up_ref, us_ref, xp_ref, xs_ref, w_ref, g_ref, xo_ref, hn_ref, *, n_first):
    i = pl.program_id(0)
    u = _pick_rows(i, n_first, up_ref, us_ref)
    x = _pick_rows(i, n_first, xp_ref, xs_ref) + _dot(u, w_ref[...])
    xo_ref[...] = x
    hn_ref[...] = _rms(x, g_ref[...]).astype(hn_ref.dtype)


def _out_proj(u_p, u_s, x_p, x_s, w, g):
    d = x_p.shape[1]
    n = x_p.shape[0] + x_s.shape[0]
    n_first = x_p.shape[0] // TM
    return pl.pallas_call(
        functools.partial(_out_proj_kernel, n_first=n_first),
        grid=(n // TM,),
        in_specs=_split_specs(n_first, u_p.shape[1]) + _split_specs(n_first, d) + [
            pl.BlockSpec(w.shape, lambda i: (0, 0)),
            pl.BlockSpec((1, d), lambda i: (0, 0)),
        ],
        out_specs=[pl.BlockSpec((TM, d), lambda i: (i, 0)), pl.BlockSpec((TM, d), lambda i: (i, 0))],
        out_shape=[jax.ShapeDtypeStruct((n, d), F32), jax.ShapeDtypeStruct((n, d), BF16)],
        compiler_params=_params(("arbitrary",)),
        name="out_proj",
    )(u_p, u_s, x_p, x_s, w, g)


def _out_proj_router_kernel(up_ref, us_ref, x_ref, w_ref, g_ref, wr_ref, br_ref,
                            xo_ref, hn_ref, gate_ref, idx_ref, cnt_ref, cnt_sc, *, n_first):
    i = pl.program_id(0)
    u = _pick_rows(i, n_first, up_ref, us_ref)

    @pl.when(i == 0)
    def _():
        cnt_sc[...] = jnp.zeros_like(cnt_sc)

    x = x_ref[...] + _dot(u, w_ref[...])
    xo_ref[...] = x
    hn = _rms(x, g_ref[...])
    hn_ref[...] = hn
    hn_hi = hn.astype(BF16)
    hn_lo = (hn - hn_hi.astype(F32)).astype(BF16)
    logits = (_dot(hn_hi, wr_ref[0]) + _dot(hn_lo, wr_ref[0]) + _dot(hn_hi, wr_ref[1])
              + br_ref[...])
    tm = logits.shape[0]
    lane = lax.broadcasted_iota(I32, logits.shape, 1)
    valid = lane < N_EXPERTS
    logits = jnp.where(valid, logits, -jnp.inf)
    ex = jnp.exp(logits - jnp.max(logits, axis=-1, keepdims=True))
    probs = jnp.where(valid, ex / jnp.sum(ex, axis=-1, keepdims=True), -1.0)
    p1 = jnp.max(probs, axis=-1, keepdims=True)
    i1 = jnp.min(jnp.where(probs == p1, lane, LANES), axis=-1, keepdims=True)
    rest = jnp.where(lane == i1, -1.0, probs)
    p2 = jnp.max(rest, axis=-1, keepdims=True)
    i2 = jnp.min(jnp.where(rest == p2, lane, LANES), axis=-1, keepdims=True)
    tot = p1 + p2
    onehot = ((lane == i1) | (lane == i2)).astype(BF16)
    rr = lax.broadcasted_iota(I32, (tm, tm), 0)
    cc = lax.broadcasted_iota(I32, (tm, tm), 1)
    strict = (rr > cc).astype(BF16)
    before = _dot(strict, onehot) + cnt_sc[...]
    r1 = jnp.sum(jnp.where(lane == i1, before, 0.0), axis=-1, keepdims=True)
    r2 = jnp.sum(jnp.where(lane == i2, before, 0.0), axis=-1, keepdims=True)
    cnt_sc[...] = cnt_sc[...] + jnp.sum(onehot.astype(F32), axis=0, keepdims=True)
    gate_ref[...] = jnp.where(lane == 0, p1 / tot, jnp.where(lane == 1, p2 / tot, 0.0))
    idx_ref[...] = jnp.where(lane == 0, i1,
                             jnp.where(lane == 1, i2,
                                       jnp.where(lane == 2, r1.astype(I32),
                                                 jnp.where(lane == 3, r2.astype(I32), 0))))
    cnt_ref[...] = cnt_sc[...]


def _out_proj_router(u_p, u_s, x, w, g, w_router, b_router):
    n, d = x.shape
    n_first = u_p.shape[0] // TM
    row = lambda i: (i, 0)
    fix = lambda i: (0, 0)
    return pl.pallas_call(
        functools.partial(_out_proj_router_kernel, n_first=n_first),
        grid=(n // TM,),
        in_specs=_split_specs(n_first, u_p.shape[1]) + [
            pl.BlockSpec((TM, d), row),
            pl.BlockSpec(w.shape, fix),
            pl.BlockSpec((1, d), fix),
            pl.BlockSpec(w_router.shape, lambda i: (0, 0, 0)),
            pl.BlockSpec((1, LANES), fix),
        ],
        out_specs=[
            pl.BlockSpec((TM, d), row),
            pl.BlockSpec((TM, d), row),
            pl.BlockSpec((TM, LANES), row),
            pl.BlockSpec((TM, LANES), row),
            pl.BlockSpec((1, LANES), fix),
        ],
        out_shape=[
            jax.ShapeDtypeStruct((n, d), F32),
            jax.ShapeDtypeStruct((n, d), F32),
            jax.ShapeDtypeStruct((n, LANES), F32),
            jax.ShapeDtypeStruct((n, LANES), I32),
            jax.ShapeDtypeStruct((1, LANES), F32),
        ],
        scratch_shapes=[pltpu.VMEM((1, LANES), F32)],
        compiler_params=_params(("arbitrary",)),
        name="out_proj_router",
    )(u_p, u_s, x, w, g, w_router, b_router)


def _swiglu_kernel(hn_ref, x_ref, wg_ref, wu_ref, wd_ref, g_ref, xo_ref, ho_ref, acc_sc, *, nf):
    f = pl.program_id(1)
    hn = hn_ref[...]
    gate = _dot(hn, wg_ref[...])
    up = _dot(hn, wu_ref[...])
    act = (gate * _sigmoid(gate) * up).astype(BF16)
    part = _dot(act, wd_ref[...])

    @pl.when(f == 0)
    def _():
        acc_sc[...] = x_ref[...] + part

    @pl.when(f > 0)
    def _():
        acc_sc[...] = acc_sc[...] + part

    @pl.when(f == nf - 1)
    def _():
        x = acc_sc[...]
        xo_ref[...] = x
        ho_ref[...] = _rms(x, g_ref[...]).astype(BF16)


def _swiglu(hn, x, w_gu, w_down, g, nf):
    n, d = x.shape
    dff = w_down.shape[0]
    tf = dff // nf
    kern = functools.partial(_swiglu_kernel, nf=nf)
    row = lambda i, f: (i, 0)
    return pl.pallas_call(
        kern,
        grid=(n // TM, nf),
        in_specs=[
            pl.BlockSpec((TM, d), row),
            pl.BlockSpec((TM, d), row),
            pl.BlockSpec((d, tf), lambda i, f: (0, f)),
            pl.BlockSpec((d, tf), lambda i, f: (0, nf + f)),
            pl.BlockSpec((tf, d), lambda i, f: (f, 0)),
            pl.BlockSpec((1, d), lambda i, f: (0, 0)),
        ],
        out_specs=[pl.BlockSpec((TM, d), row), pl.BlockSpec((TM, d), row)],
        out_shape=[jax.ShapeDtypeStruct((n, d), F32), jax.ShapeDtypeStruct((n, d), BF16)],
        scratch_shapes=[pltpu.VMEM((TM, d), F32)],
        compiler_params=_params(("arbitrary", "arbitrary")),
        name="dense_swiglu",
    )(hn, x, w_gu, w_gu, w_down, g)


DMA_UNROLL = 8


def _row_copy(src_ref, src_row, dst_ref, dst_row, sem):
    return pltpu.make_async_copy(src_ref.at[pl.ds(src_row, 1), :], dst_ref.at[pl.ds(dst_row, 1), :], sem)


def _dispatch_kernel(dest_ref, hn_ref, xs_in_ref, xs_ref, sem):
    del xs_in_ref
    i = pl.program_id(0)
    tm = hn_ref.shape[0]

    def issue(blk, carry):
        for j in range(DMA_UNROLL):
            r = blk * DMA_UNROLL + j
            t = i * tm + r
            _row_copy(hn_ref, r, xs_ref, dest_ref[2 * t], sem).start(priority=0)
            _row_copy(hn_ref, r, xs_ref, dest_ref[2 * t + 1], sem).start(priority=1)
        return carry

    lax.fori_loop(0, tm // DMA_UNROLL, issue, 0)
    for _ in range(2):
        pltpu.make_async_copy(hn_ref, xs_ref.at[pl.ds(0, tm), :], sem).wait()


def _dispatch(dest_flat, hn, xs_zero):
    n, d = hn.shape
    return pl.pallas_call(
        _dispatch_kernel,
        grid_spec=pltpu.PrefetchScalarGridSpec(
            num_scalar_prefetch=1,
            grid=(n // TM,),
            in_specs=[
                pl.BlockSpec((TM, d), lambda i, dest: (i, 0)),
                pl.BlockSpec(memory_space=pl.ANY),
            ],
            out_specs=pl.BlockSpec(memory_space=pl.ANY),
            scratch_shapes=[pltpu.SemaphoreType.DMA(())],
        ),
        out_shape=jax.ShapeDtypeStruct(xs_zero.shape, xs_zero.dtype),
        input_output_aliases={2: 0},
        compiler_params=_params(("arbitrary",)),
        name="moe_dispatch",
    )(dest_flat, hn, xs_zero)


def _expert_kernel(te_ref, tv_ref, xs_ref, wg_ref, wu_ref, wd_ref, ys_ref, *, nchunk):
    i = pl.program_id(0)

    @pl.when(tv_ref[i] > 0)
    def _():
        x = xs_ref[...].astype(BF16)
        width = wg_ref.shape[2] // nchunk
        acc = None
        for c in range(nchunk):
            cs = slice(c * width, (c + 1) * width)
            gate = _dot(x, wg_ref[0, :, cs])
            up = _dot(x, wu_ref[0, :, cs])
            act = (gate * _sigmoid(gate) * up).astype(BF16)
            part = _dot(act, wd_ref[0, cs, :])
            acc = part if acc is None else acc + part
        ys_ref[...] = acc

    @pl.when(tv_ref[i] == 0)
    def _():
        ys_ref[...] = jnp.zeros_like(ys_ref)


def _experts(tile_expert, tile_valid, xs, w_gu, w_down):
    r, d = xs.shape
    dff = w_down.shape[1]
    kern = functools.partial(_expert_kernel, nchunk=7)
    return pl.pallas_call(
        kern,
        grid_spec=pltpu.PrefetchScalarGridSpec(
            num_scalar_prefetch=2,
            grid=(r // TM_EXPERT,),
            in_specs=[
                pl.BlockSpec((TM_EXPERT, d), lambda i, te, tv: (i, 0)),
                pl.BlockSpec((1, d, dff), lambda i, te, tv: (te[i], 0, 0)),
                pl.BlockSpec((1, d, dff), lambda i, te, tv: (te[i], 0, 1)),
                pl.BlockSpec((1, dff, d), lambda i, te, tv: (te[i], 0, 0)),
            ],
            out_specs=pl.BlockSpec((TM_EXPERT, d), lambda i, te, tv: (i, 0)),
        ),
        out_shape=jax.ShapeDtypeStruct((r, d), F32),
        compiler_params=_params(("arbitrary",)),
        name="moe_experts",
    )(tile_expert, tile_valid, xs, w_gu, w_gu, w_down)


def _combine_kernel(dest_ref, ys_ref, x_ref, gate_ref, g_ref, yp_ref, ysm_ref, buf, sem, *, n_first):
    i = pl.program_id(0)
    tm = x_ref.shape[0]

    def issue(blk, carry):
        for j in range(DMA_UNROLL):
            r = blk * DMA_UNROLL + j
            t = i * tm + r
            _row_copy(ys_ref, dest_ref[2 * t], buf.at[0], r, sem).start(priority=0)
            _row_copy(ys_ref, dest_ref[2 * t + 1], buf.at[1], r, sem).start(priority=1)
        return carry

    lax.fori_loop(0, tm // DMA_UNROLL, issue, 0)
    for slot in range(2):
        pltpu.make_async_copy(ys_ref.at[pl.ds(0, tm), :], buf.at[slot], sem).wait()
    gate = gate_ref[...]
    y = x_ref[...] + gate[:, 0:1] * buf[0] + gate[:, 1:2] * buf[1]
    y = _rms(y, g_ref[...])

    @pl.when(i < n_first)
    def _():
        yp_ref[...] = y

    @pl.when(i >= n_first)
    def _():
        ysm_ref[...] = y


def _combine(dest_flat, ys, x, gates, g, n_p):
    n, d = x.shape
    n_first = n_p // TM
    p_spec, s_spec = _split_specs(n_first, d)
    return pl.pallas_call(
        functools.partial(_combine_kernel, n_first=n_first),
        grid_spec=pltpu.PrefetchScalarGridSpec(
            num_scalar_prefetch=1,
            grid=(n // TM,),
            in_specs=[
                pl.BlockSpec(memory_space=pl.ANY),
                pl.BlockSpec((TM, d), lambda i, dest: (i, 0)),
                pl.BlockSpec((TM, LANES), lambda i, dest: (i, 0)),
                pl.BlockSpec((1, d), lambda i, dest: (0, 0)),
            ],
            out_specs=[p_spec, s_spec],
            scratch_shapes=[pltpu.VMEM((2, TM, d), F32), pltpu.SemaphoreType.DMA(())],
        ),
        out_shape=[jax.ShapeDtypeStruct((n_p, d), F32), jax.ShapeDtypeStruct((n - n_p, d), F32)],
        compiler_params=_params(("arbitrary",)),
        name="moe_combine",
    )(dest_flat, ys, x, gates, g)


def _routing_tables(idx, counts, n_rows):
    e = idx[:, 0:2]
    rank = idx[:, 2:4]
    cnt = counts[0, :N_EXPERTS].astype(I32)
    padded = ((cnt + TM_EXPERT - 1) // TM_EXPERT) * TM_EXPERT
    ends = jnp.cumsum(padded)
    starts = ends - padded
    dest = (starts[e] + rank).reshape(-1)
    tile_start = jnp.arange(n_rows // TM_EXPERT, dtype=I32) * TM_EXPERT
    tile_valid = (tile_start < ends[-1]).astype(I32)
    tile_expert = jnp.sum((tile_start[:, None] >= ends[None, :]).astype(I32), axis=1)
    last_expert = jnp.sum((ends[-1] - 1 >= ends).astype(I32))
    tile_expert = jnp.minimum(tile_expert, last_expert)
    return dest, tile_expert, tile_valid


def _pad_time(a, nseq, t, fill=None):
    w = a.shape[1]
    a = a.reshape(nseq, t, w)
    a = jnp.pad(a, ((0, 0), (0, SAMPLE_PAD_T - t), (0, 0)))
    if fill is not None:
        pad_row = jnp.arange(SAMPLE_PAD_T)[None, :, None] >= t
        lane = jnp.arange(w)[None, None, :]
        a = jnp.where(pad_row & (lane < MLSTM_HEADS), fill, a)
    return a.reshape(nseq * SAMPLE_PAD_T, w)


def _unpad_time(a, nseq, t):
    w = a.shape[1]
    return a.reshape(nseq, SAMPLE_PAD_T, w)[:, :t].reshape(nseq * t, w)


def kernel(x_prompt, x_sample, state_gla_S, state_mlstm_C, state_mlstm_n, state_mlstm_m,
           norm_mix, norm_ffn, norm_final,
           gla_w_in, gla_w_a2, gla_b_a, gla_g_norm, gla_w_out,
           mlstm_w_in, mlstm_b_gate, mlstm_g_norm, mlstm_w_out,
           ffn_w_gu, ffn_w_down,
           moe_w_router, moe_b_router, moe_w_gu, moe_w_down):
    bp, tp, d = x_prompt.shape
    bs, ts, _ = x_sample.shape
    n_p = bp * tp
    n_s = bs * ts
    n = n_p + n_s
    assert n_p % TM == 0 and n_s % TM == 0 and tp % (2 * CHUNK) == 0 and ts <= SAMPLE_PAD_T
    assert norm_mix.shape[0] == 2, "one GLA layer followed by one mLSTM layer"
    qk_w = d // 2
    main_w = 2 * qk_w + 2 * d

    x_p = x_prompt.reshape(n_p, d)
    x_s = x_sample.reshape(n_s, d)

    w_in = gla_w_in[0]
    w_lr = jnp.pad(w_in[:, main_w:], ((0, 0), (0, LANES - GLA_RANK))).astype(BF16)
    w_a2 = jnp.pad(gla_w_a2[0], ((0, LANES - GLA_RANK), (0, 0))).astype(BF16)
    z, log_a = _gla_in_proj(x_p, x_s, norm_mix[0][None], w_in[:, :main_w].astype(BF16), w_lr, w_a2,
                            gla_b_a[0][None])
    s0_p = jnp.zeros((bp,) + state_gla_S.shape[2:], F32)
    u_p, s_p = _gla_recurrence(z, log_a, s0_p, gla_g_norm[0], bp, tp, CHUNK, 4 * CHUNK)
    u_s, s_s = _gla_recurrence(_pad_time(z[n_p:], bs, ts), _pad_time(log_a[n_p:], bs, ts),
                               state_gla_S[0], gla_g_norm[0], bs, SAMPLE_PAD_T, SAMPLE_PAD_T,
                               SAMPLE_PAD_T, nsq=SAMPLE_SEQS_PER_STEP)
    x, hn = _out_proj(u_p, _unpad_time(u_s, bs, ts), x_p, x_s, gla_w_out[0].astype(BF16),
                      norm_ffn[0][None])
    x, hn = _swiglu(hn, x, ffn_w_gu[0].astype(BF16), ffn_w_down[0].astype(BF16),
                    norm_mix[1][None], 2)

    w_in = mlstm_w_in[0]
    w_gt = jnp.pad(w_in[:, main_w:], ((0, 0), (0, LANES - 2 * MLSTM_HEADS))).astype(BF16)
    b_gt = jnp.pad(mlstm_b_gate[0], (0, LANES - 2 * MLSTM_HEADS))[None]
    z, gates = _mlstm_in_proj(hn, w_in[:, :main_w].astype(BF16), w_gt, b_gt)
    pairs = MLSTM_HEADS // 2
    dk2 = 2 * state_mlstm_C.shape[3]
    dvm = state_mlstm_C.shape[4]
    c0_p = jnp.zeros((bp, pairs, dk2, dvm), F32)
    n0_p = jnp.zeros((bp, pairs, dk2), F32)
    m0_p = jnp.zeros((bp, 1, LANES), F32)
    u_p, c_p, nn_p, m_p = _mlstm_recurrence_blocked(z, gates, c0_p, n0_p, m0_p,
                                                    mlstm_g_norm[0], bp, tp, CHUNK, 4 * CHUNK)
    c0_s = state_mlstm_C[0].reshape(bs, pairs, dk2, dvm)
    n0_s = state_mlstm_n[0].reshape(bs, pairs, dk2)
    m0_s = jnp.broadcast_to(state_mlstm_m[0][:, :, None], (bs, MLSTM_HEADS, LANES))
    u_s, c_s, nn_s, m_s = _mlstm_recurrence_short(
        _pad_time(z[n_p:], bs, ts), _pad_time(gates[n_p:], bs, ts, fill=NEG_BIG),
        c0_s, n0_s, m0_s, mlstm_g_norm[0], bs, SAMPLE_PAD_T, SAMPLE_SEQS_PER_STEP)

    w_r = jnp.pad(moe_w_router[0], ((0, 0), (0, LANES - N_EXPERTS)))
    w_r_hi = w_r.astype(BF16)
    w_r = jnp.stack([w_r_hi, (w_r - w_r_hi.astype(F32)).astype(BF16)])
    b_r = jnp.pad(moe_b_router[0], (0, LANES - N_EXPERTS))[None]
    x, hn, route_g, route_i, counts = _out_proj_router(
        u_p, _unpad_time(u_s, bs, ts), x, mlstm_w_out[0].astype(BF16), norm_ffn[1][None], w_r, b_r)
    n_rows = -(-(2 * n + N_EXPERTS * (TM_EXPERT - 1)) // TM_EXPERT) * TM_EXPERT
    dest, tile_expert, tile_valid = _routing_tables(route_i, counts, n_rows)
    xs = _dispatch(dest, hn, jnp.zeros((n_rows, d), F32))
    ys = _experts(tile_expert, tile_valid, xs, moe_w_gu[0].astype(BF16), moe_w_down[0].astype(BF16))
    y_p, y_s = _combine(dest, ys, x, route_g, norm_final[None], n_p)

    y_prompt = y_p.reshape(bp, tp, d)
    y_sample = y_s.reshape(bs, ts, d)
    hd = state_mlstm_C.shape[2:]
    return (y_prompt, y_sample,
            s_p[None], c_p.reshape((1, bp) + hd), nn_p.reshape(1, bp, hd[0], hd[1]),
            m_p[None, :, 0, :MLSTM_HEADS],
            s_s[None], c_s.reshape((1, bs) + hd), nn_s.reshape(1, bs, hd[0], hd[1]), m_s[None, :, :, 0])


def _mlstm_in_proj(hn, w_main, w_gate, b_gate):
    n, d = hn.shape
    wz = w_main.shape[1]

    def kern(h_ref, w_ref, wg_ref, b_ref, z_ref, gate_ref):
        h = h_ref[...]
        width = wz // IN_PROJ_COL_CHUNKS
        for c in range(IN_PROJ_COL_CHUNKS):
            cs = slice(c * width, (c + 1) * width)
            z_ref[:, cs] = _dot(h, w_ref[:, cs])
        gp = _dot(h, wg_ref[...]) + b_ref[...]
        gc = GATE_CAP * jnp.tanh(gp * (1.0 / GATE_CAP))
        lane = lax.broadcasted_iota(I32, gc.shape, 1)
        out = jnp.where(lane < MLSTM_HEADS, gc, _log_sigmoid(gc))
        gate_ref[...] = jnp.where(lane < 2 * MLSTM_HEADS, out, 0.0)

    return pl.pallas_call(
        kern,
        grid=(n // TM,),
        in_specs=[
            pl.BlockSpec((TM, d), lambda i: (i, 0)),
            pl.BlockSpec((d, wz), lambda i: (0, 0)),
            pl.BlockSpec((d, LANES), lambda i: (0, 0)),
            pl.BlockSpec((1, LANES), lambda i: (0, 0)),
        ],
        out_specs=[pl.BlockSpec((TM, wz), lambda i: (i, 0)), pl.BlockSpec((TM, LANES), lambda i: (i, 0))],
        out_shape=[jax.ShapeDtypeStruct((n, wz), F32), jax.ShapeDtypeStruct((n, LANES), F32)],
        compiler_params=_params(("arbitrary",)),
        name="in_proj_mlstm",
    )(hn, w_main, w_gate, b_gate)
```

```python
import functools

import jax
import jax.numpy as jnp
import numpy as np
from jax import lax
from jax.experimental import pallas as pl
from jax.experimental.pallas import tpu as pltpu

F32 = jnp.float32
BF16 = jnp.bfloat16
I32 = jnp.int32

EPS = 1e-6
GLA_HEADS = 4
GLA_RANK = 16
GLA_TAU = 16.0
MLSTM_HEADS = 8
GATE_CAP = 15.0
CHUNK = 64
N_EXPERTS = 8
NEG_BIG = -1e30

LANES = 128
TM = 512
TM_EXPERT = 256
SAMPLE_PAD_T = 8
SAMPLE_SEQS_PER_STEP = 16
SWIGLU_CHUNKS = 11
VMEM_LIMIT = 56 * 1024 * 1024


def _dot(a, b):
    return jnp.dot(a, b, preferred_element_type=F32)


def _dot_nt(a, b):
    return lax.dot_general(a, b, (((1,), (1,)), ((), ())), preferred_element_type=F32)


def _dot_tn(a, b):
    return lax.dot_general(a, b, (((0,), (0,)), ((), ())), preferred_element_type=F32)


def _sigmoid(x):
    return 1.0 / (1.0 + jnp.exp(-x))


def _log_sigmoid(x):
    return jnp.minimum(x, 0.0) - jnp.log1p(jnp.exp(-jnp.abs(x)))


def _rms(x, g):
    return x * lax.rsqrt(jnp.mean(x * x, axis=-1, keepdims=True) + EPS) * g


def _cumsum_rows(x, n):
    ridx = lax.broadcasted_iota(I32, x.shape, 0)
    s = 1
    while s < n:
        x = x + jnp.where(ridx >= s, pltpu.roll(x, s, 0), 0.0)
        s *= 2
    return x


def _params(sem):
    return pltpu.CompilerParams(dimension_semantics=sem, vmem_limit_bytes=VMEM_LIMIT)


def _resident_spec(block_shape, index_map):
    return pl.BlockSpec(block_shape, index_map, pipeline_mode=pl.Buffered(1))


IN_PROJ_COL_CHUNKS = 4


def _pick_rows(i, n_first, first_ref, second_ref):
    return jnp.where(i < n_first, first_ref[...], second_ref[...])


def _split_specs(n_first, width):
    return [pl.BlockSpec((TM, width), lambda i, *_: (jnp.minimum(i, n_first - 1), 0)),
            pl.BlockSpec((TM, width), lambda i, *_: (jnp.maximum(i - n_first, 0), 0))]


def _gla_in_proj_kernel(xp_ref, xs_ref, g_ref, w_ref, wg_ref, w2_ref, b_ref, z_ref, gate_ref, *,
                        n_first):
    x = _pick_rows(pl.program_id(0), n_first, xp_ref, xs_ref)
    hn = _rms(x, g_ref[...]).astype(BF16)
    width = w_ref.shape[1] // IN_PROJ_COL_CHUNKS
    for c in range(IN_PROJ_COL_CHUNKS):
        cs = slice(c * width, (c + 1) * width)
        z_ref[:, cs] = _dot(hn, w_ref[:, cs])
    a = _dot(hn, wg_ref[...])
    la = _dot(a.astype(BF16), w2_ref[...]) + b_ref[...]
    gate_ref[...] = _log_sigmoid(la) * (1.0 / GLA_TAU)


def _gla_in_proj(x_p, x_s, g, w_main, w_gate, w2, b):
    d = x_p.shape[1]
    n = x_p.shape[0] + x_s.shape[0]
    n_first = x_p.shape[0] // TM
    wz = w_main.shape[1]
    gw = b.shape[1]
    return pl.pallas_call(
        functools.partial(_gla_in_proj_kernel, n_first=n_first),
        grid=(n // TM,),
        in_specs=_split_specs(n_first, d) + [
            pl.BlockSpec((1, d), lambda i: (0, 0)),
            pl.BlockSpec((d, wz), lambda i: (0, 0)),
            pl.BlockSpec((d, LANES), lambda i: (0, 0)),
            pl.BlockSpec(w2.shape, lambda i: (0, 0)),
            pl.BlockSpec((1, gw), lambda i: (0, 0)),
        ],
        out_specs=[
            pl.BlockSpec((TM, wz), lambda i: (i, 0)),
            pl.BlockSpec((TM, gw), lambda i: (i, 0)),
        ],
        out_shape=[jax.ShapeDtypeStruct((n, wz), F32), jax.ShapeDtypeStruct((n, gw), F32)],
        compiler_params=_params(("arbitrary",)),
        name="in_proj_gla",
    )(x_p, x_s, g, w_main, w_gate, w2, b)


def _gla_kernel(q_ref, k_ref, v_ref, r_ref, la_ref, s0_ref, gn_ref, u_ref, st_ref, st_sc,
                *, chunk, tb, nblk):
    heads = GLA_HEADS
    dk = q_ref.shape[1] // heads
    dv = v_ref.shape[1] // heads
    c = pl.program_id(1)

    @pl.when(c == 0)
    def _():
        for h in range(heads):
            st_sc[h] = s0_ref[0, h].T

    ri = lax.broadcasted_iota(I32, (chunk, chunk), 0)
    ci = lax.broadcasted_iota(I32, (chunk, chunk), 1)
    causal = ri >= ci
    kscale = dk ** -0.5
    for j in range(tb // chunk):
        rows = slice(j * chunk, (j + 1) * chunk)
        b_all = _cumsum_rows(la_ref[rows, :], chunk)
        for h in range(heads):
            ks = slice(h * dk, (h + 1) * dk)
            vs = slice(h * dv, (h + 1) * dv)
            b = b_all[:, ks]
            b_end = b[chunk - 1:chunk, :]
            q = q_ref[rows, ks]
            k = k_ref[rows, ks] * kscale
            qg = (q * jnp.exp(b)).astype(BF16)
            kg = (k * jnp.exp(-b)).astype(BF16)
            ke = (k * jnp.exp(b_end - b)).astype(BF16)
            v = v_ref[rows, vs].astype(BF16)
            a = jnp.where(causal, _dot_nt(qg, kg), 0.0).astype(BF16)
            st = st_sc[h]
            o = _dot_nt(qg, st.astype(BF16)) + _dot(a, v)
            st_sc[h] = st * jnp.exp(b_end) + _dot_tn(v, ke)
            on = _rms(o, gn_ref[h:h + 1, :])
            r = r_ref[rows, vs]
            u_ref[rows, vs] = (r * _sigmoid(r) * on).astype(BF16)

    @pl.when(c == nblk - 1)
    def _():
        for h in range(heads):
            st_ref[0, h] = st_sc[h].T


def _gla_recurrence(z, log_a, s0, g_norm, nseq, t, chunk, tb):
    qk_w = log_a.shape[1]
    v_w = 2 * qk_w
    nblk = t // tb
    heads, dk, dv = s0.shape[1:]
    kern = functools.partial(_gla_kernel, chunk=chunk, tb=tb, nblk=nblk)
    row = lambda b, c: b * nblk + c
    return pl.pallas_call(
        kern,
        grid=(nseq, nblk),
        in_specs=[
            pl.BlockSpec((tb, qk_w), lambda b, c: (row(b, c), 0)),
            pl.BlockSpec((tb, qk_w), lambda b, c: (row(b, c), 1)),
            pl.BlockSpec((tb, v_w), lambda b, c: (row(b, c), 1)),
            pl.BlockSpec((tb, v_w), lambda b, c: (row(b, c), 2)),
            pl.BlockSpec((tb, qk_w), lambda b, c: (row(b, c), 0)),
            pl.BlockSpec((1, heads, dk, dv), lambda b, c: (b, 0, 0, 0)),
            pl.BlockSpec((heads, dv), lambda b, c: (0, 0)),
        ],
        out_specs=[
            pl.BlockSpec((tb, v_w), lambda b, c: (row(b, c), 0)),
            pl.BlockSpec((1, heads, dk, dv), lambda b, c: (b, 0, 0, 0)),
        ],
        out_shape=[
            jax.ShapeDtypeStruct((nseq * t, v_w), BF16),
            jax.ShapeDtypeStruct(s0.shape, F32),
        ],
        scratch_shapes=[pltpu.VMEM((heads, dv, dk), F32)],
        compiler_params=_params(("arbitrary", "arbitrary")),
        name="gla_recurrence_t%d" % t,
    )(z, z, z, z, log_a, s0, g_norm)


def _gla_short_kernel(q_ref, k_ref, v_ref, r_ref, la_ref, s0_ref, gn_ref, u_ref, st_ref, *, t, nsq):
    heads = GLA_HEADS
    dk = q_ref.shape[1] // heads
    dv = v_ref.shape[1] // heads
    nrow = nsq * t
    width = q_ref.shape[1]
    rc = lax.broadcasted_iota(I32, (nrow, width), 0) & (t - 1)
    b = _chunk_scan(la_ref[...], rc, t, jnp.add, 0.0)
    b_last = b.reshape(nsq, t, width)[:, t - 1:t, :]
    b_end = jnp.broadcast_to(b_last, (nsq, t, width)).reshape(nrow, width)
    k = k_ref[...] * (dk ** -0.5)
    qg = q_ref[...] * jnp.exp(b)
    kg = k * jnp.exp(-b)
    ke = k * jnp.exp(b_end - b)
    decay = jnp.exp(b_end)
    ri = lax.broadcasted_iota(I32, (nrow, nrow), 0)
    ci = lax.broadcasted_iota(I32, (nrow, nrow), 1)
    shift = t.bit_length() - 1
    mask = ((ri >> shift) == (ci >> shift)) & (ri >= ci)
    for h in range(heads):
        ks = slice(h * dk, (h + 1) * dk)
        vs = slice(h * dv, (h + 1) * dv)
        a = jnp.where(mask, _dot_nt(qg[:, ks].astype(BF16), kg[:, ks].astype(BF16)), 0.0)
        o = _dot(a.astype(BF16), v_ref[:, vs].astype(BF16))
        decay_t = decay[:, ks].T
        o_state = []
        for sq in range(nsq):
            r = slice(sq * t, (sq + 1) * t)
            s0 = s0_ref[sq, h]
            o_state.append(_dot(qg[r, ks].astype(BF16), s0.astype(BF16)))
            upd = _dot_tn(ke[r, ks].astype(BF16), v_ref[r, vs].astype(BF16))
            st_ref[sq, h] = decay_t[:, sq * t:sq * t + 1] * s0 + upd
        on = _rms(o + jnp.concatenate(o_state, axis=0), gn_ref[h:h + 1, :])
        rg = r_ref[:, vs]
        u_ref[:, vs] = (rg * _sigmoid(rg) * on).astype(BF16)


def _gla_recurrence_short(z, log_a, s0, g_norm, nseq, t, nsq):
    qk_w = log_a.shape[1]
    v_w = 2 * qk_w
    heads, dk, dv = s0.shape[1:]
    assert nsq * t == dk == LANES
    rb = nsq * t
    return pl.pallas_call(
        functools.partial(_gla_short_kernel, t=t, nsq=nsq),
        grid=(nseq // nsq,),
        in_specs=[
            pl.BlockSpec((rb, qk_w), lambda b: (b, 0)),
            pl.BlockSpec((rb, qk_w), lambda b: (b, 1)),
            pl.BlockSpec((rb, v_w), lambda b: (b, 1)),
            pl.BlockSpec((rb, v_w), lambda b: (b, 2)),
            pl.BlockSpec((rb, qk_w), lambda b: (b, 0)),
            pl.BlockSpec((nsq, heads, dk, dv), lambda b: (b, 0, 0, 0)),
            pl.BlockSpec((heads, dv), lambda b: (0, 0)),
        ],
        out_specs=[
            pl.BlockSpec((rb, v_w), lambda b: (b, 0)),
            pl.BlockSpec((nsq, heads, dk, dv), lambda b: (b, 0, 0, 0)),
        ],
        out_shape=[
            jax.ShapeDtypeStruct((nseq * t, v_w), BF16),
            jax.ShapeDtypeStruct(s0.shape, F32),
        ],
        compiler_params=_params(("arbitrary",)),
        name="gla_recurrence_t%d" % t,
    )(z, z, z, z, log_a, s0, g_norm)


def _mlstm_short_kernel(q_ref, k_ref, v_ref, op_ref, gt_ref, e_ref, c0_ref, n0_ref, m0_ref, gn_ref,
                        u_ref, ct_ref, nt_ref, mt_ref, *, t, nsq):
    heads = MLSTM_HEADS
    pairs = heads // 2
    dv = v_ref.shape[1] // heads
    dk = q_ref.shape[1] // heads
    nrow = nsq * t

    def per_seq(x3):
        return jnp.broadcast_to(x3, (nsq, t, x3.shape[2])).reshape(nrow, x3.shape[2])

    def last(x):
        return x.reshape(nsq, t, x.shape[1])[:, t - 1:t, :]

    rc = lax.broadcasted_iota(I32, (nrow, LANES), 0) & (t - 1)
    gts = gt_ref[...]
    f_cum = pltpu.roll(_chunk_scan(gts, rc, t, jnp.add, 0.0), LANES - heads, 1)
    b = gts - f_cum
    cmb = _chunk_scan(b, rc, t, jnp.maximum, NEG_BIG)
    mp = per_seq(m0_ref[...])
    big_m = jnp.maximum(mp, cmb)
    m_t = f_cum + big_m
    m_new = last(m_t)
    mt_ref[...] = m_new
    mn = per_seq(m_new)
    fe = per_seq(last(f_cum))
    w_i = jnp.exp(mp - big_m)
    em = jnp.exp(-m_t)
    k_sc = jnp.exp(fe - f_cum + gts - mn)
    w_c = jnp.exp(fe + mp - mn)
    ex = _dot(_pack_terms([w_i, k_sc, -big_m, em, w_c], heads), e_ref[...])
    offs = np.cumsum([0] + [heads * w for w in SHORT_EXPAND_WIDTHS])
    q = q_ref[...]
    k = k_ref[...] * (dk ** -0.5)
    qs = q * ex[:, offs[0]:offs[1]]
    ke = k * ex[:, offs[1]:offs[2]]
    neg_m = ex[:, offs[2]:offs[3]]
    em_v = ex[:, offs[3]:offs[4]]
    wc_k = ex[:, offs[4]:offs[5]]
    bt = b.T

    ri = lax.broadcasted_iota(I32, (nrow, nrow), 0)
    ci = lax.broadcasted_iota(I32, (nrow, nrow), 1)
    shift = t.bit_length() - 1
    mask = ((ri >> shift) == (ci >> shift)) & (ri >= ci)
    lane1 = lax.broadcasted_iota(I32, (1, LANES), 1)
    rowi = lax.broadcasted_iota(I32, (LANES, LANES), 0)
    ones_v = jnp.ones((nrow, dv), BF16)
    ones_k = jnp.ones((LANES, dv), BF16)
    for p in range(pairs):
        ps = slice(p * LANES, (p + 1) * LANES)
        qp = q[:, ps]
        qsp = qs[:, ps]
        kpb = k[:, ps].astype(BF16)
        n_rows = per_seq(n0_ref[:, p:p + 1, :])
        intra, den_state, qsh = [], [], []
        for jj in range(2):
            h = 2 * p + jj
            hs = slice(h * dv, (h + 1) * dv)
            mine = (lane1 >= jj * dk) & (lane1 < (jj + 1) * dk)
            qh = jnp.where(mine, qp, 0.0).astype(BF16)
            qsh.append(jnp.where(mine, qsp, 0.0))
            d = neg_m[:, hs] + bt[h:h + 1, :]
            s = _dot_nt(qh, kpb) * jnp.where(mask, jnp.exp(d), 0.0)
            vh = jnp.concatenate([v_ref[:, hs].astype(BF16), ones_v], axis=1)
            intra.append(_dot(s.astype(BF16), vh))
            den_state.append(_dot((qsh[jj] * n_rows).astype(BF16), ones_k))
        num_state = [[], []]
        for sq in range(nsq):
            r = slice(sq * t, (sq + 1) * t)
            lhs = jnp.concatenate([qsh[0][r], qsh[1][r]], axis=0).astype(BF16)
            res = _dot(lhs, c0_ref[sq, p].astype(BF16))
            num_state[0].append(res[:t])
            num_state[1].append(res[t:])
        for jj in range(2):
            h = 2 * p + jj
            hs = slice(h * dv, (h + 1) * dv)
            num = intra[jj][:, :dv] + jnp.concatenate(num_state[jj], axis=0)
            den = intra[jj][:, dv:] + den_state[jj]
            hh = num / jnp.maximum(jnp.abs(den), em_v[:, hs])
            hn = _rms(hh, gn_ref[h:h + 1, :])
            u_ref[:, hs] = (_sigmoid(op_ref[:, hs]) * hn).astype(BF16)
        kep = ke[:, ps]
        for sq in range(nsq):
            r = slice(sq * t, (sq + 1) * t)
            vp = v_ref[r, 2 * p * dv:(2 * p + 2) * dv].astype(BF16)
            full = _dot_tn(kep[r].astype(BF16), vp)
            upd = jnp.where(rowi < dk, full[:, :dv], full[:, dv:])
            w_row = w_c[sq * t:sq * t + 1, :]
            w_col = jnp.where(rowi < dk, w_row[:, 2 * p:2 * p + 1], w_row[:, 2 * p + 1:2 * p + 2])
            ct_ref[sq, p] = w_col * c0_ref[sq, p] + upd
        k_sum = jnp.sum(kep.reshape(nsq, t, LANES), axis=1, keepdims=True)
        nt_ref[:, p:p + 1, :] = last(wc_k[:, ps]) * n0_ref[:, p:p + 1, :] + k_sum


def _mlstm_recurrence_short(z, gates, c0, n0, m0, g_norm, nseq, t, nsq):
    heads = g_norm.shape[0]
    v_w = heads * g_norm.shape[1]
    qk_w = v_w // 2
    assert nsq * t == LANES
    expand = _expand_matrix(heads, SHORT_EXPAND_WIDTHS)
    kern = functools.partial(_mlstm_short_kernel, t=t, nsq=nsq)
    rb = nsq * t
    st4 = lambda b: (b, 0, 0, 0)
    st3 = lambda b: (b, 0, 0)
    return pl.pallas_call(
        kern,
        grid=(nseq // nsq,),
        in_specs=[
            pl.BlockSpec((rb, qk_w), lambda b: (b, 0)),
            pl.BlockSpec((rb, qk_w), lambda b: (b, 1)),
            pl.BlockSpec((rb, v_w), lambda b: (b, 1)),
            pl.BlockSpec((rb, v_w), lambda b: (b, 2)),
            pl.BlockSpec((rb, LANES), lambda b: (b, 0)),
            pl.BlockSpec(expand.shape, lambda b: (0, 0)),
            pl.BlockSpec((nsq,) + c0.shape[1:], st4),
            pl.BlockSpec((nsq,) + n0.shape[1:], st3),
            pl.BlockSpec((nsq,) + m0.shape[1:], st3),
            pl.BlockSpec(g_norm.shape, lambda b: (0, 0)),
        ],
        out_specs=[
            pl.BlockSpec((rb, v_w), lambda b: (b, 0)),
            pl.BlockSpec((nsq,) + c0.shape[1:], st4),
            pl.BlockSpec((nsq,) + n0.shape[1:], st3),
            pl.BlockSpec((nsq,) + m0.shape[1:], st3),
        ],
        out_shape=[
            jax.ShapeDtypeStruct((nseq * t, v_w), BF16),
            jax.ShapeDtypeStruct(c0.shape, F32),
            jax.ShapeDtypeStruct(n0.shape, F32),
            jax.ShapeDtypeStruct(m0.shape, F32),
        ],
        compiler_params=_params(("arbitrary",)),
        name="mlstm_recurrence_t%d" % t,
    )(z, z, z, z, gates, expand, c0, n0, m0, g_norm)


N_SPLIT = 3
EXPAND_WIDTHS = (64, 64, 128, 128)
SHORT_EXPAND_WIDTHS = EXPAND_WIDTHS + (64,)


def _expand_matrix(heads, widths):
    assert len(widths) * N_SPLIT * heads <= LANES
    cols = []
    for qi, width in enumerate(widths):
        sel = np.zeros((LANES, heads * width), np.float32)
        for term in range(N_SPLIT):
            for h in range(heads):
                sel[(qi * N_SPLIT + term) * heads + h, h * width:(h + 1) * width] = 1.0
        cols.append(sel)
    return jnp.asarray(np.concatenate(cols, axis=1), BF16)


def _chunk_scan(x, rc, n, op, fill):
    s = 1
    while s < n:
        x = op(x, jnp.where(rc >= s, pltpu.roll(x, s, 0), fill))
        s *= 2
    return x


def _pack_terms(values, heads):
    lane = lax.broadcasted_iota(I32, values[0].shape, 1)
    packed = jnp.zeros(values[0].shape, F32)
    slot = 0
    for val in values:
        rest = val
        for term in range(N_SPLIT):
            part = rest.astype(BF16).astype(F32)
            rest = rest - part
            moved = part if slot == 0 else pltpu.roll(part, slot * heads, 1)
            packed = jnp.where((lane >= slot * heads) & (lane < (slot + 1) * heads), moved, packed)
            slot += 1
    return packed.astype(BF16)


def _mlstm_block_kernel(q_ref, k_ref, v_ref, op_ref, gt_ref, e_ref, c0_ref, n0_ref, m0_ref, gn_ref,
                        u_ref, ct_ref, nt_ref, mt_ref, c_sc, nb_sc, m_sc, *, chunk, tb, nblk):
    heads = MLSTM_HEADS
    pairs = heads // 2
    dv = v_ref.shape[1] // heads
    dk = q_ref.shape[1] // heads
    c = pl.program_id(1)

    @pl.when(c == 0)
    def _():
        c_sc[...] = c0_ref[0]
        for p in range(pairs):
            nb_sc[p] = jnp.broadcast_to(n0_ref[0, p:p + 1, :], (LANES, LANES)).T
        m_sc[...] = m0_ref[0]

    nch = tb // chunk
    rc = lax.broadcasted_iota(I32, (tb, LANES), 0) & (chunk - 1)
    gts = gt_ref[...]
    f_cum = pltpu.roll(_chunk_scan(gts, rc, chunk, jnp.add, 0.0), LANES - heads, 1)
    b = gts - f_cum
    cmb = _chunk_scan(b, rc, chunk, jnp.maximum, NEG_BIG)
    m_prev = m_sc[...]
    mp_rows, mn_rows, fe_rows, w_c = [], [], [], []
    for j in range(nch):
        last = (j + 1) * chunk - 1
        f_end = f_cum[last:last + 1, :]
        m_new = f_end + jnp.maximum(m_prev, cmb[last:last + 1, :])
        w_c.append(jnp.exp(f_end + m_prev - m_new))
        mp_rows.append(jnp.broadcast_to(m_prev, (chunk, LANES)))
        mn_rows.append(jnp.broadcast_to(m_new, (chunk, LANES)))
        fe_rows.append(jnp.broadcast_to(f_end, (chunk, LANES)))
        m_prev = m_new
    m_sc[...] = m_prev
    mp = jnp.concatenate(mp_rows, axis=0)
    mn = jnp.concatenate(mn_rows, axis=0)
    fe = jnp.concatenate(fe_rows, axis=0)
    big_m = jnp.maximum(mp, cmb)
    w_i = jnp.exp(mp - big_m)
    em = jnp.exp(-(f_cum + big_m))
    k_sc = jnp.exp(fe - f_cum + gts - mn)
    ex = _dot(_pack_terms([w_i, k_sc, -big_m, em], heads), e_ref[...])
    o1 = heads * EXPAND_WIDTHS[0]
    o2 = o1 + heads * EXPAND_WIDTHS[1]
    o3 = o2 + heads * EXPAND_WIDTHS[2]
    q = q_ref[...]
    k = k_ref[...] * (dk ** -0.5)
    qs = q * ex[:, :o1]
    ke = k * ex[:, o1:o2]
    neg_m = ex[:, o2:o3]
    em_v = ex[:, o3:]
    bt = b.T

    ri = lax.broadcasted_iota(I32, (chunk, chunk), 0)
    ci = lax.broadcasted_iota(I32, (chunk, chunk), 1)
    causal = ri >= ci
    lane1 = lax.broadcasted_iota(I32, (1, LANES), 1)
    rowi = lax.broadcasted_iota(I32, (LANES, LANES), 0)
    ones_v = jnp.ones((chunk, dv), BF16)
    for j in range(nch):
        rows = slice(j * chunk, (j + 1) * chunk)
        for p in range(pairs):
            ps = slice(p * LANES, (p + 1) * LANES)
            qp = q[rows, ps]
            qsp = qs[rows, ps]
            kpb = k[rows, ps].astype(BF16)
            cp = c_sc[p]
            nb = nb_sc[p]
            state = jnp.concatenate([cp, nb], axis=1).astype(BF16)
            for jj in range(2):
                h = 2 * p + jj
                hs = slice(h * dv, (h + 1) * dv)
                mine = (lane1 >= jj * dk) & (lane1 < (jj + 1) * dk)
                qh = jnp.where(mine, qp, 0.0).astype(BF16)
                qsh = jnp.where(mine, qsp, 0.0).astype(BF16)
                d = neg_m[rows, h * dv:h * dv + chunk] + bt[h:h + 1, j * chunk:(j + 1) * chunk]
                s = _dot_nt(qh, kpb) * jnp.where(causal, jnp.exp(d), 0.0)
                vh = jnp.concatenate([v_ref[rows, hs].astype(BF16), ones_v], axis=1)
                out = _dot(qsh, state) + _dot(s.astype(BF16), vh)
                hh = out[:, :dv] / jnp.maximum(jnp.abs(out[:, dv:]), em_v[rows, hs])
                hn = _rms(hh, gn_ref[h:h + 1, :])
                u_ref[rows, hs] = (_sigmoid(op_ref[rows, hs]) * hn).astype(BF16)
            vp = jnp.concatenate([v_ref[rows, 2 * p * dv:(2 * p + 2) * dv].astype(BF16), ones_v], axis=1)
            full = _dot_tn(ke[rows, ps].astype(BF16), vp)
            upd = jnp.where(rowi < dk, full[:, :dv], full[:, dv:2 * dv])
            w_col = jnp.where(rowi < dk, w_c[j][:, 2 * p:2 * p + 1], w_c[j][:, 2 * p + 1:2 * p + 2])
            c_sc[p] = w_col * cp + upd
            nb_sc[p] = w_col * nb + full[:, 2 * dv:]

    @pl.when(c == nblk - 1)
    def _():
        ct_ref[0] = c_sc[...]
        for p in range(pairs):
            nt_ref[0, p:p + 1, :] = nb_sc[p].T[0:1, :]
        mt_ref[0] = m_sc[...]


def _mlstm_recurrence_blocked(z, gates, c0, n0, m0, g_norm, nseq, t, chunk, tb):
    heads = g_norm.shape[0]
    v_w = heads * g_norm.shape[1]
    qk_w = v_w // 2
    nblk = t // tb
    expand = _expand_matrix(heads, EXPAND_WIDTHS)
    kern = functools.partial(_mlstm_block_kernel, chunk=chunk, tb=tb, nblk=nblk)
    row = lambda b, c: b * nblk + c
    st4 = lambda b, c: (b, 0, 0, 0)
    st3 = lambda b, c: (b, 0, 0)
    return pl.pallas_call(
        kern,
        grid=(nseq, nblk),
        in_specs=[
            pl.BlockSpec((tb, qk_w), lambda b, c: (row(b, c), 0)),
            pl.BlockSpec((tb, qk_w), lambda b, c: (row(b, c), 1)),
            pl.BlockSpec((tb, v_w), lambda b, c: (row(b, c), 1)),
            pl.BlockSpec((tb, v_w), lambda b, c: (row(b, c), 2)),
            pl.BlockSpec((tb, LANES), lambda b, c: (row(b, c), 0)),
            pl.BlockSpec(expand.shape, lambda b, c: (0, 0)),
            pl.BlockSpec((1,) + c0.shape[1:], st4),
            pl.BlockSpec((1,) + n0.shape[1:], st3),
            pl.BlockSpec((1,) + m0.shape[1:], st3),
            pl.BlockSpec(g_norm.shape, lambda b, c: (0, 0)),
        ],
        out_specs=[
            pl.BlockSpec((tb, v_w), lambda b, c: (row(b, c), 0)),
            pl.BlockSpec((1,) + c0.shape[1:], st4),
            pl.BlockSpec((1,) + n0.shape[1:], st3),
            pl.BlockSpec((1,) + m0.shape[1:], st3),
        ],
        out_shape=[
            jax.ShapeDtypeStruct((nseq * t, v_w), BF16),
            jax.ShapeDtypeStruct(c0.shape, F32),
            jax.ShapeDtypeStruct(n0.shape, F32),
            jax.ShapeDtypeStruct(m0.shape, F32),
        ],
        scratch_shapes=[
            pltpu.VMEM(c0.shape[1:], F32),
            pltpu.VMEM(c0.shape[1:], F32),
            pltpu.VMEM(m0.shape[1:], F32),
        ],
        compiler_params=_params(("arbitrary", "arbitrary")),
        name="mlstm_recurrence_t%d" % t,
    )(z, z, z, z, gates, expand, c0, n0, m0, g_norm)


def _out_proj_kernel(up_ref, us_ref, xp_ref, xs_ref, w_ref, g_ref, xo_ref, hn_ref, *, n_first):
    i = pl.program_id(0)
    u = _pick_rows(i, n_first, up_ref, us_ref)
    x = _pick_rows(i, n_first, xp_ref, xs_ref) + _dot(u, w_ref[...])
    xo_ref[...] = x
    hn_ref[...] = _rms(x, g_ref[...]).astype(hn_ref.dtype)


def _out_proj(u_p, u_s, x_p, x_s, w, g):
    d = x_p.shape[1]
    n = x_p.shape[0] + x_s.shape[0]
    n_first = x_p.shape[0] // TM
    return pl.pallas_call(
        functools.partial(_out_proj_kernel, n_first=n_first),
        grid=(n // TM,),
        in_specs=_split_specs(n_first, u_p.shape[1]) + _split_specs(n_first, d) + [
            pl.BlockSpec(w.shape, lambda i: (0, 0)),
            pl.BlockSpec((1, d), lambda i: (0, 0)),
        ],
        out_specs=[pl.BlockSpec((TM, d), lambda i: (i, 0)), pl.BlockSpec((TM, d), lambda i: (i, 0))],
        out_shape=[jax.ShapeDtypeStruct((n, d), F32), jax.ShapeDtypeStruct((n, d), BF16)],
        compiler_params=_params(("arbitrary",)),
        name="out_proj",
    )(u_p, u_s, x_p, x_s, w, g)


def _out_proj_router_kernel(up_ref, us_ref, x_ref, w_ref, g_ref, wr_ref, br_ref,
                            xo_ref, hn_ref, gate_ref, idx_ref, cnt_ref, cnt_sc, *, n_first):
    i = pl.program_id(0)
    u = _pick_rows(i, n_first, up_ref, us_ref)

    @pl.when(i == 0)
    def _():
        cnt_sc[...] = jnp.zeros_like(cnt_sc)

    x = x_ref[...] + _dot(u, w_ref[...])
    xo_ref[...] = x
    hn = _rms(x, g_ref[...])
    hn_ref[...] = hn
    hn_hi = hn.astype(BF16)
    hn_lo = (hn - hn_hi.astype(F32)).astype(BF16)
    logits = (_dot(hn_hi, wr_ref[0]) + _dot(hn_lo, wr_ref[0]) + _dot(hn_hi, wr_ref[1])
              + br_ref[...])
    tm = logits.shape[0]
    lane = lax.broadcasted_iota(I32, logits.shape, 1)
    valid = lane < N_EXPERTS
    logits = jnp.where(valid, logits, -jnp.inf)
    ex = jnp.exp(logits - jnp.max(logits, axis=-1, keepdims=True))
    probs = jnp.where(valid, ex / jnp.sum(ex, axis=-1, keepdims=True), -1.0)
    p1 = jnp.max(probs, axis=-1, keepdims=True)
    i1 = jnp.min(jnp.where(probs == p1, lane, LANES), axis=-1, keepdims=True)
    rest = jnp.where(lane == i1, -1.0, probs)
    p2 = jnp.max(rest, axis=-1, keepdims=True)
    i2 = jnp.min(jnp.where(rest == p2, lane, LANES), axis=-1, keepdims=True)
    tot = p1 + p2
    onehot = ((lane == i1) | (lane == i2)).astype(BF16)
    rr = lax.broadcasted_iota(I32, (tm, tm), 0)
    cc = lax.broadcasted_iota(I32, (tm, tm), 1)
    strict = (rr > cc).astype(BF16)
    before = _dot(strict, onehot) + cnt_sc[...]
    r1 = jnp.sum(jnp.where(lane == i1, before, 0.0), axis=-1, keepdims=True)
    r2 = jnp.sum(jnp.where(lane == i2, before, 0.0), axis=-1, keepdims=True)
    cnt_sc[...] = cnt_sc[...] + jnp.sum(onehot.astype(F32), axis=0, keepdims=True)
    gate_ref[...] = jnp.where(lane == 0, p1 / tot, jnp.where(lane == 1, p2 / tot, 0.0))
    idx_ref[...] = jnp.where(lane == 0, i1,
                             jnp.where(lane == 1, i2,
                                       jnp.where(lane == 2, r1.astype(I32),
                                                 jnp.where(lane == 3, r2.astype(I32), 0))))
    cnt_ref[...] = cnt_sc[...]


def _out_proj_router(u_p, u_s, x, w, g, w_router, b_router):
    n, d = x.shape
    n_first = u_p.shape[0] // TM
    row = lambda i: (i, 0)
    fix = lambda i: (0, 0)
    return pl.pallas_call(
        functools.partial(_out_proj_router_kernel, n_first=n_first),
        grid=(n // TM,),
        in_specs=_split_specs(n_first, u_p.shape[1]) + [
            pl.BlockSpec((TM, d), row),
            pl.BlockSpec(w.shape, fix),
            pl.BlockSpec((1, d), fix),
            pl.BlockSpec(w_router.shape, lambda i: (0, 0, 0)),
            pl.BlockSpec((1, LANES), fix),
        ],
        out_specs=[
            pl.BlockSpec((TM, d), row),
            pl.BlockSpec((TM, d), row),
            pl.BlockSpec((TM, LANES), row),
            pl.BlockSpec((TM, LANES), row),
            pl.BlockSpec((1, LANES), fix),
        ],
        out_shape=[
            jax.ShapeDtypeStruct((n, d), F32),
            jax.ShapeDtypeStruct((n, d), F32),
            jax.ShapeDtypeStruct((n, LANES), F32),
            jax.ShapeDtypeStruct((n, LANES), I32),
            jax.ShapeDtypeStruct((1, LANES), F32),
        ],
        scratch_shapes=[pltpu.VMEM((1, LANES), F32)],
        compiler_params=_params(("arbitrary",)),
        name="out_proj_router",
    )(u_p, u_s, x, w, g, w_router, b_router)


def _swiglu_kernel(hn_ref, x_ref, wg_ref, wu_ref, wd_ref, g_ref, xo_ref, ho_ref, *, nchunk):
    hn = hn_ref[...]
    width = wd_ref.shape[0] // nchunk
    x = x_ref[...]
    for c in range(nchunk):
        cs = slice(c * width, (c + 1) * width)
        gate = _dot(hn, wg_ref[:, cs])
        up = _dot(hn, wu_ref[:, cs])
        act = (gate * _sigmoid(gate) * up).astype(BF16)
        x = x + _dot(act, wd_ref[cs, :])
    xo_ref[...] = x
    ho_ref[...] = _rms(x, g_ref[...]).astype(BF16)


def _swiglu(hn, x, w_gu, w_down, g, nchunk):
    n, d = x.shape
    dff = w_down.shape[0]
    kern = functools.partial(_swiglu_kernel, nchunk=nchunk)
    row = lambda i: (i, 0)
    return pl.pallas_call(
        kern,
        grid=(n // TM,),
        in_specs=[
            pl.BlockSpec((TM, d), row),
            pl.BlockSpec((TM, d), row),
            _resident_spec((d, dff), lambda i: (0, 0)),
            _resident_spec((d, dff), lambda i: (0, 1)),
            _resident_spec((dff, d), lambda i: (0, 0)),
            pl.BlockSpec((1, d), lambda i: (0, 0)),
        ],
        out_specs=[pl.BlockSpec((TM, d), row), pl.BlockSpec((TM, d), row)],
        out_shape=[jax.ShapeDtypeStruct((n, d), F32), jax.ShapeDtypeStruct((n, d), BF16)],
        compiler_params=_params(("arbitrary",)),
        name="dense_swiglu",
    )(hn, x, w_gu, w_gu, w_down, g)


DMA_UNROLL = 8


def _row_copy(src_ref, src_row, dst_ref, dst_row, sem):
    return pltpu.make_async_copy(src_ref.at[pl.ds(src_row, 1), :], dst_ref.at[pl.ds(dst_row, 1), :], sem)


def _dispatch_kernel(dest_ref, pad_ref, hn_ref, xs_ref, zero_sc, sem):
    i = pl.program_id(0)
    tm = hn_ref.shape[0]

    @pl.when(i == 0)
    def _():
        zero_sc[...] = jnp.zeros_like(zero_sc)
        for e in range(N_EXPERTS):
            base = pad_ref[2 * e]
            count = pad_ref[2 * e + 1]

            def issue_pad(j, carry, base=base):
                _row_copy(zero_sc, 0, xs_ref, base + j, sem).start()
                return carry

            def wait_pad(j, carry):
                _row_copy(zero_sc, 0, xs_ref, 0, sem).wait()
                return carry

            lax.fori_loop(0, count, issue_pad, 0)
            lax.fori_loop(0, count, wait_pad, 0)

        rows = zero_sc.shape[0]

        def tail_copy(j):
            first = (pad_ref[2 * N_EXPERTS] + j) * rows
            return pltpu.make_async_copy(zero_sc, xs_ref.at[pl.ds(first, rows), :], sem)

        def issue_tail(j, carry):
            tail_copy(j).start()
            return carry

        def wait_tail(j, carry):
            tail_copy(j).wait()
            return carry

        lax.fori_loop(0, pad_ref[2 * N_EXPERTS + 1], issue_tail, 0)
        lax.fori_loop(0, pad_ref[2 * N_EXPERTS + 1], wait_tail, 0)

    def issue(blk, carry):
        for j in range(DMA_UNROLL):
            r = blk * DMA_UNROLL + j
            t = i * tm + r
            _row_copy(hn_ref, r, xs_ref, dest_ref[2 * t], sem).start(priority=0)
            _row_copy(hn_ref, r, xs_ref, dest_ref[2 * t + 1], sem).start(priority=1)
        return carry

    lax.fori_loop(0, tm // DMA_UNROLL, issue, 0)
    for _ in range(2):
        pltpu.make_async_copy(hn_ref, xs_ref.at[pl.ds(0, tm), :], sem).wait()


def _dispatch(dest_flat, pad_rows, hn, n_rows):
    n, d = hn.shape
    return pl.pallas_call(
        _dispatch_kernel,
        grid_spec=pltpu.PrefetchScalarGridSpec(
            num_scalar_prefetch=2,
            grid=(n // TM,),
            in_specs=[pl.BlockSpec((TM, d), lambda i, dest, pad: (i, 0))],
            out_specs=pl.BlockSpec(memory_space=pl.ANY),
            scratch_shapes=[pltpu.VMEM((TM_EXPERT, d), F32), pltpu.SemaphoreType.DMA(())],
        ),
        out_shape=jax.ShapeDtypeStruct((n_rows, d), F32),
        compiler_params=_params(("arbitrary",)),
        name="moe_dispatch",
    )(dest_flat, pad_rows, hn)


def _expert_kernel(te_ref, tv_ref, xs_ref, wg_ref, wu_ref, wd_ref, ys_ref, *, nchunk):
    i = pl.program_id(0)

    @pl.when(tv_ref[i] > 0)
    def _():
        x = xs_ref[...].astype(BF16)
        width = wg_ref.shape[2] // nchunk
        acc = None
        for c in range(nchunk):
            cs = slice(c * width, (c + 1) * width)
            gate = _dot(x, wg_ref[0, :, cs])
            up = _dot(x, wu_ref[0, :, cs])
            act = (gate * _sigmoid(gate) * up).astype(BF16)
            part = _dot(act, wd_ref[0, cs, :])
            acc = part if acc is None else acc + part
        ys_ref[...] = acc

    @pl.when(tv_ref[i] == 0)
    def _():
        ys_ref[...] = jnp.zeros_like(ys_ref)


def _experts(tile_expert, tile_valid, xs, w_gu, w_down):
    r, d = xs.shape
    n_tiles = r // TM_EXPERT
    dff = w_down.shape[1]
    kern = functools.partial(_expert_kernel, nchunk=7)
    return pl.pallas_call(
        kern,
        grid_spec=pltpu.PrefetchScalarGridSpec(
            num_scalar_prefetch=2,
            grid=(r // TM_EXPERT,),
            in_specs=[
                pl.BlockSpec((TM_EXPERT, d), lambda i, te, tv: (jnp.minimum(i, tv[n_tiles]), 0)),
                pl.BlockSpec((1, d, dff), lambda i, te, tv: (te[i], 0, 0)),
                pl.BlockSpec((1, d, dff), lambda i, te, tv: (te[i], 0, 1)),
                pl.BlockSpec((1, dff, d), lambda i, te, tv: (te[i], 0, 0)),
            ],
            out_specs=pl.BlockSpec((TM_EXPERT, d), lambda i, te, tv: (i, 0)),
        ),
        out_shape=jax.ShapeDtypeStruct((r, d), F32),
        compiler_params=_params(("arbitrary",)),
        name="moe_experts",
    )(tile_expert, tile_valid, xs, w_gu, w_gu, w_down)


def _combine_kernel(dest_ref, ys_ref, x_ref, gate_ref, g_ref, yp_ref, ysm_ref, buf, sem, *, n_first):
    i = pl.program_id(0)
    tm = x_ref.shape[0]

    def issue(blk, carry):
        for j in range(DMA_UNROLL):
            r = blk * DMA_UNROLL + j
            t = i * tm + r
            _row_copy(ys_ref, dest_ref[2 * t], buf.at[0], r, sem).start(priority=0)
            _row_copy(ys_ref, dest_ref[2 * t + 1], buf.at[1], r, sem).start(priority=1)
        return carry

    lax.fori_loop(0, tm // DMA_UNROLL, issue, 0)
    for slot in range(2):
        pltpu.make_async_copy(ys_ref.at[pl.ds(0, tm), :], buf.at[slot], sem).wait()
    gate = gate_ref[...]
    y = x_ref[...] + gate[:, 0:1] * buf[0] + gate[:, 1:2] * buf[1]
    y = _rms(y, g_ref[...])

    @pl.when(i < n_first)
    def _():
        yp_ref[...] = y

    @pl.when(i >= n_first)
    def _():
        ysm_ref[...] = y


def _combine(dest_flat, ys, x, gates, g, n_p):
    n, d = x.shape
    n_first = n_p // TM
    p_spec, s_spec = _split_specs(n_first, d)
    return pl.pallas_call(
        functools.partial(_combine_kernel, n_first=n_first),
        grid_spec=pltpu.PrefetchScalarGridSpec(
            num_scalar_prefetch=1,
            grid=(n // TM,),
            in_specs=[
                pl.BlockSpec(memory_space=pl.ANY),
                pl.BlockSpec((TM, d), lambda i, dest: (i, 0)),
                pl.BlockSpec((TM, LANES), lambda i, dest: (i, 0)),
                pl.BlockSpec((1, d), lambda i, dest: (0, 0)),
            ],
            out_specs=[p_spec, s_spec],
            scratch_shapes=[pltpu.VMEM((2, TM, d), F32), pltpu.SemaphoreType.DMA(())],
        ),
        out_shape=[jax.ShapeDtypeStruct((n_p, d), F32), jax.ShapeDtypeStruct((n - n_p, d), F32)],
        compiler_params=_params(("arbitrary",)),
        name="moe_combine",
    )(dest_flat, ys, x, gates, g)


def _routing_tables(idx, counts, n_rows):
    e = idx[:, 0:2]
    rank = idx[:, 2:4]
    cnt = counts[0, :N_EXPERTS].astype(I32)
    padded = ((cnt + TM_EXPERT - 1) // TM_EXPERT) * TM_EXPERT
    ends = jnp.cumsum(padded)
    starts = ends - padded
    dest = (starts[e] + rank).reshape(-1)
    tile_start = jnp.arange(n_rows // TM_EXPERT, dtype=I32) * TM_EXPERT
    tile_valid = (tile_start < ends[-1]).astype(I32)
    tile_valid = jnp.concatenate([tile_valid, ends[-1:] // TM_EXPERT - 1])
    tile_expert = jnp.sum((tile_start[:, None] >= ends[None, :]).astype(I32), axis=1)
    last_expert = jnp.sum((ends[-1] - 1 >= ends).astype(I32))
    tile_expert = jnp.minimum(tile_expert, last_expert)
    pad_rows = jnp.stack([starts + cnt, padded - cnt], axis=1).reshape(-1)
    used_tiles = ends[-1:] // TM_EXPERT
    pad_rows = jnp.concatenate([pad_rows, used_tiles, n_rows // TM_EXPERT - used_tiles])
    return dest, pad_rows, tile_expert, tile_valid


def _pad_time(a, nseq, t, fill=None):
    w = a.shape[1]
    a = a.reshape(nseq, t, w)
    a = jnp.pad(a, ((0, 0), (0, SAMPLE_PAD_T - t), (0, 0)))
    if fill is not None:
        pad_row = jnp.arange(SAMPLE_PAD_T)[None, :, None] >= t
        lane = jnp.arange(w)[None, None, :]
        a = jnp.where(pad_row & (lane < MLSTM_HEADS), fill, a)
    return a.reshape(nseq * SAMPLE_PAD_T, w)


def _unpad_time(a, nseq, t):
    w = a.shape[1]
    return a.reshape(nseq, SAMPLE_PAD_T, w)[:, :t].reshape(nseq * t, w)


def kernel(x_prompt, x_sample, state_gla_S, state_mlstm_C, state_mlstm_n, state_mlstm_m,
           norm_mix, norm_ffn, norm_final,
           gla_w_in, gla_w_a2, gla_b_a, gla_g_norm, gla_w_out,
           mlstm_w_in, mlstm_b_gate, mlstm_g_norm, mlstm_w_out,
           ffn_w_gu, ffn_w_down,
           moe_w_router, moe_b_router, moe_w_gu, moe_w_down):
    bp, tp, d = x_prompt.shape
    bs, ts, _ = x_sample.shape
    n_p = bp * tp
    n_s = bs * ts
    n = n_p + n_s
    assert n_p % TM == 0 and n_s % TM == 0 and tp % (2 * CHUNK) == 0 and ts <= SAMPLE_PAD_T
    assert norm_mix.shape[0] == 2, "one GLA layer followed by one mLSTM layer"
    qk_w = d // 2
    main_w = 2 * qk_w + 2 * d

    x_p = x_prompt.reshape(n_p, d)
    x_s = x_sample.reshape(n_s, d)

    w_in = gla_w_in[0]
    w_lr = jnp.pad(w_in[:, main_w:], ((0, 0), (0, LANES - GLA_RANK))).astype(BF16)
    w_a2 = jnp.pad(gla_w_a2[0], ((0, LANES - GLA_RANK), (0, 0))).astype(BF16)
    z, log_a = _gla_in_proj(x_p, x_s, norm_mix[0][None], w_in[:, :main_w].astype(BF16), w_lr, w_a2,
                            gla_b_a[0][None])
    s0_p = jnp.zeros((bp,) + state_gla_S.shape[2:], F32)
    u_p, s_p = _gla_recurrence(z, log_a, s0_p, gla_g_norm[0], bp, tp, CHUNK, 4 * CHUNK)
    u_s, s_s = _gla_recurrence_short(_pad_time(z[n_p:], bs, ts), _pad_time(log_a[n_p:], bs, ts),
                                     state_gla_S[0], gla_g_norm[0], bs, SAMPLE_PAD_T,
                                     SAMPLE_SEQS_PER_STEP)
    x, hn = _out_proj(u_p, _unpad_time(u_s, bs, ts), x_p, x_s, gla_w_out[0].astype(BF16),
                      norm_ffn[0][None])
    x, hn = _swiglu(hn, x, ffn_w_gu[0].astype(BF16), ffn_w_down[0].astype(BF16),
                    norm_mix[1][None], SWIGLU_CHUNKS)

    w_in = mlstm_w_in[0]
    w_gt = jnp.pad(w_in[:, main_w:], ((0, 0), (0, LANES - 2 * MLSTM_HEADS))).astype(BF16)
    b_gt = jnp.pad(mlstm_b_gate[0], (0, LANES - 2 * MLSTM_HEADS))[None]
    z, gates = _mlstm_in_proj(hn, w_in[:, :main_w].astype(BF16), w_gt, b_gt)
    pairs = MLSTM_HEADS // 2
    dk2 = 2 * state_mlstm_C.shape[3]
    dvm = state_mlstm_C.shape[4]
    c0_p = jnp.zeros((bp, pairs, dk2, dvm), F32)
    n0_p = jnp.zeros((bp, pairs, dk2), F32)
    m0_p = jnp.zeros((bp, 1, LANES), F32)
    u_p, c_p, nn_p, m_p = _mlstm_recurrence_blocked(z, gates, c0_p, n0_p, m0_p,
                                                    mlstm_g_norm[0], bp, tp, CHUNK, 4 * CHUNK)
    c0_s = state_mlstm_C[0].reshape(bs, pairs, dk2, dvm)
    n0_s = state_mlstm_n[0].reshape(bs, pairs, dk2)
    m0_s = jnp.pad(state_mlstm_m[0], ((0, 0), (0, LANES - MLSTM_HEADS)))[:, None, :]
    u_s, c_s, nn_s, m_s = _mlstm_recurrence_short(
        _pad_time(z[n_p:], bs, ts), _pad_time(gates[n_p:], bs, ts, fill=NEG_BIG),
        c0_s, n0_s, m0_s, mlstm_g_norm[0], bs, SAMPLE_PAD_T, SAMPLE_SEQS_PER_STEP)

    w_r = jnp.pad(moe_w_router[0], ((0, 0), (0, LANES - N_EXPERTS)))
    w_r_hi = w_r.astype(BF16)
    w_r = jnp.stack([w_r_hi, (w_r - w_r_hi.astype(F32)).astype(BF16)])
    b_r = jnp.pad(moe_b_router[0], (0, LANES - N_EXPERTS))[None]
    x, hn, route_g, route_i, counts = _out_proj_router(
        u_p, _unpad_time(u_s, bs, ts), x, mlstm_w_out[0].astype(BF16), norm_ffn[1][None], w_r, b_r)
    n_rows = -(-(2 * n + N_EXPERTS * (TM_EXPERT - 1)) // TM_EXPERT) * TM_EXPERT
    dest, pad_rows, tile_expert, tile_valid = _routing_tables(route_i, counts, n_rows)
    xs = _dispatch(dest, pad_rows, hn, n_rows)
    ys = _experts(tile_expert, tile_valid, xs, moe_w_gu[0].astype(BF16), moe_w_down[0].astype(BF16))
    y_p, y_s = _combine(dest, ys, x, route_g, norm_final[None], n_p)

    y_prompt = y_p.reshape(bp, tp, d)
    y_sample = y_s.reshape(bs, ts, d)
    hd = state_mlstm_C.shape[2:]
    return (y_prompt, y_sample,
            s_p[None], c_p.reshape((1, bp) + hd), nn_p.reshape(1, bp, hd[0], hd[1]),
            m_p[None, :, 0, :MLSTM_HEADS],
            s_s[None], c_s.reshape((1, bs) + hd), nn_s.reshape(1, bs, hd[0], hd[1]), m_s[None, :, 0, :MLSTM_HEADS])


def _mlstm_in_proj(hn, w_main, w_gate, b_gate):
    n, d = hn.shape
    wz = w_main.shape[1]

    def kern(h_ref, w_ref, wg_ref, b_ref, z_ref, gate_ref):
        h = h_ref[...]
        width = wz // IN_PROJ_COL_CHUNKS
        for c in range(IN_PROJ_COL_CHUNKS):
            cs = slice(c * width, (c + 1) * width)
            z_ref[:, cs] = _dot(h, w_ref[:, cs])
        gp = _dot(h, wg_ref[...]) + b_ref[...]
        gc = GATE_CAP * jnp.tanh(gp * (1.0 / GATE_CAP))
        lane = lax.broadcasted_iota(I32, gc.shape, 1)
        out = jnp.where(lane < MLSTM_HEADS, gc, _log_sigmoid(gc))
        gate_ref[...] = jnp.where(lane < 2 * MLSTM_HEADS, out, 0.0)

    return pl.pallas_call(
        kern,
        grid=(n // TM,),
        in_specs=[
            pl.BlockSpec((TM, d), lambda i: (i, 0)),
            pl.BlockSpec((d, wz), lambda i: (0, 0)),
            pl.BlockSpec((d, LANES), lambda i: (0, 0)),
            pl.BlockSpec((1, LANES), lambda i: (0, 0)),
        ],
        out_specs=[pl.BlockSpec((TM, wz), lambda i: (i, 0)), pl.BlockSpec((TM, LANES), lambda i: (i, 0))],
        out_shape=[jax.ShapeDtypeStruct((n, wz), F32), jax.ShapeDtypeStruct((n, LANES), F32)],
        compiler_params=_params(("arbitrary",)),
        name="in_proj_mlstm",
    )(hn, w_main, w_gate, b_gate)
```

```python
import functools

import jax
import jax.numpy as jnp
import numpy as np
from jax import lax
from jax.experimental import pallas as pl
from jax.experimental.pallas import tpu as pltpu

F32 = jnp.float32
BF16 = jnp.bfloat16
I32 = jnp.int32

EPS = 1e-6
GLA_HEADS = 4
GLA_RANK = 16
GLA_TAU = 16.0
MLSTM_HEADS = 8
GATE_CAP = 15.0
CHUNK = 64
N_EXPERTS = 8
NEG_BIG = -1e30

LANES = 128
TM = 512
TM_EXPERT = 256
SAMPLE_PAD_T = 8
SAMPLE_SEQS_PER_STEP = 16
SWIGLU_CHUNKS = 11
VMEM_LIMIT = 56 * 1024 * 1024


def _dot(a, b):
    return jnp.dot(a, b, preferred_element_type=F32)


def _dot_nt(a, b):
    return lax.dot_general(a, b, (((1,), (1,)), ((), ())), preferred_element_type=F32)


def _dot_tn(a, b):
    return lax.dot_general(a, b, (((0,), (0,)), ((), ())), preferred_element_type=F32)


def _sigmoid(x):
    return 1.0 / (1.0 + jnp.exp(-x))


def _log_sigmoid(x):
    return jnp.minimum(x, 0.0) - jnp.log1p(jnp.exp(-jnp.abs(x)))


def _rms(x, g):
    return x * lax.rsqrt(jnp.mean(x * x, axis=-1, keepdims=True) + EPS) * g


def _cumsum_rows(x, n):
    ridx = lax.broadcasted_iota(I32, x.shape, 0)
    s = 1
    while s < n:
        x = x + jnp.where(ridx >= s, pltpu.roll(x, s, 0), 0.0)
        s *= 2
    return x


def _params(sem):
    return pltpu.CompilerParams(dimension_semantics=sem, vmem_limit_bytes=VMEM_LIMIT)


def _resident_spec(block_shape, index_map):
    return pl.BlockSpec(block_shape, index_map, pipeline_mode=pl.Buffered(1))


def _cast_side_job(src, n_steps):
    rows, cols = src.shape
    n_slabs = n_steps - 1
    assert rows % (16 * n_slabs) == 0
    spec = pl.BlockSpec((rows // n_slabs, cols), lambda i, *_: (jnp.minimum(i, n_slabs - 1), 0))
    return spec, spec, jax.ShapeDtypeStruct(src.shape, BF16)


IN_PROJ_COL_CHUNKS = 4


def _pick_rows(i, n_first, first_ref, second_ref):
    return jnp.where(i < n_first, first_ref[...], second_ref[...])


def _split_specs(n_first, width):
    return [pl.BlockSpec((TM, width), lambda i, *_: (jnp.minimum(i, n_first - 1), 0)),
            pl.BlockSpec((TM, width), lambda i, *_: (jnp.maximum(i - n_first, 0), 0))]


def _gla_in_proj_kernel(xp_ref, xs_ref, g_ref, w_ref, wg_ref, w2_ref, b_ref, z_ref, gate_ref, *,
                        n_first):
    x = _pick_rows(pl.program_id(0), n_first, xp_ref, xs_ref)
    hn = _rms(x, g_ref[...]).astype(BF16)
    width = w_ref.shape[1] // IN_PROJ_COL_CHUNKS
    for c in range(IN_PROJ_COL_CHUNKS):
        cs = slice(c * width, (c + 1) * width)
        z_ref[:, cs] = _dot(hn, w_ref[:, cs])
    a = _dot(hn, wg_ref[...])
    la = _dot(a.astype(BF16), w2_ref[...]) + b_ref[...]
    gate_ref[...] = _log_sigmoid(la) * (1.0 / GLA_TAU)


def _gla_in_proj(x_p, x_s, g, w_main, w_gate, w2, b):
    d = x_p.shape[1]
    n = x_p.shape[0] + x_s.shape[0]
    n_first = x_p.shape[0] // TM
    wz = w_main.shape[1]
    gw = b.shape[1]
    return pl.pallas_call(
        functools.partial(_gla_in_proj_kernel, n_first=n_first),
        grid=(n // TM,),
        in_specs=_split_specs(n_first, d) + [
            pl.BlockSpec((1, d), lambda i: (0, 0)),
            pl.BlockSpec((d, wz), lambda i: (0, 0)),
            pl.BlockSpec((d, LANES), lambda i: (0, 0)),
            pl.BlockSpec(w2.shape, lambda i: (0, 0)),
            pl.BlockSpec((1, gw), lambda i: (0, 0)),
        ],
        out_specs=[
            pl.BlockSpec((TM, wz), lambda i: (i, 0)),
            pl.BlockSpec((TM, gw), lambda i: (i, 0)),
        ],
        out_shape=[jax.ShapeDtypeStruct((n, wz), F32), jax.ShapeDtypeStruct((n, gw), F32)],
        compiler_params=_params(("arbitrary",)),
        name="in_proj_gla",
    )(x_p, x_s, g, w_main, w_gate, w2, b)


def _gla_kernel(q_ref, k_ref, v_ref, r_ref, la_ref, s0_ref, gn_ref, u_ref, st_ref, st_sc,
                *, chunk, tb, nblk):
    heads = GLA_HEADS
    dk = q_ref.shape[1] // heads
    dv = v_ref.shape[1] // heads
    c = pl.program_id(1)

    @pl.when(c == 0)
    def _():
        for h in range(heads):
            st_sc[h] = s0_ref[0, h].T

    ri = lax.broadcasted_iota(I32, (chunk, chunk), 0)
    ci = lax.broadcasted_iota(I32, (chunk, chunk), 1)
    causal = ri >= ci
    kscale = dk ** -0.5
    for j in range(tb // chunk):
        rows = slice(j * chunk, (j + 1) * chunk)
        b_all = _cumsum_rows(la_ref[rows, :], chunk)
        for h in range(heads):
            ks = slice(h * dk, (h + 1) * dk)
            vs = slice(h * dv, (h + 1) * dv)
            b = b_all[:, ks]
            b_end = b[chunk - 1:chunk, :]
            q = q_ref[rows, ks]
            k = k_ref[rows, ks] * kscale
            qg = (q * jnp.exp(b)).astype(BF16)
            kg = (k * jnp.exp(-b)).astype(BF16)
            ke = (k * jnp.exp(b_end - b)).astype(BF16)
            v = v_ref[rows, vs].astype(BF16)
            a = jnp.where(causal, _dot_nt(qg, kg), 0.0).astype(BF16)
            st = st_sc[h]
            o = _dot_nt(qg, st.astype(BF16)) + _dot(a, v)
            st_sc[h] = st * jnp.exp(b_end) + _dot_tn(v, ke)
            on = _rms(o, gn_ref[h:h + 1, :])
            r = r_ref[rows, vs]
            u_ref[rows, vs] = (r * _sigmoid(r) * on).astype(BF16)

    @pl.when(c == nblk - 1)
    def _():
        for h in range(heads):
            st_ref[0, h] = st_sc[h].T


def _gla_recurrence(z, log_a, s0, g_norm, nseq, t, chunk, tb):
    qk_w = log_a.shape[1]
    v_w = 2 * qk_w
    nblk = t // tb
    heads, dk, dv = s0.shape[1:]
    kern = functools.partial(_gla_kernel, chunk=chunk, tb=tb, nblk=nblk)
    row = lambda b, c: b * nblk + c
    return pl.pallas_call(
        kern,
        grid=(nseq, nblk),
        in_specs=[
            pl.BlockSpec((tb, qk_w), lambda b, c: (row(b, c), 0)),
            pl.BlockSpec((tb, qk_w), lambda b, c: (row(b, c), 1)),
            pl.BlockSpec((tb, v_w), lambda b, c: (row(b, c), 1)),
            pl.BlockSpec((tb, v_w), lambda b, c: (row(b, c), 2)),
            pl.BlockSpec((tb, qk_w), lambda b, c: (row(b, c), 0)),
            pl.BlockSpec((1, heads, dk, dv), lambda b, c: (b, 0, 0, 0)),
            pl.BlockSpec((heads, dv), lambda b, c: (0, 0)),
        ],
        out_specs=[
            pl.BlockSpec((tb, v_w), lambda b, c: (row(b, c), 0)),
            pl.BlockSpec((1, heads, dk, dv), lambda b, c: (b, 0, 0, 0)),
        ],
        out_shape=[
            jax.ShapeDtypeStruct((nseq * t, v_w), BF16),
            jax.ShapeDtypeStruct(s0.shape, F32),
        ],
        scratch_shapes=[pltpu.VMEM((heads, dv, dk), F32)],
        compiler_params=_params(("arbitrary", "arbitrary")),
        name="gla_recurrence_t%d" % t,
    )(z, z, z, z, log_a, s0, g_norm)


def _gla_short_kernel(q_ref, k_ref, v_ref, r_ref, la_ref, s0_ref, gn_ref, u_ref, st_ref, *, t, nsq):
    heads = GLA_HEADS
    dk = q_ref.shape[1] // heads
    dv = v_ref.shape[1] // heads
    nrow = nsq * t
    width = q_ref.shape[1]
    rc = lax.broadcasted_iota(I32, (nrow, width), 0) & (t - 1)
    b = _chunk_scan(la_ref[...], rc, t, jnp.add, 0.0)
    b_last = b.reshape(nsq, t, width)[:, t - 1:t, :]
    b_end = jnp.broadcast_to(b_last, (nsq, t, width)).reshape(nrow, width)
    k = k_ref[...] * (dk ** -0.5)
    qg = q_ref[...] * jnp.exp(b)
    kg = k * jnp.exp(-b)
    ke = k * jnp.exp(b_end - b)
    decay = jnp.exp(b_end)
    ri = lax.broadcasted_iota(I32, (nrow, nrow), 0)
    ci = lax.broadcasted_iota(I32, (nrow, nrow), 1)
    shift = t.bit_length() - 1
    mask = ((ri >> shift) == (ci >> shift)) & (ri >= ci)
    for h in range(heads):
        ks = slice(h * dk, (h + 1) * dk)
        vs = slice(h * dv, (h + 1) * dv)
        a = jnp.where(mask, _dot_nt(qg[:, ks].astype(BF16), kg[:, ks].astype(BF16)), 0.0)
        o = _dot(a.astype(BF16), v_ref[:, vs].astype(BF16))
        decay_t = decay[:, ks].T
        o_state = []
        for sq in range(nsq):
            r = slice(sq * t, (sq + 1) * t)
            s0 = s0_ref[sq, h]
            o_state.append(_dot(qg[r, ks].astype(BF16), s0.astype(BF16)))
            upd = _dot_tn(ke[r, ks].astype(BF16), v_ref[r, vs].astype(BF16))
            st_ref[sq, h] = decay_t[:, sq * t:sq * t + 1] * s0 + upd
        on = _rms(o + jnp.concatenate(o_state, axis=0), gn_ref[h:h + 1, :])
        rg = r_ref[:, vs]
        u_ref[:, vs] = (rg * _sigmoid(rg) * on).astype(BF16)


def _gla_recurrence_short(z, log_a, s0, g_norm, nseq, t, nsq):
    qk_w = log_a.shape[1]
    v_w = 2 * qk_w
    heads, dk, dv = s0.shape[1:]
    assert nsq * t == dk == LANES
    rb = nsq * t
    return pl.pallas_call(
        functools.partial(_gla_short_kernel, t=t, nsq=nsq),
        grid=(nseq // nsq,),
        in_specs=[
            pl.BlockSpec((rb, qk_w), lambda b: (b, 0)),
            pl.BlockSpec((rb, qk_w), lambda b: (b, 1)),
            pl.BlockSpec((rb, v_w), lambda b: (b, 1)),
            pl.BlockSpec((rb, v_w), lambda b: (b, 2)),
            pl.BlockSpec((rb, qk_w), lambda b: (b, 0)),
            pl.BlockSpec((nsq, heads, dk, dv), lambda b: (b, 0, 0, 0)),
            pl.BlockSpec((heads, dv), lambda b: (0, 0)),
        ],
        out_specs=[
            pl.BlockSpec((rb, v_w), lambda b: (b, 0)),
            pl.BlockSpec((nsq, heads, dk, dv), lambda b: (b, 0, 0, 0)),
        ],
        out_shape=[
            jax.ShapeDtypeStruct((nseq * t, v_w), BF16),
            jax.ShapeDtypeStruct(s0.shape, F32),
        ],
        compiler_params=_params(("arbitrary",)),
        name="gla_recurrence_t%d" % t,
    )(z, z, z, z, log_a, s0, g_norm)


def _mlstm_short_kernel(q_ref, k_ref, v_ref, op_ref, gt_ref, e_ref, c0_ref, n0_ref, m0_ref, gn_ref,
                        u_ref, ct_ref, nt_ref, mt_ref, *, t, nsq):
    heads = MLSTM_HEADS
    pairs = heads // 2
    dv = v_ref.shape[1] // heads
    dk = q_ref.shape[1] // heads
    nrow = nsq * t

    def per_seq(x3):
        return jnp.broadcast_to(x3, (nsq, t, x3.shape[2])).reshape(nrow, x3.shape[2])

    def last(x):
        return x.reshape(nsq, t, x.shape[1])[:, t - 1:t, :]

    rc = lax.broadcasted_iota(I32, (nrow, LANES), 0) & (t - 1)
    gts = gt_ref[...]
    f_cum = pltpu.roll(_chunk_scan(gts, rc, t, jnp.add, 0.0), LANES - heads, 1)
    b = gts - f_cum
    cmb = _chunk_scan(b, rc, t, jnp.maximum, NEG_BIG)
    mp = per_seq(m0_ref[...])
    big_m = jnp.maximum(mp, cmb)
    m_t = f_cum + big_m
    m_new = last(m_t)
    mt_ref[...] = m_new
    mn = per_seq(m_new)
    fe = per_seq(last(f_cum))
    w_i = jnp.exp(mp - big_m)
    em = jnp.exp(-m_t)
    k_sc = jnp.exp(fe - f_cum + gts - mn)
    w_c = jnp.exp(fe + mp - mn)
    ex = _dot(_pack_terms([w_i, k_sc, -big_m, em, w_c], heads), e_ref[...])
    offs = np.cumsum([0] + [heads * w for w in SHORT_EXPAND_WIDTHS])
    q = q_ref[...]
    k = k_ref[...] * (dk ** -0.5)
    qs = q * ex[:, offs[0]:offs[1]]
    ke = k * ex[:, offs[1]:offs[2]]
    neg_m = ex[:, offs[2]:offs[3]]
    em_v = ex[:, offs[3]:offs[4]]
    wc_k = ex[:, offs[4]:offs[5]]
    bt = b.T

    ri = lax.broadcasted_iota(I32, (nrow, nrow), 0)
    ci = lax.broadcasted_iota(I32, (nrow, nrow), 1)
    shift = t.bit_length() - 1
    mask = ((ri >> shift) == (ci >> shift)) & (ri >= ci)
    lane1 = lax.broadcasted_iota(I32, (1, LANES), 1)
    rowi = lax.broadcasted_iota(I32, (LANES, LANES), 0)
    ones_v = jnp.ones((nrow, dv), BF16)
    ones_k = jnp.ones((LANES, dv), BF16)
    for p in range(pairs):
        ps = slice(p * LANES, (p + 1) * LANES)
        qp = q[:, ps]
        qsp = qs[:, ps]
        kpb = k[:, ps].astype(BF16)
        n_rows = per_seq(n0_ref[:, p:p + 1, :])
        intra, den_state, qsh = [], [], []
        for jj in range(2):
            h = 2 * p + jj
            hs = slice(h * dv, (h + 1) * dv)
            mine = (lane1 >= jj * dk) & (lane1 < (jj + 1) * dk)
            qh = jnp.where(mine, qp, 0.0).astype(BF16)
            qsh.append(jnp.where(mine, qsp, 0.0))
            d = neg_m[:, hs] + bt[h:h + 1, :]
            s = _dot_nt(qh, kpb) * jnp.where(mask, jnp.exp(d), 0.0)
            vh = jnp.concatenate([v_ref[:, hs].astype(BF16), ones_v], axis=1)
            intra.append(_dot(s.astype(BF16), vh))
            den_state.append(_dot((qsh[jj] * n_rows).astype(BF16), ones_k))
        num_state = [[], []]
        for sq in range(nsq):
            r = slice(sq * t, (sq + 1) * t)
            lhs = jnp.concatenate([qsh[0][r], qsh[1][r]], axis=0).astype(BF16)
            res = _dot(lhs, c0_ref[sq, p].astype(BF16))
            num_state[0].append(res[:t])
            num_state[1].append(res[t:])
        for jj in range(2):
            h = 2 * p + jj
            hs = slice(h * dv, (h + 1) * dv)
            num = intra[jj][:, :dv] + jnp.concatenate(num_state[jj], axis=0)
            den = intra[jj][:, dv:] + den_state[jj]
            hh = num / jnp.maximum(jnp.abs(den), em_v[:, hs])
            hn = _rms(hh, gn_ref[h:h + 1, :])
            u_ref[:, hs] = (_sigmoid(op_ref[:, hs]) * hn).astype(BF16)
        kep = ke[:, ps]
        for sq in range(nsq):
            r = slice(sq * t, (sq + 1) * t)
            vp = v_ref[r, 2 * p * dv:(2 * p + 2) * dv].astype(BF16)
            full = _dot_tn(kep[r].astype(BF16), vp)
            upd = jnp.where(rowi < dk, full[:, :dv], full[:, dv:])
            w_row = w_c[sq * t:sq * t + 1, :]
            w_col = jnp.where(rowi < dk, w_row[:, 2 * p:2 * p + 1], w_row[:, 2 * p + 1:2 * p + 2])
            ct_ref[sq, p] = w_col * c0_ref[sq, p] + upd
        k_sum = jnp.sum(kep.reshape(nsq, t, LANES), axis=1, keepdims=True)
        nt_ref[:, p:p + 1, :] = last(wc_k[:, ps]) * n0_ref[:, p:p + 1, :] + k_sum


def _mlstm_recurrence_short(z, gates, c0, n0, m0, g_norm, nseq, t, nsq):
    heads = g_norm.shape[0]
    v_w = heads * g_norm.shape[1]
    qk_w = v_w // 2
    assert nsq * t == LANES
    expand = _expand_matrix(heads, SHORT_EXPAND_WIDTHS)
    kern = functools.partial(_mlstm_short_kernel, t=t, nsq=nsq)
    rb = nsq * t
    st4 = lambda b: (b, 0, 0, 0)
    st3 = lambda b: (b, 0, 0)
    return pl.pallas_call(
        kern,
        grid=(nseq // nsq,),
        in_specs=[
            pl.BlockSpec((rb, qk_w), lambda b: (b, 0)),
            pl.BlockSpec((rb, qk_w), lambda b: (b, 1)),
            pl.BlockSpec((rb, v_w), lambda b: (b, 1)),
            pl.BlockSpec((rb, v_w), lambda b: (b, 2)),
            pl.BlockSpec((rb, LANES), lambda b: (b, 0)),
            pl.BlockSpec(expand.shape, lambda b: (0, 0)),
            pl.BlockSpec((nsq,) + c0.shape[1:], st4),
            pl.BlockSpec((nsq,) + n0.shape[1:], st3),
            pl.BlockSpec((nsq,) + m0.shape[1:], st3),
            pl.BlockSpec(g_norm.shape, lambda b: (0, 0)),
        ],
        out_specs=[
            pl.BlockSpec((rb, v_w), lambda b: (b, 0)),
            pl.BlockSpec((nsq,) + c0.shape[1:], st4),
            pl.BlockSpec((nsq,) + n0.shape[1:], st3),
            pl.BlockSpec((nsq,) + m0.shape[1:], st3),
        ],
        out_shape=[
            jax.ShapeDtypeStruct((nseq * t, v_w), BF16),
            jax.ShapeDtypeStruct(c0.shape, F32),
            jax.ShapeDtypeStruct(n0.shape, F32),
            jax.ShapeDtypeStruct(m0.shape, F32),
        ],
        compiler_params=_params(("arbitrary",)),
        name="mlstm_recurrence_t%d" % t,
    )(z, z, z, z, gates, expand, c0, n0, m0, g_norm)


N_SPLIT = 3
EXPAND_WIDTHS = (64, 64, 128, 128)
SHORT_EXPAND_WIDTHS = EXPAND_WIDTHS + (64,)


def _expand_matrix(heads, widths):
    assert len(widths) * N_SPLIT * heads <= LANES
    cols = []
    for qi, width in enumerate(widths):
        sel = np.zeros((LANES, heads * width), np.float32)
        for term in range(N_SPLIT):
            for h in range(heads):
                sel[(qi * N_SPLIT + term) * heads + h, h * width:(h + 1) * width] = 1.0
        cols.append(sel)
    return jnp.asarray(np.concatenate(cols, axis=1), BF16)


def _chunk_scan(x, rc, n, op, fill):
    s = 1
    while s < n:
        x = op(x, jnp.where(rc >= s, pltpu.roll(x, s, 0), fill))
        s *= 2
    return x


def _pack_terms(values, heads):
    lane = lax.broadcasted_iota(I32, values[0].shape, 1)
    packed = jnp.zeros(values[0].shape, F32)
    slot = 0
    for val in values:
        rest = val
        for term in range(N_SPLIT):
            part = rest.astype(BF16).astype(F32)
            rest = rest - part
            moved = part if slot == 0 else pltpu.roll(part, slot * heads, 1)
            packed = jnp.where((lane >= slot * heads) & (lane < (slot + 1) * heads), moved, packed)
            slot += 1
    return packed.astype(BF16)


def _mlstm_block_kernel(q_ref, k_ref, v_ref, op_ref, gt_ref, e_ref, c0_ref, n0_ref, m0_ref, gn_ref,
                        u_ref, ct_ref, nt_ref, mt_ref, c_sc, nb_sc, m_sc, *, chunk, tb, nblk):
    heads = MLSTM_HEADS
    pairs = heads // 2
    dv = v_ref.shape[1] // heads
    dk = q_ref.shape[1] // heads
    c = pl.program_id(1)

    @pl.when(c == 0)
    def _():
        c_sc[...] = c0_ref[0]
        for p in range(pairs):
            nb_sc[p] = jnp.broadcast_to(n0_ref[0, p:p + 1, :], (LANES, LANES)).T
        m_sc[...] = m0_ref[0]

    nch = tb // chunk
    rc = lax.broadcasted_iota(I32, (tb, LANES), 0) & (chunk - 1)
    gts = gt_ref[...]
    f_cum = pltpu.roll(_chunk_scan(gts, rc, chunk, jnp.add, 0.0), LANES - heads, 1)
    b = gts - f_cum
    cmb = _chunk_scan(b, rc, chunk, jnp.maximum, NEG_BIG)
    m_prev = m_sc[...]
    mp_rows, mn_rows, fe_rows, w_c = [], [], [], []
    for j in range(nch):
        last = (j + 1) * chunk - 1
        f_end = f_cum[last:last + 1, :]
        m_new = f_end + jnp.maximum(m_prev, cmb[last:last + 1, :])
        w_c.append(jnp.exp(f_end + m_prev - m_new))
        mp_rows.append(jnp.broadcast_to(m_prev, (chunk, LANES)))
        mn_rows.append(jnp.broadcast_to(m_new, (chunk, LANES)))
        fe_rows.append(jnp.broadcast_to(f_end, (chunk, LANES)))
        m_prev = m_new
    m_sc[...] = m_prev
    mp = jnp.concatenate(mp_rows, axis=0)
    mn = jnp.concatenate(mn_rows, axis=0)
    fe = jnp.concatenate(fe_rows, axis=0)
    big_m = jnp.maximum(mp, cmb)
    w_i = jnp.exp(mp - big_m)
    em = jnp.exp(-(f_cum + big_m))
    k_sc = jnp.exp(fe - f_cum + gts - mn)
    ex = _dot(_pack_terms([w_i, k_sc, -big_m, em], heads), e_ref[...])
    o1 = heads * EXPAND_WIDTHS[0]
    o2 = o1 + heads * EXPAND_WIDTHS[1]
    o3 = o2 + heads * EXPAND_WIDTHS[2]
    q = q_ref[...]
    k = k_ref[...] * (dk ** -0.5)
    qs = q * ex[:, :o1]
    ke = k * ex[:, o1:o2]
    neg_m = ex[:, o2:o3]
    em_v = ex[:, o3:]
    bt = b.T

    ri = lax.broadcasted_iota(I32, (chunk, chunk), 0)
    ci = lax.broadcasted_iota(I32, (chunk, chunk), 1)
    causal = ri >= ci
    lane1 = lax.broadcasted_iota(I32, (1, LANES), 1)
    rowi = lax.broadcasted_iota(I32, (LANES, LANES), 0)
    ones_v = jnp.ones((chunk, dv), BF16)
    for j in range(nch):
        rows = slice(j * chunk, (j + 1) * chunk)
        for p in range(pairs):
            ps = slice(p * LANES, (p + 1) * LANES)
            qp = q[rows, ps]
            qsp = qs[rows, ps]
            kpb = k[rows, ps].astype(BF16)
            cp = c_sc[p]
            nb = nb_sc[p]
            state = jnp.concatenate([cp, nb], axis=1).astype(BF16)
            for jj in range(2):
                h = 2 * p + jj
                hs = slice(h * dv, (h + 1) * dv)
                mine = (lane1 >= jj * dk) & (lane1 < (jj + 1) * dk)
                qh = jnp.where(mine, qp, 0.0).astype(BF16)
                qsh = jnp.where(mine, qsp, 0.0).astype(BF16)
                d = neg_m[rows, h * dv:h * dv + chunk] + bt[h:h + 1, j * chunk:(j + 1) * chunk]
                s = _dot_nt(qh, kpb) * jnp.where(causal, jnp.exp(d), 0.0)
                vh = jnp.concatenate([v_ref[rows, hs].astype(BF16), ones_v], axis=1)
                out = _dot(qsh, state) + _dot(s.astype(BF16), vh)
                hh = out[:, :dv] / jnp.maximum(jnp.abs(out[:, dv:]), em_v[rows, hs])
                hn = _rms(hh, gn_ref[h:h + 1, :])
                u_ref[rows, hs] = (_sigmoid(op_ref[rows, hs]) * hn).astype(BF16)
            vp = jnp.concatenate([v_ref[rows, 2 * p * dv:(2 * p + 2) * dv].astype(BF16), ones_v], axis=1)
            full = _dot_tn(ke[rows, ps].astype(BF16), vp)
            upd = jnp.where(rowi < dk, full[:, :dv], full[:, dv:2 * dv])
            w_col = jnp.where(rowi < dk, w_c[j][:, 2 * p:2 * p + 1], w_c[j][:, 2 * p + 1:2 * p + 2])
            c_sc[p] = w_col * cp + upd
            nb_sc[p] = w_col * nb + full[:, 2 * dv:]

    @pl.when(c == nblk - 1)
    def _():
        ct_ref[0] = c_sc[...]
        for p in range(pairs):
            nt_ref[0, p:p + 1, :] = nb_sc[p].T[0:1, :]
        mt_ref[0] = m_sc[...]


def _mlstm_recurrence_blocked(z, gates, c0, n0, m0, g_norm, nseq, t, chunk, tb):
    heads = g_norm.shape[0]
    v_w = heads * g_norm.shape[1]
    qk_w = v_w // 2
    nblk = t // tb
    expand = _expand_matrix(heads, EXPAND_WIDTHS)
    kern = functools.partial(_mlstm_block_kernel, chunk=chunk, tb=tb, nblk=nblk)
    row = lambda b, c: b * nblk + c
    st4 = lambda b, c: (b, 0, 0, 0)
    st3 = lambda b, c: (b, 0, 0)
    return pl.pallas_call(
        kern,
        grid=(nseq, nblk),
        in_specs=[
            pl.BlockSpec((tb, qk_w), lambda b, c: (row(b, c), 0)),
            pl.BlockSpec((tb, qk_w), lambda b, c: (row(b, c), 1)),
            pl.BlockSpec((tb, v_w), lambda b, c: (row(b, c), 1)),
            pl.BlockSpec((tb, v_w), lambda b, c: (row(b, c), 2)),
            pl.BlockSpec((tb, LANES), lambda b, c: (row(b, c), 0)),
            pl.BlockSpec(expand.shape, lambda b, c: (0, 0)),
            pl.BlockSpec((1,) + c0.shape[1:], st4),
            pl.BlockSpec((1,) + n0.shape[1:], st3),
            pl.BlockSpec((1,) + m0.shape[1:], st3),
            pl.BlockSpec(g_norm.shape, lambda b, c: (0, 0)),
        ],
        out_specs=[
            pl.BlockSpec((tb, v_w), lambda b, c: (row(b, c), 0)),
            pl.BlockSpec((1,) + c0.shape[1:], st4),
            pl.BlockSpec((1,) + n0.shape[1:], st3),
            pl.BlockSpec((1,) + m0.shape[1:], st3),
        ],
        out_shape=[
            jax.ShapeDtypeStruct((nseq * t, v_w), BF16),
            jax.ShapeDtypeStruct(c0.shape, F32),
            jax.ShapeDtypeStruct(n0.shape, F32),
            jax.ShapeDtypeStruct(m0.shape, F32),
        ],
        scratch_shapes=[
            pltpu.VMEM(c0.shape[1:], F32),
            pltpu.VMEM(c0.shape[1:], F32),
            pltpu.VMEM(m0.shape[1:], F32),
        ],
        compiler_params=_params(("arbitrary", "arbitrary")),
        name="mlstm_recurrence_t%d" % t,
    )(z, z, z, z, gates, expand, c0, n0, m0, g_norm)


def _out_proj_kernel(up_ref, us_ref, xp_ref, xs_ref, w_ref, g_ref, xo_ref, hn_ref, *, n_first):
    i = pl.program_id(0)
    u = _pick_rows(i, n_first, up_ref, us_ref)
    x = _pick_rows(i, n_first, xp_ref, xs_ref) + _dot(u, w_ref[...])
    xo_ref[...] = x
    hn_ref[...] = _rms(x, g_ref[...]).astype(hn_ref.dtype)


def _out_proj(u_p, u_s, x_p, x_s, w, g):
    d = x_p.shape[1]
    n = x_p.shape[0] + x_s.shape[0]
    n_first = x_p.shape[0] // TM
    return pl.pallas_call(
        functools.partial(_out_proj_kernel, n_first=n_first),
        grid=(n // TM,),
        in_specs=_split_specs(n_first, u_p.shape[1]) + _split_specs(n_first, d) + [
            pl.BlockSpec(w.shape, lambda i: (0, 0)),
            pl.BlockSpec((1, d), lambda i: (0, 0)),
        ],
        out_specs=[pl.BlockSpec((TM, d), lambda i: (i, 0)), pl.BlockSpec((TM, d), lambda i: (i, 0))],
        out_shape=[jax.ShapeDtypeStruct((n, d), F32), jax.ShapeDtypeStruct((n, d), BF16)],
        compiler_params=_params(("arbitrary",)),
        name="out_proj",
    )(u_p, u_s, x_p, x_s, w, g)


def _out_proj_router_kernel(up_ref, us_ref, x_ref, w_ref, g_ref, wr_ref, br_ref,
                            xo_ref, hn_ref, gate_ref, idx_ref, cnt_ref, cnt_sc, *, n_first):
    i = pl.program_id(0)
    u = _pick_rows(i, n_first, up_ref, us_ref)

    @pl.when(i == 0)
    def _():
        cnt_sc[...] = jnp.zeros_like(cnt_sc)

    x = x_ref[...] + _dot(u, w_ref[...])
    xo_ref[...] = x
    hn = _rms(x, g_ref[...])
    hn_ref[...] = hn
    hn_hi = hn.astype(BF16)
    hn_lo = (hn - hn_hi.astype(F32)).astype(BF16)
    logits = (_dot(hn_hi, wr_ref[0]) + _dot(hn_lo, wr_ref[0]) + _dot(hn_hi, wr_ref[1])
              + br_ref[...])
    tm = logits.shape[0]
    lane = lax.broadcasted_iota(I32, logits.shape, 1)
    valid = lane < N_EXPERTS
    logits = jnp.where(valid, logits, -jnp.inf)
    ex = jnp.exp(logits - jnp.max(logits, axis=-1, keepdims=True))
    probs = jnp.where(valid, ex / jnp.sum(ex, axis=-1, keepdims=True), -1.0)
    p1 = jnp.max(probs, axis=-1, keepdims=True)
    i1 = jnp.min(jnp.where(probs == p1, lane, LANES), axis=-1, keepdims=True)
    rest = jnp.where(lane == i1, -1.0, probs)
    p2 = jnp.max(rest, axis=-1, keepdims=True)
    i2 = jnp.min(jnp.where(rest == p2, lane, LANES), axis=-1, keepdims=True)
    tot = p1 + p2
    onehot = ((lane == i1) | (lane == i2)).astype(BF16)
    rr = lax.broadcasted_iota(I32, (tm, tm), 0)
    cc = lax.broadcasted_iota(I32, (tm, tm), 1)
    strict = (rr > cc).astype(BF16)
    before = _dot(strict, onehot) + cnt_sc[...]
    r1 = jnp.sum(jnp.where(lane == i1, before, 0.0), axis=-1, keepdims=True)
    r2 = jnp.sum(jnp.where(lane == i2, before, 0.0), axis=-1, keepdims=True)
    cnt_sc[...] = cnt_sc[...] + jnp.sum(onehot.astype(F32), axis=0, keepdims=True)
    gate_ref[...] = jnp.where(lane == 0, p1 / tot, jnp.where(lane == 1, p2 / tot, 0.0))
    idx_ref[...] = jnp.where(lane == 0, i1,
                             jnp.where(lane == 1, i2,
                                       jnp.where(lane == 2, r1.astype(I32),
                                                 jnp.where(lane == 3, r2.astype(I32), 0))))
    cnt_ref[...] = cnt_sc[...]


def _out_proj_router(u_p, u_s, x, w, g, w_router, b_router):
    n, d = x.shape
    n_first = u_p.shape[0] // TM
    row = lambda i: (i, 0)
    fix = lambda i: (0, 0)
    return pl.pallas_call(
        functools.partial(_out_proj_router_kernel, n_first=n_first),
        grid=(n // TM,),
        in_specs=_split_specs(n_first, u_p.shape[1]) + [
            pl.BlockSpec((TM, d), row),
            pl.BlockSpec(w.shape, fix),
            pl.BlockSpec((1, d), fix),
            pl.BlockSpec(w_router.shape, lambda i: (0, 0, 0)),
            pl.BlockSpec((1, LANES), fix),
        ],
        out_specs=[
            pl.BlockSpec((TM, d), row),
            pl.BlockSpec((TM, d), row),
            pl.BlockSpec((TM, LANES), row),
            pl.BlockSpec((TM, LANES), row),
            pl.BlockSpec((1, LANES), fix),
        ],
        out_shape=[
            jax.ShapeDtypeStruct((n, d), F32),
            jax.ShapeDtypeStruct((n, d), F32),
            jax.ShapeDtypeStruct((n, LANES), F32),
            jax.ShapeDtypeStruct((n, LANES), I32),
            jax.ShapeDtypeStruct((1, LANES), F32),
        ],
        scratch_shapes=[pltpu.VMEM((1, LANES), F32)],
        compiler_params=_params(("arbitrary",)),
        name="out_proj_router",
    )(u_p, u_s, x, w, g, w_router, b_router)


def _swiglu_kernel(hn_ref, x_ref, wg_ref, wu_ref, wd_ref, g_ref, cast_ref, xo_ref, ho_ref, cast_out_ref,
                   *, nchunk):
    cast_out_ref[...] = cast_ref[...].astype(BF16)
    hn = hn_ref[...]
    width = wd_ref.shape[0] // nchunk
    x = x_ref[...]
    for c in range(nchunk):
        cs = slice(c * width, (c + 1) * width)
        gate = _dot(hn, wg_ref[:, cs])
        up = _dot(hn, wu_ref[:, cs])
        act = (gate * _sigmoid(gate) * up).astype(BF16)
        x = x + _dot(act, wd_ref[cs, :])
    xo_ref[...] = x
    ho_ref[...] = _rms(x, g_ref[...]).astype(BF16)


def _swiglu(hn, x, w_gu, w_down, g, nchunk, cast_src):
    n, d = x.shape
    dff = w_down.shape[0]
    kern = functools.partial(_swiglu_kernel, nchunk=nchunk)
    row = lambda i: (i, 0)
    cast_in, cast_out, cast_shape = _cast_side_job(cast_src, n // TM)
    return pl.pallas_call(
        kern,
        grid=(n // TM,),
        in_specs=[
            pl.BlockSpec((TM, d), row),
            pl.BlockSpec((TM, d), row),
            _resident_spec((d, dff), lambda i: (0, 0)),
            _resident_spec((d, dff), lambda i: (0, 1)),
            _resident_spec((dff, d), lambda i: (0, 0)),
            pl.BlockSpec((1, d), lambda i: (0, 0)),
            cast_in,
        ],
        out_specs=[pl.BlockSpec((TM, d), row), pl.BlockSpec((TM, d), row), cast_out],
        out_shape=[jax.ShapeDtypeStruct((n, d), F32), jax.ShapeDtypeStruct((n, d), BF16), cast_shape],
        compiler_params=_params(("arbitrary",)),
        name="dense_swiglu",
    )(hn, x, w_gu, w_gu, w_down, g, cast_src)


DMA_UNROLL = 8


def _row_copy(src_ref, src_row, dst_ref, dst_row, sem):
    return pltpu.make_async_copy(src_ref.at[pl.ds(src_row, 1), :], dst_ref.at[pl.ds(dst_row, 1), :], sem)


def _dispatch_kernel(dest_ref, pad_ref, hn_ref, xs_ref, zero_sc, sem):
    i = pl.program_id(0)
    tm = hn_ref.shape[0]

    @pl.when(i == 0)
    def _():
        zero_sc[...] = jnp.zeros_like(zero_sc)
        for e in range(N_EXPERTS):
            base = pad_ref[2 * e]
            count = pad_ref[2 * e + 1]

            def issue_pad(j, carry, base=base):
                _row_copy(zero_sc, 0, xs_ref, base + j, sem).start()
                return carry

            def wait_pad(j, carry):
                _row_copy(zero_sc, 0, xs_ref, 0, sem).wait()
                return carry

            lax.fori_loop(0, count, issue_pad, 0)
            lax.fori_loop(0, count, wait_pad, 0)

        rows = zero_sc.shape[0]

        def tail_copy(j):
            first = (pad_ref[2 * N_EXPERTS] + j) * rows
            return pltpu.make_async_copy(zero_sc, xs_ref.at[pl.ds(first, rows), :], sem)

        def issue_tail(j, carry):
            tail_copy(j).start()
            return carry

        def wait_tail(j, carry):
            tail_copy(j).wait()
            return carry

        lax.fori_loop(0, pad_ref[2 * N_EXPERTS + 1], issue_tail, 0)
        lax.fori_loop(0, pad_ref[2 * N_EXPERTS + 1], wait_tail, 0)

    def issue(blk, carry):
        for j in range(DMA_UNROLL):
            r = blk * DMA_UNROLL + j
            t = i * tm + r
            _row_copy(hn_ref, r, xs_ref, dest_ref[2 * t], sem).start(priority=0)
            _row_copy(hn_ref, r, xs_ref, dest_ref[2 * t + 1], sem).start(priority=1)
        return carry

    lax.fori_loop(0, tm // DMA_UNROLL, issue, 0)
    for _ in range(2):
        pltpu.make_async_copy(hn_ref, xs_ref.at[pl.ds(0, tm), :], sem).wait()


def _dispatch(dest_flat, pad_rows, hn, n_rows):
    n, d = hn.shape
    return pl.pallas_call(
        _dispatch_kernel,
        grid_spec=pltpu.PrefetchScalarGridSpec(
            num_scalar_prefetch=2,
            grid=(n // TM,),
            in_specs=[pl.BlockSpec((TM, d), lambda i, dest, pad: (i, 0))],
            out_specs=pl.BlockSpec(memory_space=pl.ANY),
            scratch_shapes=[pltpu.VMEM((TM_EXPERT, d), F32), pltpu.SemaphoreType.DMA(())],
        ),
        out_shape=jax.ShapeDtypeStruct((n_rows, d), F32),
        compiler_params=_params(("arbitrary",)),
        name="moe_dispatch",
    )(dest_flat, pad_rows, hn)


def _expert_kernel(te_ref, tv_ref, xs_ref, wg_ref, wu_ref, wd_ref, ys_ref, *, nchunk):
    i = pl.program_id(0)

    @pl.when(tv_ref[i] > 0)
    def _():
        x = xs_ref[...].astype(BF16)
        width = wg_ref.shape[2] // nchunk
        acc = None
        for c in range(nchunk):
            cs = slice(c * width, (c + 1) * width)
            gate = _dot(x, wg_ref[0, :, cs])
            up = _dot(x, wu_ref[0, :, cs])
            act = (gate * _sigmoid(gate) * up).astype(BF16)
            part = _dot(act, wd_ref[0, cs, :])
            acc = part if acc is None else acc + part
        ys_ref[...] = acc

    @pl.when(tv_ref[i] == 0)
    def _():
        ys_ref[...] = jnp.zeros_like(ys_ref)


def _experts(tile_expert, tile_valid, xs, w_gu, w_down):
    r, d = xs.shape
    n_tiles = r // TM_EXPERT
    dff = w_down.shape[1]
    kern = functools.partial(_expert_kernel, nchunk=7)
    return pl.pallas_call(
        kern,
        grid_spec=pltpu.PrefetchScalarGridSpec(
            num_scalar_prefetch=2,
            grid=(r // TM_EXPERT,),
            in_specs=[
                pl.BlockSpec((TM_EXPERT, d), lambda i, te, tv: (jnp.minimum(i, tv[n_tiles]), 0)),
                pl.BlockSpec((1, d, dff), lambda i, te, tv: (te[i], 0, 0)),
                pl.BlockSpec((1, d, dff), lambda i, te, tv: (te[i], 0, 1)),
                pl.BlockSpec((1, dff, d), lambda i, te, tv: (te[i], 0, 0)),
            ],
            out_specs=pl.BlockSpec((TM_EXPERT, d), lambda i, te, tv: (i, 0)),
        ),
        out_shape=jax.ShapeDtypeStruct((r, d), F32),
        compiler_params=_params(("arbitrary",)),
        name="moe_experts",
    )(tile_expert, tile_valid, xs, w_gu, w_gu, w_down)


def _combine_kernel(dest_ref, ys_ref, x_ref, gate_ref, g_ref, yp_ref, ysm_ref, buf, sem, *, n_first):
    i = pl.program_id(0)
    tm = x_ref.shape[0]

    def issue(blk, carry):
        for j in range(DMA_UNROLL):
            r = blk * DMA_UNROLL + j
            t = i * tm + r
            _row_copy(ys_ref, dest_ref[2 * t], buf.at[0], r, sem).start(priority=0)
            _row_copy(ys_ref, dest_ref[2 * t + 1], buf.at[1], r, sem).start(priority=1)
        return carry

    lax.fori_loop(0, tm // DMA_UNROLL, issue, 0)
    for slot in range(2):
        pltpu.make_async_copy(ys_ref.at[pl.ds(0, tm), :], buf.at[slot], sem).wait()
    gate = gate_ref[...]
    y = x_ref[...] + gate[:, 0:1] * buf[0] + gate[:, 1:2] * buf[1]
    y = _rms(y, g_ref[...])

    @pl.when(i < n_first)
    def _():
        yp_ref[...] = y

    @pl.when(i >= n_first)
    def _():
        ysm_ref[...] = y


def _combine(dest_flat, ys, x, gates, g, n_p):
    n, d = x.shape
    n_first = n_p // TM
    p_spec, s_spec = _split_specs(n_first, d)
    return pl.pallas_call(
        functools.partial(_combine_kernel, n_first=n_first),
        grid_spec=pltpu.PrefetchScalarGridSpec(
            num_scalar_prefetch=1,
            grid=(n // TM,),
            in_specs=[
                pl.BlockSpec(memory_space=pl.ANY),
                pl.BlockSpec((TM, d), lambda i, dest: (i, 0)),
                pl.BlockSpec((TM, LANES), lambda i, dest: (i, 0)),
                pl.BlockSpec((1, d), lambda i, dest: (0, 0)),
            ],
            out_specs=[p_spec, s_spec],
            scratch_shapes=[pltpu.VMEM((2, TM, d), F32), pltpu.SemaphoreType.DMA(())],
        ),
        out_shape=[jax.ShapeDtypeStruct((n_p, d), F32), jax.ShapeDtypeStruct((n - n_p, d), F32)],
        compiler_params=_params(("arbitrary",)),
        name="moe_combine",
    )(dest_flat, ys, x, gates, g)


def _routing_tables(idx, counts, n_rows):
    e = idx[:, 0:2]
    rank = idx[:, 2:4]
    cnt = counts[0, :N_EXPERTS].astype(I32)
    padded = ((cnt + TM_EXPERT - 1) // TM_EXPERT) * TM_EXPERT
    ends = jnp.cumsum(padded)
    starts = ends - padded
    dest = (starts[e] + rank).reshape(-1)
    tile_start = jnp.arange(n_rows // TM_EXPERT, dtype=I32) * TM_EXPERT
    tile_valid = (tile_start < ends[-1]).astype(I32)
    tile_valid = jnp.concatenate([tile_valid, ends[-1:] // TM_EXPERT - 1])
    tile_expert = jnp.sum((tile_start[:, None] >= ends[None, :]).astype(I32), axis=1)
    last_expert = jnp.sum((ends[-1] - 1 >= ends).astype(I32))
    tile_expert = jnp.minimum(tile_expert, last_expert)
    pad_rows = jnp.stack([starts + cnt, padded - cnt], axis=1).reshape(-1)
    used_tiles = ends[-1:] // TM_EXPERT
    pad_rows = jnp.concatenate([pad_rows, used_tiles, n_rows // TM_EXPERT - used_tiles])
    return dest, pad_rows, tile_expert, tile_valid


def _pad_time(a, nseq, t, fill=None):
    w = a.shape[1]
    a = a.reshape(nseq, t, w)
    a = jnp.pad(a, ((0, 0), (0, SAMPLE_PAD_T - t), (0, 0)))
    if fill is not None:
        pad_row = jnp.arange(SAMPLE_PAD_T)[None, :, None] >= t
        lane = jnp.arange(w)[None, None, :]
        a = jnp.where(pad_row & (lane < MLSTM_HEADS), fill, a)
    return a.reshape(nseq * SAMPLE_PAD_T, w)


def _unpad_time(a, nseq, t):
    w = a.shape[1]
    return a.reshape(nseq, SAMPLE_PAD_T, w)[:, :t].reshape(nseq * t, w)


def kernel(x_prompt, x_sample, state_gla_S, state_mlstm_C, state_mlstm_n, state_mlstm_m,
           norm_mix, norm_ffn, norm_final,
           gla_w_in, gla_w_a2, gla_b_a, gla_g_norm, gla_w_out,
           mlstm_w_in, mlstm_b_gate, mlstm_g_norm, mlstm_w_out,
           ffn_w_gu, ffn_w_down,
           moe_w_router, moe_b_router, moe_w_gu, moe_w_down):
    bp, tp, d = x_prompt.shape
    bs, ts, _ = x_sample.shape
    n_p = bp * tp
    n_s = bs * ts
    n = n_p + n_s
    assert n_p % TM == 0 and n_s % TM == 0 and tp % (2 * CHUNK) == 0 and ts <= SAMPLE_PAD_T
    assert norm_mix.shape[0] == 2, "one GLA layer followed by one mLSTM layer"
    qk_w = d // 2
    main_w = 2 * qk_w + 2 * d

    x_p = x_prompt.reshape(n_p, d)
    x_s = x_sample.reshape(n_s, d)

    w_in = gla_w_in[0]
    w_lr = jnp.pad(w_in[:, main_w:], ((0, 0), (0, LANES - GLA_RANK))).astype(BF16)
    w_a2 = jnp.pad(gla_w_a2[0], ((0, LANES - GLA_RANK), (0, 0))).astype(BF16)
    z, log_a = _gla_in_proj(x_p, x_s, norm_mix[0][None], w_in[:, :main_w].astype(BF16), w_lr, w_a2,
                            gla_b_a[0][None])
    s0_p = jnp.zeros((bp,) + state_gla_S.shape[2:], F32)
    u_p, s_p = _gla_recurrence(z, log_a, s0_p, gla_g_norm[0], bp, tp, CHUNK, 4 * CHUNK)
    u_s, s_s = _gla_recurrence_short(_pad_time(z[n_p:], bs, ts), _pad_time(log_a[n_p:], bs, ts),
                                     state_gla_S[0], gla_g_norm[0], bs, SAMPLE_PAD_T,
                                     SAMPLE_SEQS_PER_STEP)
    x, hn = _out_proj(u_p, _unpad_time(u_s, bs, ts), x_p, x_s, gla_w_out[0].astype(BF16),
                      norm_ffn[0][None])
    x, hn, moe_wgu = _swiglu(hn, x, ffn_w_gu[0].astype(BF16), ffn_w_down[0].astype(BF16),
                             norm_mix[1][None], SWIGLU_CHUNKS,
                             moe_w_gu[0].reshape(N_EXPERTS * d, moe_w_gu.shape[3]))

    w_in = mlstm_w_in[0]
    w_gt = jnp.pad(w_in[:, main_w:], ((0, 0), (0, LANES - 2 * MLSTM_HEADS))).astype(BF16)
    b_gt = jnp.pad(mlstm_b_gate[0], (0, LANES - 2 * MLSTM_HEADS))[None]
    dff_e = moe_w_down.shape[2]
    z, gates, moe_wd = _mlstm_in_proj(hn, w_in[:, :main_w].astype(BF16), w_gt, b_gt,
                                      moe_w_down[0].reshape(N_EXPERTS * dff_e, d))
    pairs = MLSTM_HEADS // 2
    dk2 = 2 * state_mlstm_C.shape[3]
    dvm = state_mlstm_C.shape[4]
    c0_p = jnp.zeros((bp, pairs, dk2, dvm), F32)
    n0_p = jnp.zeros((bp, pairs, dk2), F32)
    m0_p = jnp.zeros((bp, 1, LANES), F32)
    u_p, c_p, nn_p, m_p = _mlstm_recurrence_blocked(z, gates, c0_p, n0_p, m0_p,
                                                    mlstm_g_norm[0], bp, tp, CHUNK, 4 * CHUNK)
    c0_s = state_mlstm_C[0].reshape(bs, pairs, dk2, dvm)
    n0_s = state_mlstm_n[0].reshape(bs, pairs, dk2)
    m0_s = jnp.pad(state_mlstm_m[0], ((0, 0), (0, LANES - MLSTM_HEADS)))[:, None, :]
    u_s, c_s, nn_s, m_s = _mlstm_recurrence_short(
        _pad_time(z[n_p:], bs, ts), _pad_time(gates[n_p:], bs, ts, fill=NEG_BIG),
        c0_s, n0_s, m0_s, mlstm_g_norm[0], bs, SAMPLE_PAD_T, SAMPLE_SEQS_PER_STEP)

    w_r = jnp.pad(moe_w_router[0], ((0, 0), (0, LANES - N_EXPERTS)))
    w_r_hi = w_r.astype(BF16)
    w_r = jnp.stack([w_r_hi, (w_r - w_r_hi.astype(F32)).astype(BF16)])
    b_r = jnp.pad(moe_b_router[0], (0, LANES - N_EXPERTS))[None]
    x, hn, route_g, route_i, counts = _out_proj_router(
        u_p, _unpad_time(u_s, bs, ts), x, mlstm_w_out[0].astype(BF16), norm_ffn[1][None], w_r, b_r)
    n_rows = -(-(2 * n + N_EXPERTS * (TM_EXPERT - 1)) // TM_EXPERT) * TM_EXPERT
    dest, pad_rows, tile_expert, tile_valid = _routing_tables(route_i, counts, n_rows)
    xs = _dispatch(dest, pad_rows, hn, n_rows)
    ys = _experts(tile_expert, tile_valid, xs, moe_wgu.reshape(moe_w_gu.shape[1:]),
                  moe_wd.reshape(moe_w_down.shape[1:]))
    y_p, y_s = _combine(dest, ys, x, route_g, norm_final[None], n_p)

    y_prompt = y_p.reshape(bp, tp, d)
    y_sample = y_s.reshape(bs, ts, d)
    hd = state_mlstm_C.shape[2:]
    return (y_prompt, y_sample,
            s_p[None], c_p.reshape((1, bp) + hd), nn_p.reshape(1, bp, hd[0], hd[1]),
            m_p[None, :, 0, :MLSTM_HEADS],
            s_s[None], c_s.reshape((1, bs) + hd), nn_s.reshape(1, bs, hd[0], hd[1]), m_s[None, :, 0, :MLSTM_HEADS])


def _mlstm_in_proj(hn, w_main, w_gate, b_gate, cast_src):
    n, d = hn.shape
    wz = w_main.shape[1]
    cast_in, cast_out, cast_shape = _cast_side_job(cast_src, n // TM)

    def kern(h_ref, w_ref, wg_ref, b_ref, cast_ref, z_ref, gate_ref, cast_out_ref):
        cast_out_ref[...] = cast_ref[...].astype(BF16)
        h = h_ref[...]
        width = wz // IN_PROJ_COL_CHUNKS
        for c in range(IN_PROJ_COL_CHUNKS):
            cs = slice(c * width, (c + 1) * width)
            z_ref[:, cs] = _dot(h, w_ref[:, cs])
        gp = _dot(h, wg_ref[...]) + b_ref[...]
        gc = GATE_CAP * jnp.tanh(gp * (1.0 / GATE_CAP))
        lane = lax.broadcasted_iota(I32, gc.shape, 1)
        out = jnp.where(lane < MLSTM_HEADS, gc, _log_sigmoid(gc))
        gate_ref[...] = jnp.where(lane < 2 * MLSTM_HEADS, out, 0.0)

    return pl.pallas_call(
        kern,
        grid=(n // TM,),
        in_specs=[
            pl.BlockSpec((TM, d), lambda i: (i, 0)),
            _resident_spec((d, wz), lambda i: (0, 0)),
            _resident_spec((d, LANES), lambda i: (0, 0)),
            pl.BlockSpec((1, LANES), lambda i: (0, 0)),
            cast_in,
        ],
        out_specs=[pl.BlockSpec((TM, wz), lambda i: (i, 0)), pl.BlockSpec((TM, LANES), lambda i: (i, 0)),
                   cast_out],
        out_shape=[jax.ShapeDtypeStruct((n, wz), F32), jax.ShapeDtypeStruct((n, LANES), F32),
                   cast_shape],
        compiler_params=_params(("arbitrary",)),
        name="in_proj_mlstm",
    )(hn, w_main, w_gate, b_gate, cast_src)
```

```python
import functools

import jax
import jax.numpy as jnp
import numpy as np
from jax import lax
from jax.experimental import pallas as pl
from jax.experimental.pallas import tpu as pltpu

F32 = jnp.float32
BF16 = jnp.bfloat16
I32 = jnp.int32

EPS = 1e-6
GLA_HEADS = 4
GLA_RANK = 16
GLA_TAU = 16.0
MLSTM_HEADS = 8
GATE_CAP = 15.0
CHUNK = 64
N_EXPERTS = 8
NEG_BIG = -1e30

LANES = 128
TM = 512
TM_EXPERT = 512
SAMPLE_PAD_T = 8
SAMPLE_SEQS_PER_STEP = 16
SWIGLU_CHUNKS = 11
VMEM_LIMIT = 56 * 1024 * 1024


def _dot(a, b):
    return jnp.dot(a, b, preferred_element_type=F32)


def _dot_nt(a, b):
    return lax.dot_general(a, b, (((1,), (1,)), ((), ())), preferred_element_type=F32)


def _dot_tn(a, b):
    return lax.dot_general(a, b, (((0,), (0,)), ((), ())), preferred_element_type=F32)


def _sigmoid(x):
    return 1.0 / (1.0 + jnp.exp(-x))


def _log_sigmoid(x):
    return jnp.minimum(x, 0.0) - jnp.log1p(jnp.exp(-jnp.abs(x)))


def _rms(x, g):
    return x * lax.rsqrt(jnp.mean(x * x, axis=-1, keepdims=True) + EPS) * g


def _cumsum_rows(x, n):
    ridx = lax.broadcasted_iota(I32, x.shape, 0)
    s = 1
    while s < n:
        x = x + jnp.where(ridx >= s, pltpu.roll(x, s, 0), 0.0)
        s *= 2
    return x


def _params(sem):
    return pltpu.CompilerParams(dimension_semantics=sem, vmem_limit_bytes=VMEM_LIMIT)


def _resident_spec(block_shape, index_map):
    return pl.BlockSpec(block_shape, index_map, pipeline_mode=pl.Buffered(1))


def _cast_side_job(src, n_steps):
    rows, cols = src.shape
    n_slabs = n_steps - 1
    assert rows % (16 * n_slabs) == 0
    spec = pl.BlockSpec((rows // n_slabs, cols), lambda i, *_: (jnp.minimum(i, n_slabs - 1), 0))
    return spec, spec, jax.ShapeDtypeStruct(src.shape, BF16)


IN_PROJ_COL_CHUNKS = 4


def _pick_rows(i, n_first, first_ref, second_ref):
    return jnp.where(i < n_first, first_ref[...], second_ref[...])


def _split_specs(n_first, width):
    return [pl.BlockSpec((TM, width), lambda i, *_: (jnp.minimum(i, n_first - 1), 0)),
            pl.BlockSpec((TM, width), lambda i, *_: (jnp.maximum(i - n_first, 0), 0))]


def _gla_in_proj_kernel(xp_ref, xs_ref, g_ref, w_ref, wg_ref, w2_ref, b_ref, z_ref, gate_ref, *,
                        n_first):
    x = _pick_rows(pl.program_id(0), n_first, xp_ref, xs_ref)
    hn = _rms(x, g_ref[...]).astype(BF16)
    width = w_ref.shape[1] // IN_PROJ_COL_CHUNKS
    for c in range(IN_PROJ_COL_CHUNKS):
        cs = slice(c * width, (c + 1) * width)
        z_ref[:, cs] = _dot(hn, w_ref[:, cs])
    a = _dot(hn, wg_ref[...])
    la = _dot(a.astype(BF16), w2_ref[...]) + b_ref[...]
    gate_ref[...] = _log_sigmoid(la) * (1.0 / GLA_TAU)


def _gla_in_proj(x_p, x_s, g, w_main, w_gate, w2, b):
    d = x_p.shape[1]
    n = x_p.shape[0] + x_s.shape[0]
    n_first = x_p.shape[0] // TM
    wz = w_main.shape[1]
    gw = b.shape[1]
    return pl.pallas_call(
        functools.partial(_gla_in_proj_kernel, n_first=n_first),
        grid=(n // TM,),
        in_specs=_split_specs(n_first, d) + [
            pl.BlockSpec((1, d), lambda i: (0, 0)),
            pl.BlockSpec((d, wz), lambda i: (0, 0)),
            pl.BlockSpec((d, LANES), lambda i: (0, 0)),
            pl.BlockSpec(w2.shape, lambda i: (0, 0)),
            pl.BlockSpec((1, gw), lambda i: (0, 0)),
        ],
        out_specs=[
            pl.BlockSpec((TM, wz), lambda i: (i, 0)),
            pl.BlockSpec((TM, gw), lambda i: (i, 0)),
        ],
        out_shape=[jax.ShapeDtypeStruct((n, wz), F32), jax.ShapeDtypeStruct((n, gw), F32)],
        compiler_params=_params(("arbitrary",)),
        name="in_proj_gla",
    )(x_p, x_s, g, w_main, w_gate, w2, b)


def _gla_kernel(q_ref, k_ref, v_ref, r_ref, la_ref, s0_ref, gn_ref, u_ref, st_ref, st_sc,
                *, chunk, tb, nblk):
    heads = GLA_HEADS
    dk = q_ref.shape[1] // heads
    dv = v_ref.shape[1] // heads
    c = pl.program_id(1)

    @pl.when(c == 0)
    def _():
        for h in range(heads):
            st_sc[h] = s0_ref[0, h].T

    ri = lax.broadcasted_iota(I32, (chunk, chunk), 0)
    ci = lax.broadcasted_iota(I32, (chunk, chunk), 1)
    causal = ri >= ci
    kscale = dk ** -0.5
    for j in range(tb // chunk):
        rows = slice(j * chunk, (j + 1) * chunk)
        b_all = _cumsum_rows(la_ref[rows, :], chunk)
        for h in range(heads):
            ks = slice(h * dk, (h + 1) * dk)
            vs = slice(h * dv, (h + 1) * dv)
            b = b_all[:, ks]
            b_end = b[chunk - 1:chunk, :]
            q = q_ref[rows, ks]
            k = k_ref[rows, ks] * kscale
            qg = (q * jnp.exp(b)).astype(BF16)
            kg = (k * jnp.exp(-b)).astype(BF16)
            ke = (k * jnp.exp(b_end - b)).astype(BF16)
            v = v_ref[rows, vs].astype(BF16)
            a = jnp.where(causal, _dot_nt(qg, kg), 0.0).astype(BF16)
            st = st_sc[h]
            o = _dot_nt(qg, st.astype(BF16)) + _dot(a, v)
            st_sc[h] = st * jnp.exp(b_end) + _dot_tn(v, ke)
            on = _rms(o, gn_ref[h:h + 1, :])
            r = r_ref[rows, vs]
            u_ref[rows, vs] = (r * _sigmoid(r) * on).astype(BF16)

    @pl.when(c == nblk - 1)
    def _():
        for h in range(heads):
            st_ref[0, h] = st_sc[h].T


def _gla_recurrence(z, log_a, s0, g_norm, nseq, t, chunk, tb):
    qk_w = log_a.shape[1]
    v_w = 2 * qk_w
    nblk = t // tb
    heads, dk, dv = s0.shape[1:]
    kern = functools.partial(_gla_kernel, chunk=chunk, tb=tb, nblk=nblk)
    row = lambda b, c: b * nblk + c
    return pl.pallas_call(
        kern,
        grid=(nseq, nblk),
        in_specs=[
            pl.BlockSpec((tb, qk_w), lambda b, c: (row(b, c), 0)),
            pl.BlockSpec((tb, qk_w), lambda b, c: (row(b, c), 1)),
            pl.BlockSpec((tb, v_w), lambda b, c: (row(b, c), 1)),
            pl.BlockSpec((tb, v_w), lambda b, c: (row(b, c), 2)),
            pl.BlockSpec((tb, qk_w), lambda b, c: (row(b, c), 0)),
            pl.BlockSpec((1, heads, dk, dv), lambda b, c: (b, 0, 0, 0)),
            pl.BlockSpec((heads, dv), lambda b, c: (0, 0)),
        ],
        out_specs=[
            pl.BlockSpec((tb, v_w), lambda b, c: (row(b, c), 0)),
            pl.BlockSpec((1, heads, dk, dv), lambda b, c: (b, 0, 0, 0)),
        ],
        out_shape=[
            jax.ShapeDtypeStruct((nseq * t, v_w), BF16),
            jax.ShapeDtypeStruct(s0.shape, F32),
        ],
        scratch_shapes=[pltpu.VMEM((heads, dv, dk), F32)],
        compiler_params=_params(("arbitrary", "arbitrary")),
        name="gla_recurrence_t%d" % t,
    )(z, z, z, z, log_a, s0, g_norm)


def _gla_short_kernel(q_ref, k_ref, v_ref, r_ref, la_ref, s0_ref, gn_ref, u_ref, st_ref, *, t, nsq):
    heads = GLA_HEADS
    dk = q_ref.shape[1] // heads
    dv = v_ref.shape[1] // heads
    nrow = nsq * t
    width = q_ref.shape[1]
    rc = lax.broadcasted_iota(I32, (nrow, width), 0) & (t - 1)
    b = _chunk_scan(la_ref[...], rc, t, jnp.add, 0.0)
    b_last = b.reshape(nsq, t, width)[:, t - 1:t, :]
    b_end = jnp.broadcast_to(b_last, (nsq, t, width)).reshape(nrow, width)
    k = k_ref[...] * (dk ** -0.5)
    qg = q_ref[...] * jnp.exp(b)
    kg = k * jnp.exp(-b)
    ke = k * jnp.exp(b_end - b)
    decay = jnp.exp(b_end)
    ri = lax.broadcasted_iota(I32, (nrow, nrow), 0)
    ci = lax.broadcasted_iota(I32, (nrow, nrow), 1)
    shift = t.bit_length() - 1
    mask = ((ri >> shift) == (ci >> shift)) & (ri >= ci)
    for h in range(heads):
        ks = slice(h * dk, (h + 1) * dk)
        vs = slice(h * dv, (h + 1) * dv)
        a = jnp.where(mask, _dot_nt(qg[:, ks].astype(BF16), kg[:, ks].astype(BF16)), 0.0)
        o = _dot(a.astype(BF16), v_ref[:, vs].astype(BF16))
        decay_t = decay[:, ks].T
        o_state = []
        for sq in range(nsq):
            r = slice(sq * t, (sq + 1) * t)
            s0 = s0_ref[sq, h]
            o_state.append(_dot(qg[r, ks].astype(BF16), s0.astype(BF16)))
            upd = _dot_tn(ke[r, ks].astype(BF16), v_ref[r, vs].astype(BF16))
            st_ref[sq, h] = decay_t[:, sq * t:sq * t + 1] * s0 + upd
        on = _rms(o + jnp.concatenate(o_state, axis=0), gn_ref[h:h + 1, :])
        rg = r_ref[:, vs]
        u_ref[:, vs] = (rg * _sigmoid(rg) * on).astype(BF16)


def _gla_recurrence_short(z, log_a, s0, g_norm, nseq, t, nsq):
    qk_w = log_a.shape[1]
    v_w = 2 * qk_w
    heads, dk, dv = s0.shape[1:]
    assert nsq * t == dk == LANES
    rb = nsq * t
    return pl.pallas_call(
        functools.partial(_gla_short_kernel, t=t, nsq=nsq),
        grid=(nseq // nsq,),
        in_specs=[
            pl.BlockSpec((rb, qk_w), lambda b: (b, 0)),
            pl.BlockSpec((rb, qk_w), lambda b: (b, 1)),
            pl.BlockSpec((rb, v_w), lambda b: (b, 1)),
            pl.BlockSpec((rb, v_w), lambda b: (b, 2)),
            pl.BlockSpec((rb, qk_w), lambda b: (b, 0)),
            pl.BlockSpec((nsq, heads, dk, dv), lambda b: (b, 0, 0, 0)),
            pl.BlockSpec((heads, dv), lambda b: (0, 0)),
        ],
        out_specs=[
            pl.BlockSpec((rb, v_w), lambda b: (b, 0)),
            pl.BlockSpec((nsq, heads, dk, dv), lambda b: (b, 0, 0, 0)),
        ],
        out_shape=[
            jax.ShapeDtypeStruct((nseq * t, v_w), BF16),
            jax.ShapeDtypeStruct(s0.shape, F32),
        ],
        compiler_params=_params(("arbitrary",)),
        name="gla_recurrence_t%d" % t,
    )(z, z, z, z, log_a, s0, g_norm)


def _mlstm_short_kernel(q_ref, k_ref, v_ref, op_ref, gt_ref, e_ref, c0_ref, n0_ref, m0_ref, gn_ref,
                        u_ref, ct_ref, nt_ref, mt_ref, *, t, nsq):
    heads = MLSTM_HEADS
    pairs = heads // 2
    dv = v_ref.shape[1] // heads
    dk = q_ref.shape[1] // heads
    nrow = nsq * t

    def per_seq(x3):
        return jnp.broadcast_to(x3, (nsq, t, x3.shape[2])).reshape(nrow, x3.shape[2])

    def last(x):
        return x.reshape(nsq, t, x.shape[1])[:, t - 1:t, :]

    rc = lax.broadcasted_iota(I32, (nrow, LANES), 0) & (t - 1)
    gts = gt_ref[...]
    f_cum = pltpu.roll(_chunk_scan(gts, rc, t, jnp.add, 0.0), LANES - heads, 1)
    b = gts - f_cum
    cmb = _chunk_scan(b, rc, t, jnp.maximum, NEG_BIG)
    mp = per_seq(m0_ref[...])
    big_m = jnp.maximum(mp, cmb)
    m_t = f_cum + big_m
    m_new = last(m_t)
    mt_ref[...] = m_new
    mn = per_seq(m_new)
    fe = per_seq(last(f_cum))
    w_i = jnp.exp(mp - big_m)
    em = jnp.exp(-m_t)
    k_sc = jnp.exp(fe - f_cum + gts - mn)
    w_c = jnp.exp(fe + mp - mn)
    ex = _dot(_pack_terms([w_i, k_sc, -big_m, em, w_c], heads), e_ref[...])
    offs = np.cumsum([0] + [heads * w for w in SHORT_EXPAND_WIDTHS])
    q = q_ref[...]
    k = k_ref[...] * (dk ** -0.5)
    qs = q * ex[:, offs[0]:offs[1]]
    ke = k * ex[:, offs[1]:offs[2]]
    neg_m = ex[:, offs[2]:offs[3]]
    em_v = ex[:, offs[3]:offs[4]]
    wc_k = ex[:, offs[4]:offs[5]]
    bt = b.T

    ri = lax.broadcasted_iota(I32, (nrow, nrow), 0)
    ci = lax.broadcasted_iota(I32, (nrow, nrow), 1)
    shift = t.bit_length() - 1
    mask = ((ri >> shift) == (ci >> shift)) & (ri >= ci)
    lane1 = lax.broadcasted_iota(I32, (1, LANES), 1)
    rowi = lax.broadcasted_iota(I32, (LANES, LANES), 0)
    ones_v = jnp.ones((nrow, dv), BF16)
    ones_k = jnp.ones((LANES, dv), BF16)
    for p in range(pairs):
        ps = slice(p * LANES, (p + 1) * LANES)
        qp = q[:, ps]
        qsp = qs[:, ps]
        kpb = k[:, ps].astype(BF16)
        n_rows = per_seq(n0_ref[:, p:p + 1, :])
        intra, den_state, qsh = [], [], []
        for jj in range(2):
            h = 2 * p + jj
            hs = slice(h * dv, (h + 1) * dv)
            mine = (lane1 >= jj * dk) & (lane1 < (jj + 1) * dk)
            qh = jnp.where(mine, qp, 0.0).astype(BF16)
            qsh.append(jnp.where(mine, qsp, 0.0))
            d = neg_m[:, hs] + bt[h:h + 1, :]
            s = _dot_nt(qh, kpb) * jnp.where(mask, jnp.exp(d), 0.0)
            vh = jnp.concatenate([v_ref[:, hs].astype(BF16), ones_v], axis=1)
            intra.append(_dot(s.astype(BF16), vh))
            den_state.append(_dot((qsh[jj] * n_rows).astype(BF16), ones_k))
        num_state = [[], []]
        for sq in range(nsq):
            r = slice(sq * t, (sq + 1) * t)
            lhs = jnp.concatenate([qsh[0][r], qsh[1][r]], axis=0).astype(BF16)
            c_pair = jnp.concatenate([c0_ref[sq, 2 * p], c0_ref[sq, 2 * p + 1]], axis=0)
            res = _dot(lhs, c_pair.astype(BF16))
            num_state[0].append(res[:t])
            num_state[1].append(res[t:])
        for jj in range(2):
            h = 2 * p + jj
            hs = slice(h * dv, (h + 1) * dv)
            num = intra[jj][:, :dv] + jnp.concatenate(num_state[jj], axis=0)
            den = intra[jj][:, dv:] + den_state[jj]
            hh = num / jnp.maximum(jnp.abs(den), em_v[:, hs])
            hn = _rms(hh, gn_ref[h:h + 1, :])
            u_ref[:, hs] = (_sigmoid(op_ref[:, hs]) * hn).astype(BF16)
        kep = ke[:, ps]
        for sq in range(nsq):
            r = slice(sq * t, (sq + 1) * t)
            vp = v_ref[r, 2 * p * dv:(2 * p + 2) * dv].astype(BF16)
            full = _dot_tn(kep[r].astype(BF16), vp)
            upd = jnp.where(rowi < dk, full[:, :dv], full[:, dv:])
            w_row = w_c[sq * t:sq * t + 1, :]
            w_col = jnp.where(rowi < dk, w_row[:, 2 * p:2 * p + 1], w_row[:, 2 * p + 1:2 * p + 2])
            c_pair = jnp.concatenate([c0_ref[sq, 2 * p], c0_ref[sq, 2 * p + 1]], axis=0)
            c_new = w_col * c_pair + upd
            ct_ref[sq, 2 * p] = c_new[:dk]
            ct_ref[sq, 2 * p + 1] = c_new[dk:]
        k_sum = jnp.sum(kep.reshape(nsq, t, LANES), axis=1, keepdims=True)
        nt_ref[:, p:p + 1, :] = last(wc_k[:, ps]) * n0_ref[:, p:p + 1, :] + k_sum


def _mlstm_recurrence_short(z, gates, c0, n0, m0, g_norm, nseq, t, nsq):
    heads = g_norm.shape[0]
    v_w = heads * g_norm.shape[1]
    qk_w = v_w // 2
    assert nsq * t == LANES
    expand = _expand_matrix(heads, SHORT_EXPAND_WIDTHS)
    kern = functools.partial(_mlstm_short_kernel, t=t, nsq=nsq)
    rb = nsq * t
    st4 = lambda b: (b, 0, 0, 0)
    st3 = lambda b: (b, 0, 0)
    return pl.pallas_call(
        kern,
        grid=(nseq // nsq,),
        in_specs=[
            pl.BlockSpec((rb, qk_w), lambda b: (b, 0)),
            pl.BlockSpec((rb, qk_w), lambda b: (b, 1)),
            pl.BlockSpec((rb, v_w), lambda b: (b, 1)),
            pl.BlockSpec((rb, v_w), lambda b: (b, 2)),
            pl.BlockSpec((rb, LANES), lambda b: (b, 0)),
            pl.BlockSpec(expand.shape, lambda b: (0, 0)),
            pl.BlockSpec((nsq,) + c0.shape[1:], st4),
            pl.BlockSpec((nsq,) + n0.shape[1:], st3),
            pl.BlockSpec((nsq,) + m0.shape[1:], st3),
            pl.BlockSpec(g_norm.shape, lambda b: (0, 0)),
        ],
        out_specs=[
            pl.BlockSpec((rb, v_w), lambda b: (b, 0)),
            pl.BlockSpec((nsq,) + c0.shape[1:], st4),
            pl.BlockSpec((nsq,) + n0.shape[1:], st3),
            pl.BlockSpec((nsq,) + m0.shape[1:], st3),
        ],
        out_shape=[
            jax.ShapeDtypeStruct((nseq * t, v_w), BF16),
            jax.ShapeDtypeStruct(c0.shape, F32),
            jax.ShapeDtypeStruct(n0.shape, F32),
            jax.ShapeDtypeStruct(m0.shape, F32),
        ],
        compiler_params=_params(("arbitrary",)),
        name="mlstm_recurrence_t%d" % t,
    )(z, z, z, z, gates, expand, c0, n0, m0, g_norm)


N_SPLIT = 3
EXPAND_WIDTHS = (64, 64, 128, 128)
SHORT_EXPAND_WIDTHS = EXPAND_WIDTHS + (64,)


def _expand_matrix(heads, widths):
    assert len(widths) * N_SPLIT * heads <= LANES
    cols = []
    for qi, width in enumerate(widths):
        sel = np.zeros((LANES, heads * width), np.float32)
        for term in range(N_SPLIT):
            for h in range(heads):
                sel[(qi * N_SPLIT + term) * heads + h, h * width:(h + 1) * width] = 1.0
        cols.append(sel)
    return jnp.asarray(np.concatenate(cols, axis=1), BF16)


def _chunk_scan(x, rc, n, op, fill):
    s = 1
    while s < n:
        x = op(x, jnp.where(rc >= s, pltpu.roll(x, s, 0), fill))
        s *= 2
    return x


def _pack_terms(values, heads):
    lane = lax.broadcasted_iota(I32, values[0].shape, 1)
    packed = jnp.zeros(values[0].shape, F32)
    slot = 0
    for val in values:
        rest = val
        for term in range(N_SPLIT):
            part = rest.astype(BF16).astype(F32)
            rest = rest - part
            moved = part if slot == 0 else pltpu.roll(part, slot * heads, 1)
            packed = jnp.where((lane >= slot * heads) & (lane < (slot + 1) * heads), moved, packed)
            slot += 1
    return packed.astype(BF16)


def _mlstm_block_kernel(q_ref, k_ref, v_ref, op_ref, gt_ref, e_ref, c0_ref, n0_ref, m0_ref, gn_ref,
                        u_ref, ct_ref, nt_ref, mt_ref, c_sc, nb_sc, m_sc, *, chunk, tb, nblk):
    heads = MLSTM_HEADS
    pairs = heads // 2
    dv = v_ref.shape[1] // heads
    dk = q_ref.shape[1] // heads
    c = pl.program_id(1)

    @pl.when(c == 0)
    def _():
        c_sc[...] = c0_ref[0]
        for p in range(pairs):
            nb_sc[p] = jnp.broadcast_to(n0_ref[0, p:p + 1, :], (LANES, LANES)).T
        m_sc[...] = m0_ref[0]

    nch = tb // chunk
    rc = lax.broadcasted_iota(I32, (tb, LANES), 0) & (chunk - 1)
    gts = gt_ref[...]
    f_cum = pltpu.roll(_chunk_scan(gts, rc, chunk, jnp.add, 0.0), LANES - heads, 1)
    b = gts - f_cum
    cmb = _chunk_scan(b, rc, chunk, jnp.maximum, NEG_BIG)
    m_prev = m_sc[...]
    mp_rows, mn_rows, fe_rows, w_c = [], [], [], []
    for j in range(nch):
        last = (j + 1) * chunk - 1
        f_end = f_cum[last:last + 1, :]
        m_new = f_end + jnp.maximum(m_prev, cmb[last:last + 1, :])
        w_c.append(jnp.exp(f_end + m_prev - m_new))
        mp_rows.append(jnp.broadcast_to(m_prev, (chunk, LANES)))
        mn_rows.append(jnp.broadcast_to(m_new, (chunk, LANES)))
        fe_rows.append(jnp.broadcast_to(f_end, (chunk, LANES)))
        m_prev = m_new
    m_sc[...] = m_prev
    mp = jnp.concatenate(mp_rows, axis=0)
    mn = jnp.concatenate(mn_rows, axis=0)
    fe = jnp.concatenate(fe_rows, axis=0)
    big_m = jnp.maximum(mp, cmb)
    w_i = jnp.exp(mp - big_m)
    em = jnp.exp(-(f_cum + big_m))
    k_sc = jnp.exp(fe - f_cum + gts - mn)
    ex = _dot(_pack_terms([w_i, k_sc, -big_m, em], heads), e_ref[...])
    o1 = heads * EXPAND_WIDTHS[0]
    o2 = o1 + heads * EXPAND_WIDTHS[1]
    o3 = o2 + heads * EXPAND_WIDTHS[2]
    q = q_ref[...]
    k = k_ref[...] * (dk ** -0.5)
    qs = q * ex[:, :o1]
    ke = k * ex[:, o1:o2]
    neg_m = ex[:, o2:o3]
    em_v = ex[:, o3:]
    bt = b.T

    ri = lax.broadcasted_iota(I32, (chunk, chunk), 0)
    ci = lax.broadcasted_iota(I32, (chunk, chunk), 1)
    causal = ri >= ci
    lane1 = lax.broadcasted_iota(I32, (1, LANES), 1)
    rowi = lax.broadcasted_iota(I32, (LANES, LANES), 0)
    ones_v = jnp.ones((chunk, dv), BF16)
    for j in range(nch):
        rows = slice(j * chunk, (j + 1) * chunk)
        for p in range(pairs):
            ps = slice(p * LANES, (p + 1) * LANES)
            qp = q[rows, ps]
            qsp = qs[rows, ps]
            kpb = k[rows, ps].astype(BF16)
            cp = c_sc[p]
            nb = nb_sc[p]
            state = jnp.concatenate([cp, nb], axis=1).astype(BF16)
            for jj in range(2):
                h = 2 * p + jj
                hs = slice(h * dv, (h + 1) * dv)
                mine = (lane1 >= jj * dk) & (lane1 < (jj + 1) * dk)
                qh = jnp.where(mine, qp, 0.0).astype(BF16)
                qsh = jnp.where(mine, qsp, 0.0).astype(BF16)
                d = neg_m[rows, h * dv:h * dv + chunk] + bt[h:h + 1, j * chunk:(j + 1) * chunk]
                s = _dot_nt(qh, kpb) * jnp.where(causal, jnp.exp(d), 0.0)
                vh = jnp.concatenate([v_ref[rows, hs].astype(BF16), ones_v], axis=1)
                out = _dot(qsh, state) + _dot(s.astype(BF16), vh)
                hh = out[:, :dv] / jnp.maximum(jnp.abs(out[:, dv:]), em_v[rows, hs])
                hn = _rms(hh, gn_ref[h:h + 1, :])
                u_ref[rows, hs] = (_sigmoid(op_ref[rows, hs]) * hn).astype(BF16)
            vp = jnp.concatenate([v_ref[rows, 2 * p * dv:(2 * p + 2) * dv].astype(BF16), ones_v], axis=1)
            full = _dot_tn(ke[rows, ps].astype(BF16), vp)
            upd = jnp.where(rowi < dk, full[:, :dv], full[:, dv:2 * dv])
            w_col = jnp.where(rowi < dk, w_c[j][:, 2 * p:2 * p + 1], w_c[j][:, 2 * p + 1:2 * p + 2])
            c_sc[p] = w_col * cp + upd
            nb_sc[p] = w_col * nb + full[:, 2 * dv:]

    @pl.when(c == nblk - 1)
    def _():
        ct_ref[0] = c_sc[...]
        for p in range(pairs):
            nt_ref[0, p:p + 1, :] = nb_sc[p].T[0:1, :]
        mt_ref[0] = m_sc[...]


def _mlstm_recurrence_blocked(z, gates, c0, n0, m0, g_norm, nseq, t, chunk, tb):
    heads = g_norm.shape[0]
    v_w = heads * g_norm.shape[1]
    qk_w = v_w // 2
    nblk = t // tb
    expand = _expand_matrix(heads, EXPAND_WIDTHS)
    kern = functools.partial(_mlstm_block_kernel, chunk=chunk, tb=tb, nblk=nblk)
    row = lambda b, c: b * nblk + c
    st4 = lambda b, c: (b, 0, 0, 0)
    st3 = lambda b, c: (b, 0, 0)
    return pl.pallas_call(
        kern,
        grid=(nseq, nblk),
        in_specs=[
            pl.BlockSpec((tb, qk_w), lambda b, c: (row(b, c), 0)),
            pl.BlockSpec((tb, qk_w), lambda b, c: (row(b, c), 1)),
            pl.BlockSpec((tb, v_w), lambda b, c: (row(b, c), 1)),
            pl.BlockSpec((tb, v_w), lambda b, c: (row(b, c), 2)),
            pl.BlockSpec((tb, LANES), lambda b, c: (row(b, c), 0)),
            pl.BlockSpec(expand.shape, lambda b, c: (0, 0)),
            pl.BlockSpec((1,) + c0.shape[1:], st4),
            pl.BlockSpec((1,) + n0.shape[1:], st3),
            pl.BlockSpec((1,) + m0.shape[1:], st3),
            pl.BlockSpec(g_norm.shape, lambda b, c: (0, 0)),
        ],
        out_specs=[
            pl.BlockSpec((tb, v_w), lambda b, c: (row(b, c), 0)),
            pl.BlockSpec((1,) + c0.shape[1:], st4),
            pl.BlockSpec((1,) + n0.shape[1:], st3),
            pl.BlockSpec((1,) + m0.shape[1:], st3),
        ],
        out_shape=[
            jax.ShapeDtypeStruct((nseq * t, v_w), BF16),
            jax.ShapeDtypeStruct(c0.shape, F32),
            jax.ShapeDtypeStruct(n0.shape, F32),
            jax.ShapeDtypeStruct(m0.shape, F32),
        ],
        scratch_shapes=[
            pltpu.VMEM(c0.shape[1:], F32),
            pltpu.VMEM(c0.shape[1:], F32),
            pltpu.VMEM(m0.shape[1:], F32),
        ],
        compiler_params=_params(("arbitrary", "arbitrary")),
        name="mlstm_recurrence_t%d" % t,
    )(z, z, z, z, gates, expand, c0, n0, m0, g_norm)


def _out_proj_kernel(up_ref, us_ref, xp_ref, xs_ref, w_ref, g_ref, xo_ref, hn_ref, *, n_first):
    i = pl.program_id(0)
    u = _pick_rows(i, n_first, up_ref, us_ref)
    x = _pick_rows(i, n_first, xp_ref, xs_ref) + _dot(u, w_ref[...])
    xo_ref[...] = x
    hn_ref[...] = _rms(x, g_ref[...]).astype(hn_ref.dtype)


def _out_proj(u_p, u_s, x_p, x_s, w, g):
    d = x_p.shape[1]
    n = x_p.shape[0] + x_s.shape[0]
    n_first = x_p.shape[0] // TM
    return pl.pallas_call(
        functools.partial(_out_proj_kernel, n_first=n_first),
        grid=(n // TM,),
        in_specs=_split_specs(n_first, u_p.shape[1]) + _split_specs(n_first, d) + [
            pl.BlockSpec(w.shape, lambda i: (0, 0)),
            pl.BlockSpec((1, d), lambda i: (0, 0)),
        ],
        out_specs=[pl.BlockSpec((TM, d), lambda i: (i, 0)), pl.BlockSpec((TM, d), lambda i: (i, 0))],
        out_shape=[jax.ShapeDtypeStruct((n, d), F32), jax.ShapeDtypeStruct((n, d), BF16)],
        compiler_params=_params(("arbitrary",)),
        name="out_proj",
    )(u_p, u_s, x_p, x_s, w, g)


def _out_proj_router_kernel(up_ref, us_ref, x_ref, w_ref, g_ref, wr_ref, br_ref,
                            xo_ref, hn_ref, gate_ref, idx_ref, cnt_ref, cnt_sc, *, n_first):
    i = pl.program_id(0)
    u = _pick_rows(i, n_first, up_ref, us_ref)

    @pl.when(i == 0)
    def _():
        cnt_sc[...] = jnp.zeros_like(cnt_sc)

    x = x_ref[...] + _dot(u, w_ref[...])
    xo_ref[...] = x
    hn = _rms(x, g_ref[...])
    hn_ref[...] = hn
    hn_hi = hn.astype(BF16)
    hn_lo = (hn - hn_hi.astype(F32)).astype(BF16)
    logits = (_dot(hn_hi, wr_ref[0]) + _dot(hn_lo, wr_ref[0]) + _dot(hn_hi, wr_ref[1])
              + br_ref[...])
    tm = logits.shape[0]
    lane = lax.broadcasted_iota(I32, logits.shape, 1)
    valid = lane < N_EXPERTS
    logits = jnp.where(valid, logits, -jnp.inf)
    ex = jnp.exp(logits - jnp.max(logits, axis=-1, keepdims=True))
    probs = jnp.where(valid, ex / jnp.sum(ex, axis=-1, keepdims=True), -1.0)
    p1 = jnp.max(probs, axis=-1, keepdims=True)
    i1 = jnp.min(jnp.where(probs == p1, lane, LANES), axis=-1, keepdims=True)
    rest = jnp.where(lane == i1, -1.0, probs)
    p2 = jnp.max(rest, axis=-1, keepdims=True)
    i2 = jnp.min(jnp.where(rest == p2, lane, LANES), axis=-1, keepdims=True)
    tot = p1 + p2
    onehot = ((lane == i1) | (lane == i2)).astype(BF16)
    rr = lax.broadcasted_iota(I32, (tm, tm), 0)
    cc = lax.broadcasted_iota(I32, (tm, tm), 1)
    strict = (rr > cc).astype(BF16)
    before = _dot(strict, onehot) + cnt_sc[...]
    r1 = jnp.sum(jnp.where(lane == i1, before, 0.0), axis=-1, keepdims=True)
    r2 = jnp.sum(jnp.where(lane == i2, before, 0.0), axis=-1, keepdims=True)
    cnt_sc[...] = cnt_sc[...] + jnp.sum(onehot.astype(F32), axis=0, keepdims=True)
    gate_ref[...] = jnp.where(lane == 0, p1 / tot, jnp.where(lane == 1, p2 / tot, 0.0))
    idx_ref[...] = jnp.where(lane == 0, i1,
                             jnp.where(lane == 1, i2,
                                       jnp.where(lane == 2, r1.astype(I32),
                                                 jnp.where(lane == 3, r2.astype(I32), 0))))
    cnt_ref[...] = cnt_sc[...]


def _out_proj_router(u_p, u_s, x, w, g, w_router, b_router):
    n, d = x.shape
    n_first = u_p.shape[0] // TM
    row = lambda i: (i, 0)
    fix = lambda i: (0, 0)
    return pl.pallas_call(
        functools.partial(_out_proj_router_kernel, n_first=n_first),
        grid=(n // TM,),
        in_specs=_split_specs(n_first, u_p.shape[1]) + [
            pl.BlockSpec((TM, d), row),
            pl.BlockSpec(w.shape, fix),
            pl.BlockSpec((1, d), fix),
            pl.BlockSpec(w_router.shape, lambda i: (0, 0, 0)),
            pl.BlockSpec((1, LANES), fix),
        ],
        out_specs=[
            pl.BlockSpec((TM, d), row),
            pl.BlockSpec((TM, d), row),
            pl.BlockSpec((TM, LANES), row),
            pl.BlockSpec((TM, LANES), row),
            pl.BlockSpec((1, LANES), fix),
        ],
        out_shape=[
            jax.ShapeDtypeStruct((n, d), F32),
            jax.ShapeDtypeStruct((n, d), F32),
            jax.ShapeDtypeStruct((n, LANES), F32),
            jax.ShapeDtypeStruct((n, LANES), I32),
            jax.ShapeDtypeStruct((1, LANES), F32),
        ],
        scratch_shapes=[pltpu.VMEM((1, LANES), F32)],
        compiler_params=_params(("arbitrary",)),
        name="out_proj_router",
    )(u_p, u_s, x, w, g, w_router, b_router)


def _swiglu_kernel(hn_ref, x_ref, wg_ref, wu_ref, wd_ref, g_ref, cast_ref, xo_ref, ho_ref, cast_out_ref,
                   *, nchunk):
    cast_out_ref[...] = cast_ref[...].astype(BF16)
    hn = hn_ref[...]
    width = wd_ref.shape[0] // nchunk
    x = x_ref[...]
    for c in range(nchunk):
        cs = slice(c * width, (c + 1) * width)
        gate = _dot(hn, wg_ref[:, cs])
        up = _dot(hn, wu_ref[:, cs])
        act = (gate * _sigmoid(gate) * up).astype(BF16)
        x = x + _dot(act, wd_ref[cs, :])
    xo_ref[...] = x
    ho_ref[...] = _rms(x, g_ref[...]).astype(BF16)


def _swiglu(hn, x, w_gu, w_down, g, nchunk, cast_src):
    n, d = x.shape
    dff = w_down.shape[0]
    kern = functools.partial(_swiglu_kernel, nchunk=nchunk)
    row = lambda i: (i, 0)
    cast_in, cast_out, cast_shape = _cast_side_job(cast_src, n // TM)
    return pl.pallas_call(
        kern,
        grid=(n // TM,),
        in_specs=[
            pl.BlockSpec((TM, d), row),
            pl.BlockSpec((TM, d), row),
            _resident_spec((d, dff), lambda i: (0, 0)),
            _resident_spec((d, dff), lambda i: (0, 1)),
            _resident_spec((dff, d), lambda i: (0, 0)),
            pl.BlockSpec((1, d), lambda i: (0, 0)),
            cast_in,
        ],
        out_specs=[pl.BlockSpec((TM, d), row), pl.BlockSpec((TM, d), row), cast_out],
        out_shape=[jax.ShapeDtypeStruct((n, d), F32), jax.ShapeDtypeStruct((n, d), BF16), cast_shape],
        compiler_params=_params(("arbitrary",)),
        name="dense_swiglu",
    )(hn, x, w_gu, w_gu, w_down, g, cast_src)


DMA_UNROLL = 8


def _row_copy(src_ref, src_row, dst_ref, dst_row, sem):
    return pltpu.make_async_copy(src_ref.at[pl.ds(src_row, 1), :], dst_ref.at[pl.ds(dst_row, 1), :], sem)


def _dispatch_kernel(dest_ref, pad_ref, hn_ref, xs_ref, zero_sc, sem):
    i = pl.program_id(0)
    tm = hn_ref.shape[0]

    @pl.when(i == 0)
    def _():
        zero_sc[...] = jnp.zeros_like(zero_sc)
        for e in range(N_EXPERTS):
            base = pad_ref[2 * e]
            count = pad_ref[2 * e + 1]

            def issue_pad(j, carry, base=base):
                _row_copy(zero_sc, 0, xs_ref, base + j, sem).start()
                return carry

            def wait_pad(j, carry):
                _row_copy(zero_sc, 0, xs_ref, 0, sem).wait()
                return carry

            lax.fori_loop(0, count, issue_pad, 0)
            lax.fori_loop(0, count, wait_pad, 0)

        rows = zero_sc.shape[0]

        def tail_copy(j):
            first = (pad_ref[2 * N_EXPERTS] + j) * rows
            return pltpu.make_async_copy(zero_sc, xs_ref.at[pl.ds(first, rows), :], sem)

        def issue_tail(j, carry):
            tail_copy(j).start()
            return carry

        def wait_tail(j, carry):
            tail_copy(j).wait()
            return carry

        lax.fori_loop(0, pad_ref[2 * N_EXPERTS + 1], issue_tail, 0)
        lax.fori_loop(0, pad_ref[2 * N_EXPERTS + 1], wait_tail, 0)

    def issue(blk, carry):
        for j in range(DMA_UNROLL):
            r = blk * DMA_UNROLL + j
            t = i * tm + r
            _row_copy(hn_ref, r, xs_ref, dest_ref[2 * t], sem).start(priority=0)
            _row_copy(hn_ref, r, xs_ref, dest_ref[2 * t + 1], sem).start(priority=1)
        return carry

    lax.fori_loop(0, tm // DMA_UNROLL, issue, 0)
    for _ in range(2):
        pltpu.make_async_copy(hn_ref, xs_ref.at[pl.ds(0, tm), :], sem).wait()


def _dispatch(dest_flat, pad_rows, hn, n_rows):
    n, d = hn.shape
    return pl.pallas_call(
        _dispatch_kernel,
        grid_spec=pltpu.PrefetchScalarGridSpec(
            num_scalar_prefetch=2,
            grid=(n // TM,),
            in_specs=[pl.BlockSpec((TM, d), lambda i, dest, pad: (i, 0))],
            out_specs=pl.BlockSpec(memory_space=pl.ANY),
            scratch_shapes=[pltpu.VMEM((TM_EXPERT, d), F32), pltpu.SemaphoreType.DMA(())],
        ),
        out_shape=jax.ShapeDtypeStruct((n_rows, d), F32),
        compiler_params=_params(("arbitrary",)),
        name="moe_dispatch",
    )(dest_flat, pad_rows, hn)


def _expert_kernel(te_ref, tv_ref, xs_ref, wg_ref, wu_ref, wd_ref, ys_ref, *, nchunk):
    i = pl.program_id(0)

    @pl.when(tv_ref[i] > 0)
    def _():
        x = xs_ref[...].astype(BF16)
        width = wg_ref.shape[2] // nchunk
        acc = None
        for c in range(nchunk):
            cs = slice(c * width, (c + 1) * width)
            gate = _dot(x, wg_ref[0, :, cs])
            up = _dot(x, wu_ref[0, :, cs])
            act = (gate * _sigmoid(gate) * up).astype(BF16)
            part = _dot(act, wd_ref[0, cs, :])
            acc = part if acc is None else acc + part
        ys_ref[...] = acc

    @pl.when(tv_ref[i] == 0)
    def _():
        ys_ref[...] = jnp.zeros_like(ys_ref)


def _experts(tile_expert, tile_valid, xs, w_gu, w_down):
    r, d = xs.shape
    n_tiles = r // TM_EXPERT
    dff = w_down.shape[1]
    kern = functools.partial(_expert_kernel, nchunk=7)
    return pl.pallas_call(
        kern,
        grid_spec=pltpu.PrefetchScalarGridSpec(
            num_scalar_prefetch=2,
            grid=(r // TM_EXPERT,),
            in_specs=[
                pl.BlockSpec((TM_EXPERT, d), lambda i, te, tv: (jnp.minimum(i, tv[n_tiles]), 0)),
                pl.BlockSpec((1, d, dff), lambda i, te, tv: (te[i], 0, 0)),
                pl.BlockSpec((1, d, dff), lambda i, te, tv: (te[i], 0, 1)),
                pl.BlockSpec((1, dff, d), lambda i, te, tv: (te[i], 0, 0)),
            ],
            out_specs=pl.BlockSpec((TM_EXPERT, d), lambda i, te, tv: (i, 0)),
        ),
        out_shape=jax.ShapeDtypeStruct((r, d), F32),
        compiler_params=_params(("arbitrary",)),
        name="moe_experts",
    )(tile_expert, tile_valid, xs, w_gu, w_gu, w_down)


def _combine_kernel(dest_ref, ys_ref, x_ref, gate_ref, g_ref, yp_ref, ysm_ref, buf, sem, *, n_first):
    i = pl.program_id(0)
    tm = x_ref.shape[0]

    def issue(blk, carry):
        for j in range(DMA_UNROLL):
            r = blk * DMA_UNROLL + j
            t = i * tm + r
            _row_copy(ys_ref, dest_ref[2 * t], buf.at[0], r, sem).start(priority=0)
            _row_copy(ys_ref, dest_ref[2 * t + 1], buf.at[1], r, sem).start(priority=1)
        return carry

    lax.fori_loop(0, tm // DMA_UNROLL, issue, 0)
    for slot in range(2):
        pltpu.make_async_copy(ys_ref.at[pl.ds(0, tm), :], buf.at[slot], sem).wait()
    gate = gate_ref[...]
    y = x_ref[...] + gate[:, 0:1] * buf[0] + gate[:, 1:2] * buf[1]
    y = _rms(y, g_ref[...])

    @pl.when(i < n_first)
    def _():
        yp_ref[...] = y

    @pl.when(i >= n_first)
    def _():
        ysm_ref[...] = y


def _combine(dest_flat, ys, x, gates, g, n_p):
    n, d = x.shape
    n_first = n_p // TM
    p_spec, s_spec = _split_specs(n_first, d)
    return pl.pallas_call(
        functools.partial(_combine_kernel, n_first=n_first),
        grid_spec=pltpu.PrefetchScalarGridSpec(
            num_scalar_prefetch=1,
            grid=(n // TM,),
            in_specs=[
                pl.BlockSpec(memory_space=pl.ANY),
                pl.BlockSpec((TM, d), lambda i, dest: (i, 0)),
                pl.BlockSpec((TM, LANES), lambda i, dest: (i, 0)),
                pl.BlockSpec((1, d), lambda i, dest: (0, 0)),
            ],
            out_specs=[p_spec, s_spec],
            scratch_shapes=[pltpu.VMEM((2, TM, d), F32), pltpu.SemaphoreType.DMA(())],
        ),
        out_shape=[jax.ShapeDtypeStruct((n_p, d), F32), jax.ShapeDtypeStruct((n - n_p, d), F32)],
        compiler_params=_params(("arbitrary",)),
        name="moe_combine",
    )(dest_flat, ys, x, gates, g)


def _routing_tables(idx, counts, n_rows):
    e = idx[:, 0:2]
    rank = idx[:, 2:4]
    cnt = counts[0, :N_EXPERTS].astype(I32)
    padded = ((cnt + TM_EXPERT - 1) // TM_EXPERT) * TM_EXPERT
    ends = jnp.cumsum(padded)
    starts = ends - padded
    start_of = sum(jnp.where(e == j, starts[j], 0) for j in range(N_EXPERTS))
    dest = (start_of + rank).reshape(-1)
    tile_start = jnp.arange(n_rows // TM_EXPERT, dtype=I32) * TM_EXPERT
    tile_valid = (tile_start < ends[-1]).astype(I32)
    tile_valid = jnp.concatenate([tile_valid, ends[-1:] // TM_EXPERT - 1])
    tile_expert = jnp.sum((tile_start[:, None] >= ends[None, :]).astype(I32), axis=1)
    last_expert = jnp.sum((ends[-1] - 1 >= ends).astype(I32))
    tile_expert = jnp.minimum(tile_expert, last_expert)
    pad_rows = jnp.stack([starts + cnt, padded - cnt], axis=1).reshape(-1)
    used_tiles = ends[-1:] // TM_EXPERT
    pad_rows = jnp.concatenate([pad_rows, used_tiles, n_rows // TM_EXPERT - used_tiles])
    return dest, pad_rows, tile_expert, tile_valid


def _pad_time(a, nseq, t, fill=None):
    w = a.shape[1]
    a = a.reshape(nseq, t, w)
    a = jnp.pad(a, ((0, 0), (0, SAMPLE_PAD_T - t), (0, 0)))
    if fill is not None:
        pad_row = jnp.arange(SAMPLE_PAD_T)[None, :, None] >= t
        lane = jnp.arange(w)[None, None, :]
        a = jnp.where(pad_row & (lane < MLSTM_HEADS), fill, a)
    return a.reshape(nseq * SAMPLE_PAD_T, w)


def _unpad_time(a, nseq, t):
    w = a.shape[1]
    return a.reshape(nseq, SAMPLE_PAD_T, w)[:, :t].reshape(nseq * t, w)


def kernel(x_prompt, x_sample, state_gla_S, state_mlstm_C, state_mlstm_n, state_mlstm_m,
           norm_mix, norm_ffn, norm_final,
           gla_w_in, gla_w_a2, gla_b_a, gla_g_norm, gla_w_out,
           mlstm_w_in, mlstm_b_gate, mlstm_g_norm, mlstm_w_out,
           ffn_w_gu, ffn_w_down,
           moe_w_router, moe_b_router, moe_w_gu, moe_w_down):
    bp, tp, d = x_prompt.shape
    bs, ts, _ = x_sample.shape
    n_p = bp * tp
    n_s = bs * ts
    n = n_p + n_s
    assert n_p % TM == 0 and n_s % TM == 0 and tp % (2 * CHUNK) == 0 and ts <= SAMPLE_PAD_T
    assert norm_mix.shape[0] == 2, "one GLA layer followed by one mLSTM layer"
    qk_w = d // 2
    main_w = 2 * qk_w + 2 * d

    x_p = x_prompt.reshape(n_p, d)
    x_s = x_sample.reshape(n_s, d)

    w_in = gla_w_in[0]
    w_lr = jnp.pad(w_in[:, main_w:], ((0, 0), (0, LANES - GLA_RANK))).astype(BF16)
    w_a2 = jnp.pad(gla_w_a2[0], ((0, LANES - GLA_RANK), (0, 0))).astype(BF16)
    z, log_a = _gla_in_proj(x_p, x_s, norm_mix[0][None], w_in[:, :main_w].astype(BF16), w_lr, w_a2,
                            gla_b_a[0][None])
    s0_p = jnp.zeros((bp,) + state_gla_S.shape[2:], F32)
    u_p, s_p = _gla_recurrence(z, log_a, s0_p, gla_g_norm[0], bp, tp, CHUNK, 16 * CHUNK)
    u_s, s_s = _gla_recurrence_short(_pad_time(z[n_p:], bs, ts), _pad_time(log_a[n_p:], bs, ts),
                                     state_gla_S[0], gla_g_norm[0], bs, SAMPLE_PAD_T,
                                     SAMPLE_SEQS_PER_STEP)
    x, hn = _out_proj(u_p, _unpad_time(u_s, bs, ts), x_p, x_s, gla_w_out[0].astype(BF16),
                      norm_ffn[0][None])
    x, hn, moe_wgu = _swiglu(hn, x, ffn_w_gu[0].astype(BF16), ffn_w_down[0].astype(BF16),
                             norm_mix[1][None], SWIGLU_CHUNKS,
                             moe_w_gu[0].reshape(N_EXPERTS * d, moe_w_gu.shape[3]))

    w_in = mlstm_w_in[0]
    w_gt = jnp.pad(w_in[:, main_w:], ((0, 0), (0, LANES - 2 * MLSTM_HEADS))).astype(BF16)
    b_gt = jnp.pad(mlstm_b_gate[0], (0, LANES - 2 * MLSTM_HEADS))[None]
    dff_e = moe_w_down.shape[2]
    z, gates, moe_wd = _mlstm_in_proj(hn, w_in[:, :main_w].astype(BF16), w_gt, b_gt,
                                      moe_w_down[0].reshape(N_EXPERTS * dff_e, d))
    pairs = MLSTM_HEADS // 2
    dk2 = 2 * state_mlstm_C.shape[3]
    dvm = state_mlstm_C.shape[4]
    c0_p = jnp.zeros((bp, pairs, dk2, dvm), F32)
    n0_p = jnp.zeros((bp, pairs, dk2), F32)
    m0_p = jnp.zeros((bp, 1, LANES), F32)
    u_p, c_p, nn_p, m_p = _mlstm_recurrence_blocked(z, gates, c0_p, n0_p, m0_p,
                                                    mlstm_g_norm[0], bp, tp, CHUNK, 16 * CHUNK)
    c0_s = state_mlstm_C[0]
    n0_s = state_mlstm_n[0].reshape(bs, pairs, dk2)
    m0_s = jnp.pad(state_mlstm_m[0], ((0, 0), (0, LANES - MLSTM_HEADS)))[:, None, :]
    u_s, c_s, nn_s, m_s = _mlstm_recurrence_short(
        _pad_time(z[n_p:], bs, ts), _pad_time(gates[n_p:], bs, ts, fill=NEG_BIG),
        c0_s, n0_s, m0_s, mlstm_g_norm[0], bs, SAMPLE_PAD_T, SAMPLE_SEQS_PER_STEP)

    w_r = jnp.pad(moe_w_router[0], ((0, 0), (0, LANES - N_EXPERTS)))
    w_r_hi = w_r.astype(BF16)
    w_r = jnp.stack([w_r_hi, (w_r - w_r_hi.astype(F32)).astype(BF16)])
    b_r = jnp.pad(moe_b_router[0], (0, LANES - N_EXPERTS))[None]
    x, hn, route_g, route_i, counts = _out_proj_router(
        u_p, _unpad_time(u_s, bs, ts), x, mlstm_w_out[0].astype(BF16), norm_ffn[1][None], w_r, b_r)
    n_rows = -(-(2 * n + N_EXPERTS * (TM_EXPERT - 1)) // TM_EXPERT) * TM_EXPERT
    dest, pad_rows, tile_expert, tile_valid = _routing_tables(route_i, counts, n_rows)
    xs = _dispatch(dest, pad_rows, hn, n_rows)
    ys = _experts(tile_expert, tile_valid, xs, moe_wgu.reshape(moe_w_gu.shape[1:]),
                  moe_wd.reshape(moe_w_down.shape[1:]))
    y_p, y_s = _combine(dest, ys, x, route_g, norm_final[None], n_p)

    y_prompt = y_p.reshape(bp, tp, d)
    y_sample = y_s.reshape(bs, ts, d)
    hd = state_mlstm_C.shape[2:]
    return (y_prompt, y_sample,
            s_p[None], c_p.reshape((1, bp) + hd), nn_p.reshape(1, bp, hd[0], hd[1]),
            m_p[None, :, 0, :MLSTM_HEADS],
            s_s[None], c_s[None], nn_s.reshape(1, bs, hd[0], hd[1]), m_s[None, :, 0, :MLSTM_HEADS])


def _mlstm_in_proj(hn, w_main, w_gate, b_gate, cast_src):
    n, d = hn.shape
    wz = w_main.shape[1]
    cast_in, cast_out, cast_shape = _cast_side_job(cast_src, n // TM)

    def kern(h_ref, w_ref, wg_ref, b_ref, cast_ref, z_ref, gate_ref, cast_out_ref):
        cast_out_ref[...] = cast_ref[...].astype(BF16)
        h = h_ref[...]
        width = wz // IN_PROJ_COL_CHUNKS
        for c in range(IN_PROJ_COL_CHUNKS):
            cs = slice(c * width, (c + 1) * width)
            z_ref[:, cs] = _dot(h, w_ref[:, cs])
        gp = _dot(h, wg_ref[...]) + b_ref[...]
        gc = GATE_CAP * jnp.tanh(gp * (1.0 / GATE_CAP))
        lane = lax.broadcasted_iota(I32, gc.shape, 1)
        out = jnp.where(lane < MLSTM_HEADS, gc, _log_sigmoid(gc))
        gate_ref[...] = jnp.where(lane < 2 * MLSTM_HEADS, out, 0.0)

    return pl.pallas_call(
        kern,
        grid=(n // TM,),
        in_specs=[
            pl.BlockSpec((TM, d), lambda i: (i, 0)),
            _resident_spec((d, wz), lambda i: (0, 0)),
            _resident_spec((d, LANES), lambda i: (0, 0)),
            pl.BlockSpec((1, LANES), lambda i: (0, 0)),
            cast_in,
        ],
        out_specs=[pl.BlockSpec((TM, wz), lambda i: (i, 0)), pl.BlockSpec((TM, LANES), lambda i: (i, 0)),
                   cast_out],
        out_shape=[jax.ShapeDtypeStruct((n, wz), F32), jax.ShapeDtypeStruct((n, LANES), F32),
                   cast_shape],
        compiler_params=_params(("arbitrary",)),
        name="in_proj_mlstm",
    )(hn, w_main, w_gate, b_gate, cast_src)
```

```python
import functools

import jax
import jax.numpy as jnp
import numpy as np
from jax import lax
from jax.experimental import pallas as pl
from jax.experimental.pallas import tpu as pltpu

F32 = jnp.float32
BF16 = jnp.bfloat16
I32 = jnp.int32

EPS = 1e-6
GLA_HEADS = 4
GLA_RANK = 16
GLA_TAU = 16.0
MLSTM_HEADS = 8
GATE_CAP = 15.0
CHUNK = 64
N_EXPERTS = 8
NEG_BIG = -1e30

LANES = 128
TM = 512
TM_EXPERT = 512
SAMPLE_PAD_T = 8
SAMPLE_SEQS_PER_STEP = 16
SWIGLU_CHUNKS = 11
VMEM_LIMIT = 56 * 1024 * 1024


def _dot(a, b):
    return jnp.dot(a, b, preferred_element_type=F32)


def _dot_nt(a, b):
    return lax.dot_general(a, b, (((1,), (1,)), ((), ())), preferred_element_type=F32)


def _dot_tn(a, b):
    return lax.dot_general(a, b, (((0,), (0,)), ((), ())), preferred_element_type=F32)


def _sigmoid(x):
    return 1.0 / (1.0 + jnp.exp(-x))


def _log_sigmoid(x):
    return jnp.minimum(x, 0.0) - jnp.log1p(jnp.exp(-jnp.abs(x)))


def _rms(x, g):
    return x * lax.rsqrt(jnp.mean(x * x, axis=-1, keepdims=True) + EPS) * g


def _cumsum_rows(x, n):
    ridx = lax.broadcasted_iota(I32, x.shape, 0)
    s = 1
    while s < n:
        x = x + jnp.where(ridx >= s, pltpu.roll(x, s, 0), 0.0)
        s *= 2
    return x


def _params(sem):
    return pltpu.CompilerParams(dimension_semantics=sem, vmem_limit_bytes=VMEM_LIMIT)


def _resident_spec(block_shape, index_map):
    return pl.BlockSpec(block_shape, index_map, pipeline_mode=pl.Buffered(1))


def _cast_side_job(src, n_steps):
    rows, cols = src.shape
    n_slabs = n_steps - 1
    assert rows % (16 * n_slabs) == 0
    spec = pl.BlockSpec((rows // n_slabs, cols), lambda i, *_: (jnp.minimum(i, n_slabs - 1), 0))
    return spec, spec, jax.ShapeDtypeStruct(src.shape, BF16)


IN_PROJ_COL_CHUNKS = 4


def _pick_rows(i, n_first, first_ref, second_ref):
    return jnp.where(i < n_first, first_ref[...], second_ref[...])


def _split_specs(n_first, width):
    return [pl.BlockSpec((TM, width), lambda i, *_: (jnp.minimum(i, n_first - 1), 0)),
            pl.BlockSpec((TM, width), lambda i, *_: (jnp.maximum(i - n_first, 0), 0))]


def _gla_in_proj_kernel(xp_ref, xs_ref, g_ref, w_ref, wg_ref, w2_ref, b_ref, z_ref, gate_ref, *,
                        n_first):
    x = _pick_rows(pl.program_id(0), n_first, xp_ref, xs_ref)
    hn = _rms(x, g_ref[...]).astype(BF16)
    width = w_ref.shape[1] // IN_PROJ_COL_CHUNKS
    for c in range(IN_PROJ_COL_CHUNKS):
        cs = slice(c * width, (c + 1) * width)
        z_ref[:, cs] = _dot(hn, w_ref[:, cs])
    a = _dot(hn, wg_ref[...])
    la = _dot(a.astype(BF16), w2_ref[...]) + b_ref[...]
    gate_ref[...] = _log_sigmoid(la) * (1.0 / GLA_TAU)


def _gla_in_proj(x_p, x_s, g, w_main, w_gate, w2, b):
    d = x_p.shape[1]
    n = x_p.shape[0] + x_s.shape[0]
    n_first = x_p.shape[0] // TM
    wz = w_main.shape[1]
    gw = b.shape[1]
    return pl.pallas_call(
        functools.partial(_gla_in_proj_kernel, n_first=n_first),
        grid=(n // TM,),
        in_specs=_split_specs(n_first, d) + [
            pl.BlockSpec((1, d), lambda i: (0, 0)),
            pl.BlockSpec((d, wz), lambda i: (0, 0)),
            pl.BlockSpec((d, LANES), lambda i: (0, 0)),
            pl.BlockSpec(w2.shape, lambda i: (0, 0)),
            pl.BlockSpec((1, gw), lambda i: (0, 0)),
        ],
        out_specs=[
            pl.BlockSpec((TM, wz), lambda i: (i, 0)),
            pl.BlockSpec((TM, gw), lambda i: (i, 0)),
        ],
        out_shape=[jax.ShapeDtypeStruct((n, wz), F32), jax.ShapeDtypeStruct((n, gw), F32)],
        compiler_params=_params(("arbitrary",)),
        name="in_proj_gla",
    )(x_p, x_s, g, w_main, w_gate, w2, b)


def _gla_kernel(q_ref, k_ref, v_ref, r_ref, la_ref, s0_ref, gn_ref, u_ref, st_ref, st_sc,
                *, chunk, tb, nblk):
    heads = GLA_HEADS
    dk = q_ref.shape[1] // heads
    dv = v_ref.shape[1] // heads
    c = pl.program_id(1)

    @pl.when(c == 0)
    def _():
        for h in range(heads):
            st_sc[h] = s0_ref[0, h].T

    ri = lax.broadcasted_iota(I32, (chunk, chunk), 0)
    ci = lax.broadcasted_iota(I32, (chunk, chunk), 1)
    causal = ri >= ci
    kscale = dk ** -0.5
    for j in range(tb // chunk):
        rows = slice(j * chunk, (j + 1) * chunk)
        b_all = _cumsum_rows(la_ref[rows, :], chunk)
        for h in range(heads):
            ks = slice(h * dk, (h + 1) * dk)
            vs = slice(h * dv, (h + 1) * dv)
            b = b_all[:, ks]
            b_end = b[chunk - 1:chunk, :]
            q = q_ref[rows, ks]
            k = k_ref[rows, ks] * kscale
            qg = (q * jnp.exp(b)).astype(BF16)
            kg = (k * jnp.exp(-b)).astype(BF16)
            ke = (k * jnp.exp(b_end - b)).astype(BF16)
            v = v_ref[rows, vs].astype(BF16)
            a = jnp.where(causal, _dot_nt(qg, kg), 0.0).astype(BF16)
            st = st_sc[h]
            o = _dot_nt(qg, st.astype(BF16)) + _dot(a, v)
            st_sc[h] = st * jnp.exp(b_end) + _dot_tn(v, ke)
            on = _rms(o, gn_ref[h:h + 1, :])
            r = r_ref[rows, vs]
            u_ref[rows, vs] = (r * _sigmoid(r) * on).astype(BF16)

    @pl.when(c == nblk - 1)
    def _():
        for h in range(heads):
            st_ref[0, h] = st_sc[h].T


def _gla_recurrence(z, log_a, s0, g_norm, nseq, t, chunk, tb):
    qk_w = log_a.shape[1]
    v_w = 2 * qk_w
    nblk = t // tb
    heads, dk, dv = s0.shape[1:]
    kern = functools.partial(_gla_kernel, chunk=chunk, tb=tb, nblk=nblk)
    row = lambda b, c: b * nblk + c
    return pl.pallas_call(
        kern,
        grid=(nseq, nblk),
        in_specs=[
            pl.BlockSpec((tb, qk_w), lambda b, c: (row(b, c), 0)),
            pl.BlockSpec((tb, qk_w), lambda b, c: (row(b, c), 1)),
            pl.BlockSpec((tb, v_w), lambda b, c: (row(b, c), 1)),
            pl.BlockSpec((tb, v_w), lambda b, c: (row(b, c), 2)),
            pl.BlockSpec((tb, qk_w), lambda b, c: (row(b, c), 0)),
            pl.BlockSpec((1, heads, dk, dv), lambda b, c: (b, 0, 0, 0)),
            pl.BlockSpec((heads, dv), lambda b, c: (0, 0)),
        ],
        out_specs=[
            pl.BlockSpec((tb, v_w), lambda b, c: (row(b, c), 0)),
            pl.BlockSpec((1, heads, dk, dv), lambda b, c: (b, 0, 0, 0)),
        ],
        out_shape=[
            jax.ShapeDtypeStruct((nseq * t, v_w), BF16),
            jax.ShapeDtypeStruct(s0.shape, F32),
        ],
        scratch_shapes=[pltpu.VMEM((heads, dv, dk), F32)],
        compiler_params=_params(("arbitrary", "arbitrary")),
        name="gla_recurrence_t%d" % t,
    )(z, z, z, z, log_a, s0, g_norm)


def _gla_short_kernel(q_ref, k_ref, v_ref, r_ref, la_ref, s0_ref, gn_ref, u_ref, st_ref, *, t, nsq):
    heads = GLA_HEADS
    dk = q_ref.shape[1] // heads
    dv = v_ref.shape[1] // heads
    nrow = nsq * t
    width = q_ref.shape[1]
    rc = lax.broadcasted_iota(I32, (nrow, width), 0) & (t - 1)
    b = _chunk_scan(la_ref[...], rc, t, jnp.add, 0.0)
    b_last = b.reshape(nsq, t, width)[:, t - 1:t, :]
    b_end = jnp.broadcast_to(b_last, (nsq, t, width)).reshape(nrow, width)
    k = k_ref[...] * (dk ** -0.5)
    qg = q_ref[...] * jnp.exp(b)
    kg = k * jnp.exp(-b)
    ke = k * jnp.exp(b_end - b)
    decay = jnp.exp(b_end)
    ri = lax.broadcasted_iota(I32, (nrow, nrow), 0)
    ci = lax.broadcasted_iota(I32, (nrow, nrow), 1)
    shift = t.bit_length() - 1
    mask = ((ri >> shift) == (ci >> shift)) & (ri >= ci)
    for h in range(heads):
        ks = slice(h * dk, (h + 1) * dk)
        vs = slice(h * dv, (h + 1) * dv)
        a = jnp.where(mask, _dot_nt(qg[:, ks].astype(BF16), kg[:, ks].astype(BF16)), 0.0)
        o = _dot(a.astype(BF16), v_ref[:, vs].astype(BF16))
        decay_t = decay[:, ks].T
        o_state = []
        for sq in range(nsq):
            r = slice(sq * t, (sq + 1) * t)
            s0 = s0_ref[sq, h]
            o_state.append(_dot(qg[r, ks].astype(BF16), s0.astype(BF16)))
            upd = _dot_tn(ke[r, ks].astype(BF16), v_ref[r, vs].astype(BF16))
            st_ref[sq, h] = decay_t[:, sq * t:sq * t + 1] * s0 + upd
        on = _rms(o + jnp.concatenate(o_state, axis=0), gn_ref[h:h + 1, :])
        rg = r_ref[:, vs]
        u_ref[:, vs] = (rg * _sigmoid(rg) * on).astype(BF16)


def _gla_recurrence_short(z, log_a, s0, g_norm, nseq, t, nsq):
    qk_w = log_a.shape[1]
    v_w = 2 * qk_w
    heads, dk, dv = s0.shape[1:]
    assert nsq * t == dk == LANES
    rb = nsq * t
    return pl.pallas_call(
        functools.partial(_gla_short_kernel, t=t, nsq=nsq),
        grid=(nseq // nsq,),
        in_specs=[
            pl.BlockSpec((rb, qk_w), lambda b: (b, 0)),
            pl.BlockSpec((rb, qk_w), lambda b: (b, 1)),
            pl.BlockSpec((rb, v_w), lambda b: (b, 1)),
            pl.BlockSpec((rb, v_w), lambda b: (b, 2)),
            pl.BlockSpec((rb, qk_w), lambda b: (b, 0)),
            pl.BlockSpec((nsq, heads, dk, dv), lambda b: (b, 0, 0, 0)),
            pl.BlockSpec((heads, dv), lambda b: (0, 0)),
        ],
        out_specs=[
            pl.BlockSpec((rb, v_w), lambda b: (b, 0)),
            pl.BlockSpec((nsq, heads, dk, dv), lambda b: (b, 0, 0, 0)),
        ],
        out_shape=[
            jax.ShapeDtypeStruct((nseq * t, v_w), BF16),
            jax.ShapeDtypeStruct(s0.shape, F32),
        ],
        compiler_params=_params(("arbitrary",)),
        name="gla_recurrence_t%d" % t,
    )(z, z, z, z, log_a, s0, g_norm)


def _mlstm_short_kernel(q_ref, k_ref, v_ref, op_ref, gt_ref, e_ref, c0_ref, n0_ref, m0_ref, gn_ref,
                        u_ref, ct_ref, nt_ref, mt_ref, *, t, nsq):
    heads = MLSTM_HEADS
    pairs = heads // 2
    dv = v_ref.shape[1] // heads
    dk = q_ref.shape[1] // heads
    nrow = nsq * t

    def per_seq(x3):
        return jnp.broadcast_to(x3, (nsq, t, x3.shape[2])).reshape(nrow, x3.shape[2])

    def last(x):
        return x.reshape(nsq, t, x.shape[1])[:, t - 1:t, :]

    rc = lax.broadcasted_iota(I32, (nrow, LANES), 0) & (t - 1)
    gts = gt_ref[...]
    f_cum = pltpu.roll(_chunk_scan(gts, rc, t, jnp.add, 0.0), LANES - heads, 1)
    b = gts - f_cum
    cmb = _chunk_scan(b, rc, t, jnp.maximum, NEG_BIG)
    mp = per_seq(m0_ref[...])
    big_m = jnp.maximum(mp, cmb)
    m_t = f_cum + big_m
    m_new = last(m_t)
    mt_ref[...] = m_new
    mn = per_seq(m_new)
    fe = per_seq(last(f_cum))
    w_i = jnp.exp(mp - big_m)
    em = jnp.exp(-m_t)
    k_sc = jnp.exp(fe - f_cum + gts - mn)
    w_c = jnp.exp(fe + mp - mn)
    ex = _dot(_pack_terms([w_i, k_sc, -big_m, em, w_c], heads), e_ref[...])
    offs = np.cumsum([0] + [heads * w for w in SHORT_EXPAND_WIDTHS])
    q = q_ref[...]
    k = k_ref[...] * (dk ** -0.5)
    qs = q * ex[:, offs[0]:offs[1]]
    ke = k * ex[:, offs[1]:offs[2]]
    neg_m = ex[:, offs[2]:offs[3]]
    em_v = ex[:, offs[3]:offs[4]]
    wc_k = ex[:, offs[4]:offs[5]]
    bt = b.T

    ri = lax.broadcasted_iota(I32, (nrow, nrow), 0)
    ci = lax.broadcasted_iota(I32, (nrow, nrow), 1)
    shift = t.bit_length() - 1
    mask = ((ri >> shift) == (ci >> shift)) & (ri >= ci)
    lane1 = lax.broadcasted_iota(I32, (1, LANES), 1)
    rowi = lax.broadcasted_iota(I32, (LANES, LANES), 0)
    ones_v = jnp.ones((nrow, dv), BF16)
    ones_k = jnp.ones((LANES, dv), BF16)
    for p in range(pairs):
        ps = slice(p * LANES, (p + 1) * LANES)
        qp = q[:, ps]
        qsp = qs[:, ps]
        kpb = k[:, ps].astype(BF16)
        n_rows = per_seq(n0_ref[:, p:p + 1, :])
        intra, den_state, qsh = [], [], []
        for jj in range(2):
            h = 2 * p + jj
            hs = slice(h * dv, (h + 1) * dv)
            mine = (lane1 >= jj * dk) & (lane1 < (jj + 1) * dk)
            qh = jnp.where(mine, qp, 0.0).astype(BF16)
            qsh.append(jnp.where(mine, qsp, 0.0))
            d = neg_m[:, hs] + bt[h:h + 1, :]
            s = _dot_nt(qh, kpb) * jnp.where(mask, jnp.exp(d), 0.0)
            vh = jnp.concatenate([v_ref[:, hs].astype(BF16), ones_v], axis=1)
            intra.append(_dot(s.astype(BF16), vh))
            den_state.append(_dot((qsh[jj] * n_rows).astype(BF16), ones_k))
        num_state = [[], []]
        for sq in range(nsq):
            r = slice(sq * t, (sq + 1) * t)
            lhs = jnp.concatenate([qsh[0][r], qsh[1][r]], axis=0).astype(BF16)
            c_pair = jnp.concatenate([c0_ref[sq, 2 * p], c0_ref[sq, 2 * p + 1]], axis=0)
            res = _dot(lhs, c_pair.astype(BF16))
            num_state[0].append(res[:t])
            num_state[1].append(res[t:])
        for jj in range(2):
            h = 2 * p + jj
            hs = slice(h * dv, (h + 1) * dv)
            num = intra[jj][:, :dv] + jnp.concatenate(num_state[jj], axis=0)
            den = intra[jj][:, dv:] + den_state[jj]
            hh = num / jnp.maximum(jnp.abs(den), em_v[:, hs])
            hn = _rms(hh, gn_ref[h:h + 1, :])
            u_ref[:, hs] = (_sigmoid(op_ref[:, hs]) * hn).astype(BF16)
        kep = ke[:, ps]
        for sq in range(nsq):
            r = slice(sq * t, (sq + 1) * t)
            vp = v_ref[r, 2 * p * dv:(2 * p + 2) * dv].astype(BF16)
            full = _dot_tn(kep[r].astype(BF16), vp)
            upd = jnp.where(rowi < dk, full[:, :dv], full[:, dv:])
            w_row = w_c[sq * t:sq * t + 1, :]
            w_col = jnp.where(rowi < dk, w_row[:, 2 * p:2 * p + 1], w_row[:, 2 * p + 1:2 * p + 2])
            c_pair = jnp.concatenate([c0_ref[sq, 2 * p], c0_ref[sq, 2 * p + 1]], axis=0)
            c_new = w_col * c_pair + upd
            ct_ref[sq, 2 * p] = c_new[:dk]
            ct_ref[sq, 2 * p + 1] = c_new[dk:]
        k_sum = jnp.sum(kep.reshape(nsq, t, LANES), axis=1, keepdims=True)
        nt_ref[:, p:p + 1, :] = last(wc_k[:, ps]) * n0_ref[:, p:p + 1, :] + k_sum


def _mlstm_recurrence_short(z, gates, c0, n0, m0, g_norm, nseq, t, nsq):
    heads = g_norm.shape[0]
    v_w = heads * g_norm.shape[1]
    qk_w = v_w // 2
    assert nsq * t == LANES
    expand = _expand_matrix(heads, SHORT_EXPAND_WIDTHS)
    kern = functools.partial(_mlstm_short_kernel, t=t, nsq=nsq)
    rb = nsq * t
    st4 = lambda b: (b, 0, 0, 0)
    st3 = lambda b: (b, 0, 0)
    return pl.pallas_call(
        kern,
        grid=(nseq // nsq,),
        in_specs=[
            pl.BlockSpec((rb, qk_w), lambda b: (b, 0)),
            pl.BlockSpec((rb, qk_w), lambda b: (b, 1)),
            pl.BlockSpec((rb, v_w), lambda b: (b, 1)),
            pl.BlockSpec((rb, v_w), lambda b: (b, 2)),
            pl.BlockSpec((rb, LANES), lambda b: (b, 0)),
            pl.BlockSpec(expand.shape, lambda b: (0, 0)),
            pl.BlockSpec((nsq,) + c0.shape[1:], st4),
            pl.BlockSpec((nsq,) + n0.shape[1:], st3),
            pl.BlockSpec((nsq,) + m0.shape[1:], st3),
            pl.BlockSpec(g_norm.shape, lambda b: (0, 0)),
        ],
        out_specs=[
            pl.BlockSpec((rb, v_w), lambda b: (b, 0)),
            pl.BlockSpec((nsq,) + c0.shape[1:], st4),
            pl.BlockSpec((nsq,) + n0.shape[1:], st3),
            pl.BlockSpec((nsq,) + m0.shape[1:], st3),
        ],
        out_shape=[
            jax.ShapeDtypeStruct((nseq * t, v_w), BF16),
            jax.ShapeDtypeStruct(c0.shape, F32),
            jax.ShapeDtypeStruct(n0.shape, F32),
            jax.ShapeDtypeStruct(m0.shape, F32),
        ],
        compiler_params=_params(("arbitrary",)),
        name="mlstm_recurrence_t%d" % t,
    )(z, z, z, z, gates, expand, c0, n0, m0, g_norm)


N_SPLIT = 3
EXPAND_WIDTHS = (64, 64, 128, 128)
SHORT_EXPAND_WIDTHS = EXPAND_WIDTHS + (64,)


def _expand_matrix(heads, widths):
    assert len(widths) * N_SPLIT * heads <= LANES
    cols = []
    for qi, width in enumerate(widths):
        sel = np.zeros((LANES, heads * width), np.float32)
        for term in range(N_SPLIT):
            for h in range(heads):
                sel[(qi * N_SPLIT + term) * heads + h, h * width:(h + 1) * width] = 1.0
        cols.append(sel)
    return jnp.asarray(np.concatenate(cols, axis=1), BF16)


def _chunk_scan(x, rc, n, op, fill):
    s = 1
    while s < n:
        x = op(x, jnp.where(rc >= s, pltpu.roll(x, s, 0), fill))
        s *= 2
    return x


def _pack_terms(values, heads):
    lane = lax.broadcasted_iota(I32, values[0].shape, 1)
    packed = jnp.zeros(values[0].shape, F32)
    slot = 0
    for val in values:
        rest = val
        for term in range(N_SPLIT):
            part = rest.astype(BF16).astype(F32)
            rest = rest - part
            moved = part if slot == 0 else pltpu.roll(part, slot * heads, 1)
            packed = jnp.where((lane >= slot * heads) & (lane < (slot + 1) * heads), moved, packed)
            slot += 1
    return packed.astype(BF16)


def _mlstm_block_kernel(q_ref, k_ref, v_ref, op_ref, gt_ref, e_ref, c0_ref, n0_ref, m0_ref, gn_ref,
                        u_ref, ct_ref, nt_ref, mt_ref, c_sc, nb_sc, m_sc, *, chunk, tb, nblk):
    heads = MLSTM_HEADS
    pairs = heads // 2
    dv = v_ref.shape[1] // heads
    dk = q_ref.shape[1] // heads
    c = pl.program_id(1)

    @pl.when(c == 0)
    def _():
        c_sc[...] = c0_ref[0]
        for p in range(pairs):
            nb_sc[p] = jnp.broadcast_to(n0_ref[0, p:p + 1, :], (LANES, LANES)).T
        m_sc[...] = m0_ref[0]

    nch = tb // chunk
    rc = lax.broadcasted_iota(I32, (tb, LANES), 0) & (chunk - 1)
    gts = gt_ref[...]
    f_cum = pltpu.roll(_chunk_scan(gts, rc, chunk, jnp.add, 0.0), LANES - heads, 1)
    b = gts - f_cum
    cmb = _chunk_scan(b, rc, chunk, jnp.maximum, NEG_BIG)
    m_prev = m_sc[...]
    mp_rows, mn_rows, fe_rows, w_c = [], [], [], []
    for j in range(nch):
        last = (j + 1) * chunk - 1
        f_end = f_cum[last:last + 1, :]
        m_new = f_end + jnp.maximum(m_prev, cmb[last:last + 1, :])
        w_c.append(jnp.exp(f_end + m_prev - m_new))
        mp_rows.append(jnp.broadcast_to(m_prev, (chunk, LANES)))
        mn_rows.append(jnp.broadcast_to(m_new, (chunk, LANES)))
        fe_rows.append(jnp.broadcast_to(f_end, (chunk, LANES)))
        m_prev = m_new
    m_sc[...] = m_prev
    mp = jnp.concatenate(mp_rows, axis=0)
    mn = jnp.concatenate(mn_rows, axis=0)
    fe = jnp.concatenate(fe_rows, axis=0)
    big_m = jnp.maximum(mp, cmb)
    w_i = jnp.exp(mp - big_m)
    em = jnp.exp(-(f_cum + big_m))
    k_sc = jnp.exp(fe - f_cum + gts - mn)
    ex = _dot(_pack_terms([w_i, k_sc, -big_m, em], heads), e_ref[...])
    o1 = heads * EXPAND_WIDTHS[0]
    o2 = o1 + heads * EXPAND_WIDTHS[1]
    o3 = o2 + heads * EXPAND_WIDTHS[2]
    q = q_ref[...]
    k = k_ref[...] * (dk ** -0.5)
    qs = q * ex[:, :o1]
    ke = k * ex[:, o1:o2]
    neg_m = ex[:, o2:o3]
    em_v = ex[:, o3:]
    bt = b.T

    ri = lax.broadcasted_iota(I32, (chunk, chunk), 0)
    ci = lax.broadcasted_iota(I32, (chunk, chunk), 1)
    causal = ri >= ci
    lane1 = lax.broadcasted_iota(I32, (1, LANES), 1)
    rowi = lax.broadcasted_iota(I32, (LANES, LANES), 0)
    ones_v = jnp.ones((chunk, dv), BF16)
    for j in range(nch):
        rows = slice(j * chunk, (j + 1) * chunk)
        for p in range(pairs):
            ps = slice(p * LANES, (p + 1) * LANES)
            qp = q[rows, ps]
            qsp = qs[rows, ps]
            kpb = k[rows, ps].astype(BF16)
            cp = c_sc[p]
            nb = nb_sc[p]
            state = jnp.concatenate([cp, nb], axis=1).astype(BF16)
            for jj in range(2):
                h = 2 * p + jj
                hs = slice(h * dv, (h + 1) * dv)
                mine = (lane1 >= jj * dk) & (lane1 < (jj + 1) * dk)
                qh = jnp.where(mine, qp, 0.0).astype(BF16)
                qsh = jnp.where(mine, qsp, 0.0).astype(BF16)
                d = neg_m[rows, h * dv:h * dv + chunk] + bt[h:h + 1, j * chunk:(j + 1) * chunk]
                s = _dot_nt(qh, kpb) * jnp.where(causal, jnp.exp(d), 0.0)
                vh = jnp.concatenate([v_ref[rows, hs].astype(BF16), ones_v], axis=1)
                out = _dot(qsh, state) + _dot(s.astype(BF16), vh)
                hh = out[:, :dv] / jnp.maximum(jnp.abs(out[:, dv:]), em_v[rows, hs])
                hn = _rms(hh, gn_ref[h:h + 1, :])
                u_ref[rows, hs] = (_sigmoid(op_ref[rows, hs]) * hn).astype(BF16)
            vp = jnp.concatenate([v_ref[rows, 2 * p * dv:(2 * p + 2) * dv].astype(BF16), ones_v], axis=1)
            full = _dot_tn(ke[rows, ps].astype(BF16), vp)
            upd = jnp.where(rowi < dk, full[:, :dv], full[:, dv:2 * dv])
            w_col = jnp.where(rowi < dk, w_c[j][:, 2 * p:2 * p + 1], w_c[j][:, 2 * p + 1:2 * p + 2])
            c_sc[p] = w_col * cp + upd
            nb_sc[p] = w_col * nb + full[:, 2 * dv:]

    @pl.when(c == nblk - 1)
    def _():
        ct_ref[0] = c_sc[...]
        for p in range(pairs):
            nt_ref[0, p:p + 1, :] = nb_sc[p].T[0:1, :]
        mt_ref[0] = m_sc[...]


def _mlstm_recurrence_blocked(z, gates, c0, n0, m0, g_norm, nseq, t, chunk, tb):
    heads = g_norm.shape[0]
    v_w = heads * g_norm.shape[1]
    qk_w = v_w // 2
    nblk = t // tb
    expand = _expand_matrix(heads, EXPAND_WIDTHS)
    kern = functools.partial(_mlstm_block_kernel, chunk=chunk, tb=tb, nblk=nblk)
    row = lambda b, c: b * nblk + c
    st4 = lambda b, c: (b, 0, 0, 0)
    st3 = lambda b, c: (b, 0, 0)
    return pl.pallas_call(
        kern,
        grid=(nseq, nblk),
        in_specs=[
            pl.BlockSpec((tb, qk_w), lambda b, c: (row(b, c), 0)),
            pl.BlockSpec((tb, qk_w), lambda b, c: (row(b, c), 1)),
            pl.BlockSpec((tb, v_w), lambda b, c: (row(b, c), 1)),
            pl.BlockSpec((tb, v_w), lambda b, c: (row(b, c), 2)),
            pl.BlockSpec((tb, LANES), lambda b, c: (row(b, c), 0)),
            pl.BlockSpec(expand.shape, lambda b, c: (0, 0)),
            pl.BlockSpec((1,) + c0.shape[1:], st4),
            pl.BlockSpec((1,) + n0.shape[1:], st3),
            pl.BlockSpec((1,) + m0.shape[1:], st3),
            pl.BlockSpec(g_norm.shape, lambda b, c: (0, 0)),
        ],
        out_specs=[
            pl.BlockSpec((tb, v_w), lambda b, c: (row(b, c), 0)),
            pl.BlockSpec((1,) + c0.shape[1:], st4),
            pl.BlockSpec((1,) + n0.shape[1:], st3),
            pl.BlockSpec((1,) + m0.shape[1:], st3),
        ],
        out_shape=[
            jax.ShapeDtypeStruct((nseq * t, v_w), BF16),
            jax.ShapeDtypeStruct(c0.shape, F32),
            jax.ShapeDtypeStruct(n0.shape, F32),
            jax.ShapeDtypeStruct(m0.shape, F32),
        ],
        scratch_shapes=[
            pltpu.VMEM(c0.shape[1:], F32),
            pltpu.VMEM(c0.shape[1:], F32),
            pltpu.VMEM(m0.shape[1:], F32),
        ],
        compiler_params=_params(("arbitrary", "arbitrary")),
        name="mlstm_recurrence_t%d" % t,
    )(z, z, z, z, gates, expand, c0, n0, m0, g_norm)


def _out_proj_kernel(up_ref, us_ref, xp_ref, xs_ref, w_ref, g_ref, xo_ref, hn_ref, *, n_first):
    i = pl.program_id(0)
    u = _pick_rows(i, n_first, up_ref, us_ref)
    x = _pick_rows(i, n_first, xp_ref, xs_ref) + _dot(u, w_ref[...])
    xo_ref[...] = x
    hn_ref[...] = _rms(x, g_ref[...]).astype(hn_ref.dtype)


def _out_proj(u_p, u_s, x_p, x_s, w, g):
    d = x_p.shape[1]
    n = x_p.shape[0] + x_s.shape[0]
    n_first = x_p.shape[0] // TM
    return pl.pallas_call(
        functools.partial(_out_proj_kernel, n_first=n_first),
        grid=(n // TM,),
        in_specs=_split_specs(n_first, u_p.shape[1]) + _split_specs(n_first, d) + [
            pl.BlockSpec(w.shape, lambda i: (0, 0)),
            pl.BlockSpec((1, d), lambda i: (0, 0)),
        ],
        out_specs=[pl.BlockSpec((TM, d), lambda i: (i, 0)), pl.BlockSpec((TM, d), lambda i: (i, 0))],
        out_shape=[jax.ShapeDtypeStruct((n, d), F32), jax.ShapeDtypeStruct((n, d), BF16)],
        compiler_params=_params(("arbitrary",)),
        name="out_proj",
    )(u_p, u_s, x_p, x_s, w, g)


def _out_proj_router_kernel(up_ref, us_ref, x_ref, w_ref, g_ref, wr_ref, br_ref,
                            xo_ref, hn_ref, gate_ref, idx_ref, cnt_ref, *, n_first):
    i = pl.program_id(0)
    u = _pick_rows(i, n_first, up_ref, us_ref)
    x = x_ref[...] + _dot(u, w_ref[...])
    xo_ref[...] = x
    hn = _rms(x, g_ref[...])
    hn_hi = hn.astype(BF16)
    hn_ref[...] = hn_hi
    hn_lo = (hn - hn_hi.astype(F32)).astype(BF16)
    logits = (_dot(hn_hi, wr_ref[0]) + _dot(hn_lo, wr_ref[0]) + _dot(hn_hi, wr_ref[1])
              + br_ref[...])
    tm = logits.shape[0]
    lane = lax.broadcasted_iota(I32, logits.shape, 1)
    valid = lane < N_EXPERTS
    logits = jnp.where(valid, logits, -jnp.inf)
    ex = jnp.exp(logits - jnp.max(logits, axis=-1, keepdims=True))
    probs = jnp.where(valid, ex / jnp.sum(ex, axis=-1, keepdims=True), -1.0)
    p1 = jnp.max(probs, axis=-1, keepdims=True)
    i1 = jnp.min(jnp.where(probs == p1, lane, LANES), axis=-1, keepdims=True)
    rest = jnp.where(lane == i1, -1.0, probs)
    p2 = jnp.max(rest, axis=-1, keepdims=True)
    i2 = jnp.min(jnp.where(rest == p2, lane, LANES), axis=-1, keepdims=True)
    tot = p1 + p2
    onehot = ((lane == i1) | (lane == i2)).astype(BF16)
    rr = lax.broadcasted_iota(I32, (tm, tm), 0)
    cc = lax.broadcasted_iota(I32, (tm, tm), 1)
    strict = (rr > cc).astype(BF16)
    before = _dot(strict, onehot)
    r1 = jnp.sum(jnp.where(lane == i1, before, 0.0), axis=-1, keepdims=True)
    r2 = jnp.sum(jnp.where(lane == i2, before, 0.0), axis=-1, keepdims=True)
    gate_ref[...] = jnp.where(lane == 0, p1 / tot, jnp.where(lane == 1, p2 / tot, 0.0))
    idx_ref[...] = jnp.where(lane == 0, i1,
                             jnp.where(lane == 1, i2,
                                       jnp.where(lane == 2, r1.astype(I32),
                                                 jnp.where(lane == 3, r2.astype(I32), 0))))
    cnt_ref[0] = jnp.sum(onehot.astype(F32), axis=0, keepdims=True)


def _out_proj_router(u_p, u_s, x, w, g, w_router, b_router):
    n, d = x.shape
    n_first = u_p.shape[0] // TM
    row = lambda i: (i, 0)
    fix = lambda i: (0, 0)
    return pl.pallas_call(
        functools.partial(_out_proj_router_kernel, n_first=n_first),
        grid=(n // TM,),
        in_specs=_split_specs(n_first, u_p.shape[1]) + [
            pl.BlockSpec((TM, d), row),
            pl.BlockSpec(w.shape, fix),
            pl.BlockSpec((1, d), fix),
            pl.BlockSpec(w_router.shape, lambda i: (0, 0, 0)),
            pl.BlockSpec((1, LANES), fix),
        ],
        out_specs=[
            pl.BlockSpec((TM, d), row),
            pl.BlockSpec((TM, d), row),
            pl.BlockSpec((TM, LANES), row),
            pl.BlockSpec((TM, LANES), row),
            pl.BlockSpec((1, 1, LANES), lambda i: (i, 0, 0)),
        ],
        out_shape=[
            jax.ShapeDtypeStruct((n, d), F32),
            jax.ShapeDtypeStruct((n, d), BF16),
            jax.ShapeDtypeStruct((n, LANES), F32),
            jax.ShapeDtypeStruct((n, LANES), I32),
            jax.ShapeDtypeStruct((n // TM, 1, LANES), F32),
        ],
        compiler_params=_params(("arbitrary",)),
        name="out_proj_router",
    )(u_p, u_s, x, w, g, w_router, b_router)


def _swiglu_kernel(hn_ref, x_ref, wg_ref, wu_ref, wd_ref, g_ref, cast_ref, xo_ref, ho_ref, cast_out_ref,
                   *, nchunk):
    cast_out_ref[...] = cast_ref[...].astype(BF16)
    hn = hn_ref[...]
    width = wd_ref.shape[0] // nchunk
    x = x_ref[...]
    for c in range(nchunk):
        cs = slice(c * width, (c + 1) * width)
        gate = _dot(hn, wg_ref[:, cs])
        up = _dot(hn, wu_ref[:, cs])
        act = (gate * _sigmoid(gate) * up).astype(BF16)
        x = x + _dot(act, wd_ref[cs, :])
    xo_ref[...] = x
    ho_ref[...] = _rms(x, g_ref[...]).astype(BF16)


def _swiglu(hn, x, w_gu, w_down, g, nchunk, cast_src):
    n, d = x.shape
    dff = w_down.shape[0]
    kern = functools.partial(_swiglu_kernel, nchunk=nchunk)
    row = lambda i: (i, 0)
    cast_in, cast_out, cast_shape = _cast_side_job(cast_src, n // TM)
    return pl.pallas_call(
        kern,
        grid=(n // TM,),
        in_specs=[
            pl.BlockSpec((TM, d), row),
            pl.BlockSpec((TM, d), row),
            _resident_spec((d, dff), lambda i: (0, 0)),
            _resident_spec((d, dff), lambda i: (0, 1)),
            _resident_spec((dff, d), lambda i: (0, 0)),
            pl.BlockSpec((1, d), lambda i: (0, 0)),
            cast_in,
        ],
        out_specs=[pl.BlockSpec((TM, d), row), pl.BlockSpec((TM, d), row), cast_out],
        out_shape=[jax.ShapeDtypeStruct((n, d), F32), jax.ShapeDtypeStruct((n, d), BF16), cast_shape],
        compiler_params=_params(("arbitrary",)),
        name="dense_swiglu",
    )(hn, x, w_gu, w_gu, w_down, g, cast_src)


DMA_UNROLL = 8


def _row_copy(src_ref, src_row, dst_ref, dst_row, sem):
    return pltpu.make_async_copy(src_ref.at[pl.ds(src_row, 1), :], dst_ref.at[pl.ds(dst_row, 1), :], sem)


SUBLANES = 8


def _zero_rows(zero_ref, dst_ref, dst_row, count, limit, sem, wait):
    head = jnp.minimum((-dst_row) & (SUBLANES - 1), count)
    for j in range(SUBLANES - 1):
        @pl.when(j < head)
        def _(j=j):
            copy = _row_copy(zero_ref, 0, dst_ref, dst_row + j, sem)
            copy.wait() if wait else copy.start()

    rest = count - head
    first = dst_row + head
    for bit in range(SUBLANES.bit_length() - 1, limit.bit_length() - 1):
        size = 1 << bit
        done = (rest >> (bit + 1)) << (bit + 1)

        @pl.when(((rest >> bit) & 1) == 1)
        def _(size=size, done=done):
            start = pl.multiple_of(first + done, SUBLANES)
            copy = pltpu.make_async_copy(zero_ref.at[pl.ds(0, size), :],
                                         dst_ref.at[pl.ds(start, size), :], sem)
            copy.wait() if wait else copy.start()


def _dispatch_kernel(dest_ref, pad_ref, hn_ref, xs_ref, zero_sc, sem):
    i = pl.program_id(0)
    tm = hn_ref.shape[0]

    @pl.when(i == 0)
    def _():
        zero_sc[...] = jnp.zeros_like(zero_sc)
        for wait in (False, True):
            for e in range(N_EXPERTS):
                _zero_rows(zero_sc, xs_ref, pad_ref[2 * e], pad_ref[2 * e + 1],
                           zero_sc.shape[0], sem, wait)

        rows = zero_sc.shape[0]

        def tail_copy(j):
            first = (pad_ref[2 * N_EXPERTS] + j) * rows
            return pltpu.make_async_copy(zero_sc, xs_ref.at[pl.ds(first, rows), :], sem)

        def issue_tail(j, carry):
            tail_copy(j).start()
            return carry

        def wait_tail(j, carry):
            tail_copy(j).wait()
            return carry

        lax.fori_loop(0, pad_ref[2 * N_EXPERTS + 1], issue_tail, 0)
        lax.fori_loop(0, pad_ref[2 * N_EXPERTS + 1], wait_tail, 0)

    def issue(blk, carry):
        for j in range(DMA_UNROLL):
            r = blk * DMA_UNROLL + j
            t = i * tm + r
            _row_copy(hn_ref, r, xs_ref, dest_ref[2 * t], sem).start(priority=0)
            _row_copy(hn_ref, r, xs_ref, dest_ref[2 * t + 1], sem).start(priority=1)
        return carry

    lax.fori_loop(0, tm // DMA_UNROLL, issue, 0)
    for _ in range(2):
        pltpu.make_async_copy(hn_ref, xs_ref.at[pl.ds(0, tm), :], sem).wait()


def _dispatch(dest_flat, pad_rows, hn, n_rows):
    n, d = hn.shape
    return pl.pallas_call(
        _dispatch_kernel,
        grid_spec=pltpu.PrefetchScalarGridSpec(
            num_scalar_prefetch=2,
            grid=(n // TM,),
            in_specs=[pl.BlockSpec((TM, d), lambda i, dest, pad: (i, 0))],
            out_specs=pl.BlockSpec(memory_space=pl.ANY),
            scratch_shapes=[pltpu.VMEM((TM_EXPERT, d), F32), pltpu.SemaphoreType.DMA(())],
        ),
        out_shape=jax.ShapeDtypeStruct((n_rows, d), F32),
        compiler_params=_params(("arbitrary",)),
        name="moe_dispatch",
    )(dest_flat, pad_rows, hn)


def _expert_kernel(te_ref, tv_ref, xs_ref, wg_ref, wu_ref, wd_ref, ys_ref, *, nchunk):
    i = pl.program_id(0)

    @pl.when(tv_ref[i] > 0)
    def _():
        x = xs_ref[...].astype(BF16)
        width = wg_ref.shape[2] // nchunk
        acc = None
        for c in range(nchunk):
            cs = slice(c * width, (c + 1) * width)
            gate = _dot(x, wg_ref[0, :, cs])
            up = _dot(x, wu_ref[0, :, cs])
            act = (gate * _sigmoid(gate) * up).astype(BF16)
            part = _dot(act, wd_ref[0, cs, :])
            acc = part if acc is None else acc + part
        ys_ref[...] = acc.astype(BF16).astype(F32)

    @pl.when(tv_ref[i] == 0)
    def _():
        ys_ref[...] = jnp.zeros_like(ys_ref)


def _experts(tile_expert, tile_valid, xs, w_gu, w_down):
    r, d = xs.shape
    n_tiles = r // TM_EXPERT
    dff = w_down.shape[1]
    kern = functools.partial(_expert_kernel, nchunk=7)
    return pl.pallas_call(
        kern,
        grid_spec=pltpu.PrefetchScalarGridSpec(
            num_scalar_prefetch=2,
            grid=(r // TM_EXPERT,),
            in_specs=[
                pl.BlockSpec((TM_EXPERT, d), lambda i, te, tv: (jnp.minimum(i, tv[n_tiles]), 0)),
                pl.BlockSpec((1, d, dff), lambda i, te, tv: (te[i], 0, 0)),
                pl.BlockSpec((1, d, dff), lambda i, te, tv: (te[i], 0, 1)),
                pl.BlockSpec((1, dff, d), lambda i, te, tv: (te[i], 0, 0)),
            ],
            out_specs=pl.BlockSpec((TM_EXPERT, d), lambda i, te, tv: (i, 0)),
        ),
        out_shape=jax.ShapeDtypeStruct((r, d), F32),
        compiler_params=_params(("arbitrary",)),
        name="moe_experts",
    )(tile_expert, tile_valid, xs, w_gu, w_gu, w_down)


def _combine_kernel(dest_ref, ys_ref, x_ref, gate_ref, g_ref, yp_ref, ysm_ref, buf, sem, *, n_first):
    i = pl.program_id(0)
    tm = x_ref.shape[0]

    def issue(blk, carry):
        for j in range(DMA_UNROLL):
            r = blk * DMA_UNROLL + j
            t = i * tm + r
            _row_copy(ys_ref, dest_ref[2 * t], buf.at[0], r, sem).start(priority=0)
            _row_copy(ys_ref, dest_ref[2 * t + 1], buf.at[1], r, sem).start(priority=1)
        return carry

    lax.fori_loop(0, tm // DMA_UNROLL, issue, 0)
    for slot in range(2):
        pltpu.make_async_copy(ys_ref.at[pl.ds(0, tm), :], buf.at[slot], sem).wait()
    gate = gate_ref[...]
    y = x_ref[...] + gate[:, 0:1] * buf[0] + gate[:, 1:2] * buf[1]
    y = _rms(y, g_ref[...])

    @pl.when(i < n_first)
    def _():
        yp_ref[...] = y

    @pl.when(i >= n_first)
    def _():
        ysm_ref[...] = y


def _combine(dest_flat, ys, x, gates, g, n_p):
    n, d = x.shape
    n_first = n_p // TM
    p_spec, s_spec = _split_specs(n_first, d)
    return pl.pallas_call(
        functools.partial(_combine_kernel, n_first=n_first),
        grid_spec=pltpu.PrefetchScalarGridSpec(
            num_scalar_prefetch=1,
            grid=(n // TM,),
            in_specs=[
                pl.BlockSpec(memory_space=pl.ANY),
                pl.BlockSpec((TM, d), lambda i, dest: (i, 0)),
                pl.BlockSpec((TM, LANES), lambda i, dest: (i, 0)),
                pl.BlockSpec((1, d), lambda i, dest: (0, 0)),
            ],
            out_specs=[p_spec, s_spec],
            scratch_shapes=[pltpu.VMEM((2, TM, d), F32), pltpu.SemaphoreType.DMA(())],
        ),
        out_shape=[jax.ShapeDtypeStruct((n_p, d), F32), jax.ShapeDtypeStruct((n - n_p, d), F32)],
        compiler_params=_params(("arbitrary",)),
        name="moe_combine",
    )(dest_flat, ys, x, gates, g)


def _routing_tables(idx, counts, n_rows):
    e = idx[:, 0:2]
    rank = idx[:, 2:4]
    cnt = counts[0, :N_EXPERTS].astype(I32)
    padded = ((cnt + TM_EXPERT - 1) // TM_EXPERT) * TM_EXPERT
    ends = jnp.cumsum(padded)
    starts = ends - padded
    start_of = sum(jnp.where(e == j, starts[j], 0) for j in range(N_EXPERTS))
    dest = (start_of + rank).reshape(-1)
    tile_start = jnp.arange(n_rows // TM_EXPERT, dtype=I32) * TM_EXPERT
    tile_valid = (tile_start < ends[-1]).astype(I32)
    tile_valid = jnp.concatenate([tile_valid, ends[-1:] // TM_EXPERT - 1])
    tile_expert = jnp.sum((tile_start[:, None] >= ends[None, :]).astype(I32), axis=1)
    last_expert = jnp.sum((ends[-1] - 1 >= ends).astype(I32))
    tile_expert = jnp.minimum(tile_expert, last_expert)
    pad_rows = jnp.stack([starts + cnt, padded - cnt], axis=1).reshape(-1)
    used_tiles = ends[-1:] // TM_EXPERT
    pad_rows = jnp.concatenate([pad_rows, used_tiles, n_rows // TM_EXPERT - used_tiles])
    return dest, pad_rows, tile_expert, tile_valid


RUN_BUF_ROWS = 2 * TM + LANES


def _run_positions(idx, off_row):
    lane = lax.broadcasted_iota(I32, idx.shape, 1)
    pos = []
    for slot in range(2):
        off = jnp.sum(jnp.where(lane == idx[:, slot:slot + 1], off_row, 0.0), axis=-1, keepdims=True)
        pos.append(off.astype(I32) + idx[:, 2 + slot:3 + slot])
    return pos


def _run_copies(tab_ref, step, hbm_ref, buf_ref, slot, sem, to_hbm, wait):
    for e in range(N_EXPERTS):
        entry = (step * N_EXPERTS + e) * 3
        off, row, length = tab_ref[entry], tab_ref[entry + 1], tab_ref[entry + 2]
        for bit in range(SUBLANES.bit_length() - 1, TM.bit_length()):
            size = 1 << bit
            done = (length >> (bit + 1)) << (bit + 1)

            @pl.when(((length >> bit) & 1) == 1)
            def _(size=size, done=done, off=off, row=row):
                in_buf = buf_ref.at[slot, pl.ds(pl.multiple_of(off + done, SUBLANES), size), :]
                in_hbm = hbm_ref.at[pl.ds(pl.multiple_of(row + done, SUBLANES), size), :]
                copy = (pltpu.make_async_copy(in_buf, in_hbm, sem.at[slot]) if to_hbm
                        else pltpu.make_async_copy(in_hbm, in_buf, sem.at[slot]))
                copy.wait() if wait else copy.start()


def _run_dispatch_kernel(tab_ref, pad_ref, hn_ref, idx_ref, off_ref, xs_ref, buf, zero_sc, sem, zsem,
                         *, n_steps):
    j = pl.program_id(0)
    slot = j & 1

    @pl.when(j == 0)
    def _():
        zero_sc[...] = jnp.zeros_like(zero_sc)
        rows = zero_sc.shape[0]
        for wait in (False, True):
            for e in range(N_EXPERTS):
                _zero_rows(zero_sc, xs_ref, pad_ref[2 * e], pad_ref[2 * e + 1], 2 * rows, zsem, wait)

            def tail(t, carry, wait=wait):
                first = pl.multiple_of(pad_ref[2 * N_EXPERTS] + t * rows, SUBLANES)
                copy = pltpu.make_async_copy(zero_sc, xs_ref.at[pl.ds(first, rows), :], zsem)
                copy.wait() if wait else copy.start()
                return carry

            lax.fori_loop(0, pad_ref[2 * N_EXPERTS + 1] // rows, tail, 0)

    @pl.when(j >= 2)
    def _():
        _run_copies(tab_ref, j - 2, xs_ref, buf, slot, sem, True, True)

    pos0, pos1 = _run_positions(idx_ref[...], off_ref[0])
    lane = lax.broadcasted_iota(I32, (hn_ref.shape[0], buf.shape[1]), 1)
    sel = jnp.where((lane == pos0) | (lane == pos1), 1.0, 0.0).astype(BF16)
    buf[slot] = _dot_tn(sel, hn_ref[...])
    _run_copies(tab_ref, j, xs_ref, buf, slot, sem, True, False)

    @pl.when(j == n_steps - 1)
    def _():
        _run_copies(tab_ref, j - 1, xs_ref, buf, 1 - slot, sem, True, True)
        _run_copies(tab_ref, j, xs_ref, buf, slot, sem, True, True)


def _run_dispatch(run_table, pad_table, hn, idx, run_offsets, n_rows):
    n, d = hn.shape
    n_steps = n // TM
    assert n_steps >= 2
    return pl.pallas_call(
        functools.partial(_run_dispatch_kernel, n_steps=n_steps),
        grid_spec=pltpu.PrefetchScalarGridSpec(
            num_scalar_prefetch=2,
            grid=(n_steps,),
            in_specs=[
                pl.BlockSpec((TM, d), lambda i, *_: (i, 0)),
                pl.BlockSpec((TM, LANES), lambda i, *_: (i, 0)),
                pl.BlockSpec((1, 1, LANES), lambda i, *_: (i, 0, 0)),
            ],
            out_specs=pl.BlockSpec(memory_space=pl.ANY),
            scratch_shapes=[pltpu.VMEM((2, RUN_BUF_ROWS, d), F32),
                            pltpu.VMEM((TM_EXPERT // 2, d), F32),
                            pltpu.SemaphoreType.DMA((2,)), pltpu.SemaphoreType.DMA(())],
        ),
        out_shape=jax.ShapeDtypeStruct((n_rows, d), F32),
        compiler_params=_params(("arbitrary",)),
        name="moe_dispatch",
    )(run_table, pad_table, hn, idx, run_offsets)


def _run_combine_kernel(tab_ref, ys_ref, x_ref, gate_ref, idx_ref, off_ref, g_ref, yp_ref, ysm_ref,
                        buf, sem, *, n_first, n_steps):
    j = pl.program_id(0)
    slot = j & 1

    @pl.when(j == 0)
    def _():
        buf[...] = jnp.zeros_like(buf)
        _run_copies(tab_ref, 0, ys_ref, buf, 0, sem, False, False)

    @pl.when(j + 1 < n_steps)
    def _():
        _run_copies(tab_ref, j + 1, ys_ref, buf, 1 - slot, sem, False, False)

    _run_copies(tab_ref, j, ys_ref, buf, slot, sem, False, True)
    runs = buf[slot].astype(BF16)
    pos = _run_positions(idx_ref[...], off_ref[0])
    lane = lax.broadcasted_iota(I32, (x_ref.shape[0], buf.shape[1]), 1)
    gate = gate_ref[...]
    y = x_ref[...]
    for k in range(2):
        sel = jnp.where(lane == pos[k], 1.0, 0.0).astype(BF16)
        y = y + gate[:, k:k + 1] * _dot(sel, runs)
    y = _rms(y, g_ref[...])

    @pl.when(j < n_first)
    def _():
        yp_ref[...] = y

    @pl.when(j >= n_first)
    def _():
        ysm_ref[...] = y


def _run_combine(run_table, ys, x, gates, idx, run_offsets, g, n_p):
    n, d = x.shape
    n_first = n_p // TM
    n_steps = n // TM
    p_spec, s_spec = _split_specs(n_first, d)
    row = lambda i, *_: (i, 0)
    return pl.pallas_call(
        functools.partial(_run_combine_kernel, n_first=n_first, n_steps=n_steps),
        grid_spec=pltpu.PrefetchScalarGridSpec(
            num_scalar_prefetch=1,
            grid=(n_steps,),
            in_specs=[
                pl.BlockSpec(memory_space=pl.ANY),
                pl.BlockSpec((TM, d), row),
                pl.BlockSpec((TM, LANES), row),
                pl.BlockSpec((TM, LANES), row),
                pl.BlockSpec((1, 1, LANES), lambda i, *_: (i, 0, 0)),
                pl.BlockSpec((1, d), lambda i, *_: (0, 0)),
            ],
            out_specs=[p_spec, s_spec],
            scratch_shapes=[pltpu.VMEM((2, RUN_BUF_ROWS, d), F32), pltpu.SemaphoreType.DMA((2,))],
        ),
        out_shape=[jax.ShapeDtypeStruct((n_p, d), F32), jax.ShapeDtypeStruct((n - n_p, d), F32)],
        compiler_params=_params(("arbitrary",)),
        name="moe_combine",
    )(run_table, ys, x, gates, idx, run_offsets, g)


def _run_tables(tile_counts, n_rows):
    cnt = tile_counts[:, 0, :N_EXPERTS].astype(I32)
    run = ((cnt + SUBLANES - 1) // SUBLANES) * SUBLANES
    off = jnp.cumsum(run, axis=1) - run
    used = jnp.sum(run, axis=0)
    padded = ((used + TM_EXPERT - 1) // TM_EXPERT) * TM_EXPERT
    ends = jnp.cumsum(padded)
    starts = ends - padded
    row = starts[None, :] + jnp.cumsum(run, axis=0) - run
    run_table = jnp.stack([off, row, run], axis=2).reshape(-1)
    run_offsets = jnp.pad(off.astype(F32), ((0, 0), (0, LANES - N_EXPERTS)))[:, None, :]
    n_tiles = n_rows // TM_EXPERT
    used_tiles = ends[-1] // TM_EXPERT
    pad_table = jnp.stack([starts + used, padded - used], axis=1).reshape(-1)
    pad_table = jnp.concatenate([pad_table, jnp.stack([ends[-1], n_rows - ends[-1]])])
    tile_start = jnp.arange(n_tiles, dtype=I32) * TM_EXPERT
    tile_valid = jnp.concatenate([(tile_start < ends[-1]).astype(I32), (used_tiles - 1)[None]])
    tile_expert = jnp.sum((tile_start[:, None] >= ends[None, :]).astype(I32), axis=1)
    last_expert = jnp.sum((ends[-1] - 1 >= ends).astype(I32))
    tile_expert = jnp.minimum(tile_expert, last_expert)
    return run_table, run_offsets, pad_table, tile_expert, tile_valid


def _pad_time(a, nseq, t, fill=None):
    w = a.shape[1]
    a = a.reshape(nseq, t, w)
    a = jnp.pad(a, ((0, 0), (0, SAMPLE_PAD_T - t), (0, 0)))
    if fill is not None:
        pad_row = jnp.arange(SAMPLE_PAD_T)[None, :, None] >= t
        lane = jnp.arange(w)[None, None, :]
        a = jnp.where(pad_row & (lane < MLSTM_HEADS), fill, a)
    return a.reshape(nseq * SAMPLE_PAD_T, w)


def _unpad_time(a, nseq, t):
    w = a.shape[1]
    return a.reshape(nseq, SAMPLE_PAD_T, w)[:, :t].reshape(nseq * t, w)


def kernel(x_prompt, x_sample, state_gla_S, state_mlstm_C, state_mlstm_n, state_mlstm_m,
           norm_mix, norm_ffn, norm_final,
           gla_w_in, gla_w_a2, gla_b_a, gla_g_norm, gla_w_out,
           mlstm_w_in, mlstm_b_gate, mlstm_g_norm, mlstm_w_out,
           ffn_w_gu, ffn_w_down,
           moe_w_router, moe_b_router, moe_w_gu, moe_w_down):
    bp, tp, d = x_prompt.shape
    bs, ts, _ = x_sample.shape
    n_p = bp * tp
    n_s = bs * ts
    n = n_p + n_s
    assert n_p % TM == 0 and n_s % TM == 0 and tp % (2 * CHUNK) == 0 and ts <= SAMPLE_PAD_T
    assert norm_mix.shape[0] == 2, "one GLA layer followed by one mLSTM layer"
    qk_w = d // 2
    main_w = 2 * qk_w + 2 * d

    x_p = x_prompt.reshape(n_p, d)
    x_s = x_sample.reshape(n_s, d)

    w_in = gla_w_in[0]
    w_lr = jnp.pad(w_in[:, main_w:], ((0, 0), (0, LANES - GLA_RANK))).astype(BF16)
    w_a2 = jnp.pad(gla_w_a2[0], ((0, LANES - GLA_RANK), (0, 0))).astype(BF16)
    z, log_a = _gla_in_proj(x_p, x_s, norm_mix[0][None], w_in[:, :main_w].astype(BF16), w_lr, w_a2,
                            gla_b_a[0][None])
    s0_p = jnp.zeros((bp,) + state_gla_S.shape[2:], F32)
    u_p, s_p = _gla_recurrence(z, log_a, s0_p, gla_g_norm[0], bp, tp, CHUNK, 16 * CHUNK)
    u_s, s_s = _gla_recurrence_short(_pad_time(z[n_p:], bs, ts), _pad_time(log_a[n_p:], bs, ts),
                                     state_gla_S[0], gla_g_norm[0], bs, SAMPLE_PAD_T,
                                     SAMPLE_SEQS_PER_STEP)
    x, hn = _out_proj(u_p, _unpad_time(u_s, bs, ts), x_p, x_s, gla_w_out[0].astype(BF16),
                      norm_ffn[0][None])
    x, hn, moe_wgu = _swiglu(hn, x, ffn_w_gu[0].astype(BF16), ffn_w_down[0].astype(BF16),
                             norm_mix[1][None], SWIGLU_CHUNKS,
                             moe_w_gu[0].reshape(N_EXPERTS * d, moe_w_gu.shape[3]))

    w_in = mlstm_w_in[0]
    w_gt = jnp.pad(w_in[:, main_w:], ((0, 0), (0, LANES - 2 * MLSTM_HEADS))).astype(BF16)
    b_gt = jnp.pad(mlstm_b_gate[0], (0, LANES - 2 * MLSTM_HEADS))[None]
    dff_e = moe_w_down.shape[2]
    z, gates, moe_wd = _mlstm_in_proj(hn, w_in[:, :main_w].astype(BF16), w_gt, b_gt,
                                      moe_w_down[0].reshape(N_EXPERTS * dff_e, d))
    pairs = MLSTM_HEADS // 2
    dk2 = 2 * state_mlstm_C.shape[3]
    dvm = state_mlstm_C.shape[4]
    c0_p = jnp.zeros((bp, pairs, dk2, dvm), F32)
    n0_p = jnp.zeros((bp, pairs, dk2), F32)
    m0_p = jnp.zeros((bp, 1, LANES), F32)
    u_p, c_p, nn_p, m_p = _mlstm_recurrence_blocked(z, gates, c0_p, n0_p, m0_p,
                                                    mlstm_g_norm[0], bp, tp, CHUNK, 16 * CHUNK)
    c0_s = state_mlstm_C[0]
    n0_s = state_mlstm_n[0].reshape(bs, pairs, dk2)
    m0_s = jnp.pad(state_mlstm_m[0], ((0, 0), (0, LANES - MLSTM_HEADS)))[:, None, :]
    u_s, c_s, nn_s, m_s = _mlstm_recurrence_short(
        _pad_time(z[n_p:], bs, ts), _pad_time(gates[n_p:], bs, ts, fill=NEG_BIG),
        c0_s, n0_s, m0_s, mlstm_g_norm[0], bs, SAMPLE_PAD_T, SAMPLE_SEQS_PER_STEP)

    w_r = jnp.pad(moe_w_router[0], ((0, 0), (0, LANES - N_EXPERTS)))
    w_r_hi = w_r.astype(BF16)
    w_r = jnp.stack([w_r_hi, (w_r - w_r_hi.astype(F32)).astype(BF16)])
    b_r = jnp.pad(moe_b_router[0], (0, LANES - N_EXPERTS))[None]
    x, hn, route_g, route_i, tile_counts = _out_proj_router(
        u_p, _unpad_time(u_s, bs, ts), x, mlstm_w_out[0].astype(BF16), norm_ffn[1][None], w_r, b_r)
    n_rows = 2 * n + (n // TM) * N_EXPERTS * (SUBLANES - 1) + N_EXPERTS * (TM_EXPERT - 1)
    n_rows = -(-n_rows // TM_EXPERT) * TM_EXPERT
    run_table, run_offsets, pad_table, tile_expert, tile_valid = _run_tables(tile_counts, n_rows)
    xs = _run_dispatch(run_table, pad_table, hn, route_i, run_offsets, n_rows)
    ys = _experts(tile_expert, tile_valid, xs, moe_wgu.reshape(moe_w_gu.shape[1:]),
                  moe_wd.reshape(moe_w_down.shape[1:]))
    y_p, y_s = _run_combine(run_table, ys, x, route_g, route_i, run_offsets, norm_final[None], n_p)

    y_prompt = y_p.reshape(bp, tp, d)
    y_sample = y_s.reshape(bs, ts, d)
    hd = state_mlstm_C.shape[2:]
    return (y_prompt, y_sample,
            s_p[None], c_p.reshape((1, bp) + hd), nn_p.reshape(1, bp, hd[0], hd[1]),
            m_p[None, :, 0, :MLSTM_HEADS],
            s_s[None], c_s[None], nn_s.reshape(1, bs, hd[0], hd[1]), m_s[None, :, 0, :MLSTM_HEADS])


def _mlstm_in_proj(hn, w_main, w_gate, b_gate, cast_src):
    n, d = hn.shape
    wz = w_main.shape[1]
    cast_in, cast_out, cast_shape = _cast_side_job(cast_src, n // TM)

    def kern(h_ref, w_ref, wg_ref, b_ref, cast_ref, z_ref, gate_ref, cast_out_ref):
        cast_out_ref[...] = cast_ref[...].astype(BF16)
        h = h_ref[...]
        width = wz // IN_PROJ_COL_CHUNKS
        for c in range(IN_PROJ_COL_CHUNKS):
            cs = slice(c * width, (c + 1) * width)
            z_ref[:, cs] = _dot(h, w_ref[:, cs])
        gp = _dot(h, wg_ref[...]) + b_ref[...]
        gc = GATE_CAP * jnp.tanh(gp * (1.0 / GATE_CAP))
        lane = lax.broadcasted_iota(I32, gc.shape, 1)
        out = jnp.where(lane < MLSTM_HEADS, gc, _log_sigmoid(gc))
        gate_ref[...] = jnp.where(lane < 2 * MLSTM_HEADS, out, 0.0)

    return pl.pallas_call(
        kern,
        grid=(n // TM,),
        in_specs=[
            pl.BlockSpec((TM, d), lambda i: (i, 0)),
            _resident_spec((d, wz), lambda i: (0, 0)),
            _resident_spec((d, LANES), lambda i: (0, 0)),
            pl.BlockSpec((1, LANES), lambda i: (0, 0)),
            cast_in,
        ],
        out_specs=[pl.BlockSpec((TM, wz), lambda i: (i, 0)), pl.BlockSpec((TM, LANES), lambda i: (i, 0)),
                   cast_out],
        out_shape=[jax.ShapeDtypeStruct((n, wz), F32), jax.ShapeDtypeStruct((n, LANES), F32),
                   cast_shape],
        compiler_params=_params(("arbitrary",)),
        name="in_proj_mlstm",
    )(hn, w_main, w_gate, b_gate, cast_src)
```

```python
import functools

import jax
import jax.numpy as jnp
import numpy as np
from jax import lax
from jax.experimental import pallas as pl
from jax.experimental.pallas import tpu as pltpu

F32 = jnp.float32
BF16 = jnp.bfloat16
I32 = jnp.int32

EPS = 1e-6
GLA_HEADS = 4
GLA_RANK = 16
GLA_TAU = 16.0
MLSTM_HEADS = 8
GATE_CAP = 15.0
CHUNK = 64
N_EXPERTS = 8
NEG_BIG = -1e30

LANES = 128
TM = 512
TM_EXPERT = 512
SAMPLE_PAD_T = 8
SAMPLE_SEQS_PER_STEP = 16
SWIGLU_CHUNKS = 11
VMEM_LIMIT = 56 * 1024 * 1024


def _dot(a, b):
    return jnp.dot(a, b, preferred_element_type=F32)


def _dot_nt(a, b):
    return lax.dot_general(a, b, (((1,), (1,)), ((), ())), preferred_element_type=F32)


def _dot_tn(a, b):
    return lax.dot_general(a, b, (((0,), (0,)), ((), ())), preferred_element_type=F32)


def _sigmoid(x):
    return 1.0 / (1.0 + jnp.exp(-x))


def _log_sigmoid(x):
    return jnp.minimum(x, 0.0) - jnp.log1p(jnp.exp(-jnp.abs(x)))


def _rms(x, g):
    return x * lax.rsqrt(jnp.mean(x * x, axis=-1, keepdims=True) + EPS) * g


def _cumsum_rows(x, n):
    ridx = lax.broadcasted_iota(I32, x.shape, 0)
    s = 1
    while s < n:
        x = x + jnp.where(ridx >= s, pltpu.roll(x, s, 0), 0.0)
        s *= 2
    return x


def _params(sem):
    return pltpu.CompilerParams(dimension_semantics=sem, vmem_limit_bytes=VMEM_LIMIT)


def _resident_spec(block_shape, index_map):
    return pl.BlockSpec(block_shape, index_map, pipeline_mode=pl.Buffered(1))


def _cast_side_job(src, n_steps):
    rows, cols = src.shape
    n_slabs = max(s for s in range(1, n_steps + 1) if rows % (16 * s) == 0)
    spec = pl.BlockSpec((rows // n_slabs, cols), lambda i, *_: (jnp.minimum(i, n_slabs - 1), 0))
    return spec, spec, jax.ShapeDtypeStruct(src.shape, BF16)


IN_PROJ_COL_CHUNKS = 4


def _pick_rows(i, n_first, first_ref, second_ref):
    return jnp.where(i < n_first, first_ref[...], second_ref[...])


def _split_specs(n_first, width):
    return [pl.BlockSpec((TM, width), lambda i, *_: (jnp.minimum(i, n_first - 1), 0)),
            pl.BlockSpec((TM, width), lambda i, *_: (jnp.maximum(i - n_first, 0), 0))]


def _gla_in_proj_kernel(xp_ref, xs_ref, g_ref, w_ref, wg_ref, w2_ref, b_ref, cast_ref,
                        z_ref, gate_ref, cast_out_ref, wb_sc, *, n_first):
    @pl.when(pl.program_id(0) == 0)
    def _():
        wb_sc[...] = w_ref[...].astype(BF16)

    cast_out_ref[...] = cast_ref[...].astype(BF16)
    x = _pick_rows(pl.program_id(0), n_first, xp_ref, xs_ref)
    hn = _rms(x, g_ref[...]).astype(BF16)
    width = wb_sc.shape[1] // IN_PROJ_COL_CHUNKS
    for c in range(IN_PROJ_COL_CHUNKS):
        cs = slice(c * width, (c + 1) * width)
        z_ref[:, cs] = _dot(hn, wb_sc[:, cs])
    a = _dot(hn, wg_ref[...])
    la = _dot(a.astype(BF16), w2_ref[...]) + b_ref[...]
    gate_ref[...] = _log_sigmoid(la) * (1.0 / GLA_TAU)


def _gla_in_proj(x_p, x_s, g, w_in, wz, w_gate, w2, b, cast_src):
    d = x_p.shape[1]
    n = x_p.shape[0] + x_s.shape[0]
    n_first = x_p.shape[0] // TM
    gw = b.shape[1]
    cast_in, cast_out, cast_shape = _cast_side_job(cast_src, n // TM)
    return pl.pallas_call(
        functools.partial(_gla_in_proj_kernel, n_first=n_first),
        grid=(n // TM,),
        in_specs=_split_specs(n_first, d) + [
            pl.BlockSpec((1, d), lambda i: (0, 0)),
            _resident_spec((d, wz), lambda i: (0, 0)),
            pl.BlockSpec((d, LANES), lambda i: (0, 0)),
            pl.BlockSpec(w2.shape, lambda i: (0, 0)),
            pl.BlockSpec((1, gw), lambda i: (0, 0)),
            cast_in,
        ],
        out_specs=[
            pl.BlockSpec((TM, wz), lambda i: (i, 0)),
            pl.BlockSpec((TM, gw), lambda i: (i, 0)),
            cast_out,
        ],
        out_shape=[jax.ShapeDtypeStruct((n, wz), F32), jax.ShapeDtypeStruct((n, gw), F32), cast_shape],
        scratch_shapes=[pltpu.VMEM((d, wz), BF16)],
        compiler_params=_params(("arbitrary",)),
        name="in_proj_gla",
    )(x_p, x_s, g, w_in, w_gate, w2, b, cast_src)


def _gla_kernel(q_ref, k_ref, v_ref, r_ref, la_ref, s0_ref, gn_ref, u_ref, st_ref, st_sc,
                *, chunk, tb, nblk):
    heads = GLA_HEADS
    dk = q_ref.shape[1] // heads
    dv = v_ref.shape[1] // heads
    c = pl.program_id(1)

    @pl.when(c == 0)
    def _():
        for h in range(heads):
            st_sc[h] = s0_ref[0, h].T

    ri = lax.broadcasted_iota(I32, (chunk, chunk), 0)
    ci = lax.broadcasted_iota(I32, (chunk, chunk), 1)
    causal = ri >= ci
    kscale = dk ** -0.5
    state = [st_sc[h] for h in range(heads)]
    for j in range(tb // chunk):
        rows = slice(j * chunk, (j + 1) * chunk)
        b_all = _cumsum_rows(la_ref[rows, :], chunk)
        for h in range(heads):
            ks = slice(h * dk, (h + 1) * dk)
            vs = slice(h * dv, (h + 1) * dv)
            b = b_all[:, ks]
            b_end = b[chunk - 1:chunk, :]
            q = q_ref[rows, ks]
            k = k_ref[rows, ks] * kscale
            qg = (q * jnp.exp(b)).astype(BF16)
            kg = (k * jnp.exp(-b)).astype(BF16)
            ke = (k * jnp.exp(b_end - b)).astype(BF16)
            v = v_ref[rows, vs].astype(BF16)
            a = jnp.where(causal, _dot_nt(qg, kg), 0.0).astype(BF16)
            o = _dot_nt(qg, state[h].astype(BF16)) + _dot(a, v)
            state[h] = state[h] * jnp.exp(b_end) + _dot_tn(v, ke)
            on = _rms(o, gn_ref[h:h + 1, :])
            r = r_ref[rows, vs]
            u_ref[rows, vs] = (r * _sigmoid(r) * on).astype(BF16)
    for h in range(heads):
        st_sc[h] = state[h]

    @pl.when(c == nblk - 1)
    def _():
        for h in range(heads):
            st_ref[0, h] = st_sc[h].T


def _gla_recurrence(z, log_a, s0, g_norm, nseq, t, chunk, tb):
    qk_w = log_a.shape[1]
    v_w = 2 * qk_w
    nblk = t // tb
    heads, dk, dv = s0.shape[1:]
    kern = functools.partial(_gla_kernel, chunk=chunk, tb=tb, nblk=nblk)
    row = lambda b, c: b * nblk + c
    return pl.pallas_call(
        kern,
        grid=(nseq, nblk),
        in_specs=[
            pl.BlockSpec((tb, qk_w), lambda b, c: (row(b, c), 0)),
            pl.BlockSpec((tb, qk_w), lambda b, c: (row(b, c), 1)),
            pl.BlockSpec((tb, v_w), lambda b, c: (row(b, c), 1)),
            pl.BlockSpec((tb, v_w), lambda b, c: (row(b, c), 2)),
            pl.BlockSpec((tb, qk_w), lambda b, c: (row(b, c), 0)),
            pl.BlockSpec((1, heads, dk, dv), lambda b, c: (b, 0, 0, 0)),
            pl.BlockSpec((heads, dv), lambda b, c: (0, 0)),
        ],
        out_specs=[
            pl.BlockSpec((tb, v_w), lambda b, c: (row(b, c), 0)),
            pl.BlockSpec((1, heads, dk, dv), lambda b, c: (b, 0, 0, 0)),
        ],
        out_shape=[
            jax.ShapeDtypeStruct((nseq * t, v_w), BF16),
            jax.ShapeDtypeStruct(s0.shape, F32),
        ],
        scratch_shapes=[pltpu.VMEM((heads, dv, dk), F32)],
        compiler_params=_params(("arbitrary", "arbitrary")),
        name="gla_recurrence_t%d" % t,
    )(z, z, z, z, log_a, s0, g_norm)


def _gla_short_kernel(q_ref, k_ref, v_ref, r_ref, la_ref, s0_ref, gn_ref, u_ref, st_ref, *, t, nsq):
    heads = GLA_HEADS
    dk = q_ref.shape[1] // heads
    dv = v_ref.shape[1] // heads
    nrow = nsq * t
    width = q_ref.shape[1]
    rc = lax.broadcasted_iota(I32, (nrow, width), 0) & (t - 1)
    b = _chunk_scan(la_ref[...], rc, t, jnp.add, 0.0)
    b_last = b.reshape(nsq, t, width)[:, t - 1:t, :]
    b_end = jnp.broadcast_to(b_last, (nsq, t, width)).reshape(nrow, width)
    k = k_ref[...] * (dk ** -0.5)
    qg = q_ref[...] * jnp.exp(b)
    kg = k * jnp.exp(-b)
    ke = k * jnp.exp(b_end - b)
    decay = jnp.exp(b_end)
    ri = lax.broadcasted_iota(I32, (nrow, nrow), 0)
    ci = lax.broadcasted_iota(I32, (nrow, nrow), 1)
    shift = t.bit_length() - 1
    mask = ((ri >> shift) == (ci >> shift)) & (ri >= ci)
    for h in range(heads):
        ks = slice(h * dk, (h + 1) * dk)
        vs = slice(h * dv, (h + 1) * dv)
        a = jnp.where(mask, _dot_nt(qg[:, ks].astype(BF16), kg[:, ks].astype(BF16)), 0.0)
        o = _dot(a.astype(BF16), v_ref[:, vs].astype(BF16))
        decay_t = decay[:, ks].T
        o_state = []
        for sq in range(nsq):
            r = slice(sq * t, (sq + 1) * t)
            s0 = s0_ref[sq, h]
            o_state.append(_dot(qg[r, ks].astype(BF16), s0.astype(BF16)))
            upd = _dot_tn(ke[r, ks].astype(BF16), v_ref[r, vs].astype(BF16))
            st_ref[sq, h] = decay_t[:, sq * t:sq * t + 1] * s0 + upd
        on = _rms(o + jnp.concatenate(o_state, axis=0), gn_ref[h:h + 1, :])
        rg = r_ref[:, vs]
        u_ref[:, vs] = (rg * _sigmoid(rg) * on).astype(BF16)


def _gla_recurrence_short(z, log_a, s0, g_norm, nseq, t, nsq):
    qk_w = log_a.shape[1]
    v_w = 2 * qk_w
    heads, dk, dv = s0.shape[1:]
    assert nsq * t == dk == LANES
    rb = nsq * t
    return pl.pallas_call(
        functools.partial(_gla_short_kernel, t=t, nsq=nsq),
        grid=(nseq // nsq,),
        in_specs=[
            pl.BlockSpec((rb, qk_w), lambda b: (b, 0)),
            pl.BlockSpec((rb, qk_w), lambda b: (b, 1)),
            pl.BlockSpec((rb, v_w), lambda b: (b, 1)),
            pl.BlockSpec((rb, v_w), lambda b: (b, 2)),
            pl.BlockSpec((rb, qk_w), lambda b: (b, 0)),
            pl.BlockSpec((nsq, heads, dk, dv), lambda b: (b, 0, 0, 0)),
            pl.BlockSpec((heads, dv), lambda b: (0, 0)),
        ],
        out_specs=[
            pl.BlockSpec((rb, v_w), lambda b: (b, 0)),
            pl.BlockSpec((nsq, heads, dk, dv), lambda b: (b, 0, 0, 0)),
        ],
        out_shape=[
            jax.ShapeDtypeStruct((nseq * t, v_w), BF16),
            jax.ShapeDtypeStruct(s0.shape, F32),
        ],
        compiler_params=_params(("arbitrary",)),
        name="gla_recurrence_t%d" % t,
    )(z, z, z, z, log_a, s0, g_norm)


def _mlstm_short_kernel(q_ref, k_ref, v_ref, op_ref, gt_ref, e_ref, c0_ref, n0_ref, m0_ref, gn_ref,
                        u_ref, ct_ref, nt_ref, mt_ref, *, t, nsq):
    heads = MLSTM_HEADS
    pairs = heads // 2
    dv = v_ref.shape[1] // heads
    dk = q_ref.shape[1] // heads
    nrow = nsq * t

    def per_seq(x3):
        return jnp.broadcast_to(x3, (nsq, t, x3.shape[2])).reshape(nrow, x3.shape[2])

    def last(x):
        return x.reshape(nsq, t, x.shape[1])[:, t - 1:t, :]

    rc = lax.broadcasted_iota(I32, (nrow, LANES), 0) & (t - 1)
    gts = gt_ref[...]
    f_cum = pltpu.roll(_chunk_scan(gts, rc, t, jnp.add, 0.0), LANES - heads, 1)
    b = gts - f_cum
    cmb = _chunk_scan(b, rc, t, jnp.maximum, NEG_BIG)
    mp = per_seq(m0_ref[...])
    big_m = jnp.maximum(mp, cmb)
    m_t = f_cum + big_m
    m_new = last(m_t)
    mt_ref[...] = m_new
    mn = per_seq(m_new)
    fe = per_seq(last(f_cum))
    w_i = jnp.exp(mp - big_m)
    em = jnp.exp(-m_t)
    k_sc = jnp.exp(fe - f_cum + gts - mn)
    w_c = jnp.exp(fe + mp - mn)
    ex = _dot(_pack_terms([w_i, k_sc, -big_m, em, w_c], heads), e_ref[...])
    offs = np.cumsum([0] + [heads * w for w in SHORT_EXPAND_WIDTHS])
    q = q_ref[...]
    k = k_ref[...] * (dk ** -0.5)
    qs = q * ex[:, offs[0]:offs[1]]
    ke = k * ex[:, offs[1]:offs[2]]
    neg_m = ex[:, offs[2]:offs[3]]
    em_v = ex[:, offs[3]:offs[4]]
    wc_k = ex[:, offs[4]:offs[5]]
    bt = b.T

    ri = lax.broadcasted_iota(I32, (nrow, nrow), 0)
    ci = lax.broadcasted_iota(I32, (nrow, nrow), 1)
    shift = t.bit_length() - 1
    mask = ((ri >> shift) == (ci >> shift)) & (ri >= ci)
    lane1 = lax.broadcasted_iota(I32, (1, LANES), 1)
    rowi = lax.broadcasted_iota(I32, (LANES, LANES), 0)
    ones_v = jnp.ones((nrow, dv), BF16)
    ones_k = jnp.ones((LANES, dv), BF16)
    for p in range(pairs):
        ps = slice(p * LANES, (p + 1) * LANES)
        qp = q[:, ps]
        qsp = qs[:, ps]
        kpb = k[:, ps].astype(BF16)
        n_rows = per_seq(n0_ref[:, p:p + 1, :])
        intra, den_state, qsh = [], [], []
        for jj in range(2):
            h = 2 * p + jj
            hs = slice(h * dv, (h + 1) * dv)
            mine = (lane1 >= jj * dk) & (lane1 < (jj + 1) * dk)
            qh = jnp.where(mine, qp, 0.0).astype(BF16)
            qsh.append(jnp.where(mine, qsp, 0.0))
            d = neg_m[:, hs] + bt[h:h + 1, :]
            s = _dot_nt(qh, kpb) * jnp.where(mask, jnp.exp(d), 0.0)
            vh = jnp.concatenate([v_ref[:, hs].astype(BF16), ones_v], axis=1)
            intra.append(_dot(s.astype(BF16), vh))
            den_state.append(_dot((qsh[jj] * n_rows).astype(BF16), ones_k))
        num_state = [[], []]
        for sq in range(nsq):
            r = slice(sq * t, (sq + 1) * t)
            lhs = jnp.concatenate([qsh[0][r], qsh[1][r]], axis=0).astype(BF16)
            c_pair = jnp.concatenate([c0_ref[sq, 2 * p], c0_ref[sq, 2 * p + 1]], axis=0)
            res = _dot(lhs, c_pair.astype(BF16))
            num_state[0].append(res[:t])
            num_state[1].append(res[t:])
        for jj in range(2):
            h = 2 * p + jj
            hs = slice(h * dv, (h + 1) * dv)
            num = intra[jj][:, :dv] + jnp.concatenate(num_state[jj], axis=0)
            den = intra[jj][:, dv:] + den_state[jj]
            hh = num / jnp.maximum(jnp.abs(den), em_v[:, hs])
            hn = _rms(hh, gn_ref[h:h + 1, :])
            u_ref[:, hs] = (_sigmoid(op_ref[:, hs]) * hn).astype(BF16)
        kep = ke[:, ps]
        for sq in range(nsq):
            r = slice(sq * t, (sq + 1) * t)
            vp = v_ref[r, 2 * p * dv:(2 * p + 2) * dv].astype(BF16)
            full = _dot_tn(kep[r].astype(BF16), vp)
            upd = jnp.where(rowi < dk, full[:, :dv], full[:, dv:])
            w_row = w_c[sq * t:sq * t + 1, :]
            w_col = jnp.where(rowi < dk, w_row[:, 2 * p:2 * p + 1], w_row[:, 2 * p + 1:2 * p + 2])
            c_pair = jnp.concatenate([c0_ref[sq, 2 * p], c0_ref[sq, 2 * p + 1]], axis=0)
            c_new = w_col * c_pair + upd
            ct_ref[sq, 2 * p] = c_new[:dk]
            ct_ref[sq, 2 * p + 1] = c_new[dk:]
        k_sum = jnp.sum(kep.reshape(nsq, t, LANES), axis=1, keepdims=True)
        nt_ref[:, p:p + 1, :] = last(wc_k[:, ps]) * n0_ref[:, p:p + 1, :] + k_sum


def _mlstm_recurrence_short(z, gates, c0, n0, m0, g_norm, nseq, t, nsq):
    heads = g_norm.shape[0]
    v_w = heads * g_norm.shape[1]
    qk_w = v_w // 2
    assert nsq * t == LANES
    expand = _expand_matrix(heads, SHORT_EXPAND_WIDTHS)
    kern = functools.partial(_mlstm_short_kernel, t=t, nsq=nsq)
    rb = nsq * t
    st4 = lambda b: (b, 0, 0, 0)
    st3 = lambda b: (b, 0, 0)
    return pl.pallas_call(
        kern,
        grid=(nseq // nsq,),
        in_specs=[
            pl.BlockSpec((rb, qk_w), lambda b: (b, 0)),
            pl.BlockSpec((rb, qk_w), lambda b: (b, 1)),
            pl.BlockSpec((rb, v_w), lambda b: (b, 1)),
            pl.BlockSpec((rb, v_w), lambda b: (b, 2)),
            pl.BlockSpec((rb, LANES), lambda b: (b, 0)),
            pl.BlockSpec(expand.shape, lambda b: (0, 0)),
            pl.BlockSpec((nsq,) + c0.shape[1:], st4),
            pl.BlockSpec((nsq,) + n0.shape[1:], st3),
            pl.BlockSpec((nsq,) + m0.shape[1:], st3),
            pl.BlockSpec(g_norm.shape, lambda b: (0, 0)),
        ],
        out_specs=[
            pl.BlockSpec((rb, v_w), lambda b: (b, 0)),
            pl.BlockSpec((nsq,) + c0.shape[1:], st4),
            pl.BlockSpec((nsq,) + n0.shape[1:], st3),
            pl.BlockSpec((nsq,) + m0.shape[1:], st3),
        ],
        out_shape=[
            jax.ShapeDtypeStruct((nseq * t, v_w), BF16),
            jax.ShapeDtypeStruct(c0.shape, F32),
            jax.ShapeDtypeStruct(n0.shape, F32),
            jax.ShapeDtypeStruct(m0.shape, F32),
        ],
        compiler_params=_params(("arbitrary",)),
        name="mlstm_recurrence_t%d" % t,
    )(z, z, z, z, gates, expand, c0, n0, m0, g_norm)


N_SPLIT = 3
EXPAND_WIDTHS = (64, 64, 128, 128)
SHORT_EXPAND_WIDTHS = EXPAND_WIDTHS + (64,)


def _expand_matrix(heads, widths):
    assert len(widths) * N_SPLIT * heads <= LANES
    cols = []
    for qi, width in enumerate(widths):
        sel = np.zeros((LANES, heads * width), np.float32)
        for term in range(N_SPLIT):
            for h in range(heads):
                sel[(qi * N_SPLIT + term) * heads + h, h * width:(h + 1) * width] = 1.0
        cols.append(sel)
    return jnp.asarray(np.concatenate(cols, axis=1), BF16)


def _chunk_scan(x, rc, n, op, fill):
    s = 1
    while s < n:
        x = op(x, jnp.where(rc >= s, pltpu.roll(x, s, 0), fill))
        s *= 2
    return x


def _pack_terms(values, heads):
    lane = lax.broadcasted_iota(I32, values[0].shape, 1)
    packed = jnp.zeros(values[0].shape, F32)
    slot = 0
    for val in values:
        rest = val
        for term in range(N_SPLIT):
            part = rest.astype(BF16).astype(F32)
            rest = rest - part
            moved = part if slot == 0 else pltpu.roll(part, slot * heads, 1)
            packed = jnp.where((lane >= slot * heads) & (lane < (slot + 1) * heads), moved, packed)
            slot += 1
    return packed.astype(BF16)


def _mlstm_block_kernel(q_ref, k_ref, v_ref, op_ref, gt_ref, e_ref, c0_ref, n0_ref, m0_ref, gn_ref,
                        u_ref, ct_ref, nt_ref, mt_ref, c_sc, nb_sc, m_sc, *, chunk, tb, nblk):
    heads = MLSTM_HEADS
    pairs = heads // 2
    dv = v_ref.shape[1] // heads
    dk = q_ref.shape[1] // heads
    c = pl.program_id(1)

    @pl.when(c == 0)
    def _():
        c_sc[...] = c0_ref[0]
        for p in range(pairs):
            nb_sc[p] = jnp.broadcast_to(n0_ref[0, p:p + 1, :], (LANES, LANES)).T
        m_sc[...] = m0_ref[0]

    nch = tb // chunk
    rc = lax.broadcasted_iota(I32, (tb, LANES), 0) & (chunk - 1)
    gts = gt_ref[...]
    f_cum = pltpu.roll(_chunk_scan(gts, rc, chunk, jnp.add, 0.0), LANES - heads, 1)
    b = gts - f_cum
    cmb = _chunk_scan(b, rc, chunk, jnp.maximum, NEG_BIG)
    m_prev = m_sc[...]
    mp_rows, mn_rows, fe_rows, w_c = [], [], [], []
    for j in range(nch):
        last = (j + 1) * chunk - 1
        f_end = f_cum[last:last + 1, :]
        m_new = f_end + jnp.maximum(m_prev, cmb[last:last + 1, :])
        w_c.append(jnp.exp(f_end + m_prev - m_new))
        mp_rows.append(jnp.broadcast_to(m_prev, (chunk, LANES)))
        mn_rows.append(jnp.broadcast_to(m_new, (chunk, LANES)))
        fe_rows.append(jnp.broadcast_to(f_end, (chunk, LANES)))
        m_prev = m_new
    m_sc[...] = m_prev
    mp = jnp.concatenate(mp_rows, axis=0)
    mn = jnp.concatenate(mn_rows, axis=0)
    fe = jnp.concatenate(fe_rows, axis=0)
    big_m = jnp.maximum(mp, cmb)
    w_i = jnp.exp(mp - big_m)
    em = jnp.exp(-(f_cum + big_m))
    k_sc = jnp.exp(fe - f_cum + gts - mn)
    ex = _dot(_pack_terms([w_i, k_sc, -big_m, em], heads), e_ref[...])
    o1 = heads * EXPAND_WIDTHS[0]
    o2 = o1 + heads * EXPAND_WIDTHS[1]
    o3 = o2 + heads * EXPAND_WIDTHS[2]
    q = q_ref[...]
    k = k_ref[...] * (dk ** -0.5)
    qs = q * ex[:, :o1]
    ke = k * ex[:, o1:o2]
    neg_m = ex[:, o2:o3]
    em_v = ex[:, o3:]
    bt = b.T

    ri = lax.broadcasted_iota(I32, (chunk, chunk), 0)
    ci = lax.broadcasted_iota(I32, (chunk, chunk), 1)
    causal = ri >= ci
    lane1 = lax.broadcasted_iota(I32, (1, LANES), 1)
    rowi = lax.broadcasted_iota(I32, (LANES, LANES), 0)
    ones_v = jnp.ones((chunk, dv), BF16)
    states = [[None] * pairs for _ in range(nch)]
    for p in range(pairs):
        ps = slice(p * LANES, (p + 1) * LANES)
        cp = c_sc[p]
        nb = nb_sc[p]
        for j in range(nch):
            rows = slice(j * chunk, (j + 1) * chunk)
            states[j][p] = jnp.concatenate([cp, nb], axis=1).astype(BF16)
            vp = jnp.concatenate([v_ref[rows, 2 * p * dv:(2 * p + 2) * dv].astype(BF16), ones_v], axis=1)
            full = _dot_tn(ke[rows, ps].astype(BF16), vp)
            upd = jnp.where(rowi < dk, full[:, :dv], full[:, dv:2 * dv])
            w_col = jnp.where(rowi < dk, w_c[j][:, 2 * p:2 * p + 1], w_c[j][:, 2 * p + 1:2 * p + 2])
            cp = w_col * cp + upd
            nb = w_col * nb + full[:, 2 * dv:]
        c_sc[p] = cp
        nb_sc[p] = nb
    for j in range(nch):
        rows = slice(j * chunk, (j + 1) * chunk)
        for p in range(pairs):
            ps = slice(p * LANES, (p + 1) * LANES)
            qp = q[rows, ps]
            qsp = qs[rows, ps]
            kpb = k[rows, ps].astype(BF16)
            state = states[j][p]
            for jj in range(2):
                h = 2 * p + jj
                hs = slice(h * dv, (h + 1) * dv)
                mine = (lane1 >= jj * dk) & (lane1 < (jj + 1) * dk)
                qh = jnp.where(mine, qp, 0.0).astype(BF16)
                qsh = jnp.where(mine, qsp, 0.0).astype(BF16)
                d = neg_m[rows, h * dv:h * dv + chunk] + bt[h:h + 1, j * chunk:(j + 1) * chunk]
                s = _dot_nt(qh, kpb) * jnp.where(causal, jnp.exp(d), 0.0)
                vh = jnp.concatenate([v_ref[rows, hs].astype(BF16), ones_v], axis=1)
                out = _dot(qsh, state) + _dot(s.astype(BF16), vh)
                hh = out[:, :dv] / jnp.maximum(jnp.abs(out[:, dv:]), em_v[rows, hs])
                hn = _rms(hh, gn_ref[h:h + 1, :])
                u_ref[rows, hs] = (_sigmoid(op_ref[rows, hs]) * hn).astype(BF16)

    @pl.when(c == nblk - 1)
    def _():
        ct_ref[0] = c_sc[...]
        for p in range(pairs):
            nt_ref[0, p:p + 1, :] = nb_sc[p].T[0:1, :]
        mt_ref[0] = m_sc[...]


def _mlstm_recurrence_blocked(z, gates, c0, n0, m0, g_norm, nseq, t, chunk, tb):
    heads = g_norm.shape[0]
    v_w = heads * g_norm.shape[1]
    qk_w = v_w // 2
    nblk = t // tb
    expand = _expand_matrix(heads, EXPAND_WIDTHS)
    kern = functools.partial(_mlstm_block_kernel, chunk=chunk, tb=tb, nblk=nblk)
    row = lambda b, c: b * nblk + c
    st4 = lambda b, c: (b, 0, 0, 0)
    st3 = lambda b, c: (b, 0, 0)
    return pl.pallas_call(
        kern,
        grid=(nseq, nblk),
        in_specs=[
            pl.BlockSpec((tb, qk_w), lambda b, c: (row(b, c), 0)),
            pl.BlockSpec((tb, qk_w), lambda b, c: (row(b, c), 1)),
            pl.BlockSpec((tb, v_w), lambda b, c: (row(b, c), 1)),
            pl.BlockSpec((tb, v_w), lambda b, c: (row(b, c), 2)),
            pl.BlockSpec((tb, LANES), lambda b, c: (row(b, c), 0)),
            pl.BlockSpec(expand.shape, lambda b, c: (0, 0)),
            pl.BlockSpec((1,) + c0.shape[1:], st4),
            pl.BlockSpec((1,) + n0.shape[1:], st3),
            pl.BlockSpec((1,) + m0.shape[1:], st3),
            pl.BlockSpec(g_norm.shape, lambda b, c: (0, 0)),
        ],
        out_specs=[
            pl.BlockSpec((tb, v_w), lambda b, c: (row(b, c), 0)),
            pl.BlockSpec((1,) + c0.shape[1:], st4),
            pl.BlockSpec((1,) + n0.shape[1:], st3),
            pl.BlockSpec((1,) + m0.shape[1:], st3),
        ],
        out_shape=[
            jax.ShapeDtypeStruct((nseq * t, v_w), BF16),
            jax.ShapeDtypeStruct(c0.shape, F32),
            jax.ShapeDtypeStruct(n0.shape, F32),
            jax.ShapeDtypeStruct(m0.shape, F32),
        ],
        scratch_shapes=[
            pltpu.VMEM(c0.shape[1:], F32),
            pltpu.VMEM(c0.shape[1:], F32),
            pltpu.VMEM(m0.shape[1:], F32),
        ],
        compiler_params=_params(("arbitrary", "arbitrary")),
        name="mlstm_recurrence_t%d" % t,
    )(z, z, z, z, gates, expand, c0, n0, m0, g_norm)


def _out_proj_kernel(up_ref, us_ref, xp_ref, xs_ref, w_ref, g_ref, cast_ref,
                     xo_ref, hn_ref, cast_out_ref, *, n_first):
    cast_out_ref[...] = cast_ref[...].astype(BF16)
    i = pl.program_id(0)
    u = _pick_rows(i, n_first, up_ref, us_ref)
    x = _pick_rows(i, n_first, xp_ref, xs_ref) + _dot(u, w_ref[...])
    xo_ref[...] = x
    hn_ref[...] = _rms(x, g_ref[...]).astype(hn_ref.dtype)


def _out_proj(u_p, u_s, x_p, x_s, w, g, cast_src):
    d = x_p.shape[1]
    n = x_p.shape[0] + x_s.shape[0]
    n_first = x_p.shape[0] // TM
    cast_in, cast_out, cast_shape = _cast_side_job(cast_src, n // TM)
    return pl.pallas_call(
        functools.partial(_out_proj_kernel, n_first=n_first),
        grid=(n // TM,),
        in_specs=_split_specs(n_first, u_p.shape[1]) + _split_specs(n_first, d) + [
            pl.BlockSpec(w.shape, lambda i: (0, 0)),
            pl.BlockSpec((1, d), lambda i: (0, 0)),
            cast_in,
        ],
        out_specs=[pl.BlockSpec((TM, d), lambda i: (i, 0)), pl.BlockSpec((TM, d), lambda i: (i, 0)),
                   cast_out],
        out_shape=[jax.ShapeDtypeStruct((n, d), F32), jax.ShapeDtypeStruct((n, d), BF16), cast_shape],
        compiler_params=_params(("arbitrary",)),
        name="out_proj",
    )(u_p, u_s, x_p, x_s, w, g, cast_src)


def _out_proj_router_kernel(up_ref, us_ref, x_ref, w_ref, g_ref, wr_ref, br_ref,
                            xo_ref, hn_ref, gate_ref, idx_ref, cnt_ref, *, n_first):
    i = pl.program_id(0)
    u = _pick_rows(i, n_first, up_ref, us_ref)
    x = x_ref[...] + _dot(u, w_ref[...])
    xo_ref[...] = x
    hn = _rms(x, g_ref[...])
    hn_hi = hn.astype(BF16)
    hn_ref[...] = hn_hi
    hn_lo = (hn - hn_hi.astype(F32)).astype(BF16)
    logits = (_dot(hn_hi, wr_ref[0]) + _dot(hn_lo, wr_ref[0]) + _dot(hn_hi, wr_ref[1])
              + br_ref[...])
    tm = logits.shape[0]
    lane = lax.broadcasted_iota(I32, logits.shape, 1)
    valid = lane < N_EXPERTS
    logits = jnp.where(valid, logits, -jnp.inf)
    ex = jnp.exp(logits - jnp.max(logits, axis=-1, keepdims=True))
    probs = jnp.where(valid, ex / jnp.sum(ex, axis=-1, keepdims=True), -1.0)
    p1 = jnp.max(probs, axis=-1, keepdims=True)
    i1 = jnp.min(jnp.where(probs == p1, lane, LANES), axis=-1, keepdims=True)
    rest = jnp.where(lane == i1, -1.0, probs)
    p2 = jnp.max(rest, axis=-1, keepdims=True)
    i2 = jnp.min(jnp.where(rest == p2, lane, LANES), axis=-1, keepdims=True)
    tot = p1 + p2
    onehot = ((lane == i1) | (lane == i2)).astype(BF16)
    rr = lax.broadcasted_iota(I32, (tm, tm), 0)
    cc = lax.broadcasted_iota(I32, (tm, tm), 1)
    strict = (rr > cc).astype(BF16)
    before = _dot(strict, onehot)
    r1 = jnp.sum(jnp.where(lane == i1, before, 0.0), axis=-1, keepdims=True)
    r2 = jnp.sum(jnp.where(lane == i2, before, 0.0), axis=-1, keepdims=True)
    gate_ref[...] = jnp.where(lane == 0, p1 / tot, jnp.where(lane == 1, p2 / tot, 0.0))
    idx_ref[...] = jnp.where(lane == 0, i1,
                             jnp.where(lane == 1, i2,
                                       jnp.where(lane == 2, r1.astype(I32),
                                                 jnp.where(lane == 3, r2.astype(I32), 0))))
    cnt_ref[0] = jnp.sum(onehot.astype(F32), axis=0, keepdims=True)


def _out_proj_router(u_p, u_s, x, w, g, w_router, b_router):
    n, d = x.shape
    n_first = u_p.shape[0] // TM
    row = lambda i: (i, 0)
    fix = lambda i: (0, 0)
    return pl.pallas_call(
        functools.partial(_out_proj_router_kernel, n_first=n_first),
        grid=(n // TM,),
        in_specs=_split_specs(n_first, u_p.shape[1]) + [
            pl.BlockSpec((TM, d), row),
            pl.BlockSpec(w.shape, fix),
            pl.BlockSpec((1, d), fix),
            pl.BlockSpec(w_router.shape, lambda i: (0, 0, 0)),
            pl.BlockSpec((1, LANES), fix),
        ],
        out_specs=[
            pl.BlockSpec((TM, d), row),
            pl.BlockSpec((TM, d), row),
            pl.BlockSpec((TM, LANES), row),
            pl.BlockSpec((TM, LANES), row),
            pl.BlockSpec((1, 1, LANES), lambda i: (i, 0, 0)),
        ],
        out_shape=[
            jax.ShapeDtypeStruct((n, d), F32),
            jax.ShapeDtypeStruct((n, d), BF16),
            jax.ShapeDtypeStruct((n, LANES), F32),
            jax.ShapeDtypeStruct((n, LANES), I32),
            jax.ShapeDtypeStruct((n // TM, 1, LANES), F32),
        ],
        compiler_params=_params(("arbitrary",)),
        name="out_proj_router",
    )(u_p, u_s, x, w, g, w_router, b_router)


def _swiglu_kernel(hn_ref, x_ref, wg_ref, wu_ref, wd_ref, g_ref, cast_ref, xo_ref, ho_ref, cast_out_ref,
                   *, nchunk):
    cast_out_ref[...] = cast_ref[...].astype(BF16)
    hn = hn_ref[...]
    width = wd_ref.shape[0] // nchunk
    x = x_ref[...]
    for c in range(nchunk):
        cs = slice(c * width, (c + 1) * width)
        gate = _dot(hn, wg_ref[:, cs])
        up = _dot(hn, wu_ref[:, cs])
        act = (gate * _sigmoid(gate) * up).astype(BF16)
        x = x + _dot(act, wd_ref[cs, :])
    xo_ref[...] = x
    ho_ref[...] = _rms(x, g_ref[...]).astype(BF16)


def _swiglu(hn, x, w_gu, w_down, g, nchunk, cast_src):
    n, d = x.shape
    dff = w_down.shape[0]
    kern = functools.partial(_swiglu_kernel, nchunk=nchunk)
    row = lambda i: (i, 0)
    cast_in, cast_out, cast_shape = _cast_side_job(cast_src, n // TM)
    return pl.pallas_call(
        kern,
        grid=(n // TM,),
        in_specs=[
            pl.BlockSpec((TM, d), row),
            pl.BlockSpec((TM, d), row),
            _resident_spec((d, dff), lambda i: (0, 0)),
            _resident_spec((d, dff), lambda i: (0, 1)),
            _resident_spec((dff, d), lambda i: (0, 0)),
            pl.BlockSpec((1, d), lambda i: (0, 0)),
            cast_in,
        ],
        out_specs=[pl.BlockSpec((TM, d), row), pl.BlockSpec((TM, d), row), cast_out],
        out_shape=[jax.ShapeDtypeStruct((n, d), F32), jax.ShapeDtypeStruct((n, d), BF16), cast_shape],
        compiler_params=_params(("arbitrary",)),
        name="dense_swiglu",
    )(hn, x, w_gu, w_gu, w_down, g, cast_src)


def _row_copy(src_ref, src_row, dst_ref, dst_row, sem):
    return pltpu.make_async_copy(src_ref.at[pl.ds(src_row, 1), :], dst_ref.at[pl.ds(dst_row, 1), :], sem)


SUBLANES = 8


def _zero_rows(zero_ref, dst_ref, dst_row, count, limit, sem, wait):
    head = jnp.minimum((-dst_row) & (SUBLANES - 1), count)
    for j in range(SUBLANES - 1):
        @pl.when(j < head)
        def _(j=j):
            copy = _row_copy(zero_ref, 0, dst_ref, dst_row + j, sem)
            copy.wait() if wait else copy.start()

    rest = count - head
    first = dst_row + head
    for bit in range(SUBLANES.bit_length() - 1, limit.bit_length() - 1):
        size = 1 << bit
        done = (rest >> (bit + 1)) << (bit + 1)

        @pl.when(((rest >> bit) & 1) == 1)
        def _(size=size, done=done):
            start = pl.multiple_of(first + done, SUBLANES)
            copy = pltpu.make_async_copy(zero_ref.at[pl.ds(0, size), :],
                                         dst_ref.at[pl.ds(start, size), :], sem)
            copy.wait() if wait else copy.start()


def _expert_kernel(te_ref, tv_ref, xs_ref, wg_ref, wu_ref, wd_ref, ys_ref, *, nchunk):
    i = pl.program_id(0)

    @pl.when(tv_ref[i] > 0)
    def _():
        x = xs_ref[...].astype(BF16)
        width = wg_ref.shape[2] // nchunk
        acc = None
        for c in range(nchunk):
            cs = slice(c * width, (c + 1) * width)
            gate = _dot(x, wg_ref[0, :, cs])
            up = _dot(x, wu_ref[0, :, cs])
            act = (gate * _sigmoid(gate) * up).astype(BF16)
            part = _dot(act, wd_ref[0, cs, :])
            acc = part if acc is None else acc + part
        ys_ref[...] = acc.astype(BF16).astype(F32)

    @pl.when(tv_ref[i] == 0)
    def _():
        ys_ref[...] = jnp.zeros_like(ys_ref)


def _experts(tile_expert, tile_valid, xs, w_gu, w_down):
    r, d = xs.shape
    n_tiles = r // TM_EXPERT
    dff = w_down.shape[1]
    kern = functools.partial(_expert_kernel, nchunk=7)
    return pl.pallas_call(
        kern,
        grid_spec=pltpu.PrefetchScalarGridSpec(
            num_scalar_prefetch=2,
            grid=(r // TM_EXPERT,),
            in_specs=[
                pl.BlockSpec((TM_EXPERT, d), lambda i, te, tv: (jnp.minimum(i, tv[n_tiles]), 0)),
                pl.BlockSpec((1, d, dff), lambda i, te, tv: (te[i], 0, 0)),
                pl.BlockSpec((1, d, dff), lambda i, te, tv: (te[i], 0, 1)),
                pl.BlockSpec((1, dff, d), lambda i, te, tv: (te[i], 0, 0)),
            ],
            out_specs=pl.BlockSpec((TM_EXPERT, d), lambda i, te, tv: (i, 0)),
        ),
        out_shape=jax.ShapeDtypeStruct((r, d), F32),
        compiler_params=_params(("arbitrary",)),
        name="moe_experts",
    )(tile_expert, tile_valid, xs, w_gu, w_gu, w_down)


RUN_BUF_ROWS = 2 * TM + LANES


def _run_positions(idx, off_row):
    lane = lax.broadcasted_iota(I32, idx.shape, 1)
    pos = []
    for slot in range(2):
        off = jnp.sum(jnp.where(lane == idx[:, slot:slot + 1], off_row, 0.0), axis=-1, keepdims=True)
        pos.append(off.astype(I32) + idx[:, 2 + slot:3 + slot])
    return pos


def _run_copies(tab_ref, step, hbm_ref, buf_ref, slot, sem, to_hbm, wait):
    for e in range(N_EXPERTS):
        entry = (step * N_EXPERTS + e) * 3
        off, row, length = tab_ref[entry], tab_ref[entry + 1], tab_ref[entry + 2]
        for bit in range(SUBLANES.bit_length() - 1, TM.bit_length()):
            size = 1 << bit
            done = (length >> (bit + 1)) << (bit + 1)

            @pl.when(((length >> bit) & 1) == 1)
            def _(size=size, done=done, off=off, row=row):
                in_buf = buf_ref.at[slot, pl.ds(pl.multiple_of(off + done, SUBLANES), size), :]
                in_hbm = hbm_ref.at[pl.ds(pl.multiple_of(row + done, SUBLANES), size), :]
                copy = (pltpu.make_async_copy(in_buf, in_hbm, sem.at[slot]) if to_hbm
                        else pltpu.make_async_copy(in_hbm, in_buf, sem.at[slot]))
                copy.wait() if wait else copy.start()


def _run_dispatch_kernel(tab_ref, pad_ref, hn_ref, idx_ref, off_ref, xs_ref, buf, zero_sc, sem, zsem,
                         *, n_steps):
    j = pl.program_id(0)
    slot = j & 1

    @pl.when(j == 0)
    def _():
        zero_sc[...] = jnp.zeros_like(zero_sc)
        rows = zero_sc.shape[0]
        for wait in (False, True):
            for e in range(N_EXPERTS):
                _zero_rows(zero_sc, xs_ref, pad_ref[2 * e], pad_ref[2 * e + 1], 2 * rows, zsem, wait)

            def tail(t, carry, wait=wait):
                first = pl.multiple_of(pad_ref[2 * N_EXPERTS] + t * rows, SUBLANES)
                copy = pltpu.make_async_copy(zero_sc, xs_ref.at[pl.ds(first, rows), :], zsem)
                copy.wait() if wait else copy.start()
                return carry

            lax.fori_loop(0, pad_ref[2 * N_EXPERTS + 1] // rows, tail, 0)

    @pl.when(j >= 2)
    def _():
        _run_copies(tab_ref, j - 2, xs_ref, buf, slot, sem, True, True)

    pos0, pos1 = _run_positions(idx_ref[...], off_ref[0])
    lane = lax.broadcasted_iota(I32, (hn_ref.shape[0], buf.shape[1]), 1)
    sel = jnp.where((lane == pos0) | (lane == pos1), 1.0, 0.0).astype(BF16)
    buf[slot] = _dot_tn(sel, hn_ref[...])
    _run_copies(tab_ref, j, xs_ref, buf, slot, sem, True, False)

    @pl.when(j == n_steps - 1)
    def _():
        _run_copies(tab_ref, j - 1, xs_ref, buf, 1 - slot, sem, True, True)
        _run_copies(tab_ref, j, xs_ref, buf, slot, sem, True, True)


def _run_dispatch(run_table, pad_table, hn, idx, run_offsets, n_rows):
    n, d = hn.shape
    n_steps = n // TM
    assert n_steps >= 2
    return pl.pallas_call(
        functools.partial(_run_dispatch_kernel, n_steps=n_steps),
        grid_spec=pltpu.PrefetchScalarGridSpec(
            num_scalar_prefetch=2,
            grid=(n_steps,),
            in_specs=[
                pl.BlockSpec((TM, d), lambda i, *_: (i, 0)),
                pl.BlockSpec((TM, LANES), lambda i, *_: (i, 0)),
                pl.BlockSpec((1, 1, LANES), lambda i, *_: (i, 0, 0)),
            ],
            out_specs=pl.BlockSpec(memory_space=pl.ANY),
            scratch_shapes=[pltpu.VMEM((2, RUN_BUF_ROWS, d), F32),
                            pltpu.VMEM((TM_EXPERT // 2, d), F32),
                            pltpu.SemaphoreType.DMA((2,)), pltpu.SemaphoreType.DMA(())],
        ),
        out_shape=jax.ShapeDtypeStruct((n_rows, d), F32),
        compiler_params=_params(("arbitrary",)),
        name="moe_dispatch",
    )(run_table, pad_table, hn, idx, run_offsets)


def _run_combine_kernel(tab_ref, ys_ref, x_ref, gate_ref, idx_ref, off_ref, g_ref, yp_ref, ysm_ref,
                        buf, sem, *, n_first, n_steps):
    j = pl.program_id(0)
    slot = j & 1

    @pl.when(j == 0)
    def _():
        buf[...] = jnp.zeros_like(buf)
        _run_copies(tab_ref, 0, ys_ref, buf, 0, sem, False, False)

    @pl.when(j + 1 < n_steps)
    def _():
        _run_copies(tab_ref, j + 1, ys_ref, buf, 1 - slot, sem, False, False)

    _run_copies(tab_ref, j, ys_ref, buf, slot, sem, False, True)
    runs = buf[slot].astype(BF16)
    pos = _run_positions(idx_ref[...], off_ref[0])
    lane = lax.broadcasted_iota(I32, (x_ref.shape[0], buf.shape[1]), 1)
    gate = gate_ref[...]
    y = x_ref[...]
    for k in range(2):
        sel = jnp.where(lane == pos[k], 1.0, 0.0).astype(BF16)
        y = y + gate[:, k:k + 1] * _dot(sel, runs)
    y = _rms(y, g_ref[...])

    @pl.when(j < n_first)
    def _():
        yp_ref[...] = y

    @pl.when(j >= n_first)
    def _():
        ysm_ref[...] = y


def _run_combine(run_table, ys, x, gates, idx, run_offsets, g, n_p):
    n, d = x.shape
    n_first = n_p // TM
    n_steps = n // TM
    p_spec, s_spec = _split_specs(n_first, d)
    row = lambda i, *_: (i, 0)
    return pl.pallas_call(
        functools.partial(_run_combine_kernel, n_first=n_first, n_steps=n_steps),
        grid_spec=pltpu.PrefetchScalarGridSpec(
            num_scalar_prefetch=1,
            grid=(n_steps,),
            in_specs=[
                pl.BlockSpec(memory_space=pl.ANY),
                pl.BlockSpec((TM, d), row),
                pl.BlockSpec((TM, LANES), row),
                pl.BlockSpec((TM, LANES), row),
                pl.BlockSpec((1, 1, LANES), lambda i, *_: (i, 0, 0)),
                pl.BlockSpec((1, d), lambda i, *_: (0, 0)),
            ],
            out_specs=[p_spec, s_spec],
            scratch_shapes=[pltpu.VMEM((2, RUN_BUF_ROWS, d), F32), pltpu.SemaphoreType.DMA((2,))],
        ),
        out_shape=[jax.ShapeDtypeStruct((n_p, d), F32), jax.ShapeDtypeStruct((n - n_p, d), F32)],
        compiler_params=_params(("arbitrary",)),
        name="moe_combine",
    )(run_table, ys, x, gates, idx, run_offsets, g)


def _run_tables(tile_counts, n_rows):
    cnt = tile_counts[:, 0, :N_EXPERTS].astype(I32)
    run = ((cnt + SUBLANES - 1) // SUBLANES) * SUBLANES
    off = jnp.cumsum(run, axis=1) - run
    used = jnp.sum(run, axis=0)
    padded = ((used + TM_EXPERT - 1) // TM_EXPERT) * TM_EXPERT
    ends = jnp.cumsum(padded)
    starts = ends - padded
    row = starts[None, :] + jnp.cumsum(run, axis=0) - run
    run_table = jnp.stack([off, row, run], axis=2).reshape(-1)
    run_offsets = jnp.pad(off.astype(F32), ((0, 0), (0, LANES - N_EXPERTS)))[:, None, :]
    n_tiles = n_rows // TM_EXPERT
    used_tiles = ends[-1] // TM_EXPERT
    pad_table = jnp.stack([starts + used, padded - used], axis=1).reshape(-1)
    pad_table = jnp.concatenate([pad_table, jnp.stack([ends[-1], n_rows - ends[-1]])])
    tile_start = jnp.arange(n_tiles, dtype=I32) * TM_EXPERT
    tile_valid = jnp.concatenate([(tile_start < ends[-1]).astype(I32), (used_tiles - 1)[None]])
    tile_expert = jnp.sum((tile_start[:, None] >= ends[None, :]).astype(I32), axis=1)
    last_expert = jnp.sum((ends[-1] - 1 >= ends).astype(I32))
    tile_expert = jnp.minimum(tile_expert, last_expert)
    return run_table, run_offsets, pad_table, tile_expert, tile_valid


def _pad_time(a, nseq, t, fill=None):
    w = a.shape[1]
    a = a.reshape(nseq, t, w)
    a = jnp.pad(a, ((0, 0), (0, SAMPLE_PAD_T - t), (0, 0)))
    if fill is not None:
        pad_row = jnp.arange(SAMPLE_PAD_T)[None, :, None] >= t
        lane = jnp.arange(w)[None, None, :]
        a = jnp.where(pad_row & (lane < MLSTM_HEADS), fill, a)
    return a.reshape(nseq * SAMPLE_PAD_T, w)


def _unpad_time(a, nseq, t):
    w = a.shape[1]
    return a.reshape(nseq, SAMPLE_PAD_T, w)[:, :t].reshape(nseq * t, w)


def kernel(x_prompt, x_sample, state_gla_S, state_mlstm_C, state_mlstm_n, state_mlstm_m,
           norm_mix, norm_ffn, norm_final,
           gla_w_in, gla_w_a2, gla_b_a, gla_g_norm, gla_w_out,
           mlstm_w_in, mlstm_b_gate, mlstm_g_norm, mlstm_w_out,
           ffn_w_gu, ffn_w_down,
           moe_w_router, moe_b_router, moe_w_gu, moe_w_down):
    bp, tp, d = x_prompt.shape
    bs, ts, _ = x_sample.shape
    n_p = bp * tp
    n_s = bs * ts
    n = n_p + n_s
    assert n_p % TM == 0 and n_s % TM == 0 and tp % (2 * CHUNK) == 0 and ts <= SAMPLE_PAD_T
    assert norm_mix.shape[0] == 2, "one GLA layer followed by one mLSTM layer"
    qk_w = d // 2
    main_w = 2 * qk_w + 2 * d

    x_p = x_prompt.reshape(n_p, d)
    x_s = x_sample.reshape(n_s, d)

    w_in = gla_w_in[0]
    w_lr = jnp.pad(w_in[:, main_w:], ((0, 0), (0, LANES - GLA_RANK))).astype(BF16)
    w_a2 = jnp.pad(gla_w_a2[0], ((0, LANES - GLA_RANK), (0, 0))).astype(BF16)
    z, log_a, ffn_wgu = _gla_in_proj(x_p, x_s, norm_mix[0][None], w_in, main_w, w_lr, w_a2,
                                     gla_b_a[0][None], ffn_w_gu[0])
    s0_p = jnp.zeros((bp,) + state_gla_S.shape[2:], F32)
    u_p, s_p = _gla_recurrence(z, log_a, s0_p, gla_g_norm[0], bp, tp, CHUNK, 16 * CHUNK)
    u_s, s_s = _gla_recurrence_short(_pad_time(z[n_p:], bs, ts), _pad_time(log_a[n_p:], bs, ts),
                                     state_gla_S[0], gla_g_norm[0], bs, SAMPLE_PAD_T,
                                     SAMPLE_SEQS_PER_STEP)
    x, hn, ffn_wd = _out_proj(u_p, _unpad_time(u_s, bs, ts), x_p, x_s, gla_w_out[0].astype(BF16),
                              norm_ffn[0][None], ffn_w_down[0])
    x, hn, moe_wgu = _swiglu(hn, x, ffn_wgu, ffn_wd,
                             norm_mix[1][None], SWIGLU_CHUNKS,
                             moe_w_gu[0].reshape(N_EXPERTS * d, moe_w_gu.shape[3]))

    w_in = mlstm_w_in[0]
    w_gt = jnp.pad(w_in[:, main_w:], ((0, 0), (0, LANES - 2 * MLSTM_HEADS))).astype(BF16)
    b_gt = jnp.pad(mlstm_b_gate[0], (0, LANES - 2 * MLSTM_HEADS))[None]
    dff_e = moe_w_down.shape[2]
    z, gates, moe_wd = _mlstm_in_proj(hn, w_in, main_w, w_gt, b_gt,
                                      moe_w_down[0].reshape(N_EXPERTS * dff_e, d))
    pairs = MLSTM_HEADS // 2
    dk2 = 2 * state_mlstm_C.shape[3]
    dvm = state_mlstm_C.shape[4]
    c0_p = jnp.zeros((bp, pairs, dk2, dvm), F32)
    n0_p = jnp.zeros((bp, pairs, dk2), F32)
    m0_p = jnp.zeros((bp, 1, LANES), F32)
    u_p, c_p, nn_p, m_p = _mlstm_recurrence_blocked(z, gates, c0_p, n0_p, m0_p,
                                                    mlstm_g_norm[0], bp, tp, CHUNK, 16 * CHUNK)
    c0_s = state_mlstm_C[0]
    n0_s = state_mlstm_n[0].reshape(bs, pairs, dk2)
    m0_s = jnp.pad(state_mlstm_m[0], ((0, 0), (0, LANES - MLSTM_HEADS)))[:, None, :]
    u_s, c_s, nn_s, m_s = _mlstm_recurrence_short(
        _pad_time(z[n_p:], bs, ts), _pad_time(gates[n_p:], bs, ts, fill=NEG_BIG),
        c0_s, n0_s, m0_s, mlstm_g_norm[0], bs, SAMPLE_PAD_T, SAMPLE_SEQS_PER_STEP)

    w_r = jnp.pad(moe_w_router[0], ((0, 0), (0, LANES - N_EXPERTS)))
    w_r_hi = w_r.astype(BF16)
    w_r = jnp.stack([w_r_hi, (w_r - w_r_hi.astype(F32)).astype(BF16)])
    b_r = jnp.pad(moe_b_router[0], (0, LANES - N_EXPERTS))[None]
    x, hn, route_g, route_i, tile_counts = _out_proj_router(
        u_p, _unpad_time(u_s, bs, ts), x, mlstm_w_out[0].astype(BF16), norm_ffn[1][None], w_r, b_r)
    n_rows = 2 * n + (n // TM) * N_EXPERTS * (SUBLANES - 1) + N_EXPERTS * (TM_EXPERT - 1)
    n_rows = -(-n_rows // TM_EXPERT) * TM_EXPERT
    run_table, run_offsets, pad_table, tile_expert, tile_valid = _run_tables(tile_counts, n_rows)
    xs = _run_dispatch(run_table, pad_table, hn, route_i, run_offsets, n_rows)
    ys = _experts(tile_expert, tile_valid, xs, moe_wgu.reshape(moe_w_gu.shape[1:]),
                  moe_wd.reshape(moe_w_down.shape[1:]))
    y_p, y_s = _run_combine(run_table, ys, x, route_g, route_i, run_offsets, norm_final[None], n_p)

    y_prompt = y_p.reshape(bp, tp, d)
    y_sample = y_s.reshape(bs, ts, d)
    hd = state_mlstm_C.shape[2:]
    return (y_prompt, y_sample,
            s_p[None], c_p.reshape((1, bp) + hd), nn_p.reshape(1, bp, hd[0], hd[1]),
            m_p[None, :, 0, :MLSTM_HEADS],
            s_s[None], c_s[None], nn_s.reshape(1, bs, hd[0], hd[1]), m_s[None, :, 0, :MLSTM_HEADS])


def _mlstm_in_proj(hn, w_in, wz, w_gate, b_gate, cast_src):
    n, d = hn.shape
    cast_in, cast_out, cast_shape = _cast_side_job(cast_src, n // TM)

    def kern(h_ref, w_ref, wg_ref, b_ref, cast_ref, z_ref, gate_ref, cast_out_ref, wb_sc):
        @pl.when(pl.program_id(0) == 0)
        def _():
            wb_sc[...] = w_ref[...].astype(BF16)

        cast_out_ref[...] = cast_ref[...].astype(BF16)
        h = h_ref[...]
        width = wz // IN_PROJ_COL_CHUNKS
        for c in range(IN_PROJ_COL_CHUNKS):
            cs = slice(c * width, (c + 1) * width)
            z_ref[:, cs] = _dot(h, wb_sc[:, cs])
        gp = _dot(h, wg_ref[...]) + b_ref[...]
        gc = GATE_CAP * jnp.tanh(gp * (1.0 / GATE_CAP))
        lane = lax.broadcasted_iota(I32, gc.shape, 1)
        out = jnp.where(lane < MLSTM_HEADS, gc, _log_sigmoid(gc))
        gate_ref[...] = jnp.where(lane < 2 * MLSTM_HEADS, out, 0.0)

    return pl.pallas_call(
        kern,
        grid=(n // TM,),
        in_specs=[
            pl.BlockSpec((TM, d), lambda i: (i, 0)),
            _resident_spec((d, wz), lambda i: (0, 0)),
            _resident_spec((d, LANES), lambda i: (0, 0)),
            pl.BlockSpec((1, LANES), lambda i: (0, 0)),
            cast_in,
        ],
        out_specs=[pl.BlockSpec((TM, wz), lambda i: (i, 0)), pl.BlockSpec((TM, LANES), lambda i: (i, 0)),
                   cast_out],
        out_shape=[jax.ShapeDtypeStruct((n, wz), F32), jax.ShapeDtypeStruct((n, LANES), F32),
                   cast_shape],
        scratch_shapes=[pltpu.VMEM((d, wz), BF16)],
        compiler_params=_params(("arbitrary",)),
        name="in_proj_mlstm",
    )(hn, w_in, w_gate, b_gate, cast_src)
```

```python
import functools

import jax
import jax.numpy as jnp
import numpy as np
from jax import lax
from jax.experimental import pallas as pl
from jax.experimental.pallas import tpu as pltpu

F32 = jnp.float32
BF16 = jnp.bfloat16
I32 = jnp.int32

EPS = 1e-6
GLA_HEADS = 4
GLA_RANK = 16
GLA_TAU = 16.0
MLSTM_HEADS = 8
GATE_CAP = 15.0
CHUNK = 64
N_EXPERTS = 8
NEG_BIG = -1e30

LANES = 128
TM = 512
TM_EXPERT = 512
SAMPLE_PAD_T = 8
SAMPLE_SEQS_PER_STEP = 16
SWIGLU_CHUNKS = 11
EXPERT_CHUNKS = 7
VMEM_LIMIT = 56 * 1024 * 1024


def _dot(a, b):
    return jnp.dot(a, b, preferred_element_type=F32)


def _dot_nt(a, b):
    return lax.dot_general(a, b, (((1,), (1,)), ((), ())), preferred_element_type=F32)


def _dot_tn(a, b):
    return lax.dot_general(a, b, (((0,), (0,)), ((), ())), preferred_element_type=F32)


def _sigmoid(x):
    return 1.0 / (1.0 + jnp.exp(-x))


def _log_sigmoid(x):
    return jnp.minimum(x, 0.0) - jnp.log(1.0 + jnp.exp(-jnp.abs(x)))


def _rms(x, g):
    return x * lax.rsqrt(jnp.mean(x * x, axis=-1, keepdims=True) + EPS) * g


def _cumsum_rows(x, n):
    ridx = lax.broadcasted_iota(I32, x.shape, 0)
    s = 1
    while s < n:
        x = x + jnp.where(ridx >= s, pltpu.roll(x, s, 0), 0.0)
        s *= 2
    return x


def _params(sem):
    return pltpu.CompilerParams(dimension_semantics=sem, vmem_limit_bytes=VMEM_LIMIT)


def _resident_spec(block_shape, index_map):
    return pl.BlockSpec(block_shape, index_map, pipeline_mode=pl.Buffered(1))


def _cast_side_job(src, n_steps):
    rows, cols = src.shape
    n_slabs = max(s for s in range(1, n_steps + 1) if rows % (16 * s) == 0)
    spec = pl.BlockSpec((rows // n_slabs, cols), lambda i, *_: (jnp.minimum(i, n_slabs - 1), 0))
    return spec, spec, jax.ShapeDtypeStruct(src.shape, BF16)


IN_PROJ_COL_CHUNKS = 4


def _pick_rows(i, n_first, first_ref, second_ref):
    return jnp.where(i < n_first, first_ref[...], second_ref[...])


def _split_specs(n_first, width):
    return [pl.BlockSpec((TM, width), lambda i, *_: (jnp.minimum(i, n_first - 1), 0)),
            pl.BlockSpec((TM, width), lambda i, *_: (jnp.maximum(i - n_first, 0), 0))]


def _gla_in_proj_kernel(xp_ref, xs_ref, g_ref, w_ref, wg_ref, w2_ref, b_ref, cast_ref,
                        z_ref, gate_ref, cast_out_ref, wb_sc, *, n_first):
    @pl.when(pl.program_id(0) == 0)
    def _():
        wb_sc[...] = w_ref[...].astype(BF16)

    cast_out_ref[...] = cast_ref[...].astype(BF16)
    x = _pick_rows(pl.program_id(0), n_first, xp_ref, xs_ref)
    hn = _rms(x, g_ref[...]).astype(BF16)
    width = wb_sc.shape[1] // IN_PROJ_COL_CHUNKS
    for c in range(IN_PROJ_COL_CHUNKS):
        cs = slice(c * width, (c + 1) * width)
        z_ref[:, cs] = _dot(hn, wb_sc[:, cs])
    a = _dot(hn, wg_ref[...])
    la = _dot(a.astype(BF16), w2_ref[...]) + b_ref[...]
    gate_ref[...] = _log_sigmoid(la) * (1.0 / GLA_TAU)


def _gla_in_proj(x_p, x_s, g, w_in, wz, w_gate, w2, b, cast_src):
    d = x_p.shape[1]
    n = x_p.shape[0] + x_s.shape[0]
    n_first = x_p.shape[0] // TM
    gw = b.shape[1]
    cast_in, cast_out, cast_shape = _cast_side_job(cast_src, n // TM)
    return pl.pallas_call(
        functools.partial(_gla_in_proj_kernel, n_first=n_first),
        grid=(n // TM,),
        in_specs=_split_specs(n_first, d) + [
            pl.BlockSpec((1, d), lambda i: (0, 0)),
            _resident_spec((d, wz), lambda i: (0, 0)),
            pl.BlockSpec((d, LANES), lambda i: (0, 0)),
            pl.BlockSpec(w2.shape, lambda i: (0, 0)),
            pl.BlockSpec((1, gw), lambda i: (0, 0)),
            cast_in,
        ],
        out_specs=[
            pl.BlockSpec((TM, wz), lambda i: (i, 0)),
            pl.BlockSpec((TM, gw), lambda i: (i, 0)),
            cast_out,
        ],
        out_shape=[jax.ShapeDtypeStruct((n, wz), F32), jax.ShapeDtypeStruct((n, gw), F32), cast_shape],
        scratch_shapes=[pltpu.VMEM((d, wz), BF16)],
        compiler_params=_params(("arbitrary",)),
        name="in_proj_gla",
    )(x_p, x_s, g, w_in, w_gate, w2, b, cast_src)


def _gla_kernel(q_ref, k_ref, v_ref, r_ref, la_ref, s0_ref, gn_ref, u_ref, st_ref, st_sc,
                *, chunk, tb, nblk):
    heads = GLA_HEADS
    dk = q_ref.shape[1] // heads
    dv = v_ref.shape[1] // heads
    c = pl.program_id(1)

    @pl.when(c == 0)
    def _():
        for h in range(heads):
            st_sc[h] = s0_ref[0, h].T

    ri = lax.broadcasted_iota(I32, (chunk, chunk), 0)
    ci = lax.broadcasted_iota(I32, (chunk, chunk), 1)
    causal = ri >= ci
    kscale = dk ** -0.5
    state = [st_sc[h] for h in range(heads)]
    for j in range(tb // chunk):
        rows = slice(j * chunk, (j + 1) * chunk)
        b_all = _cumsum_rows(la_ref[rows, :], chunk)
        for h in range(heads):
            ks = slice(h * dk, (h + 1) * dk)
            vs = slice(h * dv, (h + 1) * dv)
            b = b_all[:, ks]
            b_end = b[chunk - 1:chunk, :]
            q = q_ref[rows, ks]
            k = k_ref[rows, ks] * kscale
            qg = (q * jnp.exp(b)).astype(BF16)
            kg = (k * jnp.exp(-b)).astype(BF16)
            ke = (k * jnp.exp(b_end - b)).astype(BF16)
            v = v_ref[rows, vs].astype(BF16)
            a = jnp.where(causal, _dot_nt(qg, kg), 0.0).astype(BF16)
            o = _dot_nt(qg, state[h].astype(BF16)) + _dot(a, v)
            state[h] = state[h] * jnp.exp(b_end) + _dot_tn(v, ke)
            on = _rms(o, gn_ref[h:h + 1, :])
            r = r_ref[rows, vs]
            u_ref[rows, vs] = (r * _sigmoid(r) * on).astype(BF16)
    for h in range(heads):
        st_sc[h] = state[h]

    @pl.when(c == nblk - 1)
    def _():
        for h in range(heads):
            st_ref[0, h] = st_sc[h].T


def _gla_recurrence(z, log_a, s0, g_norm, nseq, t, chunk, tb):
    qk_w = log_a.shape[1]
    v_w = 2 * qk_w
    nblk = t // tb
    heads, dk, dv = s0.shape[1:]
    kern = functools.partial(_gla_kernel, chunk=chunk, tb=tb, nblk=nblk)
    row = lambda b, c: b * nblk + c
    return pl.pallas_call(
        kern,
        grid=(nseq, nblk),
        in_specs=[
            pl.BlockSpec((tb, qk_w), lambda b, c: (row(b, c), 0)),
            pl.BlockSpec((tb, qk_w), lambda b, c: (row(b, c), 1)),
            pl.BlockSpec((tb, v_w), lambda b, c: (row(b, c), 1)),
            pl.BlockSpec((tb, v_w), lambda b, c: (row(b, c), 2)),
            pl.BlockSpec((tb, qk_w), lambda b, c: (row(b, c), 0)),
            pl.BlockSpec((1, heads, dk, dv), lambda b, c: (b, 0, 0, 0)),
            pl.BlockSpec((heads, dv), lambda b, c: (0, 0)),
        ],
        out_specs=[
            pl.BlockSpec((tb, v_w), lambda b, c: (row(b, c), 0)),
            pl.BlockSpec((1, heads, dk, dv), lambda b, c: (b, 0, 0, 0)),
        ],
        out_shape=[
            jax.ShapeDtypeStruct((nseq * t, v_w), BF16),
            jax.ShapeDtypeStruct(s0.shape, F32),
        ],
        scratch_shapes=[pltpu.VMEM((heads, dv, dk), F32)],
        compiler_params=_params(("arbitrary", "arbitrary")),
        name="gla_recurrence_t%d" % t,
    )(z, z, z, z, log_a, s0, g_norm)


def _gla_short_kernel(q_ref, k_ref, v_ref, r_ref, la_ref, s0_ref, gn_ref, u_ref, st_ref, *, t, nsq):
    heads = GLA_HEADS
    dk = q_ref.shape[1] // heads
    dv = v_ref.shape[1] // heads
    nrow = nsq * t
    width = q_ref.shape[1]
    rc = lax.broadcasted_iota(I32, (nrow, width), 0) & (t - 1)
    b = _chunk_scan(la_ref[...], rc, t, jnp.add, 0.0)
    b_last = b.reshape(nsq, t, width)[:, t - 1:t, :]
    b_end = jnp.broadcast_to(b_last, (nsq, t, width)).reshape(nrow, width)
    k = k_ref[...] * (dk ** -0.5)
    qg = q_ref[...] * jnp.exp(b)
    kg = k * jnp.exp(-b)
    ke = k * jnp.exp(b_end - b)
    decay = jnp.exp(b_end)
    ri = lax.broadcasted_iota(I32, (nrow, nrow), 0)
    ci = lax.broadcasted_iota(I32, (nrow, nrow), 1)
    shift = t.bit_length() - 1
    mask = ((ri >> shift) == (ci >> shift)) & (ri >= ci)
    for h in range(heads):
        ks = slice(h * dk, (h + 1) * dk)
        vs = slice(h * dv, (h + 1) * dv)
        a = jnp.where(mask, _dot_nt(qg[:, ks].astype(BF16), kg[:, ks].astype(BF16)), 0.0)
        o = _dot(a.astype(BF16), v_ref[:, vs].astype(BF16))
        decay_t = decay[:, ks].T
        o_state = []
        for sq in range(nsq):
            r = slice(sq * t, (sq + 1) * t)
            s0 = s0_ref[sq, h]
            o_state.append(_dot(qg[r, ks].astype(BF16), s0.astype(BF16)))
            upd = _dot_tn(ke[r, ks].astype(BF16), v_ref[r, vs].astype(BF16))
            st_ref[sq, h] = decay_t[:, sq * t:sq * t + 1] * s0 + upd
        on = _rms(o + jnp.concatenate(o_state, axis=0), gn_ref[h:h + 1, :])
        rg = r_ref[:, vs]
        u_ref[:, vs] = (rg * _sigmoid(rg) * on).astype(BF16)


def _gla_recurrence_short(z, log_a, s0, g_norm, nseq, t, nsq):
    qk_w = log_a.shape[1]
    v_w = 2 * qk_w
    heads, dk, dv = s0.shape[1:]
    assert nsq * t == dk == LANES
    rb = nsq * t
    return pl.pallas_call(
        functools.partial(_gla_short_kernel, t=t, nsq=nsq),
        grid=(nseq // nsq,),
        in_specs=[
            pl.BlockSpec((rb, qk_w), lambda b: (b, 0)),
            pl.BlockSpec((rb, qk_w), lambda b: (b, 1)),
            pl.BlockSpec((rb, v_w), lambda b: (b, 1)),
            pl.BlockSpec((rb, v_w), lambda b: (b, 2)),
            pl.BlockSpec((rb, qk_w), lambda b: (b, 0)),
            pl.BlockSpec((nsq, heads, dk, dv), lambda b: (b, 0, 0, 0)),
            pl.BlockSpec((heads, dv), lambda b: (0, 0)),
        ],
        out_specs=[
            pl.BlockSpec((rb, v_w), lambda b: (b, 0)),
            pl.BlockSpec((nsq, heads, dk, dv), lambda b: (b, 0, 0, 0)),
        ],
        out_shape=[
            jax.ShapeDtypeStruct((nseq * t, v_w), BF16),
            jax.ShapeDtypeStruct(s0.shape, F32),
        ],
        compiler_params=_params(("arbitrary",)),
        name="gla_recurrence_t%d" % t,
    )(z, z, z, z, log_a, s0, g_norm)


def _mlstm_short_kernel(q_ref, k_ref, v_ref, op_ref, gt_ref, e_ref, c0_ref, n0_ref, m0_ref, gn_ref,
                        u_ref, ct_ref, nt_ref, mt_ref, *, t, nsq):
    heads = MLSTM_HEADS
    pairs = heads // 2
    dv = v_ref.shape[1] // heads
    dk = q_ref.shape[1] // heads
    nrow = nsq * t

    def per_seq(x3):
        return jnp.broadcast_to(x3, (nsq, t, x3.shape[2])).reshape(nrow, x3.shape[2])

    def last(x):
        return x.reshape(nsq, t, x.shape[1])[:, t - 1:t, :]

    rc = lax.broadcasted_iota(I32, (nrow, LANES), 0) & (t - 1)
    gts = gt_ref[...]
    f_cum = pltpu.roll(_chunk_scan(gts, rc, t, jnp.add, 0.0), LANES - heads, 1)
    b = gts - f_cum
    cmb = _chunk_scan(b, rc, t, jnp.maximum, NEG_BIG)
    mp = per_seq(m0_ref[...])
    big_m = jnp.maximum(mp, cmb)
    m_t = f_cum + big_m
    m_new = last(m_t)
    mt_ref[...] = m_new
    mn = per_seq(m_new)
    fe = per_seq(last(f_cum))
    w_i = jnp.exp(mp - big_m)
    em = jnp.exp(-m_t)
    k_sc = jnp.exp(fe - f_cum + gts - mn)
    w_c = jnp.exp(fe + mp - mn)
    ex = _dot(_pack_terms([w_i, k_sc, -big_m, em, w_c], heads), e_ref[...])
    offs = np.cumsum([0] + [heads * w for w in SHORT_EXPAND_WIDTHS])
    q = q_ref[...]
    k = k_ref[...] * (dk ** -0.5)
    qs = q * ex[:, offs[0]:offs[1]]
    ke = k * ex[:, offs[1]:offs[2]]
    neg_m = ex[:, offs[2]:offs[3]]
    em_v = ex[:, offs[3]:offs[4]]
    wc_k = ex[:, offs[4]:offs[5]]
    bt = b.T

    ri = lax.broadcasted_iota(I32, (nrow, nrow), 0)
    ci = lax.broadcasted_iota(I32, (nrow, nrow), 1)
    shift = t.bit_length() - 1
    mask = ((ri >> shift) == (ci >> shift)) & (ri >= ci)
    lane1 = lax.broadcasted_iota(I32, (1, LANES), 1)
    rowi = lax.broadcasted_iota(I32, (LANES, LANES), 0)
    ones_v = jnp.ones((nrow, dv), BF16)
    ones_k = jnp.ones((LANES, dv), BF16)
    for p in range(pairs):
        ps = slice(p * LANES, (p + 1) * LANES)
        qp = q[:, ps]
        qsp = qs[:, ps]
        kpb = k[:, ps].astype(BF16)
        n_rows = per_seq(n0_ref[:, p:p + 1, :])
        intra, den_state, qsh = [], [], []
        for jj in range(2):
            h = 2 * p + jj
            hs = slice(h * dv, (h + 1) * dv)
            mine = (lane1 >= jj * dk) & (lane1 < (jj + 1) * dk)
            qh = jnp.where(mine, qp, 0.0).astype(BF16)
            qsh.append(jnp.where(mine, qsp, 0.0))
            d = neg_m[:, hs] + bt[h:h + 1, :]
            s = _dot_nt(qh, kpb) * jnp.where(mask, jnp.exp(d), 0.0)
            vh = jnp.concatenate([v_ref[:, hs].astype(BF16), ones_v], axis=1)
            intra.append(_dot(s.astype(BF16), vh))
            den_state.append(_dot((qsh[jj] * n_rows).astype(BF16), ones_k))
        num_state = [[], []]
        for sq in range(nsq):
            r = slice(sq * t, (sq + 1) * t)
            lhs = jnp.concatenate([qsh[0][r], qsh[1][r]], axis=0).astype(BF16)
            c_pair = jnp.concatenate([c0_ref[sq, 2 * p], c0_ref[sq, 2 * p + 1]], axis=0)
            res = _dot(lhs, c_pair.astype(BF16))
            num_state[0].append(res[:t])
            num_state[1].append(res[t:])
        for jj in range(2):
            h = 2 * p + jj
            hs = slice(h * dv, (h + 1) * dv)
            num = intra[jj][:, :dv] + jnp.concatenate(num_state[jj], axis=0)
            den = intra[jj][:, dv:] + den_state[jj]
            hh = num / jnp.maximum(jnp.abs(den), em_v[:, hs])
            hn = _rms(hh, gn_ref[h:h + 1, :])
            u_ref[:, hs] = (_sigmoid(op_ref[:, hs]) * hn).astype(BF16)
        kep = ke[:, ps]
        for sq in range(nsq):
            r = slice(sq * t, (sq + 1) * t)
            vp = v_ref[r, 2 * p * dv:(2 * p + 2) * dv].astype(BF16)
            full = _dot_tn(kep[r].astype(BF16), vp)
            upd = jnp.where(rowi < dk, full[:, :dv], full[:, dv:])
            w_row = w_c[sq * t:sq * t + 1, :]
            w_col = jnp.where(rowi < dk, w_row[:, 2 * p:2 * p + 1], w_row[:, 2 * p + 1:2 * p + 2])
            c_pair = jnp.concatenate([c0_ref[sq, 2 * p], c0_ref[sq, 2 * p + 1]], axis=0)
            c_new = w_col * c_pair + upd
            ct_ref[sq, 2 * p] = c_new[:dk]
            ct_ref[sq, 2 * p + 1] = c_new[dk:]
        k_sum = jnp.sum(kep.reshape(nsq, t, LANES), axis=1, keepdims=True)
        nt_ref[:, p:p + 1, :] = last(wc_k[:, ps]) * n0_ref[:, p:p + 1, :] + k_sum


def _mlstm_recurrence_short(z, gates, c0, n0, m0, g_norm, nseq, t, nsq):
    heads = g_norm.shape[0]
    v_w = heads * g_norm.shape[1]
    qk_w = v_w // 2
    assert nsq * t == LANES
    expand = _expand_matrix(heads, SHORT_EXPAND_WIDTHS)
    kern = functools.partial(_mlstm_short_kernel, t=t, nsq=nsq)
    rb = nsq * t
    st4 = lambda b: (b, 0, 0, 0)
    st3 = lambda b: (b, 0, 0)
    return pl.pallas_call(
        kern,
        grid=(nseq // nsq,),
        in_specs=[
            pl.BlockSpec((rb, qk_w), lambda b: (b, 0)),
            pl.BlockSpec((rb, qk_w), lambda b: (b, 1)),
            pl.BlockSpec((rb, v_w), lambda b: (b, 1)),
            pl.BlockSpec((rb, v_w), lambda b: (b, 2)),
            pl.BlockSpec((rb, LANES), lambda b: (b, 0)),
            pl.BlockSpec(expand.shape, lambda b: (0, 0)),
            pl.BlockSpec((nsq,) + c0.shape[1:], st4),
            pl.BlockSpec((nsq,) + n0.shape[1:], st3),
            pl.BlockSpec((nsq,) + m0.shape[1:], st3),
            pl.BlockSpec(g_norm.shape, lambda b: (0, 0)),
        ],
        out_specs=[
            pl.BlockSpec((rb, v_w), lambda b: (b, 0)),
            pl.BlockSpec((nsq,) + c0.shape[1:], st4),
            pl.BlockSpec((nsq,) + n0.shape[1:], st3),
            pl.BlockSpec((nsq,) + m0.shape[1:], st3),
        ],
        out_shape=[
            jax.ShapeDtypeStruct((nseq * t, v_w), BF16),
            jax.ShapeDtypeStruct(c0.shape, F32),
            jax.ShapeDtypeStruct(n0.shape, F32),
            jax.ShapeDtypeStruct(m0.shape, F32),
        ],
        compiler_params=_params(("arbitrary",)),
        name="mlstm_recurrence_t%d" % t,
    )(z, z, z, z, gates, expand, c0, n0, m0, g_norm)


N_SPLIT = 3
EXPAND_WIDTHS = (64, 64, 128, 128)
SHORT_EXPAND_WIDTHS = EXPAND_WIDTHS + (64,)


def _expand_matrix(heads, widths):
    assert len(widths) * N_SPLIT * heads <= LANES
    cols = []
    for qi, width in enumerate(widths):
        sel = np.zeros((LANES, heads * width), np.float32)
        for term in range(N_SPLIT):
            for h in range(heads):
                sel[(qi * N_SPLIT + term) * heads + h, h * width:(h + 1) * width] = 1.0
        cols.append(sel)
    return jnp.asarray(np.concatenate(cols, axis=1), BF16)


def _chunk_scan(x, rc, n, op, fill):
    s = 1
    while s < n:
        x = op(x, jnp.where(rc >= s, pltpu.roll(x, s, 0), fill))
        s *= 2
    return x


def _pack_terms(values, heads):
    lane = lax.broadcasted_iota(I32, values[0].shape, 1)
    packed = jnp.zeros(values[0].shape, F32)
    slot = 0
    for val in values:
        rest = val
        for term in range(N_SPLIT):
            part = rest.astype(BF16).astype(F32)
            rest = rest - part
            moved = part if slot == 0 else pltpu.roll(part, slot * heads, 1)
            packed = jnp.where((lane >= slot * heads) & (lane < (slot + 1) * heads), moved, packed)
            slot += 1
    return packed.astype(BF16)


def _mlstm_block_kernel(q_ref, k_ref, v_ref, op_ref, gt_ref, e_ref, c0_ref, n0_ref, m0_ref, gn_ref,
                        u_ref, ct_ref, nt_ref, mt_ref, c_sc, nb_sc, m_sc, *, chunk, tb, nblk):
    heads = MLSTM_HEADS
    pairs = heads // 2
    dv = v_ref.shape[1] // heads
    dk = q_ref.shape[1] // heads
    c = pl.program_id(1)

    @pl.when(c == 0)
    def _():
        c_sc[...] = c0_ref[0]
        for p in range(pairs):
            nb_sc[p] = jnp.broadcast_to(n0_ref[0, p:p + 1, :], (LANES, LANES)).T
        m_sc[...] = m0_ref[0]

    nch = tb // chunk
    rc = lax.broadcasted_iota(I32, (tb, LANES), 0) & (chunk - 1)
    gts = gt_ref[...]
    f_cum = pltpu.roll(_chunk_scan(gts, rc, chunk, jnp.add, 0.0), LANES - heads, 1)
    b = gts - f_cum
    cmb = _chunk_scan(b, rc, chunk, jnp.maximum, NEG_BIG)
    m_prev = m_sc[...]
    mp_rows, mn_rows, fe_rows, w_c = [], [], [], []
    for j in range(nch):
        last = (j + 1) * chunk - 1
        f_end = f_cum[last:last + 1, :]
        m_new = f_end + jnp.maximum(m_prev, cmb[last:last + 1, :])
        w_c.append(jnp.exp(f_end + m_prev - m_new))
        mp_rows.append(jnp.broadcast_to(m_prev, (chunk, LANES)))
        mn_rows.append(jnp.broadcast_to(m_new, (chunk, LANES)))
        fe_rows.append(jnp.broadcast_to(f_end, (chunk, LANES)))
        m_prev = m_new
    m_sc[...] = m_prev
    mp = jnp.concatenate(mp_rows, axis=0)
    mn = jnp.concatenate(mn_rows, axis=0)
    fe = jnp.concatenate(fe_rows, axis=0)
    big_m = jnp.maximum(mp, cmb)
    w_i = jnp.exp(mp - big_m)
    em = jnp.exp(-(f_cum + big_m))
    k_sc = jnp.exp(fe - f_cum + gts - mn)
    ex = _dot(_pack_terms([w_i, k_sc, -big_m, em], heads), e_ref[...])
    o1 = heads * EXPAND_WIDTHS[0]
    o2 = o1 + heads * EXPAND_WIDTHS[1]
    o3 = o2 + heads * EXPAND_WIDTHS[2]
    q = q_ref[...]
    k = k_ref[...] * (dk ** -0.5)
    qs = q * ex[:, :o1]
    ke = k * ex[:, o1:o2]
    neg_m = ex[:, o2:o3]
    em_v = ex[:, o3:]
    bt = b.T

    ri = lax.broadcasted_iota(I32, (chunk, chunk), 0)
    ci = lax.broadcasted_iota(I32, (chunk, chunk), 1)
    causal = ri >= ci
    lane1 = lax.broadcasted_iota(I32, (1, LANES), 1)
    rowi = lax.broadcasted_iota(I32, (LANES, LANES), 0)
    ones_v = jnp.ones((chunk, dv), BF16)
    states = [[None] * pairs for _ in range(nch)]
    for p in range(pairs):
        ps = slice(p * LANES, (p + 1) * LANES)
        cp = c_sc[p]
        nb = nb_sc[p]
        for j in range(nch):
            rows = slice(j * chunk, (j + 1) * chunk)
            states[j][p] = jnp.concatenate([cp, nb], axis=1).astype(BF16)
            vp = jnp.concatenate([v_ref[rows, 2 * p * dv:(2 * p + 2) * dv].astype(BF16), ones_v], axis=1)
            full = _dot_tn(ke[rows, ps].astype(BF16), vp)
            upd = jnp.where(rowi < dk, full[:, :dv], full[:, dv:2 * dv])
            w_col = jnp.where(rowi < dk, w_c[j][:, 2 * p:2 * p + 1], w_c[j][:, 2 * p + 1:2 * p + 2])
            cp = w_col * cp + upd
            nb = w_col * nb + full[:, 2 * dv:]
        c_sc[p] = cp
        nb_sc[p] = nb
    for j in range(nch):
        rows = slice(j * chunk, (j + 1) * chunk)
        for p in range(pairs):
            ps = slice(p * LANES, (p + 1) * LANES)
            qp = q[rows, ps]
            qsp = qs[rows, ps]
            kpb = k[rows, ps].astype(BF16)
            state = states[j][p]
            for jj in range(2):
                h = 2 * p + jj
                hs = slice(h * dv, (h + 1) * dv)
                mine = (lane1 >= jj * dk) & (lane1 < (jj + 1) * dk)
                qh = jnp.where(mine, qp, 0.0).astype(BF16)
                qsh = jnp.where(mine, qsp, 0.0).astype(BF16)
                d = neg_m[rows, h * dv:h * dv + chunk] + bt[h:h + 1, j * chunk:(j + 1) * chunk]
                s = _dot_nt(qh, kpb) * jnp.where(causal, jnp.exp(d), 0.0)
                vh = jnp.concatenate([v_ref[rows, hs].astype(BF16), ones_v], axis=1)
                out = _dot(qsh, state) + _dot(s.astype(BF16), vh)
                hh = out[:, :dv] / jnp.maximum(jnp.abs(out[:, dv:]), em_v[rows, hs])
                hn = _rms(hh, gn_ref[h:h + 1, :])
                u_ref[rows, hs] = (_sigmoid(op_ref[rows, hs]) * hn).astype(BF16)

    @pl.when(c == nblk - 1)
    def _():
        ct_ref[0] = c_sc[...]
        for p in range(pairs):
            nt_ref[0, p:p + 1, :] = nb_sc[p].T[0:1, :]
        mt_ref[0] = m_sc[...]


def _mlstm_recurrence_blocked(z, gates, c0, n0, m0, g_norm, nseq, t, chunk, tb):
    heads = g_norm.shape[0]
    v_w = heads * g_norm.shape[1]
    qk_w = v_w // 2
    nblk = t // tb
    expand = _expand_matrix(heads, EXPAND_WIDTHS)
    kern = functools.partial(_mlstm_block_kernel, chunk=chunk, tb=tb, nblk=nblk)
    row = lambda b, c: b * nblk + c
    st4 = lambda b, c: (b, 0, 0, 0)
    st3 = lambda b, c: (b, 0, 0)
    return pl.pallas_call(
        kern,
        grid=(nseq, nblk),
        in_specs=[
            pl.BlockSpec((tb, qk_w), lambda b, c: (row(b, c), 0)),
            pl.BlockSpec((tb, qk_w), lambda b, c: (row(b, c), 1)),
            pl.BlockSpec((tb, v_w), lambda b, c: (row(b, c), 1)),
            pl.BlockSpec((tb, v_w), lambda b, c: (row(b, c), 2)),
            pl.BlockSpec((tb, LANES), lambda b, c: (row(b, c), 0)),
            pl.BlockSpec(expand.shape, lambda b, c: (0, 0)),
            pl.BlockSpec((1,) + c0.shape[1:], st4),
            pl.BlockSpec((1,) + n0.shape[1:], st3),
            pl.BlockSpec((1,) + m0.shape[1:], st3),
            pl.BlockSpec(g_norm.shape, lambda b, c: (0, 0)),
        ],
        out_specs=[
            pl.BlockSpec((tb, v_w), lambda b, c: (row(b, c), 0)),
            pl.BlockSpec((1,) + c0.shape[1:], st4),
            pl.BlockSpec((1,) + n0.shape[1:], st3),
            pl.BlockSpec((1,) + m0.shape[1:], st3),
        ],
        out_shape=[
            jax.ShapeDtypeStruct((nseq * t, v_w), BF16),
            jax.ShapeDtypeStruct(c0.shape, F32),
            jax.ShapeDtypeStruct(n0.shape, F32),
            jax.ShapeDtypeStruct(m0.shape, F32),
        ],
        scratch_shapes=[
            pltpu.VMEM(c0.shape[1:], F32),
            pltpu.VMEM(c0.shape[1:], F32),
            pltpu.VMEM(m0.shape[1:], F32),
        ],
        compiler_params=_params(("arbitrary", "arbitrary")),
        name="mlstm_recurrence_t%d" % t,
    )(z, z, z, z, gates, expand, c0, n0, m0, g_norm)


def _out_proj_kernel(up_ref, us_ref, xp_ref, xs_ref, w_ref, g_ref, cast_ref,
                     xo_ref, hn_ref, cast_out_ref, *, n_first):
    cast_out_ref[...] = cast_ref[...].astype(BF16)
    i = pl.program_id(0)
    u = _pick_rows(i, n_first, up_ref, us_ref)
    x = _pick_rows(i, n_first, xp_ref, xs_ref) + _dot(u, w_ref[...])
    xo_ref[...] = x
    hn_ref[...] = _rms(x, g_ref[...]).astype(hn_ref.dtype)


def _out_proj(u_p, u_s, x_p, x_s, w, g, cast_src):
    d = x_p.shape[1]
    n = x_p.shape[0] + x_s.shape[0]
    n_first = x_p.shape[0] // TM
    cast_in, cast_out, cast_shape = _cast_side_job(cast_src, n // TM)
    return pl.pallas_call(
        functools.partial(_out_proj_kernel, n_first=n_first),
        grid=(n // TM,),
        in_specs=_split_specs(n_first, u_p.shape[1]) + _split_specs(n_first, d) + [
            pl.BlockSpec(w.shape, lambda i: (0, 0)),
            pl.BlockSpec((1, d), lambda i: (0, 0)),
            cast_in,
        ],
        out_specs=[pl.BlockSpec((TM, d), lambda i: (i, 0)), pl.BlockSpec((TM, d), lambda i: (i, 0)),
                   cast_out],
        out_shape=[jax.ShapeDtypeStruct((n, d), F32), jax.ShapeDtypeStruct((n, d), BF16), cast_shape],
        compiler_params=_params(("arbitrary",)),
        name="out_proj",
    )(u_p, u_s, x_p, x_s, w, g, cast_src)


def _out_proj_router_kernel(up_ref, us_ref, x_ref, w_ref, g_ref, wr_ref, br_ref,
                            xo_ref, hn_ref, gate_ref, idx_ref, cnt_ref, *, n_first):
    i = pl.program_id(0)
    u = _pick_rows(i, n_first, up_ref, us_ref)
    x = x_ref[...] + _dot(u, w_ref[...])
    xo_ref[...] = x
    hn = _rms(x, g_ref[...])
    hn_hi = hn.astype(BF16)
    hn_ref[...] = hn_hi
    hn_lo = (hn - hn_hi.astype(F32)).astype(BF16)
    logits = (_dot(hn_hi, wr_ref[0]) + _dot(hn_lo, wr_ref[0]) + _dot(hn_hi, wr_ref[1])
              + br_ref[...])
    tm = logits.shape[0]
    lane = lax.broadcasted_iota(I32, logits.shape, 1)
    valid = lane < N_EXPERTS
    logits = jnp.where(valid, logits, -jnp.inf)
    ex = jnp.exp(logits - jnp.max(logits, axis=-1, keepdims=True))
    probs = jnp.where(valid, ex / jnp.sum(ex, axis=-1, keepdims=True), -1.0)
    p1 = jnp.max(probs, axis=-1, keepdims=True)
    i1 = jnp.min(jnp.where(probs == p1, lane, LANES), axis=-1, keepdims=True)
    rest = jnp.where(lane == i1, -1.0, probs)
    p2 = jnp.max(rest, axis=-1, keepdims=True)
    i2 = jnp.min(jnp.where(rest == p2, lane, LANES), axis=-1, keepdims=True)
    tot = p1 + p2
    onehot = ((lane == i1) | (lane == i2)).astype(BF16)
    rr = lax.broadcasted_iota(I32, (tm, tm), 0)
    cc = lax.broadcasted_iota(I32, (tm, tm), 1)
    strict = (rr > cc).astype(BF16)
    before = _dot(strict, onehot)
    r1 = jnp.sum(jnp.where(lane == i1, before, 0.0), axis=-1, keepdims=True)
    r2 = jnp.sum(jnp.where(lane == i2, before, 0.0), axis=-1, keepdims=True)
    gate_ref[...] = jnp.where(lane == 0, p1 / tot, jnp.where(lane == 1, p2 / tot, 0.0))
    idx_ref[...] = jnp.where(lane == 0, i1,
                             jnp.where(lane == 1, i2,
                                       jnp.where(lane == 2, r1.astype(I32),
                                                 jnp.where(lane == 3, r2.astype(I32), 0))))
    cnt_ref[0] = jnp.sum(onehot.astype(F32), axis=0, keepdims=True)


def _out_proj_router(u_p, u_s, x, w, g, w_router, b_router):
    n, d = x.shape
    n_first = u_p.shape[0] // TM
    row = lambda i: (i, 0)
    fix = lambda i: (0, 0)
    return pl.pallas_call(
        functools.partial(_out_proj_router_kernel, n_first=n_first),
        grid=(n // TM,),
        in_specs=_split_specs(n_first, u_p.shape[1]) + [
            pl.BlockSpec((TM, d), row),
            pl.BlockSpec(w.shape, fix),
            pl.BlockSpec((1, d), fix),
            pl.BlockSpec(w_router.shape, lambda i: (0, 0, 0)),
            pl.BlockSpec((1, LANES), fix),
        ],
        out_specs=[
            pl.BlockSpec((TM, d), row),
            pl.BlockSpec((TM, d), row),
            pl.BlockSpec((TM, LANES), row),
            pl.BlockSpec((TM, LANES), row),
            pl.BlockSpec((1, 1, LANES), lambda i: (i, 0, 0)),
        ],
        out_shape=[
            jax.ShapeDtypeStruct((n, d), F32),
            jax.ShapeDtypeStruct((n, d), BF16),
            jax.ShapeDtypeStruct((n, LANES), F32),
            jax.ShapeDtypeStruct((n, LANES), I32),
            jax.ShapeDtypeStruct((n // TM, 1, LANES), F32),
        ],
        compiler_params=_params(("arbitrary",)),
        name="out_proj_router",
    )(u_p, u_s, x, w, g, w_router, b_router)


def _swiglu_kernel(hn_ref, x_ref, wg_ref, wu_ref, wd_ref, g_ref, cast_ref, xo_ref, ho_ref, cast_out_ref,
                   *, nchunk):
    cast_out_ref[...] = cast_ref[...].astype(BF16)
    hn = hn_ref[...]
    width = wd_ref.shape[0] // nchunk
    x = x_ref[...]
    for c in range(nchunk):
        cs = slice(c * width, (c + 1) * width)
        gate = _dot(hn, wg_ref[:, cs])
        up = _dot(hn, wu_ref[:, cs])
        act = (gate * _sigmoid(gate) * up).astype(BF16)
        x = x + _dot(act, wd_ref[cs, :])
    xo_ref[...] = x
    ho_ref[...] = _rms(x, g_ref[...]).astype(BF16)


def _swiglu(hn, x, w_gu, w_down, g, nchunk, cast_src):
    n, d = x.shape
    dff = w_down.shape[0]
    kern = functools.partial(_swiglu_kernel, nchunk=nchunk)
    row = lambda i: (i, 0)
    cast_in, cast_out, cast_shape = _cast_side_job(cast_src, n // TM)
    return pl.pallas_call(
        kern,
        grid=(n // TM,),
        in_specs=[
            pl.BlockSpec((TM, d), row),
            pl.BlockSpec((TM, d), row),
            _resident_spec((d, dff), lambda i: (0, 0)),
            _resident_spec((d, dff), lambda i: (0, 1)),
            _resident_spec((dff, d), lambda i: (0, 0)),
            pl.BlockSpec((1, d), lambda i: (0, 0)),
            cast_in,
        ],
        out_specs=[pl.BlockSpec((TM, d), row), pl.BlockSpec((TM, d), row), cast_out],
        out_shape=[jax.ShapeDtypeStruct((n, d), F32), jax.ShapeDtypeStruct((n, d), BF16), cast_shape],
        compiler_params=_params(("arbitrary",)),
        name="dense_swiglu",
    )(hn, x, w_gu, w_gu, w_down, g, cast_src)


def _row_copy(src_ref, src_row, dst_ref, dst_row, sem):
    return pltpu.make_async_copy(src_ref.at[pl.ds(src_row, 1), :], dst_ref.at[pl.ds(dst_row, 1), :], sem)


SUBLANES = 8


def _zero_rows(zero_ref, dst_ref, dst_row, count, limit, sem, wait):
    head = jnp.minimum((-dst_row) & (SUBLANES - 1), count)
    for j in range(SUBLANES - 1):
        @pl.when(j < head)
        def _(j=j):
            copy = _row_copy(zero_ref, 0, dst_ref, dst_row + j, sem)
            copy.wait() if wait else copy.start()

    rest = count - head
    first = dst_row + head
    for bit in range(SUBLANES.bit_length() - 1, limit.bit_length() - 1):
        size = 1 << bit
        done = (rest >> (bit + 1)) << (bit + 1)

        @pl.when(((rest >> bit) & 1) == 1)
        def _(size=size, done=done):
            start = pl.multiple_of(first + done, SUBLANES)
            copy = pltpu.make_async_copy(zero_ref.at[pl.ds(0, size), :],
                                         dst_ref.at[pl.ds(start, size), :], sem)
            copy.wait() if wait else copy.start()


def _expert_kernel(te_ref, tv_ref, xs_ref, wg_ref, wu_ref, wd_ref, ys_ref, *, nchunk):
    i = pl.program_id(0)
    half = xs_ref.shape[0] // 2
    width = wg_ref.shape[2] // nchunk

    def swiglu_rows(rows):
        x = xs_ref[rows, :].astype(BF16)
        acc = None
        for c in range(nchunk):
            cs = slice(c * width, (c + 1) * width)
            gate = _dot(x, wg_ref[0, :, cs])
            up = _dot(x, wu_ref[0, :, cs])
            act = (gate * _sigmoid(gate) * up).astype(BF16)
            part = _dot(act, wd_ref[0, cs, :])
            acc = part if acc is None else acc + part
        ys_ref[rows, :] = acc.astype(BF16).astype(F32)

    @pl.when(tv_ref[i] == 2)
    def _():
        swiglu_rows(slice(None))

    @pl.when(tv_ref[i] == 1)
    def _():
        swiglu_rows(slice(0, half))
        ys_ref[half:, :] = jnp.zeros((half, ys_ref.shape[1]), F32)

    @pl.when(tv_ref[i] == 0)
    def _():
        ys_ref[...] = jnp.zeros_like(ys_ref)


def _experts(tile_expert, tile_valid, xs, w_gu, w_down):
    r, d = xs.shape
    n_tiles = r // TM_EXPERT
    dff = w_down.shape[1]
    kern = functools.partial(_expert_kernel, nchunk=EXPERT_CHUNKS)
    return pl.pallas_call(
        kern,
        grid_spec=pltpu.PrefetchScalarGridSpec(
            num_scalar_prefetch=2,
            grid=(r // TM_EXPERT,),
            in_specs=[
                pl.BlockSpec((TM_EXPERT, d), lambda i, te, tv: (jnp.minimum(i, tv[n_tiles]), 0)),
                pl.BlockSpec((1, d, dff), lambda i, te, tv: (te[i], 0, 0)),
                pl.BlockSpec((1, d, dff), lambda i, te, tv: (te[i], 0, 1)),
                pl.BlockSpec((1, dff, d), lambda i, te, tv: (te[i], 0, 0)),
            ],
            out_specs=pl.BlockSpec((TM_EXPERT, d), lambda i, te, tv: (i, 0)),
        ),
        out_shape=jax.ShapeDtypeStruct((r, d), F32),
        compiler_params=_params(("arbitrary",)),
        name="moe_experts",
    )(tile_expert, tile_valid, xs, w_gu, w_gu, w_down)


RUN_BUF_ROWS = 2 * TM + LANES


def _run_positions(idx, off_row):
    lane = lax.broadcasted_iota(I32, idx.shape, 1)
    pos = []
    for slot in range(2):
        off = jnp.sum(jnp.where(lane == idx[:, slot:slot + 1], off_row, 0.0), axis=-1, keepdims=True)
        pos.append(off.astype(I32) + idx[:, 2 + slot:3 + slot])
    return pos


def _run_copies(tab_ref, step, hbm_ref, buf_ref, slot, sem, to_hbm, wait):
    for e in range(N_EXPERTS):
        entry = (step * N_EXPERTS + e) * 3
        off, row, length = tab_ref[entry], tab_ref[entry + 1], tab_ref[entry + 2]
        for bit in range(SUBLANES.bit_length() - 1, TM.bit_length()):
            size = 1 << bit
            done = (length >> (bit + 1)) << (bit + 1)

            @pl.when(((length >> bit) & 1) == 1)
            def _(size=size, done=done, off=off, row=row):
                in_buf = buf_ref.at[slot, pl.ds(pl.multiple_of(off + done, SUBLANES), size), :]
                in_hbm = hbm_ref.at[pl.ds(pl.multiple_of(row + done, SUBLANES), size), :]
                copy = (pltpu.make_async_copy(in_buf, in_hbm, sem.at[slot]) if to_hbm
                        else pltpu.make_async_copy(in_hbm, in_buf, sem.at[slot]))
                copy.wait() if wait else copy.start()


def _run_dispatch_kernel(tab_ref, pad_ref, hn_ref, idx_ref, off_ref, xs_ref, buf, zero_sc, sem, zsem,
                         *, n_steps):
    j = pl.program_id(0)
    slot = j & 1

    @pl.when(j == 0)
    def _():
        zero_sc[...] = jnp.zeros_like(zero_sc)
        rows = zero_sc.shape[0]
        for wait in (False, True):
            for e in range(N_EXPERTS):
                _zero_rows(zero_sc, xs_ref, pad_ref[2 * e], pad_ref[2 * e + 1], 2 * rows, zsem, wait)

            def tail(t, carry, wait=wait):
                first = pl.multiple_of(pad_ref[2 * N_EXPERTS] + t * rows, SUBLANES)
                copy = pltpu.make_async_copy(zero_sc, xs_ref.at[pl.ds(first, rows), :], zsem)
                copy.wait() if wait else copy.start()
                return carry

            lax.fori_loop(0, pad_ref[2 * N_EXPERTS + 1] // rows, tail, 0)

    @pl.when(j >= 2)
    def _():
        _run_copies(tab_ref, j - 2, xs_ref, buf, slot, sem, True, True)

    pos0, pos1 = _run_positions(idx_ref[...], off_ref[0])
    lane = lax.broadcasted_iota(I32, (hn_ref.shape[0], buf.shape[1]), 1)
    sel = jnp.where((lane == pos0) | (lane == pos1), 1.0, 0.0).astype(BF16)
    buf[slot] = _dot_tn(sel, hn_ref[...])
    _run_copies(tab_ref, j, xs_ref, buf, slot, sem, True, False)

    @pl.when(j == n_steps - 1)
    def _():
        _run_copies(tab_ref, j - 1, xs_ref, buf, 1 - slot, sem, True, True)
        _run_copies(tab_ref, j, xs_ref, buf, slot, sem, True, True)


def _run_dispatch(run_table, pad_table, hn, idx, run_offsets, n_rows):
    n, d = hn.shape
    n_steps = n // TM
    assert n_steps >= 2
    return pl.pallas_call(
        functools.partial(_run_dispatch_kernel, n_steps=n_steps),
        grid_spec=pltpu.PrefetchScalarGridSpec(
            num_scalar_prefetch=2,
            grid=(n_steps,),
            in_specs=[
                pl.BlockSpec((TM, d), lambda i, *_: (i, 0)),
                pl.BlockSpec((TM, LANES), lambda i, *_: (i, 0)),
                pl.BlockSpec((1, 1, LANES), lambda i, *_: (i, 0, 0)),
            ],
            out_specs=pl.BlockSpec(memory_space=pl.ANY),
            scratch_shapes=[pltpu.VMEM((2, RUN_BUF_ROWS, d), F32),
                            pltpu.VMEM((TM_EXPERT // 2, d), F32),
                            pltpu.SemaphoreType.DMA((2,)), pltpu.SemaphoreType.DMA(())],
        ),
        out_shape=jax.ShapeDtypeStruct((n_rows, d), F32),
        compiler_params=_params(("arbitrary",)),
        name="moe_dispatch",
    )(run_table, pad_table, hn, idx, run_offsets)


def _run_combine_kernel(tab_ref, ys_ref, x_ref, gate_ref, idx_ref, off_ref, g_ref, yp_ref, ysm_ref,
                        buf, sem, *, n_first, n_steps):
    j = pl.program_id(0)
    slot = j & 1

    @pl.when(j == 0)
    def _():
        buf[...] = jnp.zeros_like(buf)
        _run_copies(tab_ref, 0, ys_ref, buf, 0, sem, False, False)

    @pl.when(j + 1 < n_steps)
    def _():
        _run_copies(tab_ref, j + 1, ys_ref, buf, 1 - slot, sem, False, False)

    _run_copies(tab_ref, j, ys_ref, buf, slot, sem, False, True)
    runs = buf[slot].astype(BF16)
    pos = _run_positions(idx_ref[...], off_ref[0])
    lane = lax.broadcasted_iota(I32, (x_ref.shape[0], buf.shape[1]), 1)
    gate = gate_ref[...]
    y = x_ref[...]
    for k in range(2):
        sel = jnp.where(lane == pos[k], 1.0, 0.0).astype(BF16)
        y = y + gate[:, k:k + 1] * _dot(sel, runs)
    y = _rms(y, g_ref[...])

    @pl.when(j < n_first)
    def _():
        yp_ref[...] = y

    @pl.when(j >= n_first)
    def _():
        ysm_ref[...] = y


def _run_combine(run_table, ys, x, gates, idx, run_offsets, g, n_p):
    n, d = x.shape
    n_first = n_p // TM
    n_steps = n // TM
    p_spec, s_spec = _split_specs(n_first, d)
    row = lambda i, *_: (i, 0)
    return pl.pallas_call(
        functools.partial(_run_combine_kernel, n_first=n_first, n_steps=n_steps),
        grid_spec=pltpu.PrefetchScalarGridSpec(
            num_scalar_prefetch=1,
            grid=(n_steps,),
            in_specs=[
                pl.BlockSpec(memory_space=pl.ANY),
                pl.BlockSpec((TM, d), row),
                pl.BlockSpec((TM, LANES), row),
                pl.BlockSpec((TM, LANES), row),
                pl.BlockSpec((1, 1, LANES), lambda i, *_: (i, 0, 0)),
                pl.BlockSpec((1, d), lambda i, *_: (0, 0)),
            ],
            out_specs=[p_spec, s_spec],
            scratch_shapes=[pltpu.VMEM((2, RUN_BUF_ROWS, d), F32), pltpu.SemaphoreType.DMA((2,))],
        ),
        out_shape=[jax.ShapeDtypeStruct((n_p, d), F32), jax.ShapeDtypeStruct((n - n_p, d), F32)],
        compiler_params=_params(("arbitrary",)),
        name="moe_combine",
    )(run_table, ys, x, gates, idx, run_offsets, g)


def _run_tables(tile_counts, n_rows):
    cnt = tile_counts[:, 0, :N_EXPERTS].astype(I32)
    run = ((cnt + SUBLANES - 1) // SUBLANES) * SUBLANES
    off = jnp.cumsum(run, axis=1) - run
    used = jnp.sum(run, axis=0)
    padded = ((used + TM_EXPERT - 1) // TM_EXPERT) * TM_EXPERT
    ends = jnp.cumsum(padded)
    starts = ends - padded
    row = starts[None, :] + jnp.cumsum(run, axis=0) - run
    run_table = jnp.stack([off, row, run], axis=2).reshape(-1)
    run_offsets = jnp.pad(off.astype(F32), ((0, 0), (0, LANES - N_EXPERTS)))[:, None, :]
    n_tiles = n_rows // TM_EXPERT
    used_tiles = ends[-1] // TM_EXPERT
    pad_table = jnp.stack([starts + used, padded - used], axis=1).reshape(-1)
    pad_table = jnp.concatenate([pad_table, jnp.stack([ends[-1], n_rows - ends[-1]])])
    tile_start = jnp.arange(n_tiles, dtype=I32) * TM_EXPERT
    tile_expert = jnp.sum((tile_start[:, None] >= ends[None, :]).astype(I32), axis=1)
    last_expert = jnp.sum((ends[-1] - 1 >= ends).astype(I32))
    tile_expert = jnp.minimum(tile_expert, last_expert)
    used_end = (starts + used)[tile_expert]
    half = TM_EXPERT // 2
    tile_rows = jnp.clip(used_end - tile_start, 0, TM_EXPERT)
    tile_valid = jnp.where(tile_start < ends[-1], (tile_rows + half - 1) // half, 0)
    tile_valid = jnp.concatenate([tile_valid, (used_tiles - 1)[None]])
    return run_table, run_offsets, pad_table, tile_expert, tile_valid


def _pad_time(a, nseq, t, fill=None):
    w = a.shape[1]
    a = a.reshape(nseq, t, w)
    a = jnp.pad(a, ((0, 0), (0, SAMPLE_PAD_T - t), (0, 0)))
    if fill is not None:
        pad_row = jnp.arange(SAMPLE_PAD_T)[None, :, None] >= t
        lane = jnp.arange(w)[None, None, :]
        a = jnp.where(pad_row & (lane < MLSTM_HEADS), fill, a)
    return a.reshape(nseq * SAMPLE_PAD_T, w)


def _unpad_time(a, nseq, t):
    w = a.shape[1]
    return a.reshape(nseq, SAMPLE_PAD_T, w)[:, :t].reshape(nseq * t, w)


def kernel(x_prompt, x_sample, state_gla_S, state_mlstm_C, state_mlstm_n, state_mlstm_m,
           norm_mix, norm_ffn, norm_final,
           gla_w_in, gla_w_a2, gla_b_a, gla_g_norm, gla_w_out,
           mlstm_w_in, mlstm_b_gate, mlstm_g_norm, mlstm_w_out,
           ffn_w_gu, ffn_w_down,
           moe_w_router, moe_b_router, moe_w_gu, moe_w_down):
    bp, tp, d = x_prompt.shape
    bs, ts, _ = x_sample.shape
    n_p = bp * tp
    n_s = bs * ts
    n = n_p + n_s
    assert n_p % TM == 0 and n_s % TM == 0 and tp % (2 * CHUNK) == 0 and ts <= SAMPLE_PAD_T
    assert norm_mix.shape[0] == 2, "one GLA layer followed by one mLSTM layer"
    qk_w = d // 2
    main_w = 2 * qk_w + 2 * d

    x_p = x_prompt.reshape(n_p, d)
    x_s = x_sample.reshape(n_s, d)

    w_in = gla_w_in[0]
    w_lr = jnp.pad(w_in[:, main_w:], ((0, 0), (0, LANES - GLA_RANK))).astype(BF16)
    w_a2 = jnp.pad(gla_w_a2[0], ((0, LANES - GLA_RANK), (0, 0))).astype(BF16)
    z, log_a, ffn_wgu = _gla_in_proj(x_p, x_s, norm_mix[0][None], w_in, main_w, w_lr, w_a2,
                                     gla_b_a[0][None], ffn_w_gu[0])
    s0_p = jnp.zeros((bp,) + state_gla_S.shape[2:], F32)
    u_p, s_p = _gla_recurrence(z, log_a, s0_p, gla_g_norm[0], bp, tp, CHUNK, 16 * CHUNK)
    u_s, s_s = _gla_recurrence_short(_pad_time(z[n_p:], bs, ts), _pad_time(log_a[n_p:], bs, ts),
                                     state_gla_S[0], gla_g_norm[0], bs, SAMPLE_PAD_T,
                                     SAMPLE_SEQS_PER_STEP)
    x, hn, ffn_wd = _out_proj(u_p, _unpad_time(u_s, bs, ts), x_p, x_s, gla_w_out[0].astype(BF16),
                              norm_ffn[0][None], ffn_w_down[0])
    x, hn, moe_wgu = _swiglu(hn, x, ffn_wgu, ffn_wd,
                             norm_mix[1][None], SWIGLU_CHUNKS,
                             moe_w_gu[0].reshape(N_EXPERTS * d, moe_w_gu.shape[3]))

    w_in = mlstm_w_in[0]
    w_gt = jnp.pad(w_in[:, main_w:], ((0, 0), (0, LANES - 2 * MLSTM_HEADS))).astype(BF16)
    b_gt = jnp.pad(mlstm_b_gate[0], (0, LANES - 2 * MLSTM_HEADS))[None]
    dff_e = moe_w_down.shape[2]
    z, gates, moe_wd = _mlstm_in_proj(hn, w_in, main_w, w_gt, b_gt,
                                      moe_w_down[0].reshape(N_EXPERTS * dff_e, d))
    pairs = MLSTM_HEADS // 2
    dk2 = 2 * state_mlstm_C.shape[3]
    dvm = state_mlstm_C.shape[4]
    c0_p = jnp.zeros((bp, pairs, dk2, dvm), F32)
    n0_p = jnp.zeros((bp, pairs, dk2), F32)
    m0_p = jnp.zeros((bp, 1, LANES), F32)
    u_p, c_p, nn_p, m_p = _mlstm_recurrence_blocked(z, gates, c0_p, n0_p, m0_p,
                                                    mlstm_g_norm[0], bp, tp, CHUNK, 16 * CHUNK)
    c0_s = state_mlstm_C[0]
    n0_s = state_mlstm_n[0].reshape(bs, pairs, dk2)
    m0_s = jnp.pad(state_mlstm_m[0], ((0, 0), (0, LANES - MLSTM_HEADS)))[:, None, :]
    u_s, c_s, nn_s, m_s = _mlstm_recurrence_short(
        _pad_time(z[n_p:], bs, ts), _pad_time(gates[n_p:], bs, ts, fill=NEG_BIG),
        c0_s, n0_s, m0_s, mlstm_g_norm[0], bs, SAMPLE_PAD_T, SAMPLE_SEQS_PER_STEP)

    w_r = jnp.pad(moe_w_router[0], ((0, 0), (0, LANES - N_EXPERTS)))
    w_r_hi = w_r.astype(BF16)
    w_r = jnp.stack([w_r_hi, (w_r - w_r_hi.astype(F32)).astype(BF16)])
    b_r = jnp.pad(moe_b_router[0], (0, LANES - N_EXPERTS))[None]
    x, hn, route_g, route_i, tile_counts = _out_proj_router(
        u_p, _unpad_time(u_s, bs, ts), x, mlstm_w_out[0].astype(BF16), norm_ffn[1][None], w_r, b_r)
    n_rows = 2 * n + (n // TM) * N_EXPERTS * (SUBLANES - 1) + N_EXPERTS * (TM_EXPERT - 1)
    n_rows = -(-n_rows // TM_EXPERT) * TM_EXPERT
    run_table, run_offsets, pad_table, tile_expert, tile_valid = _run_tables(tile_counts, n_rows)
    xs = _run_dispatch(run_table, pad_table, hn, route_i, run_offsets, n_rows)
    ys = _experts(tile_expert, tile_valid, xs, moe_wgu.reshape(moe_w_gu.shape[1:]),
                  moe_wd.reshape(moe_w_down.shape[1:]))
    y_p, y_s = _run_combine(run_table, ys, x, route_g, route_i, run_offsets, norm_final[None], n_p)

    y_prompt = y_p.reshape(bp, tp, d)
    y_sample = y_s.reshape(bs, ts, d)
    hd = state_mlstm_C.shape[2:]
    return (y_prompt, y_sample,
            s_p[None], c_p.reshape((1, bp) + hd), nn_p.reshape(1, bp, hd[0], hd[1]),
            m_p[None, :, 0, :MLSTM_HEADS],
            s_s[None], c_s[None], nn_s.reshape(1, bs, hd[0], hd[1]), m_s[None, :, 0, :MLSTM_HEADS])


def _mlstm_in_proj(hn, w_in, wz, w_gate, b_gate, cast_src):
    n, d = hn.shape
    cast_in, cast_out, cast_shape = _cast_side_job(cast_src, n // TM)

    def kern(h_ref, w_ref, wg_ref, b_ref, cast_ref, z_ref, gate_ref, cast_out_ref, wb_sc):
        @pl.when(pl.program_id(0) == 0)
        def _():
            wb_sc[...] = w_ref[...].astype(BF16)

        cast_out_ref[...] = cast_ref[...].astype(BF16)
        h = h_ref[...]
        width = wz // IN_PROJ_COL_CHUNKS
        for c in range(IN_PROJ_COL_CHUNKS):
            cs = slice(c * width, (c + 1) * width)
            z_ref[:, cs] = _dot(h, wb_sc[:, cs])
        gp = _dot(h, wg_ref[...]) + b_ref[...]
        gc = GATE_CAP * jnp.tanh(gp * (1.0 / GATE_CAP))
        lane = lax.broadcasted_iota(I32, gc.shape, 1)
        out = jnp.where(lane < MLSTM_HEADS, gc, _log_sigmoid(gc))
        gate_ref[...] = jnp.where(lane < 2 * MLSTM_HEADS, out, 0.0)

    return pl.pallas_call(
        kern,
        grid=(n // TM,),
        in_specs=[
            pl.BlockSpec((TM, d), lambda i: (i, 0)),
            _resident_spec((d, wz), lambda i: (0, 0)),
            _resident_spec((d, LANES), lambda i: (0, 0)),
            pl.BlockSpec((1, LANES), lambda i: (0, 0)),
            cast_in,
        ],
        out_specs=[pl.BlockSpec((TM, wz), lambda i: (i, 0)), pl.BlockSpec((TM, LANES), lambda i: (i, 0)),
                   cast_out],
        out_shape=[jax.ShapeDtypeStruct((n, wz), F32), jax.ShapeDtypeStruct((n, LANES), F32),
                   cast_shape],
        scratch_shapes=[pltpu.VMEM((d, wz), BF16)],
        compiler_params=_params(("arbitrary",)),
        name="in_proj_mlstm",
    )(hn, w_in, w_gate, b_gate, cast_src)
```

```python
import functools

import jax
import jax.numpy as jnp
import numpy as np
from jax import lax
from jax.experimental import pallas as pl
from jax.experimental.pallas import tpu as pltpu

F32 = jnp.float32
BF16 = jnp.bfloat16
I32 = jnp.int32

EPS = 1e-6
GLA_HEADS = 4
GLA_RANK = 16
GLA_TAU = 16.0
MLSTM_HEADS = 8
GATE_CAP = 15.0
CHUNK = 64
N_EXPERTS = 8
NEG_BIG = -1e30

LANES = 128
TM = 512
TM_EXPERT = 512
SAMPLE_PAD_T = 8
SAMPLE_SEQS_PER_STEP = 16
SWIGLU_CHUNKS = 11
EXPERT_CHUNKS = 7
VMEM_LIMIT = 56 * 1024 * 1024


def _dot(a, b):
    return jnp.dot(a, b, preferred_element_type=F32)


def _dot_nt(a, b):
    return lax.dot_general(a, b, (((1,), (1,)), ((), ())), preferred_element_type=F32)


def _dot_tn(a, b):
    return lax.dot_general(a, b, (((0,), (0,)), ((), ())), preferred_element_type=F32)


def _sigmoid(x):
    return 1.0 / (1.0 + jnp.exp(-x))


def _log_sigmoid(x):
    return jnp.minimum(x, 0.0) - jnp.log(1.0 + jnp.exp(-jnp.abs(x)))


def _rms(x, g):
    return x * lax.rsqrt(jnp.mean(x * x, axis=-1, keepdims=True) + EPS) * g


def _cumsum_rows(x, n):
    ridx = lax.broadcasted_iota(I32, x.shape, 0)
    s = 1
    while s < n:
        x = x + jnp.where(ridx >= s, pltpu.roll(x, s, 0), 0.0)
        s *= 2
    return x


def _params(sem):
    return pltpu.CompilerParams(dimension_semantics=sem, vmem_limit_bytes=VMEM_LIMIT)


def _resident_spec(block_shape, index_map):
    return pl.BlockSpec(block_shape, index_map, pipeline_mode=pl.Buffered(1))


def _cast_side_job(src, n_steps):
    rows, cols = src.shape
    n_slabs = max(s for s in range(1, n_steps + 1) if rows % (16 * s) == 0)
    spec = pl.BlockSpec((rows // n_slabs, cols), lambda i, *_: (jnp.minimum(i, n_slabs - 1), 0))
    return spec, spec, jax.ShapeDtypeStruct(src.shape, BF16)


IN_PROJ_COL_CHUNKS = 4
ROW_PARTS = 2


def _pick_rows(i, n_first, first_ref, second_ref):
    return jnp.where(i < n_first, first_ref[...], second_ref[...])


def _split_specs(n_first, width):
    return [pl.BlockSpec((TM, width), lambda i, *_: (jnp.minimum(i, n_first - 1), 0)),
            pl.BlockSpec((TM, width), lambda i, *_: (jnp.maximum(i - n_first, 0), 0))]


def _gla_in_proj_kernel(xp_ref, xs_ref, g_ref, w_ref, wg_ref, w2_ref, b_ref, cast_ref,
                        z_ref, gate_ref, cast_out_ref, wb_sc, *, n_first):
    @pl.when(pl.program_id(0) == 0)
    def _():
        wb_sc[...] = w_ref[...].astype(BF16)

    cast_out_ref[...] = cast_ref[...].astype(BF16)
    first = pl.program_id(0) < n_first
    width = wb_sc.shape[1] // IN_PROJ_COL_CHUNKS
    part = xp_ref.shape[0] // ROW_PARTS
    for r in range(ROW_PARTS):
        rows = slice(r * part, (r + 1) * part)
        x = jnp.where(first, xp_ref[rows, :], xs_ref[rows, :])
        hn = _rms(x, g_ref[...]).astype(BF16)
        for c in range(IN_PROJ_COL_CHUNKS):
            cs = slice(c * width, (c + 1) * width)
            z_ref[rows, cs] = _dot(hn, wb_sc[:, cs])
        a = _dot(hn, wg_ref[...])
        la = _dot(a.astype(BF16), w2_ref[...]) + b_ref[...]
        gate_ref[rows, :] = _log_sigmoid(la) * (1.0 / GLA_TAU)


def _gla_in_proj(x_p, x_s, g, w_in, wz, w_gate, w2, b, cast_src):
    d = x_p.shape[1]
    n = x_p.shape[0] + x_s.shape[0]
    n_first = x_p.shape[0] // TM
    gw = b.shape[1]
    cast_in, cast_out, cast_shape = _cast_side_job(cast_src, n // TM)
    return pl.pallas_call(
        functools.partial(_gla_in_proj_kernel, n_first=n_first),
        grid=(n // TM,),
        in_specs=_split_specs(n_first, d) + [
            pl.BlockSpec((1, d), lambda i: (0, 0)),
            _resident_spec((d, wz), lambda i: (0, 0)),
            pl.BlockSpec((d, LANES), lambda i: (0, 0)),
            pl.BlockSpec(w2.shape, lambda i: (0, 0)),
            pl.BlockSpec((1, gw), lambda i: (0, 0)),
            cast_in,
        ],
        out_specs=[
            pl.BlockSpec((TM, wz), lambda i: (i, 0)),
            pl.BlockSpec((TM, gw), lambda i: (i, 0)),
            cast_out,
        ],
        out_shape=[jax.ShapeDtypeStruct((n, wz), F32), jax.ShapeDtypeStruct((n, gw), F32), cast_shape],
        scratch_shapes=[pltpu.VMEM((d, wz), BF16)],
        compiler_params=_params(("arbitrary",)),
        name="in_proj_gla",
    )(x_p, x_s, g, w_in, w_gate, w2, b, cast_src)


def _gla_kernel(q_ref, k_ref, v_ref, r_ref, la_ref, s0_ref, gn_ref, u_ref, st_ref, st_sc,
                *, chunk, tb, nblk):
    heads = GLA_HEADS
    dk = q_ref.shape[1] // heads
    dv = v_ref.shape[1] // heads
    c = pl.program_id(1)

    @pl.when(c == 0)
    def _():
        for h in range(heads):
            st_sc[h] = s0_ref[0, h].T

    ri = lax.broadcasted_iota(I32, (chunk, chunk), 0)
    ci = lax.broadcasted_iota(I32, (chunk, chunk), 1)
    causal = ri >= ci
    kscale = dk ** -0.5
    state = [st_sc[h] for h in range(heads)]
    for j in range(tb // chunk):
        rows = slice(j * chunk, (j + 1) * chunk)
        b_all = _cumsum_rows(la_ref[rows, :], chunk)
        for h in range(heads):
            ks = slice(h * dk, (h + 1) * dk)
            vs = slice(h * dv, (h + 1) * dv)
            b = b_all[:, ks]
            b_end = b[chunk - 1:chunk, :]
            q = q_ref[rows, ks]
            k = k_ref[rows, ks] * kscale
            qg = (q * jnp.exp(b)).astype(BF16)
            kg = (k * jnp.exp(-b)).astype(BF16)
            ke = (k * jnp.exp(b_end - b)).astype(BF16)
            v = v_ref[rows, vs].astype(BF16)
            a = jnp.where(causal, _dot_nt(qg, kg), 0.0).astype(BF16)
            o = _dot_nt(qg, state[h].astype(BF16)) + _dot(a, v)
            state[h] = state[h] * jnp.exp(b_end) + _dot_tn(v, ke)
            on = _rms(o, gn_ref[h:h + 1, :])
            r = r_ref[rows, vs]
            u_ref[rows, vs] = (r * _sigmoid(r) * on).astype(BF16)
    for h in range(heads):
        st_sc[h] = state[h]

    @pl.when(c == nblk - 1)
    def _():
        for h in range(heads):
            st_ref[0, h] = st_sc[h].T


def _gla_recurrence(z, log_a, s0, g_norm, nseq, t, chunk, tb):
    qk_w = log_a.shape[1]
    v_w = 2 * qk_w
    nblk = t // tb
    heads, dk, dv = s0.shape[1:]
    kern = functools.partial(_gla_kernel, chunk=chunk, tb=tb, nblk=nblk)
    row = lambda b, c: b * nblk + c
    return pl.pallas_call(
        kern,
        grid=(nseq, nblk),
        in_specs=[
            pl.BlockSpec((tb, qk_w), lambda b, c: (row(b, c), 0)),
            pl.BlockSpec((tb, qk_w), lambda b, c: (row(b, c), 1)),
            pl.BlockSpec((tb, v_w), lambda b, c: (row(b, c), 1)),
            pl.BlockSpec((tb, v_w), lambda b, c: (row(b, c), 2)),
            pl.BlockSpec((tb, qk_w), lambda b, c: (row(b, c), 0)),
            pl.BlockSpec((1, heads, dk, dv), lambda b, c: (b, 0, 0, 0)),
            pl.BlockSpec((heads, dv), lambda b, c: (0, 0)),
        ],
        out_specs=[
            pl.BlockSpec((tb, v_w), lambda b, c: (row(b, c), 0)),
            pl.BlockSpec((1, heads, dk, dv), lambda b, c: (b, 0, 0, 0)),
        ],
        out_shape=[
            jax.ShapeDtypeStruct((nseq * t, v_w), BF16),
            jax.ShapeDtypeStruct(s0.shape, F32),
        ],
        scratch_shapes=[pltpu.VMEM((heads, dv, dk), F32)],
        compiler_params=_params(("arbitrary", "arbitrary")),
        name="gla_recurrence_t%d" % t,
    )(z, z, z, z, log_a, s0, g_norm)


def _real_rows_selector(nsq, t, t_real):
    i = lax.broadcasted_iota(I32, (nsq * t_real, nsq * t), 0)
    r = lax.broadcasted_iota(I32, (nsq * t_real, nsq * t), 1)
    return jnp.where(r == (i // t_real) * t + i % t_real, 1.0, 0.0).astype(BF16)


def _gla_short_kernel(q_ref, k_ref, v_ref, r_ref, la_ref, s0_ref, gn_ref, u_ref, st_ref,
                      *, t, t_real, nsq):
    heads = GLA_HEADS
    dk = q_ref.shape[1] // heads
    dv = v_ref.shape[1] // heads
    nrow = nsq * t
    width = q_ref.shape[1]
    rc = lax.broadcasted_iota(I32, (nrow, width), 0) & (t - 1)
    log_a = jnp.where(rc < t_real, la_ref[...], 0.0)
    b = _chunk_scan(log_a, rc, t, jnp.add, 0.0)
    pick = _real_rows_selector(nsq, t, t_real)
    b_last = b.reshape(nsq, t, width)[:, t - 1:t, :]
    b_end = jnp.broadcast_to(b_last, (nsq, t, width)).reshape(nrow, width)
    k = k_ref[...] * (dk ** -0.5)
    qg = q_ref[...] * jnp.exp(b)
    kg = k * jnp.exp(-b)
    ke = k * jnp.exp(b_end - b)
    decay = jnp.exp(b_end)
    ri = lax.broadcasted_iota(I32, (nrow, nrow), 0)
    ci = lax.broadcasted_iota(I32, (nrow, nrow), 1)
    shift = t.bit_length() - 1
    mask = ((ri >> shift) == (ci >> shift)) & (ri >= ci)
    for h in range(heads):
        ks = slice(h * dk, (h + 1) * dk)
        vs = slice(h * dv, (h + 1) * dv)
        a = jnp.where(mask, _dot_nt(qg[:, ks].astype(BF16), kg[:, ks].astype(BF16)), 0.0)
        o = _dot(a.astype(BF16), v_ref[:, vs].astype(BF16))
        decay_t = decay[:, ks].T
        o_state = []
        for sq in range(nsq):
            r = slice(sq * t, (sq + 1) * t)
            s0 = s0_ref[sq, h]
            o_state.append(_dot(qg[r, ks].astype(BF16), s0.astype(BF16)))
            upd = _dot_tn(ke[r, ks].astype(BF16), v_ref[r, vs].astype(BF16))
            st_ref[sq, h] = decay_t[:, sq * t:sq * t + 1] * s0 + upd
        on = _rms(o + jnp.concatenate(o_state, axis=0), gn_ref[h:h + 1, :])
        rg = r_ref[:, vs]
        gated = (rg * _sigmoid(rg) * on).astype(BF16)
        u_ref[:, vs] = _dot(pick, gated).astype(BF16)


def _gla_recurrence_short(z, log_a, s0, g_norm, nseq, t, t_real, nsq, row0):
    qk_w = log_a.shape[1]
    v_w = 2 * qk_w
    heads, dk, dv = s0.shape[1:]
    assert nsq * t == dk == LANES and row0 % (nsq * t) == 0
    rb = nsq * t
    blk0 = row0 // rb
    return pl.pallas_call(
        functools.partial(_gla_short_kernel, t=t, t_real=t_real, nsq=nsq),
        grid=(nseq // nsq,),
        in_specs=[
            pl.BlockSpec((rb, qk_w), lambda b: (blk0 + b, 0)),
            pl.BlockSpec((rb, qk_w), lambda b: (blk0 + b, 1)),
            pl.BlockSpec((rb, v_w), lambda b: (blk0 + b, 1)),
            pl.BlockSpec((rb, v_w), lambda b: (blk0 + b, 2)),
            pl.BlockSpec((rb, qk_w), lambda b: (blk0 + b, 0)),
            pl.BlockSpec((nsq, heads, dk, dv), lambda b: (b, 0, 0, 0)),
            pl.BlockSpec((heads, dv), lambda b: (0, 0)),
        ],
        out_specs=[
            pl.BlockSpec((nsq * t_real, v_w), lambda b: (b, 0)),
            pl.BlockSpec((nsq, heads, dk, dv), lambda b: (b, 0, 0, 0)),
        ],
        out_shape=[
            jax.ShapeDtypeStruct((nseq * t_real, v_w), BF16),
            jax.ShapeDtypeStruct(s0.shape, F32),
        ],
        compiler_params=_params(("arbitrary",)),
        name="gla_recurrence_t%d" % t,
    )(z, z, z, z, log_a, s0, g_norm)


def _mlstm_short_kernel(q_ref, k_ref, v_ref, op_ref, gt_ref, e_ref, c0_ref, n0_ref, m0_ref, gn_ref,
                        u_ref, ct_ref, nt_ref, mt_ref, *, t, t_real, nsq):
    heads = MLSTM_HEADS
    pairs = heads // 2
    dv = v_ref.shape[1] // heads
    dk = q_ref.shape[1] // heads
    nrow = nsq * t

    def per_seq(x3):
        return jnp.broadcast_to(x3, (nsq, t, x3.shape[2])).reshape(nrow, x3.shape[2])

    def last(x):
        return x.reshape(nsq, t, x.shape[1])[:, t - 1:t, :]

    rc = lax.broadcasted_iota(I32, (nrow, LANES), 0) & (t - 1)
    lane_g = lax.broadcasted_iota(I32, (nrow, LANES), 1)
    gts = jnp.where(rc < t_real, gt_ref[...],
                    jnp.where(lane_g < heads, NEG_BIG, 0.0))
    pick = _real_rows_selector(nsq, t, t_real)
    f_cum = pltpu.roll(_chunk_scan(gts, rc, t, jnp.add, 0.0), LANES - heads, 1)
    b = gts - f_cum
    cmb = _chunk_scan(b, rc, t, jnp.maximum, NEG_BIG)
    mp = per_seq(m0_ref[...])
    big_m = jnp.maximum(mp, cmb)
    m_t = f_cum + big_m
    m_new = last(m_t)
    mt_ref[...] = m_new
    mn = per_seq(m_new)
    fe = per_seq(last(f_cum))
    w_i = jnp.exp(mp - big_m)
    em = jnp.exp(-m_t)
    k_sc = jnp.exp(fe - f_cum + gts - mn)
    w_c = jnp.exp(fe + mp - mn)
    ex = _dot(_pack_terms([w_i, k_sc, -big_m, em, w_c], heads), e_ref[...])
    offs = np.cumsum([0] + [heads * w for w in SHORT_EXPAND_WIDTHS])
    q = q_ref[...]
    k = k_ref[...] * (dk ** -0.5)
    qs = q * ex[:, offs[0]:offs[1]]
    ke = k * ex[:, offs[1]:offs[2]]
    neg_m = ex[:, offs[2]:offs[3]]
    em_v = ex[:, offs[3]:offs[4]]
    wc_k = ex[:, offs[4]:offs[5]]
    bt = b.T

    ri = lax.broadcasted_iota(I32, (nrow, nrow), 0)
    ci = lax.broadcasted_iota(I32, (nrow, nrow), 1)
    shift = t.bit_length() - 1
    mask = ((ri >> shift) == (ci >> shift)) & (ri >= ci)
    lane1 = lax.broadcasted_iota(I32, (1, LANES), 1)
    rowi = lax.broadcasted_iota(I32, (LANES, LANES), 0)
    ones_v = jnp.ones((nrow, dv), BF16)
    ones_k = jnp.ones((LANES, dv), BF16)
    for p in range(pairs):
        ps = slice(p * LANES, (p + 1) * LANES)
        qp = q[:, ps]
        qsp = qs[:, ps]
        kpb = k[:, ps].astype(BF16)
        n_rows = per_seq(n0_ref[:, p:p + 1, :])
        intra, den_state, qsh = [], [], []
        for jj in range(2):
            h = 2 * p + jj
            hs = slice(h * dv, (h + 1) * dv)
            mine = (lane1 >= jj * dk) & (lane1 < (jj + 1) * dk)
            qh = jnp.where(mine, qp, 0.0).astype(BF16)
            qsh.append(jnp.where(mine, qsp, 0.0))
            d = neg_m[:, hs] + bt[h:h + 1, :]
            s = _dot_nt(qh, kpb) * jnp.where(mask, jnp.exp(d), 0.0)
            vh = jnp.concatenate([v_ref[:, hs].astype(BF16), ones_v], axis=1)
            intra.append(_dot(s.astype(BF16), vh))
            den_state.append(_dot((qsh[jj] * n_rows).astype(BF16), ones_k))
        num_state = [[], []]
        for sq in range(nsq):
            r = slice(sq * t, (sq + 1) * t)
            lhs = jnp.concatenate([qsh[0][r], qsh[1][r]], axis=0).astype(BF16)
            c_pair = jnp.concatenate([c0_ref[sq, 2 * p], c0_ref[sq, 2 * p + 1]], axis=0)
            res = _dot(lhs, c_pair.astype(BF16))
            num_state[0].append(res[:t])
            num_state[1].append(res[t:])
        for jj in range(2):
            h = 2 * p + jj
            hs = slice(h * dv, (h + 1) * dv)
            num = intra[jj][:, :dv] + jnp.concatenate(num_state[jj], axis=0)
            den = intra[jj][:, dv:] + den_state[jj]
            hh = num / jnp.maximum(jnp.abs(den), em_v[:, hs])
            hn = _rms(hh, gn_ref[h:h + 1, :])
            gated = (_sigmoid(op_ref[:, hs]) * hn).astype(BF16)
            u_ref[:, hs] = _dot(pick, gated).astype(BF16)
        kep = ke[:, ps]
        for sq in range(nsq):
            r = slice(sq * t, (sq + 1) * t)
            vp = v_ref[r, 2 * p * dv:(2 * p + 2) * dv].astype(BF16)
            full = _dot_tn(kep[r].astype(BF16), vp)
            upd = jnp.where(rowi < dk, full[:, :dv], full[:, dv:])
            w_row = w_c[sq * t:sq * t + 1, :]
            w_col = jnp.where(rowi < dk, w_row[:, 2 * p:2 * p + 1], w_row[:, 2 * p + 1:2 * p + 2])
            c_pair = jnp.concatenate([c0_ref[sq, 2 * p], c0_ref[sq, 2 * p + 1]], axis=0)
            c_new = w_col * c_pair + upd
            ct_ref[sq, 2 * p] = c_new[:dk]
            ct_ref[sq, 2 * p + 1] = c_new[dk:]
        k_sum = jnp.sum(kep.reshape(nsq, t, LANES), axis=1, keepdims=True)
        nt_ref[:, p:p + 1, :] = last(wc_k[:, ps]) * n0_ref[:, p:p + 1, :] + k_sum


def _mlstm_recurrence_short(z, gates, c0, n0, m0, g_norm, nseq, t, t_real, nsq, row0):
    heads = g_norm.shape[0]
    v_w = heads * g_norm.shape[1]
    qk_w = v_w // 2
    assert nsq * t == LANES and row0 % (nsq * t) == 0
    expand = _expand_matrix(heads, SHORT_EXPAND_WIDTHS)
    kern = functools.partial(_mlstm_short_kernel, t=t, t_real=t_real, nsq=nsq)
    rb = nsq * t
    blk0 = row0 // rb
    st4 = lambda b: (b, 0, 0, 0)
    st3 = lambda b: (b, 0, 0)
    return pl.pallas_call(
        kern,
        grid=(nseq // nsq,),
        in_specs=[
            pl.BlockSpec((rb, qk_w), lambda b: (blk0 + b, 0)),
            pl.BlockSpec((rb, qk_w), lambda b: (blk0 + b, 1)),
            pl.BlockSpec((rb, v_w), lambda b: (blk0 + b, 1)),
            pl.BlockSpec((rb, v_w), lambda b: (blk0 + b, 2)),
            pl.BlockSpec((rb, LANES), lambda b: (blk0 + b, 0)),
            pl.BlockSpec(expand.shape, lambda b: (0, 0)),
            pl.BlockSpec((nsq,) + c0.shape[1:], st4),
            pl.BlockSpec((nsq,) + n0.shape[1:], st3),
            pl.BlockSpec((nsq,) + m0.shape[1:], st3),
            pl.BlockSpec(g_norm.shape, lambda b: (0, 0)),
        ],
        out_specs=[
            pl.BlockSpec((nsq * t_real, v_w), lambda b: (b, 0)),
            pl.BlockSpec((nsq,) + c0.shape[1:], st4),
            pl.BlockSpec((nsq,) + n0.shape[1:], st3),
            pl.BlockSpec((nsq,) + m0.shape[1:], st3),
        ],
        out_shape=[
            jax.ShapeDtypeStruct((nseq * t_real, v_w), BF16),
            jax.ShapeDtypeStruct(c0.shape, F32),
            jax.ShapeDtypeStruct(n0.shape, F32),
            jax.ShapeDtypeStruct(m0.shape, F32),
        ],
        compiler_params=_params(("arbitrary",)),
        name="mlstm_recurrence_t%d" % t,
    )(z, z, z, z, gates, expand, c0, n0, m0, g_norm)


N_SPLIT = 3
EXPAND_WIDTHS = (64, 64, 128, 128)
SHORT_EXPAND_WIDTHS = EXPAND_WIDTHS + (64,)


def _expand_matrix(heads, widths):
    assert len(widths) * N_SPLIT * heads <= LANES
    cols = []
    for qi, width in enumerate(widths):
        sel = np.zeros((LANES, heads * width), np.float32)
        for term in range(N_SPLIT):
            for h in range(heads):
                sel[(qi * N_SPLIT + term) * heads + h, h * width:(h + 1) * width] = 1.0
        cols.append(sel)
    return jnp.asarray(np.concatenate(cols, axis=1), BF16)


def _chunk_scan(x, rc, n, op, fill):
    s = 1
    while s < n:
        x = op(x, jnp.where(rc >= s, pltpu.roll(x, s, 0), fill))
        s *= 2
    return x


def _pack_terms(values, heads):
    lane = lax.broadcasted_iota(I32, values[0].shape, 1)
    packed = jnp.zeros(values[0].shape, F32)
    slot = 0
    for val in values:
        rest = val
        for term in range(N_SPLIT):
            part = rest.astype(BF16).astype(F32)
            rest = rest - part
            moved = part if slot == 0 else pltpu.roll(part, slot * heads, 1)
            packed = jnp.where((lane >= slot * heads) & (lane < (slot + 1) * heads), moved, packed)
            slot += 1
    return packed.astype(BF16)


def _mlstm_block_kernel(q_ref, k_ref, v_ref, op_ref, gt_ref, e_ref, c0_ref, n0_ref, m0_ref, gn_ref,
                        u_ref, ct_ref, nt_ref, mt_ref, c_sc, nb_sc, m_sc, *, chunk, tb, nblk):
    heads = MLSTM_HEADS
    pairs = heads // 2
    dv = v_ref.shape[1] // heads
    dk = q_ref.shape[1] // heads
    c = pl.program_id(1)

    @pl.when(c == 0)
    def _():
        c_sc[...] = c0_ref[0]
        for p in range(pairs):
            nb_sc[p] = jnp.broadcast_to(n0_ref[0, p:p + 1, :], (LANES, LANES)).T
        m_sc[...] = m0_ref[0]

    nch = tb // chunk
    rc = lax.broadcasted_iota(I32, (tb, LANES), 0) & (chunk - 1)
    gts = gt_ref[...]
    f_cum = pltpu.roll(_chunk_scan(gts, rc, chunk, jnp.add, 0.0), LANES - heads, 1)
    b = gts - f_cum
    cmb = _chunk_scan(b, rc, chunk, jnp.maximum, NEG_BIG)
    m_prev = m_sc[...]
    mp_rows, mn_rows, fe_rows, w_c = [], [], [], []
    for j in range(nch):
        last = (j + 1) * chunk - 1
        f_end = f_cum[last:last + 1, :]
        m_new = f_end + jnp.maximum(m_prev, cmb[last:last + 1, :])
        w_c.append(jnp.exp(f_end + m_prev - m_new))
        mp_rows.append(jnp.broadcast_to(m_prev, (chunk, LANES)))
        mn_rows.append(jnp.broadcast_to(m_new, (chunk, LANES)))
        fe_rows.append(jnp.broadcast_to(f_end, (chunk, LANES)))
        m_prev = m_new
    m_sc[...] = m_prev
    mp = jnp.concatenate(mp_rows, axis=0)
    mn = jnp.concatenate(mn_rows, axis=0)
    fe = jnp.concatenate(fe_rows, axis=0)
    big_m = jnp.maximum(mp, cmb)
    w_i = jnp.exp(mp - big_m)
    em = jnp.exp(-(f_cum + big_m))
    k_sc = jnp.exp(fe - f_cum + gts - mn)
    ex = _dot(_pack_terms([w_i, k_sc, -big_m, em], heads), e_ref[...])
    o1 = heads * EXPAND_WIDTHS[0]
    o2 = o1 + heads * EXPAND_WIDTHS[1]
    o3 = o2 + heads * EXPAND_WIDTHS[2]
    q = q_ref[...]
    k = k_ref[...] * (dk ** -0.5)
    qs = q * ex[:, :o1]
    ke = k * ex[:, o1:o2]
    neg_m = ex[:, o2:o3]
    em_v = ex[:, o3:]
    bt = b.T

    ri = lax.broadcasted_iota(I32, (chunk, chunk), 0)
    ci = lax.broadcasted_iota(I32, (chunk, chunk), 1)
    causal = ri >= ci
    lane1 = lax.broadcasted_iota(I32, (1, LANES), 1)
    rowi = lax.broadcasted_iota(I32, (LANES, LANES), 0)
    ones_v = jnp.ones((chunk, dv), BF16)
    states = [[None] * pairs for _ in range(nch)]
    for p in range(pairs):
        ps = slice(p * LANES, (p + 1) * LANES)
        cp = c_sc[p]
        nb = nb_sc[p]
        for j in range(nch):
            rows = slice(j * chunk, (j + 1) * chunk)
            states[j][p] = jnp.concatenate([cp, nb], axis=1).astype(BF16)
            vp = jnp.concatenate([v_ref[rows, 2 * p * dv:(2 * p + 2) * dv].astype(BF16), ones_v], axis=1)
            full = _dot_tn(ke[rows, ps].astype(BF16), vp)
            upd = jnp.where(rowi < dk, full[:, :dv], full[:, dv:2 * dv])
            w_col = jnp.where(rowi < dk, w_c[j][:, 2 * p:2 * p + 1], w_c[j][:, 2 * p + 1:2 * p + 2])
            cp = w_col * cp + upd
            nb = w_col * nb + full[:, 2 * dv:]
        c_sc[p] = cp
        nb_sc[p] = nb
    for j in range(nch):
        rows = slice(j * chunk, (j + 1) * chunk)
        for p in range(pairs):
            ps = slice(p * LANES, (p + 1) * LANES)
            qp = q[rows, ps]
            qsp = qs[rows, ps]
            kpb = k[rows, ps].astype(BF16)
            state = states[j][p]
            for jj in range(2):
                h = 2 * p + jj
                hs = slice(h * dv, (h + 1) * dv)
                mine = (lane1 >= jj * dk) & (lane1 < (jj + 1) * dk)
                qh = jnp.where(mine, qp, 0.0).astype(BF16)
                qsh = jnp.where(mine, qsp, 0.0).astype(BF16)
                d = neg_m[rows, h * dv:h * dv + chunk] + bt[h:h + 1, j * chunk:(j + 1) * chunk]
                s = _dot_nt(qh, kpb) * jnp.where(causal, jnp.exp(d), 0.0)
                vh = jnp.concatenate([v_ref[rows, hs].astype(BF16), ones_v], axis=1)
                out = _dot(qsh, state) + _dot(s.astype(BF16), vh)
                hh = out[:, :dv] / jnp.maximum(jnp.abs(out[:, dv:]), em_v[rows, hs])
                hn = _rms(hh, gn_ref[h:h + 1, :])
                u_ref[rows, hs] = (_sigmoid(op_ref[rows, hs]) * hn).astype(BF16)

    @pl.when(c == nblk - 1)
    def _():
        ct_ref[0] = c_sc[...]
        for p in range(pairs):
            nt_ref[0, p:p + 1, :] = nb_sc[p].T[0:1, :]
        mt_ref[0] = m_sc[...]


def _mlstm_recurrence_blocked(z, gates, c0, n0, m0, g_norm, nseq, t, chunk, tb):
    heads = g_norm.shape[0]
    v_w = heads * g_norm.shape[1]
    qk_w = v_w // 2
    nblk = t // tb
    expand = _expand_matrix(heads, EXPAND_WIDTHS)
    kern = functools.partial(_mlstm_block_kernel, chunk=chunk, tb=tb, nblk=nblk)
    row = lambda b, c: b * nblk + c
    st4 = lambda b, c: (b, 0, 0, 0)
    st3 = lambda b, c: (b, 0, 0)
    return pl.pallas_call(
        kern,
        grid=(nseq, nblk),
        in_specs=[
            pl.BlockSpec((tb, qk_w), lambda b, c: (row(b, c), 0)),
            pl.BlockSpec((tb, qk_w), lambda b, c: (row(b, c), 1)),
            pl.BlockSpec((tb, v_w), lambda b, c: (row(b, c), 1)),
            pl.BlockSpec((tb, v_w), lambda b, c: (row(b, c), 2)),
            pl.BlockSpec((tb, LANES), lambda b, c: (row(b, c), 0)),
            pl.BlockSpec(expand.shape, lambda b, c: (0, 0)),
            pl.BlockSpec((1,) + c0.shape[1:], st4),
            pl.BlockSpec((1,) + n0.shape[1:], st3),
            pl.BlockSpec((1,) + m0.shape[1:], st3),
            pl.BlockSpec(g_norm.shape, lambda b, c: (0, 0)),
        ],
        out_specs=[
            pl.BlockSpec((tb, v_w), lambda b, c: (row(b, c), 0)),
            pl.BlockSpec((1,) + c0.shape[1:], st4),
            pl.BlockSpec((1,) + n0.shape[1:], st3),
            pl.BlockSpec((1,) + m0.shape[1:], st3),
        ],
        out_shape=[
            jax.ShapeDtypeStruct((nseq * t, v_w), BF16),
            jax.ShapeDtypeStruct(c0.shape, F32),
            jax.ShapeDtypeStruct(n0.shape, F32),
            jax.ShapeDtypeStruct(m0.shape, F32),
        ],
        scratch_shapes=[
            pltpu.VMEM(c0.shape[1:], F32),
            pltpu.VMEM(c0.shape[1:], F32),
            pltpu.VMEM(m0.shape[1:], F32),
        ],
        compiler_params=_params(("arbitrary", "arbitrary")),
        name="mlstm_recurrence_t%d" % t,
    )(z, z, z, z, gates, expand, c0, n0, m0, g_norm)


def _out_proj_kernel(up_ref, us_ref, xp_ref, xs_ref, w_ref, g_ref, cast_ref,
                     xo_ref, hn_ref, cast_out_ref, *, n_first):
    cast_out_ref[...] = cast_ref[...].astype(BF16)
    i = pl.program_id(0)
    u = _pick_rows(i, n_first, up_ref, us_ref)
    x = _pick_rows(i, n_first, xp_ref, xs_ref) + _dot(u, w_ref[...])
    xo_ref[...] = x
    hn_ref[...] = _rms(x, g_ref[...]).astype(hn_ref.dtype)


def _out_proj(u_p, u_s, x_p, x_s, w, g, cast_src):
    d = x_p.shape[1]
    n = x_p.shape[0] + x_s.shape[0]
    n_first = x_p.shape[0] // TM
    cast_in, cast_out, cast_shape = _cast_side_job(cast_src, n // TM)
    return pl.pallas_call(
        functools.partial(_out_proj_kernel, n_first=n_first),
        grid=(n // TM,),
        in_specs=_split_specs(n_first, u_p.shape[1]) + _split_specs(n_first, d) + [
            pl.BlockSpec(w.shape, lambda i: (0, 0)),
            pl.BlockSpec((1, d), lambda i: (0, 0)),
            cast_in,
        ],
        out_specs=[pl.BlockSpec((TM, d), lambda i: (i, 0)), pl.BlockSpec((TM, d), lambda i: (i, 0)),
                   cast_out],
        out_shape=[jax.ShapeDtypeStruct((n, d), F32), jax.ShapeDtypeStruct((n, d), BF16), cast_shape],
        compiler_params=_params(("arbitrary",)),
        name="out_proj",
    )(u_p, u_s, x_p, x_s, w, g, cast_src)


def _out_proj_router_kernel(up_ref, us_ref, x_ref, w_ref, g_ref, wr_ref, br_ref,
                            xo_ref, hn_ref, gate_ref, idx_ref, cnt_ref, *, n_first):
    i = pl.program_id(0)
    u = _pick_rows(i, n_first, up_ref, us_ref)
    x = x_ref[...] + _dot(u, w_ref[...])
    xo_ref[...] = x
    hn = _rms(x, g_ref[...])
    hn_hi = hn.astype(BF16)
    hn_ref[...] = hn_hi
    hn_lo = (hn - hn_hi.astype(F32)).astype(BF16)
    logits = (_dot(hn_hi, wr_ref[0]) + _dot(hn_lo, wr_ref[0]) + _dot(hn_hi, wr_ref[1])
              + br_ref[...])
    tm = logits.shape[0]
    lane = lax.broadcasted_iota(I32, logits.shape, 1)
    valid = lane < N_EXPERTS
    logits = jnp.where(valid, logits, -jnp.inf)
    ex = jnp.exp(logits - jnp.max(logits, axis=-1, keepdims=True))
    probs = jnp.where(valid, ex / jnp.sum(ex, axis=-1, keepdims=True), -1.0)
    p1 = jnp.max(probs, axis=-1, keepdims=True)
    i1 = jnp.min(jnp.where(probs == p1, lane, LANES), axis=-1, keepdims=True)
    rest = jnp.where(lane == i1, -1.0, probs)
    p2 = jnp.max(rest, axis=-1, keepdims=True)
    i2 = jnp.min(jnp.where(rest == p2, lane, LANES), axis=-1, keepdims=True)
    tot = p1 + p2
    onehot = ((lane == i1) | (lane == i2)).astype(BF16)
    rr = lax.broadcasted_iota(I32, (tm, tm), 0)
    cc = lax.broadcasted_iota(I32, (tm, tm), 1)
    strict = (rr > cc).astype(BF16)
    before = _dot(strict, onehot)
    r1 = jnp.sum(jnp.where(lane == i1, before, 0.0), axis=-1, keepdims=True)
    r2 = jnp.sum(jnp.where(lane == i2, before, 0.0), axis=-1, keepdims=True)
    gate_ref[...] = jnp.where(lane == 0, p1 / tot, jnp.where(lane == 1, p2 / tot, 0.0))
    idx_ref[...] = jnp.where(lane == 0, i1,
                             jnp.where(lane == 1, i2,
                                       jnp.where(lane == 2, r1.astype(I32),
                                                 jnp.where(lane == 3, r2.astype(I32), 0))))
    cnt_ref[0] = jnp.sum(onehot.astype(F32), axis=0, keepdims=True)


def _out_proj_router(u_p, u_s, x, w, g, w_router, b_router):
    n, d = x.shape
    n_first = u_p.shape[0] // TM
    row = lambda i: (i, 0)
    fix = lambda i: (0, 0)
    return pl.pallas_call(
        functools.partial(_out_proj_router_kernel, n_first=n_first),
        grid=(n // TM,),
        in_specs=_split_specs(n_first, u_p.shape[1]) + [
            pl.BlockSpec((TM, d), row),
            pl.BlockSpec(w.shape, fix),
            pl.BlockSpec((1, d), fix),
            pl.BlockSpec(w_router.shape, lambda i: (0, 0, 0)),
            pl.BlockSpec((1, LANES), fix),
        ],
        out_specs=[
            pl.BlockSpec((TM, d), row),
            pl.BlockSpec((TM, d), row),
            pl.BlockSpec((TM, LANES), row),
            pl.BlockSpec((TM, LANES), row),
            pl.BlockSpec((1, 1, LANES), lambda i: (i, 0, 0)),
        ],
        out_shape=[
            jax.ShapeDtypeStruct((n, d), F32),
            jax.ShapeDtypeStruct((n, d), BF16),
            jax.ShapeDtypeStruct((n, LANES), F32),
            jax.ShapeDtypeStruct((n, LANES), I32),
            jax.ShapeDtypeStruct((n // TM, 1, LANES), F32),
        ],
        compiler_params=_params(("arbitrary",)),
        name="out_proj_router",
    )(u_p, u_s, x, w, g, w_router, b_router)


def _swiglu_kernel(hn_ref, x_ref, wg_ref, wu_ref, wd_ref, g_ref, cast_ref, xo_ref, ho_ref, cast_out_ref,
                   *, nchunk):
    cast_out_ref[...] = cast_ref[...].astype(BF16)
    hn = hn_ref[...]
    width = wd_ref.shape[0] // nchunk
    x = x_ref[...]
    for c in range(nchunk):
        cs = slice(c * width, (c + 1) * width)
        gate = _dot(hn, wg_ref[:, cs])
        up = _dot(hn, wu_ref[:, cs])
        act = (gate * _sigmoid(gate) * up).astype(BF16)
        x = x + _dot(act, wd_ref[cs, :])
    xo_ref[...] = x
    ho_ref[...] = _rms(x, g_ref[...]).astype(BF16)


def _swiglu(hn, x, w_gu, w_down, g, nchunk, cast_src):
    n, d = x.shape
    dff = w_down.shape[0]
    kern = functools.partial(_swiglu_kernel, nchunk=nchunk)
    row = lambda i: (i, 0)
    cast_in, cast_out, cast_shape = _cast_side_job(cast_src, n // TM)
    return pl.pallas_call(
        kern,
        grid=(n // TM,),
        in_specs=[
            pl.BlockSpec((TM, d), row),
            pl.BlockSpec((TM, d), row),
            _resident_spec((d, dff), lambda i: (0, 0)),
            _resident_spec((d, dff), lambda i: (0, 1)),
            _resident_spec((dff, d), lambda i: (0, 0)),
            pl.BlockSpec((1, d), lambda i: (0, 0)),
            cast_in,
        ],
        out_specs=[pl.BlockSpec((TM, d), row), pl.BlockSpec((TM, d), row), cast_out],
        out_shape=[jax.ShapeDtypeStruct((n, d), F32), jax.ShapeDtypeStruct((n, d), BF16), cast_shape],
        compiler_params=_params(("arbitrary",)),
        name="dense_swiglu",
    )(hn, x, w_gu, w_gu, w_down, g, cast_src)


def _row_copy(src_ref, src_row, dst_ref, dst_row, sem):
    return pltpu.make_async_copy(src_ref.at[pl.ds(src_row, 1), :], dst_ref.at[pl.ds(dst_row, 1), :], sem)


SUBLANES = 8


def _zero_rows(zero_ref, dst_ref, dst_row, count, limit, sem, wait):
    head = jnp.minimum((-dst_row) & (SUBLANES - 1), count)
    for j in range(SUBLANES - 1):
        @pl.when(j < head)
        def _(j=j):
            copy = _row_copy(zero_ref, 0, dst_ref, dst_row + j, sem)
            copy.wait() if wait else copy.start()

    rest = count - head
    first = dst_row + head
    for bit in range(SUBLANES.bit_length() - 1, limit.bit_length() - 1):
        size = 1 << bit
        done = (rest >> (bit + 1)) << (bit + 1)

        @pl.when(((rest >> bit) & 1) == 1)
        def _(size=size, done=done):
            start = pl.multiple_of(first + done, SUBLANES)
            copy = pltpu.make_async_copy(zero_ref.at[pl.ds(0, size), :],
                                         dst_ref.at[pl.ds(start, size), :], sem)
            copy.wait() if wait else copy.start()


def _expert_kernel(te_ref, tv_ref, xs_ref, wg_ref, wu_ref, wd_ref, ys_ref, *, nchunk):
    i = pl.program_id(0)
    half = xs_ref.shape[0] // 2
    width = wg_ref.shape[2] // nchunk

    def swiglu_rows(rows):
        x = xs_ref[rows, :].astype(BF16)
        acc = None
        for c in range(nchunk):
            cs = slice(c * width, (c + 1) * width)
            gate = _dot(x, wg_ref[0, :, cs])
            up = _dot(x, wu_ref[0, :, cs])
            act = (gate * _sigmoid(gate) * up).astype(BF16)
            part = _dot(act, wd_ref[0, cs, :])
            acc = part if acc is None else acc + part
        ys_ref[rows, :] = acc.astype(BF16).astype(F32)

    @pl.when(tv_ref[i] == 2)
    def _():
        swiglu_rows(slice(None))

    @pl.when(tv_ref[i] == 1)
    def _():
        swiglu_rows(slice(0, half))
        ys_ref[half:, :] = jnp.zeros((half, ys_ref.shape[1]), F32)

    @pl.when(tv_ref[i] == 0)
    def _():
        ys_ref[...] = jnp.zeros_like(ys_ref)


def _experts(tile_expert, tile_valid, xs, w_gu, w_down):
    r, d = xs.shape
    n_tiles = r // TM_EXPERT
    dff = w_down.shape[1]
    kern = functools.partial(_expert_kernel, nchunk=EXPERT_CHUNKS)
    return pl.pallas_call(
        kern,
        grid_spec=pltpu.PrefetchScalarGridSpec(
            num_scalar_prefetch=2,
            grid=(r // TM_EXPERT,),
            in_specs=[
                pl.BlockSpec((TM_EXPERT, d), lambda i, te, tv: (jnp.minimum(i, tv[n_tiles]), 0)),
                pl.BlockSpec((1, d, dff), lambda i, te, tv: (te[i], 0, 0)),
                pl.BlockSpec((1, d, dff), lambda i, te, tv: (te[i], 0, 1)),
                pl.BlockSpec((1, dff, d), lambda i, te, tv: (te[i], 0, 0)),
            ],
            out_specs=pl.BlockSpec((TM_EXPERT, d), lambda i, te, tv: (i, 0)),
        ),
        out_shape=jax.ShapeDtypeStruct((r, d), F32),
        compiler_params=_params(("arbitrary",)),
        name="moe_experts",
    )(tile_expert, tile_valid, xs, w_gu, w_gu, w_down)


RUN_BUF_ROWS = 2 * TM + LANES


def _run_positions(idx, off_row):
    lane = lax.broadcasted_iota(I32, idx.shape, 1)
    pos = []
    for slot in range(2):
        off = jnp.sum(jnp.where(lane == idx[:, slot:slot + 1], off_row, 0.0), axis=-1, keepdims=True)
        pos.append(off.astype(I32) + idx[:, 2 + slot:3 + slot])
    return pos


def _run_copies(tab_ref, step, hbm_ref, buf_ref, slot, sem, to_hbm, wait):
    for e in range(N_EXPERTS):
        entry = (step * N_EXPERTS + e) * 3
        off, row, length = tab_ref[entry], tab_ref[entry + 1], tab_ref[entry + 2]
        for bit in range(SUBLANES.bit_length() - 1, TM.bit_length()):
            size = 1 << bit
            done = (length >> (bit + 1)) << (bit + 1)

            @pl.when(((length >> bit) & 1) == 1)
            def _(size=size, done=done, off=off, row=row):
                in_buf = buf_ref.at[slot, pl.ds(pl.multiple_of(off + done, SUBLANES), size), :]
                in_hbm = hbm_ref.at[pl.ds(pl.multiple_of(row + done, SUBLANES), size), :]
                copy = (pltpu.make_async_copy(in_buf, in_hbm, sem.at[slot]) if to_hbm
                        else pltpu.make_async_copy(in_hbm, in_buf, sem.at[slot]))
                copy.wait() if wait else copy.start()


def _run_dispatch_kernel(tab_ref, pad_ref, hn_ref, idx_ref, off_ref, xs_ref, buf, zero_sc, sem, zsem,
                         *, n_steps):
    j = pl.program_id(0)
    slot = j & 1

    @pl.when(j == 0)
    def _():
        zero_sc[...] = jnp.zeros_like(zero_sc)
        rows = zero_sc.shape[0]
        for wait in (False, True):
            for e in range(N_EXPERTS):
                _zero_rows(zero_sc, xs_ref, pad_ref[2 * e], pad_ref[2 * e + 1], 2 * rows, zsem, wait)

            def tail(t, carry, wait=wait):
                first = pl.multiple_of(pad_ref[2 * N_EXPERTS] + t * rows, SUBLANES)
                copy = pltpu.make_async_copy(zero_sc, xs_ref.at[pl.ds(first, rows), :], zsem)
                copy.wait() if wait else copy.start()
                return carry

            lax.fori_loop(0, pad_ref[2 * N_EXPERTS + 1] // rows, tail, 0)

    @pl.when(j >= 2)
    def _():
        _run_copies(tab_ref, j - 2, xs_ref, buf, slot, sem, True, True)

    pos0, pos1 = _run_positions(idx_ref[...], off_ref[0])
    lane = lax.broadcasted_iota(I32, (hn_ref.shape[0], buf.shape[1]), 1)
    sel = jnp.where((lane == pos0) | (lane == pos1), 1.0, 0.0).astype(BF16)
    buf[slot] = _dot_tn(sel, hn_ref[...])
    _run_copies(tab_ref, j, xs_ref, buf, slot, sem, True, False)

    @pl.when(j == n_steps - 1)
    def _():
        _run_copies(tab_ref, j - 1, xs_ref, buf, 1 - slot, sem, True, True)
        _run_copies(tab_ref, j, xs_ref, buf, slot, sem, True, True)


def _run_dispatch(run_table, pad_table, hn, idx, run_offsets, n_rows):
    n, d = hn.shape
    n_steps = n // TM
    assert n_steps >= 2
    return pl.pallas_call(
        functools.partial(_run_dispatch_kernel, n_steps=n_steps),
        grid_spec=pltpu.PrefetchScalarGridSpec(
            num_scalar_prefetch=2,
            grid=(n_steps,),
            in_specs=[
                pl.BlockSpec((TM, d), lambda i, *_: (i, 0)),
                pl.BlockSpec((TM, LANES), lambda i, *_: (i, 0)),
                pl.BlockSpec((1, 1, LANES), lambda i, *_: (i, 0, 0)),
            ],
            out_specs=pl.BlockSpec(memory_space=pl.ANY),
            scratch_shapes=[pltpu.VMEM((2, RUN_BUF_ROWS, d), F32),
                            pltpu.VMEM((TM_EXPERT // 2, d), F32),
                            pltpu.SemaphoreType.DMA((2,)), pltpu.SemaphoreType.DMA(())],
        ),
        out_shape=jax.ShapeDtypeStruct((n_rows, d), F32),
        compiler_params=_params(("arbitrary",)),
        name="moe_dispatch",
    )(run_table, pad_table, hn, idx, run_offsets)


def _run_combine_kernel(tab_ref, ys_ref, x_ref, gate_ref, idx_ref, off_ref, g_ref, yp_ref, ysm_ref,
                        buf, sem, *, n_first, n_steps):
    j = pl.program_id(0)
    slot = j & 1

    @pl.when(j == 0)
    def _():
        buf[...] = jnp.zeros_like(buf)
        _run_copies(tab_ref, 0, ys_ref, buf, 0, sem, False, False)

    @pl.when(j + 1 < n_steps)
    def _():
        _run_copies(tab_ref, j + 1, ys_ref, buf, 1 - slot, sem, False, False)

    _run_copies(tab_ref, j, ys_ref, buf, slot, sem, False, True)
    runs = buf[slot].astype(BF16)
    pos = _run_positions(idx_ref[...], off_ref[0])
    lane = lax.broadcasted_iota(I32, (x_ref.shape[0], buf.shape[1]), 1)
    gate = gate_ref[...]
    y = x_ref[...]
    for k in range(2):
        sel = jnp.where(lane == pos[k], 1.0, 0.0).astype(BF16)
        y = y + gate[:, k:k + 1] * _dot(sel, runs)
    y = _rms(y, g_ref[...])

    @pl.when(j < n_first)
    def _():
        yp_ref[...] = y

    @pl.when(j >= n_first)
    def _():
        ysm_ref[...] = y


def _run_combine(run_table, ys, x, gates, idx, run_offsets, g, n_p):
    n, d = x.shape
    n_first = n_p // TM
    n_steps = n // TM
    p_spec, s_spec = _split_specs(n_first, d)
    row = lambda i, *_: (i, 0)
    return pl.pallas_call(
        functools.partial(_run_combine_kernel, n_first=n_first, n_steps=n_steps),
        grid_spec=pltpu.PrefetchScalarGridSpec(
            num_scalar_prefetch=1,
            grid=(n_steps,),
            in_specs=[
                pl.BlockSpec(memory_space=pl.ANY),
                pl.BlockSpec((TM, d), row),
                pl.BlockSpec((TM, LANES), row),
                pl.BlockSpec((TM, LANES), row),
                pl.BlockSpec((1, 1, LANES), lambda i, *_: (i, 0, 0)),
                pl.BlockSpec((1, d), lambda i, *_: (0, 0)),
            ],
            out_specs=[p_spec, s_spec],
            scratch_shapes=[pltpu.VMEM((2, RUN_BUF_ROWS, d), F32), pltpu.SemaphoreType.DMA((2,))],
        ),
        out_shape=[jax.ShapeDtypeStruct((n_p, d), F32), jax.ShapeDtypeStruct((n - n_p, d), F32)],
        compiler_params=_params(("arbitrary",)),
        name="moe_combine",
    )(run_table, ys, x, gates, idx, run_offsets, g)


def _run_tables(tile_counts, n_rows):
    cnt = tile_counts[:, 0, :N_EXPERTS].astype(I32)
    run = ((cnt + SUBLANES - 1) // SUBLANES) * SUBLANES
    off = jnp.cumsum(run, axis=1) - run
    used = jnp.sum(run, axis=0)
    padded = ((used + TM_EXPERT - 1) // TM_EXPERT) * TM_EXPERT
    ends = jnp.cumsum(padded)
    starts = ends - padded
    row = starts[None, :] + jnp.cumsum(run, axis=0) - run
    run_table = jnp.stack([off, row, run], axis=2).reshape(-1)
    run_offsets = jnp.pad(off.astype(F32), ((0, 0), (0, LANES - N_EXPERTS)))[:, None, :]
    n_tiles = n_rows // TM_EXPERT
    used_tiles = ends[-1] // TM_EXPERT
    pad_table = jnp.stack([starts + used, padded - used], axis=1).reshape(-1)
    pad_table = jnp.concatenate([pad_table, jnp.stack([ends[-1], n_rows - ends[-1]])])
    tile_start = jnp.arange(n_tiles, dtype=I32) * TM_EXPERT
    tile_expert = jnp.sum((tile_start[:, None] >= ends[None, :]).astype(I32), axis=1)
    last_expert = jnp.sum((ends[-1] - 1 >= ends).astype(I32))
    tile_expert = jnp.minimum(tile_expert, last_expert)
    used_end = (starts + used)[tile_expert]
    half = TM_EXPERT // 2
    tile_rows = jnp.clip(used_end - tile_start, 0, TM_EXPERT)
    tile_valid = jnp.where(tile_start < ends[-1], (tile_rows + half - 1) // half, 0)
    tile_valid = jnp.concatenate([tile_valid, (used_tiles - 1)[None]])
    return run_table, run_offsets, pad_table, tile_expert, tile_valid


def _pad_time(a, nseq, t):
    w = a.shape[1]
    a = jnp.pad(a.reshape(nseq, t, w), ((0, 0), (0, SAMPLE_PAD_T - t), (0, 0)))
    return a.reshape(nseq * SAMPLE_PAD_T, w)


def kernel(x_prompt, x_sample, state_gla_S, state_mlstm_C, state_mlstm_n, state_mlstm_m,
           norm_mix, norm_ffn, norm_final,
           gla_w_in, gla_w_a2, gla_b_a, gla_g_norm, gla_w_out,
           mlstm_w_in, mlstm_b_gate, mlstm_g_norm, mlstm_w_out,
           ffn_w_gu, ffn_w_down,
           moe_w_router, moe_b_router, moe_w_gu, moe_w_down):
    bp, tp, d = x_prompt.shape
    bs, ts, _ = x_sample.shape
    n_p = bp * tp
    n_s = bs * ts
    n = n_p + n_s
    assert n_p % TM == 0 and n_s % TM == 0 and tp % (2 * CHUNK) == 0 and ts <= SAMPLE_PAD_T
    assert norm_mix.shape[0] == 2, "one GLA layer followed by one mLSTM layer"
    qk_w = d // 2
    main_w = 2 * qk_w + 2 * d

    x_p = x_prompt.reshape(n_p, d)
    x_s = x_sample.reshape(n_s, d)

    w_in = gla_w_in[0]
    w_lr = jnp.pad(w_in[:, main_w:], ((0, 0), (0, LANES - GLA_RANK))).astype(BF16)
    w_a2 = jnp.pad(gla_w_a2[0], ((0, LANES - GLA_RANK), (0, 0))).astype(BF16)
    z, log_a, ffn_wgu = _gla_in_proj(x_p, _pad_time(x_s, bs, ts), norm_mix[0][None], w_in, main_w,
                                     w_lr, w_a2, gla_b_a[0][None], ffn_w_gu[0])
    s0_p = jnp.zeros((bp,) + state_gla_S.shape[2:], F32)
    u_p, s_p = _gla_recurrence(z, log_a, s0_p, gla_g_norm[0], bp, tp, CHUNK, 16 * CHUNK)
    u_s, s_s = _gla_recurrence_short(z, log_a, state_gla_S[0], gla_g_norm[0], bs, SAMPLE_PAD_T, ts,
                                     SAMPLE_SEQS_PER_STEP, n_p)
    x, hn, ffn_wd = _out_proj(u_p, u_s, x_p, x_s, gla_w_out[0].astype(BF16),
                              norm_ffn[0][None], ffn_w_down[0])
    x, hn, moe_wgu = _swiglu(hn, x, ffn_wgu, ffn_wd,
                             norm_mix[1][None], SWIGLU_CHUNKS,
                             moe_w_gu[0].reshape(N_EXPERTS * d, moe_w_gu.shape[3]))

    w_in = mlstm_w_in[0]
    w_gt = jnp.pad(w_in[:, main_w:], ((0, 0), (0, LANES - 2 * MLSTM_HEADS))).astype(BF16)
    b_gt = jnp.pad(mlstm_b_gate[0], (0, LANES - 2 * MLSTM_HEADS))[None]
    dff_e = moe_w_down.shape[2]
    z, gates, moe_wd = _mlstm_in_proj(hn, n_p, _pad_time(hn[n_p:], bs, ts), w_in, main_w, w_gt, b_gt,
                                      moe_w_down[0].reshape(N_EXPERTS * dff_e, d))
    pairs = MLSTM_HEADS // 2
    dk2 = 2 * state_mlstm_C.shape[3]
    dvm = state_mlstm_C.shape[4]
    c0_p = jnp.zeros((bp, pairs, dk2, dvm), F32)
    n0_p = jnp.zeros((bp, pairs, dk2), F32)
    m0_p = jnp.zeros((bp, 1, LANES), F32)
    u_p, c_p, nn_p, m_p = _mlstm_recurrence_blocked(z, gates, c0_p, n0_p, m0_p,
                                                    mlstm_g_norm[0], bp, tp, CHUNK, 16 * CHUNK)
    c0_s = state_mlstm_C[0]
    n0_s = state_mlstm_n[0].reshape(bs, pairs, dk2)
    m0_s = jnp.pad(state_mlstm_m[0], ((0, 0), (0, LANES - MLSTM_HEADS)))[:, None, :]
    u_s, c_s, nn_s, m_s = _mlstm_recurrence_short(z, gates, c0_s, n0_s, m0_s, mlstm_g_norm[0], bs,
                                                  SAMPLE_PAD_T, ts, SAMPLE_SEQS_PER_STEP, n_p)

    w_r = jnp.pad(moe_w_router[0], ((0, 0), (0, LANES - N_EXPERTS)))
    w_r_hi = w_r.astype(BF16)
    w_r = jnp.stack([w_r_hi, (w_r - w_r_hi.astype(F32)).astype(BF16)])
    b_r = jnp.pad(moe_b_router[0], (0, LANES - N_EXPERTS))[None]
    x, hn, route_g, route_i, tile_counts = _out_proj_router(
        u_p, u_s, x, mlstm_w_out[0].astype(BF16), norm_ffn[1][None], w_r, b_r)
    n_rows = 2 * n + (n // TM) * N_EXPERTS * (SUBLANES - 1) + N_EXPERTS * (TM_EXPERT - 1)
    n_rows = -(-n_rows // TM_EXPERT) * TM_EXPERT
    run_table, run_offsets, pad_table, tile_expert, tile_valid = _run_tables(tile_counts, n_rows)
    xs = _run_dispatch(run_table, pad_table, hn, route_i, run_offsets, n_rows)
    ys = _experts(tile_expert, tile_valid, xs, moe_wgu.reshape(moe_w_gu.shape[1:]),
                  moe_wd.reshape(moe_w_down.shape[1:]))
    y_p, y_s = _run_combine(run_table, ys, x, route_g, route_i, run_offsets, norm_final[None], n_p)

    y_prompt = y_p.reshape(bp, tp, d)
    y_sample = y_s.reshape(bs, ts, d)
    hd = state_mlstm_C.shape[2:]
    return (y_prompt, y_sample,
            s_p[None], c_p.reshape((1, bp) + hd), nn_p.reshape(1, bp, hd[0], hd[1]),
            m_p[None, :, 0, :MLSTM_HEADS],
            s_s[None], c_s[None], nn_s.reshape(1, bs, hd[0], hd[1]), m_s[None, :, 0, :MLSTM_HEADS])


def _mlstm_in_proj(hn, n_p, hn_s, w_in, wz, w_gate, b_gate, cast_src):
    d = hn.shape[1]
    n = n_p + hn_s.shape[0]
    n_first = n_p // TM
    cast_in, cast_out, cast_shape = _cast_side_job(cast_src, n // TM)

    def kern(hp_ref, hs_ref, w_ref, wg_ref, b_ref, cast_ref, z_ref, gate_ref, cast_out_ref, wb_sc):
        @pl.when(pl.program_id(0) == 0)
        def _():
            wb_sc[...] = w_ref[...].astype(BF16)

        cast_out_ref[...] = cast_ref[...].astype(BF16)
        h = _pick_rows(pl.program_id(0), n_first, hp_ref, hs_ref)
        width = wz // IN_PROJ_COL_CHUNKS
        for c in range(IN_PROJ_COL_CHUNKS):
            cs = slice(c * width, (c + 1) * width)
            z_ref[:, cs] = _dot(h, wb_sc[:, cs])
        gp = _dot(h, wg_ref[...]) + b_ref[...]
        gc = GATE_CAP * jnp.tanh(gp * (1.0 / GATE_CAP))
        lane = lax.broadcasted_iota(I32, gc.shape, 1)
        out = jnp.where(lane < MLSTM_HEADS, gc, _log_sigmoid(gc))
        gate_ref[...] = jnp.where(lane < 2 * MLSTM_HEADS, out, 0.0)

    return pl.pallas_call(
        kern,
        grid=(n // TM,),
        in_specs=_split_specs(n_first, d) + [
            _resident_spec((d, wz), lambda i: (0, 0)),
            _resident_spec((d, LANES), lambda i: (0, 0)),
            pl.BlockSpec((1, LANES), lambda i: (0, 0)),
            cast_in,
        ],
        out_specs=[pl.BlockSpec((TM, wz), lambda i: (i, 0)), pl.BlockSpec((TM, LANES), lambda i: (i, 0)),
                   cast_out],
        out_shape=[jax.ShapeDtypeStruct((n, wz), F32), jax.ShapeDtypeStruct((n, LANES), F32),
                   cast_shape],
        scratch_shapes=[pltpu.VMEM((d, wz), BF16)],
        compiler_params=_params(("arbitrary",)),
        name="in_proj_mlstm",
    )(hn, hn_s, w_in, w_gate, b_gate, cast_src)
```

```python
import functools

import jax
import jax.numpy as jnp
import numpy as np
from jax import lax
from jax.experimental import pallas as pl
from jax.experimental.pallas import tpu as pltpu

F32 = jnp.float32
BF16 = jnp.bfloat16
I32 = jnp.int32

EPS = 1e-6
GLA_HEADS = 4
GLA_RANK = 16
GLA_TAU = 16.0
MLSTM_HEADS = 8
GATE_CAP = 15.0
CHUNK = 64
N_EXPERTS = 8
NEG_BIG = -1e30

LANES = 128
TM = 512
TM_EXPERT = 512
SAMPLE_PAD_T = 8
SAMPLE_SEQS_PER_STEP = 16
SWIGLU_CHUNKS = 11
EXPERT_CHUNKS = 7
VMEM_LIMIT = 56 * 1024 * 1024


def _dot(a, b):
    return jnp.dot(a, b, preferred_element_type=F32)


def _dot_nt(a, b):
    return lax.dot_general(a, b, (((1,), (1,)), ((), ())), preferred_element_type=F32)


def _dot_tn(a, b):
    return lax.dot_general(a, b, (((0,), (0,)), ((), ())), preferred_element_type=F32)


def _sigmoid(x):
    return 1.0 / (1.0 + jnp.exp(-x))


def _log_sigmoid(x):
    return jnp.minimum(x, 0.0) - jnp.log(1.0 + jnp.exp(-jnp.abs(x)))


def _rms(x, g):
    return x * lax.rsqrt(jnp.mean(x * x, axis=-1, keepdims=True) + EPS) * g


def _cumsum_rows(x, n):
    ridx = lax.broadcasted_iota(I32, x.shape, 0)
    s = 1
    while s < n:
        x = x + jnp.where(ridx >= s, pltpu.roll(x, s, 0), 0.0)
        s *= 2
    return x


def _params(sem):
    return pltpu.CompilerParams(dimension_semantics=sem, vmem_limit_bytes=VMEM_LIMIT)


def _resident_spec(block_shape, index_map):
    return pl.BlockSpec(block_shape, index_map, pipeline_mode=pl.Buffered(1))


def _cast_side_job(src, n_steps):
    rows, cols = src.shape
    n_slabs = max(s for s in range(1, n_steps + 1) if rows % (16 * s) == 0)
    spec = pl.BlockSpec((rows // n_slabs, cols), lambda i, *_: (jnp.minimum(i, n_slabs - 1), 0))
    return spec, spec, jax.ShapeDtypeStruct(src.shape, BF16)


IN_PROJ_COL_CHUNKS = 4
ROW_PARTS = 2


def _pick_rows(i, n_first, first_ref, second_ref):
    return jnp.where(i < n_first, first_ref[...], second_ref[...])


def _split_specs(n_first, width):
    return [pl.BlockSpec((TM, width), lambda i, *_: (jnp.minimum(i, n_first - 1), 0)),
            pl.BlockSpec((TM, width), lambda i, *_: (jnp.maximum(i - n_first, 0), 0))]


def _gla_in_proj_kernel(xp_ref, xs_ref, g_ref, w_ref, wg_ref, w2_ref, b_ref, cast_ref,
                        z_ref, gate_ref, cast_out_ref, wb_sc, *, n_first):
    @pl.when(pl.program_id(0) == 0)
    def _():
        wb_sc[...] = w_ref[...].T.astype(BF16)

    cast_out_ref[...] = cast_ref[...].astype(BF16)
    first = pl.program_id(0) < n_first
    width = wb_sc.shape[1] // IN_PROJ_COL_CHUNKS
    part = xp_ref.shape[0] // ROW_PARTS
    for r in range(ROW_PARTS):
        rows = slice(r * part, (r + 1) * part)
        x = jnp.where(first, xp_ref[rows, :], xs_ref[rows, :])
        hn = _rms(x, g_ref[...]).astype(BF16)
        for c in range(IN_PROJ_COL_CHUNKS):
            cs = slice(c * width, (c + 1) * width)
            z_ref[rows, cs] = _dot(hn, wb_sc[:, cs])
        a = _dot(hn, wg_ref[...])
        la = _dot(a.astype(BF16), w2_ref[...]) + b_ref[...]
        gate_ref[rows, :] = _log_sigmoid(la) * (1.0 / GLA_TAU)


def _gla_in_proj(x_p, x_s, g, w_in, wz, w_gate, w2, b, cast_src):
    d = x_p.shape[1]
    n = x_p.shape[0] + x_s.shape[0]
    n_first = x_p.shape[0] // TM
    gw = b.shape[1]
    cast_in, cast_out, cast_shape = _cast_side_job(cast_src, n // TM)
    return pl.pallas_call(
        functools.partial(_gla_in_proj_kernel, n_first=n_first),
        grid=(n // TM,),
        in_specs=_split_specs(n_first, d) + [
            pl.BlockSpec((1, d), lambda i: (0, 0)),
            _resident_spec((wz, d), lambda i: (0, 0)),
            pl.BlockSpec((d, LANES), lambda i: (0, 0)),
            pl.BlockSpec(w2.shape, lambda i: (0, 0)),
            pl.BlockSpec((1, gw), lambda i: (0, 0)),
            cast_in,
        ],
        out_specs=[
            pl.BlockSpec((TM, wz), lambda i: (i, 0)),
            pl.BlockSpec((TM, gw), lambda i: (i, 0)),
            cast_out,
        ],
        out_shape=[jax.ShapeDtypeStruct((n, wz), F32), jax.ShapeDtypeStruct((n, gw), F32), cast_shape],
        scratch_shapes=[pltpu.VMEM((d, wz), BF16)],
        compiler_params=_params(("arbitrary",)),
        name="in_proj_gla",
    )(x_p, x_s, g, w_in, w_gate, w2, b, cast_src)


def _gla_kernel(q_ref, k_ref, v_ref, r_ref, la_ref, s0_ref, gn_ref, u_ref, st_ref, st_sc,
                *, chunk, tb, nblk):
    heads = GLA_HEADS
    dk = q_ref.shape[1] // heads
    dv = v_ref.shape[1] // heads
    c = pl.program_id(1)

    @pl.when(c == 0)
    def _():
        for h in range(heads):
            st_sc[h] = s0_ref[0, h].T

    ri = lax.broadcasted_iota(I32, (chunk, chunk), 0)
    ci = lax.broadcasted_iota(I32, (chunk, chunk), 1)
    causal = ri >= ci
    kscale = dk ** -0.5
    state = [st_sc[h] for h in range(heads)]
    for j in range(tb // chunk):
        rows = slice(j * chunk, (j + 1) * chunk)
        b_all = _cumsum_rows(la_ref[rows, :], chunk)
        for h in range(heads):
            ks = slice(h * dk, (h + 1) * dk)
            vs = slice(h * dv, (h + 1) * dv)
            b = b_all[:, ks]
            b_end = b[chunk - 1:chunk, :]
            q = q_ref[rows, ks]
            k = k_ref[rows, ks] * kscale
            qg = (q * jnp.exp(b)).astype(BF16)
            kg = (k * jnp.exp(-b)).astype(BF16)
            ke = (k * jnp.exp(b_end - b)).astype(BF16)
            v = v_ref[rows, vs].astype(BF16)
            a = jnp.where(causal, _dot_nt(qg, kg), 0.0).astype(BF16)
            o = _dot_nt(qg, state[h].astype(BF16)) + _dot(a, v)
            state[h] = state[h] * jnp.exp(b_end) + _dot_tn(v, ke)
            on = _rms(o, gn_ref[h:h + 1, :])
            r = r_ref[rows, vs]
            u_ref[rows, vs] = (r * _sigmoid(r) * on).astype(BF16)
    for h in range(heads):
        st_sc[h] = state[h]

    @pl.when(c == nblk - 1)
    def _():
        for h in range(heads):
            st_ref[0, h] = st_sc[h].T


def _gla_recurrence(z, log_a, s0, g_norm, nseq, t, chunk, tb):
    qk_w = log_a.shape[1]
    v_w = 2 * qk_w
    nblk = t // tb
    heads, dk, dv = s0.shape[1:]
    kern = functools.partial(_gla_kernel, chunk=chunk, tb=tb, nblk=nblk)
    row = lambda b, c: b * nblk + c
    return pl.pallas_call(
        kern,
        grid=(nseq, nblk),
        in_specs=[
            pl.BlockSpec((tb, qk_w), lambda b, c: (row(b, c), 0)),
            pl.BlockSpec((tb, qk_w), lambda b, c: (row(b, c), 1)),
            pl.BlockSpec((tb, v_w), lambda b, c: (row(b, c), 1)),
            pl.BlockSpec((tb, v_w), lambda b, c: (row(b, c), 2)),
            pl.BlockSpec((tb, qk_w), lambda b, c: (row(b, c), 0)),
            pl.BlockSpec((1, heads, dk, dv), lambda b, c: (b, 0, 0, 0)),
            pl.BlockSpec((heads, dv), lambda b, c: (0, 0)),
        ],
        out_specs=[
            pl.BlockSpec((tb, v_w), lambda b, c: (row(b, c), 0)),
            pl.BlockSpec((1, heads, dk, dv), lambda b, c: (b, 0, 0, 0)),
        ],
        out_shape=[
            jax.ShapeDtypeStruct((nseq * t, v_w), BF16),
            jax.ShapeDtypeStruct(s0.shape, F32),
        ],
        scratch_shapes=[pltpu.VMEM((heads, dv, dk), F32)],
        compiler_params=_params(("arbitrary", "arbitrary")),
        name="gla_recurrence_t%d" % t,
    )(z, z, z, z, log_a, s0, g_norm)


def _real_rows_selector(nsq, t, t_real):
    i = lax.broadcasted_iota(I32, (nsq * t_real, nsq * t), 0)
    r = lax.broadcasted_iota(I32, (nsq * t_real, nsq * t), 1)
    return jnp.where(r == (i // t_real) * t + i % t_real, 1.0, 0.0).astype(BF16)


def _gla_short_kernel(q_ref, k_ref, v_ref, r_ref, la_ref, s0_ref, gn_ref, u_ref, st_ref,
                      *, t, t_real, nsq):
    heads = GLA_HEADS
    dk = q_ref.shape[1] // heads
    dv = v_ref.shape[1] // heads
    nrow = nsq * t
    width = q_ref.shape[1]
    rc = lax.broadcasted_iota(I32, (nrow, width), 0) & (t - 1)
    log_a = jnp.where(rc < t_real, la_ref[...], 0.0)
    b = _chunk_scan(log_a, rc, t, jnp.add, 0.0)
    pick = _real_rows_selector(nsq, t, t_real)
    b_last = b.reshape(nsq, t, width)[:, t - 1:t, :]
    b_end = jnp.broadcast_to(b_last, (nsq, t, width)).reshape(nrow, width)
    k = k_ref[...] * (dk ** -0.5)
    qg = q_ref[...] * jnp.exp(b)
    kg = k * jnp.exp(-b)
    ke = k * jnp.exp(b_end - b)
    decay = jnp.exp(b_end)
    ri = lax.broadcasted_iota(I32, (nrow, nrow), 0)
    ci = lax.broadcasted_iota(I32, (nrow, nrow), 1)
    shift = t.bit_length() - 1
    mask = ((ri >> shift) == (ci >> shift)) & (ri >= ci)
    for h in range(heads):
        ks = slice(h * dk, (h + 1) * dk)
        vs = slice(h * dv, (h + 1) * dv)
        a = jnp.where(mask, _dot_nt(qg[:, ks].astype(BF16), kg[:, ks].astype(BF16)), 0.0)
        o = _dot(a.astype(BF16), v_ref[:, vs].astype(BF16))
        decay_t = decay[:, ks].T
        o_state = []
        for sq in range(nsq):
            r = slice(sq * t, (sq + 1) * t)
            s0 = s0_ref[sq, h]
            o_state.append(_dot(qg[r, ks].astype(BF16), s0.astype(BF16)))
            upd = _dot_tn(ke[r, ks].astype(BF16), v_ref[r, vs].astype(BF16))
            st_ref[sq, h] = decay_t[:, sq * t:sq * t + 1] * s0 + upd
        on = _rms(o + jnp.concatenate(o_state, axis=0), gn_ref[h:h + 1, :])
        rg = r_ref[:, vs]
        gated = (rg * _sigmoid(rg) * on).astype(BF16)
        u_ref[:, vs] = _dot(pick, gated).astype(BF16)


def _gla_recurrence_short(z, log_a, s0, g_norm, nseq, t, t_real, nsq, row0):
    qk_w = log_a.shape[1]
    v_w = 2 * qk_w
    heads, dk, dv = s0.shape[1:]
    assert nsq * t == dk == LANES and row0 % (nsq * t) == 0
    rb = nsq * t
    blk0 = row0 // rb
    return pl.pallas_call(
        functools.partial(_gla_short_kernel, t=t, t_real=t_real, nsq=nsq),
        grid=(nseq // nsq,),
        in_specs=[
            pl.BlockSpec((rb, qk_w), lambda b: (blk0 + b, 0)),
            pl.BlockSpec((rb, qk_w), lambda b: (blk0 + b, 1)),
            pl.BlockSpec((rb, v_w), lambda b: (blk0 + b, 1)),
            pl.BlockSpec((rb, v_w), lambda b: (blk0 + b, 2)),
            pl.BlockSpec((rb, qk_w), lambda b: (blk0 + b, 0)),
            pl.BlockSpec((nsq, heads, dk, dv), lambda b: (b, 0, 0, 0)),
            pl.BlockSpec((heads, dv), lambda b: (0, 0)),
        ],
        out_specs=[
            pl.BlockSpec((nsq * t_real, v_w), lambda b: (b, 0)),
            pl.BlockSpec((nsq, heads, dk, dv), lambda b: (b, 0, 0, 0)),
        ],
        out_shape=[
            jax.ShapeDtypeStruct((nseq * t_real, v_w), BF16),
            jax.ShapeDtypeStruct(s0.shape, F32),
        ],
        compiler_params=_params(("arbitrary",)),
        name="gla_recurrence_t%d" % t,
    )(z, z, z, z, log_a, s0, g_norm)


def _mlstm_short_kernel(q_ref, k_ref, v_ref, op_ref, gt_ref, e_ref, c0_ref, n0_ref, m0_ref, gn_ref,
                        u_ref, ct_ref, nt_ref, mt_ref, *, t, t_real, nsq):
    heads = MLSTM_HEADS
    pairs = heads // 2
    dv = v_ref.shape[1] // heads
    dk = q_ref.shape[1] // heads
    nrow = nsq * t

    def per_seq(x3):
        return jnp.broadcast_to(x3, (nsq, t, x3.shape[2])).reshape(nrow, x3.shape[2])

    def last(x):
        return x.reshape(nsq, t, x.shape[1])[:, t - 1:t, :]

    rc = lax.broadcasted_iota(I32, (nrow, LANES), 0) & (t - 1)
    lane_g = lax.broadcasted_iota(I32, (nrow, LANES), 1)
    gts = jnp.where(rc < t_real, gt_ref[...],
                    jnp.where(lane_g < heads, NEG_BIG, 0.0))
    pick = _real_rows_selector(nsq, t, t_real)
    f_cum = pltpu.roll(_chunk_scan(gts, rc, t, jnp.add, 0.0), LANES - heads, 1)
    b = gts - f_cum
    cmb = _chunk_scan(b, rc, t, jnp.maximum, NEG_BIG)
    mp = per_seq(m0_ref[...])
    big_m = jnp.maximum(mp, cmb)
    m_t = f_cum + big_m
    m_new = last(m_t)
    mt_ref[...] = m_new
    mn = per_seq(m_new)
    fe = per_seq(last(f_cum))
    w_i = jnp.exp(mp - big_m)
    em = jnp.exp(-m_t)
    k_sc = jnp.exp(fe - f_cum + gts - mn)
    w_c = jnp.exp(fe + mp - mn)
    ex = _dot(_pack_terms([w_i, k_sc, -big_m, em, w_c], heads), e_ref[...])
    offs = np.cumsum([0] + [heads * w for w in SHORT_EXPAND_WIDTHS])
    q = q_ref[...]
    k = k_ref[...] * (dk ** -0.5)
    qs = q * ex[:, offs[0]:offs[1]]
    ke = k * ex[:, offs[1]:offs[2]]
    neg_m = ex[:, offs[2]:offs[3]]
    em_v = ex[:, offs[3]:offs[4]]
    wc_k = ex[:, offs[4]:offs[5]]
    bt = b.T

    ri = lax.broadcasted_iota(I32, (nrow, nrow), 0)
    ci = lax.broadcasted_iota(I32, (nrow, nrow), 1)
    shift = t.bit_length() - 1
    mask = ((ri >> shift) == (ci >> shift)) & (ri >= ci)
    lane1 = lax.broadcasted_iota(I32, (1, LANES), 1)
    rowi = lax.broadcasted_iota(I32, (LANES, LANES), 0)
    ones_v = jnp.ones((nrow, dv), BF16)
    ones_k = jnp.ones((LANES, dv), BF16)
    for p in range(pairs):
        ps = slice(p * LANES, (p + 1) * LANES)
        qp = q[:, ps]
        qsp = qs[:, ps]
        kpb = k[:, ps].astype(BF16)
        n_rows = per_seq(n0_ref[:, p:p + 1, :])
        intra, den_state, qsh = [], [], []
        for jj in range(2):
            h = 2 * p + jj
            hs = slice(h * dv, (h + 1) * dv)
            mine = (lane1 >= jj * dk) & (lane1 < (jj + 1) * dk)
            qh = jnp.where(mine, qp, 0.0).astype(BF16)
            qsh.append(jnp.where(mine, qsp, 0.0))
            d = neg_m[:, hs] + bt[h:h + 1, :]
            s = _dot_nt(qh, kpb) * jnp.where(mask, jnp.exp(d), 0.0)
            vh = jnp.concatenate([v_ref[:, hs].astype(BF16), ones_v], axis=1)
            intra.append(_dot(s.astype(BF16), vh))
            den_state.append(_dot((qsh[jj] * n_rows).astype(BF16), ones_k))
        num_state = [[], []]
        for sq in range(nsq):
            r = slice(sq * t, (sq + 1) * t)
            lhs = jnp.concatenate([qsh[0][r], qsh[1][r]], axis=0).astype(BF16)
            c_pair = jnp.concatenate([c0_ref[sq, 2 * p], c0_ref[sq, 2 * p + 1]], axis=0)
            res = _dot(lhs, c_pair.astype(BF16))
            num_state[0].append(res[:t])
            num_state[1].append(res[t:])
        for jj in range(2):
            h = 2 * p + jj
            hs = slice(h * dv, (h + 1) * dv)
            num = intra[jj][:, :dv] + jnp.concatenate(num_state[jj], axis=0)
            den = intra[jj][:, dv:] + den_state[jj]
            hh = num / jnp.maximum(jnp.abs(den), em_v[:, hs])
            hn = _rms(hh, gn_ref[h:h + 1, :])
            gated = (_sigmoid(op_ref[:, hs]) * hn).astype(BF16)
            u_ref[:, hs] = _dot(pick, gated).astype(BF16)
        kep = ke[:, ps]
        for sq in range(nsq):
            r = slice(sq * t, (sq + 1) * t)
            vp = v_ref[r, 2 * p * dv:(2 * p + 2) * dv].astype(BF16)
            full = _dot_tn(kep[r].astype(BF16), vp)
            upd = jnp.where(rowi < dk, full[:, :dv], full[:, dv:])
            w_row = w_c[sq * t:sq * t + 1, :]
            w_col = jnp.where(rowi < dk, w_row[:, 2 * p:2 * p + 1], w_row[:, 2 * p + 1:2 * p + 2])
            c_pair = jnp.concatenate([c0_ref[sq, 2 * p], c0_ref[sq, 2 * p + 1]], axis=0)
            c_new = w_col * c_pair + upd
            ct_ref[sq, 2 * p] = c_new[:dk]
            ct_ref[sq, 2 * p + 1] = c_new[dk:]
        k_sum = jnp.sum(kep.reshape(nsq, t, LANES), axis=1, keepdims=True)
        nt_ref[:, p:p + 1, :] = last(wc_k[:, ps]) * n0_ref[:, p:p + 1, :] + k_sum


def _mlstm_recurrence_short(z, gates, c0, n0, m0, g_norm, nseq, t, t_real, nsq, row0):
    heads = g_norm.shape[0]
    v_w = heads * g_norm.shape[1]
    qk_w = v_w // 2
    assert nsq * t == LANES and row0 % (nsq * t) == 0
    expand = _expand_matrix(heads, SHORT_EXPAND_WIDTHS)
    kern = functools.partial(_mlstm_short_kernel, t=t, t_real=t_real, nsq=nsq)
    rb = nsq * t
    blk0 = row0 // rb
    st4 = lambda b: (b, 0, 0, 0)
    st3 = lambda b: (b, 0, 0)
    return pl.pallas_call(
        kern,
        grid=(nseq // nsq,),
        in_specs=[
            pl.BlockSpec((rb, qk_w), lambda b: (blk0 + b, 0)),
            pl.BlockSpec((rb, qk_w), lambda b: (blk0 + b, 1)),
            pl.BlockSpec((rb, v_w), lambda b: (blk0 + b, 1)),
            pl.BlockSpec((rb, v_w), lambda b: (blk0 + b, 2)),
            pl.BlockSpec((rb, LANES), lambda b: (blk0 + b, 0)),
            pl.BlockSpec(expand.shape, lambda b: (0, 0)),
            pl.BlockSpec((nsq,) + c0.shape[1:], st4),
            pl.BlockSpec((nsq,) + n0.shape[1:], st3),
            pl.BlockSpec((nsq,) + m0.shape[1:], st3),
            pl.BlockSpec(g_norm.shape, lambda b: (0, 0)),
        ],
        out_specs=[
            pl.BlockSpec((nsq * t_real, v_w), lambda b: (b, 0)),
            pl.BlockSpec((nsq,) + c0.shape[1:], st4),
            pl.BlockSpec((nsq,) + n0.shape[1:], st3),
            pl.BlockSpec((nsq,) + m0.shape[1:], st3),
        ],
        out_shape=[
            jax.ShapeDtypeStruct((nseq * t_real, v_w), BF16),
            jax.ShapeDtypeStruct(c0.shape, F32),
            jax.ShapeDtypeStruct(n0.shape, F32),
            jax.ShapeDtypeStruct(m0.shape, F32),
        ],
        compiler_params=_params(("arbitrary",)),
        name="mlstm_recurrence_t%d" % t,
    )(z, z, z, z, gates, expand, c0, n0, m0, g_norm)


N_SPLIT = 3
EXPAND_WIDTHS = (64, 64, 128, 128)
SHORT_EXPAND_WIDTHS = EXPAND_WIDTHS + (64,)


def _expand_matrix(heads, widths):
    assert len(widths) * N_SPLIT * heads <= LANES
    cols = []
    for qi, width in enumerate(widths):
        sel = np.zeros((LANES, heads * width), np.float32)
        for term in range(N_SPLIT):
            for h in range(heads):
                sel[(qi * N_SPLIT + term) * heads + h, h * width:(h + 1) * width] = 1.0
        cols.append(sel)
    return jnp.asarray(np.concatenate(cols, axis=1), BF16)


def _chunk_scan(x, rc, n, op, fill):
    s = 1
    while s < n:
        x = op(x, jnp.where(rc >= s, pltpu.roll(x, s, 0), fill))
        s *= 2
    return x


def _pack_terms(values, heads):
    lane = lax.broadcasted_iota(I32, values[0].shape, 1)
    packed = jnp.zeros(values[0].shape, F32)
    slot = 0
    for val in values:
        rest = val
        for term in range(N_SPLIT):
            part = rest.astype(BF16).astype(F32)
            rest = rest - part
            moved = part if slot == 0 else pltpu.roll(part, slot * heads, 1)
            packed = jnp.where((lane >= slot * heads) & (lane < (slot + 1) * heads), moved, packed)
            slot += 1
    return packed.astype(BF16)


def _mlstm_block_kernel(q_ref, k_ref, v_ref, op_ref, gt_ref, e_ref, c0_ref, n0_ref, m0_ref, gn_ref,
                        u_ref, ct_ref, nt_ref, mt_ref, c_sc, nb_sc, m_sc, *, chunk, tb, nblk):
    heads = MLSTM_HEADS
    pairs = heads // 2
    dv = v_ref.shape[1] // heads
    dk = q_ref.shape[1] // heads
    c = pl.program_id(1)

    @pl.when(c == 0)
    def _():
        c_sc[...] = c0_ref[0]
        for p in range(pairs):
            nb_sc[p] = jnp.broadcast_to(n0_ref[0, p:p + 1, :], (LANES, LANES)).T
        m_sc[...] = m0_ref[0]

    nch = tb // chunk
    rc = lax.broadcasted_iota(I32, (tb, LANES), 0) & (chunk - 1)
    gts = gt_ref[...]
    f_cum = pltpu.roll(_chunk_scan(gts, rc, chunk, jnp.add, 0.0), LANES - heads, 1)
    b = gts - f_cum
    cmb = _chunk_scan(b, rc, chunk, jnp.maximum, NEG_BIG)
    m_prev = m_sc[...]
    mp_rows, mn_rows, fe_rows, w_c = [], [], [], []
    for j in range(nch):
        last = (j + 1) * chunk - 1
        f_end = f_cum[last:last + 1, :]
        m_new = f_end + jnp.maximum(m_prev, cmb[last:last + 1, :])
        w_c.append(jnp.exp(f_end + m_prev - m_new))
        mp_rows.append(jnp.broadcast_to(m_prev, (chunk, LANES)))
        mn_rows.append(jnp.broadcast_to(m_new, (chunk, LANES)))
        fe_rows.append(jnp.broadcast_to(f_end, (chunk, LANES)))
        m_prev = m_new
    m_sc[...] = m_prev
    mp = jnp.concatenate(mp_rows, axis=0)
    mn = jnp.concatenate(mn_rows, axis=0)
    fe = jnp.concatenate(fe_rows, axis=0)
    big_m = jnp.maximum(mp, cmb)
    w_i = jnp.exp(mp - big_m)
    em = jnp.exp(-(f_cum + big_m))
    k_sc = jnp.exp(fe - f_cum + gts - mn)
    ex = _dot(_pack_terms([w_i, k_sc, -big_m, em], heads), e_ref[...])
    o1 = heads * EXPAND_WIDTHS[0]
    o2 = o1 + heads * EXPAND_WIDTHS[1]
    o3 = o2 + heads * EXPAND_WIDTHS[2]
    q = q_ref[...]
    k = k_ref[...] * (dk ** -0.5)
    qs = q * ex[:, :o1]
    ke = k * ex[:, o1:o2]
    neg_m = ex[:, o2:o3]
    em_v = ex[:, o3:]
    bt = b.T

    ri = lax.broadcasted_iota(I32, (chunk, chunk), 0)
    ci = lax.broadcasted_iota(I32, (chunk, chunk), 1)
    causal = ri >= ci
    lane1 = lax.broadcasted_iota(I32, (1, LANES), 1)
    rowi = lax.broadcasted_iota(I32, (LANES, LANES), 0)
    ones_v = jnp.ones((chunk, dv), BF16)
    states = [[None] * pairs for _ in range(nch)]
    for p in range(pairs):
        ps = slice(p * LANES, (p + 1) * LANES)
        cp = c_sc[p]
        nb = nb_sc[p]
        for j in range(nch):
            rows = slice(j * chunk, (j + 1) * chunk)
            states[j][p] = jnp.concatenate([cp, nb], axis=1).astype(BF16)
            vp = jnp.concatenate([v_ref[rows, 2 * p * dv:(2 * p + 2) * dv].astype(BF16), ones_v], axis=1)
            full = _dot_tn(ke[rows, ps].astype(BF16), vp)
            upd = jnp.where(rowi < dk, full[:, :dv], full[:, dv:2 * dv])
            w_col = jnp.where(rowi < dk, w_c[j][:, 2 * p:2 * p + 1], w_c[j][:, 2 * p + 1:2 * p + 2])
            cp = w_col * cp + upd
            nb = w_col * nb + full[:, 2 * dv:]
        c_sc[p] = cp
        nb_sc[p] = nb
    for j in range(nch):
        rows = slice(j * chunk, (j + 1) * chunk)
        for p in range(pairs):
            ps = slice(p * LANES, (p + 1) * LANES)
            qp = q[rows, ps]
            qsp = qs[rows, ps]
            kpb = k[rows, ps].astype(BF16)
            state = states[j][p]
            for jj in range(2):
                h = 2 * p + jj
                hs = slice(h * dv, (h + 1) * dv)
                mine = (lane1 >= jj * dk) & (lane1 < (jj + 1) * dk)
                qh = jnp.where(mine, qp, 0.0).astype(BF16)
                qsh = jnp.where(mine, qsp, 0.0).astype(BF16)
                d = neg_m[rows, h * dv:h * dv + chunk] + bt[h:h + 1, j * chunk:(j + 1) * chunk]
                s = _dot_nt(qh, kpb) * jnp.where(causal, jnp.exp(d), 0.0)
                vh = jnp.concatenate([v_ref[rows, hs].astype(BF16), ones_v], axis=1)
                out = _dot(qsh, state) + _dot(s.astype(BF16), vh)
                hh = out[:, :dv] / jnp.maximum(jnp.abs(out[:, dv:]), em_v[rows, hs])
                hn = _rms(hh, gn_ref[h:h + 1, :])
                u_ref[rows, hs] = (_sigmoid(op_ref[rows, hs]) * hn).astype(BF16)

    @pl.when(c == nblk - 1)
    def _():
        ct_ref[0] = c_sc[...]
        for p in range(pairs):
            nt_ref[0, p:p + 1, :] = nb_sc[p].T[0:1, :]
        mt_ref[0] = m_sc[...]


def _mlstm_recurrence_blocked(z, gates, c0, n0, m0, g_norm, nseq, t, chunk, tb):
    heads = g_norm.shape[0]
    v_w = heads * g_norm.shape[1]
    qk_w = v_w // 2
    nblk = t // tb
    expand = _expand_matrix(heads, EXPAND_WIDTHS)
    kern = functools.partial(_mlstm_block_kernel, chunk=chunk, tb=tb, nblk=nblk)
    row = lambda b, c: b * nblk + c
    st4 = lambda b, c: (b, 0, 0, 0)
    st3 = lambda b, c: (b, 0, 0)
    return pl.pallas_call(
        kern,
        grid=(nseq, nblk),
        in_specs=[
            pl.BlockSpec((tb, qk_w), lambda b, c: (row(b, c), 0)),
            pl.BlockSpec((tb, qk_w), lambda b, c: (row(b, c), 1)),
            pl.BlockSpec((tb, v_w), lambda b, c: (row(b, c), 1)),
            pl.BlockSpec((tb, v_w), lambda b, c: (row(b, c), 2)),
            pl.BlockSpec((tb, LANES), lambda b, c: (row(b, c), 0)),
            pl.BlockSpec(expand.shape, lambda b, c: (0, 0)),
            pl.BlockSpec((1,) + c0.shape[1:], st4),
            pl.BlockSpec((1,) + n0.shape[1:], st3),
            pl.BlockSpec((1,) + m0.shape[1:], st3),
            pl.BlockSpec(g_norm.shape, lambda b, c: (0, 0)),
        ],
        out_specs=[
            pl.BlockSpec((tb, v_w), lambda b, c: (row(b, c), 0)),
            pl.BlockSpec((1,) + c0.shape[1:], st4),
            pl.BlockSpec((1,) + n0.shape[1:], st3),
            pl.BlockSpec((1,) + m0.shape[1:], st3),
        ],
        out_shape=[
            jax.ShapeDtypeStruct((nseq * t, v_w), BF16),
            jax.ShapeDtypeStruct(c0.shape, F32),
            jax.ShapeDtypeStruct(n0.shape, F32),
            jax.ShapeDtypeStruct(m0.shape, F32),
        ],
        scratch_shapes=[
            pltpu.VMEM(c0.shape[1:], F32),
            pltpu.VMEM(c0.shape[1:], F32),
            pltpu.VMEM(m0.shape[1:], F32),
        ],
        compiler_params=_params(("arbitrary", "arbitrary")),
        name="mlstm_recurrence_t%d" % t,
    )(z, z, z, z, gates, expand, c0, n0, m0, g_norm)


def _out_proj_kernel(up_ref, us_ref, xp_ref, xs_ref, w_ref, g_ref, cast_ref,
                     xo_ref, hn_ref, cast_out_ref, *, n_first):
    cast_out_ref[...] = cast_ref[...].astype(BF16)
    i = pl.program_id(0)
    u = _pick_rows(i, n_first, up_ref, us_ref)
    x = _pick_rows(i, n_first, xp_ref, xs_ref) + _dot(u, w_ref[...])
    xo_ref[...] = x
    hn_ref[...] = _rms(x, g_ref[...]).astype(hn_ref.dtype)


def _out_proj(u_p, u_s, x_p, x_s, w, g, cast_src):
    d = x_p.shape[1]
    n = x_p.shape[0] + x_s.shape[0]
    n_first = x_p.shape[0] // TM
    cast_in, cast_out, cast_shape = _cast_side_job(cast_src, n // TM)
    return pl.pallas_call(
        functools.partial(_out_proj_kernel, n_first=n_first),
        grid=(n // TM,),
        in_specs=_split_specs(n_first, u_p.shape[1]) + _split_specs(n_first, d) + [
            pl.BlockSpec(w.shape, lambda i: (0, 0)),
            pl.BlockSpec((1, d), lambda i: (0, 0)),
            cast_in,
        ],
        out_specs=[pl.BlockSpec((TM, d), lambda i: (i, 0)), pl.BlockSpec((TM, d), lambda i: (i, 0)),
                   cast_out],
        out_shape=[jax.ShapeDtypeStruct((n, d), F32), jax.ShapeDtypeStruct((n, d), BF16), cast_shape],
        compiler_params=_params(("arbitrary",)),
        name="out_proj",
    )(u_p, u_s, x_p, x_s, w, g, cast_src)


def _out_proj_router_kernel(up_ref, us_ref, x_ref, w_ref, g_ref, wr_ref, br_ref,
                            xo_ref, hn_ref, gate_ref, idx_ref, cnt_ref, *, n_first):
    i = pl.program_id(0)
    u = _pick_rows(i, n_first, up_ref, us_ref)
    x = x_ref[...] + _dot(u, w_ref[...])
    xo_ref[...] = x
    hn = _rms(x, g_ref[...])
    hn_hi = hn.astype(BF16)
    hn_ref[...] = hn_hi
    hn_lo = (hn - hn_hi.astype(F32)).astype(BF16)
    logits = (_dot(hn_hi, wr_ref[0]) + _dot(hn_lo, wr_ref[0]) + _dot(hn_hi, wr_ref[1])
              + br_ref[...])
    tm = logits.shape[0]
    lane = lax.broadcasted_iota(I32, logits.shape, 1)
    valid = lane < N_EXPERTS
    logits = jnp.where(valid, logits, -jnp.inf)
    ex = jnp.exp(logits - jnp.max(logits, axis=-1, keepdims=True))
    probs = jnp.where(valid, ex / jnp.sum(ex, axis=-1, keepdims=True), -1.0)
    p1 = jnp.max(probs, axis=-1, keepdims=True)
    i1 = jnp.min(jnp.where(probs == p1, lane, LANES), axis=-1, keepdims=True)
    rest = jnp.where(lane == i1, -1.0, probs)
    p2 = jnp.max(rest, axis=-1, keepdims=True)
    i2 = jnp.min(jnp.where(rest == p2, lane, LANES), axis=-1, keepdims=True)
    tot = p1 + p2
    onehot = ((lane == i1) | (lane == i2)).astype(BF16)
    rr = lax.broadcasted_iota(I32, (tm, tm), 0)
    cc = lax.broadcasted_iota(I32, (tm, tm), 1)
    strict = (rr > cc).astype(BF16)
    before = _dot(strict, onehot)
    r1 = jnp.sum(jnp.where(lane == i1, before, 0.0), axis=-1, keepdims=True)
    r2 = jnp.sum(jnp.where(lane == i2, before, 0.0), axis=-1, keepdims=True)
    gate_ref[...] = jnp.where(lane == 0, p1 / tot, jnp.where(lane == 1, p2 / tot, 0.0))
    idx_ref[...] = jnp.where(lane == 0, i1,
                             jnp.where(lane == 1, i2,
                                       jnp.where(lane == 2, r1.astype(I32),
                                                 jnp.where(lane == 3, r2.astype(I32), 0))))
    cnt_ref[0] = jnp.sum(onehot.astype(F32), axis=0, keepdims=True)


def _out_proj_router(u_p, u_s, x, w, g, w_router, b_router):
    n, d = x.shape
    n_first = u_p.shape[0] // TM
    row = lambda i: (i, 0)
    fix = lambda i: (0, 0)
    return pl.pallas_call(
        functools.partial(_out_proj_router_kernel, n_first=n_first),
        grid=(n // TM,),
        in_specs=_split_specs(n_first, u_p.shape[1]) + [
            pl.BlockSpec((TM, d), row),
            pl.BlockSpec(w.shape, fix),
            pl.BlockSpec((1, d), fix),
            pl.BlockSpec(w_router.shape, lambda i: (0, 0, 0)),
            pl.BlockSpec((1, LANES), fix),
        ],
        out_specs=[
            pl.BlockSpec((TM, d), row),
            pl.BlockSpec((TM, d), row),
            pl.BlockSpec((TM, LANES), row),
            pl.BlockSpec((TM, LANES), row),
            pl.BlockSpec((1, 1, LANES), lambda i: (i, 0, 0)),
        ],
        out_shape=[
            jax.ShapeDtypeStruct((n, d), F32),
            jax.ShapeDtypeStruct((n, d), BF16),
            jax.ShapeDtypeStruct((n, LANES), F32),
            jax.ShapeDtypeStruct((n, LANES), I32),
            jax.ShapeDtypeStruct((n // TM, 1, LANES), F32),
        ],
        compiler_params=_params(("arbitrary",)),
        name="out_proj_router",
    )(u_p, u_s, x, w, g, w_router, b_router)


def _swiglu_kernel(hn_ref, x_ref, wg_ref, wu_ref, wd_ref, g_ref, cast_ref, xo_ref, ho_ref, cast_out_ref,
                   *, nchunk):
    cast_out_ref[...] = cast_ref[...].astype(BF16)
    hn = hn_ref[...]
    width = wd_ref.shape[0] // nchunk
    x = x_ref[...]
    for c in range(nchunk):
        cs = slice(c * width, (c + 1) * width)
        gate = _dot(hn, wg_ref[:, cs])
        up = _dot(hn, wu_ref[:, cs])
        act = (gate * _sigmoid(gate) * up).astype(BF16)
        x = x + _dot(act, wd_ref[cs, :])
    xo_ref[...] = x
    ho_ref[...] = _rms(x, g_ref[...]).astype(BF16)


def _swiglu(hn, x, w_gu, w_down, g, nchunk, cast_src):
    n, d = x.shape
    dff = w_down.shape[0]
    kern = functools.partial(_swiglu_kernel, nchunk=nchunk)
    row = lambda i: (i, 0)
    cast_in, cast_out, cast_shape = _cast_side_job(cast_src, n // TM)
    return pl.pallas_call(
        kern,
        grid=(n // TM,),
        in_specs=[
            pl.BlockSpec((TM, d), row),
            pl.BlockSpec((TM, d), row),
            _resident_spec((d, dff), lambda i: (0, 0)),
            _resident_spec((d, dff), lambda i: (0, 1)),
            _resident_spec((dff, d), lambda i: (0, 0)),
            pl.BlockSpec((1, d), lambda i: (0, 0)),
            cast_in,
        ],
        out_specs=[pl.BlockSpec((TM, d), row), pl.BlockSpec((TM, d), row), cast_out],
        out_shape=[jax.ShapeDtypeStruct((n, d), F32), jax.ShapeDtypeStruct((n, d), BF16), cast_shape],
        compiler_params=_params(("arbitrary",)),
        name="dense_swiglu",
    )(hn, x, w_gu, w_gu, w_down, g, cast_src)


def _row_copy(src_ref, src_row, dst_ref, dst_row, sem):
    return pltpu.make_async_copy(src_ref.at[pl.ds(src_row, 1), :], dst_ref.at[pl.ds(dst_row, 1), :], sem)


SUBLANES = 8


def _zero_rows(zero_ref, dst_ref, dst_row, count, limit, sem, wait):
    head = jnp.minimum((-dst_row) & (SUBLANES - 1), count)
    for j in range(SUBLANES - 1):
        @pl.when(j < head)
        def _(j=j):
            copy = _row_copy(zero_ref, 0, dst_ref, dst_row + j, sem)
            copy.wait() if wait else copy.start()

    rest = count - head
    first = dst_row + head
    for bit in range(SUBLANES.bit_length() - 1, limit.bit_length() - 1):
        size = 1 << bit
        done = (rest >> (bit + 1)) << (bit + 1)

        @pl.when(((rest >> bit) & 1) == 1)
        def _(size=size, done=done):
            start = pl.multiple_of(first + done, SUBLANES)
            copy = pltpu.make_async_copy(zero_ref.at[pl.ds(0, size), :],
                                         dst_ref.at[pl.ds(start, size), :], sem)
            copy.wait() if wait else copy.start()


def _expert_kernel(te_ref, tv_ref, xs_ref, wg_ref, wu_ref, wd_ref, ys_ref, *, nchunk):
    i = pl.program_id(0)
    half = xs_ref.shape[0] // 2
    width = wg_ref.shape[2] // nchunk

    def swiglu_rows(rows):
        x = xs_ref[rows, :].astype(BF16)
        acc = None
        for c in range(nchunk):
            cs = slice(c * width, (c + 1) * width)
            gate = _dot(x, wg_ref[0, :, cs])
            up = _dot(x, wu_ref[0, :, cs])
            act = (gate * _sigmoid(gate) * up).astype(BF16)
            part = _dot(act, wd_ref[0, cs, :])
            acc = part if acc is None else acc + part
        ys_ref[rows, :] = acc.astype(BF16).astype(F32)

    @pl.when(tv_ref[i] == 2)
    def _():
        swiglu_rows(slice(None))

    @pl.when(tv_ref[i] == 1)
    def _():
        swiglu_rows(slice(0, half))
        ys_ref[half:, :] = jnp.zeros((half, ys_ref.shape[1]), F32)

    @pl.when(tv_ref[i] == 0)
    def _():
        ys_ref[...] = jnp.zeros_like(ys_ref)


def _experts(tile_expert, tile_valid, xs, w_gu, w_down):
    r, d = xs.shape
    n_tiles = r // TM_EXPERT
    dff = w_down.shape[1]
    kern = functools.partial(_expert_kernel, nchunk=EXPERT_CHUNKS)
    return pl.pallas_call(
        kern,
        grid_spec=pltpu.PrefetchScalarGridSpec(
            num_scalar_prefetch=2,
            grid=(r // TM_EXPERT,),
            in_specs=[
                pl.BlockSpec((TM_EXPERT, d), lambda i, te, tv: (jnp.minimum(i, tv[n_tiles]), 0)),
                pl.BlockSpec((1, d, dff), lambda i, te, tv: (te[i], 0, 0)),
                pl.BlockSpec((1, d, dff), lambda i, te, tv: (te[i], 0, 1)),
                pl.BlockSpec((1, dff, d), lambda i, te, tv: (te[i], 0, 0)),
            ],
            out_specs=pl.BlockSpec((TM_EXPERT, d), lambda i, te, tv: (i, 0)),
        ),
        out_shape=jax.ShapeDtypeStruct((r, d), F32),
        compiler_params=_params(("arbitrary",)),
        name="moe_experts",
    )(tile_expert, tile_valid, xs, w_gu, w_gu, w_down)


RUN_BUF_ROWS = 2 * TM + LANES


def _run_positions(idx, off_row):
    lane = lax.broadcasted_iota(I32, idx.shape, 1)
    pos = []
    for slot in range(2):
        off = jnp.sum(jnp.where(lane == idx[:, slot:slot + 1], off_row, 0.0), axis=-1, keepdims=True)
        pos.append(off.astype(I32) + idx[:, 2 + slot:3 + slot])
    return pos


def _run_copies(tab_ref, step, hbm_ref, buf_ref, slot, sem, to_hbm, wait):
    for e in range(N_EXPERTS):
        entry = (step * N_EXPERTS + e) * 3
        off, row, length = tab_ref[entry], tab_ref[entry + 1], tab_ref[entry + 2]
        for bit in range(SUBLANES.bit_length() - 1, TM.bit_length()):
            size = 1 << bit
            done = (length >> (bit + 1)) << (bit + 1)

            @pl.when(((length >> bit) & 1) == 1)
            def _(size=size, done=done, off=off, row=row):
                in_buf = buf_ref.at[slot, pl.ds(pl.multiple_of(off + done, SUBLANES), size), :]
                in_hbm = hbm_ref.at[pl.ds(pl.multiple_of(row + done, SUBLANES), size), :]
                copy = (pltpu.make_async_copy(in_buf, in_hbm, sem.at[slot]) if to_hbm
                        else pltpu.make_async_copy(in_hbm, in_buf, sem.at[slot]))
                copy.wait() if wait else copy.start()


def _run_dispatch_kernel(tab_ref, pad_ref, hn_ref, idx_ref, off_ref, xs_ref, buf, zero_sc, sem, zsem,
                         *, n_steps):
    j = pl.program_id(0)
    slot = j & 1

    @pl.when(j == 0)
    def _():
        zero_sc[...] = jnp.zeros_like(zero_sc)
        rows = zero_sc.shape[0]
        for wait in (False, True):
            for e in range(N_EXPERTS):
                _zero_rows(zero_sc, xs_ref, pad_ref[2 * e], pad_ref[2 * e + 1], 2 * rows, zsem, wait)

            def tail(t, carry, wait=wait):
                first = pl.multiple_of(pad_ref[2 * N_EXPERTS] + t * rows, SUBLANES)
                copy = pltpu.make_async_copy(zero_sc, xs_ref.at[pl.ds(first, rows), :], zsem)
                copy.wait() if wait else copy.start()
                return carry

            lax.fori_loop(0, pad_ref[2 * N_EXPERTS + 1] // rows, tail, 0)

    @pl.when(j >= 2)
    def _():
        _run_copies(tab_ref, j - 2, xs_ref, buf, slot, sem, True, True)

    pos0, pos1 = _run_positions(idx_ref[...], off_ref[0])
    lane = lax.broadcasted_iota(I32, (hn_ref.shape[0], buf.shape[1]), 1)
    sel = jnp.where((lane == pos0) | (lane == pos1), 1.0, 0.0).astype(BF16)
    buf[slot] = _dot_tn(sel, hn_ref[...])
    _run_copies(tab_ref, j, xs_ref, buf, slot, sem, True, False)

    @pl.when(j == n_steps - 1)
    def _():
        _run_copies(tab_ref, j - 1, xs_ref, buf, 1 - slot, sem, True, True)
        _run_copies(tab_ref, j, xs_ref, buf, slot, sem, True, True)


def _run_dispatch(run_table, pad_table, hn, idx, run_offsets, n_rows):
    n, d = hn.shape
    n_steps = n // TM
    assert n_steps >= 2
    return pl.pallas_call(
        functools.partial(_run_dispatch_kernel, n_steps=n_steps),
        grid_spec=pltpu.PrefetchScalarGridSpec(
            num_scalar_prefetch=2,
            grid=(n_steps,),
            in_specs=[
                pl.BlockSpec((TM, d), lambda i, *_: (i, 0)),
                pl.BlockSpec((TM, LANES), lambda i, *_: (i, 0)),
                pl.BlockSpec((1, 1, LANES), lambda i, *_: (i, 0, 0)),
            ],
            out_specs=pl.BlockSpec(memory_space=pl.ANY),
            scratch_shapes=[pltpu.VMEM((2, RUN_BUF_ROWS, d), F32),
                            pltpu.VMEM((TM_EXPERT // 2, d), F32),
                            pltpu.SemaphoreType.DMA((2,)), pltpu.SemaphoreType.DMA(())],
        ),
        out_shape=jax.ShapeDtypeStruct((n_rows, d), F32),
        compiler_params=_params(("arbitrary",)),
        name="moe_dispatch",
    )(run_table, pad_table, hn, idx, run_offsets)


def _run_combine_kernel(tab_ref, ys_ref, x_ref, gate_ref, idx_ref, off_ref, g_ref, yp_ref, ysm_ref,
                        buf, sem, *, n_first, n_steps):
    j = pl.program_id(0)
    slot = j & 1

    @pl.when(j == 0)
    def _():
        buf[...] = jnp.zeros_like(buf)
        _run_copies(tab_ref, 0, ys_ref, buf, 0, sem, False, False)

    @pl.when(j + 1 < n_steps)
    def _():
        _run_copies(tab_ref, j + 1, ys_ref, buf, 1 - slot, sem, False, False)

    _run_copies(tab_ref, j, ys_ref, buf, slot, sem, False, True)
    runs = buf[slot].astype(BF16)
    pos = _run_positions(idx_ref[...], off_ref[0])
    lane = lax.broadcasted_iota(I32, (x_ref.shape[0], buf.shape[1]), 1)
    gate = gate_ref[...]
    y = x_ref[...]
    for k in range(2):
        sel = jnp.where(lane == pos[k], 1.0, 0.0).astype(BF16)
        y = y + gate[:, k:k + 1] * _dot(sel, runs)
    y = _rms(y, g_ref[...])

    @pl.when(j < n_first)
    def _():
        yp_ref[...] = y

    @pl.when(j >= n_first)
    def _():
        ysm_ref[...] = y


def _run_combine(run_table, ys, x, gates, idx, run_offsets, g, n_p):
    n, d = x.shape
    n_first = n_p // TM
    n_steps = n // TM
    p_spec, s_spec = _split_specs(n_first, d)
    row = lambda i, *_: (i, 0)
    return pl.pallas_call(
        functools.partial(_run_combine_kernel, n_first=n_first, n_steps=n_steps),
        grid_spec=pltpu.PrefetchScalarGridSpec(
            num_scalar_prefetch=1,
            grid=(n_steps,),
            in_specs=[
                pl.BlockSpec(memory_space=pl.ANY),
                pl.BlockSpec((TM, d), row),
                pl.BlockSpec((TM, LANES), row),
                pl.BlockSpec((TM, LANES), row),
                pl.BlockSpec((1, 1, LANES), lambda i, *_: (i, 0, 0)),
                pl.BlockSpec((1, d), lambda i, *_: (0, 0)),
            ],
            out_specs=[p_spec, s_spec],
            scratch_shapes=[pltpu.VMEM((2, RUN_BUF_ROWS, d), F32), pltpu.SemaphoreType.DMA((2,))],
        ),
        out_shape=[jax.ShapeDtypeStruct((n_p, d), F32), jax.ShapeDtypeStruct((n - n_p, d), F32)],
        compiler_params=_params(("arbitrary",)),
        name="moe_combine",
    )(run_table, ys, x, gates, idx, run_offsets, g)


def _run_tables(tile_counts, n_rows):
    cnt = tile_counts[:, 0, :N_EXPERTS].astype(I32)
    run = ((cnt + SUBLANES - 1) // SUBLANES) * SUBLANES
    off = jnp.cumsum(run, axis=1) - run
    used = jnp.sum(run, axis=0)
    padded = ((used + TM_EXPERT - 1) // TM_EXPERT) * TM_EXPERT
    ends = jnp.cumsum(padded)
    starts = ends - padded
    row = starts[None, :] + jnp.cumsum(run, axis=0) - run
    run_table = jnp.stack([off, row, run], axis=2).reshape(-1)
    run_offsets = jnp.pad(off.astype(F32), ((0, 0), (0, LANES - N_EXPERTS)))[:, None, :]
    n_tiles = n_rows // TM_EXPERT
    used_tiles = ends[-1] // TM_EXPERT
    pad_table = jnp.stack([starts + used, padded - used], axis=1).reshape(-1)
    pad_table = jnp.concatenate([pad_table, jnp.stack([ends[-1], n_rows - ends[-1]])])
    tile_start = jnp.arange(n_tiles, dtype=I32) * TM_EXPERT
    tile_expert = jnp.sum((tile_start[:, None] >= ends[None, :]).astype(I32), axis=1)
    last_expert = jnp.sum((ends[-1] - 1 >= ends).astype(I32))
    tile_expert = jnp.minimum(tile_expert, last_expert)
    used_end = (starts + used)[tile_expert]
    half = TM_EXPERT // 2
    tile_rows = jnp.clip(used_end - tile_start, 0, TM_EXPERT)
    tile_valid = jnp.where(tile_start < ends[-1], (tile_rows + half - 1) // half, 0)
    tile_valid = jnp.concatenate([tile_valid, (used_tiles - 1)[None]])
    return run_table, run_offsets, pad_table, tile_expert, tile_valid


def _pad_time(a, nseq, t):
    w = a.shape[1]
    a = jnp.pad(a.reshape(nseq, t, w), ((0, 0), (0, SAMPLE_PAD_T - t), (0, 0)))
    return a.reshape(nseq * SAMPLE_PAD_T, w)


def kernel(x_prompt, x_sample, state_gla_S, state_mlstm_C, state_mlstm_n, state_mlstm_m,
           norm_mix, norm_ffn, norm_final,
           gla_w_in, gla_w_a2, gla_b_a, gla_g_norm, gla_w_out,
           mlstm_w_in, mlstm_b_gate, mlstm_g_norm, mlstm_w_out,
           ffn_w_gu, ffn_w_down,
           moe_w_router, moe_b_router, moe_w_gu, moe_w_down):
    bp, tp, d = x_prompt.shape
    bs, ts, _ = x_sample.shape
    n_p = bp * tp
    n_s = bs * ts
    n = n_p + n_s
    assert n_p % TM == 0 and n_s % TM == 0 and tp % (2 * CHUNK) == 0 and ts <= SAMPLE_PAD_T
    assert norm_mix.shape[0] == 2, "one GLA layer followed by one mLSTM layer"
    qk_w = d // 2
    main_w = 2 * qk_w + 2 * d

    x_p = x_prompt.reshape(n_p, d)
    x_s = x_sample.reshape(n_s, d)

    w_in = gla_w_in[0].T
    w_lr = jnp.pad(w_in[main_w:, :].T, ((0, 0), (0, LANES - GLA_RANK))).astype(BF16)
    w_a2 = jnp.pad(gla_w_a2[0], ((0, LANES - GLA_RANK), (0, 0))).astype(BF16)
    z, log_a, ffn_wgu = _gla_in_proj(x_p, _pad_time(x_s, bs, ts), norm_mix[0][None], w_in, main_w,
                                     w_lr, w_a2, gla_b_a[0][None], ffn_w_gu[0])
    s0_p = jnp.zeros((bp,) + state_gla_S.shape[2:], F32)
    u_p, s_p = _gla_recurrence(z, log_a, s0_p, gla_g_norm[0], bp, tp, CHUNK, 16 * CHUNK)
    u_s, s_s = _gla_recurrence_short(z, log_a, state_gla_S[0], gla_g_norm[0], bs, SAMPLE_PAD_T, ts,
                                     SAMPLE_SEQS_PER_STEP, n_p)
    x, hn, ffn_wd = _out_proj(u_p, u_s, x_p, x_s, gla_w_out[0].astype(BF16),
                              norm_ffn[0][None], ffn_w_down[0])
    x, hn, moe_wgu = _swiglu(hn, x, ffn_wgu, ffn_wd,
                             norm_mix[1][None], SWIGLU_CHUNKS,
                             moe_w_gu[0].reshape(N_EXPERTS * d, moe_w_gu.shape[3]))

    w_in = mlstm_w_in[0].T
    w_gt = jnp.pad(w_in[main_w:, :].T, ((0, 0), (0, LANES - 2 * MLSTM_HEADS))).astype(BF16)
    b_gt = jnp.pad(mlstm_b_gate[0], (0, LANES - 2 * MLSTM_HEADS))[None]
    dff_e = moe_w_down.shape[2]
    z, gates, moe_wd = _mlstm_in_proj(hn, n_p, _pad_time(hn[n_p:], bs, ts), w_in, main_w, w_gt, b_gt,
                                      moe_w_down[0].reshape(N_EXPERTS * dff_e, d))
    pairs = MLSTM_HEADS // 2
    dk2 = 2 * state_mlstm_C.shape[3]
    dvm = state_mlstm_C.shape[4]
    c0_p = jnp.zeros((bp, pairs, dk2, dvm), F32)
    n0_p = jnp.zeros((bp, pairs, dk2), F32)
    m0_p = jnp.zeros((bp, 1, LANES), F32)
    u_p, c_p, nn_p, m_p = _mlstm_recurrence_blocked(z, gates, c0_p, n0_p, m0_p,
                                                    mlstm_g_norm[0], bp, tp, CHUNK, 16 * CHUNK)
    c0_s = state_mlstm_C[0]
    n0_s = state_mlstm_n[0].reshape(bs, pairs, dk2)
    m0_s = jnp.pad(state_mlstm_m[0], ((0, 0), (0, LANES - MLSTM_HEADS)))[:, None, :]
    u_s, c_s, nn_s, m_s = _mlstm_recurrence_short(z, gates, c0_s, n0_s, m0_s, mlstm_g_norm[0], bs,
                                                  SAMPLE_PAD_T, ts, SAMPLE_SEQS_PER_STEP, n_p)

    w_r = jnp.pad(moe_w_router[0], ((0, 0), (0, LANES - N_EXPERTS)))
    w_r_hi = w_r.astype(BF16)
    w_r = jnp.stack([w_r_hi, (w_r - w_r_hi.astype(F32)).astype(BF16)])
    b_r = jnp.pad(moe_b_router[0], (0, LANES - N_EXPERTS))[None]
    x, hn, route_g, route_i, tile_counts = _out_proj_router(
        u_p, u_s, x, mlstm_w_out[0].astype(BF16), norm_ffn[1][None], w_r, b_r)
    n_rows = 2 * n + (n // TM) * N_EXPERTS * (SUBLANES - 1) + N_EXPERTS * (TM_EXPERT - 1)
    n_rows = -(-n_rows // TM_EXPERT) * TM_EXPERT
    run_table, run_offsets, pad_table, tile_expert, tile_valid = _run_tables(tile_counts, n_rows)
    xs = _run_dispatch(run_table, pad_table, hn, route_i, run_offsets, n_rows)
    ys = _experts(tile_expert, tile_valid, xs, moe_wgu.reshape(moe_w_gu.shape[1:]),
                  moe_wd.reshape(moe_w_down.shape[1:]))
    y_p, y_s = _run_combine(run_table, ys, x, route_g, route_i, run_offsets, norm_final[None], n_p)

    y_prompt = y_p.reshape(bp, tp, d)
    y_sample = y_s.reshape(bs, ts, d)
    hd = state_mlstm_C.shape[2:]
    return (y_prompt, y_sample,
            s_p[None], c_p.reshape((1, bp) + hd), nn_p.reshape(1, bp, hd[0], hd[1]),
            m_p[None, :, 0, :MLSTM_HEADS],
            s_s[None], c_s[None], nn_s.reshape(1, bs, hd[0], hd[1]), m_s[None, :, 0, :MLSTM_HEADS])


def _mlstm_in_proj(hn, n_p, hn_s, w_in, wz, w_gate, b_gate, cast_src):
    d = hn.shape[1]
    n = n_p + hn_s.shape[0]
    n_first = n_p // TM
    cast_in, cast_out, cast_shape = _cast_side_job(cast_src, n // TM)

    def kern(hp_ref, hs_ref, w_ref, wg_ref, b_ref, cast_ref, z_ref, gate_ref, cast_out_ref, wb_sc):
        @pl.when(pl.program_id(0) == 0)
        def _():
            wb_sc[...] = w_ref[...].T.astype(BF16)

        cast_out_ref[...] = cast_ref[...].astype(BF16)
        h = _pick_rows(pl.program_id(0), n_first, hp_ref, hs_ref)
        width = wz // IN_PROJ_COL_CHUNKS
        for c in range(IN_PROJ_COL_CHUNKS):
            cs = slice(c * width, (c + 1) * width)
            z_ref[:, cs] = _dot(h, wb_sc[:, cs])
        gp = _dot(h, wg_ref[...]) + b_ref[...]
        gc = GATE_CAP * jnp.tanh(gp * (1.0 / GATE_CAP))
        lane = lax.broadcasted_iota(I32, gc.shape, 1)
        out = jnp.where(lane < MLSTM_HEADS, gc, _log_sigmoid(gc))
        gate_ref[...] = jnp.where(lane < 2 * MLSTM_HEADS, out, 0.0)

    return pl.pallas_call(
        kern,
        grid=(n // TM,),
        in_specs=_split_specs(n_first, d) + [
            _resident_spec((wz, d), lambda i: (0, 0)),
            _resident_spec((d, LANES), lambda i: (0, 0)),
            pl.BlockSpec((1, LANES), lambda i: (0, 0)),
            cast_in,
        ],
        out_specs=[pl.BlockSpec((TM, wz), lambda i: (i, 0)), pl.BlockSpec((TM, LANES), lambda i: (i, 0)),
                   cast_out],
        out_shape=[jax.ShapeDtypeStruct((n, wz), F32), jax.ShapeDtypeStruct((n, LANES), F32),
                   cast_shape],
        scratch_shapes=[pltpu.VMEM((d, wz), BF16)],
        compiler_params=_params(("arbitrary",)),
        name="in_proj_mlstm",
    )(hn, hn_s, w_in, w_gate, b_gate, cast_src)
```

```python
import functools

import jax
import jax.numpy as jnp
import numpy as np
from jax import lax
from jax.experimental import pallas as pl
from jax.experimental.pallas import tpu as pltpu

F32 = jnp.float32
BF16 = jnp.bfloat16
I32 = jnp.int32

EPS = 1e-6
GLA_HEADS = 4
GLA_RANK = 16
GLA_TAU = 16.0
MLSTM_HEADS = 8
GATE_CAP = 15.0
CHUNK = 64
N_EXPERTS = 8
NEG_BIG = -1e30

LANES = 128
TM = 512
TM_EXPERT = 512
SAMPLE_PAD_T = 8
SAMPLE_SEQS_PER_STEP = 16
SWIGLU_CHUNKS = 11
EXPERT_CHUNKS = 7
VMEM_LIMIT = 56 * 1024 * 1024


def _dot(a, b):
    return jnp.dot(a, b, preferred_element_type=F32)


def _dot_nt(a, b):
    return lax.dot_general(a, b, (((1,), (1,)), ((), ())), preferred_element_type=F32)


def _dot_tn(a, b):
    return lax.dot_general(a, b, (((0,), (0,)), ((), ())), preferred_element_type=F32)


def _sigmoid(x):
    return 1.0 / (1.0 + jnp.exp(-x))


def _log_sigmoid(x):
    return jnp.minimum(x, 0.0) - jnp.log(1.0 + jnp.exp(-jnp.abs(x)))


def _rms(x, g):
    return x * lax.rsqrt(jnp.mean(x * x, axis=-1, keepdims=True) + EPS) * g


def _cumsum_rows(x, n):
    ridx = lax.broadcasted_iota(I32, x.shape, 0)
    s = 1
    while s < n:
        x = x + jnp.where(ridx >= s, pltpu.roll(x, s, 0), 0.0)
        s *= 2
    return x


def _params(sem):
    return pltpu.CompilerParams(dimension_semantics=sem, vmem_limit_bytes=VMEM_LIMIT)


def _resident_spec(block_shape, index_map):
    return pl.BlockSpec(block_shape, index_map, pipeline_mode=pl.Buffered(1))


def _cast_side_job(src, n_steps):
    rows, cols = src.shape
    n_slabs = max(s for s in range(1, n_steps + 1) if rows % (16 * s) == 0)
    spec = pl.BlockSpec((rows // n_slabs, cols), lambda i, *_: (jnp.minimum(i, n_slabs - 1), 0))
    return spec, spec, jax.ShapeDtypeStruct(src.shape, BF16)


IN_PROJ_COL_CHUNKS = 4
ROW_PARTS = 2


def _pick_rows(i, n_first, first_ref, second_ref):
    return jnp.where(i < n_first, first_ref[...], second_ref[...])


def _split_specs(n_first, width):
    return [pl.BlockSpec((TM, width), lambda i, *_: (jnp.minimum(i, n_first - 1), 0)),
            pl.BlockSpec((TM, width), lambda i, *_: (jnp.maximum(i - n_first, 0), 0))]


def _gate_weights(wg_ref):
    rows, d = wg_ref.shape
    full = jnp.concatenate([wg_ref[...], jnp.zeros((LANES - rows, d), F32)], axis=0)
    return full.T.astype(BF16)


def _gla_in_proj_kernel(xp_ref, xs_ref, g_ref, w_ref, wg_ref, w2_ref, b_ref, cast_ref,
                        z_ref, gate_ref, cast_out_ref, wb_sc, wg_sc, *, n_first):
    @pl.when(pl.program_id(0) == 0)
    def _():
        wb_sc[...] = w_ref[...].T.astype(BF16)
        wg_sc[...] = _gate_weights(wg_ref)

    cast_out_ref[...] = cast_ref[...].astype(BF16)
    first = pl.program_id(0) < n_first
    width = wb_sc.shape[1] // IN_PROJ_COL_CHUNKS
    part = xp_ref.shape[0] // ROW_PARTS
    for r in range(ROW_PARTS):
        rows = slice(r * part, (r + 1) * part)
        x = jnp.where(first, xp_ref[rows, :], xs_ref[rows, :])
        hn = _rms(x, g_ref[...]).astype(BF16)
        for c in range(IN_PROJ_COL_CHUNKS):
            cs = slice(c * width, (c + 1) * width)
            z_ref[rows, cs] = _dot(hn, wb_sc[:, cs])
        a = _dot(hn, wg_sc[...])
        la = _dot(a.astype(BF16), w2_ref[...]) + b_ref[...]
        gate_ref[rows, :] = _log_sigmoid(la) * (1.0 / GLA_TAU)


def _gate_rows_spec(w_in, wz):
    rows = w_in.shape[0] - wz
    assert wz % rows == 0 and rows % SUBLANES == 0
    return pl.BlockSpec((rows, w_in.shape[1]), lambda i: (wz // rows, 0))


def _gla_in_proj(x_p, x_s, g, w_in, wz, w2, b, cast_src):
    d = x_p.shape[1]
    n = x_p.shape[0] + x_s.shape[0]
    n_first = x_p.shape[0] // TM
    gw = b.shape[1]
    cast_in, cast_out, cast_shape = _cast_side_job(cast_src, n // TM)
    return pl.pallas_call(
        functools.partial(_gla_in_proj_kernel, n_first=n_first),
        grid=(n // TM,),
        in_specs=_split_specs(n_first, d) + [
            pl.BlockSpec((1, d), lambda i: (0, 0)),
            _resident_spec((wz, d), lambda i: (0, 0)),
            _gate_rows_spec(w_in, wz),
            pl.BlockSpec(w2.shape, lambda i: (0, 0)),
            pl.BlockSpec((1, gw), lambda i: (0, 0)),
            cast_in,
        ],
        out_specs=[
            pl.BlockSpec((TM, wz), lambda i: (i, 0)),
            pl.BlockSpec((TM, gw), lambda i: (i, 0)),
            cast_out,
        ],
        out_shape=[jax.ShapeDtypeStruct((n, wz), F32), jax.ShapeDtypeStruct((n, gw), F32), cast_shape],
        scratch_shapes=[pltpu.VMEM((d, wz), BF16), pltpu.VMEM((d, LANES), BF16)],
        compiler_params=_params(("arbitrary",)),
        name="in_proj_gla",
    )(x_p, x_s, g, w_in, w_in, w2, b, cast_src)


def _gla_kernel(q_ref, k_ref, v_ref, r_ref, la_ref, s0_ref, gn_ref, u_ref, st_ref, st_sc,
                *, chunk, tb, nblk):
    heads = GLA_HEADS
    dk = q_ref.shape[1] // heads
    dv = v_ref.shape[1] // heads
    c = pl.program_id(1)

    @pl.when(c == 0)
    def _():
        for h in range(heads):
            st_sc[h] = s0_ref[0, h].T

    ri = lax.broadcasted_iota(I32, (chunk, chunk), 0)
    ci = lax.broadcasted_iota(I32, (chunk, chunk), 1)
    causal = ri >= ci
    kscale = dk ** -0.5
    state = [st_sc[h] for h in range(heads)]
    for j in range(tb // chunk):
        rows = slice(j * chunk, (j + 1) * chunk)
        b_all = _cumsum_rows(la_ref[rows, :], chunk)
        for h in range(heads):
            ks = slice(h * dk, (h + 1) * dk)
            vs = slice(h * dv, (h + 1) * dv)
            b = b_all[:, ks]
            b_end = b[chunk - 1:chunk, :]
            q = q_ref[rows, ks]
            k = k_ref[rows, ks] * kscale
            qg = (q * jnp.exp(b)).astype(BF16)
            kg = (k * jnp.exp(-b)).astype(BF16)
            ke = (k * jnp.exp(b_end - b)).astype(BF16)
            v = v_ref[rows, vs].astype(BF16)
            a = jnp.where(causal, _dot_nt(qg, kg), 0.0).astype(BF16)
            o = _dot_nt(qg, state[h].astype(BF16)) + _dot(a, v)
            state[h] = state[h] * jnp.exp(b_end) + _dot_tn(v, ke)
            on = _rms(o, gn_ref[h:h + 1, :])
            r = r_ref[rows, vs]
            u_ref[rows, vs] = (r * _sigmoid(r) * on).astype(BF16)
    for h in range(heads):
        st_sc[h] = state[h]

    @pl.when(c == nblk - 1)
    def _():
        for h in range(heads):
            st_ref[0, h] = st_sc[h].T


def _gla_recurrence(z, log_a, s0, g_norm, nseq, t, chunk, tb):
    qk_w = log_a.shape[1]
    v_w = 2 * qk_w
    nblk = t // tb
    heads, dk, dv = s0.shape[1:]
    kern = functools.partial(_gla_kernel, chunk=chunk, tb=tb, nblk=nblk)
    row = lambda b, c: b * nblk + c
    return pl.pallas_call(
        kern,
        grid=(nseq, nblk),
        in_specs=[
            pl.BlockSpec((tb, qk_w), lambda b, c: (row(b, c), 0)),
            pl.BlockSpec((tb, qk_w), lambda b, c: (row(b, c), 1)),
            pl.BlockSpec((tb, v_w), lambda b, c: (row(b, c), 1)),
            pl.BlockSpec((tb, v_w), lambda b, c: (row(b, c), 2)),
            pl.BlockSpec((tb, qk_w), lambda b, c: (row(b, c), 0)),
            pl.BlockSpec((1, heads, dk, dv), lambda b, c: (b, 0, 0, 0)),
            pl.BlockSpec((heads, dv), lambda b, c: (0, 0)),
        ],
        out_specs=[
            pl.BlockSpec((tb, v_w), lambda b, c: (row(b, c), 0)),
            pl.BlockSpec((1, heads, dk, dv), lambda b, c: (b, 0, 0, 0)),
        ],
        out_shape=[
            jax.ShapeDtypeStruct((nseq * t, v_w), BF16),
            jax.ShapeDtypeStruct(s0.shape, F32),
        ],
        scratch_shapes=[pltpu.VMEM((heads, dv, dk), F32)],
        compiler_params=_params(("arbitrary", "arbitrary")),
        name="gla_recurrence_t%d" % t,
    )(z, z, z, z, log_a, s0, g_norm)


def _real_rows_selector(nsq, t, t_real):
    i = lax.broadcasted_iota(I32, (nsq * t_real, nsq * t), 0)
    r = lax.broadcasted_iota(I32, (nsq * t_real, nsq * t), 1)
    return jnp.where(r == (i // t_real) * t + i % t_real, 1.0, 0.0).astype(BF16)


def _gla_short_kernel(q_ref, k_ref, v_ref, r_ref, la_ref, s0_ref, gn_ref, u_ref, st_ref,
                      *, t, t_real, nsq):
    heads = GLA_HEADS
    dk = q_ref.shape[1] // heads
    dv = v_ref.shape[1] // heads
    nrow = nsq * t
    width = q_ref.shape[1]
    rc = lax.broadcasted_iota(I32, (nrow, width), 0) & (t - 1)
    log_a = jnp.where(rc < t_real, la_ref[...], 0.0)
    b = _chunk_scan(log_a, rc, t, jnp.add, 0.0)
    pick = _real_rows_selector(nsq, t, t_real)
    b_last = b.reshape(nsq, t, width)[:, t - 1:t, :]
    b_end = jnp.broadcast_to(b_last, (nsq, t, width)).reshape(nrow, width)
    k = k_ref[...] * (dk ** -0.5)
    qg = q_ref[...] * jnp.exp(b)
    kg = k * jnp.exp(-b)
    ke = k * jnp.exp(b_end - b)
    decay = jnp.exp(b_end)
    ri = lax.broadcasted_iota(I32, (nrow, nrow), 0)
    ci = lax.broadcasted_iota(I32, (nrow, nrow), 1)
    shift = t.bit_length() - 1
    mask = ((ri >> shift) == (ci >> shift)) & (ri >= ci)
    for h in range(heads):
        ks = slice(h * dk, (h + 1) * dk)
        vs = slice(h * dv, (h + 1) * dv)
        a = jnp.where(mask, _dot_nt(qg[:, ks].astype(BF16), kg[:, ks].astype(BF16)), 0.0)
        o = _dot(a.astype(BF16), v_ref[:, vs].astype(BF16))
        decay_t = decay[:, ks].T
        o_state = []
        for sq in range(nsq):
            r = slice(sq * t, (sq + 1) * t)
            s0 = s0_ref[sq, h]
            o_state.append(_dot(qg[r, ks].astype(BF16), s0.astype(BF16)))
            upd = _dot_tn(ke[r, ks].astype(BF16), v_ref[r, vs].astype(BF16))
            st_ref[sq, h] = decay_t[:, sq * t:sq * t + 1] * s0 + upd
        on = _rms(o + jnp.concatenate(o_state, axis=0), gn_ref[h:h + 1, :])
        rg = r_ref[:, vs]
        gated = (rg * _sigmoid(rg) * on).astype(BF16)
        u_ref[:, vs] = _dot(pick, gated).astype(BF16)


def _gla_recurrence_short(z, log_a, s0, g_norm, nseq, t, t_real, nsq, row0):
    qk_w = log_a.shape[1]
    v_w = 2 * qk_w
    heads, dk, dv = s0.shape[1:]
    assert nsq * t == dk == LANES and row0 % (nsq * t) == 0
    rb = nsq * t
    blk0 = row0 // rb
    return pl.pallas_call(
        functools.partial(_gla_short_kernel, t=t, t_real=t_real, nsq=nsq),
        grid=(nseq // nsq,),
        in_specs=[
            pl.BlockSpec((rb, qk_w), lambda b: (blk0 + b, 0)),
            pl.BlockSpec((rb, qk_w), lambda b: (blk0 + b, 1)),
            pl.BlockSpec((rb, v_w), lambda b: (blk0 + b, 1)),
            pl.BlockSpec((rb, v_w), lambda b: (blk0 + b, 2)),
            pl.BlockSpec((rb, qk_w), lambda b: (blk0 + b, 0)),
            pl.BlockSpec((nsq, heads, dk, dv), lambda b: (b, 0, 0, 0)),
            pl.BlockSpec((heads, dv), lambda b: (0, 0)),
        ],
        out_specs=[
            pl.BlockSpec((nsq * t_real, v_w), lambda b: (b, 0)),
            pl.BlockSpec((nsq, heads, dk, dv), lambda b: (b, 0, 0, 0)),
        ],
        out_shape=[
            jax.ShapeDtypeStruct((nseq * t_real, v_w), BF16),
            jax.ShapeDtypeStruct(s0.shape, F32),
        ],
        compiler_params=_params(("arbitrary",)),
        name="gla_recurrence_t%d" % t,
    )(z, z, z, z, log_a, s0, g_norm)


def _mlstm_short_kernel(q_ref, k_ref, v_ref, op_ref, gt_ref, e_ref, c0_ref, n0_ref, m0_ref, gn_ref,
                        u_ref, ct_ref, nt_ref, mt_ref, *, t, t_real, nsq):
    heads = MLSTM_HEADS
    pairs = heads // 2
    dv = v_ref.shape[1] // heads
    dk = q_ref.shape[1] // heads
    nrow = nsq * t

    def per_seq(x3):
        return jnp.broadcast_to(x3, (nsq, t, x3.shape[2])).reshape(nrow, x3.shape[2])

    def last(x):
        return x.reshape(nsq, t, x.shape[1])[:, t - 1:t, :]

    rc = lax.broadcasted_iota(I32, (nrow, LANES), 0) & (t - 1)
    lane_g = lax.broadcasted_iota(I32, (nrow, LANES), 1)
    gts = jnp.where(rc < t_real, gt_ref[...],
                    jnp.where(lane_g < heads, NEG_BIG, 0.0))
    pick = _real_rows_selector(nsq, t, t_real)
    f_cum = pltpu.roll(_chunk_scan(gts, rc, t, jnp.add, 0.0), LANES - heads, 1)
    b = gts - f_cum
    cmb = _chunk_scan(b, rc, t, jnp.maximum, NEG_BIG)
    mp = per_seq(m0_ref[...])
    big_m = jnp.maximum(mp, cmb)
    m_t = f_cum + big_m
    m_new = last(m_t)
    mt_ref[...] = m_new
    mn = per_seq(m_new)
    fe = per_seq(last(f_cum))
    w_i = jnp.exp(mp - big_m)
    em = jnp.exp(-m_t)
    k_sc = jnp.exp(fe - f_cum + gts - mn)
    w_c = jnp.exp(fe + mp - mn)
    ex = _dot(_pack_terms([w_i, k_sc, -big_m, em, w_c], heads), e_ref[...])
    offs = np.cumsum([0] + [heads * w for w in SHORT_EXPAND_WIDTHS])
    q = q_ref[...]
    k = k_ref[...] * (dk ** -0.5)
    qs = q * ex[:, offs[0]:offs[1]]
    ke = k * ex[:, offs[1]:offs[2]]
    neg_m = ex[:, offs[2]:offs[3]]
    em_v = ex[:, offs[3]:offs[4]]
    wc_k = ex[:, offs[4]:offs[5]]
    bt = b.T

    ri = lax.broadcasted_iota(I32, (nrow, nrow), 0)
    ci = lax.broadcasted_iota(I32, (nrow, nrow), 1)
    shift = t.bit_length() - 1
    mask = ((ri >> shift) == (ci >> shift)) & (ri >= ci)
    lane1 = lax.broadcasted_iota(I32, (1, LANES), 1)
    rowi = lax.broadcasted_iota(I32, (LANES, LANES), 0)
    ones_v = jnp.ones((nrow, dv), BF16)
    ones_k = jnp.ones((LANES, dv), BF16)
    for p in range(pairs):
        ps = slice(p * LANES, (p + 1) * LANES)
        qp = q[:, ps]
        qsp = qs[:, ps]
        kpb = k[:, ps].astype(BF16)
        n_rows = per_seq(n0_ref[:, p:p + 1, :])
        intra, den_state, qsh = [], [], []
        for jj in range(2):
            h = 2 * p + jj
            hs = slice(h * dv, (h + 1) * dv)
            mine = (lane1 >= jj * dk) & (lane1 < (jj + 1) * dk)
            qh = jnp.where(mine, qp, 0.0).astype(BF16)
            qsh.append(jnp.where(mine, qsp, 0.0))
            d = neg_m[:, hs] + bt[h:h + 1, :]
            s = _dot_nt(qh, kpb) * jnp.where(mask, jnp.exp(d), 0.0)
            vh = jnp.concatenate([v_ref[:, hs].astype(BF16), ones_v], axis=1)
            intra.append(_dot(s.astype(BF16), vh))
            den_state.append(_dot((qsh[jj] * n_rows).astype(BF16), ones_k))
        num_state = [[], []]
        for sq in range(nsq):
            r = slice(sq * t, (sq + 1) * t)
            lhs = jnp.concatenate([qsh[0][r], qsh[1][r]], axis=0).astype(BF16)
            c_pair = jnp.concatenate([c0_ref[sq, 2 * p], c0_ref[sq, 2 * p + 1]], axis=0)
            res = _dot(lhs, c_pair.astype(BF16))
            num_state[0].append(res[:t])
            num_state[1].append(res[t:])
        for jj in range(2):
            h = 2 * p + jj
            hs = slice(h * dv, (h + 1) * dv)
            num = intra[jj][:, :dv] + jnp.concatenate(num_state[jj], axis=0)
            den = intra[jj][:, dv:] + den_state[jj]
            hh = num / jnp.maximum(jnp.abs(den), em_v[:, hs])
            hn = _rms(hh, gn_ref[h:h + 1, :])
            gated = (_sigmoid(op_ref[:, hs]) * hn).astype(BF16)
            u_ref[:, hs] = _dot(pick, gated).astype(BF16)
        kep = ke[:, ps]
        for sq in range(nsq):
            r = slice(sq * t, (sq + 1) * t)
            vp = v_ref[r, 2 * p * dv:(2 * p + 2) * dv].astype(BF16)
            full = _dot_tn(kep[r].astype(BF16), vp)
            upd = jnp.where(rowi < dk, full[:, :dv], full[:, dv:])
            w_row = w_c[sq * t:sq * t + 1, :]
            w_col = jnp.where(rowi < dk, w_row[:, 2 * p:2 * p + 1], w_row[:, 2 * p + 1:2 * p + 2])
            c_pair = jnp.concatenate([c0_ref[sq, 2 * p], c0_ref[sq, 2 * p + 1]], axis=0)
            c_new = w_col * c_pair + upd
            ct_ref[sq, 2 * p] = c_new[:dk]
            ct_ref[sq, 2 * p + 1] = c_new[dk:]
        k_sum = jnp.sum(kep.reshape(nsq, t, LANES), axis=1, keepdims=True)
        nt_ref[:, p:p + 1, :] = last(wc_k[:, ps]) * n0_ref[:, p:p + 1, :] + k_sum


def _mlstm_recurrence_short(z, gates, c0, n0, m0, g_norm, nseq, t, t_real, nsq, row0):
    heads = g_norm.shape[0]
    v_w = heads * g_norm.shape[1]
    qk_w = v_w // 2
    assert nsq * t == LANES and row0 % (nsq * t) == 0
    expand = _expand_matrix(heads, SHORT_EXPAND_WIDTHS)
    kern = functools.partial(_mlstm_short_kernel, t=t, t_real=t_real, nsq=nsq)
    rb = nsq * t
    blk0 = row0 // rb
    st4 = lambda b: (b, 0, 0, 0)
    st3 = lambda b: (b, 0, 0)
    return pl.pallas_call(
        kern,
        grid=(nseq // nsq,),
        in_specs=[
            pl.BlockSpec((rb, qk_w), lambda b: (blk0 + b, 0)),
            pl.BlockSpec((rb, qk_w), lambda b: (blk0 + b, 1)),
            pl.BlockSpec((rb, v_w), lambda b: (blk0 + b, 1)),
            pl.BlockSpec((rb, v_w), lambda b: (blk0 + b, 2)),
            pl.BlockSpec((rb, LANES), lambda b: (blk0 + b, 0)),
            pl.BlockSpec(expand.shape, lambda b: (0, 0)),
            pl.BlockSpec((nsq,) + c0.shape[1:], st4),
            pl.BlockSpec((nsq,) + n0.shape[1:], st3),
            pl.BlockSpec((nsq,) + m0.shape[1:], st3),
            pl.BlockSpec(g_norm.shape, lambda b: (0, 0)),
        ],
        out_specs=[
            pl.BlockSpec((nsq * t_real, v_w), lambda b: (b, 0)),
            pl.BlockSpec((nsq,) + c0.shape[1:], st4),
            pl.BlockSpec((nsq,) + n0.shape[1:], st3),
            pl.BlockSpec((nsq,) + m0.shape[1:], st3),
        ],
        out_shape=[
            jax.ShapeDtypeStruct((nseq * t_real, v_w), BF16),
            jax.ShapeDtypeStruct(c0.shape, F32),
            jax.ShapeDtypeStruct(n0.shape, F32),
            jax.ShapeDtypeStruct(m0.shape, F32),
        ],
        compiler_params=_params(("arbitrary",)),
        name="mlstm_recurrence_t%d" % t,
    )(z, z, z, z, gates, expand, c0, n0, m0, g_norm)


N_SPLIT = 3
EXPAND_WIDTHS = (64, 64, 128, 128)
SHORT_EXPAND_WIDTHS = EXPAND_WIDTHS + (64,)


def _expand_matrix(heads, widths):
    assert len(widths) * N_SPLIT * heads <= LANES
    cols = []
    for qi, width in enumerate(widths):
        sel = np.zeros((LANES, heads * width), np.float32)
        for term in range(N_SPLIT):
            for h in range(heads):
                sel[(qi * N_SPLIT + term) * heads + h, h * width:(h + 1) * width] = 1.0
        cols.append(sel)
    return jnp.asarray(np.concatenate(cols, axis=1), BF16)


def _chunk_scan(x, rc, n, op, fill):
    s = 1
    while s < n:
        x = op(x, jnp.where(rc >= s, pltpu.roll(x, s, 0), fill))
        s *= 2
    return x


def _pack_terms(values, heads):
    lane = lax.broadcasted_iota(I32, values[0].shape, 1)
    packed = jnp.zeros(values[0].shape, F32)
    slot = 0
    for val in values:
        rest = val
        for term in range(N_SPLIT):
            part = rest.astype(BF16).astype(F32)
            rest = rest - part
            moved = part if slot == 0 else pltpu.roll(part, slot * heads, 1)
            packed = jnp.where((lane >= slot * heads) & (lane < (slot + 1) * heads), moved, packed)
            slot += 1
    return packed.astype(BF16)


def _mlstm_block_kernel(q_ref, k_ref, v_ref, op_ref, gt_ref, e_ref, c0_ref, n0_ref, m0_ref, gn_ref,
                        u_ref, ct_ref, nt_ref, mt_ref, c_sc, nb_sc, m_sc, *, chunk, tb, nblk):
    heads = MLSTM_HEADS
    pairs = heads // 2
    dv = v_ref.shape[1] // heads
    dk = q_ref.shape[1] // heads
    c = pl.program_id(1)

    @pl.when(c == 0)
    def _():
        c_sc[...] = c0_ref[0]
        for p in range(pairs):
            nb_sc[p] = jnp.broadcast_to(n0_ref[0, p:p + 1, :], (LANES, LANES)).T
        m_sc[...] = m0_ref[0]

    nch = tb // chunk
    rc = lax.broadcasted_iota(I32, (tb, LANES), 0) & (chunk - 1)
    gts = gt_ref[...]
    f_cum = pltpu.roll(_chunk_scan(gts, rc, chunk, jnp.add, 0.0), LANES - heads, 1)
    b = gts - f_cum
    cmb = _chunk_scan(b, rc, chunk, jnp.maximum, NEG_BIG)
    m_prev = m_sc[...]
    mp_rows, mn_rows, fe_rows, w_c = [], [], [], []
    for j in range(nch):
        last = (j + 1) * chunk - 1
        f_end = f_cum[last:last + 1, :]
        m_new = f_end + jnp.maximum(m_prev, cmb[last:last + 1, :])
        w_c.append(jnp.exp(f_end + m_prev - m_new))
        mp_rows.append(jnp.broadcast_to(m_prev, (chunk, LANES)))
        mn_rows.append(jnp.broadcast_to(m_new, (chunk, LANES)))
        fe_rows.append(jnp.broadcast_to(f_end, (chunk, LANES)))
        m_prev = m_new
    m_sc[...] = m_prev
    mp = jnp.concatenate(mp_rows, axis=0)
    mn = jnp.concatenate(mn_rows, axis=0)
    fe = jnp.concatenate(fe_rows, axis=0)
    big_m = jnp.maximum(mp, cmb)
    w_i = jnp.exp(mp - big_m)
    em = jnp.exp(-(f_cum + big_m))
    k_sc = jnp.exp(fe - f_cum + gts - mn)
    ex = _dot(_pack_terms([w_i, k_sc, -big_m, em], heads), e_ref[...])
    o1 = heads * EXPAND_WIDTHS[0]
    o2 = o1 + heads * EXPAND_WIDTHS[1]
    o3 = o2 + heads * EXPAND_WIDTHS[2]
    q = q_ref[...]
    k = k_ref[...] * (dk ** -0.5)
    qs = q * ex[:, :o1]
    ke = k * ex[:, o1:o2]
    neg_m = ex[:, o2:o3]
    em_v = ex[:, o3:]
    bt = b.T

    ri = lax.broadcasted_iota(I32, (chunk, chunk), 0)
    ci = lax.broadcasted_iota(I32, (chunk, chunk), 1)
    causal = ri >= ci
    lane1 = lax.broadcasted_iota(I32, (1, LANES), 1)
    rowi = lax.broadcasted_iota(I32, (LANES, LANES), 0)
    ones_v = jnp.ones((chunk, dv), BF16)
    states = [[None] * pairs for _ in range(nch)]
    for p in range(pairs):
        ps = slice(p * LANES, (p + 1) * LANES)
        cp = c_sc[p]
        nb = nb_sc[p]
        for j in range(nch):
            rows = slice(j * chunk, (j + 1) * chunk)
            states[j][p] = jnp.concatenate([cp, nb], axis=1).astype(BF16)
            vp = jnp.concatenate([v_ref[rows, 2 * p * dv:(2 * p + 2) * dv].astype(BF16), ones_v], axis=1)
            full = _dot_tn(ke[rows, ps].astype(BF16), vp)
            upd = jnp.where(rowi < dk, full[:, :dv], full[:, dv:2 * dv])
            w_col = jnp.where(rowi < dk, w_c[j][:, 2 * p:2 * p + 1], w_c[j][:, 2 * p + 1:2 * p + 2])
            cp = w_col * cp + upd
            nb = w_col * nb + full[:, 2 * dv:]
        c_sc[p] = cp
        nb_sc[p] = nb
    for j in range(nch):
        rows = slice(j * chunk, (j + 1) * chunk)
        for p in range(pairs):
            ps = slice(p * LANES, (p + 1) * LANES)
            qp = q[rows, ps]
            qsp = qs[rows, ps]
            kpb = k[rows, ps].astype(BF16)
            state = states[j][p]
            for jj in range(2):
                h = 2 * p + jj
                hs = slice(h * dv, (h + 1) * dv)
                mine = (lane1 >= jj * dk) & (lane1 < (jj + 1) * dk)
                qh = jnp.where(mine, qp, 0.0).astype(BF16)
                qsh = jnp.where(mine, qsp, 0.0).astype(BF16)
                d = neg_m[rows, h * dv:h * dv + chunk] + bt[h:h + 1, j * chunk:(j + 1) * chunk]
                s = _dot_nt(qh, kpb) * jnp.where(causal, jnp.exp(d), 0.0)
                vh = jnp.concatenate([v_ref[rows, hs].astype(BF16), ones_v], axis=1)
                out = _dot(qsh, state) + _dot(s.astype(BF16), vh)
                hh = out[:, :dv] / jnp.maximum(jnp.abs(out[:, dv:]), em_v[rows, hs])
                hn = _rms(hh, gn_ref[h:h + 1, :])
                u_ref[rows, hs] = (_sigmoid(op_ref[rows, hs]) * hn).astype(BF16)

    @pl.when(c == nblk - 1)
    def _():
        ct_ref[0] = c_sc[...]
        for p in range(pairs):
            nt_ref[0, p:p + 1, :] = nb_sc[p].T[0:1, :]
        mt_ref[0] = m_sc[...]


def _mlstm_recurrence_blocked(z, gates, c0, n0, m0, g_norm, nseq, t, chunk, tb):
    heads = g_norm.shape[0]
    v_w = heads * g_norm.shape[1]
    qk_w = v_w // 2
    nblk = t // tb
    expand = _expand_matrix(heads, EXPAND_WIDTHS)
    kern = functools.partial(_mlstm_block_kernel, chunk=chunk, tb=tb, nblk=nblk)
    row = lambda b, c: b * nblk + c
    st4 = lambda b, c: (b, 0, 0, 0)
    st3 = lambda b, c: (b, 0, 0)
    return pl.pallas_call(
        kern,
        grid=(nseq, nblk),
        in_specs=[
            pl.BlockSpec((tb, qk_w), lambda b, c: (row(b, c), 0)),
            pl.BlockSpec((tb, qk_w), lambda b, c: (row(b, c), 1)),
            pl.BlockSpec((tb, v_w), lambda b, c: (row(b, c), 1)),
            pl.BlockSpec((tb, v_w), lambda b, c: (row(b, c), 2)),
            pl.BlockSpec((tb, LANES), lambda b, c: (row(b, c), 0)),
            pl.BlockSpec(expand.shape, lambda b, c: (0, 0)),
            pl.BlockSpec((1,) + c0.shape[1:], st4),
            pl.BlockSpec((1,) + n0.shape[1:], st3),
            pl.BlockSpec((1,) + m0.shape[1:], st3),
            pl.BlockSpec(g_norm.shape, lambda b, c: (0, 0)),
        ],
        out_specs=[
            pl.BlockSpec((tb, v_w), lambda b, c: (row(b, c), 0)),
            pl.BlockSpec((1,) + c0.shape[1:], st4),
            pl.BlockSpec((1,) + n0.shape[1:], st3),
            pl.BlockSpec((1,) + m0.shape[1:], st3),
        ],
        out_shape=[
            jax.ShapeDtypeStruct((nseq * t, v_w), BF16),
            jax.ShapeDtypeStruct(c0.shape, F32),
            jax.ShapeDtypeStruct(n0.shape, F32),
            jax.ShapeDtypeStruct(m0.shape, F32),
        ],
        scratch_shapes=[
            pltpu.VMEM(c0.shape[1:], F32),
            pltpu.VMEM(c0.shape[1:], F32),
            pltpu.VMEM(m0.shape[1:], F32),
        ],
        compiler_params=_params(("arbitrary", "arbitrary")),
        name="mlstm_recurrence_t%d" % t,
    )(z, z, z, z, gates, expand, c0, n0, m0, g_norm)


def _out_proj_kernel(up_ref, us_ref, xp_ref, xs_ref, w_ref, g_ref, cast_ref,
                     xo_ref, hn_ref, cast_out_ref, *, n_first):
    cast_out_ref[...] = cast_ref[...].astype(BF16)
    i = pl.program_id(0)
    u = _pick_rows(i, n_first, up_ref, us_ref)
    x = _pick_rows(i, n_first, xp_ref, xs_ref) + _dot(u, w_ref[...])
    xo_ref[...] = x
    hn_ref[...] = _rms(x, g_ref[...]).astype(hn_ref.dtype)


def _out_proj(u_p, u_s, x_p, x_s, w, g, cast_src):
    d = x_p.shape[1]
    n = x_p.shape[0] + x_s.shape[0]
    n_first = x_p.shape[0] // TM
    cast_in, cast_out, cast_shape = _cast_side_job(cast_src, n // TM)
    return pl.pallas_call(
        functools.partial(_out_proj_kernel, n_first=n_first),
        grid=(n // TM,),
        in_specs=_split_specs(n_first, u_p.shape[1]) + _split_specs(n_first, d) + [
            pl.BlockSpec(w.shape, lambda i: (0, 0)),
            pl.BlockSpec((1, d), lambda i: (0, 0)),
            cast_in,
        ],
        out_specs=[pl.BlockSpec((TM, d), lambda i: (i, 0)), pl.BlockSpec((TM, d), lambda i: (i, 0)),
                   cast_out],
        out_shape=[jax.ShapeDtypeStruct((n, d), F32), jax.ShapeDtypeStruct((n, d), BF16), cast_shape],
        compiler_params=_params(("arbitrary",)),
        name="out_proj",
    )(u_p, u_s, x_p, x_s, w, g, cast_src)


def _out_proj_router_kernel(up_ref, us_ref, x_ref, w_ref, g_ref, wr_ref, br_ref,
                            xo_ref, hn_ref, gate_ref, idx_ref, cnt_ref, *, n_first):
    i = pl.program_id(0)
    u = _pick_rows(i, n_first, up_ref, us_ref)
    x = x_ref[...] + _dot(u, w_ref[...])
    xo_ref[...] = x
    hn = _rms(x, g_ref[...])
    hn_hi = hn.astype(BF16)
    hn_ref[...] = hn_hi
    hn_lo = (hn - hn_hi.astype(F32)).astype(BF16)
    logits = (_dot(hn_hi, wr_ref[0]) + _dot(hn_lo, wr_ref[0]) + _dot(hn_hi, wr_ref[1])
              + br_ref[...])
    tm = logits.shape[0]
    lane = lax.broadcasted_iota(I32, logits.shape, 1)
    valid = lane < N_EXPERTS
    logits = jnp.where(valid, logits, -jnp.inf)
    ex = jnp.exp(logits - jnp.max(logits, axis=-1, keepdims=True))
    probs = jnp.where(valid, ex / jnp.sum(ex, axis=-1, keepdims=True), -1.0)
    p1 = jnp.max(probs, axis=-1, keepdims=True)
    i1 = jnp.min(jnp.where(probs == p1, lane, LANES), axis=-1, keepdims=True)
    rest = jnp.where(lane == i1, -1.0, probs)
    p2 = jnp.max(rest, axis=-1, keepdims=True)
    i2 = jnp.min(jnp.where(rest == p2, lane, LANES), axis=-1, keepdims=True)
    tot = p1 + p2
    onehot = ((lane == i1) | (lane == i2)).astype(BF16)
    rr = lax.broadcasted_iota(I32, (tm, tm), 0)
    cc = lax.broadcasted_iota(I32, (tm, tm), 1)
    strict = (rr > cc).astype(BF16)
    before = _dot(strict, onehot)
    r1 = jnp.sum(jnp.where(lane == i1, before, 0.0), axis=-1, keepdims=True)
    r2 = jnp.sum(jnp.where(lane == i2, before, 0.0), axis=-1, keepdims=True)
    gate_ref[...] = jnp.where(lane == 0, p1 / tot, jnp.where(lane == 1, p2 / tot, 0.0))
    idx_ref[...] = jnp.where(lane == 0, i1,
                             jnp.where(lane == 1, i2,
                                       jnp.where(lane == 2, r1.astype(I32),
                                                 jnp.where(lane == 3, r2.astype(I32), 0))))
    cnt_ref[0] = jnp.sum(onehot.astype(F32), axis=0, keepdims=True)


def _out_proj_router(u_p, u_s, x, w, g, w_router, b_router):
    n, d = x.shape
    n_first = u_p.shape[0] // TM
    row = lambda i: (i, 0)
    fix = lambda i: (0, 0)
    return pl.pallas_call(
        functools.partial(_out_proj_router_kernel, n_first=n_first),
        grid=(n // TM,),
        in_specs=_split_specs(n_first, u_p.shape[1]) + [
            pl.BlockSpec((TM, d), row),
            pl.BlockSpec(w.shape, fix),
            pl.BlockSpec((1, d), fix),
            pl.BlockSpec(w_router.shape, lambda i: (0, 0, 0)),
            pl.BlockSpec((1, LANES), fix),
        ],
        out_specs=[
            pl.BlockSpec((TM, d), row),
            pl.BlockSpec((TM, d), row),
            pl.BlockSpec((TM, LANES), row),
            pl.BlockSpec((TM, LANES), row),
            pl.BlockSpec((1, 1, LANES), lambda i: (i, 0, 0)),
        ],
        out_shape=[
            jax.ShapeDtypeStruct((n, d), F32),
            jax.ShapeDtypeStruct((n, d), BF16),
            jax.ShapeDtypeStruct((n, LANES), F32),
            jax.ShapeDtypeStruct((n, LANES), I32),
            jax.ShapeDtypeStruct((n // TM, 1, LANES), F32),
        ],
        compiler_params=_params(("arbitrary",)),
        name="out_proj_router",
    )(u_p, u_s, x, w, g, w_router, b_router)


def _swiglu_kernel(hn_ref, x_ref, wg_ref, wu_ref, wd_ref, g_ref, cast_ref, xo_ref, ho_ref, cast_out_ref,
                   *, nchunk):
    cast_out_ref[...] = cast_ref[...].astype(BF16)
    hn = hn_ref[...]
    width = wd_ref.shape[0] // nchunk
    x = x_ref[...]
    for c in range(nchunk):
        cs = slice(c * width, (c + 1) * width)
        gate = _dot(hn, wg_ref[:, cs])
        up = _dot(hn, wu_ref[:, cs])
        act = (gate * _sigmoid(gate) * up).astype(BF16)
        x = x + _dot(act, wd_ref[cs, :])
    xo_ref[...] = x
    ho_ref[...] = _rms(x, g_ref[...]).astype(BF16)


def _swiglu(hn, x, w_gu, w_down, g, nchunk, cast_src):
    n, d = x.shape
    dff = w_down.shape[0]
    kern = functools.partial(_swiglu_kernel, nchunk=nchunk)
    row = lambda i: (i, 0)
    cast_in, cast_out, cast_shape = _cast_side_job(cast_src, n // TM)
    return pl.pallas_call(
        kern,
        grid=(n // TM,),
        in_specs=[
            pl.BlockSpec((TM, d), row),
            pl.BlockSpec((TM, d), row),
            _resident_spec((d, dff), lambda i: (0, 0)),
            _resident_spec((d, dff), lambda i: (0, 1)),
            _resident_spec((dff, d), lambda i: (0, 0)),
            pl.BlockSpec((1, d), lambda i: (0, 0)),
            cast_in,
        ],
        out_specs=[pl.BlockSpec((TM, d), row), pl.BlockSpec((TM, d), row), cast_out],
        out_shape=[jax.ShapeDtypeStruct((n, d), F32), jax.ShapeDtypeStruct((n, d), BF16), cast_shape],
        compiler_params=_params(("arbitrary",)),
        name="dense_swiglu",
    )(hn, x, w_gu, w_gu, w_down, g, cast_src)


def _row_copy(src_ref, src_row, dst_ref, dst_row, sem):
    return pltpu.make_async_copy(src_ref.at[pl.ds(src_row, 1), :], dst_ref.at[pl.ds(dst_row, 1), :], sem)


SUBLANES = 8


def _zero_rows(zero_ref, dst_ref, dst_row, count, limit, sem, wait):
    head = jnp.minimum((-dst_row) & (SUBLANES - 1), count)
    for j in range(SUBLANES - 1):
        @pl.when(j < head)
        def _(j=j):
            copy = _row_copy(zero_ref, 0, dst_ref, dst_row + j, sem)
            copy.wait() if wait else copy.start()

    rest = count - head
    first = dst_row + head
    for bit in range(SUBLANES.bit_length() - 1, limit.bit_length() - 1):
        size = 1 << bit
        done = (rest >> (bit + 1)) << (bit + 1)

        @pl.when(((rest >> bit) & 1) == 1)
        def _(size=size, done=done):
            start = pl.multiple_of(first + done, SUBLANES)
            copy = pltpu.make_async_copy(zero_ref.at[pl.ds(0, size), :],
                                         dst_ref.at[pl.ds(start, size), :], sem)
            copy.wait() if wait else copy.start()


def _expert_kernel(te_ref, tv_ref, xs_ref, wg_ref, wu_ref, wd_ref, ys_ref, *, nchunk):
    i = pl.program_id(0)
    half = xs_ref.shape[0] // 2
    width = wg_ref.shape[2] // nchunk

    def swiglu_rows(rows):
        x = xs_ref[rows, :].astype(BF16)
        acc = None
        for c in range(nchunk):
            cs = slice(c * width, (c + 1) * width)
            gate = _dot(x, wg_ref[0, :, cs])
            up = _dot(x, wu_ref[0, :, cs])
            act = (gate * _sigmoid(gate) * up).astype(BF16)
            part = _dot(act, wd_ref[0, cs, :])
            acc = part if acc is None else acc + part
        ys_ref[rows, :] = acc.astype(BF16).astype(F32)

    @pl.when(tv_ref[i] == 2)
    def _():
        swiglu_rows(slice(None))

    @pl.when(tv_ref[i] == 1)
    def _():
        swiglu_rows(slice(0, half))
        ys_ref[half:, :] = jnp.zeros((half, ys_ref.shape[1]), F32)

    @pl.when(tv_ref[i] == 0)
    def _():
        ys_ref[...] = jnp.zeros_like(ys_ref)


def _experts(tile_expert, tile_valid, xs, w_gu, w_down):
    r, d = xs.shape
    n_tiles = r // TM_EXPERT
    dff = w_down.shape[1]
    kern = functools.partial(_expert_kernel, nchunk=EXPERT_CHUNKS)
    return pl.pallas_call(
        kern,
        grid_spec=pltpu.PrefetchScalarGridSpec(
            num_scalar_prefetch=2,
            grid=(r // TM_EXPERT,),
            in_specs=[
                pl.BlockSpec((TM_EXPERT, d), lambda i, te, tv: (jnp.minimum(i, tv[n_tiles]), 0)),
                pl.BlockSpec((1, d, dff), lambda i, te, tv: (te[i], 0, 0)),
                pl.BlockSpec((1, d, dff), lambda i, te, tv: (te[i], 0, 1)),
                pl.BlockSpec((1, dff, d), lambda i, te, tv: (te[i], 0, 0)),
            ],
            out_specs=pl.BlockSpec((TM_EXPERT, d), lambda i, te, tv: (i, 0)),
        ),
        out_shape=jax.ShapeDtypeStruct((r, d), F32),
        compiler_params=_params(("arbitrary",)),
        name="moe_experts",
    )(tile_expert, tile_valid, xs, w_gu, w_gu, w_down)


RUN_BUF_ROWS = 2 * TM + LANES


def _run_positions(idx, off_row):
    lane = lax.broadcasted_iota(I32, idx.shape, 1)
    pos = []
    for slot in range(2):
        off = jnp.sum(jnp.where(lane == idx[:, slot:slot + 1], off_row, 0.0), axis=-1, keepdims=True)
        pos.append(off.astype(I32) + idx[:, 2 + slot:3 + slot])
    return pos


def _run_copies(tab_ref, step, hbm_ref, buf_ref, slot, sem, to_hbm, wait):
    for e in range(N_EXPERTS):
        entry = (step * N_EXPERTS + e) * 3
        off, row, length = tab_ref[entry], tab_ref[entry + 1], tab_ref[entry + 2]
        for bit in range(SUBLANES.bit_length() - 1, TM.bit_length()):
            size = 1 << bit
            done = (length >> (bit + 1)) << (bit + 1)

            @pl.when(((length >> bit) & 1) == 1)
            def _(size=size, done=done, off=off, row=row):
                in_buf = buf_ref.at[slot, pl.ds(pl.multiple_of(off + done, SUBLANES), size), :]
                in_hbm = hbm_ref.at[pl.ds(pl.multiple_of(row + done, SUBLANES), size), :]
                copy = (pltpu.make_async_copy(in_buf, in_hbm, sem.at[slot]) if to_hbm
                        else pltpu.make_async_copy(in_hbm, in_buf, sem.at[slot]))
                copy.wait() if wait else copy.start()


def _run_dispatch_kernel(tab_ref, pad_ref, hn_ref, idx_ref, off_ref, xs_ref, buf, zero_sc, sem, zsem,
                         *, n_steps):
    j = pl.program_id(0)
    slot = j & 1

    @pl.when(j == 0)
    def _():
        zero_sc[...] = jnp.zeros_like(zero_sc)
        rows = zero_sc.shape[0]
        for wait in (False, True):
            for e in range(N_EXPERTS):
                _zero_rows(zero_sc, xs_ref, pad_ref[2 * e], pad_ref[2 * e + 1], 2 * rows, zsem, wait)

            def tail(t, carry, wait=wait):
                first = pl.multiple_of(pad_ref[2 * N_EXPERTS] + t * rows, SUBLANES)
                copy = pltpu.make_async_copy(zero_sc, xs_ref.at[pl.ds(first, rows), :], zsem)
                copy.wait() if wait else copy.start()
                return carry

            lax.fori_loop(0, pad_ref[2 * N_EXPERTS + 1] // rows, tail, 0)

    @pl.when(j >= 2)
    def _():
        _run_copies(tab_ref, j - 2, xs_ref, buf, slot, sem, True, True)

    pos0, pos1 = _run_positions(idx_ref[...], off_ref[0])
    lane = lax.broadcasted_iota(I32, (hn_ref.shape[0], buf.shape[1]), 1)
    sel = jnp.where((lane == pos0) | (lane == pos1), 1.0, 0.0).astype(BF16)
    buf[slot] = _dot_tn(sel, hn_ref[...])
    _run_copies(tab_ref, j, xs_ref, buf, slot, sem, True, False)

    @pl.when(j == n_steps - 1)
    def _():
        _run_copies(tab_ref, j - 1, xs_ref, buf, 1 - slot, sem, True, True)
        _run_copies(tab_ref, j, xs_ref, buf, slot, sem, True, True)


def _run_dispatch(run_table, pad_table, hn, idx, run_offsets, n_rows):
    n, d = hn.shape
    n_steps = n // TM
    assert n_steps >= 2
    return pl.pallas_call(
        functools.partial(_run_dispatch_kernel, n_steps=n_steps),
        grid_spec=pltpu.PrefetchScalarGridSpec(
            num_scalar_prefetch=2,
            grid=(n_steps,),
            in_specs=[
                pl.BlockSpec((TM, d), lambda i, *_: (i, 0)),
                pl.BlockSpec((TM, LANES), lambda i, *_: (i, 0)),
                pl.BlockSpec((1, 1, LANES), lambda i, *_: (i, 0, 0)),
            ],
            out_specs=pl.BlockSpec(memory_space=pl.ANY),
            scratch_shapes=[pltpu.VMEM((2, RUN_BUF_ROWS, d), F32),
                            pltpu.VMEM((TM_EXPERT // 2, d), F32),
                            pltpu.SemaphoreType.DMA((2,)), pltpu.SemaphoreType.DMA(())],
        ),
        out_shape=jax.ShapeDtypeStruct((n_rows, d), F32),
        compiler_params=_params(("arbitrary",)),
        name="moe_dispatch",
    )(run_table, pad_table, hn, idx, run_offsets)


def _run_combine_kernel(tab_ref, ys_ref, x_ref, gate_ref, idx_ref, off_ref, g_ref, yp_ref, ysm_ref,
                        buf, sem, *, n_first, n_steps):
    j = pl.program_id(0)
    slot = j & 1

    @pl.when(j == 0)
    def _():
        buf[...] = jnp.zeros_like(buf)
        _run_copies(tab_ref, 0, ys_ref, buf, 0, sem, False, False)

    @pl.when(j + 1 < n_steps)
    def _():
        _run_copies(tab_ref, j + 1, ys_ref, buf, 1 - slot, sem, False, False)

    _run_copies(tab_ref, j, ys_ref, buf, slot, sem, False, True)
    runs = buf[slot].astype(BF16)
    pos = _run_positions(idx_ref[...], off_ref[0])
    lane = lax.broadcasted_iota(I32, (x_ref.shape[0], buf.shape[1]), 1)
    gate = gate_ref[...]
    y = x_ref[...]
    for k in range(2):
        sel = jnp.where(lane == pos[k], 1.0, 0.0).astype(BF16)
        y = y + gate[:, k:k + 1] * _dot(sel, runs)
    y = _rms(y, g_ref[...])

    @pl.when(j < n_first)
    def _():
        yp_ref[...] = y

    @pl.when(j >= n_first)
    def _():
        ysm_ref[...] = y


def _run_combine(run_table, ys, x, gates, idx, run_offsets, g, n_p):
    n, d = x.shape
    n_first = n_p // TM
    n_steps = n // TM
    p_spec, s_spec = _split_specs(n_first, d)
    row = lambda i, *_: (i, 0)
    return pl.pallas_call(
        functools.partial(_run_combine_kernel, n_first=n_first, n_steps=n_steps),
        grid_spec=pltpu.PrefetchScalarGridSpec(
            num_scalar_prefetch=1,
            grid=(n_steps,),
            in_specs=[
                pl.BlockSpec(memory_space=pl.ANY),
                pl.BlockSpec((TM, d), row),
                pl.BlockSpec((TM, LANES), row),
                pl.BlockSpec((TM, LANES), row),
                pl.BlockSpec((1, 1, LANES), lambda i, *_: (i, 0, 0)),
                pl.BlockSpec((1, d), lambda i, *_: (0, 0)),
            ],
            out_specs=[p_spec, s_spec],
            scratch_shapes=[pltpu.VMEM((2, RUN_BUF_ROWS, d), F32), pltpu.SemaphoreType.DMA((2,))],
        ),
        out_shape=[jax.ShapeDtypeStruct((n_p, d), F32), jax.ShapeDtypeStruct((n - n_p, d), F32)],
        compiler_params=_params(("arbitrary",)),
        name="moe_combine",
    )(run_table, ys, x, gates, idx, run_offsets, g)


def _run_tables(tile_counts, n_rows):
    cnt = tile_counts[:, 0, :N_EXPERTS].astype(I32)
    run = ((cnt + SUBLANES - 1) // SUBLANES) * SUBLANES
    off = jnp.cumsum(run, axis=1) - run
    used = jnp.sum(run, axis=0)
    padded = ((used + TM_EXPERT - 1) // TM_EXPERT) * TM_EXPERT
    ends = jnp.cumsum(padded)
    starts = ends - padded
    row = starts[None, :] + jnp.cumsum(run, axis=0) - run
    run_table = jnp.stack([off, row, run], axis=2).reshape(-1)
    run_offsets = jnp.pad(off.astype(F32), ((0, 0), (0, LANES - N_EXPERTS)))[:, None, :]
    n_tiles = n_rows // TM_EXPERT
    used_tiles = ends[-1] // TM_EXPERT
    pad_table = jnp.stack([starts + used, padded - used], axis=1).reshape(-1)
    pad_table = jnp.concatenate([pad_table, jnp.stack([ends[-1], n_rows - ends[-1]])])
    tile_start = jnp.arange(n_tiles, dtype=I32) * TM_EXPERT
    tile_expert = jnp.sum((tile_start[:, None] >= ends[None, :]).astype(I32), axis=1)
    last_expert = jnp.sum((ends[-1] - 1 >= ends).astype(I32))
    tile_expert = jnp.minimum(tile_expert, last_expert)
    used_end = (starts + used)[tile_expert]
    half = TM_EXPERT // 2
    tile_rows = jnp.clip(used_end - tile_start, 0, TM_EXPERT)
    tile_valid = jnp.where(tile_start < ends[-1], (tile_rows + half - 1) // half, 0)
    tile_valid = jnp.concatenate([tile_valid, (used_tiles - 1)[None]])
    return run_table, run_offsets, pad_table, tile_expert, tile_valid


def _pad_time(a, nseq, t):
    w = a.shape[1]
    a = jnp.pad(a.reshape(nseq, t, w), ((0, 0), (0, SAMPLE_PAD_T - t), (0, 0)))
    return a.reshape(nseq * SAMPLE_PAD_T, w)


def kernel(x_prompt, x_sample, state_gla_S, state_mlstm_C, state_mlstm_n, state_mlstm_m,
           norm_mix, norm_ffn, norm_final,
           gla_w_in, gla_w_a2, gla_b_a, gla_g_norm, gla_w_out,
           mlstm_w_in, mlstm_b_gate, mlstm_g_norm, mlstm_w_out,
           ffn_w_gu, ffn_w_down,
           moe_w_router, moe_b_router, moe_w_gu, moe_w_down):
    bp, tp, d = x_prompt.shape
    bs, ts, _ = x_sample.shape
    n_p = bp * tp
    n_s = bs * ts
    n = n_p + n_s
    assert n_p % TM == 0 and n_s % TM == 0 and tp % (2 * CHUNK) == 0 and ts <= SAMPLE_PAD_T
    assert norm_mix.shape[0] == 2, "one GLA layer followed by one mLSTM layer"
    qk_w = d // 2
    main_w = 2 * qk_w + 2 * d

    x_p = x_prompt.reshape(n_p, d)
    x_s = x_sample.reshape(n_s, d)

    w_a2 = jnp.pad(gla_w_a2[0], ((0, LANES - GLA_RANK), (0, 0))).astype(BF16)
    z, log_a, ffn_wgu = _gla_in_proj(x_p, _pad_time(x_s, bs, ts), norm_mix[0][None], gla_w_in[0].T,
                                     main_w, w_a2, gla_b_a[0][None], ffn_w_gu[0])
    s0_p = jnp.zeros((bp,) + state_gla_S.shape[2:], F32)
    u_p, s_p = _gla_recurrence(z, log_a, s0_p, gla_g_norm[0], bp, tp, CHUNK, 16 * CHUNK)
    u_s, s_s = _gla_recurrence_short(z, log_a, state_gla_S[0], gla_g_norm[0], bs, SAMPLE_PAD_T, ts,
                                     SAMPLE_SEQS_PER_STEP, n_p)
    x, hn, ffn_wd = _out_proj(u_p, u_s, x_p, x_s, gla_w_out[0].astype(BF16),
                              norm_ffn[0][None], ffn_w_down[0])
    x, hn, moe_wgu = _swiglu(hn, x, ffn_wgu, ffn_wd,
                             norm_mix[1][None], SWIGLU_CHUNKS,
                             moe_w_gu[0].reshape(N_EXPERTS * d, moe_w_gu.shape[3]))

    b_gt = jnp.pad(mlstm_b_gate[0], (0, LANES - 2 * MLSTM_HEADS))[None]
    dff_e = moe_w_down.shape[2]
    z, gates, moe_wd = _mlstm_in_proj(hn, n_p, _pad_time(hn[n_p:], bs, ts), mlstm_w_in[0].T, main_w,
                                      b_gt, moe_w_down[0].reshape(N_EXPERTS * dff_e, d))
    pairs = MLSTM_HEADS // 2
    dk2 = 2 * state_mlstm_C.shape[3]
    dvm = state_mlstm_C.shape[4]
    c0_p = jnp.zeros((bp, pairs, dk2, dvm), F32)
    n0_p = jnp.zeros((bp, pairs, dk2), F32)
    m0_p = jnp.zeros((bp, 1, LANES), F32)
    u_p, c_p, nn_p, m_p = _mlstm_recurrence_blocked(z, gates, c0_p, n0_p, m0_p,
                                                    mlstm_g_norm[0], bp, tp, CHUNK, 16 * CHUNK)
    c0_s = state_mlstm_C[0]
    n0_s = state_mlstm_n[0].reshape(bs, pairs, dk2)
    m0_s = jnp.pad(state_mlstm_m[0], ((0, 0), (0, LANES - MLSTM_HEADS)))[:, None, :]
    u_s, c_s, nn_s, m_s = _mlstm_recurrence_short(z, gates, c0_s, n0_s, m0_s, mlstm_g_norm[0], bs,
                                                  SAMPLE_PAD_T, ts, SAMPLE_SEQS_PER_STEP, n_p)

    w_r = jnp.pad(moe_w_router[0], ((0, 0), (0, LANES - N_EXPERTS)))
    w_r_hi = w_r.astype(BF16)
    w_r = jnp.stack([w_r_hi, (w_r - w_r_hi.astype(F32)).astype(BF16)])
    b_r = jnp.pad(moe_b_router[0], (0, LANES - N_EXPERTS))[None]
    x, hn, route_g, route_i, tile_counts = _out_proj_router(
        u_p, u_s, x, mlstm_w_out[0].astype(BF16), norm_ffn[1][None], w_r, b_r)
    n_rows = 2 * n + (n // TM) * N_EXPERTS * (SUBLANES - 1) + N_EXPERTS * (TM_EXPERT - 1)
    n_rows = -(-n_rows // TM_EXPERT) * TM_EXPERT
    run_table, run_offsets, pad_table, tile_expert, tile_valid = _run_tables(tile_counts, n_rows)
    xs = _run_dispatch(run_table, pad_table, hn, route_i, run_offsets, n_rows)
    ys = _experts(tile_expert, tile_valid, xs, moe_wgu.reshape(moe_w_gu.shape[1:]),
                  moe_wd.reshape(moe_w_down.shape[1:]))
    y_p, y_s = _run_combine(run_table, ys, x, route_g, route_i, run_offsets, norm_final[None], n_p)

    y_prompt = y_p.reshape(bp, tp, d)
    y_sample = y_s.reshape(bs, ts, d)
    hd = state_mlstm_C.shape[2:]
    return (y_prompt, y_sample,
            s_p[None], c_p.reshape((1, bp) + hd), nn_p.reshape(1, bp, hd[0], hd[1]),
            m_p[None, :, 0, :MLSTM_HEADS],
            s_s[None], c_s[None], nn_s.reshape(1, bs, hd[0], hd[1]), m_s[None, :, 0, :MLSTM_HEADS])


def _mlstm_in_proj(hn, n_p, hn_s, w_in, wz, b_gate, cast_src):
    d = hn.shape[1]
    n = n_p + hn_s.shape[0]
    n_first = n_p // TM
    cast_in, cast_out, cast_shape = _cast_side_job(cast_src, n // TM)

    def kern(hp_ref, hs_ref, w_ref, wg_ref, b_ref, cast_ref, z_ref, gate_ref, cast_out_ref,
             wb_sc, wg_sc):
        @pl.when(pl.program_id(0) == 0)
        def _():
            wb_sc[...] = w_ref[...].T.astype(BF16)
            wg_sc[...] = _gate_weights(wg_ref)

        cast_out_ref[...] = cast_ref[...].astype(BF16)
        h = _pick_rows(pl.program_id(0), n_first, hp_ref, hs_ref)
        width = wz // IN_PROJ_COL_CHUNKS
        for c in range(IN_PROJ_COL_CHUNKS):
            cs = slice(c * width, (c + 1) * width)
            z_ref[:, cs] = _dot(h, wb_sc[:, cs])
        gp = _dot(h, wg_sc[...]) + b_ref[...]
        gc = GATE_CAP * jnp.tanh(gp * (1.0 / GATE_CAP))
        lane = lax.broadcasted_iota(I32, gc.shape, 1)
        out = jnp.where(lane < MLSTM_HEADS, gc, _log_sigmoid(gc))
        gate_ref[...] = jnp.where(lane < 2 * MLSTM_HEADS, out, 0.0)

    return pl.pallas_call(
        kern,
        grid=(n // TM,),
        in_specs=_split_specs(n_first, d) + [
            _resident_spec((wz, d), lambda i: (0, 0)),
            _gate_rows_spec(w_in, wz),
            pl.BlockSpec((1, LANES), lambda i: (0, 0)),
            cast_in,
        ],
        out_specs=[pl.BlockSpec((TM, wz), lambda i: (i, 0)), pl.BlockSpec((TM, LANES), lambda i: (i, 0)),
                   cast_out],
        out_shape=[jax.ShapeDtypeStruct((n, wz), F32), jax.ShapeDtypeStruct((n, LANES), F32),
                   cast_shape],
        scratch_shapes=[pltpu.VMEM((d, wz), BF16), pltpu.VMEM((d, LANES), BF16)],
        compiler_params=_params(("arbitrary",)),
        name="in_proj_mlstm",
    )(hn, hn_s, w_in, w_in, b_gate, cast_src)
```

```python
import functools

import jax
import jax.numpy as jnp
import numpy as np
from jax import lax
from jax.experimental import pallas as pl
from jax.experimental.pallas import tpu as pltpu

F32 = jnp.float32
BF16 = jnp.bfloat16
I32 = jnp.int32

EPS = 1e-6
GLA_HEADS = 4
GLA_RANK = 16
GLA_TAU = 16.0
MLSTM_HEADS = 8
GATE_CAP = 15.0
CHUNK = 64
N_EXPERTS = 8
NEG_BIG = -1e30

LANES = 128
TM = 512
TM_EXPERT = 512
SAMPLE_PAD_T = 8
SAMPLE_SEQS_PER_STEP = 16
SWIGLU_CHUNKS = 11
EXPERT_CHUNKS = 7
VMEM_LIMIT = 56 * 1024 * 1024


def _dot(a, b):
    return jnp.dot(a, b, preferred_element_type=F32)


def _dot_nt(a, b):
    return lax.dot_general(a, b, (((1,), (1,)), ((), ())), preferred_element_type=F32)


def _dot_tn(a, b):
    return lax.dot_general(a, b, (((0,), (0,)), ((), ())), preferred_element_type=F32)


def _sigmoid(x):
    return 1.0 / (1.0 + jnp.exp(-x))


def _log_sigmoid(x):
    return jnp.minimum(x, 0.0) - jnp.log(1.0 + jnp.exp(-jnp.abs(x)))


def _rms(x, g):
    return x * lax.rsqrt(jnp.mean(x * x, axis=-1, keepdims=True) + EPS) * g


def _cumsum_rows(x, n):
    ridx = lax.broadcasted_iota(I32, x.shape, 0)
    s = 1
    while s < n:
        x = x + jnp.where(ridx >= s, pltpu.roll(x, s, 0), 0.0)
        s *= 2
    return x


def _params(sem):
    return pltpu.CompilerParams(dimension_semantics=sem, vmem_limit_bytes=VMEM_LIMIT)


def _resident_spec(block_shape, index_map):
    return pl.BlockSpec(block_shape, index_map, pipeline_mode=pl.Buffered(1))


def _cast_side_job(src, n_steps):
    rows, cols = src.shape
    n_slabs = max(s for s in range(1, n_steps + 1) if rows % (16 * s) == 0)
    spec = pl.BlockSpec((rows // n_slabs, cols), lambda i, *_: (jnp.minimum(i, n_slabs - 1), 0))
    return spec, spec, jax.ShapeDtypeStruct(src.shape, BF16)


IN_PROJ_COL_CHUNKS = 4
ROW_PARTS = 2


def _pick_rows(i, n_first, first_ref, second_ref):
    return jnp.where(i < n_first, first_ref[...], second_ref[...])


def _split_specs(n_first, width):
    return [pl.BlockSpec((TM, width), lambda i, *_: (jnp.minimum(i, n_first - 1), 0)),
            pl.BlockSpec((TM, width), lambda i, *_: (jnp.maximum(i - n_first, 0), 0))]


def _gate_weights(wg_ref):
    rows, d = wg_ref.shape
    full = jnp.concatenate([wg_ref[...], jnp.zeros((LANES - rows, d), F32)], axis=0)
    return full.T.astype(BF16)


def _gla_in_proj_kernel(xp_ref, xs_ref, g_ref, w_ref, wg_ref, w2_ref, b_ref, cast_ref,
                        z_ref, gate_ref, cast_out_ref, wb_sc, wg_sc, *, n_first):
    @pl.when(pl.program_id(0) == 0)
    def _():
        wb_sc[...] = w_ref[...].T.astype(BF16)
        wg_sc[...] = _gate_weights(wg_ref)

    cast_out_ref[...] = cast_ref[...].astype(BF16)
    first = pl.program_id(0) < n_first
    width = wb_sc.shape[1] // IN_PROJ_COL_CHUNKS
    part = xp_ref.shape[0] // ROW_PARTS
    for r in range(ROW_PARTS):
        rows = slice(r * part, (r + 1) * part)
        x = jnp.where(first, xp_ref[rows, :], xs_ref[rows, :])
        hn = _rms(x, g_ref[...]).astype(BF16)
        for c in range(IN_PROJ_COL_CHUNKS):
            cs = slice(c * width, (c + 1) * width)
            z_ref[rows, cs] = _dot(hn, wb_sc[:, cs])
        a = _dot(hn, wg_sc[...])
        la = _dot(a.astype(BF16), w2_ref[...]) + b_ref[...]
        gate_ref[rows, :] = _log_sigmoid(la) * (1.0 / GLA_TAU)


def _gate_rows_spec(w_in, wz):
    rows = w_in.shape[0] - wz
    assert wz % rows == 0 and rows % SUBLANES == 0
    return pl.BlockSpec((rows, w_in.shape[1]), lambda i: (wz // rows, 0))


def _gla_in_proj(x_p, x_s, g, w_in, wz, w2, b, cast_src):
    d = x_p.shape[1]
    n = x_p.shape[0] + x_s.shape[0]
    n_first = x_p.shape[0] // TM
    gw = b.shape[1]
    cast_in, cast_out, cast_shape = _cast_side_job(cast_src, n // TM)
    return pl.pallas_call(
        functools.partial(_gla_in_proj_kernel, n_first=n_first),
        grid=(n // TM,),
        in_specs=_split_specs(n_first, d) + [
            pl.BlockSpec((1, d), lambda i: (0, 0)),
            _resident_spec((wz, d), lambda i: (0, 0)),
            _gate_rows_spec(w_in, wz),
            pl.BlockSpec(w2.shape, lambda i: (0, 0)),
            pl.BlockSpec((1, gw), lambda i: (0, 0)),
            cast_in,
        ],
        out_specs=[
            pl.BlockSpec((TM, wz), lambda i: (i, 0)),
            pl.BlockSpec((TM, gw), lambda i: (i, 0)),
            cast_out,
        ],
        out_shape=[jax.ShapeDtypeStruct((n, wz), F32), jax.ShapeDtypeStruct((n, gw), F32), cast_shape],
        scratch_shapes=[pltpu.VMEM((d, wz), BF16), pltpu.VMEM((d, LANES), BF16)],
        compiler_params=_params(("arbitrary",)),
        name="in_proj_gla",
    )(x_p, x_s, g, w_in, w_in, w2, b, cast_src)


def _gla_kernel(q_ref, k_ref, v_ref, r_ref, la_ref, s0_ref, gn_ref, u_ref, st_ref, st_sc,
                *, chunk, tb, nblk):
    heads = GLA_HEADS
    dk = q_ref.shape[1] // heads
    dv = v_ref.shape[1] // heads
    c = pl.program_id(1)

    @pl.when(c == 0)
    def _():
        for h in range(heads):
            st_sc[h] = s0_ref[0, h].T

    ri = lax.broadcasted_iota(I32, (chunk, chunk), 0)
    ci = lax.broadcasted_iota(I32, (chunk, chunk), 1)
    causal = ri >= ci
    kscale = dk ** -0.5
    state = [st_sc[h] for h in range(heads)]
    for j in range(tb // chunk):
        rows = slice(j * chunk, (j + 1) * chunk)
        b_all = _cumsum_rows(la_ref[rows, :], chunk)
        for h in range(heads):
            ks = slice(h * dk, (h + 1) * dk)
            vs = slice(h * dv, (h + 1) * dv)
            b = b_all[:, ks]
            b_end = b[chunk - 1:chunk, :]
            q = q_ref[rows, ks]
            k = k_ref[rows, ks] * kscale
            qg = (q * jnp.exp(b)).astype(BF16)
            kg = (k * jnp.exp(-b)).astype(BF16)
            ke = (k * jnp.exp(b_end - b)).astype(BF16)
            v = v_ref[rows, vs].astype(BF16)
            a = jnp.where(causal, _dot_nt(qg, kg), 0.0).astype(BF16)
            o = _dot_nt(qg, state[h].astype(BF16)) + _dot(a, v)
            state[h] = state[h] * jnp.exp(b_end) + _dot_tn(v, ke)
            on = _rms(o, gn_ref[h:h + 1, :])
            r = r_ref[rows, vs]
            u_ref[rows, vs] = (r * _sigmoid(r) * on).astype(BF16)
    for h in range(heads):
        st_sc[h] = state[h]

    @pl.when(c == nblk - 1)
    def _():
        for h in range(heads):
            st_ref[0, h] = st_sc[h].T


def _gla_recurrence(z, log_a, s0, g_norm, nseq, t, chunk, tb):
    qk_w = log_a.shape[1]
    v_w = 2 * qk_w
    nblk = t // tb
    heads, dk, dv = s0.shape[1:]
    kern = functools.partial(_gla_kernel, chunk=chunk, tb=tb, nblk=nblk)
    row = lambda b, c: b * nblk + c
    return pl.pallas_call(
        kern,
        grid=(nseq, nblk),
        in_specs=[
            pl.BlockSpec((tb, qk_w), lambda b, c: (row(b, c), 0)),
            pl.BlockSpec((tb, qk_w), lambda b, c: (row(b, c), 1)),
            pl.BlockSpec((tb, v_w), lambda b, c: (row(b, c), 1)),
            pl.BlockSpec((tb, v_w), lambda b, c: (row(b, c), 2)),
            pl.BlockSpec((tb, qk_w), lambda b, c: (row(b, c), 0)),
            pl.BlockSpec((1, heads, dk, dv), lambda b, c: (b, 0, 0, 0)),
            pl.BlockSpec((heads, dv), lambda b, c: (0, 0)),
        ],
        out_specs=[
            pl.BlockSpec((tb, v_w), lambda b, c: (row(b, c), 0)),
            pl.BlockSpec((1, heads, dk, dv), lambda b, c: (b, 0, 0, 0)),
        ],
        out_shape=[
            jax.ShapeDtypeStruct((nseq * t, v_w), BF16),
            jax.ShapeDtypeStruct(s0.shape, F32),
        ],
        scratch_shapes=[pltpu.VMEM((heads, dv, dk), F32)],
        compiler_params=_params(("arbitrary", "arbitrary")),
        name="gla_recurrence_t%d" % t,
    )(z, z, z, z, log_a, s0, g_norm)


def _real_rows_selector(nsq, t, t_real):
    i = lax.broadcasted_iota(I32, (nsq * t_real, nsq * t), 0)
    r = lax.broadcasted_iota(I32, (nsq * t_real, nsq * t), 1)
    return jnp.where(r == (i // t_real) * t + i % t_real, 1.0, 0.0).astype(BF16)


def _gla_short_kernel(q_ref, k_ref, v_ref, r_ref, la_ref, s0_ref, gn_ref, u_ref, st_ref,
                      *, t, t_real, nsq):
    heads = GLA_HEADS
    dk = q_ref.shape[1] // heads
    dv = v_ref.shape[1] // heads
    nrow = nsq * t
    width = q_ref.shape[1]
    rc = lax.broadcasted_iota(I32, (nrow, width), 0) & (t - 1)
    log_a = jnp.where(rc < t_real, la_ref[...], 0.0)
    b = _chunk_scan(log_a, rc, t, jnp.add, 0.0)
    pick = _real_rows_selector(nsq, t, t_real)
    b_last = b.reshape(nsq, t, width)[:, t - 1:t, :]
    b_end = jnp.broadcast_to(b_last, (nsq, t, width)).reshape(nrow, width)
    k = k_ref[...] * (dk ** -0.5)
    qg = q_ref[...] * jnp.exp(b)
    kg = k * jnp.exp(-b)
    ke = k * jnp.exp(b_end - b)
    decay = jnp.exp(b_end)
    ri = lax.broadcasted_iota(I32, (nrow, nrow), 0)
    ci = lax.broadcasted_iota(I32, (nrow, nrow), 1)
    shift = t.bit_length() - 1
    mask = ((ri >> shift) == (ci >> shift)) & (ri >= ci)
    for h in range(heads):
        ks = slice(h * dk, (h + 1) * dk)
        vs = slice(h * dv, (h + 1) * dv)
        a = jnp.where(mask, _dot_nt(qg[:, ks].astype(BF16), kg[:, ks].astype(BF16)), 0.0)
        o = _dot(a.astype(BF16), v_ref[:, vs].astype(BF16))
        decay_t = decay[:, ks].T
        o_state = []
        for sq in range(nsq):
            r = slice(sq * t, (sq + 1) * t)
            s0 = s0_ref[sq, h]
            o_state.append(_dot(qg[r, ks].astype(BF16), s0.astype(BF16)))
            upd = _dot_tn(ke[r, ks].astype(BF16), v_ref[r, vs].astype(BF16))
            st_ref[sq, h] = decay_t[:, sq * t:sq * t + 1] * s0 + upd
        on = _rms(o + jnp.concatenate(o_state, axis=0), gn_ref[h:h + 1, :])
        rg = r_ref[:, vs]
        gated = (rg * _sigmoid(rg) * on).astype(BF16)
        u_ref[:, vs] = _dot(pick, gated).astype(BF16)


def _gla_recurrence_short(z, log_a, s0, g_norm, nseq, t, t_real, nsq, row0):
    qk_w = log_a.shape[1]
    v_w = 2 * qk_w
    heads, dk, dv = s0.shape[1:]
    assert nsq * t == dk == LANES and row0 % (nsq * t) == 0
    rb = nsq * t
    blk0 = row0 // rb
    return pl.pallas_call(
        functools.partial(_gla_short_kernel, t=t, t_real=t_real, nsq=nsq),
        grid=(nseq // nsq,),
        in_specs=[
            pl.BlockSpec((rb, qk_w), lambda b: (blk0 + b, 0)),
            pl.BlockSpec((rb, qk_w), lambda b: (blk0 + b, 1)),
            pl.BlockSpec((rb, v_w), lambda b: (blk0 + b, 1)),
            pl.BlockSpec((rb, v_w), lambda b: (blk0 + b, 2)),
            pl.BlockSpec((rb, qk_w), lambda b: (blk0 + b, 0)),
            pl.BlockSpec((nsq, heads, dk, dv), lambda b: (b, 0, 0, 0)),
            pl.BlockSpec((heads, dv), lambda b: (0, 0)),
        ],
        out_specs=[
            pl.BlockSpec((nsq * t_real, v_w), lambda b: (b, 0)),
            pl.BlockSpec((nsq, heads, dk, dv), lambda b: (b, 0, 0, 0)),
        ],
        out_shape=[
            jax.ShapeDtypeStruct((nseq * t_real, v_w), BF16),
            jax.ShapeDtypeStruct(s0.shape, F32),
        ],
        compiler_params=_params(("arbitrary",)),
        name="gla_recurrence_t%d" % t,
    )(z, z, z, z, log_a, s0, g_norm)


def _mlstm_short_kernel(q_ref, k_ref, v_ref, op_ref, gt_ref, e_ref, c0_ref, n0_ref, m0_ref, gn_ref,
                        u_ref, ct_ref, nt_ref, mt_ref, *, t, t_real, nsq):
    heads = MLSTM_HEADS
    pairs = heads // 2
    dv = v_ref.shape[1] // heads
    dk = q_ref.shape[1] // heads
    nrow = nsq * t

    def per_seq(x3):
        return jnp.broadcast_to(x3, (nsq, t, x3.shape[2])).reshape(nrow, x3.shape[2])

    def last(x):
        return x.reshape(nsq, t, x.shape[1])[:, t - 1:t, :]

    rc = lax.broadcasted_iota(I32, (nrow, LANES), 0) & (t - 1)
    lane_g = lax.broadcasted_iota(I32, (nrow, LANES), 1)
    gts = jnp.where(rc < t_real, gt_ref[...],
                    jnp.where(lane_g < heads, NEG_BIG, 0.0))
    pick = _real_rows_selector(nsq, t, t_real)
    f_cum = pltpu.roll(_chunk_scan(gts, rc, t, jnp.add, 0.0), LANES - heads, 1)
    b = gts - f_cum
    cmb = _chunk_scan(b, rc, t, jnp.maximum, NEG_BIG)
    mp = per_seq(m0_ref[...])
    big_m = jnp.maximum(mp, cmb)
    m_t = f_cum + big_m
    m_new = last(m_t)
    mt_ref[...] = m_new
    mn = per_seq(m_new)
    fe = per_seq(last(f_cum))
    w_i = jnp.exp(mp - big_m)
    em = jnp.exp(-m_t)
    k_sc = jnp.exp(fe - f_cum + gts - mn)
    w_c = jnp.exp(fe + mp - mn)
    ex = _dot(_pack_terms([w_i, k_sc, -big_m, em, w_c], heads), e_ref[...])
    offs = np.cumsum([0] + [heads * w for w in SHORT_EXPAND_WIDTHS])
    q = q_ref[...]
    k = k_ref[...] * (dk ** -0.5)
    qs = q * ex[:, offs[0]:offs[1]]
    ke = k * ex[:, offs[1]:offs[2]]
    neg_m = ex[:, offs[2]:offs[3]]
    em_v = ex[:, offs[3]:offs[4]]
    wc_k = ex[:, offs[4]:offs[5]]
    bt = b.T

    ri = lax.broadcasted_iota(I32, (nrow, nrow), 0)
    ci = lax.broadcasted_iota(I32, (nrow, nrow), 1)
    shift = t.bit_length() - 1
    mask = ((ri >> shift) == (ci >> shift)) & (ri >= ci)
    lane1 = lax.broadcasted_iota(I32, (1, LANES), 1)
    rowi = lax.broadcasted_iota(I32, (LANES, LANES), 0)
    ones_v = jnp.ones((nrow, dv), BF16)
    ones_k = jnp.ones((LANES, dv), BF16)
    for p in range(pairs):
        ps = slice(p * LANES, (p + 1) * LANES)
        qp = q[:, ps]
        qsp = qs[:, ps]
        kpb = k[:, ps].astype(BF16)
        n_rows = per_seq(n0_ref[:, p:p + 1, :])
        intra, den_state, qsh = [], [], []
        for jj in range(2):
            h = 2 * p + jj
            hs = slice(h * dv, (h + 1) * dv)
            mine = (lane1 >= jj * dk) & (lane1 < (jj + 1) * dk)
            qh = jnp.where(mine, qp, 0.0).astype(BF16)
            qsh.append(jnp.where(mine, qsp, 0.0))
            d = neg_m[:, hs] + bt[h:h + 1, :]
            s = _dot_nt(qh, kpb) * jnp.where(mask, jnp.exp(d), 0.0)
            vh = jnp.concatenate([v_ref[:, hs].astype(BF16), ones_v], axis=1)
            intra.append(_dot(s.astype(BF16), vh))
            den_state.append(_dot((qsh[jj] * n_rows).astype(BF16), ones_k))
        num_state = [[], []]
        for sq in range(nsq):
            r = slice(sq * t, (sq + 1) * t)
            lhs = jnp.concatenate([qsh[0][r], qsh[1][r]], axis=0).astype(BF16)
            c_pair = jnp.concatenate([c0_ref[sq, 2 * p], c0_ref[sq, 2 * p + 1]], axis=0)
            res = _dot(lhs, c_pair.astype(BF16))
            num_state[0].append(res[:t])
            num_state[1].append(res[t:])
        for jj in range(2):
            h = 2 * p + jj
            hs = slice(h * dv, (h + 1) * dv)
            num = intra[jj][:, :dv] + jnp.concatenate(num_state[jj], axis=0)
            den = intra[jj][:, dv:] + den_state[jj]
            hh = num / jnp.maximum(jnp.abs(den), em_v[:, hs])
            hn = _rms(hh, gn_ref[h:h + 1, :])
            gated = (_sigmoid(op_ref[:, hs]) * hn).astype(BF16)
            u_ref[:, hs] = _dot(pick, gated).astype(BF16)
        kep = ke[:, ps]
        for sq in range(nsq):
            r = slice(sq * t, (sq + 1) * t)
            vp = v_ref[r, 2 * p * dv:(2 * p + 2) * dv].astype(BF16)
            full = _dot_tn(kep[r].astype(BF16), vp)
            upd = jnp.where(rowi < dk, full[:, :dv], full[:, dv:])
            w_row = w_c[sq * t:sq * t + 1, :]
            w_col = jnp.where(rowi < dk, w_row[:, 2 * p:2 * p + 1], w_row[:, 2 * p + 1:2 * p + 2])
            c_pair = jnp.concatenate([c0_ref[sq, 2 * p], c0_ref[sq, 2 * p + 1]], axis=0)
            c_new = w_col * c_pair + upd
            ct_ref[sq, 2 * p] = c_new[:dk]
            ct_ref[sq, 2 * p + 1] = c_new[dk:]
        k_sum = jnp.sum(kep.reshape(nsq, t, LANES), axis=1, keepdims=True)
        nt_ref[:, p:p + 1, :] = last(wc_k[:, ps]) * n0_ref[:, p:p + 1, :] + k_sum


def _mlstm_recurrence_short(z, gates, c0, n0, m0, g_norm, nseq, t, t_real, nsq, row0):
    heads = g_norm.shape[0]
    v_w = heads * g_norm.shape[1]
    qk_w = v_w // 2
    assert nsq * t == LANES and row0 % (nsq * t) == 0
    expand = _expand_matrix(heads, SHORT_EXPAND_WIDTHS)
    kern = functools.partial(_mlstm_short_kernel, t=t, t_real=t_real, nsq=nsq)
    rb = nsq * t
    blk0 = row0 // rb
    st4 = lambda b: (b, 0, 0, 0)
    st3 = lambda b: (b, 0, 0)
    return pl.pallas_call(
        kern,
        grid=(nseq // nsq,),
        in_specs=[
            pl.BlockSpec((rb, qk_w), lambda b: (blk0 + b, 0)),
            pl.BlockSpec((rb, qk_w), lambda b: (blk0 + b, 1)),
            pl.BlockSpec((rb, v_w), lambda b: (blk0 + b, 1)),
            pl.BlockSpec((rb, v_w), lambda b: (blk0 + b, 2)),
            pl.BlockSpec((rb, LANES), lambda b: (blk0 + b, 0)),
            pl.BlockSpec(expand.shape, lambda b: (0, 0)),
            pl.BlockSpec((nsq,) + c0.shape[1:], st4),
            pl.BlockSpec((nsq,) + n0.shape[1:], st3),
            pl.BlockSpec((nsq,) + m0.shape[1:], st3),
            pl.BlockSpec(g_norm.shape, lambda b: (0, 0)),
        ],
        out_specs=[
            pl.BlockSpec((nsq * t_real, v_w), lambda b: (b, 0)),
            pl.BlockSpec((nsq,) + c0.shape[1:], st4),
            pl.BlockSpec((nsq,) + n0.shape[1:], st3),
            pl.BlockSpec((nsq,) + m0.shape[1:], st3),
        ],
        out_shape=[
            jax.ShapeDtypeStruct((nseq * t_real, v_w), BF16),
            jax.ShapeDtypeStruct(c0.shape, F32),
            jax.ShapeDtypeStruct(n0.shape, F32),
            jax.ShapeDtypeStruct(m0.shape, F32),
        ],
        compiler_params=_params(("arbitrary",)),
        name="mlstm_recurrence_t%d" % t,
    )(z, z, z, z, gates, expand, c0, n0, m0, g_norm)


N_SPLIT = 3
EXPAND_WIDTHS = (64, 64, 128, 128)
SHORT_EXPAND_WIDTHS = EXPAND_WIDTHS + (64,)


def _expand_matrix(heads, widths):
    assert len(widths) * N_SPLIT * heads <= LANES
    cols = []
    for qi, width in enumerate(widths):
        sel = np.zeros((LANES, heads * width), np.float32)
        for term in range(N_SPLIT):
            for h in range(heads):
                sel[(qi * N_SPLIT + term) * heads + h, h * width:(h + 1) * width] = 1.0
        cols.append(sel)
    return jnp.asarray(np.concatenate(cols, axis=1), BF16)


def _chunk_scan(x, rc, n, op, fill):
    s = 1
    while s < n:
        x = op(x, jnp.where(rc >= s, pltpu.roll(x, s, 0), fill))
        s *= 2
    return x


def _pack_terms(values, heads):
    lane = lax.broadcasted_iota(I32, values[0].shape, 1)
    packed = jnp.zeros(values[0].shape, F32)
    slot = 0
    for val in values:
        rest = val
        for term in range(N_SPLIT):
            part = rest.astype(BF16).astype(F32)
            rest = rest - part
            moved = part if slot == 0 else pltpu.roll(part, slot * heads, 1)
            packed = jnp.where((lane >= slot * heads) & (lane < (slot + 1) * heads), moved, packed)
            slot += 1
    return packed.astype(BF16)


def _mlstm_block_kernel(q_ref, k_ref, v_ref, op_ref, gt_ref, e_ref, c0_ref, n0_ref, m0_ref, gn_ref,
                        u_ref, ct_ref, nt_ref, mt_ref, c_sc, nb_sc, m_sc, *, chunk, tb, nblk):
    heads = MLSTM_HEADS
    pairs = heads // 2
    dv = v_ref.shape[1] // heads
    dk = q_ref.shape[1] // heads
    c = pl.program_id(1)

    @pl.when(c == 0)
    def _():
        c_sc[...] = c0_ref[0]
        for p in range(pairs):
            nb_sc[p] = jnp.broadcast_to(n0_ref[0, p:p + 1, :], (LANES, LANES)).T
        m_sc[...] = m0_ref[0]

    nch = tb // chunk
    rc = lax.broadcasted_iota(I32, (tb, LANES), 0) & (chunk - 1)
    gts = gt_ref[...]
    f_cum = pltpu.roll(_chunk_scan(gts, rc, chunk, jnp.add, 0.0), LANES - heads, 1)
    b = gts - f_cum
    cmb = _chunk_scan(b, rc, chunk, jnp.maximum, NEG_BIG)
    m_prev = m_sc[...]
    mp_rows, mn_rows, fe_rows, w_c = [], [], [], []
    for j in range(nch):
        last = (j + 1) * chunk - 1
        f_end = f_cum[last:last + 1, :]
        m_new = f_end + jnp.maximum(m_prev, cmb[last:last + 1, :])
        w_c.append(jnp.exp(f_end + m_prev - m_new))
        mp_rows.append(jnp.broadcast_to(m_prev, (chunk, LANES)))
        mn_rows.append(jnp.broadcast_to(m_new, (chunk, LANES)))
        fe_rows.append(jnp.broadcast_to(f_end, (chunk, LANES)))
        m_prev = m_new
    m_sc[...] = m_prev
    mp = jnp.concatenate(mp_rows, axis=0)
    mn = jnp.concatenate(mn_rows, axis=0)
    fe = jnp.concatenate(fe_rows, axis=0)
    big_m = jnp.maximum(mp, cmb)
    w_i = jnp.exp(mp - big_m)
    em = jnp.exp(-(f_cum + big_m))
    k_sc = jnp.exp(fe - f_cum + gts - mn)
    ex = _dot(_pack_terms([w_i, k_sc, -big_m, em], heads), e_ref[...])
    o1 = heads * EXPAND_WIDTHS[0]
    o2 = o1 + heads * EXPAND_WIDTHS[1]
    o3 = o2 + heads * EXPAND_WIDTHS[2]
    q = q_ref[...]
    k = k_ref[...] * (dk ** -0.5)
    qs = q * ex[:, :o1]
    ke = k * ex[:, o1:o2]
    neg_m = ex[:, o2:o3]
    em_v = ex[:, o3:]
    bt = b.T

    ri = lax.broadcasted_iota(I32, (chunk, chunk), 0)
    ci = lax.broadcasted_iota(I32, (chunk, chunk), 1)
    causal = ri >= ci
    lane1 = lax.broadcasted_iota(I32, (1, LANES), 1)
    rowi = lax.broadcasted_iota(I32, (LANES, LANES), 0)
    ones_v = jnp.ones((chunk, dv), BF16)
    states = [[None] * pairs for _ in range(nch)]
    for p in range(pairs):
        ps = slice(p * LANES, (p + 1) * LANES)
        cp = c_sc[p]
        nb = nb_sc[p]
        for j in range(nch):
            rows = slice(j * chunk, (j + 1) * chunk)
            states[j][p] = jnp.concatenate([cp, nb], axis=1).astype(BF16)
            vp = jnp.concatenate([v_ref[rows, 2 * p * dv:(2 * p + 2) * dv].astype(BF16), ones_v], axis=1)
            full = _dot_tn(ke[rows, ps].astype(BF16), vp)
            upd = jnp.where(rowi < dk, full[:, :dv], full[:, dv:2 * dv])
            w_col = jnp.where(rowi < dk, w_c[j][:, 2 * p:2 * p + 1], w_c[j][:, 2 * p + 1:2 * p + 2])
            cp = w_col * cp + upd
            nb = w_col * nb + full[:, 2 * dv:]
        c_sc[p] = cp
        nb_sc[p] = nb
    for j in range(nch):
        rows = slice(j * chunk, (j + 1) * chunk)
        for p in range(pairs):
            ps = slice(p * LANES, (p + 1) * LANES)
            qp = q[rows, ps]
            qsp = qs[rows, ps]
            kpb = k[rows, ps].astype(BF16)
            state = states[j][p]
            for jj in range(2):
                h = 2 * p + jj
                hs = slice(h * dv, (h + 1) * dv)
                mine = (lane1 >= jj * dk) & (lane1 < (jj + 1) * dk)
                qh = jnp.where(mine, qp, 0.0).astype(BF16)
                qsh = jnp.where(mine, qsp, 0.0).astype(BF16)
                d = neg_m[rows, h * dv:h * dv + chunk] + bt[h:h + 1, j * chunk:(j + 1) * chunk]
                s = _dot_nt(qh, kpb) * jnp.where(causal, jnp.exp(d), 0.0)
                vh = jnp.concatenate([v_ref[rows, hs].astype(BF16), ones_v], axis=1)
                out = _dot(qsh, state) + _dot(s.astype(BF16), vh)
                hh = out[:, :dv] / jnp.maximum(jnp.abs(out[:, dv:]), em_v[rows, hs])
                hn = _rms(hh, gn_ref[h:h + 1, :])
                u_ref[rows, hs] = (_sigmoid(op_ref[rows, hs]) * hn).astype(BF16)

    @pl.when(c == nblk - 1)
    def _():
        ct_ref[0] = c_sc[...]
        for p in range(pairs):
            nt_ref[0, p:p + 1, :] = nb_sc[p].T[0:1, :]
        mt_ref[0] = m_sc[...]


def _mlstm_recurrence_blocked(z, gates, c0, n0, m0, g_norm, nseq, t, chunk, tb):
    heads = g_norm.shape[0]
    v_w = heads * g_norm.shape[1]
    qk_w = v_w // 2
    nblk = t // tb
    expand = _expand_matrix(heads, EXPAND_WIDTHS)
    kern = functools.partial(_mlstm_block_kernel, chunk=chunk, tb=tb, nblk=nblk)
    row = lambda b, c: b * nblk + c
    st4 = lambda b, c: (b, 0, 0, 0)
    st3 = lambda b, c: (b, 0, 0)
    return pl.pallas_call(
        kern,
        grid=(nseq, nblk),
        in_specs=[
            pl.BlockSpec((tb, qk_w), lambda b, c: (row(b, c), 0)),
            pl.BlockSpec((tb, qk_w), lambda b, c: (row(b, c), 1)),
            pl.BlockSpec((tb, v_w), lambda b, c: (row(b, c), 1)),
            pl.BlockSpec((tb, v_w), lambda b, c: (row(b, c), 2)),
            pl.BlockSpec((tb, LANES), lambda b, c: (row(b, c), 0)),
            pl.BlockSpec(expand.shape, lambda b, c: (0, 0)),
            pl.BlockSpec((1,) + c0.shape[1:], st4),
            pl.BlockSpec((1,) + n0.shape[1:], st3),
            pl.BlockSpec((1,) + m0.shape[1:], st3),
            pl.BlockSpec(g_norm.shape, lambda b, c: (0, 0)),
        ],
        out_specs=[
            pl.BlockSpec((tb, v_w), lambda b, c: (row(b, c), 0)),
            pl.BlockSpec((1,) + c0.shape[1:], st4),
            pl.BlockSpec((1,) + n0.shape[1:], st3),
            pl.BlockSpec((1,) + m0.shape[1:], st3),
        ],
        out_shape=[
            jax.ShapeDtypeStruct((nseq * t, v_w), BF16),
            jax.ShapeDtypeStruct(c0.shape, F32),
            jax.ShapeDtypeStruct(n0.shape, F32),
            jax.ShapeDtypeStruct(m0.shape, F32),
        ],
        scratch_shapes=[
            pltpu.VMEM(c0.shape[1:], F32),
            pltpu.VMEM(c0.shape[1:], F32),
            pltpu.VMEM(m0.shape[1:], F32),
        ],
        compiler_params=_params(("arbitrary", "arbitrary")),
        name="mlstm_recurrence_t%d" % t,
    )(z, z, z, z, gates, expand, c0, n0, m0, g_norm)


def _out_proj_kernel(up_ref, us_ref, xp_ref, xs_ref, w_ref, g_ref, cast_ref,
                     xo_ref, hn_ref, cast_out_ref, *, n_first):
    cast_out_ref[...] = cast_ref[...].astype(BF16)
    i = pl.program_id(0)
    u = _pick_rows(i, n_first, up_ref, us_ref)
    x = _pick_rows(i, n_first, xp_ref, xs_ref) + _dot(u, w_ref[...])
    xo_ref[...] = x
    hn_ref[...] = _rms(x, g_ref[...]).astype(hn_ref.dtype)


def _out_proj(u_p, u_s, x_p, x_s, w, g, cast_src):
    d = x_p.shape[1]
    n = x_p.shape[0] + x_s.shape[0]
    n_first = x_p.shape[0] // TM
    cast_in, cast_out, cast_shape = _cast_side_job(cast_src, n // TM)
    return pl.pallas_call(
        functools.partial(_out_proj_kernel, n_first=n_first),
        grid=(n // TM,),
        in_specs=_split_specs(n_first, u_p.shape[1]) + _split_specs(n_first, d) + [
            pl.BlockSpec(w.shape, lambda i: (0, 0)),
            pl.BlockSpec((1, d), lambda i: (0, 0)),
            cast_in,
        ],
        out_specs=[pl.BlockSpec((TM, d), lambda i: (i, 0)), pl.BlockSpec((TM, d), lambda i: (i, 0)),
                   cast_out],
        out_shape=[jax.ShapeDtypeStruct((n, d), F32), jax.ShapeDtypeStruct((n, d), BF16), cast_shape],
        compiler_params=_params(("arbitrary",)),
        name="out_proj",
    )(u_p, u_s, x_p, x_s, w, g, cast_src)


def _out_proj_router_kernel(up_ref, us_ref, x_ref, w_ref, g_ref, wr_ref, br_ref,
                            xo_ref, hn_ref, gate_ref, idx_ref, cnt_ref, logit_sc, *, n_first, n_tiles):
    i = pl.program_id(0)

    @pl.when(i == 0)
    def _():
        logit_sc[...] = jnp.zeros_like(logit_sc)

    logits = logit_sc[(i + 1) & 1]
    u = _pick_rows(jnp.minimum(i, n_tiles - 1), n_first, up_ref, us_ref)
    x = x_ref[...] + _dot(u, w_ref[...])
    xo_ref[...] = x
    hn = _rms(x, g_ref[...])
    hn_hi = hn.astype(BF16)
    hn_ref[...] = hn_hi
    hn_lo = (hn - hn_hi.astype(F32)).astype(BF16)
    logit_sc[i & 1] = (_dot(hn_hi, wr_ref[0]) + _dot(hn_lo, wr_ref[0]) + _dot(hn_hi, wr_ref[1])
                       + br_ref[...])
    tm = logits.shape[0]
    lane = lax.broadcasted_iota(I32, logits.shape, 1)
    valid = lane < N_EXPERTS
    logits = jnp.where(valid, logits, -jnp.inf)
    ex = jnp.exp(logits - jnp.max(logits, axis=-1, keepdims=True))
    probs = jnp.where(valid, ex / jnp.sum(ex, axis=-1, keepdims=True), -1.0)
    p1 = jnp.max(probs, axis=-1, keepdims=True)
    i1 = jnp.min(jnp.where(probs == p1, lane, LANES), axis=-1, keepdims=True)
    rest = jnp.where(lane == i1, -1.0, probs)
    p2 = jnp.max(rest, axis=-1, keepdims=True)
    i2 = jnp.min(jnp.where(rest == p2, lane, LANES), axis=-1, keepdims=True)
    tot = p1 + p2
    onehot = ((lane == i1) | (lane == i2)).astype(BF16)
    rr = lax.broadcasted_iota(I32, (tm, tm), 0)
    cc = lax.broadcasted_iota(I32, (tm, tm), 1)
    strict = (rr > cc).astype(BF16)
    before = _dot(strict, onehot)
    r1 = jnp.sum(jnp.where(lane == i1, before, 0.0), axis=-1, keepdims=True)
    r2 = jnp.sum(jnp.where(lane == i2, before, 0.0), axis=-1, keepdims=True)
    gate_ref[...] = jnp.where(lane == 0, p1 / tot, jnp.where(lane == 1, p2 / tot, 0.0))
    idx_ref[...] = jnp.where(lane == 0, i1,
                             jnp.where(lane == 1, i2,
                                       jnp.where(lane == 2, r1.astype(I32),
                                                 jnp.where(lane == 3, r2.astype(I32), 0))))
    cnt_ref[0] = jnp.sum(onehot.astype(F32), axis=0, keepdims=True)


def _out_proj_router(u_p, u_s, x, w, g, w_router, b_router):
    n, d = x.shape
    n_first = u_p.shape[0] // TM
    n_tiles = n // TM
    row = lambda i: (jnp.minimum(i, n_tiles - 1), 0)
    routed = lambda i: (jnp.maximum(i - 1, 0), 0)
    fix = lambda i: (0, 0)
    u_specs = [pl.BlockSpec((TM, u_p.shape[1]), lambda i: (jnp.minimum(i, n_first - 1), 0)),
               pl.BlockSpec((TM, u_p.shape[1]),
                            lambda i: (jnp.clip(i - n_first, 0, n_tiles - n_first - 1), 0))]
    return pl.pallas_call(
        functools.partial(_out_proj_router_kernel, n_first=n_first, n_tiles=n_tiles),
        grid=(n_tiles + 1,),
        in_specs=u_specs + [
            pl.BlockSpec((TM, d), row),
            pl.BlockSpec(w.shape, fix),
            pl.BlockSpec((1, d), fix),
            pl.BlockSpec(w_router.shape, lambda i: (0, 0, 0)),
            pl.BlockSpec((1, LANES), fix),
        ],
        out_specs=[
            pl.BlockSpec((TM, d), row),
            pl.BlockSpec((TM, d), row),
            pl.BlockSpec((TM, LANES), routed),
            pl.BlockSpec((TM, LANES), routed),
            pl.BlockSpec((1, 1, LANES), lambda i: (jnp.maximum(i - 1, 0), 0, 0)),
        ],
        out_shape=[
            jax.ShapeDtypeStruct((n, d), F32),
            jax.ShapeDtypeStruct((n, d), BF16),
            jax.ShapeDtypeStruct((n, LANES), F32),
            jax.ShapeDtypeStruct((n, LANES), I32),
            jax.ShapeDtypeStruct((n_tiles, 1, LANES), F32),
        ],
        scratch_shapes=[pltpu.VMEM((2, TM, LANES), F32)],
        compiler_params=_params(("arbitrary",)),
        name="out_proj_router",
    )(u_p, u_s, x, w, g, w_router, b_router)


def _swiglu_kernel(hn_ref, x_ref, wg_ref, wu_ref, wd_ref, g_ref, cast_ref, xo_ref, ho_ref, cast_out_ref,
                   *, nchunk):
    cast_out_ref[...] = cast_ref[...].astype(BF16)
    hn = hn_ref[...]
    width = wd_ref.shape[0] // nchunk
    x = x_ref[...]
    for c in range(nchunk):
        cs = slice(c * width, (c + 1) * width)
        gate = _dot(hn, wg_ref[:, cs])
        up = _dot(hn, wu_ref[:, cs])
        act = (gate * _sigmoid(gate) * up).astype(BF16)
        x = x + _dot(act, wd_ref[cs, :])
    xo_ref[...] = x
    ho_ref[...] = _rms(x, g_ref[...]).astype(BF16)


def _swiglu(hn, x, w_gu, w_down, g, nchunk, cast_src):
    n, d = x.shape
    dff = w_down.shape[0]
    kern = functools.partial(_swiglu_kernel, nchunk=nchunk)
    row = lambda i: (i, 0)
    cast_in, cast_out, cast_shape = _cast_side_job(cast_src, n // TM)
    return pl.pallas_call(
        kern,
        grid=(n // TM,),
        in_specs=[
            pl.BlockSpec((TM, d), row),
            pl.BlockSpec((TM, d), row),
            _resident_spec((d, dff), lambda i: (0, 0)),
            _resident_spec((d, dff), lambda i: (0, 1)),
            _resident_spec((dff, d), lambda i: (0, 0)),
            pl.BlockSpec((1, d), lambda i: (0, 0)),
            cast_in,
        ],
        out_specs=[pl.BlockSpec((TM, d), row), pl.BlockSpec((TM, d), row), cast_out],
        out_shape=[jax.ShapeDtypeStruct((n, d), F32), jax.ShapeDtypeStruct((n, d), BF16), cast_shape],
        compiler_params=_params(("arbitrary",)),
        name="dense_swiglu",
    )(hn, x, w_gu, w_gu, w_down, g, cast_src)


def _row_copy(src_ref, src_row, dst_ref, dst_row, sem):
    return pltpu.make_async_copy(src_ref.at[pl.ds(src_row, 1), :], dst_ref.at[pl.ds(dst_row, 1), :], sem)


SUBLANES = 8


def _zero_rows(zero_ref, dst_ref, dst_row, count, limit, sem, wait):
    head = jnp.minimum((-dst_row) & (SUBLANES - 1), count)
    for j in range(SUBLANES - 1):
        @pl.when(j < head)
        def _(j=j):
            copy = _row_copy(zero_ref, 0, dst_ref, dst_row + j, sem)
            copy.wait() if wait else copy.start()

    rest = count - head
    first = dst_row + head
    for bit in range(SUBLANES.bit_length() - 1, limit.bit_length() - 1):
        size = 1 << bit
        done = (rest >> (bit + 1)) << (bit + 1)

        @pl.when(((rest >> bit) & 1) == 1)
        def _(size=size, done=done):
            start = pl.multiple_of(first + done, SUBLANES)
            copy = pltpu.make_async_copy(zero_ref.at[pl.ds(0, size), :],
                                         dst_ref.at[pl.ds(start, size), :], sem)
            copy.wait() if wait else copy.start()


def _expert_kernel(te_ref, tv_ref, xs_ref, wg_ref, wu_ref, wd_ref, ys_ref, *, nchunk):
    i = pl.program_id(0)
    half = xs_ref.shape[0] // 2
    width = wg_ref.shape[2] // nchunk

    def swiglu_rows(rows):
        x = xs_ref[rows, :].astype(BF16)
        acc = None
        for c in range(nchunk):
            cs = slice(c * width, (c + 1) * width)
            gate = _dot(x, wg_ref[0, :, cs])
            up = _dot(x, wu_ref[0, :, cs])
            act = (gate * _sigmoid(gate) * up).astype(BF16)
            part = _dot(act, wd_ref[0, cs, :])
            acc = part if acc is None else acc + part
        ys_ref[rows, :] = acc.astype(BF16).astype(F32)

    @pl.when(tv_ref[i] == 2)
    def _():
        swiglu_rows(slice(None))

    @pl.when(tv_ref[i] == 1)
    def _():
        swiglu_rows(slice(0, half))
        ys_ref[half:, :] = jnp.zeros((half, ys_ref.shape[1]), F32)

    @pl.when(tv_ref[i] == 0)
    def _():
        ys_ref[...] = jnp.zeros_like(ys_ref)


def _experts(tile_expert, tile_valid, xs, w_gu, w_down):
    r, d = xs.shape
    n_tiles = r // TM_EXPERT
    dff = w_down.shape[1]
    kern = functools.partial(_expert_kernel, nchunk=EXPERT_CHUNKS)
    return pl.pallas_call(
        kern,
        grid_spec=pltpu.PrefetchScalarGridSpec(
            num_scalar_prefetch=2,
            grid=(r // TM_EXPERT,),
            in_specs=[
                pl.BlockSpec((TM_EXPERT, d), lambda i, te, tv: (jnp.minimum(i, tv[n_tiles]), 0)),
                pl.BlockSpec((1, d, dff), lambda i, te, tv: (te[i], 0, 0)),
                pl.BlockSpec((1, d, dff), lambda i, te, tv: (te[i], 0, 1)),
                pl.BlockSpec((1, dff, d), lambda i, te, tv: (te[i], 0, 0)),
            ],
            out_specs=pl.BlockSpec((TM_EXPERT, d), lambda i, te, tv: (i, 0)),
        ),
        out_shape=jax.ShapeDtypeStruct((r, d), F32),
        compiler_params=_params(("arbitrary",)),
        name="moe_experts",
    )(tile_expert, tile_valid, xs, w_gu, w_gu, w_down)


RUN_BUF_ROWS = 2 * TM + LANES


def _run_positions(idx, off_row):
    lane = lax.broadcasted_iota(I32, idx.shape, 1)
    pos = []
    for slot in range(2):
        off = jnp.sum(jnp.where(lane == idx[:, slot:slot + 1], off_row, 0.0), axis=-1, keepdims=True)
        pos.append(off.astype(I32) + idx[:, 2 + slot:3 + slot])
    return pos


def _run_copies(tab_ref, step, hbm_ref, buf_ref, slot, sem, to_hbm, wait):
    for e in range(N_EXPERTS):
        entry = (step * N_EXPERTS + e) * 3
        off, row, length = tab_ref[entry], tab_ref[entry + 1], tab_ref[entry + 2]
        for bit in range(SUBLANES.bit_length() - 1, TM.bit_length()):
            size = 1 << bit
            done = (length >> (bit + 1)) << (bit + 1)

            @pl.when(((length >> bit) & 1) == 1)
            def _(size=size, done=done, off=off, row=row):
                in_buf = buf_ref.at[slot, pl.ds(pl.multiple_of(off + done, SUBLANES), size), :]
                in_hbm = hbm_ref.at[pl.ds(pl.multiple_of(row + done, SUBLANES), size), :]
                copy = (pltpu.make_async_copy(in_buf, in_hbm, sem.at[slot]) if to_hbm
                        else pltpu.make_async_copy(in_hbm, in_buf, sem.at[slot]))
                copy.wait() if wait else copy.start()


def _run_dispatch_kernel(tab_ref, pad_ref, hn_ref, idx_ref, off_ref, xs_ref, buf, zero_sc, sem, zsem,
                         *, n_steps):
    j = pl.program_id(0)
    slot = j & 1

    @pl.when(j == 0)
    def _():
        zero_sc[...] = jnp.zeros_like(zero_sc)
        rows = zero_sc.shape[0]
        for wait in (False, True):
            for e in range(N_EXPERTS):
                _zero_rows(zero_sc, xs_ref, pad_ref[2 * e], pad_ref[2 * e + 1], 2 * rows, zsem, wait)

            def tail(t, carry, wait=wait):
                first = pl.multiple_of(pad_ref[2 * N_EXPERTS] + t * rows, SUBLANES)
                copy = pltpu.make_async_copy(zero_sc, xs_ref.at[pl.ds(first, rows), :], zsem)
                copy.wait() if wait else copy.start()
                return carry

            lax.fori_loop(0, pad_ref[2 * N_EXPERTS + 1] // rows, tail, 0)

    @pl.when(j >= 2)
    def _():
        _run_copies(tab_ref, j - 2, xs_ref, buf, slot, sem, True, True)

    pos0, pos1 = _run_positions(idx_ref[...], off_ref[0])
    lane = lax.broadcasted_iota(I32, (hn_ref.shape[0], buf.shape[1]), 1)
    sel = jnp.where((lane == pos0) | (lane == pos1), 1.0, 0.0).astype(BF16)
    buf[slot] = _dot_tn(sel, hn_ref[...])
    _run_copies(tab_ref, j, xs_ref, buf, slot, sem, True, False)

    @pl.when(j == n_steps - 1)
    def _():
        _run_copies(tab_ref, j - 1, xs_ref, buf, 1 - slot, sem, True, True)
        _run_copies(tab_ref, j, xs_ref, buf, slot, sem, True, True)


def _run_dispatch(run_table, pad_table, hn, idx, run_offsets, n_rows):
    n, d = hn.shape
    n_steps = n // TM
    assert n_steps >= 2
    return pl.pallas_call(
        functools.partial(_run_dispatch_kernel, n_steps=n_steps),
        grid_spec=pltpu.PrefetchScalarGridSpec(
            num_scalar_prefetch=2,
            grid=(n_steps,),
            in_specs=[
                pl.BlockSpec((TM, d), lambda i, *_: (i, 0)),
                pl.BlockSpec((TM, LANES), lambda i, *_: (i, 0)),
                pl.BlockSpec((1, 1, LANES), lambda i, *_: (i, 0, 0)),
            ],
            out_specs=pl.BlockSpec(memory_space=pl.ANY),
            scratch_shapes=[pltpu.VMEM((2, RUN_BUF_ROWS, d), F32),
                            pltpu.VMEM((TM_EXPERT // 2, d), F32),
                            pltpu.SemaphoreType.DMA((2,)), pltpu.SemaphoreType.DMA(())],
        ),
        out_shape=jax.ShapeDtypeStruct((n_rows, d), F32),
        compiler_params=_params(("arbitrary",)),
        name="moe_dispatch",
    )(run_table, pad_table, hn, idx, run_offsets)


def _run_combine_kernel(tab_ref, ys_ref, x_ref, gate_ref, idx_ref, off_ref, g_ref, yp_ref, ysm_ref,
                        buf, sem, *, n_first, n_steps):
    j = pl.program_id(0)
    slot = j & 1

    @pl.when(j == 0)
    def _():
        buf[...] = jnp.zeros_like(buf)
        _run_copies(tab_ref, 0, ys_ref, buf, 0, sem, False, False)

    @pl.when(j + 1 < n_steps)
    def _():
        _run_copies(tab_ref, j + 1, ys_ref, buf, 1 - slot, sem, False, False)

    _run_copies(tab_ref, j, ys_ref, buf, slot, sem, False, True)
    runs = buf[slot].astype(BF16)
    pos = _run_positions(idx_ref[...], off_ref[0])
    lane = lax.broadcasted_iota(I32, (x_ref.shape[0], buf.shape[1]), 1)
    gate = gate_ref[...]
    y = x_ref[...]
    for k in range(2):
        sel = jnp.where(lane == pos[k], 1.0, 0.0).astype(BF16)
        y = y + gate[:, k:k + 1] * _dot(sel, runs)
    y = _rms(y, g_ref[...])

    @pl.when(j < n_first)
    def _():
        yp_ref[...] = y

    @pl.when(j >= n_first)
    def _():
        ysm_ref[...] = y


def _run_combine(run_table, ys, x, gates, idx, run_offsets, g, n_p):
    n, d = x.shape
    n_first = n_p // TM
    n_steps = n // TM
    p_spec, s_spec = _split_specs(n_first, d)
    row = lambda i, *_: (i, 0)
    return pl.pallas_call(
        functools.partial(_run_combine_kernel, n_first=n_first, n_steps=n_steps),
        grid_spec=pltpu.PrefetchScalarGridSpec(
            num_scalar_prefetch=1,
            grid=(n_steps,),
            in_specs=[
                pl.BlockSpec(memory_space=pl.ANY),
                pl.BlockSpec((TM, d), row),
                pl.BlockSpec((TM, LANES), row),
                pl.BlockSpec((TM, LANES), row),
                pl.BlockSpec((1, 1, LANES), lambda i, *_: (i, 0, 0)),
                pl.BlockSpec((1, d), lambda i, *_: (0, 0)),
            ],
            out_specs=[p_spec, s_spec],
            scratch_shapes=[pltpu.VMEM((2, RUN_BUF_ROWS, d), F32), pltpu.SemaphoreType.DMA((2,))],
        ),
        out_shape=[jax.ShapeDtypeStruct((n_p, d), F32), jax.ShapeDtypeStruct((n - n_p, d), F32)],
        compiler_params=_params(("arbitrary",)),
        name="moe_combine",
    )(run_table, ys, x, gates, idx, run_offsets, g)


def _run_tables(tile_counts, n_rows):
    cnt = tile_counts[:, 0, :N_EXPERTS].astype(I32)
    run = ((cnt + SUBLANES - 1) // SUBLANES) * SUBLANES
    off = jnp.cumsum(run, axis=1) - run
    used = jnp.sum(run, axis=0)
    padded = ((used + TM_EXPERT - 1) // TM_EXPERT) * TM_EXPERT
    ends = jnp.cumsum(padded)
    starts = ends - padded
    row = starts[None, :] + jnp.cumsum(run, axis=0) - run
    run_table = jnp.stack([off, row, run], axis=2).reshape(-1)
    run_offsets = jnp.pad(off.astype(F32), ((0, 0), (0, LANES - N_EXPERTS)))[:, None, :]
    n_tiles = n_rows // TM_EXPERT
    used_tiles = ends[-1] // TM_EXPERT
    pad_table = jnp.stack([starts + used, padded - used], axis=1).reshape(-1)
    pad_table = jnp.concatenate([pad_table, jnp.stack([ends[-1], n_rows - ends[-1]])])
    tile_start = jnp.arange(n_tiles, dtype=I32) * TM_EXPERT
    tile_expert = jnp.sum((tile_start[:, None] >= ends[None, :]).astype(I32), axis=1)
    last_expert = jnp.sum((ends[-1] - 1 >= ends).astype(I32))
    tile_expert = jnp.minimum(tile_expert, last_expert)
    used_end = (starts + used)[tile_expert]
    half = TM_EXPERT // 2
    tile_rows = jnp.clip(used_end - tile_start, 0, TM_EXPERT)
    tile_valid = jnp.where(tile_start < ends[-1], (tile_rows + half - 1) // half, 0)
    tile_valid = jnp.concatenate([tile_valid, (used_tiles - 1)[None]])
    return run_table, run_offsets, pad_table, tile_expert, tile_valid


def _pad_time(a, nseq, t):
    w = a.shape[1]
    a = jnp.pad(a.reshape(nseq, t, w), ((0, 0), (0, SAMPLE_PAD_T - t), (0, 0)))
    return a.reshape(nseq * SAMPLE_PAD_T, w)


def kernel(x_prompt, x_sample, state_gla_S, state_mlstm_C, state_mlstm_n, state_mlstm_m,
           norm_mix, norm_ffn, norm_final,
           gla_w_in, gla_w_a2, gla_b_a, gla_g_norm, gla_w_out,
           mlstm_w_in, mlstm_b_gate, mlstm_g_norm, mlstm_w_out,
           ffn_w_gu, ffn_w_down,
           moe_w_router, moe_b_router, moe_w_gu, moe_w_down):
    bp, tp, d = x_prompt.shape
    bs, ts, _ = x_sample.shape
    n_p = bp * tp
    n_s = bs * ts
    n = n_p + n_s
    assert n_p % TM == 0 and n_s % TM == 0 and tp % (2 * CHUNK) == 0 and ts <= SAMPLE_PAD_T
    assert norm_mix.shape[0] == 2, "one GLA layer followed by one mLSTM layer"
    qk_w = d // 2
    main_w = 2 * qk_w + 2 * d

    x_p = x_prompt.reshape(n_p, d)
    x_s = x_sample.reshape(n_s, d)

    w_a2 = jnp.pad(gla_w_a2[0], ((0, LANES - GLA_RANK), (0, 0))).astype(BF16)
    z, log_a, ffn_wgu = _gla_in_proj(x_p, _pad_time(x_s, bs, ts), norm_mix[0][None], gla_w_in[0].T,
                                     main_w, w_a2, gla_b_a[0][None], ffn_w_gu[0])
    s0_p = jnp.zeros((bp,) + state_gla_S.shape[2:], F32)
    u_p, s_p = _gla_recurrence(z, log_a, s0_p, gla_g_norm[0], bp, tp, CHUNK, 16 * CHUNK)
    u_s, s_s = _gla_recurrence_short(z, log_a, state_gla_S[0], gla_g_norm[0], bs, SAMPLE_PAD_T, ts,
                                     SAMPLE_SEQS_PER_STEP, n_p)
    x, hn, ffn_wd = _out_proj(u_p, u_s, x_p, x_s, gla_w_out[0].astype(BF16),
                              norm_ffn[0][None], ffn_w_down[0])
    x, hn, moe_wgu = _swiglu(hn, x, ffn_wgu, ffn_wd,
                             norm_mix[1][None], SWIGLU_CHUNKS,
                             moe_w_gu[0].reshape(N_EXPERTS * d, moe_w_gu.shape[3]))

    b_gt = jnp.pad(mlstm_b_gate[0], (0, LANES - 2 * MLSTM_HEADS))[None]
    dff_e = moe_w_down.shape[2]
    z, gates, moe_wd = _mlstm_in_proj(hn, n_p, _pad_time(hn[n_p:], bs, ts), mlstm_w_in[0].T, main_w,
                                      b_gt, moe_w_down[0].reshape(N_EXPERTS * dff_e, d))
    pairs = MLSTM_HEADS // 2
    dk2 = 2 * state_mlstm_C.shape[3]
    dvm = state_mlstm_C.shape[4]
    c0_p = jnp.zeros((bp, pairs, dk2, dvm), F32)
    n0_p = jnp.zeros((bp, pairs, dk2), F32)
    m0_p = jnp.zeros((bp, 1, LANES), F32)
    u_p, c_p, nn_p, m_p = _mlstm_recurrence_blocked(z, gates, c0_p, n0_p, m0_p,
                                                    mlstm_g_norm[0], bp, tp, CHUNK, 16 * CHUNK)
    c0_s = state_mlstm_C[0]
    n0_s = state_mlstm_n[0].reshape(bs, pairs, dk2)
    m0_s = jnp.pad(state_mlstm_m[0], ((0, 0), (0, LANES - MLSTM_HEADS)))[:, None, :]
    u_s, c_s, nn_s, m_s = _mlstm_recurrence_short(z, gates, c0_s, n0_s, m0_s, mlstm_g_norm[0], bs,
                                                  SAMPLE_PAD_T, ts, SAMPLE_SEQS_PER_STEP, n_p)

    w_r = jnp.pad(moe_w_router[0], ((0, 0), (0, LANES - N_EXPERTS)))
    w_r_hi = w_r.astype(BF16)
    w_r = jnp.stack([w_r_hi, (w_r - w_r_hi.astype(F32)).astype(BF16)])
    b_r = jnp.pad(moe_b_router[0], (0, LANES - N_EXPERTS))[None]
    x, hn, route_g, route_i, tile_counts = _out_proj_router(
        u_p, u_s, x, mlstm_w_out[0].astype(BF16), norm_ffn[1][None], w_r, b_r)
    n_rows = 2 * n + (n // TM) * N_EXPERTS * (SUBLANES - 1) + N_EXPERTS * (TM_EXPERT - 1)
    n_rows = -(-n_rows // TM_EXPERT) * TM_EXPERT
    run_table, run_offsets, pad_table, tile_expert, tile_valid = _run_tables(tile_counts, n_rows)
    xs = _run_dispatch(run_table, pad_table, hn, route_i, run_offsets, n_rows)
    ys = _experts(tile_expert, tile_valid, xs, moe_wgu.reshape(moe_w_gu.shape[1:]),
                  moe_wd.reshape(moe_w_down.shape[1:]))
    y_p, y_s = _run_combine(run_table, ys, x, route_g, route_i, run_offsets, norm_final[None], n_p)

    y_prompt = y_p.reshape(bp, tp, d)
    y_sample = y_s.reshape(bs, ts, d)
    hd = state_mlstm_C.shape[2:]
    return (y_prompt, y_sample,
            s_p[None], c_p.reshape((1, bp) + hd), nn_p.reshape(1, bp, hd[0], hd[1]),
            m_p[None, :, 0, :MLSTM_HEADS],
            s_s[None], c_s[None], nn_s.reshape(1, bs, hd[0], hd[1]), m_s[None, :, 0, :MLSTM_HEADS])


def _mlstm_in_proj(hn, n_p, hn_s, w_in, wz, b_gate, cast_src):
    d = hn.shape[1]
    n = n_p + hn_s.shape[0]
    n_first = n_p // TM
    cast_in, cast_out, cast_shape = _cast_side_job(cast_src, n // TM)

    def kern(hp_ref, hs_ref, w_ref, wg_ref, b_ref, cast_ref, z_ref, gate_ref, cast_out_ref,
             wb_sc, wg_sc):
        @pl.when(pl.program_id(0) == 0)
        def _():
            wb_sc[...] = w_ref[...].T.astype(BF16)
            wg_sc[...] = _gate_weights(wg_ref)

        cast_out_ref[...] = cast_ref[...].astype(BF16)
        h = _pick_rows(pl.program_id(0), n_first, hp_ref, hs_ref)
        width = wz // IN_PROJ_COL_CHUNKS
        for c in range(IN_PROJ_COL_CHUNKS):
            cs = slice(c * width, (c + 1) * width)
            z_ref[:, cs] = _dot(h, wb_sc[:, cs])
        gp = _dot(h, wg_sc[...]) + b_ref[...]
        gc = GATE_CAP * jnp.tanh(gp * (1.0 / GATE_CAP))
        lane = lax.broadcasted_iota(I32, gc.shape, 1)
        out = jnp.where(lane < MLSTM_HEADS, gc, _log_sigmoid(gc))
        gate_ref[...] = jnp.where(lane < 2 * MLSTM_HEADS, out, 0.0)

    return pl.pallas_call(
        kern,
        grid=(n // TM,),
        in_specs=_split_specs(n_first, d) + [
            _resident_spec((wz, d), lambda i: (0, 0)),
            _gate_rows_spec(w_in, wz),
            pl.BlockSpec((1, LANES), lambda i: (0, 0)),
            cast_in,
        ],
        out_specs=[pl.BlockSpec((TM, wz), lambda i: (i, 0)), pl.BlockSpec((TM, LANES), lambda i: (i, 0)),
                   cast_out],
        out_shape=[jax.ShapeDtypeStruct((n, wz), F32), jax.ShapeDtypeStruct((n, LANES), F32),
                   cast_shape],
        scratch_shapes=[pltpu.VMEM((d, wz), BF16), pltpu.VMEM((d, LANES), BF16)],
        compiler_params=_params(("arbitrary",)),
        name="in_proj_mlstm",
    )(hn, hn_s, w_in, w_in, b_gate, cast_src)
```

```python
import functools

import jax
import jax.numpy as jnp
import numpy as np
from jax import lax
from jax.experimental import pallas as pl
from jax.experimental.pallas import tpu as pltpu

F32 = jnp.float32
BF16 = jnp.bfloat16
I32 = jnp.int32

EPS = 1e-6
GLA_HEADS = 4
GLA_RANK = 16
GLA_TAU = 16.0
MLSTM_HEADS = 8
GATE_CAP = 15.0
CHUNK = 64
N_EXPERTS = 8
NEG_BIG = -1e30

LANES = 128
TM = 512
TM_EXPERT = 512
SAMPLE_PAD_T = 8
SAMPLE_SEQS_PER_STEP = 16
SWIGLU_CHUNKS = 11
EXPERT_CHUNKS = 7
VMEM_LIMIT = 56 * 1024 * 1024


def _dot(a, b):
    return jnp.dot(a, b, preferred_element_type=F32)


def _dot_nt(a, b):
    return lax.dot_general(a, b, (((1,), (1,)), ((), ())), preferred_element_type=F32)


def _dot_tn(a, b):
    return lax.dot_general(a, b, (((0,), (0,)), ((), ())), preferred_element_type=F32)


def _sigmoid(x):
    return 1.0 / (1.0 + jnp.exp(-x))


def _log_sigmoid(x):
    return jnp.minimum(x, 0.0) - jnp.log(1.0 + jnp.exp(-jnp.abs(x)))


def _rms(x, g):
    return x * lax.rsqrt(jnp.mean(x * x, axis=-1, keepdims=True) + EPS) * g


def _cumsum_rows(x, n):
    ridx = lax.broadcasted_iota(I32, x.shape, 0)
    s = 1
    while s < n:
        x = x + jnp.where(ridx >= s, pltpu.roll(x, s, 0), 0.0)
        s *= 2
    return x


def _params(sem):
    return pltpu.CompilerParams(dimension_semantics=sem, vmem_limit_bytes=VMEM_LIMIT)


def _resident_spec(block_shape, index_map):
    return pl.BlockSpec(block_shape, index_map, pipeline_mode=pl.Buffered(1))


def _cast_side_job(src, n_steps):
    rows, cols = src.shape
    n_slabs = max(s for s in range(1, n_steps + 1) if rows % (16 * s) == 0)
    spec = pl.BlockSpec((rows // n_slabs, cols), lambda i, *_: (jnp.minimum(i, n_slabs - 1), 0))
    return spec, spec, jax.ShapeDtypeStruct(src.shape, BF16)


IN_PROJ_COL_CHUNKS = 4
ROW_PARTS = 2


def _pick_rows(i, n_first, first_ref, second_ref):
    return jnp.where(i < n_first, first_ref[...], second_ref[...])


def _split_specs(n_first, width):
    return [pl.BlockSpec((TM, width), lambda i, *_: (jnp.minimum(i, n_first - 1), 0)),
            pl.BlockSpec((TM, width), lambda i, *_: (jnp.maximum(i - n_first, 0), 0))]


def _gate_weights(wg_ref):
    rows, d = wg_ref.shape
    full = jnp.concatenate([wg_ref[...], jnp.zeros((LANES - rows, d), F32)], axis=0)
    return full.T.astype(BF16)


def _gla_in_proj_kernel(xp_ref, xs_ref, g_ref, w_ref, wg_ref, w2_ref, b_ref, cast_ref,
                        z_ref, gate_ref, cast_out_ref, wb_sc, wg_sc, *, n_first):
    @pl.when(pl.program_id(0) == 0)
    def _():
        wb_sc[...] = w_ref[...].T.astype(BF16)
        wg_sc[...] = _gate_weights(wg_ref)

    cast_out_ref[...] = cast_ref[...].astype(BF16)
    first = pl.program_id(0) < n_first
    width = wb_sc.shape[1] // IN_PROJ_COL_CHUNKS
    part = xp_ref.shape[0] // ROW_PARTS
    for r in range(ROW_PARTS):
        rows = slice(r * part, (r + 1) * part)
        x = jnp.where(first, xp_ref[rows, :], xs_ref[rows, :])
        hn = _rms(x, g_ref[...]).astype(BF16)
        for c in range(IN_PROJ_COL_CHUNKS):
            cs = slice(c * width, (c + 1) * width)
            z_ref[rows, cs] = _dot(hn, wb_sc[:, cs])
        a = _dot(hn, wg_sc[...])
        la = _dot(a.astype(BF16), w2_ref[...]) + b_ref[...]
        gate_ref[rows, :] = _log_sigmoid(la) * (1.0 / GLA_TAU)


def _gate_rows_spec(w_in, wz):
    rows = w_in.shape[0] - wz
    assert wz % rows == 0 and rows % SUBLANES == 0
    return pl.BlockSpec((rows, w_in.shape[1]), lambda i: (wz // rows, 0))


def _gla_in_proj(x_p, x_s, g, w_in, wz, w2, b, cast_src):
    d = x_p.shape[1]
    n = x_p.shape[0] + x_s.shape[0]
    n_first = x_p.shape[0] // TM
    gw = b.shape[1]
    cast_in, cast_out, cast_shape = _cast_side_job(cast_src, n // TM)
    return pl.pallas_call(
        functools.partial(_gla_in_proj_kernel, n_first=n_first),
        grid=(n // TM,),
        in_specs=_split_specs(n_first, d) + [
            pl.BlockSpec((1, d), lambda i: (0, 0)),
            _resident_spec((wz, d), lambda i: (0, 0)),
            _gate_rows_spec(w_in, wz),
            pl.BlockSpec(w2.shape, lambda i: (0, 0)),
            pl.BlockSpec((1, gw), lambda i: (0, 0)),
            cast_in,
        ],
        out_specs=[
            pl.BlockSpec((TM, wz), lambda i: (i, 0)),
            pl.BlockSpec((TM, gw), lambda i: (i, 0)),
            cast_out,
        ],
        out_shape=[jax.ShapeDtypeStruct((n, wz), F32), jax.ShapeDtypeStruct((n, gw), F32), cast_shape],
        scratch_shapes=[pltpu.VMEM((d, wz), BF16), pltpu.VMEM((d, LANES), BF16)],
        compiler_params=_params(("arbitrary",)),
        name="in_proj_gla",
    )(x_p, x_s, g, w_in, w_in, w2, b, cast_src)


def _gla_kernel(q_ref, k_ref, v_ref, r_ref, la_ref, s0_ref, gn_ref, u_ref, st_ref, st_sc,
                *, chunk, tb, nblk):
    heads = GLA_HEADS
    dk = q_ref.shape[1] // heads
    dv = v_ref.shape[1] // heads
    c = pl.program_id(1)

    @pl.when(c == 0)
    def _():
        for h in range(heads):
            st_sc[h] = s0_ref[0, h].T

    ri = lax.broadcasted_iota(I32, (chunk, chunk), 0)
    ci = lax.broadcasted_iota(I32, (chunk, chunk), 1)
    causal = ri >= ci
    kscale = dk ** -0.5
    state = [st_sc[h] for h in range(heads)]
    for j in range(tb // chunk):
        rows = slice(j * chunk, (j + 1) * chunk)
        b_all = _cumsum_rows(la_ref[rows, :], chunk)
        for h in range(heads):
            ks = slice(h * dk, (h + 1) * dk)
            vs = slice(h * dv, (h + 1) * dv)
            b = b_all[:, ks]
            b_end = b[chunk - 1:chunk, :]
            q = q_ref[rows, ks]
            k = k_ref[rows, ks] * kscale
            qg = (q * jnp.exp(b)).astype(BF16)
            kg = (k * jnp.exp(-b)).astype(BF16)
            ke = (k * jnp.exp(b_end - b)).astype(BF16)
            v = v_ref[rows, vs].astype(BF16)
            a = jnp.where(causal, _dot_nt(qg, kg), 0.0).astype(BF16)
            o = _dot_nt(qg, state[h].astype(BF16)) + _dot(a, v)
            state[h] = state[h] * jnp.exp(b_end) + _dot_tn(v, ke)
            on = _rms(o, gn_ref[h:h + 1, :])
            r = r_ref[rows, vs]
            u_ref[rows, vs] = (r * _sigmoid(r) * on).astype(BF16)
    for h in range(heads):
        st_sc[h] = state[h]

    @pl.when(c == nblk - 1)
    def _():
        for h in range(heads):
            st_ref[0, h] = st_sc[h].T


def _gla_recurrence(z, log_a, s0, g_norm, nseq, t, chunk, tb):
    qk_w = log_a.shape[1]
    v_w = 2 * qk_w
    nblk = t // tb
    heads, dk, dv = s0.shape[1:]
    kern = functools.partial(_gla_kernel, chunk=chunk, tb=tb, nblk=nblk)
    row = lambda b, c: b * nblk + c
    return pl.pallas_call(
        kern,
        grid=(nseq, nblk),
        in_specs=[
            pl.BlockSpec((tb, qk_w), lambda b, c: (row(b, c), 0)),
            pl.BlockSpec((tb, qk_w), lambda b, c: (row(b, c), 1)),
            pl.BlockSpec((tb, v_w), lambda b, c: (row(b, c), 1)),
            pl.BlockSpec((tb, v_w), lambda b, c: (row(b, c), 2)),
            pl.BlockSpec((tb, qk_w), lambda b, c: (row(b, c), 0)),
            pl.BlockSpec((1, heads, dk, dv), lambda b, c: (b, 0, 0, 0)),
            pl.BlockSpec((heads, dv), lambda b, c: (0, 0)),
        ],
        out_specs=[
            pl.BlockSpec((tb, v_w), lambda b, c: (row(b, c), 0)),
            pl.BlockSpec((1, heads, dk, dv), lambda b, c: (b, 0, 0, 0)),
        ],
        out_shape=[
            jax.ShapeDtypeStruct((nseq * t, v_w), BF16),
            jax.ShapeDtypeStruct(s0.shape, F32),
        ],
        scratch_shapes=[pltpu.VMEM((heads, dv, dk), F32)],
        compiler_params=_params(("arbitrary", "arbitrary")),
        name="gla_recurrence_t%d" % t,
    )(z, z, z, z, log_a, s0, g_norm)


def _real_rows_selector(nsq, t, t_real):
    i = lax.broadcasted_iota(I32, (nsq * t_real, nsq * t), 0)
    r = lax.broadcasted_iota(I32, (nsq * t_real, nsq * t), 1)
    return jnp.where(r == (i // t_real) * t + i % t_real, 1.0, 0.0).astype(BF16)


def _gla_short_kernel(q_ref, k_ref, v_ref, r_ref, la_ref, s0_ref, gn_ref, u_ref, st_ref,
                      *, t, t_real, nsq):
    heads = GLA_HEADS
    dk = q_ref.shape[1] // heads
    dv = v_ref.shape[1] // heads
    nrow = nsq * t
    width = q_ref.shape[1]
    rc = lax.broadcasted_iota(I32, (nrow, width), 0) & (t - 1)
    log_a = jnp.where(rc < t_real, la_ref[...], 0.0)
    b = _chunk_scan(log_a, rc, t, jnp.add, 0.0)
    pick = _real_rows_selector(nsq, t, t_real)
    b_last = b.reshape(nsq, t, width)[:, t - 1:t, :]
    b_end = jnp.broadcast_to(b_last, (nsq, t, width)).reshape(nrow, width)
    k = k_ref[...] * (dk ** -0.5)
    qg = q_ref[...] * jnp.exp(b)
    kg = k * jnp.exp(-b)
    ke = k * jnp.exp(b_end - b)
    decay = jnp.exp(b_end)
    ri = lax.broadcasted_iota(I32, (nrow, nrow), 0)
    ci = lax.broadcasted_iota(I32, (nrow, nrow), 1)
    shift = t.bit_length() - 1
    mask = ((ri >> shift) == (ci >> shift)) & (ri >= ci)
    for h in range(heads):
        ks = slice(h * dk, (h + 1) * dk)
        vs = slice(h * dv, (h + 1) * dv)
        a = jnp.where(mask, _dot_nt(qg[:, ks].astype(BF16), kg[:, ks].astype(BF16)), 0.0)
        o = _dot(a.astype(BF16), v_ref[:, vs].astype(BF16))
        decay_t = decay[:, ks].T
        o_state = []
        for sq in range(nsq):
            r = slice(sq * t, (sq + 1) * t)
            s0 = s0_ref[sq, h]
            o_state.append(_dot(qg[r, ks].astype(BF16), s0.astype(BF16)))
            upd = _dot_tn(ke[r, ks].astype(BF16), v_ref[r, vs].astype(BF16))
            st_ref[sq, h] = decay_t[:, sq * t:sq * t + 1] * s0 + upd
        on = _rms(o + jnp.concatenate(o_state, axis=0), gn_ref[h:h + 1, :])
        rg = r_ref[:, vs]
        gated = (rg * _sigmoid(rg) * on).astype(BF16)
        u_ref[:, vs] = _dot(pick, gated).astype(BF16)


def _gla_recurrence_short(z, log_a, s0, g_norm, nseq, t, t_real, nsq, row0):
    qk_w = log_a.shape[1]
    v_w = 2 * qk_w
    heads, dk, dv = s0.shape[1:]
    assert nsq * t == dk == LANES and row0 % (nsq * t) == 0
    rb = nsq * t
    blk0 = row0 // rb
    return pl.pallas_call(
        functools.partial(_gla_short_kernel, t=t, t_real=t_real, nsq=nsq),
        grid=(nseq // nsq,),
        in_specs=[
            pl.BlockSpec((rb, qk_w), lambda b: (blk0 + b, 0)),
            pl.BlockSpec((rb, qk_w), lambda b: (blk0 + b, 1)),
            pl.BlockSpec((rb, v_w), lambda b: (blk0 + b, 1)),
            pl.BlockSpec((rb, v_w), lambda b: (blk0 + b, 2)),
            pl.BlockSpec((rb, qk_w), lambda b: (blk0 + b, 0)),
            pl.BlockSpec((nsq, heads, dk, dv), lambda b: (b, 0, 0, 0)),
            pl.BlockSpec((heads, dv), lambda b: (0, 0)),
        ],
        out_specs=[
            pl.BlockSpec((nsq * t_real, v_w), lambda b: (b, 0)),
            pl.BlockSpec((nsq, heads, dk, dv), lambda b: (b, 0, 0, 0)),
        ],
        out_shape=[
            jax.ShapeDtypeStruct((nseq * t_real, v_w), BF16),
            jax.ShapeDtypeStruct(s0.shape, F32),
        ],
        compiler_params=_params(("arbitrary",)),
        name="gla_recurrence_t%d" % t,
    )(z, z, z, z, log_a, s0, g_norm)


def _mlstm_short_kernel(q_ref, k_ref, v_ref, op_ref, gt_ref, e_ref, c0_ref, n0_ref, m0_ref, gn_ref,
                        u_ref, ct_ref, nt_ref, mt_ref, *, t, t_real, nsq):
    heads = MLSTM_HEADS
    pairs = heads // 2
    dv = v_ref.shape[1] // heads
    dk = q_ref.shape[1] // heads
    nrow = nsq * t

    def per_seq(x3):
        return jnp.broadcast_to(x3, (nsq, t, x3.shape[2])).reshape(nrow, x3.shape[2])

    def last(x):
        return x.reshape(nsq, t, x.shape[1])[:, t - 1:t, :]

    rc = lax.broadcasted_iota(I32, (nrow, LANES), 0) & (t - 1)
    lane_g = lax.broadcasted_iota(I32, (nrow, LANES), 1)
    gts = jnp.where(rc < t_real, gt_ref[...],
                    jnp.where(lane_g < heads, NEG_BIG, 0.0))
    pick = _real_rows_selector(nsq, t, t_real)
    f_cum = pltpu.roll(_chunk_scan(gts, rc, t, jnp.add, 0.0), LANES - heads, 1)
    b = gts - f_cum
    cmb = _chunk_scan(b, rc, t, jnp.maximum, NEG_BIG)
    mp = per_seq(m0_ref[...])
    big_m = jnp.maximum(mp, cmb)
    m_t = f_cum + big_m
    m_new = last(m_t)
    mt_ref[...] = m_new
    mn = per_seq(m_new)
    fe = per_seq(last(f_cum))
    w_i = jnp.exp(mp - big_m)
    em = jnp.exp(-m_t)
    k_sc = jnp.exp(fe - f_cum + gts - mn)
    w_c = jnp.exp(fe + mp - mn)
    ex = _dot(_pack_terms([w_i, k_sc, -big_m, em, w_c], heads), e_ref[...])
    offs = np.cumsum([0] + [heads * w for w in SHORT_EXPAND_WIDTHS])
    q = q_ref[...]
    k = k_ref[...] * (dk ** -0.5)
    qs = q * ex[:, offs[0]:offs[1]]
    ke = k * ex[:, offs[1]:offs[2]]
    neg_m = ex[:, offs[2]:offs[3]]
    em_v = ex[:, offs[3]:offs[4]]
    wc_k = ex[:, offs[4]:offs[5]]
    bt = b.T

    ri = lax.broadcasted_iota(I32, (nrow, nrow), 0)
    ci = lax.broadcasted_iota(I32, (nrow, nrow), 1)
    shift = t.bit_length() - 1
    mask = ((ri >> shift) == (ci >> shift)) & (ri >= ci)
    lane1 = lax.broadcasted_iota(I32, (1, LANES), 1)
    rowi = lax.broadcasted_iota(I32, (LANES, LANES), 0)
    ones_v = jnp.ones((nrow, dv), BF16)
    ones_k = jnp.ones((LANES, dv), BF16)
    for p in range(pairs):
        ps = slice(p * LANES, (p + 1) * LANES)
        qp = q[:, ps]
        qsp = qs[:, ps]
        kpb = k[:, ps].astype(BF16)
        n_rows = per_seq(n0_ref[:, p:p + 1, :])
        intra, den_state, qsh = [], [], []
        for jj in range(2):
            h = 2 * p + jj
            hs = slice(h * dv, (h + 1) * dv)
            mine = (lane1 >= jj * dk) & (lane1 < (jj + 1) * dk)
            qh = jnp.where(mine, qp, 0.0).astype(BF16)
            qsh.append(jnp.where(mine, qsp, 0.0))
            d = neg_m[:, hs] + bt[h:h + 1, :]
            s = _dot_nt(qh, kpb) * jnp.where(mask, jnp.exp(d), 0.0)
            vh = jnp.concatenate([v_ref[:, hs].astype(BF16), ones_v], axis=1)
            intra.append(_dot(s.astype(BF16), vh))
            den_state.append(_dot((qsh[jj] * n_rows).astype(BF16), ones_k))
        num_state = [[], []]
        for sq in range(nsq):
            r = slice(sq * t, (sq + 1) * t)
            lhs = jnp.concatenate([qsh[0][r], qsh[1][r]], axis=0).astype(BF16)
            c_pair = jnp.concatenate([c0_ref[sq, 2 * p], c0_ref[sq, 2 * p + 1]], axis=0)
            res = _dot(lhs, c_pair.astype(BF16))
            num_state[0].append(res[:t])
            num_state[1].append(res[t:])
        for jj in range(2):
            h = 2 * p + jj
            hs = slice(h * dv, (h + 1) * dv)
            num = intra[jj][:, :dv] + jnp.concatenate(num_state[jj], axis=0)
            den = intra[jj][:, dv:] + den_state[jj]
            hh = num / jnp.maximum(jnp.abs(den), em_v[:, hs])
            hn = _rms(hh, gn_ref[h:h + 1, :])
            gated = (_sigmoid(op_ref[:, hs]) * hn).astype(BF16)
            u_ref[:, hs] = _dot(pick, gated).astype(BF16)
        kep = ke[:, ps]
        for sq in range(nsq):
            r = slice(sq * t, (sq + 1) * t)
            vp = v_ref[r, 2 * p * dv:(2 * p + 2) * dv].astype(BF16)
            full = _dot_tn(kep[r].astype(BF16), vp)
            upd = jnp.where(rowi < dk, full[:, :dv], full[:, dv:])
            w_row = w_c[sq * t:sq * t + 1, :]
            w_col = jnp.where(rowi < dk, w_row[:, 2 * p:2 * p + 1], w_row[:, 2 * p + 1:2 * p + 2])
            c_pair = jnp.concatenate([c0_ref[sq, 2 * p], c0_ref[sq, 2 * p + 1]], axis=0)
            c_new = w_col * c_pair + upd
            ct_ref[sq, 2 * p] = c_new[:dk]
            ct_ref[sq, 2 * p + 1] = c_new[dk:]
        k_sum = jnp.sum(kep.reshape(nsq, t, LANES), axis=1, keepdims=True)
        nt_ref[:, p:p + 1, :] = last(wc_k[:, ps]) * n0_ref[:, p:p + 1, :] + k_sum


def _mlstm_recurrence_short(z, gates, c0, n0, m0, g_norm, nseq, t, t_real, nsq, row0):
    heads = g_norm.shape[0]
    v_w = heads * g_norm.shape[1]
    qk_w = v_w // 2
    assert nsq * t == LANES and row0 % (nsq * t) == 0
    expand = _expand_matrix(heads, SHORT_EXPAND_WIDTHS)
    kern = functools.partial(_mlstm_short_kernel, t=t, t_real=t_real, nsq=nsq)
    rb = nsq * t
    blk0 = row0 // rb
    st4 = lambda b: (b, 0, 0, 0)
    st3 = lambda b: (b, 0, 0)
    return pl.pallas_call(
        kern,
        grid=(nseq // nsq,),
        in_specs=[
            pl.BlockSpec((rb, qk_w), lambda b: (blk0 + b, 0)),
            pl.BlockSpec((rb, qk_w), lambda b: (blk0 + b, 1)),
            pl.BlockSpec((rb, v_w), lambda b: (blk0 + b, 1)),
            pl.BlockSpec((rb, v_w), lambda b: (blk0 + b, 2)),
            pl.BlockSpec((rb, LANES), lambda b: (blk0 + b, 0)),
            pl.BlockSpec(expand.shape, lambda b: (0, 0)),
            pl.BlockSpec((nsq,) + c0.shape[1:], st4),
            pl.BlockSpec((nsq,) + n0.shape[1:], st3),
            pl.BlockSpec((nsq,) + m0.shape[1:], st3),
            pl.BlockSpec(g_norm.shape, lambda b: (0, 0)),
        ],
        out_specs=[
            pl.BlockSpec((nsq * t_real, v_w), lambda b: (b, 0)),
            pl.BlockSpec((nsq,) + c0.shape[1:], st4),
            pl.BlockSpec((nsq,) + n0.shape[1:], st3),
            pl.BlockSpec((nsq,) + m0.shape[1:], st3),
        ],
        out_shape=[
            jax.ShapeDtypeStruct((nseq * t_real, v_w), BF16),
            jax.ShapeDtypeStruct(c0.shape, F32),
            jax.ShapeDtypeStruct(n0.shape, F32),
            jax.ShapeDtypeStruct(m0.shape, F32),
        ],
        compiler_params=_params(("arbitrary",)),
        name="mlstm_recurrence_t%d" % t,
    )(z, z, z, z, gates, expand, c0, n0, m0, g_norm)


N_SPLIT = 3
EXPAND_WIDTHS = (64, 64, 128, 128)
SHORT_EXPAND_WIDTHS = EXPAND_WIDTHS + (64,)


def _expand_matrix(heads, widths):
    assert len(widths) * N_SPLIT * heads <= LANES
    cols = []
    for qi, width in enumerate(widths):
        sel = np.zeros((LANES, heads * width), np.float32)
        for term in range(N_SPLIT):
            for h in range(heads):
                sel[(qi * N_SPLIT + term) * heads + h, h * width:(h + 1) * width] = 1.0
        cols.append(sel)
    return jnp.asarray(np.concatenate(cols, axis=1), BF16)


def _chunk_scan(x, rc, n, op, fill):
    s = 1
    while s < n:
        x = op(x, jnp.where(rc >= s, pltpu.roll(x, s, 0), fill))
        s *= 2
    return x


def _pack_terms(values, heads):
    lane = lax.broadcasted_iota(I32, values[0].shape, 1)
    packed = jnp.zeros(values[0].shape, F32)
    slot = 0
    for val in values:
        rest = val
        for term in range(N_SPLIT):
            part = rest.astype(BF16).astype(F32)
            rest = rest - part
            moved = part if slot == 0 else pltpu.roll(part, slot * heads, 1)
            packed = jnp.where((lane >= slot * heads) & (lane < (slot + 1) * heads), moved, packed)
            slot += 1
    return packed.astype(BF16)


def _mlstm_block_kernel(q_ref, k_ref, v_ref, op_ref, gt_ref, e_ref, c0_ref, n0_ref, m0_ref, gn_ref,
                        u_ref, ct_ref, nt_ref, mt_ref, c_sc, nb_sc, m_sc, *, chunk, tb, nblk):
    heads = MLSTM_HEADS
    pairs = heads // 2
    dv = v_ref.shape[1] // heads
    dk = q_ref.shape[1] // heads
    c = pl.program_id(1)

    @pl.when(c == 0)
    def _():
        c_sc[...] = c0_ref[0]
        for p in range(pairs):
            nb_sc[p] = jnp.broadcast_to(n0_ref[0, p:p + 1, :], (LANES, LANES)).T
        m_sc[...] = m0_ref[0]

    nch = tb // chunk
    rc = lax.broadcasted_iota(I32, (tb, LANES), 0) & (chunk - 1)
    gts = gt_ref[...]
    f_cum = pltpu.roll(_chunk_scan(gts, rc, chunk, jnp.add, 0.0), LANES - heads, 1)
    b = gts - f_cum
    cmb = _chunk_scan(b, rc, chunk, jnp.maximum, NEG_BIG)
    m_prev = m_sc[...]
    mp_rows, mn_rows, fe_rows, w_c = [], [], [], []
    for j in range(nch):
        last = (j + 1) * chunk - 1
        f_end = f_cum[last:last + 1, :]
        m_new = f_end + jnp.maximum(m_prev, cmb[last:last + 1, :])
        w_c.append(jnp.exp(f_end + m_prev - m_new))
        mp_rows.append(jnp.broadcast_to(m_prev, (chunk, LANES)))
        mn_rows.append(jnp.broadcast_to(m_new, (chunk, LANES)))
        fe_rows.append(jnp.broadcast_to(f_end, (chunk, LANES)))
        m_prev = m_new
    m_sc[...] = m_prev
    mp = jnp.concatenate(mp_rows, axis=0)
    mn = jnp.concatenate(mn_rows, axis=0)
    fe = jnp.concatenate(fe_rows, axis=0)
    big_m = jnp.maximum(mp, cmb)
    w_i = jnp.exp(mp - big_m)
    em = jnp.exp(-(f_cum + big_m))
    k_sc = jnp.exp(fe - f_cum + gts - mn)
    ex = _dot(_pack_terms([w_i, k_sc, -big_m, em], heads), e_ref[...])
    o1 = heads * EXPAND_WIDTHS[0]
    o2 = o1 + heads * EXPAND_WIDTHS[1]
    o3 = o2 + heads * EXPAND_WIDTHS[2]
    q = q_ref[...]
    k = k_ref[...] * (dk ** -0.5)
    qs = q * ex[:, :o1]
    ke = k * ex[:, o1:o2]
    neg_m = ex[:, o2:o3]
    em_v = ex[:, o3:]
    bt = b.T

    ri = lax.broadcasted_iota(I32, (chunk, chunk), 0)
    ci = lax.broadcasted_iota(I32, (chunk, chunk), 1)
    causal = ri >= ci
    lane1 = lax.broadcasted_iota(I32, (1, LANES), 1)
    rowi = lax.broadcasted_iota(I32, (LANES, LANES), 0)
    ones_v = jnp.ones((chunk, dv), BF16)
    states = [[None] * pairs for _ in range(nch)]
    for p in range(pairs):
        ps = slice(p * LANES, (p + 1) * LANES)
        cp = c_sc[p]
        nb = nb_sc[p]
        for j in range(nch):
            rows = slice(j * chunk, (j + 1) * chunk)
            states[j][p] = jnp.concatenate([cp, nb], axis=1).astype(BF16)
            vp = jnp.concatenate([v_ref[rows, 2 * p * dv:(2 * p + 2) * dv].astype(BF16), ones_v], axis=1)
            full = _dot_tn(ke[rows, ps].astype(BF16), vp)
            upd = jnp.where(rowi < dk, full[:, :dv], full[:, dv:2 * dv])
            w_col = jnp.where(rowi < dk, w_c[j][:, 2 * p:2 * p + 1], w_c[j][:, 2 * p + 1:2 * p + 2])
            cp = w_col * cp + upd
            nb = w_col * nb + full[:, 2 * dv:]
        c_sc[p] = cp
        nb_sc[p] = nb
    for j in range(nch):
        rows = slice(j * chunk, (j + 1) * chunk)
        for p in range(pairs):
            ps = slice(p * LANES, (p + 1) * LANES)
            qp = q[rows, ps]
            qsp = qs[rows, ps]
            kpb = k[rows, ps].astype(BF16)
            state = states[j][p]
            for jj in range(2):
                h = 2 * p + jj
                hs = slice(h * dv, (h + 1) * dv)
                mine = (lane1 >= jj * dk) & (lane1 < (jj + 1) * dk)
                qh = jnp.where(mine, qp, 0.0).astype(BF16)
                qsh = jnp.where(mine, qsp, 0.0).astype(BF16)
                d = neg_m[rows, h * dv:h * dv + chunk] + bt[h:h + 1, j * chunk:(j + 1) * chunk]
                s = _dot_nt(qh, kpb) * jnp.where(causal, jnp.exp(d), 0.0)
                vh = jnp.concatenate([v_ref[rows, hs].astype(BF16), ones_v], axis=1)
                out = _dot(qsh, state) + _dot(s.astype(BF16), vh)
                hh = out[:, :dv] / jnp.maximum(jnp.abs(out[:, dv:]), em_v[rows, hs])
                hn = _rms(hh, gn_ref[h:h + 1, :])
                u_ref[rows, hs] = (_sigmoid(op_ref[rows, hs]) * hn).astype(BF16)

    @pl.when(c == nblk - 1)
    def _():
        ct_ref[0] = c_sc[...]
        for p in range(pairs):
            nt_ref[0, p:p + 1, :] = nb_sc[p].T[0:1, :]
        mt_ref[0] = m_sc[...]


def _mlstm_recurrence_blocked(z, gates, c0, n0, m0, g_norm, nseq, t, chunk, tb):
    heads = g_norm.shape[0]
    v_w = heads * g_norm.shape[1]
    qk_w = v_w // 2
    nblk = t // tb
    expand = _expand_matrix(heads, EXPAND_WIDTHS)
    kern = functools.partial(_mlstm_block_kernel, chunk=chunk, tb=tb, nblk=nblk)
    row = lambda b, c: b * nblk + c
    st4 = lambda b, c: (b, 0, 0, 0)
    st3 = lambda b, c: (b, 0, 0)
    return pl.pallas_call(
        kern,
        grid=(nseq, nblk),
        in_specs=[
            pl.BlockSpec((tb, qk_w), lambda b, c: (row(b, c), 0)),
            pl.BlockSpec((tb, qk_w), lambda b, c: (row(b, c), 1)),
            pl.BlockSpec((tb, v_w), lambda b, c: (row(b, c), 1)),
            pl.BlockSpec((tb, v_w), lambda b, c: (row(b, c), 2)),
            pl.BlockSpec((tb, LANES), lambda b, c: (row(b, c), 0)),
            pl.BlockSpec(expand.shape, lambda b, c: (0, 0)),
            pl.BlockSpec((1,) + c0.shape[1:], st4),
            pl.BlockSpec((1,) + n0.shape[1:], st3),
            pl.BlockSpec((1,) + m0.shape[1:], st3),
            pl.BlockSpec(g_norm.shape, lambda b, c: (0, 0)),
        ],
        out_specs=[
            pl.BlockSpec((tb, v_w), lambda b, c: (row(b, c), 0)),
            pl.BlockSpec((1,) + c0.shape[1:], st4),
            pl.BlockSpec((1,) + n0.shape[1:], st3),
            pl.BlockSpec((1,) + m0.shape[1:], st3),
        ],
        out_shape=[
            jax.ShapeDtypeStruct((nseq * t, v_w), BF16),
            jax.ShapeDtypeStruct(c0.shape, F32),
            jax.ShapeDtypeStruct(n0.shape, F32),
            jax.ShapeDtypeStruct(m0.shape, F32),
        ],
        scratch_shapes=[
            pltpu.VMEM(c0.shape[1:], F32),
            pltpu.VMEM(c0.shape[1:], F32),
            pltpu.VMEM(m0.shape[1:], F32),
        ],
        compiler_params=_params(("arbitrary", "arbitrary")),
        name="mlstm_recurrence_t%d" % t,
    )(z, z, z, z, gates, expand, c0, n0, m0, g_norm)


def _out_proj_kernel(up_ref, us_ref, xp_ref, xs_ref, w_ref, g_ref, cast_ref,
                     xo_ref, hn_ref, cast_out_ref, wb_sc, *, n_first):
    i = pl.program_id(0)

    @pl.when(i == 0)
    def _():
        wb_sc[...] = w_ref[...].astype(BF16)

    cast_out_ref[...] = cast_ref[...].astype(BF16)
    u = _pick_rows(i, n_first, up_ref, us_ref)
    x = _pick_rows(i, n_first, xp_ref, xs_ref) + _dot(u, wb_sc[...])
    xo_ref[...] = x
    hn_ref[...] = _rms(x, g_ref[...]).astype(hn_ref.dtype)


def _out_proj(u_p, u_s, x_p, x_s, w, g, cast_src):
    d = x_p.shape[1]
    n = x_p.shape[0] + x_s.shape[0]
    n_first = x_p.shape[0] // TM
    cast_in, cast_out, cast_shape = _cast_side_job(cast_src, n // TM)
    return pl.pallas_call(
        functools.partial(_out_proj_kernel, n_first=n_first),
        grid=(n // TM,),
        in_specs=_split_specs(n_first, u_p.shape[1]) + _split_specs(n_first, d) + [
            _resident_spec(w.shape, lambda i: (0, 0)),
            pl.BlockSpec((1, d), lambda i: (0, 0)),
            cast_in,
        ],
        out_specs=[pl.BlockSpec((TM, d), lambda i: (i, 0)), pl.BlockSpec((TM, d), lambda i: (i, 0)),
                   cast_out],
        out_shape=[jax.ShapeDtypeStruct((n, d), F32), jax.ShapeDtypeStruct((n, d), BF16), cast_shape],
        scratch_shapes=[pltpu.VMEM(w.shape, BF16)],
        compiler_params=_params(("arbitrary",)),
        name="out_proj",
    )(u_p, u_s, x_p, x_s, w, g, cast_src)


def _out_proj_router_kernel(up_ref, us_ref, x_ref, w_ref, g_ref, wr_ref, br_ref,
                            xo_ref, hn_ref, gate_ref, idx_ref, cnt_ref, logit_sc, wb_sc,
                            *, n_first, n_tiles):
    i = pl.program_id(0)

    @pl.when(i == 0)
    def _():
        logit_sc[...] = jnp.zeros_like(logit_sc)
        wb_sc[...] = w_ref[...].astype(BF16)

    logits = logit_sc[(i + 1) & 1]
    u = _pick_rows(jnp.minimum(i, n_tiles - 1), n_first, up_ref, us_ref)
    x = x_ref[...] + _dot(u, wb_sc[...])
    xo_ref[...] = x
    hn = _rms(x, g_ref[...])
    hn_hi = hn.astype(BF16)
    hn_ref[...] = hn_hi
    hn_lo = (hn - hn_hi.astype(F32)).astype(BF16)
    logit_sc[i & 1] = (_dot(hn_hi, wr_ref[0]) + _dot(hn_lo, wr_ref[0]) + _dot(hn_hi, wr_ref[1])
                       + br_ref[...])
    tm = logits.shape[0]
    lane = lax.broadcasted_iota(I32, logits.shape, 1)
    valid = lane < N_EXPERTS
    logits = jnp.where(valid, logits, -jnp.inf)
    ex = jnp.exp(logits - jnp.max(logits, axis=-1, keepdims=True))
    probs = jnp.where(valid, ex / jnp.sum(ex, axis=-1, keepdims=True), -1.0)
    p1 = jnp.max(probs, axis=-1, keepdims=True)
    i1 = jnp.min(jnp.where(probs == p1, lane, LANES), axis=-1, keepdims=True)
    rest = jnp.where(lane == i1, -1.0, probs)
    p2 = jnp.max(rest, axis=-1, keepdims=True)
    i2 = jnp.min(jnp.where(rest == p2, lane, LANES), axis=-1, keepdims=True)
    tot = p1 + p2
    onehot = ((lane == i1) | (lane == i2)).astype(BF16)
    rr = lax.broadcasted_iota(I32, (tm, tm), 0)
    cc = lax.broadcasted_iota(I32, (tm, tm), 1)
    strict = (rr > cc).astype(BF16)
    before = _dot(strict, onehot)
    r1 = jnp.sum(jnp.where(lane == i1, before, 0.0), axis=-1, keepdims=True)
    r2 = jnp.sum(jnp.where(lane == i2, before, 0.0), axis=-1, keepdims=True)
    gate_ref[...] = jnp.where(lane == 0, p1 / tot, jnp.where(lane == 1, p2 / tot, 0.0))
    idx_ref[...] = jnp.where(lane == 0, i1,
                             jnp.where(lane == 1, i2,
                                       jnp.where(lane == 2, r1.astype(I32),
                                                 jnp.where(lane == 3, r2.astype(I32), 0))))
    cnt_ref[0] = jnp.sum(onehot.astype(F32), axis=0, keepdims=True)


def _out_proj_router(u_p, u_s, x, w, g, w_router, b_router):
    n, d = x.shape
    n_first = u_p.shape[0] // TM
    n_tiles = n // TM
    row = lambda i: (jnp.minimum(i, n_tiles - 1), 0)
    routed = lambda i: (jnp.maximum(i - 1, 0), 0)
    fix = lambda i: (0, 0)
    u_specs = [pl.BlockSpec((TM, u_p.shape[1]), lambda i: (jnp.minimum(i, n_first - 1), 0)),
               pl.BlockSpec((TM, u_p.shape[1]),
                            lambda i: (jnp.clip(i - n_first, 0, n_tiles - n_first - 1), 0))]
    return pl.pallas_call(
        functools.partial(_out_proj_router_kernel, n_first=n_first, n_tiles=n_tiles),
        grid=(n_tiles + 1,),
        in_specs=u_specs + [
            pl.BlockSpec((TM, d), row),
            _resident_spec(w.shape, fix),
            pl.BlockSpec((1, d), fix),
            pl.BlockSpec(w_router.shape, lambda i: (0, 0, 0)),
            pl.BlockSpec((1, LANES), fix),
        ],
        out_specs=[
            pl.BlockSpec((TM, d), row),
            pl.BlockSpec((TM, d), row),
            pl.BlockSpec((TM, LANES), routed),
            pl.BlockSpec((TM, LANES), routed),
            pl.BlockSpec((1, 1, LANES), lambda i: (jnp.maximum(i - 1, 0), 0, 0)),
        ],
        out_shape=[
            jax.ShapeDtypeStruct((n, d), F32),
            jax.ShapeDtypeStruct((n, d), BF16),
            jax.ShapeDtypeStruct((n, LANES), F32),
            jax.ShapeDtypeStruct((n, LANES), I32),
            jax.ShapeDtypeStruct((n_tiles, 1, LANES), F32),
        ],
        scratch_shapes=[pltpu.VMEM((2, TM, LANES), F32), pltpu.VMEM(w.shape, BF16)],
        compiler_params=_params(("arbitrary",)),
        name="out_proj_router",
    )(u_p, u_s, x, w, g, w_router, b_router)


def _swiglu_kernel(hn_ref, x_ref, wg_ref, wu_ref, wd_ref, g_ref, cast_ref, xo_ref, ho_ref, cast_out_ref,
                   *, nchunk):
    cast_out_ref[...] = cast_ref[...].astype(BF16)
    hn = hn_ref[...]
    width = wd_ref.shape[0] // nchunk
    x = x_ref[...]
    for c in range(nchunk):
        cs = slice(c * width, (c + 1) * width)
        gate = _dot(hn, wg_ref[:, cs])
        up = _dot(hn, wu_ref[:, cs])
        act = (gate * _sigmoid(gate) * up).astype(BF16)
        x = x + _dot(act, wd_ref[cs, :])
    xo_ref[...] = x
    ho_ref[...] = _rms(x, g_ref[...]).astype(BF16)


def _swiglu(hn, x, w_gu, w_down, g, nchunk, cast_src):
    n, d = x.shape
    dff = w_down.shape[0]
    kern = functools.partial(_swiglu_kernel, nchunk=nchunk)
    row = lambda i: (i, 0)
    cast_in, cast_out, cast_shape = _cast_side_job(cast_src, n // TM)
    return pl.pallas_call(
        kern,
        grid=(n // TM,),
        in_specs=[
            pl.BlockSpec((TM, d), row),
            pl.BlockSpec((TM, d), row),
            _resident_spec((d, dff), lambda i: (0, 0)),
            _resident_spec((d, dff), lambda i: (0, 1)),
            _resident_spec((dff, d), lambda i: (0, 0)),
            pl.BlockSpec((1, d), lambda i: (0, 0)),
            cast_in,
        ],
        out_specs=[pl.BlockSpec((TM, d), row), pl.BlockSpec((TM, d), row), cast_out],
        out_shape=[jax.ShapeDtypeStruct((n, d), F32), jax.ShapeDtypeStruct((n, d), BF16), cast_shape],
        compiler_params=_params(("arbitrary",)),
        name="dense_swiglu",
    )(hn, x, w_gu, w_gu, w_down, g, cast_src)


def _row_copy(src_ref, src_row, dst_ref, dst_row, sem):
    return pltpu.make_async_copy(src_ref.at[pl.ds(src_row, 1), :], dst_ref.at[pl.ds(dst_row, 1), :], sem)


SUBLANES = 8


def _zero_rows(zero_ref, dst_ref, dst_row, count, limit, sem, wait):
    head = jnp.minimum((-dst_row) & (SUBLANES - 1), count)
    for j in range(SUBLANES - 1):
        @pl.when(j < head)
        def _(j=j):
            copy = _row_copy(zero_ref, 0, dst_ref, dst_row + j, sem)
            copy.wait() if wait else copy.start()

    rest = count - head
    first = dst_row + head
    for bit in range(SUBLANES.bit_length() - 1, limit.bit_length() - 1):
        size = 1 << bit
        done = (rest >> (bit + 1)) << (bit + 1)

        @pl.when(((rest >> bit) & 1) == 1)
        def _(size=size, done=done):
            start = pl.multiple_of(first + done, SUBLANES)
            copy = pltpu.make_async_copy(zero_ref.at[pl.ds(0, size), :],
                                         dst_ref.at[pl.ds(start, size), :], sem)
            copy.wait() if wait else copy.start()


def _expert_kernel(te_ref, tv_ref, xs_ref, wg_ref, wu_ref, wd_ref, ys_ref, *, nchunk):
    i = pl.program_id(0)
    half = xs_ref.shape[0] // 2
    width = wg_ref.shape[2] // nchunk

    def swiglu_rows(rows):
        x = xs_ref[rows, :].astype(BF16)
        acc = None
        for c in range(nchunk):
            cs = slice(c * width, (c + 1) * width)
            gate = _dot(x, wg_ref[0, :, cs])
            up = _dot(x, wu_ref[0, :, cs])
            act = (gate * _sigmoid(gate) * up).astype(BF16)
            part = _dot(act, wd_ref[0, cs, :])
            acc = part if acc is None else acc + part
        ys_ref[rows, :] = acc.astype(BF16).astype(F32)

    @pl.when(tv_ref[i] == 2)
    def _():
        swiglu_rows(slice(None))

    @pl.when(tv_ref[i] == 1)
    def _():
        swiglu_rows(slice(0, half))
        ys_ref[half:, :] = jnp.zeros((half, ys_ref.shape[1]), F32)

    @pl.when(tv_ref[i] == 0)
    def _():
        ys_ref[...] = jnp.zeros_like(ys_ref)


def _experts(tile_expert, tile_valid, xs, w_gu, w_down):
    r, d = xs.shape
    n_tiles = r // TM_EXPERT
    dff = w_down.shape[1]
    kern = functools.partial(_expert_kernel, nchunk=EXPERT_CHUNKS)
    return pl.pallas_call(
        kern,
        grid_spec=pltpu.PrefetchScalarGridSpec(
            num_scalar_prefetch=2,
            grid=(r // TM_EXPERT,),
            in_specs=[
                pl.BlockSpec((TM_EXPERT, d), lambda i, te, tv: (jnp.minimum(i, tv[n_tiles]), 0)),
                pl.BlockSpec((1, d, dff), lambda i, te, tv: (te[i], 0, 0)),
                pl.BlockSpec((1, d, dff), lambda i, te, tv: (te[i], 0, 1)),
                pl.BlockSpec((1, dff, d), lambda i, te, tv: (te[i], 0, 0)),
            ],
            out_specs=pl.BlockSpec((TM_EXPERT, d), lambda i, te, tv: (i, 0)),
        ),
        out_shape=jax.ShapeDtypeStruct((r, d), F32),
        compiler_params=_params(("arbitrary",)),
        name="moe_experts",
    )(tile_expert, tile_valid, xs, w_gu, w_gu, w_down)


RUN_BUF_ROWS = 2 * TM + LANES


def _run_positions(idx, off_row):
    lane = lax.broadcasted_iota(I32, idx.shape, 1)
    pos = []
    for slot in range(2):
        off = jnp.sum(jnp.where(lane == idx[:, slot:slot + 1], off_row, 0.0), axis=-1, keepdims=True)
        pos.append(off.astype(I32) + idx[:, 2 + slot:3 + slot])
    return pos


def _run_copies(tab_ref, step, hbm_ref, buf_ref, slot, sem, to_hbm, wait):
    for e in range(N_EXPERTS):
        entry = (step * N_EXPERTS + e) * 3
        off, row, length = tab_ref[entry], tab_ref[entry + 1], tab_ref[entry + 2]
        for bit in range(SUBLANES.bit_length() - 1, TM.bit_length()):
            size = 1 << bit
            done = (length >> (bit + 1)) << (bit + 1)

            @pl.when(((length >> bit) & 1) == 1)
            def _(size=size, done=done, off=off, row=row):
                in_buf = buf_ref.at[slot, pl.ds(pl.multiple_of(off + done, SUBLANES), size), :]
                in_hbm = hbm_ref.at[pl.ds(pl.multiple_of(row + done, SUBLANES), size), :]
                copy = (pltpu.make_async_copy(in_buf, in_hbm, sem.at[slot]) if to_hbm
                        else pltpu.make_async_copy(in_hbm, in_buf, sem.at[slot]))
                copy.wait() if wait else copy.start()


def _run_dispatch_kernel(tab_ref, pad_ref, hn_ref, idx_ref, off_ref, xs_ref, buf, zero_sc, sem, zsem,
                         *, n_steps):
    j = pl.program_id(0)
    slot = j & 1

    @pl.when(j == 0)
    def _():
        zero_sc[...] = jnp.zeros_like(zero_sc)
        rows = zero_sc.shape[0]
        for wait in (False, True):
            for e in range(N_EXPERTS):
                _zero_rows(zero_sc, xs_ref, pad_ref[2 * e], pad_ref[2 * e + 1], 2 * rows, zsem, wait)

            def tail(t, carry, wait=wait):
                first = pl.multiple_of(pad_ref[2 * N_EXPERTS] + t * rows, SUBLANES)
                copy = pltpu.make_async_copy(zero_sc, xs_ref.at[pl.ds(first, rows), :], zsem)
                copy.wait() if wait else copy.start()
                return carry

            lax.fori_loop(0, pad_ref[2 * N_EXPERTS + 1] // rows, tail, 0)

    @pl.when(j >= 2)
    def _():
        _run_copies(tab_ref, j - 2, xs_ref, buf, slot, sem, True, True)

    pos0, pos1 = _run_positions(idx_ref[...], off_ref[0])
    lane = lax.broadcasted_iota(I32, (hn_ref.shape[0], buf.shape[1]), 1)
    sel = jnp.where((lane == pos0) | (lane == pos1), 1.0, 0.0).astype(BF16)
    buf[slot] = _dot_tn(sel, hn_ref[...])
    _run_copies(tab_ref, j, xs_ref, buf, slot, sem, True, False)

    @pl.when(j == n_steps - 1)
    def _():
        _run_copies(tab_ref, j - 1, xs_ref, buf, 1 - slot, sem, True, True)
        _run_copies(tab_ref, j, xs_ref, buf, slot, sem, True, True)


def _run_dispatch(run_table, pad_table, hn, idx, run_offsets, n_rows):
    n, d = hn.shape
    n_steps = n // TM
    assert n_steps >= 2
    return pl.pallas_call(
        functools.partial(_run_dispatch_kernel, n_steps=n_steps),
        grid_spec=pltpu.PrefetchScalarGridSpec(
            num_scalar_prefetch=2,
            grid=(n_steps,),
            in_specs=[
                pl.BlockSpec((TM, d), lambda i, *_: (i, 0)),
                pl.BlockSpec((TM, LANES), lambda i, *_: (i, 0)),
                pl.BlockSpec((1, 1, LANES), lambda i, *_: (i, 0, 0)),
            ],
            out_specs=pl.BlockSpec(memory_space=pl.ANY),
            scratch_shapes=[pltpu.VMEM((2, RUN_BUF_ROWS, d), F32),
                            pltpu.VMEM((TM_EXPERT // 2, d), F32),
                            pltpu.SemaphoreType.DMA((2,)), pltpu.SemaphoreType.DMA(())],
        ),
        out_shape=jax.ShapeDtypeStruct((n_rows, d), F32),
        compiler_params=_params(("arbitrary",)),
        name="moe_dispatch",
    )(run_table, pad_table, hn, idx, run_offsets)


def _run_combine_kernel(tab_ref, ys_ref, x_ref, gate_ref, idx_ref, off_ref, g_ref, yp_ref, ysm_ref,
                        buf, sem, *, n_first, n_steps):
    j = pl.program_id(0)
    slot = j & 1

    @pl.when(j == 0)
    def _():
        buf[...] = jnp.zeros_like(buf)
        _run_copies(tab_ref, 0, ys_ref, buf, 0, sem, False, False)

    @pl.when(j + 1 < n_steps)
    def _():
        _run_copies(tab_ref, j + 1, ys_ref, buf, 1 - slot, sem, False, False)

    pos = _run_positions(idx_ref[...], off_ref[0])
    lane = lax.broadcasted_iota(I32, (x_ref.shape[0], buf.shape[1]), 1)
    sel = [jnp.where(lane == pos[k], 1.0, 0.0).astype(BF16) for k in range(2)]
    _run_copies(tab_ref, j, ys_ref, buf, slot, sem, False, True)
    runs = buf[slot].astype(BF16)
    gate = gate_ref[...]
    y = x_ref[...]
    for k in range(2):
        y = y + gate[:, k:k + 1] * _dot(sel[k], runs)
    y = _rms(y, g_ref[...])

    @pl.when(j < n_first)
    def _():
        yp_ref[...] = y

    @pl.when(j >= n_first)
    def _():
        ysm_ref[...] = y


def _run_combine(run_table, ys, x, gates, idx, run_offsets, g, n_p):
    n, d = x.shape
    n_first = n_p // TM
    n_steps = n // TM
    p_spec, s_spec = _split_specs(n_first, d)
    row = lambda i, *_: (i, 0)
    return pl.pallas_call(
        functools.partial(_run_combine_kernel, n_first=n_first, n_steps=n_steps),
        grid_spec=pltpu.PrefetchScalarGridSpec(
            num_scalar_prefetch=1,
            grid=(n_steps,),
            in_specs=[
                pl.BlockSpec(memory_space=pl.ANY),
                pl.BlockSpec((TM, d), row),
                pl.BlockSpec((TM, LANES), row),
                pl.BlockSpec((TM, LANES), row),
                pl.BlockSpec((1, 1, LANES), lambda i, *_: (i, 0, 0)),
                pl.BlockSpec((1, d), lambda i, *_: (0, 0)),
            ],
            out_specs=[p_spec, s_spec],
            scratch_shapes=[pltpu.VMEM((2, RUN_BUF_ROWS, d), F32), pltpu.SemaphoreType.DMA((2,))],
        ),
        out_shape=[jax.ShapeDtypeStruct((n_p, d), F32), jax.ShapeDtypeStruct((n - n_p, d), F32)],
        compiler_params=_params(("arbitrary",)),
        name="moe_combine",
    )(run_table, ys, x, gates, idx, run_offsets, g)


def _run_tables(tile_counts, n_rows):
    cnt = tile_counts[:, 0, :N_EXPERTS].astype(I32)
    run = ((cnt + SUBLANES - 1) // SUBLANES) * SUBLANES
    off = jnp.cumsum(run, axis=1) - run
    used = jnp.sum(run, axis=0)
    padded = ((used + TM_EXPERT - 1) // TM_EXPERT) * TM_EXPERT
    ends = jnp.cumsum(padded)
    starts = ends - padded
    row = starts[None, :] + jnp.cumsum(run, axis=0) - run
    run_table = jnp.stack([off, row, run], axis=2).reshape(-1)
    run_offsets = jnp.pad(off.astype(F32), ((0, 0), (0, LANES - N_EXPERTS)))[:, None, :]
    n_tiles = n_rows // TM_EXPERT
    used_tiles = ends[-1] // TM_EXPERT
    pad_table = jnp.stack([starts + used, padded - used], axis=1).reshape(-1)
    pad_table = jnp.concatenate([pad_table, jnp.stack([ends[-1], n_rows - ends[-1]])])
    tile_start = jnp.arange(n_tiles, dtype=I32) * TM_EXPERT
    tile_expert = jnp.sum((tile_start[:, None] >= ends[None, :]).astype(I32), axis=1)
    last_expert = jnp.sum((ends[-1] - 1 >= ends).astype(I32))
    tile_expert = jnp.minimum(tile_expert, last_expert)
    used_end = (starts + used)[tile_expert]
    half = TM_EXPERT // 2
    tile_rows = jnp.clip(used_end - tile_start, 0, TM_EXPERT)
    tile_valid = jnp.where(tile_start < ends[-1], (tile_rows + half - 1) // half, 0)
    tile_valid = jnp.concatenate([tile_valid, (used_tiles - 1)[None]])
    return run_table, run_offsets, pad_table, tile_expert, tile_valid


def _pad_time(a, nseq, t):
    w = a.shape[1]
    a = jnp.pad(a.reshape(nseq, t, w), ((0, 0), (0, SAMPLE_PAD_T - t), (0, 0)))
    return a.reshape(nseq * SAMPLE_PAD_T, w)


def kernel(x_prompt, x_sample, state_gla_S, state_mlstm_C, state_mlstm_n, state_mlstm_m,
           norm_mix, norm_ffn, norm_final,
           gla_w_in, gla_w_a2, gla_b_a, gla_g_norm, gla_w_out,
           mlstm_w_in, mlstm_b_gate, mlstm_g_norm, mlstm_w_out,
           ffn_w_gu, ffn_w_down,
           moe_w_router, moe_b_router, moe_w_gu, moe_w_down):
    bp, tp, d = x_prompt.shape
    bs, ts, _ = x_sample.shape
    n_p = bp * tp
    n_s = bs * ts
    n = n_p + n_s
    assert n_p % TM == 0 and n_s % TM == 0 and tp % (2 * CHUNK) == 0 and ts <= SAMPLE_PAD_T
    assert norm_mix.shape[0] == 2, "one GLA layer followed by one mLSTM layer"
    qk_w = d // 2
    main_w = 2 * qk_w + 2 * d

    x_p = x_prompt.reshape(n_p, d)
    x_s = x_sample.reshape(n_s, d)

    w_a2 = jnp.pad(gla_w_a2[0], ((0, LANES - GLA_RANK), (0, 0))).astype(BF16)
    z, log_a, ffn_wgu = _gla_in_proj(x_p, _pad_time(x_s, bs, ts), norm_mix[0][None], gla_w_in[0].T,
                                     main_w, w_a2, gla_b_a[0][None], ffn_w_gu[0])
    s0_p = jnp.zeros((bp,) + state_gla_S.shape[2:], F32)
    u_p, s_p = _gla_recurrence(z, log_a, s0_p, gla_g_norm[0], bp, tp, CHUNK, 16 * CHUNK)
    u_s, s_s = _gla_recurrence_short(z, log_a, state_gla_S[0], gla_g_norm[0], bs, SAMPLE_PAD_T, ts,
                                     SAMPLE_SEQS_PER_STEP, n_p)
    x, hn, ffn_wd = _out_proj(u_p, u_s, x_p, x_s, gla_w_out[0],
                              norm_ffn[0][None], ffn_w_down[0])
    x, hn, moe_wgu = _swiglu(hn, x, ffn_wgu, ffn_wd,
                             norm_mix[1][None], SWIGLU_CHUNKS,
                             moe_w_gu[0].reshape(N_EXPERTS * d, moe_w_gu.shape[3]))

    b_gt = jnp.pad(mlstm_b_gate[0], (0, LANES - 2 * MLSTM_HEADS))[None]
    dff_e = moe_w_down.shape[2]
    z, gates, moe_wd = _mlstm_in_proj(hn, n_p, _pad_time(hn[n_p:], bs, ts), mlstm_w_in[0].T, main_w,
                                      b_gt, moe_w_down[0].reshape(N_EXPERTS * dff_e, d))
    pairs = MLSTM_HEADS // 2
    dk2 = 2 * state_mlstm_C.shape[3]
    dvm = state_mlstm_C.shape[4]
    c0_p = jnp.zeros((bp, pairs, dk2, dvm), F32)
    n0_p = jnp.zeros((bp, pairs, dk2), F32)
    m0_p = jnp.zeros((bp, 1, LANES), F32)
    u_p, c_p, nn_p, m_p = _mlstm_recurrence_blocked(z, gates, c0_p, n0_p, m0_p,
                                                    mlstm_g_norm[0], bp, tp, CHUNK, 16 * CHUNK)
    c0_s = state_mlstm_C[0]
    n0_s = state_mlstm_n[0].reshape(bs, pairs, dk2)
    m0_s = jnp.pad(state_mlstm_m[0], ((0, 0), (0, LANES - MLSTM_HEADS)))[:, None, :]
    u_s, c_s, nn_s, m_s = _mlstm_recurrence_short(z, gates, c0_s, n0_s, m0_s, mlstm_g_norm[0], bs,
                                                  SAMPLE_PAD_T, ts, SAMPLE_SEQS_PER_STEP, n_p)

    w_r = jnp.pad(moe_w_router[0], ((0, 0), (0, LANES - N_EXPERTS)))
    w_r_hi = w_r.astype(BF16)
    w_r = jnp.stack([w_r_hi, (w_r - w_r_hi.astype(F32)).astype(BF16)])
    b_r = jnp.pad(moe_b_router[0], (0, LANES - N_EXPERTS))[None]
    x, hn, route_g, route_i, tile_counts = _out_proj_router(
        u_p, u_s, x, mlstm_w_out[0], norm_ffn[1][None], w_r, b_r)
    n_rows = 2 * n + (n // TM) * N_EXPERTS * (SUBLANES - 1) + N_EXPERTS * (TM_EXPERT - 1)
    n_rows = -(-n_rows // TM_EXPERT) * TM_EXPERT
    run_table, run_offsets, pad_table, tile_expert, tile_valid = _run_tables(tile_counts, n_rows)
    xs = _run_dispatch(run_table, pad_table, hn, route_i, run_offsets, n_rows)
    ys = _experts(tile_expert, tile_valid, xs, moe_wgu.reshape(moe_w_gu.shape[1:]),
                  moe_wd.reshape(moe_w_down.shape[1:]))
    y_p, y_s = _run_combine(run_table, ys, x, route_g, route_i, run_offsets, norm_final[None], n_p)

    y_prompt = y_p.reshape(bp, tp, d)
    y_sample = y_s.reshape(bs, ts, d)
    hd = state_mlstm_C.shape[2:]
    return (y_prompt, y_sample,
            s_p[None], c_p.reshape((1, bp) + hd), nn_p.reshape(1, bp, hd[0], hd[1]),
            m_p[None, :, 0, :MLSTM_HEADS],
            s_s[None], c_s[None], nn_s.reshape(1, bs, hd[0], hd[1]), m_s[None, :, 0, :MLSTM_HEADS])


def _mlstm_in_proj(hn, n_p, hn_s, w_in, wz, b_gate, cast_src):
    d = hn.shape[1]
    n = n_p + hn_s.shape[0]
    n_first = n_p // TM
    cast_in, cast_out, cast_shape = _cast_side_job(cast_src, n // TM)

    def kern(hp_ref, hs_ref, w_ref, wg_ref, b_ref, cast_ref, z_ref, gate_ref, cast_out_ref,
             wb_sc, wg_sc):
        @pl.when(pl.program_id(0) == 0)
        def _():
            wb_sc[...] = w_ref[...].T.astype(BF16)
            wg_sc[...] = _gate_weights(wg_ref)

        cast_out_ref[...] = cast_ref[...].astype(BF16)
        h = _pick_rows(pl.program_id(0), n_first, hp_ref, hs_ref)
        width = wz // IN_PROJ_COL_CHUNKS
        for c in range(IN_PROJ_COL_CHUNKS):
            cs = slice(c * width, (c + 1) * width)
            z_ref[:, cs] = _dot(h, wb_sc[:, cs])
        gp = _dot(h, wg_sc[...]) + b_ref[...]
        gc = GATE_CAP * jnp.tanh(gp * (1.0 / GATE_CAP))
        lane = lax.broadcasted_iota(I32, gc.shape, 1)
        out = jnp.where(lane < MLSTM_HEADS, gc, _log_sigmoid(gc))
        gate_ref[...] = jnp.where(lane < 2 * MLSTM_HEADS, out, 0.0)

    return pl.pallas_call(
        kern,
        grid=(n // TM,),
        in_specs=_split_specs(n_first, d) + [
            _resident_spec((wz, d), lambda i: (0, 0)),
            _gate_rows_spec(w_in, wz),
            pl.BlockSpec((1, LANES), lambda i: (0, 0)),
            cast_in,
        ],
        out_specs=[pl.BlockSpec((TM, wz), lambda i: (i, 0)), pl.BlockSpec((TM, LANES), lambda i: (i, 0)),
                   cast_out],
        out_shape=[jax.ShapeDtypeStruct((n, wz), F32), jax.ShapeDtypeStruct((n, LANES), F32),
                   cast_shape],
        scratch_shapes=[pltpu.VMEM((d, wz), BF16), pltpu.VMEM((d, LANES), BF16)],
        compiler_params=_params(("arbitrary",)),
        name="in_proj_mlstm",
    )(hn, hn_s, w_in, w_in, b_gate, cast_src)
```

```python
import functools

import jax
import jax.numpy as jnp
import numpy as np
from jax import lax
from jax.experimental import pallas as pl
from jax.experimental.pallas import tpu as pltpu

F32 = jnp.float32
BF16 = jnp.bfloat16
I32 = jnp.int32

EPS = 1e-6
GLA_HEADS = 4
GLA_RANK = 16
GLA_TAU = 16.0
MLSTM_HEADS = 8
GATE_CAP = 15.0
CHUNK = 64
N_EXPERTS = 8
NEG_BIG = -1e30

LANES = 128
TM = 512
TM_EXPERT = 512
SAMPLE_PAD_T = 8
SAMPLE_SEQS_PER_STEP = 16
SWIGLU_CHUNKS = 11
EXPERT_CHUNKS = 7
VMEM_LIMIT = 56 * 1024 * 1024


def _dot(a, b):
    return jnp.dot(a, b, preferred_element_type=F32)


def _dot_nt(a, b):
    return lax.dot_general(a, b, (((1,), (1,)), ((), ())), preferred_element_type=F32)


def _dot_tn(a, b):
    return lax.dot_general(a, b, (((0,), (0,)), ((), ())), preferred_element_type=F32)


def _sigmoid(x):
    return 1.0 / (1.0 + jnp.exp(-x))


def _log_sigmoid(x):
    return jnp.minimum(x, 0.0) - jnp.log(1.0 + jnp.exp(-jnp.abs(x)))


def _rms(x, g):
    return x * lax.rsqrt(jnp.mean(x * x, axis=-1, keepdims=True) + EPS) * g


def _cumsum_rows(x, n):
    ridx = lax.broadcasted_iota(I32, x.shape, 0)
    s = 1
    while s < n:
        x = x + jnp.where(ridx >= s, pltpu.roll(x, s, 0), 0.0)
        s *= 2
    return x


def _params(sem):
    return pltpu.CompilerParams(dimension_semantics=sem, vmem_limit_bytes=VMEM_LIMIT)


def _resident_spec(block_shape, index_map):
    return pl.BlockSpec(block_shape, index_map, pipeline_mode=pl.Buffered(1))


def _cast_side_job(src, n_steps):
    rows, cols = src.shape
    n_slabs = max(s for s in range(1, n_steps + 1) if rows % (16 * s) == 0)
    spec = pl.BlockSpec((rows // n_slabs, cols), lambda i, *_: (jnp.minimum(i, n_slabs - 1), 0))
    return spec, spec, jax.ShapeDtypeStruct(src.shape, BF16)


IN_PROJ_COL_CHUNKS = 4
ROW_PARTS = 2


def _pick_rows(i, n_first, first_ref, second_ref):
    return jnp.where(i < n_first, first_ref[...], second_ref[...])


def _split_specs(n_first, width):
    return [pl.BlockSpec((TM, width), lambda i, *_: (jnp.minimum(i, n_first - 1), 0)),
            pl.BlockSpec((TM, width), lambda i, *_: (jnp.maximum(i - n_first, 0), 0))]


def _gate_weights(wg_ref):
    rows, d = wg_ref.shape
    full = jnp.concatenate([wg_ref[...], jnp.zeros((LANES - rows, d), F32)], axis=0)
    return full.T.astype(BF16)


def _gla_in_proj_kernel(xp_ref, xs_ref, g_ref, w_ref, wg_ref, w2_ref, b_ref, cast_ref,
                        z_ref, gate_ref, cast_out_ref, wb_sc, wg_sc, *, n_first):
    @pl.when(pl.program_id(0) == 0)
    def _():
        wb_sc[...] = w_ref[...].T.astype(BF16)
        wg_sc[...] = _gate_weights(wg_ref)

    cast_out_ref[...] = cast_ref[...].astype(BF16)
    first = pl.program_id(0) < n_first
    width = wb_sc.shape[1] // IN_PROJ_COL_CHUNKS
    part = xp_ref.shape[0] // ROW_PARTS
    for r in range(ROW_PARTS):
        rows = slice(r * part, (r + 1) * part)
        x = jnp.where(first, xp_ref[rows, :], xs_ref[rows, :])
        hn = _rms(x, g_ref[...]).astype(BF16)
        for c in range(IN_PROJ_COL_CHUNKS):
            cs = slice(c * width, (c + 1) * width)
            z_ref[rows, cs] = _dot(hn, wb_sc[:, cs])
        a = _dot(hn, wg_sc[...])
        la = _dot(a.astype(BF16), w2_ref[...]) + b_ref[...]
        gate_ref[rows, :] = _log_sigmoid(la) * (1.0 / GLA_TAU)


def _gate_rows_spec(w_in, wz):
    rows = w_in.shape[0] - wz
    assert wz % rows == 0 and rows % SUBLANES == 0
    return pl.BlockSpec((rows, w_in.shape[1]), lambda i: (wz // rows, 0))


def _gla_in_proj(x_p, x_s, g, w_in, wz, w2, b, cast_src):
    d = x_p.shape[1]
    n = x_p.shape[0] + x_s.shape[0]
    n_first = x_p.shape[0] // TM
    gw = b.shape[1]
    cast_in, cast_out, cast_shape = _cast_side_job(cast_src, n // TM)
    return pl.pallas_call(
        functools.partial(_gla_in_proj_kernel, n_first=n_first),
        grid=(n // TM,),
        in_specs=_split_specs(n_first, d) + [
            pl.BlockSpec((1, d), lambda i: (0, 0)),
            _resident_spec((wz, d), lambda i: (0, 0)),
            _gate_rows_spec(w_in, wz),
            pl.BlockSpec(w2.shape, lambda i: (0, 0)),
            pl.BlockSpec((1, gw), lambda i: (0, 0)),
            cast_in,
        ],
        out_specs=[
            pl.BlockSpec((TM, wz), lambda i: (i, 0)),
            pl.BlockSpec((TM, gw), lambda i: (i, 0)),
            cast_out,
        ],
        out_shape=[jax.ShapeDtypeStruct((n, wz), F32), jax.ShapeDtypeStruct((n, gw), F32), cast_shape],
        scratch_shapes=[pltpu.VMEM((d, wz), BF16), pltpu.VMEM((d, LANES), BF16)],
        compiler_params=_params(("arbitrary",)),
        name="in_proj_gla",
    )(x_p, x_s, g, w_in, w_in, w2, b, cast_src)


def _gla_kernel(q_ref, k_ref, v_ref, r_ref, la_ref, s0_ref, gn_ref, u_ref, st_ref, st_sc,
                *, chunk, tb, nblk):
    heads = GLA_HEADS
    dk = q_ref.shape[1] // heads
    dv = v_ref.shape[1] // heads
    c = pl.program_id(1)

    @pl.when(c == 0)
    def _():
        for h in range(heads):
            st_sc[h] = s0_ref[0, h].T

    ri = lax.broadcasted_iota(I32, (chunk, chunk), 0)
    ci = lax.broadcasted_iota(I32, (chunk, chunk), 1)
    causal = ri >= ci
    kscale = dk ** -0.5
    state = [st_sc[h] for h in range(heads)]
    for j in range(tb // chunk):
        rows = slice(j * chunk, (j + 1) * chunk)
        b_all = _cumsum_rows(la_ref[rows, :], chunk)
        for h in range(heads):
            ks = slice(h * dk, (h + 1) * dk)
            vs = slice(h * dv, (h + 1) * dv)
            b = b_all[:, ks]
            b_end = b[chunk - 1:chunk, :]
            q = q_ref[rows, ks]
            k = k_ref[rows, ks] * kscale
            qg = (q * jnp.exp(b)).astype(BF16)
            kg = (k * jnp.exp(-b)).astype(BF16)
            ke = (k * jnp.exp(b_end - b)).astype(BF16)
            v = v_ref[rows, vs].astype(BF16)
            a = jnp.where(causal, _dot_nt(qg, kg), 0.0).astype(BF16)
            o = _dot_nt(qg, state[h].astype(BF16)) + _dot(a, v)
            state[h] = state[h] * jnp.exp(b_end) + _dot_tn(v, ke)
            on = _rms(o, gn_ref[h:h + 1, :])
            r = r_ref[rows, vs]
            u_ref[rows, vs] = (r * _sigmoid(r) * on).astype(BF16)
    for h in range(heads):
        st_sc[h] = state[h]

    @pl.when(c == nblk - 1)
    def _():
        for h in range(heads):
            st_ref[0, h] = st_sc[h].T


def _gla_recurrence(z, log_a, s0, g_norm, nseq, t, chunk, tb):
    qk_w = log_a.shape[1]
    v_w = 2 * qk_w
    nblk = t // tb
    heads, dk, dv = s0.shape[1:]
    kern = functools.partial(_gla_kernel, chunk=chunk, tb=tb, nblk=nblk)
    row = lambda b, c: b * nblk + c
    return pl.pallas_call(
        kern,
        grid=(nseq, nblk),
        in_specs=[
            pl.BlockSpec((tb, qk_w), lambda b, c: (row(b, c), 0)),
            pl.BlockSpec((tb, qk_w), lambda b, c: (row(b, c), 1)),
            pl.BlockSpec((tb, v_w), lambda b, c: (row(b, c), 1)),
            pl.BlockSpec((tb, v_w), lambda b, c: (row(b, c), 2)),
            pl.BlockSpec((tb, qk_w), lambda b, c: (row(b, c), 0)),
            pl.BlockSpec((1, heads, dk, dv), lambda b, c: (b, 0, 0, 0)),
            pl.BlockSpec((heads, dv), lambda b, c: (0, 0)),
        ],
        out_specs=[
            pl.BlockSpec((tb, v_w), lambda b, c: (row(b, c), 0)),
            pl.BlockSpec((1, heads, dk, dv), lambda b, c: (b, 0, 0, 0)),
        ],
        out_shape=[
            jax.ShapeDtypeStruct((nseq * t, v_w), BF16),
            jax.ShapeDtypeStruct(s0.shape, F32),
        ],
        scratch_shapes=[pltpu.VMEM((heads, dv, dk), F32)],
        compiler_params=_params(("arbitrary", "arbitrary")),
        name="gla_recurrence_t%d" % t,
    )(z, z, z, z, log_a, s0, g_norm)


def _real_rows_selector(nsq, t, t_real):
    i = lax.broadcasted_iota(I32, (nsq * t_real, nsq * t), 0)
    r = lax.broadcasted_iota(I32, (nsq * t_real, nsq * t), 1)
    return jnp.where(r == (i // t_real) * t + i % t_real, 1.0, 0.0).astype(BF16)


def _gla_short_kernel(q_ref, k_ref, v_ref, r_ref, la_ref, s0_ref, gn_ref, u_ref, st_ref,
                      *, t, t_real, nsq):
    heads = GLA_HEADS
    dk = q_ref.shape[1] // heads
    dv = v_ref.shape[1] // heads
    nrow = nsq * t
    width = q_ref.shape[1]
    rc = lax.broadcasted_iota(I32, (nrow, width), 0) & (t - 1)
    log_a = jnp.where(rc < t_real, la_ref[...], 0.0)
    b = _chunk_scan(log_a, rc, t, jnp.add, 0.0)
    pick = _real_rows_selector(nsq, t, t_real)
    b_last = b.reshape(nsq, t, width)[:, t - 1:t, :]
    b_end = jnp.broadcast_to(b_last, (nsq, t, width)).reshape(nrow, width)
    k = k_ref[...] * (dk ** -0.5)
    qg = q_ref[...] * jnp.exp(b)
    kg = k * jnp.exp(-b)
    ke = k * jnp.exp(b_end - b)
    decay = jnp.exp(b_end)
    ri = lax.broadcasted_iota(I32, (nrow, nrow), 0)
    ci = lax.broadcasted_iota(I32, (nrow, nrow), 1)
    shift = t.bit_length() - 1
    mask = ((ri >> shift) == (ci >> shift)) & (ri >= ci)
    for h in range(heads):
        ks = slice(h * dk, (h + 1) * dk)
        vs = slice(h * dv, (h + 1) * dv)
        a = jnp.where(mask, _dot_nt(qg[:, ks].astype(BF16), kg[:, ks].astype(BF16)), 0.0)
        o = _dot(a.astype(BF16), v_ref[:, vs].astype(BF16))
        decay_t = decay[:, ks].T
        o_state = []
        for sq in range(nsq):
            r = slice(sq * t, (sq + 1) * t)
            s0 = s0_ref[sq, h]
            o_state.append(_dot(qg[r, ks].astype(BF16), s0.astype(BF16)))
            upd = _dot_tn(ke[r, ks].astype(BF16), v_ref[r, vs].astype(BF16))
            st_ref[sq, h] = decay_t[:, sq * t:sq * t + 1] * s0 + upd
        on = _rms(o + jnp.concatenate(o_state, axis=0), gn_ref[h:h + 1, :])
        rg = r_ref[:, vs]
        gated = (rg * _sigmoid(rg) * on).astype(BF16)
        u_ref[:, vs] = _dot(pick, gated).astype(BF16)


def _gla_recurrence_short(z, log_a, s0, g_norm, nseq, t, t_real, nsq, row0):
    qk_w = log_a.shape[1]
    v_w = 2 * qk_w
    heads, dk, dv = s0.shape[1:]
    assert nsq * t == dk == LANES and row0 % (nsq * t) == 0
    rb = nsq * t
    blk0 = row0 // rb
    return pl.pallas_call(
        functools.partial(_gla_short_kernel, t=t, t_real=t_real, nsq=nsq),
        grid=(nseq // nsq,),
        in_specs=[
            pl.BlockSpec((rb, qk_w), lambda b: (blk0 + b, 0)),
            pl.BlockSpec((rb, qk_w), lambda b: (blk0 + b, 1)),
            pl.BlockSpec((rb, v_w), lambda b: (blk0 + b, 1)),
            pl.BlockSpec((rb, v_w), lambda b: (blk0 + b, 2)),
            pl.BlockSpec((rb, qk_w), lambda b: (blk0 + b, 0)),
            pl.BlockSpec((nsq, heads, dk, dv), lambda b: (b, 0, 0, 0)),
            pl.BlockSpec((heads, dv), lambda b: (0, 0)),
        ],
        out_specs=[
            pl.BlockSpec((nsq * t_real, v_w), lambda b: (b, 0)),
            pl.BlockSpec((nsq, heads, dk, dv), lambda b: (b, 0, 0, 0)),
        ],
        out_shape=[
            jax.ShapeDtypeStruct((nseq * t_real, v_w), BF16),
            jax.ShapeDtypeStruct(s0.shape, F32),
        ],
        compiler_params=_params(("arbitrary",)),
        name="gla_recurrence_t%d" % t,
    )(z, z, z, z, log_a, s0, g_norm)


def _mlstm_short_kernel(q_ref, k_ref, v_ref, op_ref, gt_ref, e_ref, c0_ref, n0_ref, m0_ref, gn_ref,
                        u_ref, ct_ref, nt_ref, mt_ref, *, t, t_real, nsq):
    heads = MLSTM_HEADS
    pairs = heads // 2
    dv = v_ref.shape[1] // heads
    dk = q_ref.shape[1] // heads
    nrow = nsq * t

    def per_seq(x3):
        return jnp.broadcast_to(x3, (nsq, t, x3.shape[2])).reshape(nrow, x3.shape[2])

    def last(x):
        return x.reshape(nsq, t, x.shape[1])[:, t - 1:t, :]

    rc = lax.broadcasted_iota(I32, (nrow, LANES), 0) & (t - 1)
    lane_g = lax.broadcasted_iota(I32, (nrow, LANES), 1)
    gts = jnp.where(rc < t_real, gt_ref[...],
                    jnp.where(lane_g < heads, NEG_BIG, 0.0))
    pick = _real_rows_selector(nsq, t, t_real)
    f_cum = pltpu.roll(_chunk_scan(gts, rc, t, jnp.add, 0.0), LANES - heads, 1)
    b = gts - f_cum
    cmb = _chunk_scan(b, rc, t, jnp.maximum, NEG_BIG)
    mp = per_seq(m0_ref[...])
    big_m = jnp.maximum(mp, cmb)
    m_t = f_cum + big_m
    m_new = last(m_t)
    mt_ref[...] = m_new
    mn = per_seq(m_new)
    fe = per_seq(last(f_cum))
    w_i = jnp.exp(mp - big_m)
    em = jnp.exp(-m_t)
    k_sc = jnp.exp(fe - f_cum + gts - mn)
    w_c = jnp.exp(fe + mp - mn)
    ex = _dot(_pack_terms([w_i, k_sc, -big_m, em, w_c], heads, SHORT_EXPAND_TERMS), e_ref[...])
    offs = np.cumsum([0] + [heads * w for w in SHORT_EXPAND_WIDTHS])
    q = q_ref[...]
    k = k_ref[...] * (dk ** -0.5)
    qs = q * ex[:, offs[0]:offs[1]]
    ke = k * ex[:, offs[1]:offs[2]]
    neg_m = ex[:, offs[2]:offs[3]]
    em_v = ex[:, offs[3]:offs[4]]
    wc_k = ex[:, offs[4]:offs[5]]
    bt = b.T

    ri = lax.broadcasted_iota(I32, (nrow, nrow), 0)
    ci = lax.broadcasted_iota(I32, (nrow, nrow), 1)
    shift = t.bit_length() - 1
    mask = ((ri >> shift) == (ci >> shift)) & (ri >= ci)
    lane1 = lax.broadcasted_iota(I32, (1, LANES), 1)
    rowi = lax.broadcasted_iota(I32, (LANES, LANES), 0)
    ones_v = jnp.ones((nrow, dv), BF16)
    ones_k = jnp.ones((LANES, dv), BF16)
    for p in range(pairs):
        ps = slice(p * LANES, (p + 1) * LANES)
        qp = q[:, ps]
        qsp = qs[:, ps]
        kpb = k[:, ps].astype(BF16)
        n_rows = per_seq(n0_ref[:, p:p + 1, :])
        intra, den_state, qsh = [], [], []
        for jj in range(2):
            h = 2 * p + jj
            hs = slice(h * dv, (h + 1) * dv)
            mine = (lane1 >= jj * dk) & (lane1 < (jj + 1) * dk)
            qh = jnp.where(mine, qp, 0.0).astype(BF16)
            qsh.append(jnp.where(mine, qsp, 0.0))
            d = neg_m[:, hs] + bt[h:h + 1, :]
            s = _dot_nt(qh, kpb) * jnp.where(mask, jnp.exp(d), 0.0)
            vh = jnp.concatenate([v_ref[:, hs].astype(BF16), ones_v], axis=1)
            intra.append(_dot(s.astype(BF16), vh))
            den_state.append(_dot((qsh[jj] * n_rows).astype(BF16), ones_k))
        num_state = [[], []]
        for sq in range(nsq):
            r = slice(sq * t, (sq + 1) * t)
            lhs = jnp.concatenate([qsh[0][r], qsh[1][r]], axis=0).astype(BF16)
            c_pair = jnp.concatenate([c0_ref[sq, 2 * p], c0_ref[sq, 2 * p + 1]], axis=0)
            res = _dot(lhs, c_pair.astype(BF16))
            num_state[0].append(res[:t])
            num_state[1].append(res[t:])
        for jj in range(2):
            h = 2 * p + jj
            hs = slice(h * dv, (h + 1) * dv)
            num = intra[jj][:, :dv] + jnp.concatenate(num_state[jj], axis=0)
            den = intra[jj][:, dv:] + den_state[jj]
            hh = num / jnp.maximum(jnp.abs(den), em_v[:, hs])
            hn = _rms(hh, gn_ref[h:h + 1, :])
            gated = (_sigmoid(op_ref[:, hs]) * hn).astype(BF16)
            u_ref[:, hs] = _dot(pick, gated).astype(BF16)
        kep = ke[:, ps]
        for sq in range(nsq):
            r = slice(sq * t, (sq + 1) * t)
            vp = v_ref[r, 2 * p * dv:(2 * p + 2) * dv].astype(BF16)
            full = _dot_tn(kep[r].astype(BF16), vp)
            upd = jnp.where(rowi < dk, full[:, :dv], full[:, dv:])
            w_row = w_c[sq * t:sq * t + 1, :]
            w_col = jnp.where(rowi < dk, w_row[:, 2 * p:2 * p + 1], w_row[:, 2 * p + 1:2 * p + 2])
            c_pair = jnp.concatenate([c0_ref[sq, 2 * p], c0_ref[sq, 2 * p + 1]], axis=0)
            c_new = w_col * c_pair + upd
            ct_ref[sq, 2 * p] = c_new[:dk]
            ct_ref[sq, 2 * p + 1] = c_new[dk:]
        k_sum = jnp.sum(kep.reshape(nsq, t, LANES), axis=1, keepdims=True)
        nt_ref[:, p:p + 1, :] = last(wc_k[:, ps]) * n0_ref[:, p:p + 1, :] + k_sum


def _mlstm_recurrence_short(z, gates, c0, n0, m0, g_norm, nseq, t, t_real, nsq, row0):
    heads = g_norm.shape[0]
    v_w = heads * g_norm.shape[1]
    qk_w = v_w // 2
    assert nsq * t == LANES and row0 % (nsq * t) == 0
    expand = _expand_matrix(heads, SHORT_EXPAND_WIDTHS, SHORT_EXPAND_TERMS)
    kern = functools.partial(_mlstm_short_kernel, t=t, t_real=t_real, nsq=nsq)
    rb = nsq * t
    blk0 = row0 // rb
    st4 = lambda b: (b, 0, 0, 0)
    st3 = lambda b: (b, 0, 0)
    return pl.pallas_call(
        kern,
        grid=(nseq // nsq,),
        in_specs=[
            pl.BlockSpec((rb, qk_w), lambda b: (blk0 + b, 0)),
            pl.BlockSpec((rb, qk_w), lambda b: (blk0 + b, 1)),
            pl.BlockSpec((rb, v_w), lambda b: (blk0 + b, 1)),
            pl.BlockSpec((rb, v_w), lambda b: (blk0 + b, 2)),
            pl.BlockSpec((rb, LANES), lambda b: (blk0 + b, 0)),
            pl.BlockSpec(expand.shape, lambda b: (0, 0)),
            pl.BlockSpec((nsq,) + c0.shape[1:], st4),
            pl.BlockSpec((nsq,) + n0.shape[1:], st3),
            pl.BlockSpec((nsq,) + m0.shape[1:], st3),
            pl.BlockSpec(g_norm.shape, lambda b: (0, 0)),
        ],
        out_specs=[
            pl.BlockSpec((nsq * t_real, v_w), lambda b: (b, 0)),
            pl.BlockSpec((nsq,) + c0.shape[1:], st4),
            pl.BlockSpec((nsq,) + n0.shape[1:], st3),
            pl.BlockSpec((nsq,) + m0.shape[1:], st3),
        ],
        out_shape=[
            jax.ShapeDtypeStruct((nseq * t_real, v_w), BF16),
            jax.ShapeDtypeStruct(c0.shape, F32),
            jax.ShapeDtypeStruct(n0.shape, F32),
            jax.ShapeDtypeStruct(m0.shape, F32),
        ],
        compiler_params=_params(("arbitrary",)),
        name="mlstm_recurrence_t%d" % t,
    )(z, z, z, z, gates, expand, c0, n0, m0, g_norm)


EXPAND_WIDTHS = (64, 64, 128, 128)
SHORT_EXPAND_WIDTHS = EXPAND_WIDTHS + (64,)
EXPAND_TERMS = (2, 2, 3, 2)
SHORT_EXPAND_TERMS = EXPAND_TERMS + (2,)


def _expand_matrix(heads, widths, terms):
    assert sum(terms) * heads <= LANES
    cols = []
    slot = 0
    for width, n_terms in zip(widths, terms):
        sel = np.zeros((LANES, heads * width), np.float32)
        for _ in range(n_terms):
            for h in range(heads):
                sel[slot * heads + h, h * width:(h + 1) * width] = 1.0
            slot += 1
        cols.append(sel)
    return jnp.asarray(np.concatenate(cols, axis=1), BF16)


def _chunk_scan(x, rc, n, op, fill):
    s = 1
    while s < n:
        x = op(x, jnp.where(rc >= s, pltpu.roll(x, s, 0), fill))
        s *= 2
    return x


def _pack_terms(values, heads, terms):
    lane = lax.broadcasted_iota(I32, values[0].shape, 1)
    packed = jnp.zeros(values[0].shape, F32)
    slot = 0
    for val, n_terms in zip(values, terms):
        rest = val
        for term in range(n_terms):
            part = rest.astype(BF16).astype(F32)
            rest = rest - part
            moved = part if slot == 0 else pltpu.roll(part, slot * heads, 1)
            packed = jnp.where((lane >= slot * heads) & (lane < (slot + 1) * heads), moved, packed)
            slot += 1
    return packed.astype(BF16)


def _mlstm_block_kernel(q_ref, k_ref, v_ref, op_ref, gt_ref, e_ref, c0_ref, n0_ref, m0_ref, gn_ref,
                        u_ref, ct_ref, nt_ref, mt_ref, c_sc, nb_sc, m_sc, *, chunk, tb, nblk):
    heads = MLSTM_HEADS
    pairs = heads // 2
    dv = v_ref.shape[1] // heads
    dk = q_ref.shape[1] // heads
    c = pl.program_id(1)

    @pl.when(c == 0)
    def _():
        c_sc[...] = c0_ref[0]
        for p in range(pairs):
            nb_sc[p] = jnp.broadcast_to(n0_ref[0, p:p + 1, :], (LANES, LANES)).T
        m_sc[...] = m0_ref[0]

    nch = tb // chunk
    rc = lax.broadcasted_iota(I32, (tb, LANES), 0) & (chunk - 1)
    gts = gt_ref[...]
    f_cum = pltpu.roll(_chunk_scan(gts, rc, chunk, jnp.add, 0.0), LANES - heads, 1)
    b = gts - f_cum
    cmb = _chunk_scan(b, rc, chunk, jnp.maximum, NEG_BIG)
    m_prev = m_sc[...]
    mp_rows, mn_rows, fe_rows, w_c = [], [], [], []
    for j in range(nch):
        last = (j + 1) * chunk - 1
        f_end = f_cum[last:last + 1, :]
        m_new = f_end + jnp.maximum(m_prev, cmb[last:last + 1, :])
        w_c.append(jnp.exp(f_end + m_prev - m_new))
        mp_rows.append(jnp.broadcast_to(m_prev, (chunk, LANES)))
        mn_rows.append(jnp.broadcast_to(m_new, (chunk, LANES)))
        fe_rows.append(jnp.broadcast_to(f_end, (chunk, LANES)))
        m_prev = m_new
    m_sc[...] = m_prev
    mp = jnp.concatenate(mp_rows, axis=0)
    mn = jnp.concatenate(mn_rows, axis=0)
    fe = jnp.concatenate(fe_rows, axis=0)
    big_m = jnp.maximum(mp, cmb)
    w_i = jnp.exp(mp - big_m)
    em = jnp.exp(-(f_cum + big_m))
    k_sc = jnp.exp(fe - f_cum + gts - mn)
    ex = _dot(_pack_terms([w_i, k_sc, -big_m, em], heads, EXPAND_TERMS), e_ref[...])
    o1 = heads * EXPAND_WIDTHS[0]
    o2 = o1 + heads * EXPAND_WIDTHS[1]
    o3 = o2 + heads * EXPAND_WIDTHS[2]
    q = q_ref[...]
    k = k_ref[...] * (dk ** -0.5)
    qs = q * ex[:, :o1]
    ke = k * ex[:, o1:o2]
    neg_m = ex[:, o2:o3]
    em_v = ex[:, o3:]
    bt = b.T

    ri = lax.broadcasted_iota(I32, (chunk, chunk), 0)
    ci = lax.broadcasted_iota(I32, (chunk, chunk), 1)
    causal = ri >= ci
    lane1 = lax.broadcasted_iota(I32, (1, LANES), 1)
    rowi = lax.broadcasted_iota(I32, (LANES, LANES), 0)
    ones_v = jnp.ones((chunk, dv), BF16)
    states = [[None] * pairs for _ in range(nch)]
    for p in range(pairs):
        ps = slice(p * LANES, (p + 1) * LANES)
        cp = c_sc[p]
        nb = nb_sc[p]
        for j in range(nch):
            rows = slice(j * chunk, (j + 1) * chunk)
            states[j][p] = jnp.concatenate([cp, nb], axis=1).astype(BF16)
            vp = jnp.concatenate([v_ref[rows, 2 * p * dv:(2 * p + 2) * dv].astype(BF16), ones_v], axis=1)
            full = _dot_tn(ke[rows, ps].astype(BF16), vp)
            upd = jnp.where(rowi < dk, full[:, :dv], full[:, dv:2 * dv])
            w_col = jnp.where(rowi < dk, w_c[j][:, 2 * p:2 * p + 1], w_c[j][:, 2 * p + 1:2 * p + 2])
            cp = w_col * cp + upd
            nb = w_col * nb + full[:, 2 * dv:]
        c_sc[p] = cp
        nb_sc[p] = nb
    for j in range(nch):
        rows = slice(j * chunk, (j + 1) * chunk)
        for p in range(pairs):
            ps = slice(p * LANES, (p + 1) * LANES)
            qp = q[rows, ps]
            qsp = qs[rows, ps]
            kpb = k[rows, ps].astype(BF16)
            state = states[j][p]
            for jj in range(2):
                h = 2 * p + jj
                hs = slice(h * dv, (h + 1) * dv)
                mine = (lane1 >= jj * dk) & (lane1 < (jj + 1) * dk)
                qh = jnp.where(mine, qp, 0.0).astype(BF16)
                qsh = jnp.where(mine, qsp, 0.0).astype(BF16)
                d = neg_m[rows, h * dv:h * dv + chunk] + bt[h:h + 1, j * chunk:(j + 1) * chunk]
                s = _dot_nt(qh, kpb) * jnp.where(causal, jnp.exp(d), 0.0)
                vh = jnp.concatenate([v_ref[rows, hs].astype(BF16), ones_v], axis=1)
                out = _dot(qsh, state) + _dot(s.astype(BF16), vh)
                hh = out[:, :dv] / jnp.maximum(jnp.abs(out[:, dv:]), em_v[rows, hs])
                hn = _rms(hh, gn_ref[h:h + 1, :])
                u_ref[rows, hs] = (_sigmoid(op_ref[rows, hs]) * hn).astype(BF16)

    @pl.when(c == nblk - 1)
    def _():
        ct_ref[0] = c_sc[...]
        for p in range(pairs):
            nt_ref[0, p:p + 1, :] = nb_sc[p].T[0:1, :]
        mt_ref[0] = m_sc[...]


def _mlstm_recurrence_blocked(z, gates, c0, n0, m0, g_norm, nseq, t, chunk, tb):
    heads = g_norm.shape[0]
    v_w = heads * g_norm.shape[1]
    qk_w = v_w // 2
    nblk = t // tb
    expand = _expand_matrix(heads, EXPAND_WIDTHS, EXPAND_TERMS)
    kern = functools.partial(_mlstm_block_kernel, chunk=chunk, tb=tb, nblk=nblk)
    row = lambda b, c: b * nblk + c
    st4 = lambda b, c: (b, 0, 0, 0)
    st3 = lambda b, c: (b, 0, 0)
    return pl.pallas_call(
        kern,
        grid=(nseq, nblk),
        in_specs=[
            pl.BlockSpec((tb, qk_w), lambda b, c: (row(b, c), 0)),
            pl.BlockSpec((tb, qk_w), lambda b, c: (row(b, c), 1)),
            pl.BlockSpec((tb, v_w), lambda b, c: (row(b, c), 1)),
            pl.BlockSpec((tb, v_w), lambda b, c: (row(b, c), 2)),
            pl.BlockSpec((tb, LANES), lambda b, c: (row(b, c), 0)),
            pl.BlockSpec(expand.shape, lambda b, c: (0, 0)),
            pl.BlockSpec((1,) + c0.shape[1:], st4),
            pl.BlockSpec((1,) + n0.shape[1:], st3),
            pl.BlockSpec((1,) + m0.shape[1:], st3),
            pl.BlockSpec(g_norm.shape, lambda b, c: (0, 0)),
        ],
        out_specs=[
            pl.BlockSpec((tb, v_w), lambda b, c: (row(b, c), 0)),
            pl.BlockSpec((1,) + c0.shape[1:], st4),
            pl.BlockSpec((1,) + n0.shape[1:], st3),
            pl.BlockSpec((1,) + m0.shape[1:], st3),
        ],
        out_shape=[
            jax.ShapeDtypeStruct((nseq * t, v_w), BF16),
            jax.ShapeDtypeStruct(c0.shape, F32),
            jax.ShapeDtypeStruct(n0.shape, F32),
            jax.ShapeDtypeStruct(m0.shape, F32),
        ],
        scratch_shapes=[
            pltpu.VMEM(c0.shape[1:], F32),
            pltpu.VMEM(c0.shape[1:], F32),
            pltpu.VMEM(m0.shape[1:], F32),
        ],
        compiler_params=_params(("arbitrary", "arbitrary")),
        name="mlstm_recurrence_t%d" % t,
    )(z, z, z, z, gates, expand, c0, n0, m0, g_norm)


def _out_proj_kernel(up_ref, us_ref, xp_ref, xs_ref, w_ref, g_ref, cast_ref,
                     xo_ref, hn_ref, cast_out_ref, wb_sc, *, n_first):
    i = pl.program_id(0)

    @pl.when(i == 0)
    def _():
        wb_sc[...] = w_ref[...].astype(BF16)

    cast_out_ref[...] = cast_ref[...].astype(BF16)
    u = _pick_rows(i, n_first, up_ref, us_ref)
    x = _pick_rows(i, n_first, xp_ref, xs_ref) + _dot(u, wb_sc[...])
    xo_ref[...] = x
    hn_ref[...] = _rms(x, g_ref[...]).astype(hn_ref.dtype)


def _out_proj(u_p, u_s, x_p, x_s, w, g, cast_src):
    d = x_p.shape[1]
    n = x_p.shape[0] + x_s.shape[0]
    n_first = x_p.shape[0] // TM
    cast_in, cast_out, cast_shape = _cast_side_job(cast_src, n // TM)
    return pl.pallas_call(
        functools.partial(_out_proj_kernel, n_first=n_first),
        grid=(n // TM,),
        in_specs=_split_specs(n_first, u_p.shape[1]) + _split_specs(n_first, d) + [
            _resident_spec(w.shape, lambda i: (0, 0)),
            pl.BlockSpec((1, d), lambda i: (0, 0)),
            cast_in,
        ],
        out_specs=[pl.BlockSpec((TM, d), lambda i: (i, 0)), pl.BlockSpec((TM, d), lambda i: (i, 0)),
                   cast_out],
        out_shape=[jax.ShapeDtypeStruct((n, d), F32), jax.ShapeDtypeStruct((n, d), BF16), cast_shape],
        scratch_shapes=[pltpu.VMEM(w.shape, BF16)],
        compiler_params=_params(("arbitrary",)),
        name="out_proj",
    )(u_p, u_s, x_p, x_s, w, g, cast_src)


def _out_proj_router_kernel(up_ref, us_ref, x_ref, w_ref, g_ref, wr_ref, br_ref,
                            xo_ref, hn_ref, gate_ref, idx_ref, cnt_ref, logit_sc, wb_sc,
                            *, n_first, n_tiles):
    i = pl.program_id(0)

    @pl.when(i == 0)
    def _():
        logit_sc[...] = jnp.zeros_like(logit_sc)
        wb_sc[...] = w_ref[...].astype(BF16)

    logits = logit_sc[(i + 1) & 1]
    u = _pick_rows(jnp.minimum(i, n_tiles - 1), n_first, up_ref, us_ref)
    x = x_ref[...] + _dot(u, wb_sc[...])
    xo_ref[...] = x
    hn = _rms(x, g_ref[...])
    hn_hi = hn.astype(BF16)
    hn_ref[...] = hn_hi
    hn_lo = (hn - hn_hi.astype(F32)).astype(BF16)
    logit_sc[i & 1] = (_dot(hn_hi, wr_ref[0]) + _dot(hn_lo, wr_ref[0]) + _dot(hn_hi, wr_ref[1])
                       + br_ref[...])
    tm = logits.shape[0]
    lane = lax.broadcasted_iota(I32, logits.shape, 1)
    valid = lane < N_EXPERTS
    logits = jnp.where(valid, logits, -jnp.inf)
    ex = jnp.exp(logits - jnp.max(logits, axis=-1, keepdims=True))
    probs = jnp.where(valid, ex / jnp.sum(ex, axis=-1, keepdims=True), -1.0)
    p1 = jnp.max(probs, axis=-1, keepdims=True)
    i1 = jnp.min(jnp.where(probs == p1, lane, LANES), axis=-1, keepdims=True)
    rest = jnp.where(lane == i1, -1.0, probs)
    p2 = jnp.max(rest, axis=-1, keepdims=True)
    i2 = jnp.min(jnp.where(rest == p2, lane, LANES), axis=-1, keepdims=True)
    tot = p1 + p2
    onehot = ((lane == i1) | (lane == i2)).astype(BF16)
    rr = lax.broadcasted_iota(I32, (tm, tm), 0)
    cc = lax.broadcasted_iota(I32, (tm, tm), 1)
    strict = (rr > cc).astype(BF16)
    before = _dot(strict, onehot)
    r1 = jnp.sum(jnp.where(lane == i1, before, 0.0), axis=-1, keepdims=True)
    r2 = jnp.sum(jnp.where(lane == i2, before, 0.0), axis=-1, keepdims=True)
    gate_ref[...] = jnp.where(lane == 0, p1 / tot, jnp.where(lane == 1, p2 / tot, 0.0))
    idx_ref[...] = jnp.where(lane == 0, i1,
                             jnp.where(lane == 1, i2,
                                       jnp.where(lane == 2, r1.astype(I32),
                                                 jnp.where(lane == 3, r2.astype(I32), 0))))
    cnt_ref[0] = jnp.sum(onehot.astype(F32), axis=0, keepdims=True)


def _out_proj_router(u_p, u_s, x, w, g, w_router, b_router):
    n, d = x.shape
    n_first = u_p.shape[0] // TM
    n_tiles = n // TM
    row = lambda i: (jnp.minimum(i, n_tiles - 1), 0)
    routed = lambda i: (jnp.maximum(i - 1, 0), 0)
    fix = lambda i: (0, 0)
    u_specs = [pl.BlockSpec((TM, u_p.shape[1]), lambda i: (jnp.minimum(i, n_first - 1), 0)),
               pl.BlockSpec((TM, u_p.shape[1]),
                            lambda i: (jnp.clip(i - n_first, 0, n_tiles - n_first - 1), 0))]
    return pl.pallas_call(
        functools.partial(_out_proj_router_kernel, n_first=n_first, n_tiles=n_tiles),
        grid=(n_tiles + 1,),
        in_specs=u_specs + [
            pl.BlockSpec((TM, d), row),
            _resident_spec(w.shape, fix),
            pl.BlockSpec((1, d), fix),
            pl.BlockSpec(w_router.shape, lambda i: (0, 0, 0)),
            pl.BlockSpec((1, LANES), fix),
        ],
        out_specs=[
            pl.BlockSpec((TM, d), row),
            pl.BlockSpec((TM, d), row),
            pl.BlockSpec((TM, LANES), routed),
            pl.BlockSpec((TM, LANES), routed),
            pl.BlockSpec((1, 1, LANES), lambda i: (jnp.maximum(i - 1, 0), 0, 0)),
        ],
        out_shape=[
            jax.ShapeDtypeStruct((n, d), F32),
            jax.ShapeDtypeStruct((n, d), BF16),
            jax.ShapeDtypeStruct((n, LANES), F32),
            jax.ShapeDtypeStruct((n, LANES), I32),
            jax.ShapeDtypeStruct((n_tiles, 1, LANES), F32),
        ],
        scratch_shapes=[pltpu.VMEM((2, TM, LANES), F32), pltpu.VMEM(w.shape, BF16)],
        compiler_params=_params(("arbitrary",)),
        name="out_proj_router",
    )(u_p, u_s, x, w, g, w_router, b_router)


def _swiglu_kernel(hn_ref, x_ref, wg_ref, wu_ref, wd_ref, g_ref, cast_ref, xo_ref, ho_ref, cast_out_ref,
                   *, nchunk):
    cast_out_ref[...] = cast_ref[...].astype(BF16)
    hn = hn_ref[...]
    width = wd_ref.shape[0] // nchunk
    x = x_ref[...]
    for c in range(nchunk):
        cs = slice(c * width, (c + 1) * width)
        gate = _dot(hn, wg_ref[:, cs])
        up = _dot(hn, wu_ref[:, cs])
        act = (gate * _sigmoid(gate) * up).astype(BF16)
        x = x + _dot(act, wd_ref[cs, :])
    xo_ref[...] = x
    ho_ref[...] = _rms(x, g_ref[...]).astype(BF16)


def _swiglu(hn, x, w_gu, w_down, g, nchunk, cast_src):
    n, d = x.shape
    dff = w_down.shape[0]
    kern = functools.partial(_swiglu_kernel, nchunk=nchunk)
    row = lambda i: (i, 0)
    cast_in, cast_out, cast_shape = _cast_side_job(cast_src, n // TM)
    return pl.pallas_call(
        kern,
        grid=(n // TM,),
        in_specs=[
            pl.BlockSpec((TM, d), row),
            pl.BlockSpec((TM, d), row),
            _resident_spec((d, dff), lambda i: (0, 0)),
            _resident_spec((d, dff), lambda i: (0, 1)),
            _resident_spec((dff, d), lambda i: (0, 0)),
            pl.BlockSpec((1, d), lambda i: (0, 0)),
            cast_in,
        ],
        out_specs=[pl.BlockSpec((TM, d), row), pl.BlockSpec((TM, d), row), cast_out],
        out_shape=[jax.ShapeDtypeStruct((n, d), F32), jax.ShapeDtypeStruct((n, d), BF16), cast_shape],
        compiler_params=_params(("arbitrary",)),
        name="dense_swiglu",
    )(hn, x, w_gu, w_gu, w_down, g, cast_src)


def _row_copy(src_ref, src_row, dst_ref, dst_row, sem):
    return pltpu.make_async_copy(src_ref.at[pl.ds(src_row, 1), :], dst_ref.at[pl.ds(dst_row, 1), :], sem)


SUBLANES = 8


def _zero_rows(zero_ref, dst_ref, dst_row, count, limit, sem, wait):
    head = jnp.minimum((-dst_row) & (SUBLANES - 1), count)
    for j in range(SUBLANES - 1):
        @pl.when(j < head)
        def _(j=j):
            copy = _row_copy(zero_ref, 0, dst_ref, dst_row + j, sem)
            copy.wait() if wait else copy.start()

    rest = count - head
    first = dst_row + head
    for bit in range(SUBLANES.bit_length() - 1, limit.bit_length() - 1):
        size = 1 << bit
        done = (rest >> (bit + 1)) << (bit + 1)

        @pl.when(((rest >> bit) & 1) == 1)
        def _(size=size, done=done):
            start = pl.multiple_of(first + done, SUBLANES)
            copy = pltpu.make_async_copy(zero_ref.at[pl.ds(0, size), :],
                                         dst_ref.at[pl.ds(start, size), :], sem)
            copy.wait() if wait else copy.start()


def _expert_kernel(te_ref, tv_ref, xs_ref, wg_ref, wu_ref, wd_ref, ys_ref, *, nchunk):
    i = pl.program_id(0)
    half = xs_ref.shape[0] // 2
    width = wg_ref.shape[2] // nchunk

    def swiglu_rows(rows):
        x = xs_ref[rows, :].astype(BF16)
        acc = None
        for c in range(nchunk):
            cs = slice(c * width, (c + 1) * width)
            gate = _dot(x, wg_ref[0, :, cs])
            up = _dot(x, wu_ref[0, :, cs])
            act = (gate * _sigmoid(gate) * up).astype(BF16)
            part = _dot(act, wd_ref[0, cs, :])
            acc = part if acc is None else acc + part
        ys_ref[rows, :] = acc.astype(BF16).astype(F32)

    @pl.when(tv_ref[i] == 2)
    def _():
        swiglu_rows(slice(None))

    @pl.when(tv_ref[i] == 1)
    def _():
        swiglu_rows(slice(0, half))
        ys_ref[half:, :] = jnp.zeros((half, ys_ref.shape[1]), F32)

    @pl.when(tv_ref[i] == 0)
    def _():
        ys_ref[...] = jnp.zeros_like(ys_ref)


def _experts(tile_expert, tile_valid, xs, w_gu, w_down):
    r, d = xs.shape
    n_tiles = r // TM_EXPERT
    dff = w_down.shape[1]
    kern = functools.partial(_expert_kernel, nchunk=EXPERT_CHUNKS)
    return pl.pallas_call(
        kern,
        grid_spec=pltpu.PrefetchScalarGridSpec(
            num_scalar_prefetch=2,
            grid=(r // TM_EXPERT,),
            in_specs=[
                pl.BlockSpec((TM_EXPERT, d), lambda i, te, tv: (jnp.minimum(i, tv[n_tiles]), 0)),
                pl.BlockSpec((1, d, dff), lambda i, te, tv: (te[i], 0, 0)),
                pl.BlockSpec((1, d, dff), lambda i, te, tv: (te[i], 0, 1)),
                pl.BlockSpec((1, dff, d), lambda i, te, tv: (te[i], 0, 0)),
            ],
            out_specs=pl.BlockSpec((TM_EXPERT, d), lambda i, te, tv: (i, 0)),
        ),
        out_shape=jax.ShapeDtypeStruct((r, d), F32),
        compiler_params=_params(("arbitrary",)),
        name="moe_experts",
    )(tile_expert, tile_valid, xs, w_gu, w_gu, w_down)


RUN_BUF_ROWS = 2 * TM + LANES


def _run_positions(idx, off_row):
    lane = lax.broadcasted_iota(I32, idx.shape, 1)
    pos = []
    for slot in range(2):
        off = jnp.sum(jnp.where(lane == idx[:, slot:slot + 1], off_row, 0.0), axis=-1, keepdims=True)
        pos.append(off.astype(I32) + idx[:, 2 + slot:3 + slot])
    return pos


def _run_copies(tab_ref, step, hbm_ref, buf_ref, slot, sem, to_hbm, wait):
    for e in range(N_EXPERTS):
        entry = (step * N_EXPERTS + e) * 3
        off, row, length = tab_ref[entry], tab_ref[entry + 1], tab_ref[entry + 2]
        for bit in range(SUBLANES.bit_length() - 1, TM.bit_length()):
            size = 1 << bit
            done = (length >> (bit + 1)) << (bit + 1)

            @pl.when(((length >> bit) & 1) == 1)
            def _(size=size, done=done, off=off, row=row):
                in_buf = buf_ref.at[slot, pl.ds(pl.multiple_of(off + done, SUBLANES), size), :]
                in_hbm = hbm_ref.at[pl.ds(pl.multiple_of(row + done, SUBLANES), size), :]
                copy = (pltpu.make_async_copy(in_buf, in_hbm, sem.at[slot]) if to_hbm
                        else pltpu.make_async_copy(in_hbm, in_buf, sem.at[slot]))
                copy.wait() if wait else copy.start()


def _run_dispatch_kernel(tab_ref, pad_ref, hn_ref, idx_ref, off_ref, xs_ref, buf, zero_sc, sem, zsem,
                         *, n_steps):
    j = pl.program_id(0)
    slot = j & 1

    @pl.when(j == 0)
    def _():
        zero_sc[...] = jnp.zeros_like(zero_sc)
        rows = zero_sc.shape[0]
        for wait in (False, True):
            for e in range(N_EXPERTS):
                _zero_rows(zero_sc, xs_ref, pad_ref[2 * e], pad_ref[2 * e + 1], 2 * rows, zsem, wait)

            def tail(t, carry, wait=wait):
                first = pl.multiple_of(pad_ref[2 * N_EXPERTS] + t * rows, SUBLANES)
                copy = pltpu.make_async_copy(zero_sc, xs_ref.at[pl.ds(first, rows), :], zsem)
                copy.wait() if wait else copy.start()
                return carry

            lax.fori_loop(0, pad_ref[2 * N_EXPERTS + 1] // rows, tail, 0)

    @pl.when(j >= 2)
    def _():
        _run_copies(tab_ref, j - 2, xs_ref, buf, slot, sem, True, True)

    pos0, pos1 = _run_positions(idx_ref[...], off_ref[0])
    lane = lax.broadcasted_iota(I32, (hn_ref.shape[0], buf.shape[1]), 1)
    sel = jnp.where((lane == pos0) | (lane == pos1), 1.0, 0.0).astype(BF16)
    buf[slot] = _dot_tn(sel, hn_ref[...])
    _run_copies(tab_ref, j, xs_ref, buf, slot, sem, True, False)

    @pl.when(j == n_steps - 1)
    def _():
        _run_copies(tab_ref, j - 1, xs_ref, buf, 1 - slot, sem, True, True)
        _run_copies(tab_ref, j, xs_ref, buf, slot, sem, True, True)


def _run_dispatch(run_table, pad_table, hn, idx, run_offsets, n_rows):
    n, d = hn.shape
    n_steps = n // TM
    assert n_steps >= 2
    return pl.pallas_call(
        functools.partial(_run_dispatch_kernel, n_steps=n_steps),
        grid_spec=pltpu.PrefetchScalarGridSpec(
            num_scalar_prefetch=2,
            grid=(n_steps,),
            in_specs=[
                pl.BlockSpec((TM, d), lambda i, *_: (i, 0)),
                pl.BlockSpec((TM, LANES), lambda i, *_: (i, 0)),
                pl.BlockSpec((1, 1, LANES), lambda i, *_: (i, 0, 0)),
            ],
            out_specs=pl.BlockSpec(memory_space=pl.ANY),
            scratch_shapes=[pltpu.VMEM((2, RUN_BUF_ROWS, d), F32),
                            pltpu.VMEM((TM_EXPERT // 2, d), F32),
                            pltpu.SemaphoreType.DMA((2,)), pltpu.SemaphoreType.DMA(())],
        ),
        out_shape=jax.ShapeDtypeStruct((n_rows, d), F32),
        compiler_params=_params(("arbitrary",)),
        name="moe_dispatch",
    )(run_table, pad_table, hn, idx, run_offsets)


def _run_combine_kernel(tab_ref, ys_ref, x_ref, gate_ref, idx_ref, off_ref, g_ref, yp_ref, ysm_ref,
                        buf, sem, *, n_first, n_steps):
    j = pl.program_id(0)
    slot = j & 1

    @pl.when(j == 0)
    def _():
        buf[...] = jnp.zeros_like(buf)
        _run_copies(tab_ref, 0, ys_ref, buf, 0, sem, False, False)

    @pl.when(j + 1 < n_steps)
    def _():
        _run_copies(tab_ref, j + 1, ys_ref, buf, 1 - slot, sem, False, False)

    pos = _run_positions(idx_ref[...], off_ref[0])
    lane = lax.broadcasted_iota(I32, (x_ref.shape[0], buf.shape[1]), 1)
    sel = [jnp.where(lane == pos[k], 1.0, 0.0).astype(BF16) for k in range(2)]
    _run_copies(tab_ref, j, ys_ref, buf, slot, sem, False, True)
    runs = buf[slot].astype(BF16)
    gate = gate_ref[...]
    y = x_ref[...]
    for k in range(2):
        y = y + gate[:, k:k + 1] * _dot(sel[k], runs)
    y = _rms(y, g_ref[...])

    @pl.when(j < n_first)
    def _():
        yp_ref[...] = y

    @pl.when(j >= n_first)
    def _():
        ysm_ref[...] = y


def _run_combine(run_table, ys, x, gates, idx, run_offsets, g, n_p):
    n, d = x.shape
    n_first = n_p // TM
    n_steps = n // TM
    p_spec, s_spec = _split_specs(n_first, d)
    row = lambda i, *_: (i, 0)
    return pl.pallas_call(
        functools.partial(_run_combine_kernel, n_first=n_first, n_steps=n_steps),
        grid_spec=pltpu.PrefetchScalarGridSpec(
            num_scalar_prefetch=1,
            grid=(n_steps,),
            in_specs=[
                pl.BlockSpec(memory_space=pl.ANY),
                pl.BlockSpec((TM, d), row),
                pl.BlockSpec((TM, LANES), row),
                pl.BlockSpec((TM, LANES), row),
                pl.BlockSpec((1, 1, LANES), lambda i, *_: (i, 0, 0)),
                pl.BlockSpec((1, d), lambda i, *_: (0, 0)),
            ],
            out_specs=[p_spec, s_spec],
            scratch_shapes=[pltpu.VMEM((2, RUN_BUF_ROWS, d), F32), pltpu.SemaphoreType.DMA((2,))],
        ),
        out_shape=[jax.ShapeDtypeStruct((n_p, d), F32), jax.ShapeDtypeStruct((n - n_p, d), F32)],
        compiler_params=_params(("arbitrary",)),
        name="moe_combine",
    )(run_table, ys, x, gates, idx, run_offsets, g)


def _run_tables(tile_counts, n_rows):
    cnt = tile_counts[:, 0, :N_EXPERTS].astype(I32)
    run = ((cnt + SUBLANES - 1) // SUBLANES) * SUBLANES
    off = jnp.cumsum(run, axis=1) - run
    used = jnp.sum(run, axis=0)
    padded = ((used + TM_EXPERT - 1) // TM_EXPERT) * TM_EXPERT
    ends = jnp.cumsum(padded)
    starts = ends - padded
    row = starts[None, :] + jnp.cumsum(run, axis=0) - run
    run_table = jnp.stack([off, row, run], axis=2).reshape(-1)
    run_offsets = jnp.pad(off.astype(F32), ((0, 0), (0, LANES - N_EXPERTS)))[:, None, :]
    n_tiles = n_rows // TM_EXPERT
    used_tiles = ends[-1] // TM_EXPERT
    pad_table = jnp.stack([starts + used, padded - used], axis=1).reshape(-1)
    pad_table = jnp.concatenate([pad_table, jnp.stack([ends[-1], n_rows - ends[-1]])])
    tile_start = jnp.arange(n_tiles, dtype=I32) * TM_EXPERT
    tile_expert = jnp.sum((tile_start[:, None] >= ends[None, :]).astype(I32), axis=1)
    last_expert = jnp.sum((ends[-1] - 1 >= ends).astype(I32))
    tile_expert = jnp.minimum(tile_expert, last_expert)
    used_end = (starts + used)[tile_expert]
    half = TM_EXPERT // 2
    tile_rows = jnp.clip(used_end - tile_start, 0, TM_EXPERT)
    tile_valid = jnp.where(tile_start < ends[-1], (tile_rows + half - 1) // half, 0)
    tile_valid = jnp.concatenate([tile_valid, (used_tiles - 1)[None]])
    return run_table, run_offsets, pad_table, tile_expert, tile_valid


def _pad_time(a, nseq, t):
    w = a.shape[1]
    a = jnp.pad(a.reshape(nseq, t, w), ((0, 0), (0, SAMPLE_PAD_T - t), (0, 0)))
    return a.reshape(nseq * SAMPLE_PAD_T, w)


def kernel(x_prompt, x_sample, state_gla_S, state_mlstm_C, state_mlstm_n, state_mlstm_m,
           norm_mix, norm_ffn, norm_final,
           gla_w_in, gla_w_a2, gla_b_a, gla_g_norm, gla_w_out,
           mlstm_w_in, mlstm_b_gate, mlstm_g_norm, mlstm_w_out,
           ffn_w_gu, ffn_w_down,
           moe_w_router, moe_b_router, moe_w_gu, moe_w_down):
    bp, tp, d = x_prompt.shape
    bs, ts, _ = x_sample.shape
    n_p = bp * tp
    n_s = bs * ts
    n = n_p + n_s
    assert n_p % TM == 0 and n_s % TM == 0 and tp % (2 * CHUNK) == 0 and ts <= SAMPLE_PAD_T
    assert norm_mix.shape[0] == 2, "one GLA layer followed by one mLSTM layer"
    qk_w = d // 2
    main_w = 2 * qk_w + 2 * d

    x_p = x_prompt.reshape(n_p, d)
    x_s = x_sample.reshape(n_s, d)

    w_a2 = jnp.pad(gla_w_a2[0], ((0, LANES - GLA_RANK), (0, 0))).astype(BF16)
    z, log_a, ffn_wgu = _gla_in_proj(x_p, _pad_time(x_s, bs, ts), norm_mix[0][None], gla_w_in[0].T,
                                     main_w, w_a2, gla_b_a[0][None], ffn_w_gu[0])
    s0_p = jnp.zeros((bp,) + state_gla_S.shape[2:], F32)
    u_p, s_p = _gla_recurrence(z, log_a, s0_p, gla_g_norm[0], bp, tp, CHUNK, 16 * CHUNK)
    u_s, s_s = _gla_recurrence_short(z, log_a, state_gla_S[0], gla_g_norm[0], bs, SAMPLE_PAD_T, ts,
                                     SAMPLE_SEQS_PER_STEP, n_p)
    x, hn, ffn_wd = _out_proj(u_p, u_s, x_p, x_s, gla_w_out[0],
                              norm_ffn[0][None], ffn_w_down[0])
    x, hn, moe_wgu = _swiglu(hn, x, ffn_wgu, ffn_wd,
                             norm_mix[1][None], SWIGLU_CHUNKS,
                             moe_w_gu[0].reshape(N_EXPERTS * d, moe_w_gu.shape[3]))

    b_gt = jnp.pad(mlstm_b_gate[0], (0, LANES - 2 * MLSTM_HEADS))[None]
    dff_e = moe_w_down.shape[2]
    z, gates, moe_wd = _mlstm_in_proj(hn, n_p, _pad_time(hn[n_p:], bs, ts), mlstm_w_in[0].T, main_w,
                                      b_gt, moe_w_down[0].reshape(N_EXPERTS * dff_e, d))
    pairs = MLSTM_HEADS // 2
    dk2 = 2 * state_mlstm_C.shape[3]
    dvm = state_mlstm_C.shape[4]
    c0_p = jnp.zeros((bp, pairs, dk2, dvm), F32)
    n0_p = jnp.zeros((bp, pairs, dk2), F32)
    m0_p = jnp.zeros((bp, 1, LANES), F32)
    u_p, c_p, nn_p, m_p = _mlstm_recurrence_blocked(z, gates, c0_p, n0_p, m0_p,
                                                    mlstm_g_norm[0], bp, tp, CHUNK, 16 * CHUNK)
    c0_s = state_mlstm_C[0]
    n0_s = state_mlstm_n[0].reshape(bs, pairs, dk2)
    m0_s = jnp.pad(state_mlstm_m[0], ((0, 0), (0, LANES - MLSTM_HEADS)))[:, None, :]
    u_s, c_s, nn_s, m_s = _mlstm_recurrence_short(z, gates, c0_s, n0_s, m0_s, mlstm_g_norm[0], bs,
                                                  SAMPLE_PAD_T, ts, SAMPLE_SEQS_PER_STEP, n_p)

    w_r = jnp.pad(moe_w_router[0], ((0, 0), (0, LANES - N_EXPERTS)))
    w_r_hi = w_r.astype(BF16)
    w_r = jnp.stack([w_r_hi, (w_r - w_r_hi.astype(F32)).astype(BF16)])
    b_r = jnp.pad(moe_b_router[0], (0, LANES - N_EXPERTS))[None]
    x, hn, route_g, route_i, tile_counts = _out_proj_router(
        u_p, u_s, x, mlstm_w_out[0], norm_ffn[1][None], w_r, b_r)
    n_rows = 2 * n + (n // TM) * N_EXPERTS * (SUBLANES - 1) + N_EXPERTS * (TM_EXPERT - 1)
    n_rows = -(-n_rows // TM_EXPERT) * TM_EXPERT
    run_table, run_offsets, pad_table, tile_expert, tile_valid = _run_tables(tile_counts, n_rows)
    xs = _run_dispatch(run_table, pad_table, hn, route_i, run_offsets, n_rows)
    ys = _experts(tile_expert, tile_valid, xs, moe_wgu.reshape(moe_w_gu.shape[1:]),
                  moe_wd.reshape(moe_w_down.shape[1:]))
    y_p, y_s = _run_combine(run_table, ys, x, route_g, route_i, run_offsets, norm_final[None], n_p)

    y_prompt = y_p.reshape(bp, tp, d)
    y_sample = y_s.reshape(bs, ts, d)
    hd = state_mlstm_C.shape[2:]
    return (y_prompt, y_sample,
            s_p[None], c_p.reshape((1, bp) + hd), nn_p.reshape(1, bp, hd[0], hd[1]),
            m_p[None, :, 0, :MLSTM_HEADS],
            s_s[None], c_s[None], nn_s.reshape(1, bs, hd[0], hd[1]), m_s[None, :, 0, :MLSTM_HEADS])


def _mlstm_in_proj(hn, n_p, hn_s, w_in, wz, b_gate, cast_src):
    d = hn.shape[1]
    n = n_p + hn_s.shape[0]
    n_first = n_p // TM
    cast_in, cast_out, cast_shape = _cast_side_job(cast_src, n // TM)

    def kern(hp_ref, hs_ref, w_ref, wg_ref, b_ref, cast_ref, z_ref, gate_ref, cast_out_ref,
             wb_sc, wg_sc):
        @pl.when(pl.program_id(0) == 0)
        def _():
            wb_sc[...] = w_ref[...].T.astype(BF16)
            wg_sc[...] = _gate_weights(wg_ref)

        cast_out_ref[...] = cast_ref[...].astype(BF16)
        h = _pick_rows(pl.program_id(0), n_first, hp_ref, hs_ref)
        width = wz // IN_PROJ_COL_CHUNKS
        for c in range(IN_PROJ_COL_CHUNKS):
            cs = slice(c * width, (c + 1) * width)
            z_ref[:, cs] = _dot(h, wb_sc[:, cs])
        gp = _dot(h, wg_sc[...]) + b_ref[...]
        gc = GATE_CAP * jnp.tanh(gp * (1.0 / GATE_CAP))
        lane = lax.broadcasted_iota(I32, gc.shape, 1)
        out = jnp.where(lane < MLSTM_HEADS, gc, _log_sigmoid(gc))
        gate_ref[...] = jnp.where(lane < 2 * MLSTM_HEADS, out, 0.0)

    return pl.pallas_call(
        kern,
        grid=(n // TM,),
        in_specs=_split_specs(n_first, d) + [
            _resident_spec((wz, d), lambda i: (0, 0)),
            _gate_rows_spec(w_in, wz),
            pl.BlockSpec((1, LANES), lambda i: (0, 0)),
            cast_in,
        ],
        out_specs=[pl.BlockSpec((TM, wz), lambda i: (i, 0)), pl.BlockSpec((TM, LANES), lambda i: (i, 0)),
                   cast_out],
        out_shape=[jax.ShapeDtypeStruct((n, wz), F32), jax.ShapeDtypeStruct((n, LANES), F32),
                   cast_shape],
        scratch_shapes=[pltpu.VMEM((d, wz), BF16), pltpu.VMEM((d, LANES), BF16)],
        compiler_params=_params(("arbitrary",)),
        name="in_proj_mlstm",
    )(hn, hn_s, w_in, w_in, b_gate, cast_src)
```

```python
import functools

import jax
import jax.numpy as jnp
import numpy as np
from jax import lax
from jax.experimental import pallas as pl
from jax.experimental.pallas import tpu as pltpu

F32 = jnp.float32
BF16 = jnp.bfloat16
I32 = jnp.int32

EPS = 1e-6
GLA_HEADS = 4
GLA_RANK = 16
GLA_TAU = 16.0
MLSTM_HEADS = 8
GATE_CAP = 15.0
CHUNK = 64
N_EXPERTS = 8
NEG_BIG = -1e30

LANES = 128
TM = 512
TM_EXPERT = 512
SAMPLE_PAD_T = 8
SAMPLE_SEQS_PER_STEP = 16
SWIGLU_CHUNKS = 11
EXPERT_CHUNKS = 7
VMEM_LIMIT = 56 * 1024 * 1024


def _dot(a, b):
    return jnp.dot(a, b, preferred_element_type=F32)


def _dot_nt(a, b):
    return lax.dot_general(a, b, (((1,), (1,)), ((), ())), preferred_element_type=F32)


def _dot_tn(a, b):
    return lax.dot_general(a, b, (((0,), (0,)), ((), ())), preferred_element_type=F32)


def _sigmoid(x):
    return 1.0 / (1.0 + jnp.exp(-x))


def _log_sigmoid(x):
    return jnp.minimum(x, 0.0) - jnp.log(1.0 + jnp.exp(-jnp.abs(x)))


def _rms(x, g):
    return x * lax.rsqrt(jnp.mean(x * x, axis=-1, keepdims=True) + EPS) * g


def _cumsum_rows(x, n):
    ridx = lax.broadcasted_iota(I32, x.shape, 0)
    s = 1
    while s < n:
        x = x + jnp.where(ridx >= s, pltpu.roll(x, s, 0), 0.0)
        s *= 2
    return x


def _params(sem):
    return pltpu.CompilerParams(dimension_semantics=sem, vmem_limit_bytes=VMEM_LIMIT)


def _resident_spec(block_shape, index_map):
    return pl.BlockSpec(block_shape, index_map, pipeline_mode=pl.Buffered(1))


def _cast_side_job(src, n_steps):
    rows, cols = src.shape
    n_slabs = max(s for s in range(1, n_steps + 1) if rows % (16 * s) == 0)
    spec = pl.BlockSpec((rows // n_slabs, cols), lambda i, *_: (jnp.minimum(i, n_slabs - 1), 0))
    return spec, spec, jax.ShapeDtypeStruct(src.shape, BF16)


IN_PROJ_COL_CHUNKS = 4
ROW_PARTS = 2


def _pick_rows(i, n_first, first_ref, second_ref):
    return jnp.where(i < n_first, first_ref[...], second_ref[...])


def _split_specs(n_first, width):
    return [pl.BlockSpec((TM, width), lambda i, *_: (jnp.minimum(i, n_first - 1), 0)),
            pl.BlockSpec((TM, width), lambda i, *_: (jnp.maximum(i - n_first, 0), 0))]


def _gate_weights(wg_ref):
    rows, d = wg_ref.shape
    full = jnp.concatenate([wg_ref[...], jnp.zeros((LANES - rows, d), F32)], axis=0)
    return full.T.astype(BF16)


def _gla_in_proj_kernel(xp_ref, xs_ref, g_ref, w_ref, wg_ref, w2_ref, b_ref, cast_ref,
                        z_ref, gate_ref, cast_out_ref, wb_sc, wg_sc, *, n_first):
    @pl.when(pl.program_id(0) == 0)
    def _():
        wb_sc[...] = w_ref[...].T.astype(BF16)
        wg_sc[...] = _gate_weights(wg_ref)

    cast_out_ref[...] = cast_ref[...].astype(BF16)
    first = pl.program_id(0) < n_first
    width = wb_sc.shape[1] // IN_PROJ_COL_CHUNKS
    part = xp_ref.shape[0] // ROW_PARTS
    for r in range(ROW_PARTS):
        rows = slice(r * part, (r + 1) * part)
        x = jnp.where(first, xp_ref[rows, :], xs_ref[rows, :])
        hn = _rms(x, g_ref[...]).astype(BF16)
        for c in range(IN_PROJ_COL_CHUNKS):
            cs = slice(c * width, (c + 1) * width)
            z_ref[rows, cs] = _dot(hn, wb_sc[:, cs])
        a = _dot(hn, wg_sc[...])
        la = _dot(a.astype(BF16), w2_ref[...]) + b_ref[...]
        gate_ref[rows, :] = _log_sigmoid(la) * (1.0 / GLA_TAU)


def _gate_rows_spec(w_in, wz):
    rows = w_in.shape[0] - wz
    assert wz % rows == 0 and rows % SUBLANES == 0
    return pl.BlockSpec((rows, w_in.shape[1]), lambda i: (wz // rows, 0))


def _gla_in_proj(x_p, x_s, g, w_in, wz, w2, b, cast_src):
    d = x_p.shape[1]
    n = x_p.shape[0] + x_s.shape[0]
    n_first = x_p.shape[0] // TM
    gw = b.shape[1]
    cast_in, cast_out, cast_shape = _cast_side_job(cast_src, n // TM)
    return pl.pallas_call(
        functools.partial(_gla_in_proj_kernel, n_first=n_first),
        grid=(n // TM,),
        in_specs=_split_specs(n_first, d) + [
            pl.BlockSpec((1, d), lambda i: (0, 0)),
            _resident_spec((wz, d), lambda i: (0, 0)),
            _gate_rows_spec(w_in, wz),
            pl.BlockSpec(w2.shape, lambda i: (0, 0)),
            pl.BlockSpec((1, gw), lambda i: (0, 0)),
            cast_in,
        ],
        out_specs=[
            pl.BlockSpec((TM, wz), lambda i: (i, 0)),
            pl.BlockSpec((TM, gw), lambda i: (i, 0)),
            cast_out,
        ],
        out_shape=[jax.ShapeDtypeStruct((n, wz), F32), jax.ShapeDtypeStruct((n, gw), F32), cast_shape],
        scratch_shapes=[pltpu.VMEM((d, wz), BF16), pltpu.VMEM((d, LANES), BF16)],
        compiler_params=_params(("arbitrary",)),
        name="in_proj_gla",
    )(x_p, x_s, g, w_in, w_in, w2, b, cast_src)


def _gla_kernel(q_ref, k_ref, v_ref, r_ref, la_ref, s0_ref, gn_ref, u_ref, st_ref, st_sc,
                *, chunk, tb, nblk):
    heads = GLA_HEADS
    dk = q_ref.shape[1] // heads
    dv = v_ref.shape[1] // heads
    c = pl.program_id(1)

    @pl.when(c == 0)
    def _():
        for h in range(heads):
            st_sc[h] = s0_ref[0, h].T

    ri = lax.broadcasted_iota(I32, (chunk, chunk), 0)
    ci = lax.broadcasted_iota(I32, (chunk, chunk), 1)
    causal = ri >= ci
    kscale = dk ** -0.5
    state = [st_sc[h] for h in range(heads)]
    for j in range(tb // chunk):
        rows = slice(j * chunk, (j + 1) * chunk)
        b_all = _cumsum_rows(la_ref[rows, :], chunk)
        for h in range(heads):
            ks = slice(h * dk, (h + 1) * dk)
            vs = slice(h * dv, (h + 1) * dv)
            b = b_all[:, ks]
            b_end = b[chunk - 1:chunk, :]
            q = q_ref[rows, ks]
            k = k_ref[rows, ks] * kscale
            qg = (q * jnp.exp(b)).astype(BF16)
            kg = (k * jnp.exp(-b)).astype(BF16)
            ke = (k * jnp.exp(b_end - b)).astype(BF16)
            v = v_ref[rows, vs].astype(BF16)
            a = jnp.where(causal, _dot_nt(qg, kg), 0.0).astype(BF16)
            o = _dot_nt(qg, state[h].astype(BF16)) + _dot(a, v)
            state[h] = state[h] * jnp.exp(b_end) + _dot_tn(v, ke)
            on = _rms(o, gn_ref[h:h + 1, :])
            r = r_ref[rows, vs]
            u_ref[rows, vs] = (r * _sigmoid(r) * on).astype(BF16)
    for h in range(heads):
        st_sc[h] = state[h]

    @pl.when(c == nblk - 1)
    def _():
        for h in range(heads):
            st_ref[0, h] = st_sc[h].T


def _gla_recurrence(z, log_a, s0, g_norm, nseq, t, chunk, tb):
    qk_w = log_a.shape[1]
    v_w = 2 * qk_w
    nblk = t // tb
    heads, dk, dv = s0.shape[1:]
    kern = functools.partial(_gla_kernel, chunk=chunk, tb=tb, nblk=nblk)
    row = lambda b, c: b * nblk + c
    return pl.pallas_call(
        kern,
        grid=(nseq, nblk),
        in_specs=[
            pl.BlockSpec((tb, qk_w), lambda b, c: (row(b, c), 0)),
            pl.BlockSpec((tb, qk_w), lambda b, c: (row(b, c), 1)),
            pl.BlockSpec((tb, v_w), lambda b, c: (row(b, c), 1)),
            pl.BlockSpec((tb, v_w), lambda b, c: (row(b, c), 2)),
            pl.BlockSpec((tb, qk_w), lambda b, c: (row(b, c), 0)),
            pl.BlockSpec((1, heads, dk, dv), lambda b, c: (b, 0, 0, 0)),
            pl.BlockSpec((heads, dv), lambda b, c: (0, 0)),
        ],
        out_specs=[
            pl.BlockSpec((tb, v_w), lambda b, c: (row(b, c), 0)),
            pl.BlockSpec((1, heads, dk, dv), lambda b, c: (b, 0, 0, 0)),
        ],
        out_shape=[
            jax.ShapeDtypeStruct((nseq * t, v_w), BF16),
            jax.ShapeDtypeStruct(s0.shape, F32),
        ],
        scratch_shapes=[pltpu.VMEM((heads, dv, dk), F32)],
        compiler_params=_params(("arbitrary", "arbitrary")),
        name="gla_recurrence_t%d" % t,
    )(z, z, z, z, log_a, s0, g_norm)


def _real_rows_selector(nsq, t, t_real):
    i = lax.broadcasted_iota(I32, (nsq * t_real, nsq * t), 0)
    r = lax.broadcasted_iota(I32, (nsq * t_real, nsq * t), 1)
    return jnp.where(r == (i // t_real) * t + i % t_real, 1.0, 0.0).astype(BF16)


def _gla_short_kernel(q_ref, k_ref, v_ref, r_ref, la_ref, s0_ref, gn_ref, u_ref, st_ref,
                      *, t, t_real, nsq):
    heads = GLA_HEADS
    dk = q_ref.shape[1] // heads
    dv = v_ref.shape[1] // heads
    nrow = nsq * t
    width = q_ref.shape[1]
    rc = lax.broadcasted_iota(I32, (nrow, width), 0) & (t - 1)
    log_a = jnp.where(rc < t_real, la_ref[...], 0.0)
    b = _chunk_scan(log_a, rc, t, jnp.add, 0.0)
    pick = _real_rows_selector(nsq, t, t_real)
    b_last = b.reshape(nsq, t, width)[:, t - 1:t, :]
    b_end = jnp.broadcast_to(b_last, (nsq, t, width)).reshape(nrow, width)
    k = k_ref[...] * (dk ** -0.5)
    qg = q_ref[...] * jnp.exp(b)
    kg = k * jnp.exp(-b)
    ke = k * jnp.exp(b_end - b)
    decay = jnp.exp(b_end)
    ri = lax.broadcasted_iota(I32, (nrow, nrow), 0)
    ci = lax.broadcasted_iota(I32, (nrow, nrow), 1)
    shift = t.bit_length() - 1
    mask = ((ri >> shift) == (ci >> shift)) & (ri >= ci)
    for h in range(heads):
        ks = slice(h * dk, (h + 1) * dk)
        vs = slice(h * dv, (h + 1) * dv)
        a = jnp.where(mask, _dot_nt(qg[:, ks].astype(BF16), kg[:, ks].astype(BF16)), 0.0)
        o = _dot(a.astype(BF16), v_ref[:, vs].astype(BF16))
        decay_t = decay[:, ks].T
        o_state = []
        for sq in range(nsq):
            r = slice(sq * t, (sq + 1) * t)
            s0 = s0_ref[sq, h]
            o_state.append(_dot(qg[r, ks].astype(BF16), s0.astype(BF16)))
            upd = _dot_tn(ke[r, ks].astype(BF16), v_ref[r, vs].astype(BF16))
            st_ref[sq, h] = decay_t[:, sq * t:sq * t + 1] * s0 + upd
        on = _rms(o + jnp.concatenate(o_state, axis=0), gn_ref[h:h + 1, :])
        rg = r_ref[:, vs]
        gated = (rg * _sigmoid(rg) * on).astype(BF16)
        u_ref[:, vs] = _dot(pick, gated).astype(BF16)


def _gla_recurrence_short(z, log_a, s0, g_norm, nseq, t, t_real, nsq, row0):
    qk_w = log_a.shape[1]
    v_w = 2 * qk_w
    heads, dk, dv = s0.shape[1:]
    assert nsq * t == dk == LANES and row0 % (nsq * t) == 0
    rb = nsq * t
    blk0 = row0 // rb
    return pl.pallas_call(
        functools.partial(_gla_short_kernel, t=t, t_real=t_real, nsq=nsq),
        grid=(nseq // nsq,),
        in_specs=[
            pl.BlockSpec((rb, qk_w), lambda b: (blk0 + b, 0)),
            pl.BlockSpec((rb, qk_w), lambda b: (blk0 + b, 1)),
            pl.BlockSpec((rb, v_w), lambda b: (blk0 + b, 1)),
            pl.BlockSpec((rb, v_w), lambda b: (blk0 + b, 2)),
            pl.BlockSpec((rb, qk_w), lambda b: (blk0 + b, 0)),
            pl.BlockSpec((nsq, heads, dk, dv), lambda b: (b, 0, 0, 0)),
            pl.BlockSpec((heads, dv), lambda b: (0, 0)),
        ],
        out_specs=[
            pl.BlockSpec((nsq * t_real, v_w), lambda b: (b, 0)),
            pl.BlockSpec((nsq, heads, dk, dv), lambda b: (b, 0, 0, 0)),
        ],
        out_shape=[
            jax.ShapeDtypeStruct((nseq * t_real, v_w), BF16),
            jax.ShapeDtypeStruct(s0.shape, F32),
        ],
        compiler_params=_params(("arbitrary",)),
        name="gla_recurrence_t%d" % t,
    )(z, z, z, z, log_a, s0, g_norm)


def _mlstm_short_kernel(q_ref, k_ref, v_ref, op_ref, gt_ref, e_ref, c0_ref, n0_ref, m0_ref, gn_ref,
                        u_ref, ct_ref, nt_ref, mt_ref, *, t, t_real, nsq):
    heads = MLSTM_HEADS
    pairs = heads // 2
    dv = v_ref.shape[1] // heads
    dk = q_ref.shape[1] // heads
    nrow = nsq * t

    def per_seq(x3):
        return jnp.broadcast_to(x3, (nsq, t, x3.shape[2])).reshape(nrow, x3.shape[2])

    def last(x):
        return x.reshape(nsq, t, x.shape[1])[:, t - 1:t, :]

    rc = lax.broadcasted_iota(I32, (nrow, LANES), 0) & (t - 1)
    lane_g = lax.broadcasted_iota(I32, (nrow, LANES), 1)
    gts = jnp.where(rc < t_real, gt_ref[...],
                    jnp.where(lane_g < heads, NEG_BIG, 0.0))
    pick = _real_rows_selector(nsq, t, t_real)
    f_cum = pltpu.roll(_chunk_scan(gts, rc, t, jnp.add, 0.0), LANES - heads, 1)
    b = gts - f_cum
    cmb = _chunk_scan(b, rc, t, jnp.maximum, NEG_BIG)
    mp = per_seq(m0_ref[...])
    big_m = jnp.maximum(mp, cmb)
    m_t = f_cum + big_m
    m_new = last(m_t)
    mt_ref[...] = m_new
    mn = per_seq(m_new)
    fe = per_seq(last(f_cum))
    w_i = jnp.exp(mp - big_m)
    em = jnp.exp(-m_t)
    k_sc = jnp.exp(fe - f_cum + gts - mn)
    w_c = jnp.exp(fe + mp - mn)
    ex = _dot(_pack_terms([w_i, k_sc, -big_m, em, w_c], heads, SHORT_EXPAND_TERMS), e_ref[...])
    offs = np.cumsum([0] + [heads * w for w in SHORT_EXPAND_WIDTHS])
    q = q_ref[...]
    k = k_ref[...] * (dk ** -0.5)
    qs = q * ex[:, offs[0]:offs[1]]
    ke = k * ex[:, offs[1]:offs[2]]
    neg_m = ex[:, offs[2]:offs[3]]
    em_v = ex[:, offs[3]:offs[4]]
    wc_k = ex[:, offs[4]:offs[5]]
    bt = b.T

    ri = lax.broadcasted_iota(I32, (nrow, nrow), 0)
    ci = lax.broadcasted_iota(I32, (nrow, nrow), 1)
    shift = t.bit_length() - 1
    mask = ((ri >> shift) == (ci >> shift)) & (ri >= ci)
    lane1 = lax.broadcasted_iota(I32, (1, LANES), 1)
    rowi = lax.broadcasted_iota(I32, (LANES, LANES), 0)
    ones_v = jnp.ones((nrow, dv), BF16)
    ones_k = jnp.ones((LANES, dv), BF16)
    for p in range(pairs):
        ps = slice(p * LANES, (p + 1) * LANES)
        qp = q[:, ps]
        qsp = qs[:, ps]
        kpb = k[:, ps].astype(BF16)
        n_rows = per_seq(n0_ref[:, p:p + 1, :])
        intra, den_state, qsh = [], [], []
        for jj in range(2):
            h = 2 * p + jj
            hs = slice(h * dv, (h + 1) * dv)
            mine = (lane1 >= jj * dk) & (lane1 < (jj + 1) * dk)
            qh = jnp.where(mine, qp, 0.0).astype(BF16)
            qsh.append(jnp.where(mine, qsp, 0.0))
            d = neg_m[:, hs] + bt[h:h + 1, :]
            s = _dot_nt(qh, kpb) * jnp.where(mask, jnp.exp(d), 0.0)
            vh = jnp.concatenate([v_ref[:, hs].astype(BF16), ones_v], axis=1)
            intra.append(_dot(s.astype(BF16), vh))
            den_state.append(_dot((qsh[jj] * n_rows).astype(BF16), ones_k))
        num_state = [[], []]
        for sq in range(nsq):
            r = slice(sq * t, (sq + 1) * t)
            lhs = jnp.concatenate([qsh[0][r], qsh[1][r]], axis=0).astype(BF16)
            c_pair = jnp.concatenate([c0_ref[sq, 2 * p], c0_ref[sq, 2 * p + 1]], axis=0)
            res = _dot(lhs, c_pair.astype(BF16))
            num_state[0].append(res[:t])
            num_state[1].append(res[t:])
        for jj in range(2):
            h = 2 * p + jj
            hs = slice(h * dv, (h + 1) * dv)
            num = intra[jj][:, :dv] + jnp.concatenate(num_state[jj], axis=0)
            den = intra[jj][:, dv:] + den_state[jj]
            hh = num / jnp.maximum(jnp.abs(den), em_v[:, hs])
            hn = _rms(hh, gn_ref[h:h + 1, :])
            gated = (_sigmoid(op_ref[:, hs]) * hn).astype(BF16)
            u_ref[:, hs] = _dot(pick, gated).astype(BF16)
        kep = ke[:, ps]
        for sq in range(nsq):
            r = slice(sq * t, (sq + 1) * t)
            vp = v_ref[r, 2 * p * dv:(2 * p + 2) * dv].astype(BF16)
            full = _dot_tn(kep[r].astype(BF16), vp)
            upd = jnp.where(rowi < dk, full[:, :dv], full[:, dv:])
            w_row = w_c[sq * t:sq * t + 1, :]
            w_col = jnp.where(rowi < dk, w_row[:, 2 * p:2 * p + 1], w_row[:, 2 * p + 1:2 * p + 2])
            c_pair = jnp.concatenate([c0_ref[sq, 2 * p], c0_ref[sq, 2 * p + 1]], axis=0)
            c_new = w_col * c_pair + upd
            ct_ref[sq, 2 * p] = c_new[:dk]
            ct_ref[sq, 2 * p + 1] = c_new[dk:]
        k_sum = jnp.sum(kep.reshape(nsq, t, LANES), axis=1, keepdims=True)
        nt_ref[:, p:p + 1, :] = last(wc_k[:, ps]) * n0_ref[:, p:p + 1, :] + k_sum


def _mlstm_recurrence_short(z, gates, c0, n0, m0, g_norm, nseq, t, t_real, nsq, row0):
    heads = g_norm.shape[0]
    v_w = heads * g_norm.shape[1]
    qk_w = v_w // 2
    assert nsq * t == LANES and row0 % (nsq * t) == 0
    expand = _expand_matrix(heads, SHORT_EXPAND_WIDTHS, SHORT_EXPAND_TERMS)
    kern = functools.partial(_mlstm_short_kernel, t=t, t_real=t_real, nsq=nsq)
    rb = nsq * t
    blk0 = row0 // rb
    st4 = lambda b: (b, 0, 0, 0)
    st3 = lambda b: (b, 0, 0)
    return pl.pallas_call(
        kern,
        grid=(nseq // nsq,),
        in_specs=[
            pl.BlockSpec((rb, qk_w), lambda b: (blk0 + b, 0)),
            pl.BlockSpec((rb, qk_w), lambda b: (blk0 + b, 1)),
            pl.BlockSpec((rb, v_w), lambda b: (blk0 + b, 1)),
            pl.BlockSpec((rb, v_w), lambda b: (blk0 + b, 2)),
            pl.BlockSpec((rb, LANES), lambda b: (blk0 + b, 0)),
            pl.BlockSpec(expand.shape, lambda b: (0, 0)),
            pl.BlockSpec((nsq,) + c0.shape[1:], st4),
            pl.BlockSpec((nsq,) + n0.shape[1:], st3),
            pl.BlockSpec((nsq,) + m0.shape[1:], st3),
            pl.BlockSpec(g_norm.shape, lambda b: (0, 0)),
        ],
        out_specs=[
            pl.BlockSpec((nsq * t_real, v_w), lambda b: (b, 0)),
            pl.BlockSpec((nsq,) + c0.shape[1:], st4),
            pl.BlockSpec((nsq,) + n0.shape[1:], st3),
            pl.BlockSpec((nsq,) + m0.shape[1:], st3),
        ],
        out_shape=[
            jax.ShapeDtypeStruct((nseq * t_real, v_w), BF16),
            jax.ShapeDtypeStruct(c0.shape, F32),
            jax.ShapeDtypeStruct(n0.shape, F32),
            jax.ShapeDtypeStruct(m0.shape, F32),
        ],
        compiler_params=_params(("arbitrary",)),
        name="mlstm_recurrence_t%d" % t,
    )(z, z, z, z, gates, expand, c0, n0, m0, g_norm)


EXPAND_WIDTHS = (64, 64, 128, 128)
SHORT_EXPAND_WIDTHS = EXPAND_WIDTHS + (64,)
EXPAND_TERMS = (2, 2, 3, 2)
SHORT_EXPAND_TERMS = EXPAND_TERMS + (2,)


def _expand_matrix(heads, widths, terms):
    assert sum(terms) * heads <= LANES
    cols = []
    slot = 0
    for width, n_terms in zip(widths, terms):
        sel = np.zeros((LANES, heads * width), np.float32)
        for _ in range(n_terms):
            for h in range(heads):
                sel[slot * heads + h, h * width:(h + 1) * width] = 1.0
            slot += 1
        cols.append(sel)
    return jnp.asarray(np.concatenate(cols, axis=1), BF16)


def _chunk_scan(x, rc, n, op, fill):
    s = 1
    while s < n:
        x = op(x, jnp.where(rc >= s, pltpu.roll(x, s, 0), fill))
        s *= 2
    return x


def _pack_terms(values, heads, terms):
    lane = lax.broadcasted_iota(I32, values[0].shape, 1)
    packed = jnp.zeros(values[0].shape, F32)
    slot = 0
    for val, n_terms in zip(values, terms):
        rest = val
        for term in range(n_terms):
            part = rest.astype(BF16).astype(F32)
            rest = rest - part
            moved = part if slot == 0 else pltpu.roll(part, slot * heads, 1)
            packed = jnp.where((lane >= slot * heads) & (lane < (slot + 1) * heads), moved, packed)
            slot += 1
    return packed.astype(BF16)


def _mlstm_block_kernel(q_ref, k_ref, v_ref, op_ref, gt_ref, e_ref, c0_ref, n0_ref, m0_ref, gn_ref,
                        u_ref, ct_ref, nt_ref, mt_ref, c_sc, nb_sc, m_sc, *, chunk, tb, nblk):
    heads = MLSTM_HEADS
    pairs = heads // 2
    dv = v_ref.shape[1] // heads
    dk = q_ref.shape[1] // heads
    c = pl.program_id(1)

    @pl.when(c == 0)
    def _():
        c_sc[...] = c0_ref[0]
        for p in range(pairs):
            nb_sc[p] = jnp.broadcast_to(n0_ref[0, p:p + 1, :], (LANES, LANES)).T
        m_sc[...] = m0_ref[0]

    nch = tb // chunk
    rc = lax.broadcasted_iota(I32, (tb, LANES), 0) & (chunk - 1)
    gts = gt_ref[...]
    f_cum = pltpu.roll(_chunk_scan(gts, rc, chunk, jnp.add, 0.0), LANES - heads, 1)
    b = gts - f_cum
    cmb = _chunk_scan(b, rc, chunk, jnp.maximum, NEG_BIG)
    m_prev = m_sc[...]
    mp_rows, mn_rows, fe_rows, w_c = [], [], [], []
    for j in range(nch):
        last = (j + 1) * chunk - 1
        f_end = f_cum[last:last + 1, :]
        m_new = f_end + jnp.maximum(m_prev, cmb[last:last + 1, :])
        w_c.append(jnp.exp(f_end + m_prev - m_new))
        mp_rows.append(jnp.broadcast_to(m_prev, (chunk, LANES)))
        mn_rows.append(jnp.broadcast_to(m_new, (chunk, LANES)))
        fe_rows.append(jnp.broadcast_to(f_end, (chunk, LANES)))
        m_prev = m_new
    m_sc[...] = m_prev
    mp = jnp.concatenate(mp_rows, axis=0)
    mn = jnp.concatenate(mn_rows, axis=0)
    fe = jnp.concatenate(fe_rows, axis=0)
    big_m = jnp.maximum(mp, cmb)
    w_i = jnp.exp(mp - big_m)
    em = jnp.exp(-(f_cum + big_m))
    k_sc = jnp.exp(fe - f_cum + gts - mn)
    ex = _dot(_pack_terms([w_i, k_sc, -big_m, em], heads, EXPAND_TERMS), e_ref[...])
    o1 = heads * EXPAND_WIDTHS[0]
    o2 = o1 + heads * EXPAND_WIDTHS[1]
    o3 = o2 + heads * EXPAND_WIDTHS[2]
    q = q_ref[...]
    k = k_ref[...] * (dk ** -0.5)
    qs = q * ex[:, :o1]
    ke = k * ex[:, o1:o2]
    neg_m = ex[:, o2:o3]
    em_v = ex[:, o3:]
    bt = b.T

    ri = lax.broadcasted_iota(I32, (chunk, chunk), 0)
    ci = lax.broadcasted_iota(I32, (chunk, chunk), 1)
    causal = ri >= ci
    lane1 = lax.broadcasted_iota(I32, (1, LANES), 1)
    rowi = lax.broadcasted_iota(I32, (LANES, LANES), 0)
    ones_v = jnp.ones((chunk, dv), BF16)
    states = [[None] * pairs for _ in range(nch)]
    for p in range(pairs):
        ps = slice(p * LANES, (p + 1) * LANES)
        cp = c_sc[p]
        nb = nb_sc[p]
        for j in range(nch):
            rows = slice(j * chunk, (j + 1) * chunk)
            states[j][p] = jnp.concatenate([cp, nb], axis=1).astype(BF16)
            vp = jnp.concatenate([v_ref[rows, 2 * p * dv:(2 * p + 2) * dv].astype(BF16), ones_v], axis=1)
            full = _dot_tn(ke[rows, ps].astype(BF16), vp)
            upd = jnp.where(rowi < dk, full[:, :dv], full[:, dv:2 * dv])
            w_col = jnp.where(rowi < dk, w_c[j][:, 2 * p:2 * p + 1], w_c[j][:, 2 * p + 1:2 * p + 2])
            cp = w_col * cp + upd
            nb = w_col * nb + full[:, 2 * dv:]
        c_sc[p] = cp
        nb_sc[p] = nb
    for j in range(nch):
        rows = slice(j * chunk, (j + 1) * chunk)
        for p in range(pairs):
            ps = slice(p * LANES, (p + 1) * LANES)
            qp = q[rows, ps]
            qsp = qs[rows, ps]
            kpb = k[rows, ps].astype(BF16)
            state = states[j][p]
            for jj in range(2):
                h = 2 * p + jj
                hs = slice(h * dv, (h + 1) * dv)
                mine = (lane1 >= jj * dk) & (lane1 < (jj + 1) * dk)
                qh = jnp.where(mine, qp, 0.0).astype(BF16)
                qsh = jnp.where(mine, qsp, 0.0).astype(BF16)
                d = neg_m[rows, h * dv:h * dv + chunk] + bt[h:h + 1, j * chunk:(j + 1) * chunk]
                s = _dot_nt(qh, kpb) * jnp.where(causal, jnp.exp(d), 0.0)
                vh = jnp.concatenate([v_ref[rows, hs].astype(BF16), ones_v], axis=1)
                out = _dot(qsh, state) + _dot(s.astype(BF16), vh)
                hh = out[:, :dv] / jnp.maximum(jnp.abs(out[:, dv:]), em_v[rows, hs])
                hn = _rms(hh, gn_ref[h:h + 1, :])
                u_ref[rows, hs] = (_sigmoid(op_ref[rows, hs]) * hn).astype(BF16)

    @pl.when(c == nblk - 1)
    def _():
        ct_ref[0] = c_sc[...]
        for p in range(pairs):
            nt_ref[0, p:p + 1, :] = nb_sc[p].T[0:1, :]
        mt_ref[0] = m_sc[...]


def _mlstm_recurrence_blocked(z, gates, c0, n0, m0, g_norm, nseq, t, chunk, tb):
    heads = g_norm.shape[0]
    v_w = heads * g_norm.shape[1]
    qk_w = v_w // 2
    nblk = t // tb
    expand = _expand_matrix(heads, EXPAND_WIDTHS, EXPAND_TERMS)
    kern = functools.partial(_mlstm_block_kernel, chunk=chunk, tb=tb, nblk=nblk)
    row = lambda b, c: b * nblk + c
    st4 = lambda b, c: (b, 0, 0, 0)
    st3 = lambda b, c: (b, 0, 0)
    return pl.pallas_call(
        kern,
        grid=(nseq, nblk),
        in_specs=[
            pl.BlockSpec((tb, qk_w), lambda b, c: (row(b, c), 0)),
            pl.BlockSpec((tb, qk_w), lambda b, c: (row(b, c), 1)),
            pl.BlockSpec((tb, v_w), lambda b, c: (row(b, c), 1)),
            pl.BlockSpec((tb, v_w), lambda b, c: (row(b, c), 2)),
            pl.BlockSpec((tb, LANES), lambda b, c: (row(b, c), 0)),
            pl.BlockSpec(expand.shape, lambda b, c: (0, 0)),
            pl.BlockSpec((1,) + c0.shape[1:], st4),
            pl.BlockSpec((1,) + n0.shape[1:], st3),
            pl.BlockSpec((1,) + m0.shape[1:], st3),
            pl.BlockSpec(g_norm.shape, lambda b, c: (0, 0)),
        ],
        out_specs=[
            pl.BlockSpec((tb, v_w), lambda b, c: (row(b, c), 0)),
            pl.BlockSpec((1,) + c0.shape[1:], st4),
            pl.BlockSpec((1,) + n0.shape[1:], st3),
            pl.BlockSpec((1,) + m0.shape[1:], st3),
        ],
        out_shape=[
            jax.ShapeDtypeStruct((nseq * t, v_w), BF16),
            jax.ShapeDtypeStruct(c0.shape, F32),
            jax.ShapeDtypeStruct(n0.shape, F32),
            jax.ShapeDtypeStruct(m0.shape, F32),
        ],
        scratch_shapes=[
            pltpu.VMEM(c0.shape[1:], F32),
            pltpu.VMEM(c0.shape[1:], F32),
            pltpu.VMEM(m0.shape[1:], F32),
        ],
        compiler_params=_params(("arbitrary", "arbitrary")),
        name="mlstm_recurrence_t%d" % t,
    )(z, z, z, z, gates, expand, c0, n0, m0, g_norm)


def _out_proj_kernel(up_ref, us_ref, xp_ref, xs_ref, w_ref, g_ref, cast_ref,
                     xo_ref, hn_ref, cast_out_ref, wb_sc, *, n_first):
    i = pl.program_id(0)

    @pl.when(i == 0)
    def _():
        wb_sc[...] = w_ref[...].astype(BF16)

    cast_out_ref[...] = cast_ref[...].astype(BF16)
    u = _pick_rows(i, n_first, up_ref, us_ref)
    x = _pick_rows(i, n_first, xp_ref, xs_ref) + _dot(u, wb_sc[...])
    xo_ref[...] = x
    hn_ref[...] = _rms(x, g_ref[...]).astype(hn_ref.dtype)


def _out_proj(u_p, u_s, x_p, x_s, w, g, cast_src):
    d = x_p.shape[1]
    n = x_p.shape[0] + x_s.shape[0]
    n_first = x_p.shape[0] // TM
    cast_in, cast_out, cast_shape = _cast_side_job(cast_src, n // TM)
    return pl.pallas_call(
        functools.partial(_out_proj_kernel, n_first=n_first),
        grid=(n // TM,),
        in_specs=_split_specs(n_first, u_p.shape[1]) + _split_specs(n_first, d) + [
            _resident_spec(w.shape, lambda i: (0, 0)),
            pl.BlockSpec((1, d), lambda i: (0, 0)),
            cast_in,
        ],
        out_specs=[pl.BlockSpec((TM, d), lambda i: (i, 0)), pl.BlockSpec((TM, d), lambda i: (i, 0)),
                   cast_out],
        out_shape=[jax.ShapeDtypeStruct((n, d), F32), jax.ShapeDtypeStruct((n, d), BF16), cast_shape],
        scratch_shapes=[pltpu.VMEM(w.shape, BF16)],
        compiler_params=_params(("arbitrary",)),
        name="out_proj",
    )(u_p, u_s, x_p, x_s, w, g, cast_src)


def _out_proj_router_kernel(up_ref, us_ref, x_ref, w_ref, g_ref, wr_ref, br_ref,
                            xo_ref, hn_ref, gate_ref, idx_ref, cnt_ref, logit_sc, wb_sc,
                            *, n_first, n_tiles):
    i = pl.program_id(0)

    @pl.when(i == 0)
    def _():
        logit_sc[...] = jnp.zeros_like(logit_sc)
        wb_sc[...] = w_ref[...].astype(BF16)

    logits = logit_sc[(i + 1) & 1]
    u = _pick_rows(jnp.minimum(i, n_tiles - 1), n_first, up_ref, us_ref)
    x = x_ref[...] + _dot(u, wb_sc[...])
    xo_ref[...] = x
    hn = _rms(x, g_ref[...])
    hn_hi = hn.astype(BF16)
    hn_ref[...] = hn_hi
    hn_lo = (hn - hn_hi.astype(F32)).astype(BF16)
    logit_sc[i & 1] = (_dot(hn_hi, wr_ref[0]) + _dot(hn_lo, wr_ref[0]) + _dot(hn_hi, wr_ref[1])
                       + br_ref[...])
    tm = logits.shape[0]
    lane = lax.broadcasted_iota(I32, logits.shape, 1)
    valid = lane < N_EXPERTS
    logits = jnp.where(valid, logits, -jnp.inf)
    ex = jnp.exp(logits - jnp.max(logits, axis=-1, keepdims=True))
    probs = jnp.where(valid, ex / jnp.sum(ex, axis=-1, keepdims=True), -1.0)
    p1 = jnp.max(probs, axis=-1, keepdims=True)
    i1 = jnp.min(jnp.where(probs == p1, lane, LANES), axis=-1, keepdims=True)
    rest = jnp.where(lane == i1, -1.0, probs)
    p2 = jnp.max(rest, axis=-1, keepdims=True)
    i2 = jnp.min(jnp.where(rest == p2, lane, LANES), axis=-1, keepdims=True)
    tot = p1 + p2
    onehot = ((lane == i1) | (lane == i2)).astype(BF16)
    rr = lax.broadcasted_iota(I32, (tm, tm), 0)
    cc = lax.broadcasted_iota(I32, (tm, tm), 1)
    strict = (rr > cc).astype(BF16)
    before = _dot(strict, onehot)
    r1 = jnp.sum(jnp.where(lane == i1, before, 0.0), axis=-1, keepdims=True)
    r2 = jnp.sum(jnp.where(lane == i2, before, 0.0), axis=-1, keepdims=True)
    gate_ref[...] = jnp.where(lane == 0, p1 / tot, jnp.where(lane == 1, p2 / tot, 0.0))
    idx_ref[...] = jnp.where(lane == 0, i1,
                             jnp.where(lane == 1, i2,
                                       jnp.where(lane == 2, r1.astype(I32),
                                                 jnp.where(lane == 3, r2.astype(I32), 0))))
    cnt_ref[0] = jnp.sum(onehot.astype(F32), axis=0, keepdims=True)


def _out_proj_router(u_p, u_s, x, w, g, w_router, b_router):
    n, d = x.shape
    n_first = u_p.shape[0] // TM
    n_tiles = n // TM
    row = lambda i: (jnp.minimum(i, n_tiles - 1), 0)
    routed = lambda i: (jnp.maximum(i - 1, 0), 0)
    fix = lambda i: (0, 0)
    u_specs = [pl.BlockSpec((TM, u_p.shape[1]), lambda i: (jnp.minimum(i, n_first - 1), 0)),
               pl.BlockSpec((TM, u_p.shape[1]),
                            lambda i: (jnp.clip(i - n_first, 0, n_tiles - n_first - 1), 0))]
    return pl.pallas_call(
        functools.partial(_out_proj_router_kernel, n_first=n_first, n_tiles=n_tiles),
        grid=(n_tiles + 1,),
        in_specs=u_specs + [
            pl.BlockSpec((TM, d), row),
            _resident_spec(w.shape, fix),
            pl.BlockSpec((1, d), fix),
            pl.BlockSpec(w_router.shape, lambda i: (0, 0, 0)),
            pl.BlockSpec((1, LANES), fix),
        ],
        out_specs=[
            pl.BlockSpec((TM, d), row),
            pl.BlockSpec((TM, d), row),
            pl.BlockSpec((TM, LANES), routed),
            pl.BlockSpec((TM, LANES), routed),
            pl.BlockSpec((1, 1, LANES), lambda i: (jnp.maximum(i - 1, 0), 0, 0)),
        ],
        out_shape=[
            jax.ShapeDtypeStruct((n, d), F32),
            jax.ShapeDtypeStruct((n, d), BF16),
            jax.ShapeDtypeStruct((n, LANES), F32),
            jax.ShapeDtypeStruct((n, LANES), I32),
            jax.ShapeDtypeStruct((n_tiles, 1, LANES), F32),
        ],
        scratch_shapes=[pltpu.VMEM((2, TM, LANES), F32), pltpu.VMEM(w.shape, BF16)],
        compiler_params=_params(("arbitrary",)),
        name="out_proj_router",
    )(u_p, u_s, x, w, g, w_router, b_router)


def _swiglu_kernel(hn_ref, x_ref, wg_ref, wu_ref, wd_ref, g_ref, cast_ref, xo_ref, ho_ref, cast_out_ref,
                   *, nchunk):
    cast_out_ref[...] = cast_ref[...].astype(BF16)
    hn = hn_ref[...]
    width = wd_ref.shape[0] // nchunk
    x = x_ref[...]
    for c in range(nchunk):
        cs = slice(c * width, (c + 1) * width)
        gate = _dot(hn, wg_ref[:, cs])
        up = _dot(hn, wu_ref[:, cs])
        act = (gate * _sigmoid(gate) * up).astype(BF16)
        x = x + _dot(act, wd_ref[cs, :])
    xo_ref[...] = x
    ho_ref[...] = _rms(x, g_ref[...]).astype(BF16)


def _swiglu(hn, x, w_gu, w_down, g, nchunk, cast_src):
    n, d = x.shape
    dff = w_down.shape[0]
    kern = functools.partial(_swiglu_kernel, nchunk=nchunk)
    row = lambda i: (i, 0)
    cast_in, cast_out, cast_shape = _cast_side_job(cast_src, n // TM)
    return pl.pallas_call(
        kern,
        grid=(n // TM,),
        in_specs=[
            pl.BlockSpec((TM, d), row),
            pl.BlockSpec((TM, d), row),
            _resident_spec((d, dff), lambda i: (0, 0)),
            _resident_spec((d, dff), lambda i: (0, 1)),
            _resident_spec((dff, d), lambda i: (0, 0)),
            pl.BlockSpec((1, d), lambda i: (0, 0)),
            cast_in,
        ],
        out_specs=[pl.BlockSpec((TM, d), row), pl.BlockSpec((TM, d), row), cast_out],
        out_shape=[jax.ShapeDtypeStruct((n, d), F32), jax.ShapeDtypeStruct((n, d), BF16), cast_shape],
        compiler_params=_params(("arbitrary",)),
        name="dense_swiglu",
    )(hn, x, w_gu, w_gu, w_down, g, cast_src)


def _row_copy(src_ref, src_row, dst_ref, dst_row, sem):
    return pltpu.make_async_copy(src_ref.at[pl.ds(src_row, 1), :], dst_ref.at[pl.ds(dst_row, 1), :], sem)


SUBLANES = 8


def _zero_rows(zero_ref, dst_ref, dst_row, count, limit, sem, wait):
    head = jnp.minimum((-dst_row) & (SUBLANES - 1), count)
    for j in range(SUBLANES - 1):
        @pl.when(j < head)
        def _(j=j):
            copy = _row_copy(zero_ref, 0, dst_ref, dst_row + j, sem)
            copy.wait() if wait else copy.start()

    rest = count - head
    first = dst_row + head
    for bit in range(SUBLANES.bit_length() - 1, limit.bit_length() - 1):
        size = 1 << bit
        done = (rest >> (bit + 1)) << (bit + 1)

        @pl.when(((rest >> bit) & 1) == 1)
        def _(size=size, done=done):
            start = pl.multiple_of(first + done, SUBLANES)
            copy = pltpu.make_async_copy(zero_ref.at[pl.ds(0, size), :],
                                         dst_ref.at[pl.ds(start, size), :], sem)
            copy.wait() if wait else copy.start()


def _expert_kernel(te_ref, tv_ref, xs_ref, wg_ref, wu_ref, wd_ref, ys_ref, *, nchunk):
    i = pl.program_id(0)
    half = xs_ref.shape[0] // 2
    width = wg_ref.shape[2] // nchunk

    def swiglu_rows(rows):
        x = xs_ref[rows, :].astype(BF16)
        acc = None
        for c in range(nchunk):
            cs = slice(c * width, (c + 1) * width)
            gate = _dot(x, wg_ref[0, :, cs])
            up = _dot(x, wu_ref[0, :, cs])
            act = (gate * _sigmoid(gate) * up).astype(BF16)
            part = _dot(act, wd_ref[0, cs, :])
            acc = part if acc is None else acc + part
        ys_ref[rows, :] = acc.astype(BF16).astype(F32)

    @pl.when(tv_ref[i] == 2)
    def _():
        swiglu_rows(slice(None))

    @pl.when(tv_ref[i] == 1)
    def _():
        swiglu_rows(slice(half, 2 * half))
        ys_ref[:half, :] = jnp.zeros((half, ys_ref.shape[1]), F32)

    @pl.when(tv_ref[i] == 0)
    def _():
        ys_ref[...] = jnp.zeros_like(ys_ref)


def _experts(tile_expert, tile_valid, xs, w_gu, w_down):
    r, d = xs.shape
    n_tiles = r // TM_EXPERT
    dff = w_down.shape[1]
    kern = functools.partial(_expert_kernel, nchunk=EXPERT_CHUNKS)
    return pl.pallas_call(
        kern,
        grid_spec=pltpu.PrefetchScalarGridSpec(
            num_scalar_prefetch=2,
            grid=(r // TM_EXPERT,),
            in_specs=[
                pl.BlockSpec((TM_EXPERT, d), lambda i, te, tv: (jnp.minimum(i, tv[n_tiles]), 0)),
                pl.BlockSpec((1, d, dff), lambda i, te, tv: (te[i], 0, 0)),
                pl.BlockSpec((1, d, dff), lambda i, te, tv: (te[i], 0, 1)),
                pl.BlockSpec((1, dff, d), lambda i, te, tv: (te[i], 0, 0)),
            ],
            out_specs=pl.BlockSpec((TM_EXPERT, d), lambda i, te, tv: (i, 0)),
        ),
        out_shape=jax.ShapeDtypeStruct((r, d), F32),
        compiler_params=_params(("arbitrary",)),
        name="moe_experts",
    )(tile_expert, tile_valid, xs, w_gu, w_gu, w_down)


RUN_BUF_ROWS = 2 * TM + LANES


def _run_positions(idx, off_row):
    lane = lax.broadcasted_iota(I32, idx.shape, 1)
    pos = []
    for slot in range(2):
        off = jnp.sum(jnp.where(lane == idx[:, slot:slot + 1], off_row, 0.0), axis=-1, keepdims=True)
        pos.append(off.astype(I32) + idx[:, 2 + slot:3 + slot])
    return pos


def _run_copies(tab_ref, step, hbm_ref, buf_ref, slot, sem, to_hbm, wait):
    for e in range(N_EXPERTS):
        entry = (step * N_EXPERTS + e) * 3
        off, row, length = tab_ref[entry], tab_ref[entry + 1], tab_ref[entry + 2]
        for bit in range(SUBLANES.bit_length() - 1, TM.bit_length()):
            size = 1 << bit
            done = (length >> (bit + 1)) << (bit + 1)

            @pl.when(((length >> bit) & 1) == 1)
            def _(size=size, done=done, off=off, row=row):
                in_buf = buf_ref.at[slot, pl.ds(pl.multiple_of(off + done, SUBLANES), size), :]
                in_hbm = hbm_ref.at[pl.ds(pl.multiple_of(row + done, SUBLANES), size), :]
                copy = (pltpu.make_async_copy(in_buf, in_hbm, sem.at[slot]) if to_hbm
                        else pltpu.make_async_copy(in_hbm, in_buf, sem.at[slot]))
                copy.wait() if wait else copy.start()


def _run_dispatch_kernel(tab_ref, pad_ref, hn_ref, idx_ref, off_ref, xs_ref, buf, zero_sc, sem, zsem,
                         *, n_steps):
    j = pl.program_id(0)
    slot = j & 1

    @pl.when(j == 0)
    def _():
        zero_sc[...] = jnp.zeros_like(zero_sc)
        rows = zero_sc.shape[0]
        for wait in (False, True):
            for e in range(N_EXPERTS):
                _zero_rows(zero_sc, xs_ref, pad_ref[2 * e], pad_ref[2 * e + 1], 2 * rows, zsem, wait)

            def tail(t, carry, wait=wait):
                first = pl.multiple_of(pad_ref[2 * N_EXPERTS] + t * rows, SUBLANES)
                copy = pltpu.make_async_copy(zero_sc, xs_ref.at[pl.ds(first, rows), :], zsem)
                copy.wait() if wait else copy.start()
                return carry

            lax.fori_loop(0, pad_ref[2 * N_EXPERTS + 1] // rows, tail, 0)

    @pl.when(j >= 2)
    def _():
        _run_copies(tab_ref, j - 2, xs_ref, buf, slot, sem, True, True)

    pos0, pos1 = _run_positions(idx_ref[...], off_ref[0])
    lane = lax.broadcasted_iota(I32, (hn_ref.shape[0], buf.shape[1]), 1)
    sel = jnp.where((lane == pos0) | (lane == pos1), 1.0, 0.0).astype(BF16)
    buf[slot] = _dot_tn(sel, hn_ref[...])
    _run_copies(tab_ref, j, xs_ref, buf, slot, sem, True, False)

    @pl.when(j == n_steps - 1)
    def _():
        _run_copies(tab_ref, j - 1, xs_ref, buf, 1 - slot, sem, True, True)
        _run_copies(tab_ref, j, xs_ref, buf, slot, sem, True, True)


def _run_dispatch(run_table, pad_table, hn, idx, run_offsets, n_rows):
    n, d = hn.shape
    n_steps = n // TM
    assert n_steps >= 2
    return pl.pallas_call(
        functools.partial(_run_dispatch_kernel, n_steps=n_steps),
        grid_spec=pltpu.PrefetchScalarGridSpec(
            num_scalar_prefetch=2,
            grid=(n_steps,),
            in_specs=[
                pl.BlockSpec((TM, d), lambda i, *_: (i, 0)),
                pl.BlockSpec((TM, LANES), lambda i, *_: (i, 0)),
                pl.BlockSpec((1, 1, LANES), lambda i, *_: (i, 0, 0)),
            ],
            out_specs=pl.BlockSpec(memory_space=pl.ANY),
            scratch_shapes=[pltpu.VMEM((2, RUN_BUF_ROWS, d), F32),
                            pltpu.VMEM((TM_EXPERT // 2, d), F32),
                            pltpu.SemaphoreType.DMA((2,)), pltpu.SemaphoreType.DMA(())],
        ),
        out_shape=jax.ShapeDtypeStruct((n_rows, d), F32),
        compiler_params=_params(("arbitrary",)),
        name="moe_dispatch",
    )(run_table, pad_table, hn, idx, run_offsets)


def _run_combine_kernel(tab_ref, ys_ref, x_ref, gate_ref, idx_ref, off_ref, g_ref, yp_ref, ysm_ref,
                        buf, sem, *, n_first, n_steps):
    j = pl.program_id(0)
    slot = j & 1

    @pl.when(j == 0)
    def _():
        buf[...] = jnp.zeros_like(buf)
        _run_copies(tab_ref, 0, ys_ref, buf, 0, sem, False, False)

    @pl.when(j + 1 < n_steps)
    def _():
        _run_copies(tab_ref, j + 1, ys_ref, buf, 1 - slot, sem, False, False)

    pos = _run_positions(idx_ref[...], off_ref[0])
    lane = lax.broadcasted_iota(I32, (x_ref.shape[0], buf.shape[1]), 1)
    sel = [jnp.where(lane == pos[k], 1.0, 0.0).astype(BF16) for k in range(2)]
    _run_copies(tab_ref, j, ys_ref, buf, slot, sem, False, True)
    runs = buf[slot].astype(BF16)
    gate = gate_ref[...]
    y = x_ref[...]
    for k in range(2):
        y = y + gate[:, k:k + 1] * _dot(sel[k], runs)
    y = _rms(y, g_ref[...])

    @pl.when(j < n_first)
    def _():
        yp_ref[...] = y

    @pl.when(j >= n_first)
    def _():
        ysm_ref[...] = y


def _run_combine(run_table, ys, x, gates, idx, run_offsets, g, n_p):
    n, d = x.shape
    n_first = n_p // TM
    n_steps = n // TM
    p_spec, s_spec = _split_specs(n_first, d)
    row = lambda i, *_: (i, 0)
    return pl.pallas_call(
        functools.partial(_run_combine_kernel, n_first=n_first, n_steps=n_steps),
        grid_spec=pltpu.PrefetchScalarGridSpec(
            num_scalar_prefetch=1,
            grid=(n_steps,),
            in_specs=[
                pl.BlockSpec(memory_space=pl.ANY),
                pl.BlockSpec((TM, d), row),
                pl.BlockSpec((TM, LANES), row),
                pl.BlockSpec((TM, LANES), row),
                pl.BlockSpec((1, 1, LANES), lambda i, *_: (i, 0, 0)),
                pl.BlockSpec((1, d), lambda i, *_: (0, 0)),
            ],
            out_specs=[p_spec, s_spec],
            scratch_shapes=[pltpu.VMEM((2, RUN_BUF_ROWS, d), F32), pltpu.SemaphoreType.DMA((2,))],
        ),
        out_shape=[jax.ShapeDtypeStruct((n_p, d), F32), jax.ShapeDtypeStruct((n - n_p, d), F32)],
        compiler_params=_params(("arbitrary",)),
        name="moe_combine",
    )(run_table, ys, x, gates, idx, run_offsets, g)


def _run_tables(tile_counts, n_rows):
    cnt = tile_counts[:, 0, :N_EXPERTS].astype(I32)
    run = ((cnt + SUBLANES - 1) // SUBLANES) * SUBLANES
    off = jnp.cumsum(run, axis=1) - run
    used = jnp.sum(run, axis=0)
    padded = ((used + TM_EXPERT - 1) // TM_EXPERT) * TM_EXPERT
    ends = jnp.cumsum(padded)
    starts = ends - padded
    lead = padded - used
    row = (starts + lead)[None, :] + jnp.cumsum(run, axis=0) - run
    run_table = jnp.stack([off, row, run], axis=2).reshape(-1)
    run_offsets = jnp.pad(off.astype(F32), ((0, 0), (0, LANES - N_EXPERTS)))[:, None, :]
    n_tiles = n_rows // TM_EXPERT
    used_tiles = ends[-1] // TM_EXPERT
    pad_table = jnp.stack([starts, lead], axis=1).reshape(-1)
    pad_table = jnp.concatenate([pad_table, jnp.stack([ends[-1], n_rows - ends[-1]])])
    tile_start = jnp.arange(n_tiles, dtype=I32) * TM_EXPERT
    tile_expert = jnp.sum((tile_start[:, None] >= ends[None, :]).astype(I32), axis=1)
    last_expert = jnp.sum((ends[-1] - 1 >= ends).astype(I32))
    tile_expert = jnp.minimum(tile_expert, last_expert)
    first_row = (starts + lead)[tile_expert]
    half = TM_EXPERT // 2
    tile_rows = jnp.clip(tile_start + TM_EXPERT - first_row, 0, TM_EXPERT)
    tile_valid = jnp.where(tile_start < ends[-1], (tile_rows + half - 1) // half, 0)
    tile_valid = jnp.concatenate([tile_valid, (used_tiles - 1)[None]])
    return run_table, run_offsets, pad_table, tile_expert, tile_valid


def _pad_time(a, nseq, t):
    w = a.shape[1]
    a = jnp.pad(a.reshape(nseq, t, w), ((0, 0), (0, SAMPLE_PAD_T - t), (0, 0)))
    return a.reshape(nseq * SAMPLE_PAD_T, w)


def kernel(x_prompt, x_sample, state_gla_S, state_mlstm_C, state_mlstm_n, state_mlstm_m,
           norm_mix, norm_ffn, norm_final,
           gla_w_in, gla_w_a2, gla_b_a, gla_g_norm, gla_w_out,
           mlstm_w_in, mlstm_b_gate, mlstm_g_norm, mlstm_w_out,
           ffn_w_gu, ffn_w_down,
           moe_w_router, moe_b_router, moe_w_gu, moe_w_down):
    bp, tp, d = x_prompt.shape
    bs, ts, _ = x_sample.shape
    n_p = bp * tp
    n_s = bs * ts
    n = n_p + n_s
    assert n_p % TM == 0 and n_s % TM == 0 and tp % (2 * CHUNK) == 0 and ts <= SAMPLE_PAD_T
    assert norm_mix.shape[0] == 2, "one GLA layer followed by one mLSTM layer"
    qk_w = d // 2
    main_w = 2 * qk_w + 2 * d

    x_p = x_prompt.reshape(n_p, d)
    x_s = x_sample.reshape(n_s, d)

    w_a2 = jnp.pad(gla_w_a2[0], ((0, LANES - GLA_RANK), (0, 0))).astype(BF16)
    z, log_a, ffn_wgu = _gla_in_proj(x_p, _pad_time(x_s, bs, ts), norm_mix[0][None], gla_w_in[0].T,
                                     main_w, w_a2, gla_b_a[0][None], ffn_w_gu[0])
    s0_p = jnp.zeros((bp,) + state_gla_S.shape[2:], F32)
    u_p, s_p = _gla_recurrence(z, log_a, s0_p, gla_g_norm[0], bp, tp, CHUNK, 16 * CHUNK)
    u_s, s_s = _gla_recurrence_short(z, log_a, state_gla_S[0], gla_g_norm[0], bs, SAMPLE_PAD_T, ts,
                                     SAMPLE_SEQS_PER_STEP, n_p)
    x, hn, ffn_wd = _out_proj(u_p, u_s, x_p, x_s, gla_w_out[0],
                              norm_ffn[0][None], ffn_w_down[0])
    x, hn, moe_wgu = _swiglu(hn, x, ffn_wgu, ffn_wd,
                             norm_mix[1][None], SWIGLU_CHUNKS,
                             moe_w_gu[0].reshape(N_EXPERTS * d, moe_w_gu.shape[3]))

    b_gt = jnp.pad(mlstm_b_gate[0], (0, LANES - 2 * MLSTM_HEADS))[None]
    dff_e = moe_w_down.shape[2]
    z, gates, moe_wd = _mlstm_in_proj(hn, n_p, _pad_time(hn[n_p:], bs, ts), mlstm_w_in[0].T, main_w,
                                      b_gt, moe_w_down[0].reshape(N_EXPERTS * dff_e, d))
    pairs = MLSTM_HEADS // 2
    dk2 = 2 * state_mlstm_C.shape[3]
    dvm = state_mlstm_C.shape[4]
    c0_p = jnp.zeros((bp, pairs, dk2, dvm), F32)
    n0_p = jnp.zeros((bp, pairs, dk2), F32)
    m0_p = jnp.zeros((bp, 1, LANES), F32)
    u_p, c_p, nn_p, m_p = _mlstm_recurrence_blocked(z, gates, c0_p, n0_p, m0_p,
                                                    mlstm_g_norm[0], bp, tp, CHUNK, 16 * CHUNK)
    c0_s = state_mlstm_C[0]
    n0_s = state_mlstm_n[0].reshape(bs, pairs, dk2)
    m0_s = jnp.pad(state_mlstm_m[0], ((0, 0), (0, LANES - MLSTM_HEADS)))[:, None, :]
    u_s, c_s, nn_s, m_s = _mlstm_recurrence_short(z, gates, c0_s, n0_s, m0_s, mlstm_g_norm[0], bs,
                                                  SAMPLE_PAD_T, ts, SAMPLE_SEQS_PER_STEP, n_p)

    w_r = jnp.pad(moe_w_router[0], ((0, 0), (0, LANES - N_EXPERTS)))
    w_r_hi = w_r.astype(BF16)
    w_r = jnp.stack([w_r_hi, (w_r - w_r_hi.astype(F32)).astype(BF16)])
    b_r = jnp.pad(moe_b_router[0], (0, LANES - N_EXPERTS))[None]
    x, hn, route_g, route_i, tile_counts = _out_proj_router(
        u_p, u_s, x, mlstm_w_out[0], norm_ffn[1][None], w_r, b_r)
    n_rows = 2 * n + (n // TM) * N_EXPERTS * (SUBLANES - 1) + N_EXPERTS * (TM_EXPERT - 1)
    n_rows = -(-n_rows // TM_EXPERT) * TM_EXPERT
    run_table, run_offsets, pad_table, tile_expert, tile_valid = _run_tables(tile_counts, n_rows)
    xs = _run_dispatch(run_table, pad_table, hn, route_i, run_offsets, n_rows)
    ys = _experts(tile_expert, tile_valid, xs, moe_wgu.reshape(moe_w_gu.shape[1:]),
                  moe_wd.reshape(moe_w_down.shape[1:]))
    y_p, y_s = _run_combine(run_table, ys, x, route_g, route_i, run_offsets, norm_final[None], n_p)

    y_prompt = y_p.reshape(bp, tp, d)
    y_sample = y_s.reshape(bs, ts, d)
    hd = state_mlstm_C.shape[2:]
    return (y_prompt, y_sample,
            s_p[None], c_p.reshape((1, bp) + hd), nn_p.reshape(1, bp, hd[0], hd[1]),
            m_p[None, :, 0, :MLSTM_HEADS],
            s_s[None], c_s[None], nn_s.reshape(1, bs, hd[0], hd[1]), m_s[None, :, 0, :MLSTM_HEADS])


def _mlstm_in_proj(hn, n_p, hn_s, w_in, wz, b_gate, cast_src):
    d = hn.shape[1]
    n = n_p + hn_s.shape[0]
    n_first = n_p // TM
    cast_in, cast_out, cast_shape = _cast_side_job(cast_src, n // TM)

    def kern(hp_ref, hs_ref, w_ref, wg_ref, b_ref, cast_ref, z_ref, gate_ref, cast_out_ref,
             wb_sc, wg_sc):
        @pl.when(pl.program_id(0) == 0)
        def _():
            wb_sc[...] = w_ref[...].T.astype(BF16)
            wg_sc[...] = _gate_weights(wg_ref)

        cast_out_ref[...] = cast_ref[...].astype(BF16)
        h = _pick_rows(pl.program_id(0), n_first, hp_ref, hs_ref)
        width = wz // IN_PROJ_COL_CHUNKS
        for c in range(IN_PROJ_COL_CHUNKS):
            cs = slice(c * width, (c + 1) * width)
            z_ref[:, cs] = _dot(h, wb_sc[:, cs])
        gp = _dot(h, wg_sc[...]) + b_ref[...]
        gc = GATE_CAP * jnp.tanh(gp * (1.0 / GATE_CAP))
        lane = lax.broadcasted_iota(I32, gc.shape, 1)
        out = jnp.where(lane < MLSTM_HEADS, gc, _log_sigmoid(gc))
        gate_ref[...] = jnp.where(lane < 2 * MLSTM_HEADS, out, 0.0)

    return pl.pallas_call(
        kern,
        grid=(n // TM,),
        in_specs=_split_specs(n_first, d) + [
            _resident_spec((wz, d), lambda i: (0, 0)),
            _gate_rows_spec(w_in, wz),
            pl.BlockSpec((1, LANES), lambda i: (0, 0)),
            cast_in,
        ],
        out_specs=[pl.BlockSpec((TM, wz), lambda i: (i, 0)), pl.BlockSpec((TM, LANES), lambda i: (i, 0)),
                   cast_out],
        out_shape=[jax.ShapeDtypeStruct((n, wz), F32), jax.ShapeDtypeStruct((n, LANES), F32),
                   cast_shape],
        scratch_shapes=[pltpu.VMEM((d, wz), BF16), pltpu.VMEM((d, LANES), BF16)],
        compiler_params=_params(("arbitrary",)),
        name="in_proj_mlstm",
    )(hn, hn_s, w_in, w_in, b_gate, cast_src)
```

```python
import functools

import jax
import jax.numpy as jnp
import numpy as np
from jax import lax
from jax.experimental import pallas as pl
from jax.experimental.pallas import tpu as pltpu

F32 = jnp.float32
BF16 = jnp.bfloat16
I32 = jnp.int32

EPS = 1e-6
GLA_HEADS = 4
GLA_RANK = 16
GLA_TAU = 16.0
MLSTM_HEADS = 8
GATE_CAP = 15.0
CHUNK = 64
N_EXPERTS = 8
NEG_BIG = -1e30

LANES = 128
TM = 512
TM_EXPERT = 512
SAMPLE_PAD_T = 8
SAMPLE_SEQS_PER_STEP = 16
SWIGLU_CHUNKS = 11
EXPERT_CHUNKS = 7
VMEM_LIMIT = 56 * 1024 * 1024


def _dot(a, b):
    return jnp.dot(a, b, preferred_element_type=F32)


def _dot_nt(a, b):
    return lax.dot_general(a, b, (((1,), (1,)), ((), ())), preferred_element_type=F32)


def _dot_tn(a, b):
    return lax.dot_general(a, b, (((0,), (0,)), ((), ())), preferred_element_type=F32)


def _sigmoid(x):
    return 1.0 / (1.0 + jnp.exp(-x))


def _log_sigmoid(x):
    return jnp.minimum(x, 0.0) - jnp.log(1.0 + jnp.exp(-jnp.abs(x)))


def _rms(x, g):
    return x * lax.rsqrt(jnp.mean(x * x, axis=-1, keepdims=True) + EPS) * g


def _cumsum_rows(x, n):
    ridx = lax.broadcasted_iota(I32, x.shape, 0)
    s = 1
    while s < n:
        x = x + jnp.where(ridx >= s, pltpu.roll(x, s, 0), 0.0)
        s *= 2
    return x


def _params(sem):
    return pltpu.CompilerParams(dimension_semantics=sem, vmem_limit_bytes=VMEM_LIMIT)


def _resident_spec(block_shape, index_map):
    return pl.BlockSpec(block_shape, index_map, pipeline_mode=pl.Buffered(1))


def _cast_side_job(src, n_steps):
    rows, cols = src.shape
    n_slabs = max(s for s in range(1, n_steps + 1) if rows % (16 * s) == 0)
    spec = pl.BlockSpec((rows // n_slabs, cols), lambda i, *_: (jnp.minimum(i, n_slabs - 1), 0))
    return spec, spec, jax.ShapeDtypeStruct(src.shape, BF16)


IN_PROJ_COL_CHUNKS = 4
ROW_PARTS = 2


def _pick_rows(i, n_first, first_ref, second_ref):
    return jnp.where(i < n_first, first_ref[...], second_ref[...])


def _split_specs(n_first, width):
    return [pl.BlockSpec((TM, width), lambda i, *_: (jnp.minimum(i, n_first - 1), 0)),
            pl.BlockSpec((TM, width), lambda i, *_: (jnp.maximum(i - n_first, 0), 0))]


def _gate_weights(wg_ref):
    rows, d = wg_ref.shape
    full = jnp.concatenate([wg_ref[...], jnp.zeros((LANES - rows, d), F32)], axis=0)
    return full.T.astype(BF16)


def _gla_in_proj_kernel(xp_ref, xs_ref, g_ref, w_ref, wg_ref, w2_ref, b_ref, cast_ref,
                        z_ref, gate_ref, cast_out_ref, wb_sc, wg_sc, *, n_first):
    @pl.when(pl.program_id(0) == 0)
    def _():
        wb_sc[...] = w_ref[...].T.astype(BF16)
        wg_sc[...] = _gate_weights(wg_ref)

    cast_out_ref[...] = cast_ref[...].astype(BF16)
    first = pl.program_id(0) < n_first
    width = wb_sc.shape[1] // IN_PROJ_COL_CHUNKS
    part = xp_ref.shape[0] // ROW_PARTS
    for r in range(ROW_PARTS):
        rows = slice(r * part, (r + 1) * part)
        x = jnp.where(first, xp_ref[rows, :], xs_ref[rows, :])
        hn = _rms(x, g_ref[...]).astype(BF16)
        a = _dot(hn, wg_sc[...])
        la = _dot(a.astype(BF16), w2_ref[...]) + b_ref[...]
        gate_ref[rows, :] = _log_sigmoid(la) * (1.0 / GLA_TAU)
        for c in range(IN_PROJ_COL_CHUNKS):
            cs = slice(c * width, (c + 1) * width)
            z_ref[rows, cs] = _dot(hn, wb_sc[:, cs])


def _gate_rows_spec(w_in, wz):
    rows = w_in.shape[0] - wz
    assert wz % rows == 0 and rows % SUBLANES == 0
    return pl.BlockSpec((rows, w_in.shape[1]), lambda i: (wz // rows, 0))


def _gla_in_proj(x_p, x_s, g, w_in, wz, w2, b, cast_src):
    d = x_p.shape[1]
    n = x_p.shape[0] + x_s.shape[0]
    n_first = x_p.shape[0] // TM
    gw = b.shape[1]
    cast_in, cast_out, cast_shape = _cast_side_job(cast_src, n // TM)
    return pl.pallas_call(
        functools.partial(_gla_in_proj_kernel, n_first=n_first),
        grid=(n // TM,),
        in_specs=_split_specs(n_first, d) + [
            pl.BlockSpec((1, d), lambda i: (0, 0)),
            _resident_spec((wz, d), lambda i: (0, 0)),
            _gate_rows_spec(w_in, wz),
            pl.BlockSpec(w2.shape, lambda i: (0, 0)),
            pl.BlockSpec((1, gw), lambda i: (0, 0)),
            cast_in,
        ],
        out_specs=[
            pl.BlockSpec((TM, wz), lambda i: (i, 0)),
            pl.BlockSpec((TM, gw), lambda i: (i, 0)),
            cast_out,
        ],
        out_shape=[jax.ShapeDtypeStruct((n, wz), F32), jax.ShapeDtypeStruct((n, gw), F32), cast_shape],
        scratch_shapes=[pltpu.VMEM((d, wz), BF16), pltpu.VMEM((d, LANES), BF16)],
        compiler_params=_params(("arbitrary",)),
        name="in_proj_gla",
    )(x_p, x_s, g, w_in, w_in, w2, b, cast_src)


def _gla_kernel(q_ref, k_ref, v_ref, r_ref, la_ref, s0_ref, gn_ref, u_ref, st_ref, st_sc,
                *, chunk, tb, nblk):
    heads = GLA_HEADS
    dk = q_ref.shape[1] // heads
    dv = v_ref.shape[1] // heads
    c = pl.program_id(1)

    @pl.when(c == 0)
    def _():
        for h in range(heads):
            st_sc[h] = s0_ref[0, h].T

    ri = lax.broadcasted_iota(I32, (chunk, chunk), 0)
    ci = lax.broadcasted_iota(I32, (chunk, chunk), 1)
    causal = ri >= ci
    kscale = dk ** -0.5
    state = [st_sc[h] for h in range(heads)]
    for j in range(tb // chunk):
        rows = slice(j * chunk, (j + 1) * chunk)
        b_all = _cumsum_rows(la_ref[rows, :], chunk)
        for h in range(heads):
            ks = slice(h * dk, (h + 1) * dk)
            vs = slice(h * dv, (h + 1) * dv)
            b = b_all[:, ks]
            b_end = b[chunk - 1:chunk, :]
            q = q_ref[rows, ks]
            k = k_ref[rows, ks] * kscale
            qg = (q * jnp.exp(b)).astype(BF16)
            kg = (k * jnp.exp(-b)).astype(BF16)
            ke = (k * jnp.exp(b_end - b)).astype(BF16)
            v = v_ref[rows, vs].astype(BF16)
            a = jnp.where(causal, _dot_nt(qg, kg), 0.0).astype(BF16)
            o = _dot_nt(qg, state[h].astype(BF16)) + _dot(a, v)
            state[h] = state[h] * jnp.exp(b_end) + _dot_tn(v, ke)
            on = _rms(o, gn_ref[h:h + 1, :])
            r = r_ref[rows, vs]
            u_ref[rows, vs] = (r * _sigmoid(r) * on).astype(BF16)
    for h in range(heads):
        st_sc[h] = state[h]

    @pl.when(c == nblk - 1)
    def _():
        for h in range(heads):
            st_ref[0, h] = st_sc[h].T


def _gla_recurrence(z, log_a, s0, g_norm, nseq, t, chunk, tb):
    qk_w = log_a.shape[1]
    v_w = 2 * qk_w
    nblk = t // tb
    heads, dk, dv = s0.shape[1:]
    kern = functools.partial(_gla_kernel, chunk=chunk, tb=tb, nblk=nblk)
    row = lambda b, c: b * nblk + c
    return pl.pallas_call(
        kern,
        grid=(nseq, nblk),
        in_specs=[
            pl.BlockSpec((tb, qk_w), lambda b, c: (row(b, c), 0)),
            pl.BlockSpec((tb, qk_w), lambda b, c: (row(b, c), 1)),
            pl.BlockSpec((tb, v_w), lambda b, c: (row(b, c), 1)),
            pl.BlockSpec((tb, v_w), lambda b, c: (row(b, c), 2)),
            pl.BlockSpec((tb, qk_w), lambda b, c: (row(b, c), 0)),
            pl.BlockSpec((1, heads, dk, dv), lambda b, c: (b, 0, 0, 0)),
            pl.BlockSpec((heads, dv), lambda b, c: (0, 0)),
        ],
        out_specs=[
            pl.BlockSpec((tb, v_w), lambda b, c: (row(b, c), 0)),
            pl.BlockSpec((1, heads, dk, dv), lambda b, c: (b, 0, 0, 0)),
        ],
        out_shape=[
            jax.ShapeDtypeStruct((nseq * t, v_w), BF16),
            jax.ShapeDtypeStruct(s0.shape, F32),
        ],
        scratch_shapes=[pltpu.VMEM((heads, dv, dk), F32)],
        compiler_params=_params(("arbitrary", "arbitrary")),
        name="gla_recurrence_t%d" % t,
    )(z, z, z, z, log_a, s0, g_norm)


def _real_rows_selector(nsq, t, t_real):
    i = lax.broadcasted_iota(I32, (nsq * t_real, nsq * t), 0)
    r = lax.broadcasted_iota(I32, (nsq * t_real, nsq * t), 1)
    return jnp.where(r == (i // t_real) * t + i % t_real, 1.0, 0.0).astype(BF16)


def _gla_short_kernel(q_ref, k_ref, v_ref, r_ref, la_ref, s0_ref, gn_ref, u_ref, st_ref,
                      *, t, t_real, nsq):
    heads = GLA_HEADS
    dk = q_ref.shape[1] // heads
    dv = v_ref.shape[1] // heads
    nrow = nsq * t
    width = q_ref.shape[1]
    rc = lax.broadcasted_iota(I32, (nrow, width), 0) & (t - 1)
    log_a = jnp.where(rc < t_real, la_ref[...], 0.0)
    b = _chunk_scan(log_a, rc, t, jnp.add, 0.0)
    pick = _real_rows_selector(nsq, t, t_real)
    b_last = b.reshape(nsq, t, width)[:, t - 1:t, :]
    b_end = jnp.broadcast_to(b_last, (nsq, t, width)).reshape(nrow, width)
    k = k_ref[...] * (dk ** -0.5)
    qg = q_ref[...] * jnp.exp(b)
    kg = k * jnp.exp(-b)
    ke = k * jnp.exp(b_end - b)
    decay = jnp.exp(b_end)
    ri = lax.broadcasted_iota(I32, (nrow, nrow), 0)
    ci = lax.broadcasted_iota(I32, (nrow, nrow), 1)
    shift = t.bit_length() - 1
    mask = ((ri >> shift) == (ci >> shift)) & (ri >= ci)
    for h in range(heads):
        ks = slice(h * dk, (h + 1) * dk)
        vs = slice(h * dv, (h + 1) * dv)
        a = jnp.where(mask, _dot_nt(qg[:, ks].astype(BF16), kg[:, ks].astype(BF16)), 0.0)
        o = _dot(a.astype(BF16), v_ref[:, vs].astype(BF16))
        decay_t = decay[:, ks].T
        o_state = []
        for sq in range(nsq):
            r = slice(sq * t, (sq + 1) * t)
            s0 = s0_ref[sq, h]
            o_state.append(_dot(qg[r, ks].astype(BF16), s0.astype(BF16)))
            upd = _dot_tn(ke[r, ks].astype(BF16), v_ref[r, vs].astype(BF16))
            st_ref[sq, h] = decay_t[:, sq * t:sq * t + 1] * s0 + upd
        on = _rms(o + jnp.concatenate(o_state, axis=0), gn_ref[h:h + 1, :])
        rg = r_ref[:, vs]
        gated = (rg * _sigmoid(rg) * on).astype(BF16)
        u_ref[:, vs] = _dot(pick, gated).astype(BF16)


def _gla_recurrence_short(z, log_a, s0, g_norm, nseq, t, t_real, nsq, row0):
    qk_w = log_a.shape[1]
    v_w = 2 * qk_w
    heads, dk, dv = s0.shape[1:]
    assert nsq * t == dk == LANES and row0 % (nsq * t) == 0
    rb = nsq * t
    blk0 = row0 // rb
    return pl.pallas_call(
        functools.partial(_gla_short_kernel, t=t, t_real=t_real, nsq=nsq),
        grid=(nseq // nsq,),
        in_specs=[
            pl.BlockSpec((rb, qk_w), lambda b: (blk0 + b, 0)),
            pl.BlockSpec((rb, qk_w), lambda b: (blk0 + b, 1)),
            pl.BlockSpec((rb, v_w), lambda b: (blk0 + b, 1)),
            pl.BlockSpec((rb, v_w), lambda b: (blk0 + b, 2)),
            pl.BlockSpec((rb, qk_w), lambda b: (blk0 + b, 0)),
            pl.BlockSpec((nsq, heads, dk, dv), lambda b: (b, 0, 0, 0)),
            pl.BlockSpec((heads, dv), lambda b: (0, 0)),
        ],
        out_specs=[
            pl.BlockSpec((nsq * t_real, v_w), lambda b: (b, 0)),
            pl.BlockSpec((nsq, heads, dk, dv), lambda b: (b, 0, 0, 0)),
        ],
        out_shape=[
            jax.ShapeDtypeStruct((nseq * t_real, v_w), BF16),
            jax.ShapeDtypeStruct(s0.shape, F32),
        ],
        compiler_params=_params(("arbitrary",)),
        name="gla_recurrence_t%d" % t,
    )(z, z, z, z, log_a, s0, g_norm)


def _mlstm_short_kernel(q_ref, k_ref, v_ref, op_ref, gt_ref, e_ref, c0_ref, n0_ref, m0_ref, gn_ref,
                        u_ref, ct_ref, nt_ref, mt_ref, *, t, t_real, nsq):
    heads = MLSTM_HEADS
    pairs = heads // 2
    dv = v_ref.shape[1] // heads
    dk = q_ref.shape[1] // heads
    nrow = nsq * t

    def per_seq(x3):
        return jnp.broadcast_to(x3, (nsq, t, x3.shape[2])).reshape(nrow, x3.shape[2])

    def last(x):
        return x.reshape(nsq, t, x.shape[1])[:, t - 1:t, :]

    rc = lax.broadcasted_iota(I32, (nrow, LANES), 0) & (t - 1)
    lane_g = lax.broadcasted_iota(I32, (nrow, LANES), 1)
    gts = jnp.where(rc < t_real, gt_ref[...],
                    jnp.where(lane_g < heads, NEG_BIG, 0.0))
    pick = _real_rows_selector(nsq, t, t_real)
    f_cum = pltpu.roll(_chunk_scan(gts, rc, t, jnp.add, 0.0), LANES - heads, 1)
    b = gts - f_cum
    cmb = _chunk_scan(b, rc, t, jnp.maximum, NEG_BIG)
    mp = per_seq(m0_ref[...])
    big_m = jnp.maximum(mp, cmb)
    m_t = f_cum + big_m
    m_new = last(m_t)
    mt_ref[...] = m_new
    mn = per_seq(m_new)
    fe = per_seq(last(f_cum))
    w_i = jnp.exp(mp - big_m)
    em = jnp.exp(-m_t)
    k_sc = jnp.exp(fe - f_cum + gts - mn)
    w_c = jnp.exp(fe + mp - mn)
    ex = _dot(_pack_terms([w_i, k_sc, -big_m, em, w_c], heads, SHORT_EXPAND_TERMS), e_ref[...])
    offs = np.cumsum([0] + [heads * w for w in SHORT_EXPAND_WIDTHS])
    q = q_ref[...]
    k = k_ref[...] * (dk ** -0.5)
    qs = q * ex[:, offs[0]:offs[1]]
    ke = k * ex[:, offs[1]:offs[2]]
    neg_m = ex[:, offs[2]:offs[3]]
    em_v = ex[:, offs[3]:offs[4]]
    wc_k = ex[:, offs[4]:offs[5]]
    bt = b.T

    ri = lax.broadcasted_iota(I32, (nrow, nrow), 0)
    ci = lax.broadcasted_iota(I32, (nrow, nrow), 1)
    shift = t.bit_length() - 1
    mask = ((ri >> shift) == (ci >> shift)) & (ri >= ci)
    lane1 = lax.broadcasted_iota(I32, (1, LANES), 1)
    rowi = lax.broadcasted_iota(I32, (LANES, LANES), 0)
    ones_v = jnp.ones((nrow, dv), BF16)
    ones_k = jnp.ones((LANES, dv), BF16)
    for p in range(pairs):
        ps = slice(p * LANES, (p + 1) * LANES)
        qp = q[:, ps]
        qsp = qs[:, ps]
        kpb = k[:, ps].astype(BF16)
        n_rows = per_seq(n0_ref[:, p:p + 1, :])
        intra, den_state, qsh = [], [], []
        for jj in range(2):
            h = 2 * p + jj
            hs = slice(h * dv, (h + 1) * dv)
            mine = (lane1 >= jj * dk) & (lane1 < (jj + 1) * dk)
            qh = jnp.where(mine, qp, 0.0).astype(BF16)
            qsh.append(jnp.where(mine, qsp, 0.0))
            d = neg_m[:, hs] + bt[h:h + 1, :]
            s = _dot_nt(qh, kpb) * jnp.where(mask, jnp.exp(d), 0.0)
            vh = jnp.concatenate([v_ref[:, hs].astype(BF16), ones_v], axis=1)
            intra.append(_dot(s.astype(BF16), vh))
            den_state.append(_dot((qsh[jj] * n_rows).astype(BF16), ones_k))
        num_state = [[], []]
        for sq in range(nsq):
            r = slice(sq * t, (sq + 1) * t)
            lhs = jnp.concatenate([qsh[0][r], qsh[1][r]], axis=0).astype(BF16)
            c_pair = jnp.concatenate([c0_ref[sq, 2 * p], c0_ref[sq, 2 * p + 1]], axis=0)
            res = _dot(lhs, c_pair.astype(BF16))
            num_state[0].append(res[:t])
            num_state[1].append(res[t:])
        for jj in range(2):
            h = 2 * p + jj
            hs = slice(h * dv, (h + 1) * dv)
            num = intra[jj][:, :dv] + jnp.concatenate(num_state[jj], axis=0)
            den = intra[jj][:, dv:] + den_state[jj]
            hh = num / jnp.maximum(jnp.abs(den), em_v[:, hs])
            hn = _rms(hh, gn_ref[h:h + 1, :])
            gated = (_sigmoid(op_ref[:, hs]) * hn).astype(BF16)
            u_ref[:, hs] = _dot(pick, gated).astype(BF16)
        kep = ke[:, ps]
        for sq in range(nsq):
            r = slice(sq * t, (sq + 1) * t)
            vp = v_ref[r, 2 * p * dv:(2 * p + 2) * dv].astype(BF16)
            full = _dot_tn(kep[r].astype(BF16), vp)
            upd = jnp.where(rowi < dk, full[:, :dv], full[:, dv:])
            w_row = w_c[sq * t:sq * t + 1, :]
            w_col = jnp.where(rowi < dk, w_row[:, 2 * p:2 * p + 1], w_row[:, 2 * p + 1:2 * p + 2])
            c_pair = jnp.concatenate([c0_ref[sq, 2 * p], c0_ref[sq, 2 * p + 1]], axis=0)
            c_new = w_col * c_pair + upd
            ct_ref[sq, 2 * p] = c_new[:dk]
            ct_ref[sq, 2 * p + 1] = c_new[dk:]
        k_sum = jnp.sum(kep.reshape(nsq, t, LANES), axis=1, keepdims=True)
        nt_ref[:, p:p + 1, :] = last(wc_k[:, ps]) * n0_ref[:, p:p + 1, :] + k_sum


def _mlstm_recurrence_short(z, gates, c0, n0, m0, g_norm, nseq, t, t_real, nsq, row0):
    heads = g_norm.shape[0]
    v_w = heads * g_norm.shape[1]
    qk_w = v_w // 2
    assert nsq * t == LANES and row0 % (nsq * t) == 0
    expand = _expand_matrix(heads, SHORT_EXPAND_WIDTHS, SHORT_EXPAND_TERMS)
    kern = functools.partial(_mlstm_short_kernel, t=t, t_real=t_real, nsq=nsq)
    rb = nsq * t
    blk0 = row0 // rb
    st4 = lambda b: (b, 0, 0, 0)
    st3 = lambda b: (b, 0, 0)
    return pl.pallas_call(
        kern,
        grid=(nseq // nsq,),
        in_specs=[
            pl.BlockSpec((rb, qk_w), lambda b: (blk0 + b, 0)),
            pl.BlockSpec((rb, qk_w), lambda b: (blk0 + b, 1)),
            pl.BlockSpec((rb, v_w), lambda b: (blk0 + b, 1)),
            pl.BlockSpec((rb, v_w), lambda b: (blk0 + b, 2)),
            pl.BlockSpec((rb, LANES), lambda b: (blk0 + b, 0)),
            pl.BlockSpec(expand.shape, lambda b: (0, 0)),
            pl.BlockSpec((nsq,) + c0.shape[1:], st4),
            pl.BlockSpec((nsq,) + n0.shape[1:], st3),
            pl.BlockSpec((nsq,) + m0.shape[1:], st3),
            pl.BlockSpec(g_norm.shape, lambda b: (0, 0)),
        ],
        out_specs=[
            pl.BlockSpec((nsq * t_real, v_w), lambda b: (b, 0)),
            pl.BlockSpec((nsq,) + c0.shape[1:], st4),
            pl.BlockSpec((nsq,) + n0.shape[1:], st3),
            pl.BlockSpec((nsq,) + m0.shape[1:], st3),
        ],
        out_shape=[
            jax.ShapeDtypeStruct((nseq * t_real, v_w), BF16),
            jax.ShapeDtypeStruct(c0.shape, F32),
            jax.ShapeDtypeStruct(n0.shape, F32),
            jax.ShapeDtypeStruct(m0.shape, F32),
        ],
        compiler_params=_params(("arbitrary",)),
        name="mlstm_recurrence_t%d" % t,
    )(z, z, z, z, gates, expand, c0, n0, m0, g_norm)


EXPAND_WIDTHS = (64, 64, 128, 128)
SHORT_EXPAND_WIDTHS = EXPAND_WIDTHS + (64,)
EXPAND_TERMS = (2, 2, 3, 2)
SHORT_EXPAND_TERMS = EXPAND_TERMS + (2,)


def _expand_matrix(heads, widths, terms):
    assert sum(terms) * heads <= LANES
    cols = []
    slot = 0
    for width, n_terms in zip(widths, terms):
        sel = np.zeros((LANES, heads * width), np.float32)
        for _ in range(n_terms):
            for h in range(heads):
                sel[slot * heads + h, h * width:(h + 1) * width] = 1.0
            slot += 1
        cols.append(sel)
    return jnp.asarray(np.concatenate(cols, axis=1), BF16)


def _chunk_scan(x, rc, n, op, fill):
    s = 1
    while s < n:
        x = op(x, jnp.where(rc >= s, pltpu.roll(x, s, 0), fill))
        s *= 2
    return x


def _pack_terms(values, heads, terms):
    lane = lax.broadcasted_iota(I32, values[0].shape, 1)
    packed = jnp.zeros(values[0].shape, F32)
    slot = 0
    for val, n_terms in zip(values, terms):
        rest = val
        for term in range(n_terms):
            part = rest.astype(BF16).astype(F32)
            rest = rest - part
            moved = part if slot == 0 else pltpu.roll(part, slot * heads, 1)
            packed = jnp.where((lane >= slot * heads) & (lane < (slot + 1) * heads), moved, packed)
            slot += 1
    return packed.astype(BF16)


def _mlstm_block_kernel(q_ref, k_ref, v_ref, op_ref, gt_ref, e_ref, c0_ref, n0_ref, m0_ref, gn_ref,
                        u_ref, ct_ref, nt_ref, mt_ref, c_sc, nb_sc, m_sc, *, chunk, tb, nblk):
    heads = MLSTM_HEADS
    pairs = heads // 2
    dv = v_ref.shape[1] // heads
    dk = q_ref.shape[1] // heads
    c = pl.program_id(1)

    @pl.when(c == 0)
    def _():
        c_sc[...] = c0_ref[0]
        for p in range(pairs):
            nb_sc[p] = jnp.broadcast_to(n0_ref[0, p:p + 1, :], (LANES, LANES)).T
        m_sc[...] = m0_ref[0]

    nch = tb // chunk
    rc = lax.broadcasted_iota(I32, (tb, LANES), 0) & (chunk - 1)
    gts = gt_ref[...]
    f_cum = pltpu.roll(_chunk_scan(gts, rc, chunk, jnp.add, 0.0), LANES - heads, 1)
    b = gts - f_cum
    cmb = _chunk_scan(b, rc, chunk, jnp.maximum, NEG_BIG)
    m_prev = m_sc[...]
    mp_rows, mn_rows, fe_rows, w_c = [], [], [], []
    for j in range(nch):
        last = (j + 1) * chunk - 1
        f_end = f_cum[last:last + 1, :]
        m_new = f_end + jnp.maximum(m_prev, cmb[last:last + 1, :])
        w_c.append(jnp.exp(f_end + m_prev - m_new))
        mp_rows.append(jnp.broadcast_to(m_prev, (chunk, LANES)))
        mn_rows.append(jnp.broadcast_to(m_new, (chunk, LANES)))
        fe_rows.append(jnp.broadcast_to(f_end, (chunk, LANES)))
        m_prev = m_new
    m_sc[...] = m_prev
    mp = jnp.concatenate(mp_rows, axis=0)
    mn = jnp.concatenate(mn_rows, axis=0)
    fe = jnp.concatenate(fe_rows, axis=0)
    big_m = jnp.maximum(mp, cmb)
    w_i = jnp.exp(mp - big_m)
    em = jnp.exp(-(f_cum + big_m))
    k_sc = jnp.exp(fe - f_cum + gts - mn)
    ex = _dot(_pack_terms([w_i, k_sc, -big_m, em], heads, EXPAND_TERMS), e_ref[...])
    o1 = heads * EXPAND_WIDTHS[0]
    o2 = o1 + heads * EXPAND_WIDTHS[1]
    o3 = o2 + heads * EXPAND_WIDTHS[2]
    q = q_ref[...]
    k = k_ref[...] * (dk ** -0.5)
    qs = q * ex[:, :o1]
    ke = k * ex[:, o1:o2]
    neg_m = ex[:, o2:o3]
    em_v = ex[:, o3:]
    bt = b.T

    ri = lax.broadcasted_iota(I32, (chunk, chunk), 0)
    ci = lax.broadcasted_iota(I32, (chunk, chunk), 1)
    causal = ri >= ci
    lane1 = lax.broadcasted_iota(I32, (1, LANES), 1)
    rowi = lax.broadcasted_iota(I32, (LANES, LANES), 0)
    ones_v = jnp.ones((chunk, dv), BF16)
    states = [[None] * pairs for _ in range(nch)]
    for p in range(pairs):
        ps = slice(p * LANES, (p + 1) * LANES)
        cp = c_sc[p]
        nb = nb_sc[p]
        for j in range(nch):
            rows = slice(j * chunk, (j + 1) * chunk)
            states[j][p] = jnp.concatenate([cp, nb], axis=1).astype(BF16)
            vp = jnp.concatenate([v_ref[rows, 2 * p * dv:(2 * p + 2) * dv].astype(BF16), ones_v], axis=1)
            full = _dot_tn(ke[rows, ps].astype(BF16), vp)
            upd = jnp.where(rowi < dk, full[:, :dv], full[:, dv:2 * dv])
            w_col = jnp.where(rowi < dk, w_c[j][:, 2 * p:2 * p + 1], w_c[j][:, 2 * p + 1:2 * p + 2])
            cp = w_col * cp + upd
            nb = w_col * nb + full[:, 2 * dv:]
        c_sc[p] = cp
        nb_sc[p] = nb
    for j in range(nch):
        rows = slice(j * chunk, (j + 1) * chunk)
        for p in range(pairs):
            ps = slice(p * LANES, (p + 1) * LANES)
            qp = q[rows, ps]
            qsp = qs[rows, ps]
            kpb = k[rows, ps].astype(BF16)
            state = states[j][p]
            for jj in range(2):
                h = 2 * p + jj
                hs = slice(h * dv, (h + 1) * dv)
                mine = (lane1 >= jj * dk) & (lane1 < (jj + 1) * dk)
                qh = jnp.where(mine, qp, 0.0).astype(BF16)
                qsh = jnp.where(mine, qsp, 0.0).astype(BF16)
                d = neg_m[rows, h * dv:h * dv + chunk] + bt[h:h + 1, j * chunk:(j + 1) * chunk]
                s = _dot_nt(qh, kpb) * jnp.where(causal, jnp.exp(d), 0.0)
                vh = jnp.concatenate([v_ref[rows, hs].astype(BF16), ones_v], axis=1)
                out = _dot(qsh, state) + _dot(s.astype(BF16), vh)
                hh = out[:, :dv] / jnp.maximum(jnp.abs(out[:, dv:]), em_v[rows, hs])
                hn = _rms(hh, gn_ref[h:h + 1, :])
                u_ref[rows, hs] = (_sigmoid(op_ref[rows, hs]) * hn).astype(BF16)

    @pl.when(c == nblk - 1)
    def _():
        ct_ref[0] = c_sc[...]
        for p in range(pairs):
            nt_ref[0, p:p + 1, :] = nb_sc[p].T[0:1, :]
        mt_ref[0] = m_sc[...]


def _mlstm_recurrence_blocked(z, gates, c0, n0, m0, g_norm, nseq, t, chunk, tb):
    heads = g_norm.shape[0]
    v_w = heads * g_norm.shape[1]
    qk_w = v_w // 2
    nblk = t // tb
    expand = _expand_matrix(heads, EXPAND_WIDTHS, EXPAND_TERMS)
    kern = functools.partial(_mlstm_block_kernel, chunk=chunk, tb=tb, nblk=nblk)
    row = lambda b, c: b * nblk + c
    st4 = lambda b, c: (b, 0, 0, 0)
    st3 = lambda b, c: (b, 0, 0)
    return pl.pallas_call(
        kern,
        grid=(nseq, nblk),
        in_specs=[
            pl.BlockSpec((tb, qk_w), lambda b, c: (row(b, c), 0)),
            pl.BlockSpec((tb, qk_w), lambda b, c: (row(b, c), 1)),
            pl.BlockSpec((tb, v_w), lambda b, c: (row(b, c), 1)),
            pl.BlockSpec((tb, v_w), lambda b, c: (row(b, c), 2)),
            pl.BlockSpec((tb, LANES), lambda b, c: (row(b, c), 0)),
            pl.BlockSpec(expand.shape, lambda b, c: (0, 0)),
            pl.BlockSpec((1,) + c0.shape[1:], st4),
            pl.BlockSpec((1,) + n0.shape[1:], st3),
            pl.BlockSpec((1,) + m0.shape[1:], st3),
            pl.BlockSpec(g_norm.shape, lambda b, c: (0, 0)),
        ],
        out_specs=[
            pl.BlockSpec((tb, v_w), lambda b, c: (row(b, c), 0)),
            pl.BlockSpec((1,) + c0.shape[1:], st4),
            pl.BlockSpec((1,) + n0.shape[1:], st3),
            pl.BlockSpec((1,) + m0.shape[1:], st3),
        ],
        out_shape=[
            jax.ShapeDtypeStruct((nseq * t, v_w), BF16),
            jax.ShapeDtypeStruct(c0.shape, F32),
            jax.ShapeDtypeStruct(n0.shape, F32),
            jax.ShapeDtypeStruct(m0.shape, F32),
        ],
        scratch_shapes=[
            pltpu.VMEM(c0.shape[1:], F32),
            pltpu.VMEM(c0.shape[1:], F32),
            pltpu.VMEM(m0.shape[1:], F32),
        ],
        compiler_params=_params(("arbitrary", "arbitrary")),
        name="mlstm_recurrence_t%d" % t,
    )(z, z, z, z, gates, expand, c0, n0, m0, g_norm)


def _out_proj_kernel(up_ref, us_ref, xp_ref, xs_ref, w_ref, g_ref, cast_ref,
                     xo_ref, hn_ref, cast_out_ref, wb_sc, *, n_first):
    i = pl.program_id(0)

    @pl.when(i == 0)
    def _():
        wb_sc[...] = w_ref[...].astype(BF16)

    cast_out_ref[...] = cast_ref[...].astype(BF16)
    u = _pick_rows(i, n_first, up_ref, us_ref)
    x = _pick_rows(i, n_first, xp_ref, xs_ref) + _dot(u, wb_sc[...])
    xo_ref[...] = x
    hn_ref[...] = _rms(x, g_ref[...]).astype(hn_ref.dtype)


def _out_proj(u_p, u_s, x_p, x_s, w, g, cast_src):
    d = x_p.shape[1]
    n = x_p.shape[0] + x_s.shape[0]
    n_first = x_p.shape[0] // TM
    cast_in, cast_out, cast_shape = _cast_side_job(cast_src, n // TM)
    return pl.pallas_call(
        functools.partial(_out_proj_kernel, n_first=n_first),
        grid=(n // TM,),
        in_specs=_split_specs(n_first, u_p.shape[1]) + _split_specs(n_first, d) + [
            _resident_spec(w.shape, lambda i: (0, 0)),
            pl.BlockSpec((1, d), lambda i: (0, 0)),
            cast_in,
        ],
        out_specs=[pl.BlockSpec((TM, d), lambda i: (i, 0)), pl.BlockSpec((TM, d), lambda i: (i, 0)),
                   cast_out],
        out_shape=[jax.ShapeDtypeStruct((n, d), F32), jax.ShapeDtypeStruct((n, d), BF16), cast_shape],
        scratch_shapes=[pltpu.VMEM(w.shape, BF16)],
        compiler_params=_params(("arbitrary",)),
        name="out_proj",
    )(u_p, u_s, x_p, x_s, w, g, cast_src)


def _out_proj_router_kernel(up_ref, us_ref, x_ref, w_ref, g_ref, wr_ref, br_ref,
                            xo_ref, hn_ref, gate_ref, idx_ref, cnt_ref, logit_sc, wb_sc,
                            *, n_first, n_tiles):
    i = pl.program_id(0)

    @pl.when(i == 0)
    def _():
        logit_sc[...] = jnp.zeros_like(logit_sc)
        wb_sc[...] = w_ref[...].astype(BF16)

    logits = logit_sc[(i + 1) & 1]
    u = _pick_rows(jnp.minimum(i, n_tiles - 1), n_first, up_ref, us_ref)
    x = x_ref[...] + _dot(u, wb_sc[...])
    xo_ref[...] = x
    hn = _rms(x, g_ref[...])
    hn_hi = hn.astype(BF16)
    hn_ref[...] = hn_hi
    hn_lo = (hn - hn_hi.astype(F32)).astype(BF16)
    logit_sc[i & 1] = (_dot(hn_hi, wr_ref[0]) + _dot(hn_lo, wr_ref[0]) + _dot(hn_hi, wr_ref[1])
                       + br_ref[...])
    tm = logits.shape[0]
    lane = lax.broadcasted_iota(I32, logits.shape, 1)
    valid = lane < N_EXPERTS
    logits = jnp.where(valid, logits, -jnp.inf)
    ex = jnp.exp(logits - jnp.max(logits, axis=-1, keepdims=True))
    probs = jnp.where(valid, ex / jnp.sum(ex, axis=-1, keepdims=True), -1.0)
    p1 = jnp.max(probs, axis=-1, keepdims=True)
    i1 = jnp.min(jnp.where(probs == p1, lane, LANES), axis=-1, keepdims=True)
    rest = jnp.where(lane == i1, -1.0, probs)
    p2 = jnp.max(rest, axis=-1, keepdims=True)
    i2 = jnp.min(jnp.where(rest == p2, lane, LANES), axis=-1, keepdims=True)
    tot = p1 + p2
    onehot = ((lane == i1) | (lane == i2)).astype(BF16)
    rr = lax.broadcasted_iota(I32, (tm, tm), 0)
    cc = lax.broadcasted_iota(I32, (tm, tm), 1)
    strict = (rr > cc).astype(BF16)
    before = _dot(strict, onehot)
    r1 = jnp.sum(jnp.where(lane == i1, before, 0.0), axis=-1, keepdims=True)
    r2 = jnp.sum(jnp.where(lane == i2, before, 0.0), axis=-1, keepdims=True)
    gate_ref[...] = jnp.where(lane == 0, p1 / tot, jnp.where(lane == 1, p2 / tot, 0.0))
    idx_ref[...] = jnp.where(lane == 0, i1,
                             jnp.where(lane == 1, i2,
                                       jnp.where(lane == 2, r1.astype(I32),
                                                 jnp.where(lane == 3, r2.astype(I32), 0))))
    cnt_ref[0] = jnp.sum(onehot.astype(F32), axis=0, keepdims=True)


def _out_proj_router(u_p, u_s, x, w, g, w_router, b_router):
    n, d = x.shape
    n_first = u_p.shape[0] // TM
    n_tiles = n // TM
    row = lambda i: (jnp.minimum(i, n_tiles - 1), 0)
    routed = lambda i: (jnp.maximum(i - 1, 0), 0)
    fix = lambda i: (0, 0)
    u_specs = [pl.BlockSpec((TM, u_p.shape[1]), lambda i: (jnp.minimum(i, n_first - 1), 0)),
               pl.BlockSpec((TM, u_p.shape[1]),
                            lambda i: (jnp.clip(i - n_first, 0, n_tiles - n_first - 1), 0))]
    return pl.pallas_call(
        functools.partial(_out_proj_router_kernel, n_first=n_first, n_tiles=n_tiles),
        grid=(n_tiles + 1,),
        in_specs=u_specs + [
            pl.BlockSpec((TM, d), row),
            _resident_spec(w.shape, fix),
            pl.BlockSpec((1, d), fix),
            pl.BlockSpec(w_router.shape, lambda i: (0, 0, 0)),
            pl.BlockSpec((1, LANES), fix),
        ],
        out_specs=[
            pl.BlockSpec((TM, d), row),
            pl.BlockSpec((TM, d), row),
            pl.BlockSpec((TM, LANES), routed),
            pl.BlockSpec((TM, LANES), routed),
            pl.BlockSpec((1, 1, LANES), lambda i: (jnp.maximum(i - 1, 0), 0, 0)),
        ],
        out_shape=[
            jax.ShapeDtypeStruct((n, d), F32),
            jax.ShapeDtypeStruct((n, d), BF16),
            jax.ShapeDtypeStruct((n, LANES), F32),
            jax.ShapeDtypeStruct((n, LANES), I32),
            jax.ShapeDtypeStruct((n_tiles, 1, LANES), F32),
        ],
        scratch_shapes=[pltpu.VMEM((2, TM, LANES), F32), pltpu.VMEM(w.shape, BF16)],
        compiler_params=_params(("arbitrary",)),
        name="out_proj_router",
    )(u_p, u_s, x, w, g, w_router, b_router)


def _swiglu_kernel(hn_ref, x_ref, wg_ref, wu_ref, wd_ref, g_ref, cast_ref, xo_ref, ho_ref, cast_out_ref,
                   *, nchunk):
    cast_out_ref[...] = cast_ref[...].astype(BF16)
    hn = hn_ref[...]
    width = wd_ref.shape[0] // nchunk
    x = x_ref[...]
    for c in range(nchunk):
        cs = slice(c * width, (c + 1) * width)
        gate = _dot(hn, wg_ref[:, cs])
        up = _dot(hn, wu_ref[:, cs])
        act = (gate * _sigmoid(gate) * up).astype(BF16)
        x = x + _dot(act, wd_ref[cs, :])
    xo_ref[...] = x
    ho_ref[...] = _rms(x, g_ref[...]).astype(BF16)


def _swiglu(hn, x, w_gu, w_down, g, nchunk, cast_src):
    n, d = x.shape
    dff = w_down.shape[0]
    kern = functools.partial(_swiglu_kernel, nchunk=nchunk)
    row = lambda i: (i, 0)
    cast_in, cast_out, cast_shape = _cast_side_job(cast_src, n // TM)
    return pl.pallas_call(
        kern,
        grid=(n // TM,),
        in_specs=[
            pl.BlockSpec((TM, d), row),
            pl.BlockSpec((TM, d), row),
            _resident_spec((d, dff), lambda i: (0, 0)),
            _resident_spec((d, dff), lambda i: (0, 1)),
            _resident_spec((dff, d), lambda i: (0, 0)),
            pl.BlockSpec((1, d), lambda i: (0, 0)),
            cast_in,
        ],
        out_specs=[pl.BlockSpec((TM, d), row), pl.BlockSpec((TM, d), row), cast_out],
        out_shape=[jax.ShapeDtypeStruct((n, d), F32), jax.ShapeDtypeStruct((n, d), BF16), cast_shape],
        compiler_params=_params(("arbitrary",)),
        name="dense_swiglu",
    )(hn, x, w_gu, w_gu, w_down, g, cast_src)


def _row_copy(src_ref, src_row, dst_ref, dst_row, sem):
    return pltpu.make_async_copy(src_ref.at[pl.ds(src_row, 1), :], dst_ref.at[pl.ds(dst_row, 1), :], sem)


SUBLANES = 8


def _zero_rows(zero_ref, dst_ref, dst_row, count, limit, sem, wait):
    head = jnp.minimum((-dst_row) & (SUBLANES - 1), count)
    for j in range(SUBLANES - 1):
        @pl.when(j < head)
        def _(j=j):
            copy = _row_copy(zero_ref, 0, dst_ref, dst_row + j, sem)
            copy.wait() if wait else copy.start()

    rest = count - head
    first = dst_row + head
    for bit in range(SUBLANES.bit_length() - 1, limit.bit_length() - 1):
        size = 1 << bit
        done = (rest >> (bit + 1)) << (bit + 1)

        @pl.when(((rest >> bit) & 1) == 1)
        def _(size=size, done=done):
            start = pl.multiple_of(first + done, SUBLANES)
            copy = pltpu.make_async_copy(zero_ref.at[pl.ds(0, size), :],
                                         dst_ref.at[pl.ds(start, size), :], sem)
            copy.wait() if wait else copy.start()


def _expert_kernel(te_ref, tv_ref, xs_ref, wg_ref, wu_ref, wd_ref, ys_ref, *, nchunk):
    i = pl.program_id(0)
    half = xs_ref.shape[0] // 2
    width = wg_ref.shape[2] // nchunk

    def swiglu_rows(rows):
        x = xs_ref[rows, :].astype(BF16)
        acc = None
        for c in range(nchunk):
            cs = slice(c * width, (c + 1) * width)
            gate = _dot(x, wg_ref[0, :, cs])
            up = _dot(x, wu_ref[0, :, cs])
            act = (gate * _sigmoid(gate) * up).astype(BF16)
            part = _dot(act, wd_ref[0, cs, :])
            acc = part if acc is None else acc + part
        ys_ref[rows, :] = acc.astype(BF16).astype(F32)

    @pl.when(tv_ref[i] == 2)
    def _():
        swiglu_rows(slice(None))

    @pl.when(tv_ref[i] == 1)
    def _():
        swiglu_rows(slice(half, 2 * half))
        ys_ref[:half, :] = jnp.zeros((half, ys_ref.shape[1]), F32)

    @pl.when(tv_ref[i] == 0)
    def _():
        ys_ref[...] = jnp.zeros_like(ys_ref)


def _experts(tile_expert, tile_valid, xs, w_gu, w_down):
    r, d = xs.shape
    n_tiles = r // TM_EXPERT
    dff = w_down.shape[1]
    kern = functools.partial(_expert_kernel, nchunk=EXPERT_CHUNKS)
    return pl.pallas_call(
        kern,
        grid_spec=pltpu.PrefetchScalarGridSpec(
            num_scalar_prefetch=2,
            grid=(r // TM_EXPERT,),
            in_specs=[
                pl.BlockSpec((TM_EXPERT, d), lambda i, te, tv: (jnp.minimum(i, tv[n_tiles]), 0)),
                pl.BlockSpec((1, d, dff), lambda i, te, tv: (te[i], 0, 0)),
                pl.BlockSpec((1, d, dff), lambda i, te, tv: (te[i], 0, 1)),
                pl.BlockSpec((1, dff, d), lambda i, te, tv: (te[i], 0, 0)),
            ],
            out_specs=pl.BlockSpec((TM_EXPERT, d), lambda i, te, tv: (i, 0)),
        ),
        out_shape=jax.ShapeDtypeStruct((r, d), F32),
        compiler_params=_params(("arbitrary",)),
        name="moe_experts",
    )(tile_expert, tile_valid, xs, w_gu, w_gu, w_down)


RUN_BUF_ROWS = 2 * TM + LANES


def _run_positions(idx, off_row):
    lane = lax.broadcasted_iota(I32, idx.shape, 1)
    pos = []
    for slot in range(2):
        off = jnp.sum(jnp.where(lane == idx[:, slot:slot + 1], off_row, 0.0), axis=-1, keepdims=True)
        pos.append(off.astype(I32) + idx[:, 2 + slot:3 + slot])
    return pos


def _run_copies(tab_ref, step, hbm_ref, buf_ref, slot, sem, to_hbm, wait):
    for e in range(N_EXPERTS):
        entry = (step * N_EXPERTS + e) * 3
        off, row, length = tab_ref[entry], tab_ref[entry + 1], tab_ref[entry + 2]
        for bit in range(SUBLANES.bit_length() - 1, TM.bit_length()):
            size = 1 << bit
            done = (length >> (bit + 1)) << (bit + 1)

            @pl.when(((length >> bit) & 1) == 1)
            def _(size=size, done=done, off=off, row=row):
                in_buf = buf_ref.at[slot, pl.ds(pl.multiple_of(off + done, SUBLANES), size), :]
                in_hbm = hbm_ref.at[pl.ds(pl.multiple_of(row + done, SUBLANES), size), :]
                copy = (pltpu.make_async_copy(in_buf, in_hbm, sem.at[slot]) if to_hbm
                        else pltpu.make_async_copy(in_hbm, in_buf, sem.at[slot]))
                copy.wait() if wait else copy.start()


def _run_dispatch_kernel(tab_ref, pad_ref, hn_ref, idx_ref, off_ref, xs_ref, buf, zero_sc, sem, zsem,
                         *, n_steps):
    j = pl.program_id(0)
    slot = j & 1

    @pl.when(j == 0)
    def _():
        zero_sc[...] = jnp.zeros_like(zero_sc)
        rows = zero_sc.shape[0]
        for wait in (False, True):
            for e in range(N_EXPERTS):
                _zero_rows(zero_sc, xs_ref, pad_ref[2 * e], pad_ref[2 * e + 1], 2 * rows, zsem, wait)

            def tail(t, carry, wait=wait):
                first = pl.multiple_of(pad_ref[2 * N_EXPERTS] + t * rows, SUBLANES)
                copy = pltpu.make_async_copy(zero_sc, xs_ref.at[pl.ds(first, rows), :], zsem)
                copy.wait() if wait else copy.start()
                return carry

            lax.fori_loop(0, pad_ref[2 * N_EXPERTS + 1] // rows, tail, 0)

    @pl.when(j >= 2)
    def _():
        _run_copies(tab_ref, j - 2, xs_ref, buf, slot, sem, True, True)

    pos0, pos1 = _run_positions(idx_ref[...], off_ref[0])
    lane = lax.broadcasted_iota(I32, (hn_ref.shape[0], buf.shape[1]), 1)
    sel = jnp.where((lane == pos0) | (lane == pos1), 1.0, 0.0).astype(BF16)
    buf[slot] = _dot_tn(sel, hn_ref[...])
    _run_copies(tab_ref, j, xs_ref, buf, slot, sem, True, False)

    @pl.when(j == n_steps - 1)
    def _():
        _run_copies(tab_ref, j - 1, xs_ref, buf, 1 - slot, sem, True, True)
        _run_copies(tab_ref, j, xs_ref, buf, slot, sem, True, True)


def _run_dispatch(run_table, pad_table, hn, idx, run_offsets, n_rows):
    n, d = hn.shape
    n_steps = n // TM
    assert n_steps >= 2
    return pl.pallas_call(
        functools.partial(_run_dispatch_kernel, n_steps=n_steps),
        grid_spec=pltpu.PrefetchScalarGridSpec(
            num_scalar_prefetch=2,
            grid=(n_steps,),
            in_specs=[
                pl.BlockSpec((TM, d), lambda i, *_: (i, 0)),
                pl.BlockSpec((TM, LANES), lambda i, *_: (i, 0)),
                pl.BlockSpec((1, 1, LANES), lambda i, *_: (i, 0, 0)),
            ],
            out_specs=pl.BlockSpec(memory_space=pl.ANY),
            scratch_shapes=[pltpu.VMEM((2, RUN_BUF_ROWS, d), F32),
                            pltpu.VMEM((TM_EXPERT // 2, d), F32),
                            pltpu.SemaphoreType.DMA((2,)), pltpu.SemaphoreType.DMA(())],
        ),
        out_shape=jax.ShapeDtypeStruct((n_rows, d), F32),
        compiler_params=_params(("arbitrary",)),
        name="moe_dispatch",
    )(run_table, pad_table, hn, idx, run_offsets)


def _run_combine_kernel(tab_ref, ys_ref, x_ref, gate_ref, idx_ref, off_ref, g_ref, yp_ref, ysm_ref,
                        buf, sem, *, n_first, n_steps):
    j = pl.program_id(0)
    slot = j & 1

    @pl.when(j == 0)
    def _():
        buf[...] = jnp.zeros_like(buf)
        _run_copies(tab_ref, 0, ys_ref, buf, 0, sem, False, False)

    @pl.when(j + 1 < n_steps)
    def _():
        _run_copies(tab_ref, j + 1, ys_ref, buf, 1 - slot, sem, False, False)

    pos = _run_positions(idx_ref[...], off_ref[0])
    lane = lax.broadcasted_iota(I32, (x_ref.shape[0], buf.shape[1]), 1)
    sel = [jnp.where(lane == pos[k], 1.0, 0.0).astype(BF16) for k in range(2)]
    _run_copies(tab_ref, j, ys_ref, buf, slot, sem, False, True)
    runs = buf[slot].astype(BF16)
    gate = gate_ref[...]
    y = x_ref[...]
    for k in range(2):
        y = y + gate[:, k:k + 1] * _dot(sel[k], runs)
    y = _rms(y, g_ref[...])

    @pl.when(j < n_first)
    def _():
        yp_ref[...] = y

    @pl.when(j >= n_first)
    def _():
        ysm_ref[...] = y


def _run_combine(run_table, ys, x, gates, idx, run_offsets, g, n_p):
    n, d = x.shape
    n_first = n_p // TM
    n_steps = n // TM
    p_spec, s_spec = _split_specs(n_first, d)
    row = lambda i, *_: (i, 0)
    return pl.pallas_call(
        functools.partial(_run_combine_kernel, n_first=n_first, n_steps=n_steps),
        grid_spec=pltpu.PrefetchScalarGridSpec(
            num_scalar_prefetch=1,
            grid=(n_steps,),
            in_specs=[
                pl.BlockSpec(memory_space=pl.ANY),
                pl.BlockSpec((TM, d), row),
                pl.BlockSpec((TM, LANES), row),
                pl.BlockSpec((TM, LANES), row),
                pl.BlockSpec((1, 1, LANES), lambda i, *_: (i, 0, 0)),
                pl.BlockSpec((1, d), lambda i, *_: (0, 0)),
            ],
            out_specs=[p_spec, s_spec],
            scratch_shapes=[pltpu.VMEM((2, RUN_BUF_ROWS, d), F32), pltpu.SemaphoreType.DMA((2,))],
        ),
        out_shape=[jax.ShapeDtypeStruct((n_p, d), F32), jax.ShapeDtypeStruct((n - n_p, d), F32)],
        compiler_params=_params(("arbitrary",)),
        name="moe_combine",
    )(run_table, ys, x, gates, idx, run_offsets, g)


def _run_tables(tile_counts, n_rows):
    cnt = tile_counts[:, 0, :N_EXPERTS].astype(I32)
    run = ((cnt + SUBLANES - 1) // SUBLANES) * SUBLANES
    off = jnp.cumsum(run, axis=1) - run
    used = jnp.sum(run, axis=0)
    padded = ((used + TM_EXPERT - 1) // TM_EXPERT) * TM_EXPERT
    ends = jnp.cumsum(padded)
    starts = ends - padded
    lead = padded - used
    row = (starts + lead)[None, :] + jnp.cumsum(run, axis=0) - run
    run_table = jnp.stack([off, row, run], axis=2).reshape(-1)
    run_offsets = jnp.pad(off.astype(F32), ((0, 0), (0, LANES - N_EXPERTS)))[:, None, :]
    n_tiles = n_rows // TM_EXPERT
    used_tiles = ends[-1] // TM_EXPERT
    pad_table = jnp.stack([starts, lead], axis=1).reshape(-1)
    pad_table = jnp.concatenate([pad_table, jnp.stack([ends[-1], n_rows - ends[-1]])])
    tile_start = jnp.arange(n_tiles, dtype=I32) * TM_EXPERT
    tile_expert = jnp.sum((tile_start[:, None] >= ends[None, :]).astype(I32), axis=1)
    last_expert = jnp.sum((ends[-1] - 1 >= ends).astype(I32))
    tile_expert = jnp.minimum(tile_expert, last_expert)
    first_row = (starts + lead)[tile_expert]
    half = TM_EXPERT // 2
    tile_rows = jnp.clip(tile_start + TM_EXPERT - first_row, 0, TM_EXPERT)
    tile_valid = jnp.where(tile_start < ends[-1], (tile_rows + half - 1) // half, 0)
    tile_valid = jnp.concatenate([tile_valid, (used_tiles - 1)[None]])
    return run_table, run_offsets, pad_table, tile_expert, tile_valid


def _pad_time(a, nseq, t):
    w = a.shape[1]
    a = jnp.pad(a.reshape(nseq, t, w), ((0, 0), (0, SAMPLE_PAD_T - t), (0, 0)))
    return a.reshape(nseq * SAMPLE_PAD_T, w)


def kernel(x_prompt, x_sample, state_gla_S, state_mlstm_C, state_mlstm_n, state_mlstm_m,
           norm_mix, norm_ffn, norm_final,
           gla_w_in, gla_w_a2, gla_b_a, gla_g_norm, gla_w_out,
           mlstm_w_in, mlstm_b_gate, mlstm_g_norm, mlstm_w_out,
           ffn_w_gu, ffn_w_down,
           moe_w_router, moe_b_router, moe_w_gu, moe_w_down):
    bp, tp, d = x_prompt.shape
    bs, ts, _ = x_sample.shape
    n_p = bp * tp
    n_s = bs * ts
    n = n_p + n_s
    assert n_p % TM == 0 and n_s % TM == 0 and tp % (2 * CHUNK) == 0 and ts <= SAMPLE_PAD_T
    assert norm_mix.shape[0] == 2, "one GLA layer followed by one mLSTM layer"
    qk_w = d // 2
    main_w = 2 * qk_w + 2 * d

    x_p = x_prompt.reshape(n_p, d)
    x_s = x_sample.reshape(n_s, d)

    w_a2 = jnp.pad(gla_w_a2[0], ((0, LANES - GLA_RANK), (0, 0))).astype(BF16)
    z, log_a, ffn_wgu = _gla_in_proj(x_p, _pad_time(x_s, bs, ts), norm_mix[0][None], gla_w_in[0].T,
                                     main_w, w_a2, gla_b_a[0][None], ffn_w_gu[0])
    s0_p = jnp.zeros((bp,) + state_gla_S.shape[2:], F32)
    u_p, s_p = _gla_recurrence(z, log_a, s0_p, gla_g_norm[0], bp, tp, CHUNK, 16 * CHUNK)
    u_s, s_s = _gla_recurrence_short(z, log_a, state_gla_S[0], gla_g_norm[0], bs, SAMPLE_PAD_T, ts,
                                     SAMPLE_SEQS_PER_STEP, n_p)
    x, hn, ffn_wd = _out_proj(u_p, u_s, x_p, x_s, gla_w_out[0],
                              norm_ffn[0][None], ffn_w_down[0])
    x, hn, moe_wgu = _swiglu(hn, x, ffn_wgu, ffn_wd,
                             norm_mix[1][None], SWIGLU_CHUNKS,
                             moe_w_gu[0].reshape(N_EXPERTS * d, moe_w_gu.shape[3]))

    b_gt = jnp.pad(mlstm_b_gate[0], (0, LANES - 2 * MLSTM_HEADS))[None]
    dff_e = moe_w_down.shape[2]
    z, gates, moe_wd = _mlstm_in_proj(hn, n_p, _pad_time(hn[n_p:], bs, ts), mlstm_w_in[0].T, main_w,
                                      b_gt, moe_w_down[0].reshape(N_EXPERTS * dff_e, d))
    pairs = MLSTM_HEADS // 2
    dk2 = 2 * state_mlstm_C.shape[3]
    dvm = state_mlstm_C.shape[4]
    c0_p = jnp.zeros((bp, pairs, dk2, dvm), F32)
    n0_p = jnp.zeros((bp, pairs, dk2), F32)
    m0_p = jnp.zeros((bp, 1, LANES), F32)
    u_p, c_p, nn_p, m_p = _mlstm_recurrence_blocked(z, gates, c0_p, n0_p, m0_p,
                                                    mlstm_g_norm[0], bp, tp, CHUNK, 16 * CHUNK)
    c0_s = state_mlstm_C[0]
    n0_s = state_mlstm_n[0].reshape(bs, pairs, dk2)
    m0_s = jnp.pad(state_mlstm_m[0], ((0, 0), (0, LANES - MLSTM_HEADS)))[:, None, :]
    u_s, c_s, nn_s, m_s = _mlstm_recurrence_short(z, gates, c0_s, n0_s, m0_s, mlstm_g_norm[0], bs,
                                                  SAMPLE_PAD_T, ts, SAMPLE_SEQS_PER_STEP, n_p)

    w_r = jnp.pad(moe_w_router[0], ((0, 0), (0, LANES - N_EXPERTS)))
    w_r_hi = w_r.astype(BF16)
    w_r = jnp.stack([w_r_hi, (w_r - w_r_hi.astype(F32)).astype(BF16)])
    b_r = jnp.pad(moe_b_router[0], (0, LANES - N_EXPERTS))[None]
    x, hn, route_g, route_i, tile_counts = _out_proj_router(
        u_p, u_s, x, mlstm_w_out[0], norm_ffn[1][None], w_r, b_r)
    n_rows = 2 * n + (n // TM) * N_EXPERTS * (SUBLANES - 1) + N_EXPERTS * (TM_EXPERT - 1)
    n_rows = -(-n_rows // TM_EXPERT) * TM_EXPERT
    run_table, run_offsets, pad_table, tile_expert, tile_valid = _run_tables(tile_counts, n_rows)
    xs = _run_dispatch(run_table, pad_table, hn, route_i, run_offsets, n_rows)
    ys = _experts(tile_expert, tile_valid, xs, moe_wgu.reshape(moe_w_gu.shape[1:]),
                  moe_wd.reshape(moe_w_down.shape[1:]))
    y_p, y_s = _run_combine(run_table, ys, x, route_g, route_i, run_offsets, norm_final[None], n_p)

    y_prompt = y_p.reshape(bp, tp, d)
    y_sample = y_s.reshape(bs, ts, d)
    hd = state_mlstm_C.shape[2:]
    return (y_prompt, y_sample,
            s_p[None], c_p.reshape((1, bp) + hd), nn_p.reshape(1, bp, hd[0], hd[1]),
            m_p[None, :, 0, :MLSTM_HEADS],
            s_s[None], c_s[None], nn_s.reshape(1, bs, hd[0], hd[1]), m_s[None, :, 0, :MLSTM_HEADS])


def _mlstm_in_proj(hn, n_p, hn_s, w_in, wz, b_gate, cast_src):
    d = hn.shape[1]
    n = n_p + hn_s.shape[0]
    n_first = n_p // TM
    cast_in, cast_out, cast_shape = _cast_side_job(cast_src, n // TM)

    def kern(hp_ref, hs_ref, w_ref, wg_ref, b_ref, cast_ref, z_ref, gate_ref, cast_out_ref,
             wb_sc, wg_sc):
        @pl.when(pl.program_id(0) == 0)
        def _():
            wb_sc[...] = w_ref[...].T.astype(BF16)
            wg_sc[...] = _gate_weights(wg_ref)

        cast_out_ref[...] = cast_ref[...].astype(BF16)
        h = _pick_rows(pl.program_id(0), n_first, hp_ref, hs_ref)
        gp = _dot(h, wg_sc[...]) + b_ref[...]
        gc = GATE_CAP * jnp.tanh(gp * (1.0 / GATE_CAP))
        lane = lax.broadcasted_iota(I32, gc.shape, 1)
        out = jnp.where(lane < MLSTM_HEADS, gc, _log_sigmoid(gc))
        gate_ref[...] = jnp.where(lane < 2 * MLSTM_HEADS, out, 0.0)
        width = wz // IN_PROJ_COL_CHUNKS
        for c in range(IN_PROJ_COL_CHUNKS):
            cs = slice(c * width, (c + 1) * width)
            z_ref[:, cs] = _dot(h, wb_sc[:, cs])

    return pl.pallas_call(
        kern,
        grid=(n // TM,),
        in_specs=_split_specs(n_first, d) + [
            _resident_spec((wz, d), lambda i: (0, 0)),
            _gate_rows_spec(w_in, wz),
            pl.BlockSpec((1, LANES), lambda i: (0, 0)),
            cast_in,
        ],
        out_specs=[pl.BlockSpec((TM, wz), lambda i: (i, 0)), pl.BlockSpec((TM, LANES), lambda i: (i, 0)),
                   cast_out],
        out_shape=[jax.ShapeDtypeStruct((n, wz), F32), jax.ShapeDtypeStruct((n, LANES), F32),
                   cast_shape],
        scratch_shapes=[pltpu.VMEM((d, wz), BF16), pltpu.VMEM((d, LANES), BF16)],
        compiler_params=_params(("arbitrary",)),
        name="in_proj_mlstm",
    )(hn, hn_s, w_in, w_in, b_gate, cast_src)
```
